```python
import math
import jax, jax.numpy as jnp
from jax import lax
import numpy as np

D_MODEL = 2048
BATCH = 4
SEQ = 4096
DEPTH = 1

CHUNK = 64
A_HEADS = 4
A_HEAD_DIM = 256
A_WIDTH = A_HEADS * A_HEAD_DIM
CONV_WIDTH = 4
FORGET_BIAS_LO = 3.0
FORGET_BIAS_HI = 6.0
B_HEADS = 8
B_HEAD_DIM = 64
B_V_DIM = 2 * B_HEAD_DIM
B_QK_WIDTH = 2 * B_HEADS * B_HEAD_DIM
B_WIDTH = B_HEADS * B_V_DIM
Q_BLOCK = 128
N_GROUPS = 4
EXPERTS_PER_GROUP = 8
N_EXPERTS = N_GROUPS * EXPERTS_PER_GROUP
TOP_K = 2
D_EXPERT = 512
MOE_BLOCK = 128
DEEPNORM_ALPHA = (2 * DEPTH) ** 0.25
DEEPNORM_BETA = (8 * DEPTH) ** -0.25
LN_EPS = 1e-5
IN_COLS = (A_WIDTH, A_WIDTH, A_WIDTH, A_WIDTH, A_HEADS, A_HEADS,
           B_QK_WIDTH, B_QK_WIDTH, B_WIDTH, D_MODEL, D_MODEL)
V_COL_IDS = (2, 8)
F_COL_ID = 5
N_IN = sum(IN_COLS)

kernel_name = "hybrid_mlstm_diffattn_hmoe_block"


def _col_offsets():
    offs = [0]
    for c in IN_COLS:
        offs.append(offs[-1] + c)
    return offs


def _layer_norm(x, g, b):
    xf = x.astype(jnp.float32)
    mu = jnp.mean(xf, axis=-1, keepdims=True)
    var = jnp.mean(jnp.square(xf - mu), axis=-1, keepdims=True)
    y = (xf - mu) * lax.rsqrt(var + LN_EPS)
    return (y * g + b).astype(x.dtype)


def _causal_dwconv(u, w, bias):
    c = u.shape[-1]
    y = lax.conv_general_dilated(u, w.astype(u.dtype)[:, None, :], window_strides=(1,),
                                 padding=[(CONV_WIDTH - 1, 0)],
                                 dimension_numbers=('NWC', 'WIO', 'NWC'),
                                 feature_group_count=c)
    return y + bias


def _mlstm(q, k, v, i_pre, f_pre):
    B, S, H, dh = q.shape
    nc = S // CHUNK
    f32 = jnp.float32
    to_chunks = lambda t: t.astype(f32).reshape(B, nc, CHUNK, H, dh).transpose(1, 0, 3, 2, 4)
    gate_chunks = lambda t: t.astype(f32).reshape(B, nc, CHUNK, H).transpose(1, 0, 3, 2)
    qc = to_chunks(q)
    kc = to_chunks(k) * (dh ** -0.5)
    vc = to_chunks(v)
    ic = gate_chunks(i_pre)
    lfc = jax.nn.log_sigmoid(gate_chunks(f_pre))
    tril = jnp.tril(jnp.ones((CHUNK, CHUNK), dtype=bool))

    def step(carry, xs):
        C, n, m = carry
        q_, k_, v_, i_, lf_ = xs
        b = jnp.cumsum(lf_, axis=-1)
        dmat = b[..., :, None] - b[..., None, :] + i_[..., None, :]
        dmat = jnp.where(tril, dmat, -jnp.inf)
        inter = b + m[..., None]
        m_t = jnp.maximum(inter, jnp.max(dmat, axis=-1))
        w_intra = jnp.exp(dmat - m_t[..., None]) * jnp.einsum('bhtd,bhsd->bhts', q_, k_)
        w_inter = jnp.exp(inter - m_t)
        num = (w_inter[..., None] * jnp.einsum('bhtk,bhkv->bhtv', q_, C)
               + jnp.einsum('bhts,bhsv->bhtv', w_intra, v_))
        den = w_inter * jnp.einsum('bhtk,bhk->bht', q_, n) + jnp.sum(w_intra, axis=-1)
        h = num / jnp.maximum(jnp.abs(den), jnp.exp(-m_t))[..., None]
        b_last = b[..., -1]
        g = b_last[..., None] - b + i_
        m_new = jnp.maximum(b_last + m, jnp.max(g, axis=-1))
        decay = jnp.exp(b_last + m - m_new)
        wk = jnp.exp(g - m_new[..., None])
        C = decay[..., None, None] * C + jnp.einsum('bhs,bhsk,bhsv->bhkv', wk, k_, v_)
        n = decay[..., None] * n + jnp.einsum('bhs,bhsk->bhk', wk, k_)
        return (C, n, m_new), h

    init = (jnp.zeros((B, H, dh, dh), f32), jnp.zeros((B, H, dh), f32), jnp.zeros((B, H), f32))
    _, h = lax.scan(step, init, (qc, kc, vc, ic, lfc))
    return h.transpose(1, 0, 3, 2, 4).reshape(B, S, H, dh)


def _diff_attention(q, k, v, lam):
    B, S, H, _, d = q.shape
    f32 = jnp.float32
    nqb = S // Q_BLOCK
    q_blocks = jnp.moveaxis(q.reshape(B, nqb, Q_BLOCK, H, 2, d), 1, 0)
    slopes = 2.0 ** (-8.0 * jnp.arange(1, H + 1, dtype=f32) / H)
    k_pos = jnp.arange(S)
    scale = d ** -0.5

    def one_block(args):
        q_blk, blk = args
        q_pos = blk * Q_BLOCK + jnp.arange(Q_BLOCK)
        s = jnp.einsum('bqhnd,bkhnd->bhnqk', q_blk, k).astype(f32) * scale
        dist = jnp.abs(q_pos[:, None] - k_pos[None, :]).astype(f32)
        visible = (k_pos[None, :] // CHUNK) <= (q_pos[:, None] // CHUNK)
        s = jnp.where(visible, s - slopes[None, :, None, None, None] * dist, -jnp.inf)
        p = jax.nn.softmax(s, axis=-1)
        a = p[:, :, 0] - lam * p[:, :, 1]
        return jnp.einsum('bhqk,bkhe->bqhe', a.astype(v.dtype), v)

    o = lax.map(one_block, (q_blocks, jnp.arange(nqb)))
    return jnp.moveaxis(o, 0, 1).reshape(B, S, H, v.shape[-1]).astype(f32)


def _mixer(x, w_in, b_in, conv_w, conv_b, norm_a_g, lq1, lk1, lq2, lk2, norm_b_g,
           w_a, w_b, w_out, lambda_init):
    B, S, _ = x.shape
    f32 = jnp.float32
    offs = _col_offsets()
    z = x @ w_in + b_in
    qa, ka, va, oa, ia, fa, qb, kb, vb, ga, gb = [z[..., offs[j]:offs[j + 1]] for j in range(len(IN_COLS))]
    qk = jax.nn.silu(_causal_dwconv(jnp.concatenate([qa, ka], axis=-1), conv_w, conv_b))
    qa, ka = qk[..., :A_WIDTH], qk[..., A_WIDTH:]
    heads_a = lambda t: t.reshape(B, S, A_HEADS, A_HEAD_DIM)
    h_a = _mlstm(heads_a(qa), heads_a(ka), heads_a(va), ia, fa)
    h_a = jax.nn.sigmoid(oa.astype(f32)).reshape(B, S, A_HEADS, A_HEAD_DIM) * h_a
    mu = jnp.mean(h_a, axis=-1, keepdims=True)
    var = jnp.mean(jnp.square(h_a - mu), axis=-1, keepdims=True)
    h_a = ((h_a - mu) * lax.rsqrt(var + LN_EPS)).reshape(B, S, A_WIDTH) * norm_a_g
    y_a = h_a.astype(x.dtype) @ w_a
    lam = (jnp.exp(jnp.sum(lq1.astype(f32) * lk1.astype(f32)))
           - jnp.exp(jnp.sum(lq2.astype(f32) * lk2.astype(f32))) + lambda_init)
    o_b = _diff_attention(qb.reshape(B, S, B_HEADS, 2, B_HEAD_DIM),
                          kb.reshape(B, S, B_HEADS, 2, B_HEAD_DIM),
                          vb.reshape(B, S, B_HEADS, B_V_DIM), lam)
    o_b = o_b * lax.rsqrt(jnp.mean(jnp.square(o_b), axis=-1, keepdims=True) + LN_EPS) * norm_b_g * (1.0 - lambda_init)
    y_b = o_b.reshape(B, S, B_WIDTH).astype(x.dtype) @ w_b
    merged = jax.nn.sigmoid(ga) * y_a + jax.nn.sigmoid(gb) * y_b
    return merged @ w_out


def _hier_moe(h, w_grp, b_grp, w_exp, b_exp, w_gate, w_up, w_down):
    B, S, D = h.shape
    T = B * S
    M = T * TOP_K
    f32 = jnp.float32
    ht = h.reshape(T, D)
    tok = jnp.arange(T)
    grp_logits = (ht @ w_grp + b_grp).astype(f32)
    grp_sel = jnp.argmax(grp_logits, axis=-1)
    grp_prob = jax.nn.softmax(grp_logits, axis=-1)[tok, grp_sel]
    exp_logits = (ht @ w_exp + b_exp).astype(f32).reshape(T, N_GROUPS, EXPERTS_PER_GROUP)
    in_grp = exp_logits[tok, grp_sel]
    top_val, top_idx = lax.top_k(in_grp, TOP_K)
    slot_w = (grp_prob[:, None] * jax.nn.softmax(top_val, axis=-1)).reshape(M)
    slot_e = (grp_sel[:, None] * EXPERTS_PER_GROUP + top_idx).reshape(M).astype(jnp.int32)
    slot_tok = jnp.repeat(jnp.arange(T, dtype=jnp.int32), TOP_K)
    order = jnp.argsort(slot_e)
    e_sorted = slot_e[order]
    counts = jnp.bincount(slot_e, length=N_EXPERTS)
    starts = jnp.cumsum(counts) - counts
    padded = ((counts + MOE_BLOCK - 1) // MOE_BLOCK) * MOE_BLOCK
    pad_ends = jnp.cumsum(padded)
    pad_starts = pad_ends - padded
    pos = pad_starts[e_sorted] + (jnp.arange(M) - starts[e_sorted])
    n_blocks = M // MOE_BLOCK + N_EXPERTS
    P = n_blocks * MOE_BLOCK
    tok_at = jnp.full((P,), T, jnp.int32).at[pos].set(slot_tok[order])
    w_at = jnp.zeros((P,), f32).at[pos].set(slot_w[order])
    block_e = jnp.clip(jnp.searchsorted(pad_ends, jnp.arange(n_blocks) * MOE_BLOCK, side='right'),
                       0, N_EXPERTS - 1)
    x_pad = jnp.concatenate([ht, jnp.zeros((1, D), ht.dtype)], axis=0)[tok_at].reshape(n_blocks, MOE_BLOCK, D)

    def expert_block(args):
        xb, e = args
        return (jax.nn.silu(xb @ w_gate[e]) * (xb @ w_up[e])) @ w_down[e]

    y = lax.map(expert_block, (x_pad, block_e)).reshape(P, D)
    out = jnp.zeros((T + 1, D), y.dtype).at[tok_at].add(y * w_at[:, None].astype(y.dtype))
    return out[:T].reshape(B, S, D)


def setup_inputs(seed: int = 0) -> dict:
    key = jax.random.key(seed)
    ks = jax.random.split(key, 24)
    nrm = lambda k, shape: jax.random.normal(k, shape, jnp.float32)
    offs = _col_offsets()
    col_scale = jnp.concatenate([jnp.full((c,), DEEPNORM_BETA if j in V_COL_IDS else 1.0, jnp.float32)
                                 for j, c in enumerate(IN_COLS)])
    w_in = nrm(ks[1], (DEPTH, D_MODEL, N_IN)) * (D_MODEL ** -0.5) * col_scale
    b_in = 0.01 * nrm(ks[2], (DEPTH, N_IN))
    b_in = b_in.at[:, offs[F_COL_ID]:offs[F_COL_ID + 1]].add(
        jnp.linspace(FORGET_BIAS_LO, FORGET_BIAS_HI, A_HEADS, dtype=jnp.float32))
    return {
        "x": nrm(ks[0], (BATCH, SEQ, D_MODEL)),
        "w_in": w_in,
        "b_in": b_in,
        "conv_w": nrm(ks[3], (DEPTH, CONV_WIDTH, 2 * A_WIDTH)) * (CONV_WIDTH ** -0.5),
        "conv_b": 0.01 * nrm(ks[4], (DEPTH, 2 * A_WIDTH)),
        "mlstm_norm_g": 1.0 + 0.02 * nrm(ks[5], (DEPTH, A_WIDTH)),
        "lambda_q1": 0.1 * nrm(ks[6], (DEPTH, B_HEAD_DIM)),
        "lambda_k1": 0.1 * nrm(ks[7], (DEPTH, B_HEAD_DIM)),
        "lambda_q2": 0.1 * nrm(ks[8], (DEPTH, B_HEAD_DIM)),
        "lambda_k2": 0.1 * nrm(ks[9], (DEPTH, B_HEAD_DIM)),
        "diff_norm_g": 1.0 + 0.02 * nrm(ks[10], (DEPTH, B_V_DIM)),
        "w_a": nrm(ks[11], (DEPTH, A_WIDTH, D_MODEL)) * (A_WIDTH ** -0.5) * DEEPNORM_BETA,
        "w_b": nrm(ks[12], (DEPTH, B_WIDTH, D_MODEL)) * (B_WIDTH ** -0.5) * DEEPNORM_BETA,
        "w_out": nrm(ks[13], (DEPTH, D_MODEL, D_MODEL)) * (D_MODEL ** -0.5) * DEEPNORM_BETA,
        "ln1_g": 1.0 + 0.02 * nrm(ks[14], (DEPTH, D_MODEL)),
        "ln1_b": 0.02 * nrm(ks[15], (DEPTH, D_MODEL)),
        "w_grp": nrm(ks[16], (DEPTH, D_MODEL, N_GROUPS)) * (D_MODEL ** -0.5),
        "b_grp": 0.01 * nrm(ks[17], (DEPTH, N_GROUPS)),
        "w_exp": nrm(ks[18], (DEPTH, D_MODEL, N_EXPERTS)) * (D_MODEL ** -0.5),
        "b_exp": 0.01 * nrm(ks[19], (DEPTH, N_EXPERTS)),
        "w_gate": nrm(ks[20], (DEPTH, N_EXPERTS, D_MODEL, D_EXPERT)) * (D_MODEL ** -0.5),
        "w_up": nrm(ks[21], (DEPTH, N_EXPERTS, D_MODEL, D_EXPERT)) * (D_MODEL ** -0.5),
        "w_down": nrm(ks[22], (DEPTH, N_EXPERTS, D_EXPERT, D_MODEL)) * (D_EXPERT ** -0.5) * DEEPNORM_BETA,
        "ln2_g": 1.0 + 0.02 * nrm(ks[23], (DEPTH, D_MODEL)),
        "ln2_b": 0.02 * nrm(jax.random.fold_in(ks[23], 1), (DEPTH, D_MODEL)),
    }


def reference(x, w_in, b_in, conv_w, conv_b, mlstm_norm_g, lambda_q1, lambda_k1, lambda_q2, lambda_k2,
              diff_norm_g, w_a, w_b, w_out, ln1_g, ln1_b, w_grp, b_grp, w_exp, b_exp,
              w_gate, w_up, w_down, ln2_g, ln2_b):
    for l in range(DEPTH):
        lambda_init = 0.8 - 0.6 * math.exp(-0.3 * l)
        mix = _mixer(x, w_in[l], b_in[l], conv_w[l], conv_b[l], mlstm_norm_g[l],
                     lambda_q1[l], lambda_k1[l], lambda_q2[l], lambda_k2[l], diff_norm_g[l],
                     w_a[l], w_b[l], w_out[l], lambda_init)
        x = _layer_norm(DEEPNORM_ALPHA * x + mix, ln1_g[l], ln1_b[l])
        ffn = _hier_moe(x, w_grp[l], b_grp[l], w_exp[l], b_exp[l], w_gate[l], w_up[l], w_down[l])
        x = _layer_norm(DEEPNORM_ALPHA * x + ffn, ln2_g[l], ln2_b[l])
    return x
```

```python
import functools
import math

import jax
import jax.numpy as jnp
from jax import lax
from jax.experimental import pallas as pl
from jax.experimental.pallas import tpu as pltpu

F32 = jnp.float32
BF16 = jnp.bfloat16
I32 = jnp.int32

D_MODEL = 2048
A_HEADS = 4
A_HEAD_DIM = 256
A_WIDTH = A_HEADS * A_HEAD_DIM
CONV_WIDTH = 4
B_HEADS = 8
B_HEAD_DIM = 64
B_V_DIM = 2 * B_HEAD_DIM
B_WIDTH = B_HEADS * B_V_DIM
ATTN_CHUNK = 64
N_GROUPS = 4
EXPERTS_PER_GROUP = 8
N_EXPERTS = N_GROUPS * EXPERTS_PER_GROUP
TOP_K = 2
D_EXPERT = 512
DEPTH = 1
DEEPNORM_ALPHA = (2 * DEPTH) ** 0.25
LN_EPS = 1e-5
NEG_BIG = -1e30

LANES = 128
SUBLANES = 8
VMEM_LIMIT_BYTES = 56 * 1024 * 1024

COL_GA = 0
COL_GB = COL_GA + D_MODEL
COL_QA = COL_GB + D_MODEL
COL_KA = COL_QA + A_WIDTH
COL_VA = COL_KA + A_WIDTH
COL_OA = COL_VA + A_WIDTH
COL_QB = COL_OA + A_WIDTH
COL_KB = COL_QB + B_WIDTH
COL_VB = COL_KB + B_WIDTH
N_MAIN = COL_VB + B_WIDTH

PROJ_TM, PROJ_TN = 1024, 512
MLSTM_CHUNK = 256
ATTN_TQ = 256
MIX_TM = 256
MOE_TB = 256
DISPATCH_TM = 256
COMBINE_TM = 256


def _cparams(sem, vmem=VMEM_LIMIT_BYTES):
    return pltpu.CompilerParams(dimension_semantics=sem, vmem_limit_bytes=vmem)


def _sigmoid(x):
    return 1.0 / (1.0 + jnp.exp(-x))


def _proj_kernel(x_ref, w_ref, b_ref, wg_ref, bg_ref, z_ref, zg_ref, xb_ref):
    @pl.when(pl.program_id(1) == 0)
    def _():
        x = x_ref[...]
        xb_ref[...] = x.astype(BF16)
        zg_ref[...] = jnp.dot(x, wg_ref[...], preferred_element_type=F32,
                              precision=lax.Precision.HIGHEST) + bg_ref[...]

    acc = jnp.dot(xb_ref[...], w_ref[...], preferred_element_type=F32)
    z_ref[...] = (acc + b_ref[...]).astype(z_ref.dtype)


def _in_projection(x2d, w_main, b_main, w_gate, b_gate):
    T, K = x2d.shape
    N = w_main.shape[1]
    tm, tn = min(PROJ_TM, T), PROJ_TN
    return pl.pallas_call(
        _proj_kernel,
        grid=(T // tm, N // tn),
        in_specs=[
            pl.BlockSpec((tm, K), lambda i, j: (i, 0)),
            pl.BlockSpec((K, tn), lambda i, j: (0, j)),
            pl.BlockSpec((1, tn), lambda i, j: (0, j)),
            pl.BlockSpec((K, LANES), lambda i, j: (0, 0)),
            pl.BlockSpec((1, LANES), lambda i, j: (0, 0)),
        ],
        out_specs=[
            pl.BlockSpec((tm, tn), lambda i, j: (i, j)),
            pl.BlockSpec((tm, LANES), lambda i, j: (i, 0)),
        ],
        out_shape=[
            jax.ShapeDtypeStruct((T, N), BF16),
            jax.ShapeDtypeStruct((T, LANES), F32),
        ],
        scratch_shapes=[pltpu.VMEM((tm, K), BF16)],
        compiler_params=_cparams(("parallel", "arbitrary")),
        name="in_projection",
    )(x2d, w_main, b_main, w_gate, b_gate)


def _gate_prep_kernel(g_ref, o_ref):
    g = g_ref[...]
    L = g.shape[1]
    lf = jnp.minimum(g, 0.0) - jnp.log(1.0 + jnp.exp(-jnp.abs(g)))
    r = lax.broadcasted_iota(I32, (L, L), 0)
    c = lax.broadcasted_iota(I32, (L, L), 1)
    tri = (r <= c).astype(F32)
    b = jnp.dot(lf, tri, preferred_element_type=F32, precision=lax.Precision.HIGHEST)
    row = lax.broadcasted_iota(I32, g.shape, 0)
    o_ref[...] = jnp.where(row < A_HEADS, g, b)


def _gate_prep(g_rows, L):
    R, T = g_rows.shape
    return pl.pallas_call(
        _gate_prep_kernel,
        grid=(T // L,),
        in_specs=[pl.BlockSpec((R, L), lambda i: (0, i))],
        out_specs=pl.BlockSpec((R, L), lambda i: (0, i)),
        out_shape=jax.ShapeDtypeStruct((R, T), F32),
        compiler_params=_cparams(("parallel",)),
        name="gate_prep",
    )(g_rows)


def _mlstm_kernel(q_ref, k_ref, v_ref, o_ref, gcol_ref, grow_ref, cwq_ref, cwk_ref, cbq_ref, cbk_ref,
                  ng_ref, out_ref, c_ref, n_ref, m_ref, qcar_ref, kcar_ref):
    ci = pl.program_id(2)
    L = q_ref.shape[0]
    dh = q_ref.shape[1]

    @pl.when(ci == 0)
    def _():
        c_ref[...] = jnp.zeros_like(c_ref)
        n_ref[...] = jnp.zeros_like(n_ref)
        m_ref[...] = jnp.zeros_like(m_ref)
        qcar_ref[...] = jnp.zeros_like(qcar_ref)
        kcar_ref[...] = jnp.zeros_like(kcar_ref)

    def conv_silu(u_ref, car_ref, w_ref, b_ref):
        u = u_ref[...].astype(F32)
        ext = jnp.concatenate([car_ref[...], u], axis=0)
        w = w_ref[...]
        y = b_ref[...] + w[3:4, :] * u
        for j in range(CONV_WIDTH - 1):
            off = SUBLANES - (CONV_WIDTH - 1) + j
            y = y + w[j:j + 1, :] * ext[off:off + L, :]
        car_ref[...] = u[L - SUBLANES:, :]
        return y * _sigmoid(y)

    q = conv_silu(q_ref, qcar_ref, cwq_ref, cbq_ref)
    k = conv_silu(k_ref, kcar_ref, cwk_ref, cbk_ref) * (dh ** -0.5)
    v_b = v_ref[...]
    q_b = q.astype(BF16)
    k_b = k.astype(BF16)

    gcol = gcol_ref[...]
    grow = grow_ref[...]
    i_col, b_col = gcol[:, 0:1], gcol[:, 1:2]
    i_row, b_row = grow[0:1, :], grow[1:2, :]
    b_last = b_row[:, L - 1:L]
    m_prev = m_ref[...]

    r = lax.broadcasted_iota(I32, (L, L), 0)
    c = lax.broadcasted_iota(I32, (L, L), 1)
    dmat = jnp.where(r >= c, b_col - b_row + i_row, NEG_BIG)
    inter = b_col + m_prev
    m_t = jnp.maximum(inter, jnp.max(dmat, axis=-1, keepdims=True))
    s = lax.dot_general(q_b, k_b, (((1,), (1,)), ((), ())), preferred_element_type=F32)
    w_intra = jnp.exp(dmat - m_t) * s
    w_inter = jnp.exp(inter - m_t)
    qc = jnp.dot(q_b, c_ref[...].astype(BF16), preferred_element_type=F32)
    num = w_inter * qc + jnp.dot(w_intra.astype(BF16), v_b, preferred_element_type=F32)
    qn = jnp.sum(q * n_ref[...], axis=-1, keepdims=True)
    den = w_inter * qn + jnp.sum(w_intra, axis=-1, keepdims=True)
    hh = num / jnp.maximum(jnp.abs(den), jnp.exp(-m_t))

    g_col = b_last - b_col + i_col
    m_new = jnp.maximum(b_last + m_prev, jnp.max(g_col, axis=0, keepdims=True))
    decay = jnp.exp(b_last + m_prev - m_new)
    kw = jnp.exp(g_col - m_new) * k
    c_ref[...] = decay * c_ref[...] + jnp.dot(kw.T.astype(BF16), v_b, preferred_element_type=F32)
    n_ref[...] = decay * n_ref[...] + jnp.sum(kw, axis=0, keepdims=True)
    m_ref[...] = m_new

    y = _sigmoid(o_ref[...].astype(F32)) * hh
    mu = jnp.mean(y, axis=-1, keepdims=True)
    yc = y - mu
    var = jnp.mean(yc * yc, axis=-1, keepdims=True)
    out_ref[...] = (yc * lax.rsqrt(var + LN_EPS) * ng_ref[...]).astype(out_ref.dtype)


def _mlstm(z_main, gcol, grow, conv_w, conv_b, norm_g, B, S):
    L = min(MLSTM_CHUNK, S)
    nc = S // L
    dh = A_HEAD_DIM
    H = A_HEADS
    row = lambda b, h, c: b * nc + c
    return pl.pallas_call(
        _mlstm_kernel,
        grid=(B, H, nc),
        in_specs=[
            pl.BlockSpec((L, dh), lambda b, h, c: (row(b, h, c), COL_QA // dh + h)),
            pl.BlockSpec((L, dh), lambda b, h, c: (row(b, h, c), COL_KA // dh + h)),
            pl.BlockSpec((L, dh), lambda b, h, c: (row(b, h, c), COL_VA // dh + h)),
            pl.BlockSpec((L, dh), lambda b, h, c: (row(b, h, c), COL_OA // dh + h)),
            pl.BlockSpec((None, L, SUBLANES), lambda b, h, c: (h, row(b, h, c), 0)),
            pl.BlockSpec((None, SUBLANES, L), lambda b, h, c: (h, 0, row(b, h, c))),
            pl.BlockSpec((CONV_WIDTH, dh), lambda b, h, c: (0, h)),
            pl.BlockSpec((CONV_WIDTH, dh), lambda b, h, c: (0, H + h)),
            pl.BlockSpec((1, dh), lambda b, h, c: (0, h)),
            pl.BlockSpec((1, dh), lambda b, h, c: (0, H + h)),
            pl.BlockSpec((1, dh), lambda b, h, c: (0, h)),
        ],
        out_specs=pl.BlockSpec((L, dh), lambda b, h, c: (row(b, h, c), h)),
        out_shape=jax.ShapeDtypeStruct((B * S, A_WIDTH), BF16),
        scratch_shapes=[
            pltpu.VMEM((dh, dh), F32),
            pltpu.VMEM((1, dh), F32),
            pltpu.VMEM((1, 1), F32),
            pltpu.VMEM((SUBLANES, dh), F32),
            pltpu.VMEM((SUBLANES, dh), F32),
        ],
        compiler_params=_cparams(("parallel", "parallel", "arbitrary")),
        name="mlstm",
    )(z_main, z_main, z_main, z_main, gcol, grow, conv_w, conv_w, conv_b, conv_b, norm_g)


def _attn_kernel(slope_ref, q_ref, k_ref, v_ref, lq1_ref, lk1_ref, lq2_ref, lk2_ref, g_ref, o_ref,
                 m_ref, l_ref, acc_ref, *, lambda_init):
    h = pl.program_id(1)
    qi = pl.program_id(2)
    tq = q_ref.shape[0]
    tk = tq
    slope = slope_ref[h]

    lane = lax.broadcasted_iota(I32, (tq, B_V_DIM), 1)
    q = q_ref[...] * (B_HEAD_DIM ** -0.5)
    zero = jnp.zeros_like(q)
    qm = (jnp.where(lane < B_HEAD_DIM, q, zero).astype(BF16),
          jnp.where(lane >= B_HEAD_DIM, q, zero).astype(BF16))

    r = lax.broadcasted_iota(I32, (tq, tk), 0)
    c = lax.broadcasted_iota(I32, (tq, tk), 1)
    rel = (r - c).astype(F32)

    m_ref[...] = jnp.full(m_ref.shape, NEG_BIG, F32)
    l_ref[...] = jnp.zeros_like(l_ref)
    acc_ref[...] = jnp.zeros_like(acc_ref)

    def block(j, bias, shift):
        start = pl.multiple_of(j * tk, tk)
        kj = k_ref[pl.ds(start, tk), :]
        vj = v_ref[pl.ds(start, tk), :]
        for n in range(2):
            t = lax.dot_general(qm[n], kj, (((1,), (1,)), ((), ())), preferred_element_type=F32) + bias
            m_old = m_ref[n]
            m_new = jnp.maximum(m_old, jnp.max(t, axis=-1, keepdims=True) + shift)
            alpha = jnp.exp(m_old - m_new)
            p = jnp.exp(t - (m_new - shift))
            l_ref[n] = alpha * l_ref[n] + jnp.sum(p, axis=-1, keepdims=True)
            acc_ref[n] = alpha * acc_ref[n] + jnp.dot(p.astype(BF16), vj, preferred_element_type=F32)
            m_ref[n] = m_new

    off_bias = rel * (-slope)

    def off_diag(j, carry):
        shift = (qi - j).astype(F32) * (-slope * tq)
        block(j, off_bias, shift)
        return carry

    lax.fori_loop(0, qi, off_diag, 0)
    visible = (c // ATTN_CHUNK) <= (r // ATTN_CHUNK)
    diag_bias = jnp.where(visible, jnp.abs(rel) * (-slope), NEG_BIG)
    block(qi, diag_bias, jnp.float32(0.0))

    lam = (jnp.exp(jnp.sum(lq1_ref[...] * lk1_ref[...], axis=-1, keepdims=True))
           - jnp.exp(jnp.sum(lq2_ref[...] * lk2_ref[...], axis=-1, keepdims=True)) + lambda_init)
    o = acc_ref[0] / l_ref[0] - lam * (acc_ref[1] / l_ref[1])
    ms = jnp.mean(o * o, axis=-1, keepdims=True)
    o_ref[...] = (o * lax.rsqrt(ms + LN_EPS) * g_ref[...] * (1.0 - lambda_init)).astype(o_ref.dtype)


def _diff_attention(z_main, lq1, lk1, lq2, lk2, norm_g, B, S, lambda_init):
    tq = min(ATTN_TQ, S)
    nq = S // tq
    H = B_HEADS
    dv = B_V_DIM
    slopes = 2.0 ** (-8.0 * jnp.arange(1, H + 1, dtype=F32) / H)
    small = pl.BlockSpec((1, B_HEAD_DIM), lambda b, h, i: (0, 0))
    return pl.pallas_call(
        functools.partial(_attn_kernel, lambda_init=lambda_init),
        grid=(B, H, nq),
        in_specs=[
            pl.BlockSpec(memory_space=pltpu.SMEM),
            pl.BlockSpec((tq, dv), lambda b, h, i: (b * nq + i, COL_QB // dv + h)),
            pl.BlockSpec((S, dv), lambda b, h, i: (b, COL_KB // dv + h)),
            pl.BlockSpec((S, dv), lambda b, h, i: (b, COL_VB // dv + h)),
            small, small, small, small,
            pl.BlockSpec((1, dv), lambda b, h, i: (0, 0)),
        ],
        out_specs=pl.BlockSpec((tq, dv), lambda b, h, i: (b * nq + i, h)),
        out_shape=jax.ShapeDtypeStruct((B * S, B_WIDTH), BF16),
        scratch_shapes=[
            pltpu.VMEM((2, tq, 1), F32),
            pltpu.VMEM((2, tq, 1), F32),
            pltpu.VMEM((2, tq, dv), F32),
        ],
        compiler_params=_cparams(("parallel", "parallel", "arbitrary")),
        name="diff_attention",
    )(slopes, z_main, z_main, z_main, lq1, lk1, lq2, lk2, norm_g)


def _mix_kernel(ha_ref, ob_ref, ga_ref, gb_ref, x_ref, wa_ref, wb_ref, wo_ref, g1_ref, b1_ref, wr_ref, br_ref,
                x1_ref, ri_ref, rw_ref, cnt_ref, carry_ref):
    i = pl.program_id(0)
    tm = x_ref.shape[0]

    @pl.when(i == 0)
    def _():
        carry_ref[...] = jnp.zeros_like(carry_ref)

    ya = jnp.dot(ha_ref[...], wa_ref[...], preferred_element_type=F32)
    yb = jnp.dot(ob_ref[...], wb_ref[...], preferred_element_type=F32)
    merged = _sigmoid(ga_ref[...].astype(F32)) * ya + _sigmoid(gb_ref[...].astype(F32)) * yb
    mix = jnp.dot(merged.astype(BF16), wo_ref[...], preferred_element_type=F32)
    pre = DEEPNORM_ALPHA * x_ref[...] + mix
    mu = jnp.mean(pre, axis=-1, keepdims=True)
    pc = pre - mu
    var = jnp.mean(pc * pc, axis=-1, keepdims=True)
    x1 = pc * lax.rsqrt(var + LN_EPS) * g1_ref[...] + b1_ref[...]
    x1_ref[...] = x1

    logits = jnp.dot(x1, wr_ref[...], preferred_element_type=F32, precision=lax.Precision.HIGHEST) + br_ref[...]
    lt = logits.T
    row8 = lax.broadcasted_iota(I32, (SUBLANES, tm), 0).astype(F32)
    gl = jnp.where(row8 < N_GROUPS, lt[0:SUBLANES, :], NEG_BIG)
    gmax = jnp.max(gl, axis=0, keepdims=True)
    gsel = jnp.min(jnp.where(gl == gmax, row8, float(SUBLANES)), axis=0, keepdims=True)
    gprob = 1.0 / jnp.sum(jnp.exp(gl - gmax), axis=0, keepdims=True)
    ing = lt[SUBLANES:2 * SUBLANES, :]
    for g in range(1, N_GROUPS):
        ing = jnp.where(gsel == g, lt[(g + 1) * SUBLANES:(g + 2) * SUBLANES, :], ing)
    v0 = jnp.max(ing, axis=0, keepdims=True)
    i0 = jnp.min(jnp.where(ing == v0, row8, float(SUBLANES)), axis=0, keepdims=True)
    ing2 = jnp.where(row8 == i0, -jnp.inf, ing)
    v1 = jnp.max(ing2, axis=0, keepdims=True)
    i1 = jnp.min(jnp.where(ing2 == v1, row8, float(SUBLANES)), axis=0, keepdims=True)
    ex = jnp.exp(v1 - v0)
    inv = 1.0 / (1.0 + ex)
    w0 = gprob * inv
    w1 = gprob * (ex * inv)
    e0 = (gsel * EXPERTS_PER_GROUP + i0).astype(I32)
    e1 = (gsel * EXPERTS_PER_GROUP + i1).astype(I32)

    rowe = lax.broadcasted_iota(I32, (N_EXPERTS, tm), 0)
    is0 = rowe == e0
    is1 = rowe == e1
    oh = jnp.where(is0, 1.0, 0.0) + jnp.where(is1, 1.0, 0.0)
    tr = lax.broadcasted_iota(I32, (tm, tm), 0)
    tc = lax.broadcasted_iota(I32, (tm, tm), 1)
    su = jnp.where(tr < tc, 1.0, 0.0).astype(BF16)
    tot = jnp.dot(oh.astype(BF16), su, preferred_element_type=F32) + carry_ref[...]
    rank0 = jnp.sum(jnp.where(is0, tot, 0.0), axis=0, keepdims=True)
    rank1 = jnp.sum(jnp.where(is1, tot, 0.0), axis=0, keepdims=True)
    carry = carry_ref[...] + jnp.sum(oh, axis=1, keepdims=True)
    carry_ref[...] = carry
    cnt_ref[...] = jnp.broadcast_to(carry, cnt_ref.shape).astype(I32)

    zi = jnp.zeros((1, tm), I32)
    ri_ref[...] = jnp.concatenate([e0, e1, rank0.astype(I32), rank1.astype(I32), zi, zi, zi, zi], axis=0)
    zf = jnp.zeros((1, tm), F32)
    rw_ref[...] = jnp.concatenate([w0, w1, zf, zf, zf, zf, zf, zf], axis=0)


def _mix(h_a, o_b, z_main, x2d, w_a, w_b, w_out, ln_g, ln_b, w_r, b_r):
    T, D = x2d.shape
    tm = min(MIX_TM, T)
    const = lambda shape: pl.BlockSpec(shape, lambda i: (0, 0), pipeline_mode=pl.Buffered(1))
    return pl.pallas_call(
        _mix_kernel,
        grid=(T // tm,),
        in_specs=[
            pl.BlockSpec((tm, A_WIDTH), lambda i: (i, 0)),
            pl.BlockSpec((tm, B_WIDTH), lambda i: (i, 0)),
            pl.BlockSpec((tm, D), lambda i: (i, COL_GA // D)),
            pl.BlockSpec((tm, D), lambda i: (i, COL_GB // D)),
            pl.BlockSpec((tm, D), lambda i: (i, 0)),
            const((A_WIDTH, D)), const((B_WIDTH, D)), const((D, D)),
            const((1, D)), const((1, D)), const((D, LANES)), const((1, LANES)),
        ],
        out_specs=[
            pl.BlockSpec((tm, D), lambda i: (i, 0)),
            pl.BlockSpec((SUBLANES, tm), lambda i: (0, i)),
            pl.BlockSpec((SUBLANES, tm), lambda i: (0, i)),
            pl.BlockSpec((N_EXPERTS, LANES), lambda i: (0, 0)),
        ],
        out_shape=[
            jax.ShapeDtypeStruct((T, D), F32),
            jax.ShapeDtypeStruct((SUBLANES, T), I32),
            jax.ShapeDtypeStruct((SUBLANES, T), F32),
            jax.ShapeDtypeStruct((N_EXPERTS, LANES), I32),
        ],
        scratch_shapes=[pltpu.VMEM((N_EXPERTS, 1), F32)],
        compiler_params=_cparams(("arbitrary",)),
        name="mix_ln1_router",
    )(h_a, o_b, z_main, z_main, x2d, w_a, w_b, w_out, ln_g, ln_b, w_r, b_r)


def _row_copy(src_ref, src_row, dst_ref, dst_row, sem):
    return pltpu.make_async_copy(src_ref.at[pl.ds(src_row, 1), :], dst_ref.at[pl.ds(dst_row, 1), :], sem)


def _dispatch_kernel(cnt_ref, ri_ref, x1_ref, xpad_ref, pstart_ref, bexp_ref, nused_ref,
                     pst_ref, zero_ref, sem, *, tb, n_blocks):
    i = pl.program_id(0)
    tm = x1_ref.shape[0]

    @pl.when(i == 0)
    def _():
        def fill(b, carry):
            bexp_ref[b] = 0
            return carry
        lax.fori_loop(0, n_blocks, fill, 0)

        def per_expert(e, blk):
            pst_ref[e] = blk * tb
            pstart_ref[e] = blk * tb
            nb = (cnt_ref[e] + (tb - 1)) // tb

            def mark(b, carry):
                bexp_ref[blk + b] = e
                return carry
            lax.fori_loop(0, nb, mark, 0)
            return blk + nb
        nused = lax.fori_loop(0, N_EXPERTS, per_expert, 0)
        nused_ref[0] = nused
        pst_ref[N_EXPERTS] = nused

    def issue(t, carry):
        for k in range(TOP_K):
            pos = pst_ref[ri_ref[k, t]] + ri_ref[TOP_K + k, t]
            _row_copy(x1_ref, t, xpad_ref, pos, sem).start()
        return carry
    lax.fori_loop(0, tm, issue, 0)

    def drain(t, carry):
        _row_copy(x1_ref, 0, xpad_ref, 0, sem).wait()
        return carry
    lax.fori_loop(0, TOP_K * tm, drain, 0)

    @pl.when(i == pl.num_programs(0) - 1)
    def _():
        zero_ref[...] = jnp.zeros_like(zero_ref)

        def per_expert(e, carry):
            cnt = cnt_ref[e]
            first = pst_ref[e] + cnt
            npad = ((cnt + (tb - 1)) // tb) * tb - cnt

            def start(p, c2):
                _row_copy(zero_ref, 0, xpad_ref, first + p, sem).start()
                return c2
            lax.fori_loop(0, npad, start, 0)

            def wait(p, c2):
                _row_copy(zero_ref, 0, xpad_ref, 0, sem).wait()
                return c2
            lax.fori_loop(0, npad, wait, 0)
            return carry
        lax.fori_loop(0, N_EXPERTS, per_expert, 0)

        def block_copy(b):
            return pltpu.make_async_copy(zero_ref, xpad_ref.at[pl.ds(pl.multiple_of(b * tb, tb), tb), :], sem)

        def start_block(b, carry):
            block_copy(b).start()
            return carry
        lax.fori_loop(pst_ref[N_EXPERTS], n_blocks, start_block, 0)

        def wait_block(b, carry):
            block_copy(b).wait()
            return carry
        lax.fori_loop(pst_ref[N_EXPERTS], n_blocks, wait_block, 0)


def _dispatch(counts, route_i, x1, tb, n_blocks):
    T, D = x1.shape
    tm = min(DISPATCH_TM, T)
    smem_full = pl.BlockSpec(memory_space=pltpu.SMEM)
    return pl.pallas_call(
        functools.partial(_dispatch_kernel, tb=tb, n_blocks=n_blocks),
        grid=(T // tm,),
        in_specs=[
            smem_full,
            pl.BlockSpec((SUBLANES, tm), lambda i: (0, i), memory_space=pltpu.SMEM),
            pl.BlockSpec((tm, D), lambda i: (i, 0)),
        ],
        out_specs=[
            pl.BlockSpec(memory_space=pl.ANY),
            smem_full, smem_full, smem_full,
        ],
        out_shape=[
            jax.ShapeDtypeStruct((n_blocks * tb, D), F32),
            jax.ShapeDtypeStruct((N_EXPERTS,), I32),
            jax.ShapeDtypeStruct((n_blocks,), I32),
            jax.ShapeDtypeStruct((1,), I32),
        ],
        scratch_shapes=[
            pltpu.SMEM((N_EXPERTS + 1,), I32),
            pltpu.VMEM((tb, D), F32),
            pltpu.SemaphoreType.DMA(()),
        ],
        compiler_params=_cparams(("arbitrary",)),
        name="moe_dispatch",
    )(counts, route_i, x1)


def _expert_kernel(bexp_ref, nused_ref, x_ref, wg_ref, wu_ref, wd_ref, y_ref):
    used = pl.program_id(0) < nused_ref[0]

    @pl.when(used)
    def _():
        x = x_ref[...].astype(BF16)
        gate = jnp.dot(x, wg_ref[...], preferred_element_type=F32)
        up = jnp.dot(x, wu_ref[...], preferred_element_type=F32)
        hid = (gate * _sigmoid(gate) * up).astype(BF16)
        y_ref[...] = jnp.dot(hid, wd_ref[...], preferred_element_type=F32)

    @pl.when(jnp.logical_not(used))
    def _():
        y_ref[...] = jnp.zeros_like(y_ref)


def _experts(bexp, nused, x_pad, w_gate, w_up, w_down, tb):
    P, D = x_pad.shape
    n_blocks = P // tb
    F = w_gate.shape[-1]
    blk = lambda b, be, nu: jnp.minimum(b, nu[0] - 1)
    exp = lambda b, be, nu: be[jnp.minimum(b, nu[0] - 1)]
    return pl.pallas_call(
        _expert_kernel,
        grid_spec=pltpu.PrefetchScalarGridSpec(
            num_scalar_prefetch=2,
            grid=(n_blocks,),
            in_specs=[
                pl.BlockSpec((tb, D), lambda b, be, nu: (blk(b, be, nu), 0)),
                pl.BlockSpec((None, D, F), lambda b, be, nu: (exp(b, be, nu), 0, 0)),
                pl.BlockSpec((None, D, F), lambda b, be, nu: (exp(b, be, nu), 0, 0)),
                pl.BlockSpec((None, F, D), lambda b, be, nu: (exp(b, be, nu), 0, 0)),
            ],
            out_specs=pl.BlockSpec((tb, D), lambda b, be, nu: (b, 0)),
        ),
        out_shape=jax.ShapeDtypeStruct((P, D), F32),
        compiler_params=_cparams(("arbitrary",)),
        name="moe_experts",
    )(bexp, nused, x_pad, w_gate, w_up, w_down)


def _combine_kernel(pstart_ref, ri_ref, x1_ref, rw_ref, g_ref, b_ref, ypad_ref, o_ref, ybuf_ref, sem):
    tm = x1_ref.shape[0]

    def issue(t, carry):
        for k in range(TOP_K):
            pos = pstart_ref[ri_ref[k, t]] + ri_ref[TOP_K + k, t]
            _row_copy(ypad_ref, pos, ybuf_ref.at[k], t, sem).start()
        return carry
    lax.fori_loop(0, tm, issue, 0)

    def drain(t, carry):
        _row_copy(ypad_ref, 0, ybuf_ref.at[0], 0, sem).wait()
        return carry
    lax.fori_loop(0, TOP_K * tm, drain, 0)

    rw = rw_ref[...]
    ffn = ybuf_ref[0] * rw[:, 0:1] + ybuf_ref[1] * rw[:, 1:2]
    pre = DEEPNORM_ALPHA * x1_ref[...] + ffn
    mu = jnp.mean(pre, axis=-1, keepdims=True)
    pc = pre - mu
    var = jnp.mean(pc * pc, axis=-1, keepdims=True)
    o_ref[...] = pc * lax.rsqrt(var + LN_EPS) * g_ref[...] + b_ref[...]


def _combine(pstart, route_i, x1, rw_col, ln_g, ln_b, y_pad):
    T, D = x1.shape
    tm = min(COMBINE_TM, T)
    return pl.pallas_call(
        _combine_kernel,
        grid=(T // tm,),
        in_specs=[
            pl.BlockSpec(memory_space=pltpu.SMEM),
            pl.BlockSpec((SUBLANES, tm), lambda i: (0, i), memory_space=pltpu.SMEM),
            pl.BlockSpec((tm, D), lambda i: (i, 0)),
            pl.BlockSpec((tm, SUBLANES), lambda i: (i, 0)),
            pl.BlockSpec((1, D), lambda i: (0, 0)),
            pl.BlockSpec((1, D), lambda i: (0, 0)),
            pl.BlockSpec(memory_space=pl.ANY),
        ],
        out_specs=pl.BlockSpec((tm, D), lambda i: (i, 0)),
        out_shape=jax.ShapeDtypeStruct((T, D), F32),
        scratch_shapes=[
            pltpu.VMEM((TOP_K, tm, D), F32),
            pltpu.SemaphoreType.DMA(()),
        ],
        compiler_params=_cparams(("arbitrary",)),
        name="moe_combine_ln2",
    )(pstart, route_i, x1, rw_col, ln_g, ln_b, y_pad)


def _rearrange_in_proj(w_in, b_in):
    offs = [0]
    for cwidth in (A_WIDTH, A_WIDTH, A_WIDTH, A_WIDTH, A_HEADS, A_HEADS, B_WIDTH, B_WIDTH, B_WIDTH, D_MODEL, D_MODEL):
        offs.append(offs[-1] + cwidth)
    main_cols = [(9, 11), (0, 4), (6, 9)]
    w_main = jnp.concatenate([w_in[:, offs[a]:offs[b]] for a, b in main_cols], axis=1).astype(BF16)
    b_main = jnp.concatenate([b_in[offs[a]:offs[b]] for a, b in main_cols])[None, :]
    ngate = 2 * A_HEADS
    w_gate = jnp.pad(w_in[:, offs[4]:offs[6]], ((0, 0), (0, LANES - ngate)))
    b_gate = jnp.pad(b_in[offs[4]:offs[6]], (0, LANES - ngate))[None, :]
    return w_main, b_main, w_gate, b_gate


def _layer(x, w_in, b_in, conv_w, conv_b, norm_a_g, lq1, lk1, lq2, lk2, norm_b_g, w_a, w_b, w_out,
           ln1_g, ln1_b, w_grp, b_grp, w_exp, b_exp, w_gate, w_up, w_down, ln2_g, ln2_b, lambda_init):
    B, S, D = x.shape
    T = B * S
    x2d = x.reshape(T, D)

    w_main, b_main, w_g, b_g = _rearrange_in_proj(w_in, b_in)
    z_main, z_gate = _in_projection(x2d, w_main, b_main, w_g, b_g)

    L = min(MLSTM_CHUNK, S)
    gp = _gate_prep(z_gate[:, :2 * A_HEADS].T, L)
    grow = jnp.stack([gp[:A_HEADS], gp[A_HEADS:]], axis=1)
    grow = jnp.pad(grow, ((0, 0), (0, SUBLANES - 2), (0, 0)))
    gcol = jnp.swapaxes(grow, 1, 2)

    h_a = _mlstm(z_main, gcol, grow, conv_w, conv_b[None, :], norm_a_g[None, :], B, S)
    o_b = _diff_attention(z_main, lq1[None, :], lk1[None, :], lq2[None, :], lk2[None, :],
                          norm_b_g[None, :], B, S, lambda_init)

    w_r = jnp.zeros((D, LANES), F32).at[:, :N_GROUPS].set(w_grp).at[:, SUBLANES:SUBLANES + N_EXPERTS].set(w_exp)
    b_r = jnp.zeros((LANES,), F32).at[:N_GROUPS].set(b_grp).at[SUBLANES:SUBLANES + N_EXPERTS].set(b_exp)[None, :]
    x1, route_i, route_w, counts = _mix(h_a, o_b, z_main, x2d, w_a.astype(BF16), w_b.astype(BF16),
                                        w_out.astype(BF16), ln1_g[None, :], ln1_b[None, :], w_r, b_r)

    tb = MOE_TB
    n_blocks = (T * TOP_K) // tb + N_EXPERTS
    x_pad, pstart, bexp, nused = _dispatch(counts[:, 0], route_i, x1, tb, n_blocks)
    y_pad = _experts(bexp, nused, x_pad, w_gate.astype(BF16), w_up.astype(BF16), w_down.astype(BF16), tb)
    out = _combine(pstart, route_i, x1, route_w.T, ln2_g[None, :], ln2_b[None, :], y_pad)
    return out.reshape(B, S, D)


def kernel(x, w_in, b_in, conv_w, conv_b, mlstm_norm_g, lambda_q1, lambda_k1, lambda_q2, lambda_k2,
           diff_norm_g, w_a, w_b, w_out, ln1_g, ln1_b, w_grp, b_grp, w_exp, b_exp,
           w_gate, w_up, w_down, ln2_g, ln2_b):
    for l in range(DEPTH):
        lambda_init = 0.8 - 0.6 * math.exp(-0.3 * l)
        x = _layer(x, w_in[l], b_in[l], conv_w[l], conv_b[l], mlstm_norm_g[l], lambda_q1[l], lambda_k1[l],
                   lambda_q2[l], lambda_k2[l], diff_norm_g[l], w_a[l], w_b[l], w_out[l], ln1_g[l], ln1_b[l],
                   w_grp[l], b_grp[l], w_exp[l], b_exp[l], w_gate[l], w_up[l], w_down[l], ln2_g[l], ln2_b[l],
                   lambda_init)
    return x
```

```python
import functools
import math

import jax
import jax.numpy as jnp
from jax import lax
from jax.experimental import pallas as pl
from jax.experimental.pallas import tpu as pltpu

F32 = jnp.float32
BF16 = jnp.bfloat16
I32 = jnp.int32

D_MODEL = 2048
A_HEADS = 4
A_HEAD_DIM = 256
A_WIDTH = A_HEADS * A_HEAD_DIM
CONV_WIDTH = 4
B_HEADS = 8
B_HEAD_DIM = 64
B_V_DIM = 2 * B_HEAD_DIM
B_WIDTH = B_HEADS * B_V_DIM
ATTN_CHUNK = 64
N_GROUPS = 4
EXPERTS_PER_GROUP = 8
N_EXPERTS = N_GROUPS * EXPERTS_PER_GROUP
TOP_K = 2
D_EXPERT = 512
DEPTH = 1
DEEPNORM_ALPHA = (2 * DEPTH) ** 0.25
LN_EPS = 1e-5
NEG_BIG = -1e30

LANES = 128
SUBLANES = 8
VMEM_LIMIT_BYTES = 56 * 1024 * 1024

COL_GA = 0
COL_GB = COL_GA + D_MODEL
COL_QA = COL_GB + D_MODEL
COL_KA = COL_QA + A_WIDTH
COL_VA = COL_KA + A_WIDTH
COL_OA = COL_VA + A_WIDTH
COL_QB = COL_OA + A_WIDTH
COL_KB = COL_QB + B_WIDTH
COL_VB = COL_KB + B_WIDTH
N_MAIN = COL_VB + B_WIDTH

PROJ_TM, PROJ_TN = 1024, 512
MLSTM_CHUNK = 256
ATTN_TQ = 512
MIX_TM = 256
MOE_TB = 256
DISPATCH_TM = 256
COMBINE_TM = 256


def _cparams(sem, vmem=VMEM_LIMIT_BYTES):
    return pltpu.CompilerParams(dimension_semantics=sem, vmem_limit_bytes=vmem)


def _sigmoid(x):
    return 1.0 / (1.0 + jnp.exp(-x))


def _proj_kernel(x_ref, w_ref, b_ref, wg_ref, bg_ref, z_ref, zg_ref, xb_ref):
    @pl.when(pl.program_id(1) == 0)
    def _():
        x = x_ref[...]
        xb_ref[...] = x.astype(BF16)
        zg_ref[...] = jnp.dot(x, wg_ref[...], preferred_element_type=F32,
                              precision=lax.Precision.HIGHEST) + bg_ref[...]

    acc = jnp.dot(xb_ref[...], w_ref[...], preferred_element_type=F32)
    z_ref[...] = (acc + b_ref[...]).astype(z_ref.dtype)


def _in_projection(x2d, w_main, b_main, w_gate, b_gate):
    T, K = x2d.shape
    N = w_main.shape[1]
    tm, tn = min(PROJ_TM, T), PROJ_TN
    return pl.pallas_call(
        _proj_kernel,
        grid=(T // tm, N // tn),
        in_specs=[
            pl.BlockSpec((tm, K), lambda i, j: (i, 0)),
            pl.BlockSpec((K, tn), lambda i, j: (0, j)),
            pl.BlockSpec((1, tn), lambda i, j: (0, j)),
            pl.BlockSpec((K, LANES), lambda i, j: (0, 0)),
            pl.BlockSpec((1, LANES), lambda i, j: (0, 0)),
        ],
        out_specs=[
            pl.BlockSpec((tm, tn), lambda i, j: (i, j)),
            pl.BlockSpec((tm, LANES), lambda i, j: (i, 0)),
        ],
        out_shape=[
            jax.ShapeDtypeStruct((T, N), BF16),
            jax.ShapeDtypeStruct((T, LANES), F32),
        ],
        scratch_shapes=[pltpu.VMEM((tm, K), BF16)],
        compiler_params=_cparams(("parallel", "arbitrary")),
        name="in_projection",
    )(x2d, w_main, b_main, w_gate, b_gate)


def _gate_prep_kernel(g_ref, o_ref):
    g = g_ref[...]
    L = g.shape[1]
    lf = jnp.minimum(g, 0.0) - jnp.log(1.0 + jnp.exp(-jnp.abs(g)))
    r = lax.broadcasted_iota(I32, (L, L), 0)
    c = lax.broadcasted_iota(I32, (L, L), 1)
    tri = (r <= c).astype(F32)
    b = jnp.dot(lf, tri, preferred_element_type=F32, precision=lax.Precision.HIGHEST)
    row = lax.broadcasted_iota(I32, g.shape, 0)
    o_ref[...] = jnp.where(row < A_HEADS, g, b)


def _gate_prep(g_rows, L):
    R, T = g_rows.shape
    return pl.pallas_call(
        _gate_prep_kernel,
        grid=(T // L,),
        in_specs=[pl.BlockSpec((R, L), lambda i: (0, i))],
        out_specs=pl.BlockSpec((R, L), lambda i: (0, i)),
        out_shape=jax.ShapeDtypeStruct((R, T), F32),
        compiler_params=_cparams(("parallel",)),
        name="gate_prep",
    )(g_rows)


def _mlstm_kernel(q_ref, k_ref, v_ref, o_ref, gcol_ref, grow_ref, cwq_ref, cwk_ref, cbq_ref, cbk_ref,
                  ng_ref, out_ref, c_ref, n_ref, m_ref, qcar_ref, kcar_ref):
    ci = pl.program_id(2)
    L = q_ref.shape[0]
    dh = q_ref.shape[1]

    @pl.when(ci == 0)
    def _():
        c_ref[...] = jnp.zeros_like(c_ref)
        n_ref[...] = jnp.zeros_like(n_ref)
        m_ref[...] = jnp.zeros_like(m_ref)
        qcar_ref[...] = jnp.zeros_like(qcar_ref)
        kcar_ref[...] = jnp.zeros_like(kcar_ref)

    def conv_silu(u_ref, car_ref, w_ref, b_ref):
        u = u_ref[...].astype(F32)
        ext = jnp.concatenate([car_ref[...], u], axis=0)
        w = w_ref[...]
        y = b_ref[...] + w[3:4, :] * u
        for j in range(CONV_WIDTH - 1):
            off = SUBLANES - (CONV_WIDTH - 1) + j
            y = y + w[j:j + 1, :] * ext[off:off + L, :]
        car_ref[...] = u[L - SUBLANES:, :]
        return y * _sigmoid(y)

    q = conv_silu(q_ref, qcar_ref, cwq_ref, cbq_ref)
    k = conv_silu(k_ref, kcar_ref, cwk_ref, cbk_ref) * (dh ** -0.5)
    v_b = v_ref[...]
    q_b = q.astype(BF16)
    k_b = k.astype(BF16)

    gcol = gcol_ref[...]
    grow = grow_ref[...]
    i_col, b_col = gcol[:, 0:1], gcol[:, 1:2]
    i_row, b_row = grow[0:1, :], grow[1:2, :]
    b_last = b_row[:, L - 1:L]
    m_prev = m_ref[...]

    r = lax.broadcasted_iota(I32, (L, L), 0)
    c = lax.broadcasted_iota(I32, (L, L), 1)
    dmat = jnp.where(r >= c, b_col - b_row + i_row, NEG_BIG)
    inter = b_col + m_prev
    m_t = jnp.maximum(inter, jnp.max(dmat, axis=-1, keepdims=True))
    s = lax.dot_general(q_b, k_b, (((1,), (1,)), ((), ())), preferred_element_type=F32)
    w_intra = jnp.exp(dmat - m_t) * s
    w_inter = jnp.exp(inter - m_t)
    qc = jnp.dot(q_b, c_ref[...].astype(BF16), preferred_element_type=F32)
    num = w_inter * qc + jnp.dot(w_intra.astype(BF16), v_b, preferred_element_type=F32)
    qn = jnp.sum(q * n_ref[...], axis=-1, keepdims=True)
    den = w_inter * qn + jnp.sum(w_intra, axis=-1, keepdims=True)
    hh = num / jnp.maximum(jnp.abs(den), jnp.exp(-m_t))

    g_col = b_last - b_col + i_col
    m_new = jnp.maximum(b_last + m_prev, jnp.max(g_col, axis=0, keepdims=True))
    decay = jnp.exp(b_last + m_prev - m_new)
    kw = jnp.exp(g_col - m_new) * k
    c_ref[...] = decay * c_ref[...] + jnp.dot(kw.T.astype(BF16), v_b, preferred_element_type=F32)
    n_ref[...] = decay * n_ref[...] + jnp.sum(kw, axis=0, keepdims=True)
    m_ref[...] = m_new

    y = _sigmoid(o_ref[...].astype(F32)) * hh
    mu = jnp.mean(y, axis=-1, keepdims=True)
    yc = y - mu
    var = jnp.mean(yc * yc, axis=-1, keepdims=True)
    out_ref[...] = (yc * lax.rsqrt(var + LN_EPS) * ng_ref[...]).astype(out_ref.dtype)


def _mlstm(z_main, gcol, grow, conv_w, conv_b, norm_g, B, S):
    L = min(MLSTM_CHUNK, S)
    nc = S // L
    dh = A_HEAD_DIM
    H = A_HEADS
    row = lambda b, h, c: b * nc + c
    return pl.pallas_call(
        _mlstm_kernel,
        grid=(B, H, nc),
        in_specs=[
            pl.BlockSpec((L, dh), lambda b, h, c: (row(b, h, c), COL_QA // dh + h)),
            pl.BlockSpec((L, dh), lambda b, h, c: (row(b, h, c), COL_KA // dh + h)),
            pl.BlockSpec((L, dh), lambda b, h, c: (row(b, h, c), COL_VA // dh + h)),
            pl.BlockSpec((L, dh), lambda b, h, c: (row(b, h, c), COL_OA // dh + h)),
            pl.BlockSpec((None, L, SUBLANES), lambda b, h, c: (h, row(b, h, c), 0)),
            pl.BlockSpec((None, SUBLANES, L), lambda b, h, c: (h, 0, row(b, h, c))),
            pl.BlockSpec((CONV_WIDTH, dh), lambda b, h, c: (0, h)),
            pl.BlockSpec((CONV_WIDTH, dh), lambda b, h, c: (0, H + h)),
            pl.BlockSpec((1, dh), lambda b, h, c: (0, h)),
            pl.BlockSpec((1, dh), lambda b, h, c: (0, H + h)),
            pl.BlockSpec((1, dh), lambda b, h, c: (0, h)),
        ],
        out_specs=pl.BlockSpec((L, dh), lambda b, h, c: (row(b, h, c), h)),
        out_shape=jax.ShapeDtypeStruct((B * S, A_WIDTH), BF16),
        scratch_shapes=[
            pltpu.VMEM((dh, dh), F32),
            pltpu.VMEM((1, dh), F32),
            pltpu.VMEM((1, 1), F32),
            pltpu.VMEM((SUBLANES, dh), F32),
            pltpu.VMEM((SUBLANES, dh), F32),
        ],
        compiler_params=_cparams(("parallel", "parallel", "arbitrary")),
        name="mlstm",
    )(z_main, z_main, z_main, z_main, gcol, grow, conv_w, conv_w, conv_b, conv_b, norm_g)


def _lane_tiles(t):
    return [t[:, i * LANES:(i + 1) * LANES] for i in range(t.shape[1] // LANES)]


def _attn_kernel(slope_ref, q_ref, k_ref, kb_ref, v_ref, lq1_ref, lk1_ref, lq2_ref, lk2_ref, g_ref, o_ref,
                 s_ref, mx_ref, ls_ref, acc_ref, *, lambda_init):
    h = pl.program_id(1)
    qi = pl.program_id(2)
    tq = q_ref.shape[0]
    tk = tq
    slope = slope_ref[h]
    nt = (((1,), (1,)), ((), ()))

    lane = lax.broadcasted_iota(I32, (tq, B_V_DIM), 1)
    q = q_ref[...] * (B_HEAD_DIM ** -0.5)
    zero = jnp.zeros_like(q)
    bias_cols = jnp.where(lane < 2, 1.0, 0.0).astype(BF16)
    qa = (jnp.concatenate([jnp.where(lane < B_HEAD_DIM, q, zero), bias_cols], axis=1),
          jnp.concatenate([jnp.where(lane >= B_HEAD_DIM, q, zero), bias_cols], axis=1))

    def scores(j):
        start = pl.multiple_of(j * tk, tk)
        ka = jnp.concatenate([k_ref[pl.ds(start, tk), :], kb_ref[pl.ds(start, tk), :]], axis=1)
        return [lax.dot_general(qa[n], ka, nt, preferred_element_type=F32) for n in range(2)]

    def keep(j, n, t):
        s_ref[n, j] = t
        mx_ref[n] = functools.reduce(jnp.maximum, _lane_tiles(t), mx_ref[n])

    mx_ref[...] = jnp.full(mx_ref.shape, NEG_BIG, F32)

    def pass_a(j, carry):
        t = scores(j)
        for n in range(2):
            keep(j, n, t[n])
        return carry
    lax.fori_loop(0, qi, pass_a, 0)

    r = lax.broadcasted_iota(I32, (tq, tk), 0)
    c = lax.broadcasted_iota(I32, (tq, tk), 1)
    ahead = jnp.where(c <= r, 0.0, (r - c).astype(F32) * (2.0 * slope))
    corr = jnp.where((c // ATTN_CHUNK) <= (r // ATTN_CHUNK), ahead, NEG_BIG)
    t = scores(qi)
    for n in range(2):
        keep(qi, n, t[n] + corr)

    for n in range(2):
        m = jnp.max(mx_ref[n], axis=-1, keepdims=True)
        mx_ref[n] = jnp.broadcast_to(m, (tq, LANES))
    ls_ref[...] = jnp.zeros_like(ls_ref)
    acc_ref[...] = jnp.zeros_like(acc_ref)

    def pass_b(j, carry):
        start = pl.multiple_of(j * tk, tk)
        vj = v_ref[pl.ds(start, tk), :]
        for n in range(2):
            mb = mx_ref[n]
            ps = [jnp.exp(t - mb) for t in _lane_tiles(s_ref[n, j])]
            ls_ref[n] += functools.reduce(jnp.add, ps)
            p = jnp.concatenate(ps, axis=1).astype(BF16)
            acc_ref[n] += jnp.dot(p, vj, preferred_element_type=F32)
        return carry
    lax.fori_loop(0, qi + 1, pass_b, 0)

    lam = (jnp.exp(jnp.sum(lq1_ref[...] * lk1_ref[...], axis=-1, keepdims=True))
           - jnp.exp(jnp.sum(lq2_ref[...] * lk2_ref[...], axis=-1, keepdims=True)) + lambda_init)
    l0 = jnp.sum(ls_ref[0], axis=-1, keepdims=True)
    l1 = jnp.sum(ls_ref[1], axis=-1, keepdims=True)
    o = acc_ref[0] / l0 - lam * (acc_ref[1] / l1)
    ms = jnp.mean(o * o, axis=-1, keepdims=True)
    o_ref[...] = (o * lax.rsqrt(ms + LN_EPS) * g_ref[...] * (1.0 - lambda_init)).astype(o_ref.dtype)


def _diff_attention(z_main, lq1, lk1, lq2, lk2, norm_g, B, S, lambda_init):
    tq = min(ATTN_TQ, S)
    nq = S // tq
    H = B_HEADS
    dv = B_V_DIM
    slopes = 2.0 ** (-8.0 * jnp.arange(1, H + 1, dtype=F32) / H)
    assert S <= 256 * 256
    pos = jnp.arange(S, dtype=I32)
    within = (pos % 256).astype(F32)
    base = (pos - pos % 256).astype(F32)
    kbias = jnp.zeros((H, S, dv), F32)
    kbias = kbias.at[:, :, 0].set(slopes[:, None] * within[None, :]).at[:, :, 1].set(slopes[:, None] * base[None, :])
    kbias = kbias.astype(BF16)
    small = pl.BlockSpec((1, B_HEAD_DIM), lambda b, h, i: (0, 0))
    return pl.pallas_call(
        functools.partial(_attn_kernel, lambda_init=lambda_init),
        grid=(B, H, nq),
        in_specs=[
            pl.BlockSpec(memory_space=pltpu.SMEM),
            pl.BlockSpec((tq, dv), lambda b, h, i: (b * nq + i, COL_QB // dv + h)),
            pl.BlockSpec((S, dv), lambda b, h, i: (b, COL_KB // dv + h)),
            pl.BlockSpec((None, S, dv), lambda b, h, i: (h, 0, 0)),
            pl.BlockSpec((S, dv), lambda b, h, i: (b, COL_VB // dv + h)),
            small, small, small, small,
            pl.BlockSpec((1, dv), lambda b, h, i: (0, 0)),
        ],
        out_specs=pl.BlockSpec((tq, dv), lambda b, h, i: (b * nq + i, h)),
        out_shape=jax.ShapeDtypeStruct((B * S, B_WIDTH), BF16),
        scratch_shapes=[
            pltpu.VMEM((2, nq, tq, tq), F32),
            pltpu.VMEM((2, tq, LANES), F32),
            pltpu.VMEM((2, tq, LANES), F32),
            pltpu.VMEM((2, tq, dv), F32),
        ],
        compiler_params=_cparams(("parallel", "parallel", "arbitrary")),
        name="diff_attention",
    )(slopes, z_main, z_main, kbias, z_main, lq1, lk1, lq2, lk2, norm_g)


def _mix_kernel(ha_ref, ob_ref, ga_ref, gb_ref, x_ref, wa_ref, wb_ref, wo_ref, g1_ref, b1_ref, wr_ref, br_ref,
                x1_ref, ri_ref, rw_ref, cnt_ref, carry_ref):
    i = pl.program_id(0)
    tm = x_ref.shape[0]

    @pl.when(i == 0)
    def _():
        carry_ref[...] = jnp.zeros_like(carry_ref)

    ya = jnp.dot(ha_ref[...], wa_ref[...], preferred_element_type=F32)
    yb = jnp.dot(ob_ref[...], wb_ref[...], preferred_element_type=F32)
    merged = _sigmoid(ga_ref[...].astype(F32)) * ya + _sigmoid(gb_ref[...].astype(F32)) * yb
    mix = jnp.dot(merged.astype(BF16), wo_ref[...], preferred_element_type=F32)
    pre = DEEPNORM_ALPHA * x_ref[...] + mix
    mu = jnp.mean(pre, axis=-1, keepdims=True)
    pc = pre - mu
    var = jnp.mean(pc * pc, axis=-1, keepdims=True)
    x1 = pc * lax.rsqrt(var + LN_EPS) * g1_ref[...] + b1_ref[...]
    x1_ref[...] = x1

    logits = jnp.dot(x1, wr_ref[...], preferred_element_type=F32, precision=lax.Precision.HIGHEST) + br_ref[...]
    lt = logits.T
    row8 = lax.broadcasted_iota(I32, (SUBLANES, tm), 0).astype(F32)
    gl = jnp.where(row8 < N_GROUPS, lt[0:SUBLANES, :], NEG_BIG)
    gmax = jnp.max(gl, axis=0, keepdims=True)
    gsel = jnp.min(jnp.where(gl == gmax, row8, float(SUBLANES)), axis=0, keepdims=True)
    gprob = 1.0 / jnp.sum(jnp.exp(gl - gmax), axis=0, keepdims=True)
    ing = lt[SUBLANES:2 * SUBLANES, :]
    for g in range(1, N_GROUPS):
        ing = jnp.where(gsel == g, lt[(g + 1) * SUBLANES:(g + 2) * SUBLANES, :], ing)
    v0 = jnp.max(ing, axis=0, keepdims=True)
    i0 = jnp.min(jnp.where(ing == v0, row8, float(SUBLANES)), axis=0, keepdims=True)
    ing2 = jnp.where(row8 == i0, -jnp.inf, ing)
    v1 = jnp.max(ing2, axis=0, keepdims=True)
    i1 = jnp.min(jnp.where(ing2 == v1, row8, float(SUBLANES)), axis=0, keepdims=True)
    ex = jnp.exp(v1 - v0)
    inv = 1.0 / (1.0 + ex)
    w0 = gprob * inv
    w1 = gprob * (ex * inv)
    e0 = (gsel * EXPERTS_PER_GROUP + i0).astype(I32)
    e1 = (gsel * EXPERTS_PER_GROUP + i1).astype(I32)

    rowe = lax.broadcasted_iota(I32, (N_EXPERTS, tm), 0)
    is0 = rowe == e0
    is1 = rowe == e1
    oh = jnp.where(is0, 1.0, 0.0) + jnp.where(is1, 1.0, 0.0)
    tr = lax.broadcasted_iota(I32, (tm, tm), 0)
    tc = lax.broadcasted_iota(I32, (tm, tm), 1)
    su = jnp.where(tr < tc, 1.0, 0.0).astype(BF16)
    tot = jnp.dot(oh.astype(BF16), su, preferred_element_type=F32) + carry_ref[...]
    rank0 = jnp.sum(jnp.where(is0, tot, 0.0), axis=0, keepdims=True)
    rank1 = jnp.sum(jnp.where(is1, tot, 0.0), axis=0, keepdims=True)
    carry = carry_ref[...] + jnp.sum(oh, axis=1, keepdims=True)
    carry_ref[...] = carry
    cnt_ref[...] = jnp.broadcast_to(carry, cnt_ref.shape).astype(I32)

    zi = jnp.zeros((1, tm), I32)
    ri_ref[...] = jnp.concatenate([e0, e1, rank0.astype(I32), rank1.astype(I32), zi, zi, zi, zi], axis=0)
    zf = jnp.zeros((1, tm), F32)
    rw_ref[...] = jnp.concatenate([w0, w1, zf, zf, zf, zf, zf, zf], axis=0)


def _mix(h_a, o_b, z_main, x2d, w_a, w_b, w_out, ln_g, ln_b, w_r, b_r):
    T, D = x2d.shape
    tm = min(MIX_TM, T)
    const = lambda shape: pl.BlockSpec(shape, lambda i: (0, 0), pipeline_mode=pl.Buffered(1))
    return pl.pallas_call(
        _mix_kernel,
        grid=(T // tm,),
        in_specs=[
            pl.BlockSpec((tm, A_WIDTH), lambda i: (i, 0)),
            pl.BlockSpec((tm, B_WIDTH), lambda i: (i, 0)),
            pl.BlockSpec((tm, D), lambda i: (i, COL_GA // D)),
            pl.BlockSpec((tm, D), lambda i: (i, COL_GB // D)),
            pl.BlockSpec((tm, D), lambda i: (i, 0)),
            const((A_WIDTH, D)), const((B_WIDTH, D)), const((D, D)),
            const((1, D)), const((1, D)), const((D, LANES)), const((1, LANES)),
        ],
        out_specs=[
            pl.BlockSpec((tm, D), lambda i: (i, 0)),
            pl.BlockSpec((SUBLANES, tm), lambda i: (0, i)),
            pl.BlockSpec((SUBLANES, tm), lambda i: (0, i)),
            pl.BlockSpec((N_EXPERTS, LANES), lambda i: (0, 0)),
        ],
        out_shape=[
            jax.ShapeDtypeStruct((T, D), F32),
            jax.ShapeDtypeStruct((SUBLANES, T), I32),
            jax.ShapeDtypeStruct((SUBLANES, T), F32),
            jax.ShapeDtypeStruct((N_EXPERTS, LANES), I32),
        ],
        scratch_shapes=[pltpu.VMEM((N_EXPERTS, 1), F32)],
        compiler_params=_cparams(("arbitrary",)),
        name="mix_ln1_router",
    )(h_a, o_b, z_main, z_main, x2d, w_a, w_b, w_out, ln_g, ln_b, w_r, b_r)


def _row_copy(src_ref, src_row, dst_ref, dst_row, sem):
    return pltpu.make_async_copy(src_ref.at[pl.ds(src_row, 1), :], dst_ref.at[pl.ds(dst_row, 1), :], sem)


def _dispatch_kernel(cnt_ref, ri_ref, x1_ref, xpad_ref, pstart_ref, bexp_ref, nused_ref,
                     pst_ref, zero_ref, sem, *, tb, n_blocks):
    i = pl.program_id(0)
    tm = x1_ref.shape[0]

    @pl.when(i == 0)
    def _():
        def fill(b, carry):
            bexp_ref[b] = 0
            return carry
        lax.fori_loop(0, n_blocks, fill, 0)

        def per_expert(e, blk):
            pst_ref[e] = blk * tb
            pstart_ref[e] = blk * tb
            nb = (cnt_ref[e] + (tb - 1)) // tb

            def mark(b, carry):
                bexp_ref[blk + b] = e
                return carry
            lax.fori_loop(0, nb, mark, 0)
            return blk + nb
        nused = lax.fori_loop(0, N_EXPERTS, per_expert, 0)
        nused_ref[0] = nused
        pst_ref[N_EXPERTS] = nused

    def issue(t, carry):
        for k in range(TOP_K):
            pos = pst_ref[ri_ref[k, t]] + ri_ref[TOP_K + k, t]
            _row_copy(x1_ref, t, xpad_ref, pos, sem).start()
        return carry
    lax.fori_loop(0, tm, issue, 0)

    def drain(t, carry):
        _row_copy(x1_ref, 0, xpad_ref, 0, sem).wait()
        return carry
    lax.fori_loop(0, TOP_K * tm, drain, 0)

    @pl.when(i == pl.num_programs(0) - 1)
    def _():
        zero_ref[...] = jnp.zeros_like(zero_ref)

        def per_expert(e, carry):
            cnt = cnt_ref[e]
            first = pst_ref[e] + cnt
            npad = ((cnt + (tb - 1)) // tb) * tb - cnt

            def start(p, c2):
                _row_copy(zero_ref, 0, xpad_ref, first + p, sem).start()
                return c2
            lax.fori_loop(0, npad, start, 0)

            def wait(p, c2):
                _row_copy(zero_ref, 0, xpad_ref, 0, sem).wait()
                return c2
            lax.fori_loop(0, npad, wait, 0)
            return carry
        lax.fori_loop(0, N_EXPERTS, per_expert, 0)

        def block_copy(b):
            return pltpu.make_async_copy(zero_ref, xpad_ref.at[pl.ds(pl.multiple_of(b * tb, tb), tb), :], sem)

        def start_block(b, carry):
            block_copy(b).start()
            return carry
        lax.fori_loop(pst_ref[N_EXPERTS], n_blocks, start_block, 0)

        def wait_block(b, carry):
            block_copy(b).wait()
            return carry
        lax.fori_loop(pst_ref[N_EXPERTS], n_blocks, wait_block, 0)


def _dispatch(counts, route_i, x1, tb, n_blocks):
    T, D = x1.shape
    tm = min(DISPATCH_TM, T)
    smem_full = pl.BlockSpec(memory_space=pltpu.SMEM)
    return pl.pallas_call(
        functools.partial(_dispatch_kernel, tb=tb, n_blocks=n_blocks),
        grid=(T // tm,),
        in_specs=[
            smem_full,
            pl.BlockSpec((SUBLANES, tm), lambda i: (0, i), memory_space=pltpu.SMEM),
            pl.BlockSpec((tm, D), lambda i: (i, 0)),
        ],
        out_specs=[
            pl.BlockSpec(memory_space=pl.ANY),
            smem_full, smem_full, smem_full,
        ],
        out_shape=[
            jax.ShapeDtypeStruct((n_blocks * tb, D), F32),
            jax.ShapeDtypeStruct((N_EXPERTS,), I32),
            jax.ShapeDtypeStruct((n_blocks,), I32),
            jax.ShapeDtypeStruct((1,), I32),
        ],
        scratch_shapes=[
            pltpu.SMEM((N_EXPERTS + 1,), I32),
            pltpu.VMEM((tb, D), F32),
            pltpu.SemaphoreType.DMA(()),
        ],
        compiler_params=_cparams(("arbitrary",)),
        name="moe_dispatch",
    )(counts, route_i, x1)


def _expert_kernel(bexp_ref, nused_ref, x_ref, wg_ref, wu_ref, wd_ref, y_ref):
    used = pl.program_id(0) < nused_ref[0]

    @pl.when(used)
    def _():
        x = x_ref[...].astype(BF16)
        gate = jnp.dot(x, wg_ref[...], preferred_element_type=F32)
        up = jnp.dot(x, wu_ref[...], preferred_element_type=F32)
        hid = (gate * _sigmoid(gate) * up).astype(BF16)
        y_ref[...] = jnp.dot(hid, wd_ref[...], preferred_element_type=F32)

    @pl.when(jnp.logical_not(used))
    def _():
        y_ref[...] = jnp.zeros_like(y_ref)


def _experts(bexp, nused, x_pad, w_gate, w_up, w_down, tb):
    P, D = x_pad.shape
    n_blocks = P // tb
    F = w_gate.shape[-1]
    blk = lambda b, be, nu: jnp.minimum(b, nu[0] - 1)
    exp = lambda b, be, nu: be[jnp.minimum(b, nu[0] - 1)]
    return pl.pallas_call(
        _expert_kernel,
        grid_spec=pltpu.PrefetchScalarGridSpec(
            num_scalar_prefetch=2,
            grid=(n_blocks,),
            in_specs=[
                pl.BlockSpec((tb, D), lambda b, be, nu: (blk(b, be, nu), 0)),
                pl.BlockSpec((None, D, F), lambda b, be, nu: (exp(b, be, nu), 0, 0)),
                pl.BlockSpec((None, D, F), lambda b, be, nu: (exp(b, be, nu), 0, 0)),
                pl.BlockSpec((None, F, D), lambda b, be, nu: (exp(b, be, nu), 0, 0)),
            ],
            out_specs=pl.BlockSpec((tb, D), lambda b, be, nu: (b, 0)),
        ),
        out_shape=jax.ShapeDtypeStruct((P, D), F32),
        compiler_params=_cparams(("arbitrary",)),
        name="moe_experts",
    )(bexp, nused, x_pad, w_gate, w_up, w_down)


def _combine_kernel(pstart_ref, ri_ref, x1_ref, rw_ref, g_ref, b_ref, ypad_ref, o_ref, ybuf_ref, sem):
    tm = x1_ref.shape[0]

    def issue(t, carry):
        for k in range(TOP_K):
            pos = pstart_ref[ri_ref[k, t]] + ri_ref[TOP_K + k, t]
            _row_copy(ypad_ref, pos, ybuf_ref.at[k], t, sem).start()
        return carry
    lax.fori_loop(0, tm, issue, 0)

    def drain(t, carry):
        _row_copy(ypad_ref, 0, ybuf_ref.at[0], 0, sem).wait()
        return carry
    lax.fori_loop(0, TOP_K * tm, drain, 0)

    rw = rw_ref[...]
    ffn = ybuf_ref[0] * rw[:, 0:1] + ybuf_ref[1] * rw[:, 1:2]
    pre = DEEPNORM_ALPHA * x1_ref[...] + ffn
    mu = jnp.mean(pre, axis=-1, keepdims=True)
    pc = pre - mu
    var = jnp.mean(pc * pc, axis=-1, keepdims=True)
    o_ref[...] = pc * lax.rsqrt(var + LN_EPS) * g_ref[...] + b_ref[...]


def _combine(pstart, route_i, x1, rw_col, ln_g, ln_b, y_pad):
    T, D = x1.shape
    tm = min(COMBINE_TM, T)
    return pl.pallas_call(
        _combine_kernel,
        grid=(T // tm,),
        in_specs=[
            pl.BlockSpec(memory_space=pltpu.SMEM),
            pl.BlockSpec((SUBLANES, tm), lambda i: (0, i), memory_space=pltpu.SMEM),
            pl.BlockSpec((tm, D), lambda i: (i, 0)),
            pl.BlockSpec((tm, SUBLANES), lambda i: (i, 0)),
            pl.BlockSpec((1, D), lambda i: (0, 0)),
            pl.BlockSpec((1, D), lambda i: (0, 0)),
            pl.BlockSpec(memory_space=pl.ANY),
        ],
        out_specs=pl.BlockSpec((tm, D), lambda i: (i, 0)),
        out_shape=jax.ShapeDtypeStruct((T, D), F32),
        scratch_shapes=[
            pltpu.VMEM((TOP_K, tm, D), F32),
            pltpu.SemaphoreType.DMA(()),
        ],
        compiler_params=_cparams(("arbitrary",)),
        name="moe_combine_ln2",
    )(pstart, route_i, x1, rw_col, ln_g, ln_b, y_pad)


def _rearrange_in_proj(w_in, b_in):
    offs = [0]
    for cwidth in (A_WIDTH, A_WIDTH, A_WIDTH, A_WIDTH, A_HEADS, A_HEADS, B_WIDTH, B_WIDTH, B_WIDTH, D_MODEL, D_MODEL):
        offs.append(offs[-1] + cwidth)
    main_cols = [(9, 11), (0, 4), (6, 9)]
    w_main = jnp.concatenate([w_in[:, offs[a]:offs[b]] for a, b in main_cols], axis=1).astype(BF16)
    b_main = jnp.concatenate([b_in[offs[a]:offs[b]] for a, b in main_cols])[None, :]
    ngate = 2 * A_HEADS
    w_gate = jnp.pad(w_in[:, offs[4]:offs[6]], ((0, 0), (0, LANES - ngate)))
    b_gate = jnp.pad(b_in[offs[4]:offs[6]], (0, LANES - ngate))[None, :]
    return w_main, b_main, w_gate, b_gate


def _layer(x, w_in, b_in, conv_w, conv_b, norm_a_g, lq1, lk1, lq2, lk2, norm_b_g, w_a, w_b, w_out,
           ln1_g, ln1_b, w_grp, b_grp, w_exp, b_exp, w_gate, w_up, w_down, ln2_g, ln2_b, lambda_init):
    B, S, D = x.shape
    T = B * S
    x2d = x.reshape(T, D)

    w_main, b_main, w_g, b_g = _rearrange_in_proj(w_in, b_in)
    z_main, z_gate = _in_projection(x2d, w_main, b_main, w_g, b_g)

    L = min(MLSTM_CHUNK, S)
    gp = _gate_prep(z_gate[:, :2 * A_HEADS].T, L)
    grow = jnp.stack([gp[:A_HEADS], gp[A_HEADS:]], axis=1)
    grow = jnp.pad(grow, ((0, 0), (0, SUBLANES - 2), (0, 0)))
    gcol = jnp.swapaxes(grow, 1, 2)

    h_a = _mlstm(z_main, gcol, grow, conv_w, conv_b[None, :], norm_a_g[None, :], B, S)
    o_b = _diff_attention(z_main, lq1[None, :], lk1[None, :], lq2[None, :], lk2[None, :],
                          norm_b_g[None, :], B, S, lambda_init)

    w_r = jnp.zeros((D, LANES), F32).at[:, :N_GROUPS].set(w_grp).at[:, SUBLANES:SUBLANES + N_EXPERTS].set(w_exp)
    b_r = jnp.zeros((LANES,), F32).at[:N_GROUPS].set(b_grp).at[SUBLANES:SUBLANES + N_EXPERTS].set(b_exp)[None, :]
    x1, route_i, route_w, counts = _mix(h_a, o_b, z_main, x2d, w_a.astype(BF16), w_b.astype(BF16),
                                        w_out.astype(BF16), ln1_g[None, :], ln1_b[None, :], w_r, b_r)

    tb = MOE_TB
    n_blocks = (T * TOP_K) // tb + N_EXPERTS
    x_pad, pstart, bexp, nused = _dispatch(counts[:, 0], route_i, x1, tb, n_blocks)
    y_pad = _experts(bexp, nused, x_pad, w_gate.astype(BF16), w_up.astype(BF16), w_down.astype(BF16), tb)
    out = _combine(pstart, route_i, x1, route_w.T, ln2_g[None, :], ln2_b[None, :], y_pad)
    return out.reshape(B, S, D)


def kernel(x, w_in, b_in, conv_w, conv_b, mlstm_norm_g, lambda_q1, lambda_k1, lambda_q2, lambda_k2,
           diff_norm_g, w_a, w_b, w_out, ln1_g, ln1_b, w_grp, b_grp, w_exp, b_exp,
           w_gate, w_up, w_down, ln2_g, ln2_b):
    for l in range(DEPTH):
        lambda_init = 0.8 - 0.6 * math.exp(-0.3 * l)
        x = _layer(x, w_in[l], b_in[l], conv_w[l], conv_b[l], mlstm_norm_g[l], lambda_q1[l], lambda_k1[l],
                   lambda_q2[l], lambda_k2[l], diff_norm_g[l], w_a[l], w_b[l], w_out[l], ln1_g[l], ln1_b[l],
                   w_grp[l], b_grp[l], w_exp[l], b_exp[l], w_gate[l], w_up[l], w_down[l], ln2_g[l], ln2_b[l],
                   lambda_init)
    return x
```

```python
import functools
import math

import jax
import jax.numpy as jnp
from jax import lax
from jax.experimental import pallas as pl
from jax.experimental.pallas import tpu as pltpu

F32 = jnp.float32
BF16 = jnp.bfloat16
I32 = jnp.int32

D_MODEL = 2048
A_HEADS = 4
A_HEAD_DIM = 256
A_WIDTH = A_HEADS * A_HEAD_DIM
CONV_WIDTH = 4
B_HEADS = 8
B_HEAD_DIM = 64
B_V_DIM = 2 * B_HEAD_DIM
B_WIDTH = B_HEADS * B_V_DIM
ATTN_CHUNK = 64
N_GROUPS = 4
EXPERTS_PER_GROUP = 8
N_EXPERTS = N_GROUPS * EXPERTS_PER_GROUP
TOP_K = 2
D_EXPERT = 512
DEPTH = 1
DEEPNORM_ALPHA = (2 * DEPTH) ** 0.25
LN_EPS = 1e-5
NEG_BIG = -1e30
LOG2_E = 1.4426950408889634

LANES = 128
SUBLANES = 8
VMEM_LIMIT_BYTES = 56 * 1024 * 1024

COL_GA = 0
COL_GB = COL_GA + D_MODEL
COL_QA = COL_GB + D_MODEL
COL_KA = COL_QA + A_WIDTH
COL_VA = COL_KA + A_WIDTH
COL_OA = COL_VA + A_WIDTH
COL_QB = COL_OA + A_WIDTH
COL_KB = COL_QB + B_WIDTH
COL_VB = COL_KB + B_WIDTH
N_MAIN = COL_VB + B_WIDTH

PROJ_TM, PROJ_TN = 1024, 512
MLSTM_CHUNK = 256
ATTN_TQ = 512
MIX_TM = 256
MOE_TB = 256
DISPATCH_TM = 256
COMBINE_TM = 256


def _cparams(sem, vmem=VMEM_LIMIT_BYTES):
    return pltpu.CompilerParams(dimension_semantics=sem, vmem_limit_bytes=vmem)


def _sigmoid(x):
    return 1.0 / (1.0 + jnp.exp(-x))


def _proj_kernel(x_ref, w_ref, b_ref, wg_ref, bg_ref, z_ref, zg_ref, xb_ref):
    @pl.when(pl.program_id(1) == 0)
    def _():
        x = x_ref[...]
        xb_ref[...] = x.astype(BF16)
        zg_ref[...] = jnp.dot(x, wg_ref[...], preferred_element_type=F32,
                              precision=lax.Precision.HIGHEST) + bg_ref[...]

    acc = jnp.dot(xb_ref[...], w_ref[...], preferred_element_type=F32)
    z_ref[...] = (acc + b_ref[...]).astype(z_ref.dtype)


def _in_projection(x2d, w_main, b_main, w_gate, b_gate):
    T, K = x2d.shape
    N = w_main.shape[1]
    tm, tn = min(PROJ_TM, T), PROJ_TN
    return pl.pallas_call(
        _proj_kernel,
        grid=(T // tm, N // tn),
        in_specs=[
            pl.BlockSpec((tm, K), lambda i, j: (i, 0)),
            pl.BlockSpec((K, tn), lambda i, j: (0, j)),
            pl.BlockSpec((1, tn), lambda i, j: (0, j)),
            pl.BlockSpec((K, LANES), lambda i, j: (0, 0)),
            pl.BlockSpec((1, LANES), lambda i, j: (0, 0)),
        ],
        out_specs=[
            pl.BlockSpec((tm, tn), lambda i, j: (i, j)),
            pl.BlockSpec((tm, LANES), lambda i, j: (i, 0)),
        ],
        out_shape=[
            jax.ShapeDtypeStruct((T, N), BF16),
            jax.ShapeDtypeStruct((T, LANES), F32),
        ],
        scratch_shapes=[pltpu.VMEM((tm, K), BF16)],
        compiler_params=_cparams(("parallel", "arbitrary")),
        name="in_projection",
    )(x2d, w_main, b_main, w_gate, b_gate)


def _gate_prep_kernel(g_ref, o_ref):
    g = g_ref[...]
    L = g.shape[1]
    lf = jnp.minimum(g, 0.0) - jnp.log(1.0 + jnp.exp(-jnp.abs(g)))
    r = lax.broadcasted_iota(I32, (L, L), 0)
    c = lax.broadcasted_iota(I32, (L, L), 1)
    tri = (r <= c).astype(F32)
    b = jnp.dot(lf, tri, preferred_element_type=F32, precision=lax.Precision.HIGHEST)
    row = lax.broadcasted_iota(I32, g.shape, 0)
    o_ref[...] = jnp.where(row < A_HEADS, g, b)


def _gate_prep(g_rows, L):
    R, T = g_rows.shape
    return pl.pallas_call(
        _gate_prep_kernel,
        grid=(T // L,),
        in_specs=[pl.BlockSpec((R, L), lambda i: (0, i))],
        out_specs=pl.BlockSpec((R, L), lambda i: (0, i)),
        out_shape=jax.ShapeDtypeStruct((R, T), F32),
        compiler_params=_cparams(("parallel",)),
        name="gate_prep",
    )(g_rows)


def _mlstm_kernel(q_ref, k_ref, v_ref, o_ref, gcol_ref, grow_ref, cwq_ref, cwk_ref, cbq_ref, cbk_ref,
                  ng_ref, out_ref, c_ref, n_ref, m_ref, qcar_ref, kcar_ref):
    ci = pl.program_id(2)
    L = q_ref.shape[0]
    dh = q_ref.shape[1]

    @pl.when(ci == 0)
    def _():
        c_ref[...] = jnp.zeros_like(c_ref)
        n_ref[...] = jnp.zeros_like(n_ref)
        m_ref[...] = jnp.zeros_like(m_ref)
        qcar_ref[...] = jnp.zeros_like(qcar_ref)
        kcar_ref[...] = jnp.zeros_like(kcar_ref)

    def conv_silu(u_ref, car_ref, w_ref, b_ref):
        u = u_ref[...].astype(F32)
        ext = jnp.concatenate([car_ref[...], u], axis=0)
        w = w_ref[...]
        y = b_ref[...] + w[3:4, :] * u
        for j in range(CONV_WIDTH - 1):
            off = SUBLANES - (CONV_WIDTH - 1) + j
            y = y + w[j:j + 1, :] * ext[off:off + L, :]
        car_ref[...] = u[L - SUBLANES:, :]
        return y * _sigmoid(y)

    q = conv_silu(q_ref, qcar_ref, cwq_ref, cbq_ref)
    k = conv_silu(k_ref, kcar_ref, cwk_ref, cbk_ref) * (dh ** -0.5)
    v_b = v_ref[...]
    q_b = q.astype(BF16)
    k_b = k.astype(BF16)

    gcol = gcol_ref[...]
    grow = grow_ref[...]
    i_col, b_col = gcol[:, 0:1], gcol[:, 1:2]
    i_row, b_row = grow[0:1, :], grow[1:2, :]
    b_last = b_row[:, L - 1:L]
    m_prev = m_ref[...]

    r = lax.broadcasted_iota(I32, (L, L), 0)
    c = lax.broadcasted_iota(I32, (L, L), 1)
    dmat = jnp.where(r >= c, b_col - b_row + i_row, NEG_BIG)
    inter = b_col + m_prev
    m_t = jnp.maximum(inter, jnp.max(dmat, axis=-1, keepdims=True))
    s = lax.dot_general(q_b, k_b, (((1,), (1,)), ((), ())), preferred_element_type=F32)
    w_intra = jnp.exp(dmat - m_t) * s
    w_inter = jnp.exp(inter - m_t)
    qc = jnp.dot(q_b, c_ref[...].astype(BF16), preferred_element_type=F32)
    num = w_inter * qc + jnp.dot(w_intra.astype(BF16), v_b, preferred_element_type=F32)
    qn = jnp.sum(q * n_ref[...], axis=-1, keepdims=True)
    den = w_inter * qn + jnp.sum(w_intra, axis=-1, keepdims=True)
    hh = num / jnp.maximum(jnp.abs(den), jnp.exp(-m_t))

    g_col = b_last - b_col + i_col
    m_new = jnp.maximum(b_last + m_prev, jnp.max(g_col, axis=0, keepdims=True))
    decay = jnp.exp(b_last + m_prev - m_new)
    kw = jnp.exp(g_col - m_new) * k
    c_ref[...] = decay * c_ref[...] + jnp.dot(kw.T.astype(BF16), v_b, preferred_element_type=F32)
    n_ref[...] = decay * n_ref[...] + jnp.sum(kw, axis=0, keepdims=True)
    m_ref[...] = m_new

    y = _sigmoid(o_ref[...].astype(F32)) * hh
    mu = jnp.mean(y, axis=-1, keepdims=True)
    yc = y - mu
    var = jnp.mean(yc * yc, axis=-1, keepdims=True)
    out_ref[...] = (yc * lax.rsqrt(var + LN_EPS) * ng_ref[...]).astype(out_ref.dtype)


def _mlstm(z_main, gcol, grow, conv_w, conv_b, norm_g, B, S):
    L = min(MLSTM_CHUNK, S)
    nc = S // L
    dh = A_HEAD_DIM
    H = A_HEADS
    row = lambda b, h, c: b * nc + c
    return pl.pallas_call(
        _mlstm_kernel,
        grid=(B, H, nc),
        in_specs=[
            pl.BlockSpec((L, dh), lambda b, h, c: (row(b, h, c), COL_QA // dh + h)),
            pl.BlockSpec((L, dh), lambda b, h, c: (row(b, h, c), COL_KA // dh + h)),
            pl.BlockSpec((L, dh), lambda b, h, c: (row(b, h, c), COL_VA // dh + h)),
            pl.BlockSpec((L, dh), lambda b, h, c: (row(b, h, c), COL_OA // dh + h)),
            pl.BlockSpec((None, L, SUBLANES), lambda b, h, c: (h, row(b, h, c), 0)),
            pl.BlockSpec((None, SUBLANES, L), lambda b, h, c: (h, 0, row(b, h, c))),
            pl.BlockSpec((CONV_WIDTH, dh), lambda b, h, c: (0, h)),
            pl.BlockSpec((CONV_WIDTH, dh), lambda b, h, c: (0, H + h)),
            pl.BlockSpec((1, dh), lambda b, h, c: (0, h)),
            pl.BlockSpec((1, dh), lambda b, h, c: (0, H + h)),
            pl.BlockSpec((1, dh), lambda b, h, c: (0, h)),
        ],
        out_specs=pl.BlockSpec((L, dh), lambda b, h, c: (row(b, h, c), h)),
        out_shape=jax.ShapeDtypeStruct((B * S, A_WIDTH), BF16),
        scratch_shapes=[
            pltpu.VMEM((dh, dh), F32),
            pltpu.VMEM((1, dh), F32),
            pltpu.VMEM((1, 1), F32),
            pltpu.VMEM((SUBLANES, dh), F32),
            pltpu.VMEM((SUBLANES, dh), F32),
        ],
        compiler_params=_cparams(("parallel", "parallel", "arbitrary")),
        name="mlstm",
    )(z_main, z_main, z_main, z_main, gcol, grow, conv_w, conv_w, conv_b, conv_b, norm_g)


def _lane_tiles(t):
    return [t[:, i * LANES:(i + 1) * LANES] for i in range(t.shape[1] // LANES)]


def _attn_kernel(slope_ref, q_ref, k_ref, kb_ref, v_ref, lq1_ref, lk1_ref, lq2_ref, lk2_ref, g_ref, o_ref,
                 s_ref, mx_ref, ls_ref, acc_ref, *, lambda_init):
    h = pl.program_id(1)
    qi = pl.program_id(2)
    tq = q_ref.shape[0]
    tk = tq
    slope = slope_ref[h]
    nt = (((1,), (1,)), ((), ()))

    lane = lax.broadcasted_iota(I32, (tq, B_V_DIM), 1)
    q = q_ref[...] * (B_HEAD_DIM ** -0.5)
    zero = jnp.zeros_like(q)
    bias_cols = jnp.where(lane < 2, 1.0, 0.0).astype(BF16)
    qa = (jnp.concatenate([jnp.where(lane < B_HEAD_DIM, q, zero), bias_cols], axis=1),
          jnp.concatenate([jnp.where(lane >= B_HEAD_DIM, q, zero), bias_cols], axis=1))

    def scores(j):
        start = pl.multiple_of(j * tk, tk)
        ka = jnp.concatenate([k_ref[pl.ds(start, tk), :], kb_ref[pl.ds(start, tk), :]], axis=1)
        return [lax.dot_general(qa[n], ka, nt, preferred_element_type=F32) for n in range(2)]

    def keep(j, n, t):
        t = t * LOG2_E
        s_ref[n, j] = t
        mx_ref[n] = functools.reduce(jnp.maximum, _lane_tiles(t), mx_ref[n])

    mx_ref[...] = jnp.full(mx_ref.shape, NEG_BIG, F32)

    def pass_a(j, carry):
        t = scores(j)
        for n in range(2):
            keep(j, n, t[n])
        return carry
    lax.fori_loop(0, qi, pass_a, 0)

    r = lax.broadcasted_iota(I32, (tq, tk), 0)
    c = lax.broadcasted_iota(I32, (tq, tk), 1)
    ahead = jnp.where(c <= r, 0.0, (r - c).astype(F32) * (2.0 * slope))
    corr = jnp.where((c // ATTN_CHUNK) <= (r // ATTN_CHUNK), ahead, NEG_BIG)
    t = scores(qi)
    for n in range(2):
        keep(qi, n, t[n] + corr)

    for n in range(2):
        m = jnp.max(mx_ref[n], axis=-1, keepdims=True)
        mx_ref[n] = jnp.broadcast_to(m, (tq, LANES))
    ls_ref[...] = jnp.zeros_like(ls_ref)
    acc_ref[...] = jnp.zeros_like(acc_ref)

    def pass_b(j, carry):
        start = pl.multiple_of(j * tk, tk)
        vj = v_ref[pl.ds(start, tk), :]
        for n in range(2):
            mb = mx_ref[n]
            ps = [jnp.exp2(t - mb) for t in _lane_tiles(s_ref[n, j])]
            ls_ref[n] += functools.reduce(jnp.add, ps)
            p = jnp.concatenate(ps, axis=1).astype(BF16)
            acc_ref[n] += jnp.dot(p, vj, preferred_element_type=F32)
        return carry
    lax.fori_loop(0, qi + 1, pass_b, 0)

    lam = (jnp.exp(jnp.sum(lq1_ref[...] * lk1_ref[...], axis=-1, keepdims=True))
           - jnp.exp(jnp.sum(lq2_ref[...] * lk2_ref[...], axis=-1, keepdims=True)) + lambda_init)
    l0 = jnp.sum(ls_ref[0], axis=-1, keepdims=True)
    l1 = jnp.sum(ls_ref[1], axis=-1, keepdims=True)
    o = acc_ref[0] / l0 - lam * (acc_ref[1] / l1)
    ms = jnp.mean(o * o, axis=-1, keepdims=True)
    o_ref[...] = (o * lax.rsqrt(ms + LN_EPS) * g_ref[...] * (1.0 - lambda_init)).astype(o_ref.dtype)


def _diff_attention(z_main, lq1, lk1, lq2, lk2, norm_g, B, S, lambda_init):
    tq = min(ATTN_TQ, S)
    nq = S // tq
    H = B_HEADS
    dv = B_V_DIM
    slopes = 2.0 ** (-8.0 * jnp.arange(1, H + 1, dtype=F32) / H)
    assert S <= 256 * 256
    pos = lax.broadcasted_iota(I32, (H, S, dv), 1)
    col = lax.broadcasted_iota(I32, (H, S, dv), 2)
    within = pos % 256
    kbias = jnp.where(col == 0, within, jnp.where(col == 1, pos - within, 0)).astype(F32)
    kbias = (kbias * slopes[:, None, None]).astype(BF16)
    small = pl.BlockSpec((1, B_HEAD_DIM), lambda b, h, i: (0, 0))
    return pl.pallas_call(
        functools.partial(_attn_kernel, lambda_init=lambda_init),
        grid=(B, H, nq),
        in_specs=[
            pl.BlockSpec(memory_space=pltpu.SMEM),
            pl.BlockSpec((tq, dv), lambda b, h, i: (b * nq + i, COL_QB // dv + h)),
            pl.BlockSpec((S, dv), lambda b, h, i: (b, COL_KB // dv + h)),
            pl.BlockSpec((None, S, dv), lambda b, h, i: (h, 0, 0)),
            pl.BlockSpec((S, dv), lambda b, h, i: (b, COL_VB // dv + h)),
            small, small, small, small,
            pl.BlockSpec((1, dv), lambda b, h, i: (0, 0)),
        ],
        out_specs=pl.BlockSpec((tq, dv), lambda b, h, i: (b * nq + i, h)),
        out_shape=jax.ShapeDtypeStruct((B * S, B_WIDTH), BF16),
        scratch_shapes=[
            pltpu.VMEM((2, nq, tq, tq), F32),
            pltpu.VMEM((2, tq, LANES), F32),
            pltpu.VMEM((2, tq, LANES), F32),
            pltpu.VMEM((2, tq, dv), F32),
        ],
        compiler_params=_cparams(("parallel", "parallel", "arbitrary")),
        name="diff_attention",
    )(slopes, z_main, z_main, kbias, z_main, lq1, lk1, lq2, lk2, norm_g)


def _mix_kernel(ha_ref, ob_ref, ga_ref, gb_ref, x_ref, wa_ref, wb_ref, wo_ref, g1_ref, b1_ref, wr_ref, br_ref,
                x1_ref, ri_ref, rw_ref, cnt_ref, carry_ref):
    i = pl.program_id(0)
    tm = x_ref.shape[0]

    @pl.when(i == 0)
    def _():
        carry_ref[...] = jnp.zeros_like(carry_ref)

    ya = jnp.dot(ha_ref[...], wa_ref[...], preferred_element_type=F32)
    yb = jnp.dot(ob_ref[...], wb_ref[...], preferred_element_type=F32)
    merged = _sigmoid(ga_ref[...].astype(F32)) * ya + _sigmoid(gb_ref[...].astype(F32)) * yb
    mix = jnp.dot(merged.astype(BF16), wo_ref[...], preferred_element_type=F32)
    pre = DEEPNORM_ALPHA * x_ref[...] + mix
    mu = jnp.mean(pre, axis=-1, keepdims=True)
    pc = pre - mu
    var = jnp.mean(pc * pc, axis=-1, keepdims=True)
    x1 = pc * lax.rsqrt(var + LN_EPS) * g1_ref[...] + b1_ref[...]
    x1_ref[...] = x1

    logits = jnp.dot(x1, wr_ref[...], preferred_element_type=F32, precision=lax.Precision.HIGHEST) + br_ref[...]
    lt = logits.T
    row8 = lax.broadcasted_iota(I32, (SUBLANES, tm), 0).astype(F32)
    gl = jnp.where(row8 < N_GROUPS, lt[0:SUBLANES, :], NEG_BIG)
    gmax = jnp.max(gl, axis=0, keepdims=True)
    gsel = jnp.min(jnp.where(gl == gmax, row8, float(SUBLANES)), axis=0, keepdims=True)
    gprob = 1.0 / jnp.sum(jnp.exp(gl - gmax), axis=0, keepdims=True)
    ing = lt[SUBLANES:2 * SUBLANES, :]
    for g in range(1, N_GROUPS):
        ing = jnp.where(gsel == g, lt[(g + 1) * SUBLANES:(g + 2) * SUBLANES, :], ing)
    v0 = jnp.max(ing, axis=0, keepdims=True)
    i0 = jnp.min(jnp.where(ing == v0, row8, float(SUBLANES)), axis=0, keepdims=True)
    ing2 = jnp.where(row8 == i0, -jnp.inf, ing)
    v1 = jnp.max(ing2, axis=0, keepdims=True)
    i1 = jnp.min(jnp.where(ing2 == v1, row8, float(SUBLANES)), axis=0, keepdims=True)
    ex = jnp.exp(v1 - v0)
    inv = 1.0 / (1.0 + ex)
    w0 = gprob * inv
    w1 = gprob * (ex * inv)
    e0 = (gsel * EXPERTS_PER_GROUP + i0).astype(I32)
    e1 = (gsel * EXPERTS_PER_GROUP + i1).astype(I32)

    rowe = lax.broadcasted_iota(I32, (N_EXPERTS, tm), 0)
    is0 = rowe == e0
    is1 = rowe == e1
    oh = jnp.where(is0, 1.0, 0.0) + jnp.where(is1, 1.0, 0.0)
    tr = lax.broadcasted_iota(I32, (tm, tm), 0)
    tc = lax.broadcasted_iota(I32, (tm, tm), 1)
    su = jnp.where(tr < tc, 1.0, 0.0).astype(BF16)
    tot = jnp.dot(oh.astype(BF16), su, preferred_element_type=F32) + carry_ref[...]
    rank0 = jnp.sum(jnp.where(is0, tot, 0.0), axis=0, keepdims=True)
    rank1 = jnp.sum(jnp.where(is1, tot, 0.0), axis=0, keepdims=True)
    carry = carry_ref[...] + jnp.sum(oh, axis=1, keepdims=True)
    carry_ref[...] = carry
    cnt_ref[...] = jnp.broadcast_to(carry, cnt_ref.shape).astype(I32)

    zi = jnp.zeros((1, tm), I32)
    ri_ref[...] = jnp.concatenate([e0, e1, rank0.astype(I32), rank1.astype(I32), zi, zi, zi, zi], axis=0)
    zf = jnp.zeros((1, tm), F32)
    rw_ref[...] = jnp.concatenate([w0, w1, zf, zf, zf, zf, zf, zf], axis=0)


def _mix(h_a, o_b, z_main, x2d, w_a, w_b, w_out, ln_g, ln_b, w_r, b_r):
    T, D = x2d.shape
    tm = min(MIX_TM, T)
    const = lambda shape: pl.BlockSpec(shape, lambda i: (0, 0), pipeline_mode=pl.Buffered(1))
    return pl.pallas_call(
        _mix_kernel,
        grid=(T // tm,),
        in_specs=[
            pl.BlockSpec((tm, A_WIDTH), lambda i: (i, 0)),
            pl.BlockSpec((tm, B_WIDTH), lambda i: (i, 0)),
            pl.BlockSpec((tm, D), lambda i: (i, COL_GA // D)),
            pl.BlockSpec((tm, D), lambda i: (i, COL_GB // D)),
            pl.BlockSpec((tm, D), lambda i: (i, 0)),
            const((A_WIDTH, D)), const((B_WIDTH, D)), const((D, D)),
            const((1, D)), const((1, D)), const((D, LANES)), const((1, LANES)),
        ],
        out_specs=[
            pl.BlockSpec((tm, D), lambda i: (i, 0)),
            pl.BlockSpec((SUBLANES, tm), lambda i: (0, i)),
            pl.BlockSpec((SUBLANES, tm), lambda i: (0, i)),
            pl.BlockSpec((N_EXPERTS, LANES), lambda i: (0, 0)),
        ],
        out_shape=[
            jax.ShapeDtypeStruct((T, D), F32),
            jax.ShapeDtypeStruct((SUBLANES, T), I32),
            jax.ShapeDtypeStruct((SUBLANES, T), F32),
            jax.ShapeDtypeStruct((N_EXPERTS, LANES), I32),
        ],
        scratch_shapes=[pltpu.VMEM((N_EXPERTS, 1), F32)],
        compiler_params=_cparams(("arbitrary",)),
        name="mix_ln1_router",
    )(h_a, o_b, z_main, z_main, x2d, w_a, w_b, w_out, ln_g, ln_b, w_r, b_r)


def _row_copy(src_ref, src_row, dst_ref, dst_row, sem):
    return pltpu.make_async_copy(src_ref.at[pl.ds(src_row, 1), :], dst_ref.at[pl.ds(dst_row, 1), :], sem)


def _rows_wait(src_ref, dst_ref, rows, sem):
    pltpu.make_async_copy(src_ref.at[pl.ds(0, rows), :], dst_ref.at[pl.ds(0, rows), :], sem).wait()


def _dispatch_kernel(cnt_ref, ri_ref, x1_ref, xpad_ref, pstart_ref, bexp_ref, nused_ref,
                     pst_ref, zero_ref, sem, zsem, *, tb, n_blocks, tm):
    i = pl.program_id(0)
    last = pl.num_programs(0) - 1

    @pl.when(i == 0)
    def _():
        def fill(b, carry):
            bexp_ref[b] = 0
            return carry
        lax.fori_loop(0, n_blocks, fill, 0)

        def per_expert(e, blk):
            pst_ref[e] = blk * tb
            pstart_ref[e] = blk * tb
            nb = (cnt_ref[e] + (tb - 1)) // tb

            def mark(b, carry):
                bexp_ref[blk + b] = e
                return carry
            lax.fori_loop(0, nb, mark, 0)
            return blk + nb
        nused = lax.fori_loop(0, N_EXPERTS, per_expert, 0)
        nused_ref[0] = nused
        pst_ref[N_EXPERTS] = nused

    base = i * tm

    def issue(t, carry):
        for k in range(TOP_K):
            pos = pst_ref[ri_ref[k, t]] + ri_ref[TOP_K + k, t]
            _row_copy(x1_ref, base + t, xpad_ref, pos, sem).start()
        return carry
    lax.fori_loop(0, tm, issue, 0, unroll=8)

    @pl.when(i > 0)
    def _():
        _rows_wait(x1_ref, xpad_ref, TOP_K * tm, sem)

    @pl.when(i == last)
    def _():
        _rows_wait(x1_ref, xpad_ref, TOP_K * tm, sem)
        zero_ref[...] = jnp.zeros_like(zero_ref)

        def per_expert(e, carry):
            cnt = cnt_ref[e]
            first = pst_ref[e] + cnt
            npad = ((cnt + (tb - 1)) // tb) * tb - cnt

            def start(p, c2):
                _row_copy(zero_ref, 0, xpad_ref, first + p, zsem).start()
                return c2
            lax.fori_loop(0, npad, start, 0)

            def wait(p, c2):
                _row_copy(zero_ref, 0, xpad_ref, 0, zsem).wait()
                return c2
            lax.fori_loop(0, npad, wait, 0)
            return carry
        lax.fori_loop(0, N_EXPERTS, per_expert, 0)

        def block_copy(b):
            return pltpu.make_async_copy(zero_ref, xpad_ref.at[pl.ds(pl.multiple_of(b * tb, tb), tb), :], zsem)

        def start_block(b, carry):
            block_copy(b).start()
            return carry
        lax.fori_loop(pst_ref[N_EXPERTS], n_blocks, start_block, 0)

        def wait_block(b, carry):
            block_copy(b).wait()
            return carry
        lax.fori_loop(pst_ref[N_EXPERTS], n_blocks, wait_block, 0)


def _dispatch(counts, route_i, x1, tb, n_blocks):
    T, D = x1.shape
    tm = min(DISPATCH_TM, T)
    smem_full = pl.BlockSpec(memory_space=pltpu.SMEM)
    return pl.pallas_call(
        functools.partial(_dispatch_kernel, tb=tb, n_blocks=n_blocks, tm=tm),
        grid=(T // tm,),
        in_specs=[
            smem_full,
            pl.BlockSpec((SUBLANES, tm), lambda i: (0, i), memory_space=pltpu.SMEM),
            pl.BlockSpec(memory_space=pl.ANY),
        ],
        out_specs=[
            pl.BlockSpec(memory_space=pl.ANY),
            smem_full, smem_full, smem_full,
        ],
        out_shape=[
            jax.ShapeDtypeStruct((n_blocks * tb, D), F32),
            jax.ShapeDtypeStruct((N_EXPERTS,), I32),
            jax.ShapeDtypeStruct((n_blocks,), I32),
            jax.ShapeDtypeStruct((1,), I32),
        ],
        scratch_shapes=[
            pltpu.SMEM((N_EXPERTS + 1,), I32),
            pltpu.VMEM((tb, D), F32),
            pltpu.SemaphoreType.DMA(()),
            pltpu.SemaphoreType.DMA(()),
        ],
        compiler_params=_cparams(("arbitrary",)),
        name="moe_dispatch",
    )(counts, route_i, x1)


def _expert_kernel(bexp_ref, nused_ref, x_ref, wg_ref, wu_ref, wd_ref, y_ref, wgb_ref, wub_ref, wdb_ref):
    b = pl.program_id(0)
    used = b < nused_ref[0]
    new_expert = jnp.logical_or(b == 0, bexp_ref[b] != bexp_ref[jnp.maximum(b - 1, 0)])

    @pl.when(jnp.logical_and(used, new_expert))
    def _():
        wgb_ref[...] = wg_ref[...].astype(BF16)
        wub_ref[...] = wu_ref[...].astype(BF16)
        wdb_ref[...] = wd_ref[...].astype(BF16)

    @pl.when(used)
    def _():
        x = x_ref[...].astype(BF16)
        gate = jnp.dot(x, wgb_ref[...], preferred_element_type=F32)
        up = jnp.dot(x, wub_ref[...], preferred_element_type=F32)
        hid = (gate * _sigmoid(gate) * up).astype(BF16)
        y_ref[...] = jnp.dot(hid, wdb_ref[...], preferred_element_type=F32)

    @pl.when(jnp.logical_not(used))
    def _():
        y_ref[...] = jnp.zeros_like(y_ref)


def _experts(bexp, nused, x_pad, w_gate, w_up, w_down, tb):
    P, D = x_pad.shape
    n_blocks = P // tb
    F = w_gate.shape[-1]
    blk = lambda b, be, nu: jnp.maximum(jnp.minimum(b, nu[0] - 1), 0)
    exp = lambda b, be, nu: be[blk(b, be, nu)]
    return pl.pallas_call(
        _expert_kernel,
        grid_spec=pltpu.PrefetchScalarGridSpec(
            num_scalar_prefetch=2,
            grid=(n_blocks,),
            in_specs=[
                pl.BlockSpec((tb, D), lambda b, be, nu: (blk(b, be, nu), 0)),
                pl.BlockSpec((None, D, F), lambda b, be, nu: (exp(b, be, nu), 0, 0)),
                pl.BlockSpec((None, D, F), lambda b, be, nu: (exp(b, be, nu), 0, 0)),
                pl.BlockSpec((None, F, D), lambda b, be, nu: (exp(b, be, nu), 0, 0)),
            ],
            out_specs=pl.BlockSpec((tb, D), lambda b, be, nu: (b, 0)),
            scratch_shapes=[pltpu.VMEM((D, F), BF16), pltpu.VMEM((D, F), BF16), pltpu.VMEM((F, D), BF16)],
        ),
        out_shape=jax.ShapeDtypeStruct((P, D), F32),
        compiler_params=_cparams(("arbitrary",)),
        name="moe_experts",
    )(bexp, nused, x_pad, w_gate, w_up, w_down)


def _combine_kernel(pstart_ref, ri_ref, rin_ref, x1_ref, rw_ref, g_ref, b_ref, ypad_ref, o_ref, ybuf_ref, sem):
    i = pl.program_id(0)
    tm = x1_ref.shape[0]
    slot = i % 2

    def gather(route_ref, s):
        def issue(t, carry):
            for k in range(TOP_K):
                pos = pstart_ref[route_ref[k, t]] + route_ref[TOP_K + k, t]
                _row_copy(ypad_ref, pos, ybuf_ref.at[s, k], t, sem.at[s]).start()
            return carry
        lax.fori_loop(0, tm, issue, 0, unroll=8)

    @pl.when(i == 0)
    def _():
        gather(ri_ref, 0)

    @pl.when(i + 1 < pl.num_programs(0))
    def _():
        gather(rin_ref, 1 - slot)

    for k in range(TOP_K):
        _rows_wait(ypad_ref, ybuf_ref.at[slot, k], tm, sem.at[slot])

    rw = rw_ref[...]
    ffn = ybuf_ref[slot, 0] * rw[:, 0:1] + ybuf_ref[slot, 1] * rw[:, 1:2]
    pre = DEEPNORM_ALPHA * x1_ref[...] + ffn
    mu = jnp.mean(pre, axis=-1, keepdims=True)
    pc = pre - mu
    var = jnp.mean(pc * pc, axis=-1, keepdims=True)
    o_ref[...] = pc * lax.rsqrt(var + LN_EPS) * g_ref[...] + b_ref[...]


def _combine(pstart, route_i, x1, rw_col, ln_g, ln_b, y_pad):
    T, D = x1.shape
    tm = min(COMBINE_TM, T)
    n = T // tm
    return pl.pallas_call(
        _combine_kernel,
        grid=(n,),
        in_specs=[
            pl.BlockSpec(memory_space=pltpu.SMEM),
            pl.BlockSpec((SUBLANES, tm), lambda i: (0, i), memory_space=pltpu.SMEM),
            pl.BlockSpec((SUBLANES, tm), lambda i: (0, jnp.minimum(i + 1, n - 1)), memory_space=pltpu.SMEM),
            pl.BlockSpec((tm, D), lambda i: (i, 0)),
            pl.BlockSpec((tm, SUBLANES), lambda i: (i, 0)),
            pl.BlockSpec((1, D), lambda i: (0, 0)),
            pl.BlockSpec((1, D), lambda i: (0, 0)),
            pl.BlockSpec(memory_space=pl.ANY),
        ],
        out_specs=pl.BlockSpec((tm, D), lambda i: (i, 0)),
        out_shape=jax.ShapeDtypeStruct((T, D), F32),
        scratch_shapes=[
            pltpu.VMEM((2, TOP_K, tm, D), F32),
            pltpu.SemaphoreType.DMA((2,)),
        ],
        compiler_params=_cparams(("arbitrary",)),
        name="moe_combine_ln2",
    )(pstart, route_i, route_i, x1, rw_col, ln_g, ln_b, y_pad)


def _rearrange_in_proj(w_in, b_in):
    offs = [0]
    for cwidth in (A_WIDTH, A_WIDTH, A_WIDTH, A_WIDTH, A_HEADS, A_HEADS, B_WIDTH, B_WIDTH, B_WIDTH, D_MODEL, D_MODEL):
        offs.append(offs[-1] + cwidth)
    main_cols = [(9, 11), (0, 4), (6, 9)]
    w_main = jnp.concatenate([w_in[:, offs[a]:offs[b]] for a, b in main_cols], axis=1).astype(BF16)
    b_main = jnp.concatenate([b_in[offs[a]:offs[b]] for a, b in main_cols])[None, :]
    ngate = 2 * A_HEADS
    w_gate = jnp.pad(w_in[:, offs[4]:offs[6]], ((0, 0), (0, LANES - ngate)))
    b_gate = jnp.pad(b_in[offs[4]:offs[6]], (0, LANES - ngate))[None, :]
    return w_main, b_main, w_gate, b_gate


def _layer(x, w_in, b_in, conv_w, conv_b, norm_a_g, lq1, lk1, lq2, lk2, norm_b_g, w_a, w_b, w_out,
           ln1_g, ln1_b, w_grp, b_grp, w_exp, b_exp, w_gate, w_up, w_down, ln2_g, ln2_b, lambda_init):
    B, S, D = x.shape
    T = B * S
    x2d = x.reshape(T, D)

    w_main, b_main, w_g, b_g = _rearrange_in_proj(w_in, b_in)
    z_main, z_gate = _in_projection(x2d, w_main, b_main, w_g, b_g)

    L = min(MLSTM_CHUNK, S)
    gp = _gate_prep(z_gate[:, :2 * A_HEADS].T, L)
    grow = jnp.stack([gp[:A_HEADS], gp[A_HEADS:]], axis=1)
    grow = jnp.pad(grow, ((0, 0), (0, SUBLANES - 2), (0, 0)))
    gcol = jnp.swapaxes(grow, 1, 2)

    h_a = _mlstm(z_main, gcol, grow, conv_w, conv_b[None, :], norm_a_g[None, :], B, S)
    o_b = _diff_attention(z_main, lq1[None, :], lk1[None, :], lq2[None, :], lk2[None, :],
                          norm_b_g[None, :], B, S, lambda_init)

    w_r = jnp.zeros((D, LANES), F32).at[:, :N_GROUPS].set(w_grp).at[:, SUBLANES:SUBLANES + N_EXPERTS].set(w_exp)
    b_r = jnp.zeros((LANES,), F32).at[:N_GROUPS].set(b_grp).at[SUBLANES:SUBLANES + N_EXPERTS].set(b_exp)[None, :]
    x1, route_i, route_w, counts = _mix(h_a, o_b, z_main, x2d, w_a.astype(BF16), w_b.astype(BF16),
                                        w_out.astype(BF16), ln1_g[None, :], ln1_b[None, :], w_r, b_r)

    tb = MOE_TB
    n_blocks = (T * TOP_K) // tb + N_EXPERTS
    x_pad, pstart, bexp, nused = _dispatch(counts[:, 0], route_i, x1, tb, n_blocks)
    y_pad = _experts(bexp, nused, x_pad, w_gate, w_up, w_down, tb)
    out = _combine(pstart, route_i, x1, route_w.T, ln2_g[None, :], ln2_b[None, :], y_pad)
    return out.reshape(B, S, D)


def kernel(x, w_in, b_in, conv_w, conv_b, mlstm_norm_g, lambda_q1, lambda_k1, lambda_q2, lambda_k2,
           diff_norm_g, w_a, w_b, w_out, ln1_g, ln1_b, w_grp, b_grp, w_exp, b_exp,
           w_gate, w_up, w_down, ln2_g, ln2_b):
    for l in range(DEPTH):
        lambda_init = 0.8 - 0.6 * math.exp(-0.3 * l)
        x = _layer(x, w_in[l], b_in[l], conv_w[l], conv_b[l], mlstm_norm_g[l], lambda_q1[l], lambda_k1[l],
                   lambda_q2[l], lambda_k2[l], diff_norm_g[l], w_a[l], w_b[l], w_out[l], ln1_g[l], ln1_b[l],
                   w_grp[l], b_grp[l], w_exp[l], b_exp[l], w_gate[l], w_up[l], w_down[l], ln2_g[l], ln2_b[l],
                   lambda_init)
    return x
```

```python
import functools
import math

import jax
import jax.numpy as jnp
from jax import lax
from jax.experimental import pallas as pl
from jax.experimental.pallas import tpu as pltpu

F32 = jnp.float32
BF16 = jnp.bfloat16
I32 = jnp.int32

D_MODEL = 2048
A_HEADS = 4
A_HEAD_DIM = 256
A_WIDTH = A_HEADS * A_HEAD_DIM
CONV_WIDTH = 4
B_HEADS = 8
B_HEAD_DIM = 64
B_V_DIM = 2 * B_HEAD_DIM
B_WIDTH = B_HEADS * B_V_DIM
ATTN_CHUNK = 64
N_GROUPS = 4
EXPERTS_PER_GROUP = 8
N_EXPERTS = N_GROUPS * EXPERTS_PER_GROUP
TOP_K = 2
D_EXPERT = 512
DEPTH = 1
DEEPNORM_ALPHA = (2 * DEPTH) ** 0.25
LN_EPS = 1e-5
NEG_BIG = -1e30
LOG2_E = 1.4426950408889634

LANES = 128
SUBLANES = 8
VMEM_LIMIT_BYTES = 56 * 1024 * 1024

COL_GA = 0
COL_GB = COL_GA + D_MODEL
COL_QA = COL_GB + D_MODEL
COL_KA = COL_QA + A_WIDTH
COL_VA = COL_KA + A_WIDTH
COL_OA = COL_VA + A_WIDTH
COL_QB = COL_OA + A_WIDTH
COL_KB = COL_QB + B_WIDTH
COL_VB = COL_KB + B_WIDTH
N_MAIN = COL_VB + B_WIDTH

PROJ_TM, PROJ_TN = 1024, 1024
MLSTM_CHUNK = 256
ATTN_TQ = 512
MIX_TM = 256
MOE_TB = 256
DISPATCH_TM = 256
COMBINE_TM = 256


def _cparams(sem, vmem=VMEM_LIMIT_BYTES):
    return pltpu.CompilerParams(dimension_semantics=sem, vmem_limit_bytes=vmem)


def _sigmoid(x):
    return 1.0 / (1.0 + jnp.exp(-x))


def _split_bf16(a):
    hi = a.astype(BF16)
    return hi, (a - hi.astype(F32)).astype(BF16)


def _dot_3pass(a, b):
    a_hi, a_lo = _split_bf16(a)
    b_hi, b_lo = _split_bf16(b)
    dot = functools.partial(jnp.dot, preferred_element_type=F32)
    return dot(a_hi, b_hi) + (dot(a_hi, b_lo) + dot(a_lo, b_hi))


def _proj_kernel(x_ref, w_ref, b_ref, wg_ref, bg_ref, z_ref, zg_ref, xb_ref):
    @pl.when(pl.program_id(1) == 0)
    def _():
        x = x_ref[...]
        xb_ref[...] = x.astype(BF16)
        zg_ref[...] = _dot_3pass(x, wg_ref[...]) + bg_ref[...]

    acc = jnp.dot(xb_ref[...], w_ref[...], preferred_element_type=F32)
    z_ref[...] = (acc + b_ref[...]).astype(z_ref.dtype)


def _in_projection(x2d, w_main, b_main, w_gate, b_gate):
    T, K = x2d.shape
    N = w_main.shape[1]
    tm, tn = min(PROJ_TM, T), PROJ_TN
    return pl.pallas_call(
        _proj_kernel,
        grid=(T // tm, N // tn),
        in_specs=[
            pl.BlockSpec((tm, K), lambda i, j: (i, 0)),
            pl.BlockSpec((K, tn), lambda i, j: (0, j)),
            pl.BlockSpec((1, tn), lambda i, j: (0, j)),
            pl.BlockSpec((K, LANES), lambda i, j: (0, 0)),
            pl.BlockSpec((1, LANES), lambda i, j: (0, 0)),
        ],
        out_specs=[
            pl.BlockSpec((tm, tn), lambda i, j: (i, j)),
            pl.BlockSpec((tm, LANES), lambda i, j: (i, 0)),
        ],
        out_shape=[
            jax.ShapeDtypeStruct((T, N), BF16),
            jax.ShapeDtypeStruct((T, LANES), F32),
        ],
        scratch_shapes=[pltpu.VMEM((tm, K), BF16)],
        compiler_params=_cparams(("parallel", "arbitrary")),
        name="in_projection",
    )(x2d, w_main, b_main, w_gate, b_gate)


def _gate_prep_kernel(g_ref, o_ref):
    g = g_ref[...]
    L = g.shape[1]
    lf = jnp.minimum(g, 0.0) - jnp.log(1.0 + jnp.exp(-jnp.abs(g)))
    r = lax.broadcasted_iota(I32, (L, L), 0)
    c = lax.broadcasted_iota(I32, (L, L), 1)
    tri = (r <= c).astype(F32)
    b = jnp.dot(lf, tri, preferred_element_type=F32, precision=lax.Precision.HIGHEST)
    row = lax.broadcasted_iota(I32, g.shape, 0)
    o_ref[...] = jnp.where(row < A_HEADS, g, b)


def _gate_prep(g_rows, L):
    R, T = g_rows.shape
    return pl.pallas_call(
        _gate_prep_kernel,
        grid=(T // L,),
        in_specs=[pl.BlockSpec((R, L), lambda i: (0, i))],
        out_specs=pl.BlockSpec((R, L), lambda i: (0, i)),
        out_shape=jax.ShapeDtypeStruct((R, T), F32),
        compiler_params=_cparams(("parallel",)),
        name="gate_prep",
    )(g_rows)


def _mlstm_kernel(q_ref, k_ref, v_ref, o_ref, gcol_ref, grow_ref, cwq_ref, cwk_ref, cbq_ref, cbk_ref,
                  ng_ref, out_ref, c_ref, n_ref, m_ref, qcar_ref, kcar_ref):
    ci = pl.program_id(2)
    L = q_ref.shape[0]
    dh = q_ref.shape[1]

    @pl.when(ci == 0)
    def _():
        c_ref[...] = jnp.zeros_like(c_ref)
        n_ref[...] = jnp.zeros_like(n_ref)
        m_ref[...] = jnp.zeros_like(m_ref)
        qcar_ref[...] = jnp.zeros_like(qcar_ref)
        kcar_ref[...] = jnp.zeros_like(kcar_ref)

    def conv_silu(u_ref, car_ref, w_ref, b_ref):
        u = u_ref[...].astype(F32)
        ext = jnp.concatenate([car_ref[...], u], axis=0)
        w = w_ref[...]
        y = b_ref[...] + w[3:4, :] * u
        for j in range(CONV_WIDTH - 1):
            off = SUBLANES - (CONV_WIDTH - 1) + j
            y = y + w[j:j + 1, :] * ext[off:off + L, :]
        car_ref[...] = u[L - SUBLANES:, :]
        return y * _sigmoid(y)

    q = conv_silu(q_ref, qcar_ref, cwq_ref, cbq_ref)
    k = conv_silu(k_ref, kcar_ref, cwk_ref, cbk_ref) * (dh ** -0.5)
    v_b = v_ref[...]
    q_b = q.astype(BF16)
    k_b = k.astype(BF16)

    gcol = gcol_ref[...]
    grow = grow_ref[...]
    i_col, b_col = gcol[:, 0:1], gcol[:, 1:2]
    i_row, b_row = grow[0:1, :], grow[1:2, :]
    b_last = b_row[:, L - 1:L]
    m_prev = m_ref[...]

    r = lax.broadcasted_iota(I32, (L, L), 0)
    c = lax.broadcasted_iota(I32, (L, L), 1)
    dmat = jnp.where(r >= c, b_col - b_row + i_row, NEG_BIG)
    inter = b_col + m_prev
    m_t = jnp.maximum(inter, jnp.max(dmat, axis=-1, keepdims=True))
    s = lax.dot_general(q_b, k_b, (((1,), (1,)), ((), ())), preferred_element_type=F32)
    w_intra = jnp.exp(dmat - m_t) * s
    w_inter = jnp.exp(inter - m_t)
    qc = jnp.dot(q_b, c_ref[...].astype(BF16), preferred_element_type=F32)
    num = w_inter * qc + jnp.dot(w_intra.astype(BF16), v_b, preferred_element_type=F32)
    qn = jnp.sum(q * n_ref[...], axis=-1, keepdims=True)
    den = w_inter * qn + jnp.sum(w_intra, axis=-1, keepdims=True)
    hh = num / jnp.maximum(jnp.abs(den), jnp.exp(-m_t))

    g_col = b_last - b_col + i_col
    m_new = jnp.maximum(b_last + m_prev, jnp.max(g_col, axis=0, keepdims=True))
    decay = jnp.exp(b_last + m_prev - m_new)
    kw = jnp.exp(g_col - m_new) * k
    c_ref[...] = decay * c_ref[...] + jnp.dot(kw.T.astype(BF16), v_b, preferred_element_type=F32)
    n_ref[...] = decay * n_ref[...] + jnp.sum(kw, axis=0, keepdims=True)
    m_ref[...] = m_new

    y = _sigmoid(o_ref[...].astype(F32)) * hh
    mu = jnp.mean(y, axis=-1, keepdims=True)
    yc = y - mu
    var = jnp.mean(yc * yc, axis=-1, keepdims=True)
    out_ref[...] = (yc * lax.rsqrt(var + LN_EPS) * ng_ref[...]).astype(out_ref.dtype)


def _mlstm(z_main, gcol, grow, conv_w, conv_b, norm_g, B, S):
    L = min(MLSTM_CHUNK, S)
    nc = S // L
    dh = A_HEAD_DIM
    H = A_HEADS
    row = lambda b, h, c: b * nc + c
    return pl.pallas_call(
        _mlstm_kernel,
        grid=(B, H, nc),
        in_specs=[
            pl.BlockSpec((L, dh), lambda b, h, c: (row(b, h, c), COL_QA // dh + h)),
            pl.BlockSpec((L, dh), lambda b, h, c: (row(b, h, c), COL_KA // dh + h)),
            pl.BlockSpec((L, dh), lambda b, h, c: (row(b, h, c), COL_VA // dh + h)),
            pl.BlockSpec((L, dh), lambda b, h, c: (row(b, h, c), COL_OA // dh + h)),
            pl.BlockSpec((None, L, SUBLANES), lambda b, h, c: (h, row(b, h, c), 0)),
            pl.BlockSpec((None, SUBLANES, L), lambda b, h, c: (h, 0, row(b, h, c))),
            pl.BlockSpec((CONV_WIDTH, dh), lambda b, h, c: (0, h)),
            pl.BlockSpec((CONV_WIDTH, dh), lambda b, h, c: (0, H + h)),
            pl.BlockSpec((1, dh), lambda b, h, c: (0, h)),
            pl.BlockSpec((1, dh), lambda b, h, c: (0, H + h)),
            pl.BlockSpec((1, dh), lambda b, h, c: (0, h)),
        ],
        out_specs=pl.BlockSpec((L, dh), lambda b, h, c: (row(b, h, c), h)),
        out_shape=jax.ShapeDtypeStruct((B * S, A_WIDTH), BF16),
        scratch_shapes=[
            pltpu.VMEM((dh, dh), F32),
            pltpu.VMEM((1, dh), F32),
            pltpu.VMEM((1, 1), F32),
            pltpu.VMEM((SUBLANES, dh), F32),
            pltpu.VMEM((SUBLANES, dh), F32),
        ],
        compiler_params=_cparams(("parallel", "parallel", "arbitrary")),
        name="mlstm",
    )(z_main, z_main, z_main, z_main, gcol, grow, conv_w, conv_w, conv_b, conv_b, norm_g)


def _lane_tiles(t):
    return [t[:, i * LANES:(i + 1) * LANES] for i in range(t.shape[1] // LANES)]


def _attn_kernel(slope_ref, q_ref, k_ref, kb_ref, v_ref, lq1_ref, lk1_ref, lq2_ref, lk2_ref, g_ref, o_ref,
                 s_ref, mx_ref, ls_ref, acc_ref, *, lambda_init):
    h = pl.program_id(1)
    qi = pl.program_id(2)
    tq = q_ref.shape[0]
    tk = tq
    slope = slope_ref[h]
    nt = (((1,), (1,)), ((), ()))

    lane = lax.broadcasted_iota(I32, (tq, B_V_DIM), 1)
    q = q_ref[...] * (B_HEAD_DIM ** -0.5)
    zero = jnp.zeros_like(q)
    bias_cols = jnp.where(lane < 2, 1.0, 0.0).astype(BF16)
    qa = (jnp.concatenate([jnp.where(lane < B_HEAD_DIM, q, zero), bias_cols], axis=1),
          jnp.concatenate([jnp.where(lane >= B_HEAD_DIM, q, zero), bias_cols], axis=1))

    def scores(j):
        start = pl.multiple_of(j * tk, tk)
        ka = jnp.concatenate([k_ref[pl.ds(start, tk), :], kb_ref[pl.ds(start, tk), :]], axis=1)
        return [lax.dot_general(qa[n], ka, nt, preferred_element_type=F32) for n in range(2)]

    def keep(j, n, t):
        t = t * LOG2_E
        s_ref[n, j] = t
        mx_ref[n] = functools.reduce(jnp.maximum, _lane_tiles(t), mx_ref[n])

    mx_ref[...] = jnp.full(mx_ref.shape, NEG_BIG, F32)

    def pass_a(j, carry):
        t = scores(j)
        for n in range(2):
            keep(j, n, t[n])
        return carry
    lax.fori_loop(0, qi, pass_a, 0)

    r = lax.broadcasted_iota(I32, (tq, tk), 0)
    c = lax.broadcasted_iota(I32, (tq, tk), 1)
    ahead = jnp.where(c <= r, 0.0, (r - c).astype(F32) * (2.0 * slope))
    corr = jnp.where((c // ATTN_CHUNK) <= (r // ATTN_CHUNK), ahead, NEG_BIG)
    t = scores(qi)
    for n in range(2):
        keep(qi, n, t[n] + corr)

    for n in range(2):
        m = jnp.max(mx_ref[n], axis=-1, keepdims=True)
        mx_ref[n] = jnp.broadcast_to(m, (tq, LANES))
    ls_ref[...] = jnp.zeros_like(ls_ref)
    acc_ref[...] = jnp.zeros_like(acc_ref)

    def pass_b(j, carry):
        start = pl.multiple_of(j * tk, tk)
        vj = v_ref[pl.ds(start, tk), :]
        for n in range(2):
            mb = mx_ref[n]
            ps = [jnp.exp2(t - mb) for t in _lane_tiles(s_ref[n, j])]
            ls_ref[n] += functools.reduce(jnp.add, ps)
            p = jnp.concatenate(ps, axis=1).astype(BF16)
            acc_ref[n] += jnp.dot(p, vj, preferred_element_type=F32)
        return carry
    lax.fori_loop(0, qi + 1, pass_b, 0)

    lam = (jnp.exp(jnp.sum(lq1_ref[...] * lk1_ref[...], axis=-1, keepdims=True))
           - jnp.exp(jnp.sum(lq2_ref[...] * lk2_ref[...], axis=-1, keepdims=True)) + lambda_init)
    l0 = jnp.sum(ls_ref[0], axis=-1, keepdims=True)
    l1 = jnp.sum(ls_ref[1], axis=-1, keepdims=True)
    o = acc_ref[0] / l0 - lam * (acc_ref[1] / l1)
    ms = jnp.mean(o * o, axis=-1, keepdims=True)
    o_ref[...] = (o * lax.rsqrt(ms + LN_EPS) * g_ref[...] * (1.0 - lambda_init)).astype(o_ref.dtype)


def _diff_attention(z_main, lq1, lk1, lq2, lk2, norm_g, B, S, lambda_init):
    tq = min(ATTN_TQ, S)
    nq = S // tq
    H = B_HEADS
    dv = B_V_DIM
    slopes = 2.0 ** (-8.0 * jnp.arange(1, H + 1, dtype=F32) / H)
    assert S <= 256 * 256
    pos = lax.broadcasted_iota(I32, (H, S, dv), 1)
    col = lax.broadcasted_iota(I32, (H, S, dv), 2)
    within = pos % 256
    kbias = jnp.where(col == 0, within, jnp.where(col == 1, pos - within, 0)).astype(F32)
    kbias = (kbias * slopes[:, None, None]).astype(BF16)
    small = pl.BlockSpec((1, B_HEAD_DIM), lambda b, h, i: (0, 0))
    return pl.pallas_call(
        functools.partial(_attn_kernel, lambda_init=lambda_init),
        grid=(B, H, nq),
        in_specs=[
            pl.BlockSpec(memory_space=pltpu.SMEM),
            pl.BlockSpec((tq, dv), lambda b, h, i: (b * nq + i, COL_QB // dv + h)),
            pl.BlockSpec((S, dv), lambda b, h, i: (b, COL_KB // dv + h)),
            pl.BlockSpec((None, S, dv), lambda b, h, i: (h, 0, 0)),
            pl.BlockSpec((S, dv), lambda b, h, i: (b, COL_VB // dv + h)),
            small, small, small, small,
            pl.BlockSpec((1, dv), lambda b, h, i: (0, 0)),
        ],
        out_specs=pl.BlockSpec((tq, dv), lambda b, h, i: (b * nq + i, h)),
        out_shape=jax.ShapeDtypeStruct((B * S, B_WIDTH), BF16),
        scratch_shapes=[
            pltpu.VMEM((2, nq, tq, tq), F32),
            pltpu.VMEM((2, tq, LANES), F32),
            pltpu.VMEM((2, tq, LANES), F32),
            pltpu.VMEM((2, tq, dv), F32),
        ],
        compiler_params=_cparams(("parallel", "parallel", "arbitrary")),
        name="diff_attention",
    )(slopes, z_main, z_main, kbias, z_main, lq1, lk1, lq2, lk2, norm_g)


def _mix_kernel(ha_ref, ob_ref, ga_ref, gb_ref, x_ref, wa_ref, wb_ref, wo_ref, g1_ref, b1_ref, wr_ref, br_ref,
                x1_ref, ri_ref, rw_ref, cnt_ref, carry_ref):
    i = pl.program_id(0)
    tm = x_ref.shape[0]

    @pl.when(i == 0)
    def _():
        carry_ref[...] = jnp.zeros_like(carry_ref)

    ya = jnp.dot(ha_ref[...], wa_ref[...], preferred_element_type=F32)
    yb = jnp.dot(ob_ref[...], wb_ref[...], preferred_element_type=F32)
    merged = _sigmoid(ga_ref[...].astype(F32)) * ya + _sigmoid(gb_ref[...].astype(F32)) * yb
    mix = jnp.dot(merged.astype(BF16), wo_ref[...], preferred_element_type=F32)
    pre = DEEPNORM_ALPHA * x_ref[...] + mix
    mu = jnp.mean(pre, axis=-1, keepdims=True)
    pc = pre - mu
    var = jnp.mean(pc * pc, axis=-1, keepdims=True)
    x1 = pc * lax.rsqrt(var + LN_EPS) * g1_ref[...] + b1_ref[...]
    x1_ref[...] = x1

    logits = _dot_3pass(x1, wr_ref[...]) + br_ref[...]
    lt = logits.T
    row8 = lax.broadcasted_iota(I32, (SUBLANES, tm), 0).astype(F32)
    gl = jnp.where(row8 < N_GROUPS, lt[0:SUBLANES, :], NEG_BIG)
    gmax = jnp.max(gl, axis=0, keepdims=True)
    gsel = jnp.min(jnp.where(gl == gmax, row8, float(SUBLANES)), axis=0, keepdims=True)
    gprob = 1.0 / jnp.sum(jnp.exp(gl - gmax), axis=0, keepdims=True)
    ing = lt[SUBLANES:2 * SUBLANES, :]
    for g in range(1, N_GROUPS):
        ing = jnp.where(gsel == g, lt[(g + 1) * SUBLANES:(g + 2) * SUBLANES, :], ing)
    v0 = jnp.max(ing, axis=0, keepdims=True)
    i0 = jnp.min(jnp.where(ing == v0, row8, float(SUBLANES)), axis=0, keepdims=True)
    ing2 = jnp.where(row8 == i0, -jnp.inf, ing)
    v1 = jnp.max(ing2, axis=0, keepdims=True)
    i1 = jnp.min(jnp.where(ing2 == v1, row8, float(SUBLANES)), axis=0, keepdims=True)
    ex = jnp.exp(v1 - v0)
    inv = 1.0 / (1.0 + ex)
    w0 = gprob * inv
    w1 = gprob * (ex * inv)
    e0 = (gsel * EXPERTS_PER_GROUP + i0).astype(I32)
    e1 = (gsel * EXPERTS_PER_GROUP + i1).astype(I32)

    rowe = lax.broadcasted_iota(I32, (N_EXPERTS, tm), 0)
    is0 = rowe == e0
    is1 = rowe == e1
    oh = jnp.where(is0, 1.0, 0.0) + jnp.where(is1, 1.0, 0.0)
    tr = lax.broadcasted_iota(I32, (tm, tm), 0)
    tc = lax.broadcasted_iota(I32, (tm, tm), 1)
    su = jnp.where(tr < tc, 1.0, 0.0).astype(BF16)
    tot = jnp.dot(oh.astype(BF16), su, preferred_element_type=F32) + carry_ref[...]
    rank0 = jnp.sum(jnp.where(is0, tot, 0.0), axis=0, keepdims=True)
    rank1 = jnp.sum(jnp.where(is1, tot, 0.0), axis=0, keepdims=True)
    carry = carry_ref[...] + jnp.sum(oh, axis=1, keepdims=True)
    carry_ref[...] = carry
    cnt_ref[...] = jnp.broadcast_to(carry, cnt_ref.shape).astype(I32)

    zi = jnp.zeros((1, tm), I32)
    ri_ref[...] = jnp.concatenate([e0, e1, rank0.astype(I32), rank1.astype(I32), zi, zi, zi, zi], axis=0)
    zf = jnp.zeros((1, tm), F32)
    rw_ref[...] = jnp.concatenate([w0, w1, zf, zf, zf, zf, zf, zf], axis=0)


def _mix(h_a, o_b, z_main, x2d, w_a, w_b, w_out, ln_g, ln_b, w_r, b_r):
    T, D = x2d.shape
    tm = min(MIX_TM, T)
    const = lambda shape: pl.BlockSpec(shape, lambda i: (0, 0), pipeline_mode=pl.Buffered(1))
    return pl.pallas_call(
        _mix_kernel,
        grid=(T // tm,),
        in_specs=[
            pl.BlockSpec((tm, A_WIDTH), lambda i: (i, 0)),
            pl.BlockSpec((tm, B_WIDTH), lambda i: (i, 0)),
            pl.BlockSpec((tm, D), lambda i: (i, COL_GA // D)),
            pl.BlockSpec((tm, D), lambda i: (i, COL_GB // D)),
            pl.BlockSpec((tm, D), lambda i: (i, 0)),
            const((A_WIDTH, D)), const((B_WIDTH, D)), const((D, D)),
            const((1, D)), const((1, D)), const((D, LANES)), const((1, LANES)),
        ],
        out_specs=[
            pl.BlockSpec((tm, D), lambda i: (i, 0)),
            pl.BlockSpec((SUBLANES, tm), lambda i: (0, i)),
            pl.BlockSpec((SUBLANES, tm), lambda i: (0, i)),
            pl.BlockSpec((N_EXPERTS, LANES), lambda i: (0, 0)),
        ],
        out_shape=[
            jax.ShapeDtypeStruct((T, D), F32),
            jax.ShapeDtypeStruct((SUBLANES, T), I32),
            jax.ShapeDtypeStruct((SUBLANES, T), F32),
            jax.ShapeDtypeStruct((N_EXPERTS, LANES), I32),
        ],
        scratch_shapes=[pltpu.VMEM((N_EXPERTS, 1), F32)],
        compiler_params=_cparams(("arbitrary",)),
        name="mix_ln1_router",
    )(h_a, o_b, z_main, z_main, x2d, w_a, w_b, w_out, ln_g, ln_b, w_r, b_r)


def _row_copy(src_ref, src_row, dst_ref, dst_row, sem):
    return pltpu.make_async_copy(src_ref.at[pl.ds(src_row, 1), :], dst_ref.at[pl.ds(dst_row, 1), :], sem)


def _rows_wait(src_ref, dst_ref, rows, sem):
    pltpu.make_async_copy(src_ref.at[pl.ds(0, rows), :], dst_ref.at[pl.ds(0, rows), :], sem).wait()


def _dispatch_kernel(cnt_ref, ri_ref, x1_ref, xpad_ref, pstart_ref, bexp_ref, nused_ref,
                     pst_ref, zero_ref, sem, zsem, *, tb, n_blocks, tm):
    i = pl.program_id(0)
    last = pl.num_programs(0) - 1

    @pl.when(i == 0)
    def _():
        def fill(b, carry):
            bexp_ref[b] = 0
            return carry
        lax.fori_loop(0, n_blocks, fill, 0)

        def per_expert(e, blk):
            pst_ref[e] = blk * tb
            pstart_ref[e] = blk * tb
            nb = (cnt_ref[e] + (tb - 1)) // tb

            def mark(b, carry):
                bexp_ref[blk + b] = e
                return carry
            lax.fori_loop(0, nb, mark, 0)
            return blk + nb
        nused = lax.fori_loop(0, N_EXPERTS, per_expert, 0)
        nused_ref[0] = nused
        pst_ref[N_EXPERTS] = nused

    def issue(t, carry):
        for k in range(TOP_K):
            pos = pst_ref[ri_ref[k, t]] + ri_ref[TOP_K + k, t]
            _row_copy(x1_ref, t, xpad_ref, pos, sem).start()
        return carry
    lax.fori_loop(0, tm, issue, 0, unroll=8)
    for k in range(TOP_K):
        _rows_wait(x1_ref, xpad_ref, tm, sem)

    @pl.when(i == last)
    def _():
        zero_ref[...] = jnp.zeros_like(zero_ref)

        def per_expert(e, carry):
            cnt = cnt_ref[e]
            first = pst_ref[e] + cnt
            npad = ((cnt + (tb - 1)) // tb) * tb - cnt

            def start(p, c2):
                _row_copy(zero_ref, 0, xpad_ref, first + p, zsem).start()
                return c2
            lax.fori_loop(0, npad, start, 0)

            def wait(p, c2):
                _row_copy(zero_ref, 0, xpad_ref, 0, zsem).wait()
                return c2
            lax.fori_loop(0, npad, wait, 0)
            return carry
        lax.fori_loop(0, N_EXPERTS, per_expert, 0)

        def block_copy(b):
            return pltpu.make_async_copy(zero_ref, xpad_ref.at[pl.ds(pl.multiple_of(b * tb, tb), tb), :], zsem)

        def start_block(b, carry):
            block_copy(b).start()
            return carry
        lax.fori_loop(pst_ref[N_EXPERTS], n_blocks, start_block, 0)

        def wait_block(b, carry):
            block_copy(b).wait()
            return carry
        lax.fori_loop(pst_ref[N_EXPERTS], n_blocks, wait_block, 0)


def _dispatch(counts, route_i, x1, tb, n_blocks):
    T, D = x1.shape
    tm = min(DISPATCH_TM, T)
    smem_full = pl.BlockSpec(memory_space=pltpu.SMEM)
    return pl.pallas_call(
        functools.partial(_dispatch_kernel, tb=tb, n_blocks=n_blocks, tm=tm),
        grid=(T // tm,),
        in_specs=[
            smem_full,
            pl.BlockSpec((SUBLANES, tm), lambda i: (0, i), memory_space=pltpu.SMEM),
            pl.BlockSpec((tm, D), lambda i: (i, 0)),
        ],
        out_specs=[
            pl.BlockSpec(memory_space=pl.ANY),
            smem_full, smem_full, smem_full,
        ],
        out_shape=[
            jax.ShapeDtypeStruct((n_blocks * tb, D), F32),
            jax.ShapeDtypeStruct((N_EXPERTS,), I32),
            jax.ShapeDtypeStruct((n_blocks,), I32),
            jax.ShapeDtypeStruct((1,), I32),
        ],
        scratch_shapes=[
            pltpu.SMEM((N_EXPERTS + 1,), I32),
            pltpu.VMEM((tb, D), F32),
            pltpu.SemaphoreType.DMA(()),
            pltpu.SemaphoreType.DMA(()),
        ],
        compiler_params=_cparams(("arbitrary",)),
        name="moe_dispatch",
    )(counts, route_i, x1)


def _expert_kernel(bexp_ref, nused_ref, x_ref, wg_ref, wu_ref, wd_ref, y_ref, wgb_ref, wub_ref, wdb_ref):
    b = pl.program_id(0)
    used = b < nused_ref[0]
    new_expert = jnp.logical_or(b == 0, bexp_ref[b] != bexp_ref[jnp.maximum(b - 1, 0)])

    @pl.when(jnp.logical_and(used, new_expert))
    def _():
        wgb_ref[...] = wg_ref[...].astype(BF16)
        wub_ref[...] = wu_ref[...].astype(BF16)
        wdb_ref[...] = wd_ref[...].astype(BF16)

    @pl.when(used)
    def _():
        x = x_ref[...].astype(BF16)
        gate = jnp.dot(x, wgb_ref[...], preferred_element_type=F32)
        up = jnp.dot(x, wub_ref[...], preferred_element_type=F32)
        hid = (gate * _sigmoid(gate) * up).astype(BF16)
        y_ref[...] = jnp.dot(hid, wdb_ref[...], preferred_element_type=F32)

    @pl.when(jnp.logical_not(used))
    def _():
        y_ref[...] = jnp.zeros_like(y_ref)


def _experts(bexp, nused, x_pad, w_gate, w_up, w_down, tb):
    P, D = x_pad.shape
    n_blocks = P // tb
    F = w_gate.shape[-1]
    blk = lambda b, be, nu: jnp.maximum(jnp.minimum(b, nu[0] - 1), 0)
    exp = lambda b, be, nu: be[blk(b, be, nu)]
    return pl.pallas_call(
        _expert_kernel,
        grid_spec=pltpu.PrefetchScalarGridSpec(
            num_scalar_prefetch=2,
            grid=(n_blocks,),
            in_specs=[
                pl.BlockSpec((tb, D), lambda b, be, nu: (blk(b, be, nu), 0)),
                pl.BlockSpec((None, D, F), lambda b, be, nu: (exp(b, be, nu), 0, 0)),
                pl.BlockSpec((None, D, F), lambda b, be, nu: (exp(b, be, nu), 0, 0)),
                pl.BlockSpec((None, F, D), lambda b, be, nu: (exp(b, be, nu), 0, 0)),
            ],
            out_specs=pl.BlockSpec((tb, D), lambda b, be, nu: (b, 0)),
            scratch_shapes=[pltpu.VMEM((D, F), BF16), pltpu.VMEM((D, F), BF16), pltpu.VMEM((F, D), BF16)],
        ),
        out_shape=jax.ShapeDtypeStruct((P, D), F32),
        compiler_params=_cparams(("arbitrary",)),
        name="moe_experts",
    )(bexp, nused, x_pad, w_gate, w_up, w_down)


def _combine_kernel(pstart_ref, ri_ref, rin_ref, x1_ref, rw_ref, g_ref, b_ref, ypad_ref, o_ref, ybuf_ref, sem):
    i = pl.program_id(0)
    tm = x1_ref.shape[0]
    slot = i % 2

    def gather(route_ref, s):
        def issue(t, carry):
            for k in range(TOP_K):
                pos = pstart_ref[route_ref[k, t]] + route_ref[TOP_K + k, t]
                _row_copy(ypad_ref, pos, ybuf_ref.at[s, k], t, sem.at[s]).start()
            return carry
        lax.fori_loop(0, tm, issue, 0, unroll=8)

    @pl.when(i == 0)
    def _():
        gather(ri_ref, 0)

    @pl.when(i + 1 < pl.num_programs(0))
    def _():
        gather(rin_ref, 1 - slot)

    for k in range(TOP_K):
        _rows_wait(ypad_ref, ybuf_ref.at[slot, k], tm, sem.at[slot])

    rw = rw_ref[...]
    ffn = ybuf_ref[slot, 0] * rw[:, 0:1] + ybuf_ref[slot, 1] * rw[:, 1:2]
    pre = DEEPNORM_ALPHA * x1_ref[...] + ffn
    mu = jnp.mean(pre, axis=-1, keepdims=True)
    pc = pre - mu
    var = jnp.mean(pc * pc, axis=-1, keepdims=True)
    o_ref[...] = pc * lax.rsqrt(var + LN_EPS) * g_ref[...] + b_ref[...]


def _combine(pstart, route_i, x1, rw_col, ln_g, ln_b, y_pad):
    T, D = x1.shape
    tm = min(COMBINE_TM, T)
    n = T // tm
    return pl.pallas_call(
        _combine_kernel,
        grid=(n,),
        in_specs=[
            pl.BlockSpec(memory_space=pltpu.SMEM),
            pl.BlockSpec((SUBLANES, tm), lambda i: (0, i), memory_space=pltpu.SMEM),
            pl.BlockSpec((SUBLANES, tm), lambda i: (0, jnp.minimum(i + 1, n - 1)), memory_space=pltpu.SMEM),
            pl.BlockSpec((tm, D), lambda i: (i, 0)),
            pl.BlockSpec((tm, SUBLANES), lambda i: (i, 0)),
            pl.BlockSpec((1, D), lambda i: (0, 0)),
            pl.BlockSpec((1, D), lambda i: (0, 0)),
            pl.BlockSpec(memory_space=pl.ANY),
        ],
        out_specs=pl.BlockSpec((tm, D), lambda i: (i, 0)),
        out_shape=jax.ShapeDtypeStruct((T, D), F32),
        scratch_shapes=[
            pltpu.VMEM((2, TOP_K, tm, D), F32),
            pltpu.SemaphoreType.DMA((2,)),
        ],
        compiler_params=_cparams(("arbitrary",)),
        name="moe_combine_ln2",
    )(pstart, route_i, route_i, x1, rw_col, ln_g, ln_b, y_pad)


def _rearrange_in_proj(w_in, b_in):
    offs = [0]
    for cwidth in (A_WIDTH, A_WIDTH, A_WIDTH, A_WIDTH, A_HEADS, A_HEADS, B_WIDTH, B_WIDTH, B_WIDTH, D_MODEL, D_MODEL):
        offs.append(offs[-1] + cwidth)
    main_cols = [(9, 11), (0, 4), (6, 9)]
    w_main = jnp.concatenate([w_in[:, offs[a]:offs[b]] for a, b in main_cols], axis=1).astype(BF16)
    b_main = jnp.concatenate([b_in[offs[a]:offs[b]] for a, b in main_cols])[None, :]
    ngate = 2 * A_HEADS
    w_gate = jnp.pad(w_in[:, offs[4]:offs[6]], ((0, 0), (0, LANES - ngate)))
    b_gate = jnp.pad(b_in[offs[4]:offs[6]], (0, LANES - ngate))[None, :]
    return w_main, b_main, w_gate, b_gate


def _layer(x, w_in, b_in, conv_w, conv_b, norm_a_g, lq1, lk1, lq2, lk2, norm_b_g, w_a, w_b, w_out,
           ln1_g, ln1_b, w_grp, b_grp, w_exp, b_exp, w_gate, w_up, w_down, ln2_g, ln2_b, lambda_init):
    B, S, D = x.shape
    T = B * S
    x2d = x.reshape(T, D)

    w_main, b_main, w_g, b_g = _rearrange_in_proj(w_in, b_in)
    z_main, z_gate = _in_projection(x2d, w_main, b_main, w_g, b_g)

    L = min(MLSTM_CHUNK, S)
    gp = _gate_prep(z_gate[:, :2 * A_HEADS].T, L)
    grow = jnp.stack([gp[:A_HEADS], gp[A_HEADS:]], axis=1)
    grow = jnp.pad(grow, ((0, 0), (0, SUBLANES - 2), (0, 0)))
    gcol = jnp.swapaxes(grow, 1, 2)

    h_a = _mlstm(z_main, gcol, grow, conv_w, conv_b[None, :], norm_a_g[None, :], B, S)
    o_b = _diff_attention(z_main, lq1[None, :], lk1[None, :], lq2[None, :], lk2[None, :],
                          norm_b_g[None, :], B, S, lambda_init)

    w_r = jnp.zeros((D, LANES), F32).at[:, :N_GROUPS].set(w_grp).at[:, SUBLANES:SUBLANES + N_EXPERTS].set(w_exp)
    b_r = jnp.zeros((LANES,), F32).at[:N_GROUPS].set(b_grp).at[SUBLANES:SUBLANES + N_EXPERTS].set(b_exp)[None, :]
    x1, route_i, route_w, counts = _mix(h_a, o_b, z_main, x2d, w_a.astype(BF16), w_b.astype(BF16),
                                        w_out.astype(BF16), ln1_g[None, :], ln1_b[None, :], w_r, b_r)

    tb = MOE_TB
    n_blocks = (T * TOP_K) // tb + N_EXPERTS
    x_pad, pstart, bexp, nused = _dispatch(counts[:, 0], route_i, x1, tb, n_blocks)
    y_pad = _experts(bexp, nused, x_pad, w_gate, w_up, w_down, tb)
    out = _combine(pstart, route_i, x1, route_w.T, ln2_g[None, :], ln2_b[None, :], y_pad)
    return out.reshape(B, S, D)


def kernel(x, w_in, b_in, conv_w, conv_b, mlstm_norm_g, lambda_q1, lambda_k1, lambda_q2, lambda_k2,
           diff_norm_g, w_a, w_b, w_out, ln1_g, ln1_b, w_grp, b_grp, w_exp, b_exp,
           w_gate, w_up, w_down, ln2_g, ln2_b):
    for l in range(DEPTH):
        lambda_init = 0.8 - 0.6 * math.exp(-0.3 * l)
        x = _layer(x, w_in[l], b_in[l], conv_w[l], conv_b[l], mlstm_norm_g[l], lambda_q1[l], lambda_k1[l],
                   lambda_q2[l], lambda_k2[l], diff_norm_g[l], w_a[l], w_b[l], w_out[l], ln1_g[l], ln1_b[l],
                   w_grp[l], b_grp[l], w_exp[l], b_exp[l], w_gate[l], w_up[l], w_down[l], ln2_g[l], ln2_b[l],
                   lambda_init)
    return x
```

```python
import functools
import math

import jax
import jax.numpy as jnp
from jax import lax
from jax.experimental import pallas as pl
from jax.experimental.pallas import tpu as pltpu

F32 = jnp.float32
BF16 = jnp.bfloat16
I32 = jnp.int32

D_MODEL = 2048
A_HEADS = 4
A_HEAD_DIM = 256
A_WIDTH = A_HEADS * A_HEAD_DIM
CONV_WIDTH = 4
B_HEADS = 8
B_HEAD_DIM = 64
B_V_DIM = 2 * B_HEAD_DIM
B_WIDTH = B_HEADS * B_V_DIM
ATTN_CHUNK = 64
N_GROUPS = 4
EXPERTS_PER_GROUP = 8
N_EXPERTS = N_GROUPS * EXPERTS_PER_GROUP
TOP_K = 2
D_EXPERT = 512
DEPTH = 1
DEEPNORM_ALPHA = (2 * DEPTH) ** 0.25
LN_EPS = 1e-5
NEG_BIG = -1e30
LOG2_E = 1.4426950408889634

LANES = 128
SUBLANES = 8
VMEM_LIMIT_BYTES = 56 * 1024 * 1024

COL_GA = 0
COL_GB = COL_GA + D_MODEL
COL_QA = COL_GB + D_MODEL
COL_KA = COL_QA + A_WIDTH
COL_VA = COL_KA + A_WIDTH
COL_OA = COL_VA + A_WIDTH
COL_QB = COL_OA + A_WIDTH
COL_KB = COL_QB + B_WIDTH
COL_VB = COL_KB + B_WIDTH
N_MAIN = COL_VB + B_WIDTH

PROJ_TM, PROJ_TN = 1024, 1024
MLSTM_CHUNK = 256
ATTN_TQ = 512
MIX_TM = 256
MOE_TB = 256
DISPATCH_TM = 256
COMBINE_TM = 256


def _cparams(sem, vmem=VMEM_LIMIT_BYTES):
    return pltpu.CompilerParams(dimension_semantics=sem, vmem_limit_bytes=vmem)


def _sigmoid(x):
    return 1.0 / (1.0 + jnp.exp(-x))


def _split_bf16(a):
    hi = a.astype(BF16)
    return hi, (a - hi.astype(F32)).astype(BF16)


def _dot_3pass(a, b):
    a_hi, a_lo = _split_bf16(a)
    b_hi, b_lo = _split_bf16(b)
    dot = functools.partial(jnp.dot, preferred_element_type=F32)
    return dot(a_hi, b_hi) + (dot(a_hi, b_lo) + dot(a_lo, b_hi))


def _proj_kernel(x_ref, w_ref, b_ref, wg_ref, bg_ref, z_ref, zg_ref, xb_ref):
    @pl.when(pl.program_id(1) == 0)
    def _():
        x = x_ref[...]
        xb_ref[...] = x.astype(BF16)
        zg_ref[...] = _dot_3pass(x, wg_ref[...]) + bg_ref[...]

    acc = jnp.dot(xb_ref[...], w_ref[...], preferred_element_type=F32)
    z_ref[...] = (acc + b_ref[...]).astype(z_ref.dtype)


def _in_projection(x2d, w_main, b_main, w_gate, b_gate):
    T, K = x2d.shape
    N = w_main.shape[1]
    tm, tn = min(PROJ_TM, T), PROJ_TN
    return pl.pallas_call(
        _proj_kernel,
        grid=(T // tm, N // tn),
        in_specs=[
            pl.BlockSpec((tm, K), lambda i, j: (i, 0)),
            pl.BlockSpec((K, tn), lambda i, j: (0, j)),
            pl.BlockSpec((1, tn), lambda i, j: (0, j)),
            pl.BlockSpec((K, LANES), lambda i, j: (0, 0)),
            pl.BlockSpec((1, LANES), lambda i, j: (0, 0)),
        ],
        out_specs=[
            pl.BlockSpec((tm, tn), lambda i, j: (i, j)),
            pl.BlockSpec((tm, LANES), lambda i, j: (i, 0)),
        ],
        out_shape=[
            jax.ShapeDtypeStruct((T, N), BF16),
            jax.ShapeDtypeStruct((T, LANES), F32),
        ],
        scratch_shapes=[pltpu.VMEM((tm, K), BF16)],
        compiler_params=_cparams(("parallel", "arbitrary")),
        name="in_projection",
    )(x2d, w_main, b_main, w_gate, b_gate)


def _gate_prep_kernel(g_ref, o_ref):
    g = g_ref[...]
    L = g.shape[1]
    lf = jnp.minimum(g, 0.0) - jnp.log(1.0 + jnp.exp(-jnp.abs(g)))
    r = lax.broadcasted_iota(I32, (L, L), 0)
    c = lax.broadcasted_iota(I32, (L, L), 1)
    tri = (r <= c).astype(F32)
    b = jnp.dot(lf, tri, preferred_element_type=F32, precision=lax.Precision.HIGHEST)
    row = lax.broadcasted_iota(I32, g.shape, 0)
    o_ref[...] = jnp.where(row < A_HEADS, g, b)


def _gate_prep(g_rows, L):
    R, T = g_rows.shape
    return pl.pallas_call(
        _gate_prep_kernel,
        grid=(T // L,),
        in_specs=[pl.BlockSpec((R, L), lambda i: (0, i))],
        out_specs=pl.BlockSpec((R, L), lambda i: (0, i)),
        out_shape=jax.ShapeDtypeStruct((R, T), F32),
        compiler_params=_cparams(("parallel",)),
        name="gate_prep",
    )(g_rows)


def _mlstm_kernel(q_ref, k_ref, v_ref, o_ref, gcol_ref, grow_ref, cw_ref, cb_ref, ng_ref, out_ref,
                  c_ref, n_ref, m_ref, qcar_ref, kcar_ref):
    ci = pl.program_id(1)
    L = q_ref.shape[0]
    dh = A_HEAD_DIM

    @pl.when(ci == 0)
    def _():
        c_ref[...] = jnp.zeros_like(c_ref)
        n_ref[...] = jnp.zeros_like(n_ref)
        m_ref[...] = jnp.zeros_like(m_ref)
        qcar_ref[...] = jnp.zeros_like(qcar_ref)
        kcar_ref[...] = jnp.zeros_like(kcar_ref)

    def conv_silu(u_ref, car_ref, lo, wcol):
        u = u_ref[:, lo:lo + dh].astype(F32)
        ext = jnp.concatenate([car_ref[:, lo:lo + dh], u], axis=0)
        w = cw_ref[:, wcol:wcol + dh]
        y = cb_ref[:, wcol:wcol + dh] + w[3:4, :] * u
        for j in range(CONV_WIDTH - 1):
            off = SUBLANES - (CONV_WIDTH - 1) + j
            y = y + w[j:j + 1, :] * ext[off:off + L, :]
        car_ref[:, lo:lo + dh] = u[L - SUBLANES:, :]
        return y * _sigmoid(y)

    r = lax.broadcasted_iota(I32, (L, L), 0)
    c = lax.broadcasted_iota(I32, (L, L), 1)
    causal = r >= c
    gcol = gcol_ref[...]
    grow = grow_ref[...]

    for h in range(A_HEADS):
        lo = h * dh
        q = conv_silu(q_ref, qcar_ref, lo, lo)
        k = conv_silu(k_ref, kcar_ref, lo, A_WIDTH + lo) * (dh ** -0.5)
        v_b = v_ref[:, lo:lo + dh]
        q_b = q.astype(BF16)
        k_b = k.astype(BF16)

        i_col, b_col = gcol[:, h:h + 1], gcol[:, A_HEADS + h:A_HEADS + h + 1]
        i_row, b_row = grow[h:h + 1, :], grow[A_HEADS + h:A_HEADS + h + 1, :]
        b_last = b_row[:, L - 1:L]
        m_prev = m_ref[h]

        dmat = jnp.where(causal, b_col - b_row + i_row, NEG_BIG)
        inter = b_col + m_prev
        m_t = jnp.maximum(inter, jnp.max(dmat, axis=-1, keepdims=True))
        s = lax.dot_general(q_b, k_b, (((1,), (1,)), ((), ())), preferred_element_type=F32)
        w_intra = jnp.exp(dmat - m_t) * s
        w_inter = jnp.exp(inter - m_t)
        qc = jnp.dot(q_b, c_ref[h].astype(BF16), preferred_element_type=F32)
        num = w_inter * qc + jnp.dot(w_intra.astype(BF16), v_b, preferred_element_type=F32)
        qn = jnp.sum(q * n_ref[h], axis=-1, keepdims=True)
        den = w_inter * qn + jnp.sum(w_intra, axis=-1, keepdims=True)
        hh = num / jnp.maximum(jnp.abs(den), jnp.exp(-m_t))

        g_col = b_last - b_col + i_col
        m_new = jnp.maximum(b_last + m_prev, jnp.max(g_col, axis=0, keepdims=True))
        decay = jnp.exp(b_last + m_prev - m_new)
        kw = jnp.exp(g_col - m_new) * k
        c_ref[h] = decay * c_ref[h] + jnp.dot(kw.T.astype(BF16), v_b, preferred_element_type=F32)
        n_ref[h] = decay * n_ref[h] + jnp.sum(kw, axis=0, keepdims=True)
        m_ref[h] = m_new

        y = _sigmoid(o_ref[:, lo:lo + dh].astype(F32)) * hh
        mu = jnp.mean(y, axis=-1, keepdims=True)
        yc = y - mu
        var = jnp.mean(yc * yc, axis=-1, keepdims=True)
        out_ref[:, lo:lo + dh] = (yc * lax.rsqrt(var + LN_EPS) * ng_ref[:, lo:lo + dh]).astype(out_ref.dtype)


def _mlstm(z_main, gcol, grow, conv_w, conv_b, norm_g, B, S):
    L = min(MLSTM_CHUNK, S)
    nc = S // L
    dh = A_HEAD_DIM
    H = A_HEADS
    W = A_WIDTH
    row = lambda b, c: b * nc + c
    full = lambda shape: pl.BlockSpec(shape, lambda b, c: (0, 0))
    return pl.pallas_call(
        _mlstm_kernel,
        grid=(B, nc),
        in_specs=[
            pl.BlockSpec((L, W), lambda b, c: (row(b, c), COL_QA // W)),
            pl.BlockSpec((L, W), lambda b, c: (row(b, c), COL_KA // W)),
            pl.BlockSpec((L, W), lambda b, c: (row(b, c), COL_VA // W)),
            pl.BlockSpec((L, W), lambda b, c: (row(b, c), COL_OA // W)),
            pl.BlockSpec((L, SUBLANES), lambda b, c: (row(b, c), 0)),
            pl.BlockSpec((SUBLANES, L), lambda b, c: (0, row(b, c))),
            full((CONV_WIDTH, 2 * W)), full((1, 2 * W)), full((1, W)),
        ],
        out_specs=pl.BlockSpec((L, W), lambda b, c: (row(b, c), 0)),
        out_shape=jax.ShapeDtypeStruct((B * S, W), BF16),
        scratch_shapes=[
            pltpu.VMEM((H, dh, dh), F32),
            pltpu.VMEM((H, 1, dh), F32),
            pltpu.VMEM((H, 1, 1), F32),
            pltpu.VMEM((SUBLANES, W), F32),
            pltpu.VMEM((SUBLANES, W), F32),
        ],
        compiler_params=_cparams(("parallel", "arbitrary")),
        name="mlstm",
    )(z_main, z_main, z_main, z_main, gcol, grow, conv_w, conv_b, norm_g)


def _lane_tiles(t):
    return [t[:, i * LANES:(i + 1) * LANES] for i in range(t.shape[1] // LANES)]


def _attn_kernel(slope_ref, q_ref, k_ref, kb_ref, v_ref, lq1_ref, lk1_ref, lq2_ref, lk2_ref, g_ref, o_ref,
                 s_ref, mx_ref, ls_ref, acc_ref, corr_ref, *, lambda_init):
    h = pl.program_id(1)
    qi = pl.program_id(2)
    tq = q_ref.shape[0]
    tk = tq
    slope = slope_ref[h]
    nt = (((1,), (1,)), ((), ()))

    lane = lax.broadcasted_iota(I32, (tq, B_V_DIM), 1)
    q = q_ref[...] * (B_HEAD_DIM ** -0.5)
    zero = jnp.zeros_like(q)
    bias_cols = jnp.where(lane < 2, 1.0, 0.0).astype(BF16)
    qa = (jnp.concatenate([jnp.where(lane < B_HEAD_DIM, q, zero), bias_cols], axis=1),
          jnp.concatenate([jnp.where(lane >= B_HEAD_DIM, q, zero), bias_cols], axis=1))

    def scores(j):
        start = pl.multiple_of(j * tk, tk)
        ka = jnp.concatenate([k_ref[pl.ds(start, tk), :], kb_ref[pl.ds(start, tk), :]], axis=1)
        return [lax.dot_general(qa[n], ka, nt, preferred_element_type=F32) for n in range(2)]

    def keep(j, n, t):
        t = t * LOG2_E
        s_ref[n, j] = t
        mx_ref[n] = functools.reduce(jnp.maximum, _lane_tiles(t), mx_ref[n])

    mx_ref[...] = jnp.full(mx_ref.shape, NEG_BIG, F32)

    def pass_a(j):
        t = scores(j)
        for n in range(2):
            keep(j, n, t[n])

    def pass_a_pair(jj, carry):
        pass_a(2 * jj)
        pass_a(2 * jj + 1)
        return carry
    lax.fori_loop(0, qi // 2, pass_a_pair, 0)

    @pl.when(qi % 2 == 1)
    def _():
        pass_a(qi - 1)

    @pl.when(qi == 0)
    def _():
        r = lax.broadcasted_iota(I32, (tq, tk), 0)
        c = lax.broadcasted_iota(I32, (tq, tk), 1)
        ahead = jnp.where(c <= r, 0.0, (r - c).astype(F32) * (2.0 * slope))
        corr_ref[...] = jnp.where((c // ATTN_CHUNK) <= (r // ATTN_CHUNK), ahead, NEG_BIG)

    t = scores(qi)
    for n in range(2):
        keep(qi, n, t[n] + corr_ref[...])

    for n in range(2):
        m = jnp.max(mx_ref[n], axis=-1, keepdims=True)
        mx_ref[n] = jnp.broadcast_to(m, (tq, LANES))
    ls_ref[...] = jnp.zeros_like(ls_ref)
    acc_ref[...] = jnp.zeros_like(acc_ref)

    def pass_b(j, carry):
        start = pl.multiple_of(j * tk, tk)
        vj = v_ref[pl.ds(start, tk), :]
        for n in range(2):
            mb = mx_ref[n]
            ps = [jnp.exp2(t - mb) for t in _lane_tiles(s_ref[n, j])]
            ls_ref[n] += functools.reduce(jnp.add, ps)
            p = jnp.concatenate(ps, axis=1).astype(BF16)
            acc_ref[n] += jnp.dot(p, vj, preferred_element_type=F32)
        return carry
    lax.fori_loop(0, qi + 1, pass_b, 0)

    lam = (jnp.exp(jnp.sum(lq1_ref[...] * lk1_ref[...], axis=-1, keepdims=True))
           - jnp.exp(jnp.sum(lq2_ref[...] * lk2_ref[...], axis=-1, keepdims=True)) + lambda_init)
    l0 = jnp.sum(ls_ref[0], axis=-1, keepdims=True)
    l1 = jnp.sum(ls_ref[1], axis=-1, keepdims=True)
    o = acc_ref[0] / l0 - lam * (acc_ref[1] / l1)
    ms = jnp.mean(o * o, axis=-1, keepdims=True)
    o_ref[...] = (o * lax.rsqrt(ms + LN_EPS) * g_ref[...] * (1.0 - lambda_init)).astype(o_ref.dtype)


def _diff_attention(z_main, lq1, lk1, lq2, lk2, norm_g, B, S, lambda_init):
    tq = min(ATTN_TQ, S)
    nq = S // tq
    H = B_HEADS
    dv = B_V_DIM
    slopes = 2.0 ** (-8.0 * jnp.arange(1, H + 1, dtype=F32) / H)
    assert S <= 256 * 256
    pos = lax.broadcasted_iota(I32, (H, S, dv), 1)
    col = lax.broadcasted_iota(I32, (H, S, dv), 2)
    within = pos % 256
    kbias = jnp.where(col == 0, within, jnp.where(col == 1, pos - within, 0)).astype(F32)
    kbias = (kbias * slopes[:, None, None]).astype(BF16)
    small = pl.BlockSpec((1, B_HEAD_DIM), lambda b, h, i: (0, 0))
    return pl.pallas_call(
        functools.partial(_attn_kernel, lambda_init=lambda_init),
        grid=(B, H, nq),
        in_specs=[
            pl.BlockSpec(memory_space=pltpu.SMEM),
            pl.BlockSpec((tq, dv), lambda b, h, i: (b * nq + i, COL_QB // dv + h)),
            pl.BlockSpec((S, dv), lambda b, h, i: (b, COL_KB // dv + h)),
            pl.BlockSpec((None, S, dv), lambda b, h, i: (h, 0, 0)),
            pl.BlockSpec((S, dv), lambda b, h, i: (b, COL_VB // dv + h)),
            small, small, small, small,
            pl.BlockSpec((1, dv), lambda b, h, i: (0, 0)),
        ],
        out_specs=pl.BlockSpec((tq, dv), lambda b, h, i: (b * nq + i, h)),
        out_shape=jax.ShapeDtypeStruct((B * S, B_WIDTH), BF16),
        scratch_shapes=[
            pltpu.VMEM((2, nq, tq, tq), F32),
            pltpu.VMEM((2, tq, LANES), F32),
            pltpu.VMEM((2, tq, LANES), F32),
            pltpu.VMEM((2, tq, dv), F32),
            pltpu.VMEM((tq, tq), F32),
        ],
        compiler_params=_cparams(("parallel", "parallel", "arbitrary")),
        name="diff_attention",
    )(slopes, z_main, z_main, kbias, z_main, lq1, lk1, lq2, lk2, norm_g)


def _mix_kernel(ha_ref, ob_ref, ga_ref, gb_ref, x_ref, wa_ref, wb_ref, wo_ref, g1_ref, b1_ref, wr_ref, br_ref,
                x1_ref, ri_ref, rw_ref, cnt_ref, carry_ref):
    i = pl.program_id(0)
    tm = x_ref.shape[0]

    @pl.when(i == 0)
    def _():
        carry_ref[...] = jnp.zeros_like(carry_ref)

    ya = jnp.dot(ha_ref[...], wa_ref[...], preferred_element_type=F32)
    yb = jnp.dot(ob_ref[...], wb_ref[...], preferred_element_type=F32)
    merged = _sigmoid(ga_ref[...].astype(F32)) * ya + _sigmoid(gb_ref[...].astype(F32)) * yb
    mix = jnp.dot(merged.astype(BF16), wo_ref[...], preferred_element_type=F32)
    pre = DEEPNORM_ALPHA * x_ref[...] + mix
    mu = jnp.mean(pre, axis=-1, keepdims=True)
    pc = pre - mu
    var = jnp.mean(pc * pc, axis=-1, keepdims=True)
    x1 = pc * lax.rsqrt(var + LN_EPS) * g1_ref[...] + b1_ref[...]
    x1_ref[...] = x1

    logits = _dot_3pass(x1, wr_ref[...]) + br_ref[...]
    lt = logits.T
    row8 = lax.broadcasted_iota(I32, (SUBLANES, tm), 0).astype(F32)
    gl = jnp.where(row8 < N_GROUPS, lt[0:SUBLANES, :], NEG_BIG)
    gmax = jnp.max(gl, axis=0, keepdims=True)
    gsel = jnp.min(jnp.where(gl == gmax, row8, float(SUBLANES)), axis=0, keepdims=True)
    gprob = 1.0 / jnp.sum(jnp.exp(gl - gmax), axis=0, keepdims=True)
    ing = lt[SUBLANES:2 * SUBLANES, :]
    for g in range(1, N_GROUPS):
        ing = jnp.where(gsel == g, lt[(g + 1) * SUBLANES:(g + 2) * SUBLANES, :], ing)
    v0 = jnp.max(ing, axis=0, keepdims=True)
    i0 = jnp.min(jnp.where(ing == v0, row8, float(SUBLANES)), axis=0, keepdims=True)
    ing2 = jnp.where(row8 == i0, -jnp.inf, ing)
    v1 = jnp.max(ing2, axis=0, keepdims=True)
    i1 = jnp.min(jnp.where(ing2 == v1, row8, float(SUBLANES)), axis=0, keepdims=True)
    ex = jnp.exp(v1 - v0)
    inv = 1.0 / (1.0 + ex)
    w0 = gprob * inv
    w1 = gprob * (ex * inv)
    e0 = (gsel * EXPERTS_PER_GROUP + i0).astype(I32)
    e1 = (gsel * EXPERTS_PER_GROUP + i1).astype(I32)

    rowe = lax.broadcasted_iota(I32, (N_EXPERTS, tm), 0)
    is0 = rowe == e0
    is1 = rowe == e1
    oh = jnp.where(is0, 1.0, 0.0) + jnp.where(is1, 1.0, 0.0)
    tr = lax.broadcasted_iota(I32, (tm, tm), 0)
    tc = lax.broadcasted_iota(I32, (tm, tm), 1)
    su = jnp.where(tr < tc, 1.0, 0.0).astype(BF16)
    tot = jnp.dot(oh.astype(BF16), su, preferred_element_type=F32) + carry_ref[...]
    rank0 = jnp.sum(jnp.where(is0, tot, 0.0), axis=0, keepdims=True)
    rank1 = jnp.sum(jnp.where(is1, tot, 0.0), axis=0, keepdims=True)
    carry = carry_ref[...] + jnp.sum(oh, axis=1, keepdims=True)
    carry_ref[...] = carry
    cnt_ref[...] = jnp.broadcast_to(carry, cnt_ref.shape).astype(I32)

    zi = jnp.zeros((1, tm), I32)
    ri_ref[...] = jnp.concatenate([e0, e1, rank0.astype(I32), rank1.astype(I32), zi, zi, zi, zi], axis=0)
    zf = jnp.zeros((1, tm), F32)
    rw_ref[...] = jnp.concatenate([w0, w1, zf, zf, zf, zf, zf, zf], axis=0)


def _mix(h_a, o_b, z_main, x2d, w_a, w_b, w_out, ln_g, ln_b, w_r, b_r):
    T, D = x2d.shape
    tm = min(MIX_TM, T)
    const = lambda shape: pl.BlockSpec(shape, lambda i: (0, 0), pipeline_mode=pl.Buffered(1))
    return pl.pallas_call(
        _mix_kernel,
        grid=(T // tm,),
        in_specs=[
            pl.BlockSpec((tm, A_WIDTH), lambda i: (i, 0)),
            pl.BlockSpec((tm, B_WIDTH), lambda i: (i, 0)),
            pl.BlockSpec((tm, D), lambda i: (i, COL_GA // D)),
            pl.BlockSpec((tm, D), lambda i: (i, COL_GB // D)),
            pl.BlockSpec((tm, D), lambda i: (i, 0)),
            const((A_WIDTH, D)), const((B_WIDTH, D)), const((D, D)),
            const((1, D)), const((1, D)), const((D, LANES)), const((1, LANES)),
        ],
        out_specs=[
            pl.BlockSpec((tm, D), lambda i: (i, 0)),
            pl.BlockSpec((SUBLANES, tm), lambda i: (0, i)),
            pl.BlockSpec((SUBLANES, tm), lambda i: (0, i)),
            pl.BlockSpec((N_EXPERTS, LANES), lambda i: (0, 0)),
        ],
        out_shape=[
            jax.ShapeDtypeStruct((T, D), F32),
            jax.ShapeDtypeStruct((SUBLANES, T), I32),
            jax.ShapeDtypeStruct((SUBLANES, T), F32),
            jax.ShapeDtypeStruct((N_EXPERTS, LANES), I32),
        ],
        scratch_shapes=[pltpu.VMEM((N_EXPERTS, 1), F32)],
        compiler_params=_cparams(("arbitrary",)),
        name="mix_ln1_router",
    )(h_a, o_b, z_main, z_main, x2d, w_a, w_b, w_out, ln_g, ln_b, w_r, b_r)


def _row_copy(src_ref, src_row, dst_ref, dst_row, sem):
    return pltpu.make_async_copy(src_ref.at[pl.ds(src_row, 1), :], dst_ref.at[pl.ds(dst_row, 1), :], sem)


def _rows_wait(src_ref, dst_ref, rows, sem):
    pltpu.make_async_copy(src_ref.at[pl.ds(0, rows), :], dst_ref.at[pl.ds(0, rows), :], sem).wait()


def _dispatch_kernel(cnt_ref, ri_ref, x1_ref, xpad_ref, pstart_ref, bexp_ref, nused_ref,
                     pst_ref, zero_ref, sem, zsem, *, tb, n_blocks, tm):
    i = pl.program_id(0)
    last = pl.num_programs(0) - 1

    @pl.when(i == 0)
    def _():
        def fill(b, carry):
            bexp_ref[b] = 0
            return carry
        lax.fori_loop(0, n_blocks, fill, 0)

        def per_expert(e, blk):
            pst_ref[e] = blk * tb
            pstart_ref[e] = blk * tb
            nb = (cnt_ref[e] + (tb - 1)) // tb

            def mark(b, carry):
                bexp_ref[blk + b] = e
                return carry
            lax.fori_loop(0, nb, mark, 0)
            return blk + nb
        nused = lax.fori_loop(0, N_EXPERTS, per_expert, 0)
        nused_ref[0] = nused
        pst_ref[N_EXPERTS] = nused

    def issue(t, carry):
        for k in range(TOP_K):
            pos = pst_ref[ri_ref[k, t]] + ri_ref[TOP_K + k, t]
            _row_copy(x1_ref, t, xpad_ref, pos, sem).start()
        return carry
    lax.fori_loop(0, tm, issue, 0, unroll=8)
    for k in range(TOP_K):
        _rows_wait(x1_ref, xpad_ref, tm, sem)

    @pl.when(i == last)
    def _():
        zero_ref[...] = jnp.zeros_like(zero_ref)

        def per_expert(e, carry):
            cnt = cnt_ref[e]
            first = pst_ref[e] + cnt
            npad = ((cnt + (tb - 1)) // tb) * tb - cnt

            def start(p, c2):
                _row_copy(zero_ref, 0, xpad_ref, first + p, zsem).start()
                return c2
            lax.fori_loop(0, npad, start, 0)

            def wait(p, c2):
                _row_copy(zero_ref, 0, xpad_ref, 0, zsem).wait()
                return c2
            lax.fori_loop(0, npad, wait, 0)
            return carry
        lax.fori_loop(0, N_EXPERTS, per_expert, 0)

        def block_copy(b):
            return pltpu.make_async_copy(zero_ref, xpad_ref.at[pl.ds(pl.multiple_of(b * tb, tb), tb), :], zsem)

        def start_block(b, carry):
            block_copy(b).start()
            return carry
        lax.fori_loop(pst_ref[N_EXPERTS], n_blocks, start_block, 0)

        def wait_block(b, carry):
            block_copy(b).wait()
            return carry
        lax.fori_loop(pst_ref[N_EXPERTS], n_blocks, wait_block, 0)


def _dispatch(counts, route_i, x1, tb, n_blocks):
    T, D = x1.shape
    tm = min(DISPATCH_TM, T)
    smem_full = pl.BlockSpec(memory_space=pltpu.SMEM)
    return pl.pallas_call(
        functools.partial(_dispatch_kernel, tb=tb, n_blocks=n_blocks, tm=tm),
        grid=(T // tm,),
        in_specs=[
            smem_full,
            pl.BlockSpec((SUBLANES, tm), lambda i: (0, i), memory_space=pltpu.SMEM),
            pl.BlockSpec((tm, D), lambda i: (i, 0)),
        ],
        out_specs=[
            pl.BlockSpec(memory_space=pl.ANY),
            smem_full, smem_full, smem_full,
        ],
        out_shape=[
            jax.ShapeDtypeStruct((n_blocks * tb, D), F32),
            jax.ShapeDtypeStruct((N_EXPERTS,), I32),
            jax.ShapeDtypeStruct((n_blocks,), I32),
            jax.ShapeDtypeStruct((1,), I32),
        ],
        scratch_shapes=[
            pltpu.SMEM((N_EXPERTS + 1,), I32),
            pltpu.VMEM((tb, D), F32),
            pltpu.SemaphoreType.DMA(()),
            pltpu.SemaphoreType.DMA(()),
        ],
        compiler_params=_cparams(("arbitrary",)),
        name="moe_dispatch",
    )(counts, route_i, x1)


def _expert_kernel(bexp_ref, nused_ref, x_ref, wg_ref, wu_ref, wd_ref, y_ref, wgb_ref, wub_ref, wdb_ref):
    b = pl.program_id(0)
    used = b < nused_ref[0]
    new_expert = jnp.logical_or(b == 0, bexp_ref[b] != bexp_ref[jnp.maximum(b - 1, 0)])

    @pl.when(jnp.logical_and(used, new_expert))
    def _():
        wgb_ref[...] = wg_ref[...].astype(BF16)
        wub_ref[...] = wu_ref[...].astype(BF16)
        wdb_ref[...] = wd_ref[...].astype(BF16)

    @pl.when(used)
    def _():
        x = x_ref[...].astype(BF16)
        gate = jnp.dot(x, wgb_ref[...], preferred_element_type=F32)
        up = jnp.dot(x, wub_ref[...], preferred_element_type=F32)
        hid = (gate * _sigmoid(gate) * up).astype(BF16)
        y_ref[...] = jnp.dot(hid, wdb_ref[...], preferred_element_type=F32)

    @pl.when(jnp.logical_not(used))
    def _():
        y_ref[...] = jnp.zeros_like(y_ref)


def _experts(bexp, nused, x_pad, w_gate, w_up, w_down, tb):
    P, D = x_pad.shape
    n_blocks = P // tb
    F = w_gate.shape[-1]
    blk = lambda b, be, nu: jnp.maximum(jnp.minimum(b, nu[0] - 1), 0)
    exp = lambda b, be, nu: be[blk(b, be, nu)]
    return pl.pallas_call(
        _expert_kernel,
        grid_spec=pltpu.PrefetchScalarGridSpec(
            num_scalar_prefetch=2,
            grid=(n_blocks,),
            in_specs=[
                pl.BlockSpec((tb, D), lambda b, be, nu: (blk(b, be, nu), 0)),
                pl.BlockSpec((None, D, F), lambda b, be, nu: (exp(b, be, nu), 0, 0)),
                pl.BlockSpec((None, D, F), lambda b, be, nu: (exp(b, be, nu), 0, 0)),
                pl.BlockSpec((None, F, D), lambda b, be, nu: (exp(b, be, nu), 0, 0)),
            ],
            out_specs=pl.BlockSpec((tb, D), lambda b, be, nu: (b, 0)),
            scratch_shapes=[pltpu.VMEM((D, F), BF16), pltpu.VMEM((D, F), BF16), pltpu.VMEM((F, D), BF16)],
        ),
        out_shape=jax.ShapeDtypeStruct((P, D), F32),
        compiler_params=_cparams(("arbitrary",)),
        name="moe_experts",
    )(bexp, nused, x_pad, w_gate, w_up, w_down)


def _combine_kernel(pstart_ref, ri_ref, rin_ref, x1_ref, rw_ref, g_ref, b_ref, ypad_ref, o_ref, ybuf_ref, sem):
    i = pl.program_id(0)
    tm = x1_ref.shape[0]
    slot = i % 2

    def gather(route_ref, s):
        def issue(t, carry):
            for k in range(TOP_K):
                pos = pstart_ref[route_ref[k, t]] + route_ref[TOP_K + k, t]
                _row_copy(ypad_ref, pos, ybuf_ref.at[s, k], t, sem.at[s]).start()
            return carry
        lax.fori_loop(0, tm, issue, 0, unroll=8)

    @pl.when(i == 0)
    def _():
        gather(ri_ref, 0)

    @pl.when(i + 1 < pl.num_programs(0))
    def _():
        gather(rin_ref, 1 - slot)

    for k in range(TOP_K):
        _rows_wait(ypad_ref, ybuf_ref.at[slot, k], tm, sem.at[slot])

    rw = rw_ref[...]
    ffn = ybuf_ref[slot, 0] * rw[:, 0:1] + ybuf_ref[slot, 1] * rw[:, 1:2]
    pre = DEEPNORM_ALPHA * x1_ref[...] + ffn
    mu = jnp.mean(pre, axis=-1, keepdims=True)
    pc = pre - mu
    var = jnp.mean(pc * pc, axis=-1, keepdims=True)
    o_ref[...] = pc * lax.rsqrt(var + LN_EPS) * g_ref[...] + b_ref[...]


def _combine(pstart, route_i, x1, rw_col, ln_g, ln_b, y_pad):
    T, D = x1.shape
    tm = min(COMBINE_TM, T)
    n = T // tm
    return pl.pallas_call(
        _combine_kernel,
        grid=(n,),
        in_specs=[
            pl.BlockSpec(memory_space=pltpu.SMEM),
            pl.BlockSpec((SUBLANES, tm), lambda i: (0, i), memory_space=pltpu.SMEM),
            pl.BlockSpec((SUBLANES, tm), lambda i: (0, jnp.minimum(i + 1, n - 1)), memory_space=pltpu.SMEM),
            pl.BlockSpec((tm, D), lambda i: (i, 0)),
            pl.BlockSpec((tm, SUBLANES), lambda i: (i, 0)),
            pl.BlockSpec((1, D), lambda i: (0, 0)),
            pl.BlockSpec((1, D), lambda i: (0, 0)),
            pl.BlockSpec(memory_space=pl.ANY),
        ],
        out_specs=pl.BlockSpec((tm, D), lambda i: (i, 0)),
        out_shape=jax.ShapeDtypeStruct((T, D), F32),
        scratch_shapes=[
            pltpu.VMEM((2, TOP_K, tm, D), F32),
            pltpu.SemaphoreType.DMA((2,)),
        ],
        compiler_params=_cparams(("arbitrary",)),
        name="moe_combine_ln2",
    )(pstart, route_i, route_i, x1, rw_col, ln_g, ln_b, y_pad)


def _rearrange_in_proj(w_in, b_in):
    offs = [0]
    for cwidth in (A_WIDTH, A_WIDTH, A_WIDTH, A_WIDTH, A_HEADS, A_HEADS, B_WIDTH, B_WIDTH, B_WIDTH, D_MODEL, D_MODEL):
        offs.append(offs[-1] + cwidth)
    main_cols = [(9, 11), (0, 4), (6, 9)]
    w_main = jnp.concatenate([w_in[:, offs[a]:offs[b]] for a, b in main_cols], axis=1).astype(BF16)
    b_main = jnp.concatenate([b_in[offs[a]:offs[b]] for a, b in main_cols])[None, :]
    ngate = 2 * A_HEADS
    w_gate = jnp.pad(w_in[:, offs[4]:offs[6]], ((0, 0), (0, LANES - ngate)))
    b_gate = jnp.pad(b_in[offs[4]:offs[6]], (0, LANES - ngate))[None, :]
    return w_main, b_main, w_gate, b_gate


def _layer(x, w_in, b_in, conv_w, conv_b, norm_a_g, lq1, lk1, lq2, lk2, norm_b_g, w_a, w_b, w_out,
           ln1_g, ln1_b, w_grp, b_grp, w_exp, b_exp, w_gate, w_up, w_down, ln2_g, ln2_b, lambda_init):
    B, S, D = x.shape
    T = B * S
    x2d = x.reshape(T, D)

    w_main, b_main, w_g, b_g = _rearrange_in_proj(w_in, b_in)
    z_main, z_gate = _in_projection(x2d, w_main, b_main, w_g, b_g)

    L = min(MLSTM_CHUNK, S)
    gp = _gate_prep(z_gate[:, :2 * A_HEADS].T, L)
    h_a = _mlstm(z_main, gp.T, gp, conv_w, conv_b[None, :], norm_a_g[None, :], B, S)
    o_b = _diff_attention(z_main, lq1[None, :], lk1[None, :], lq2[None, :], lk2[None, :],
                          norm_b_g[None, :], B, S, lambda_init)

    w_r = jnp.zeros((D, LANES), F32).at[:, :N_GROUPS].set(w_grp).at[:, SUBLANES:SUBLANES + N_EXPERTS].set(w_exp)
    b_r = jnp.zeros((LANES,), F32).at[:N_GROUPS].set(b_grp).at[SUBLANES:SUBLANES + N_EXPERTS].set(b_exp)[None, :]
    x1, route_i, route_w, counts = _mix(h_a, o_b, z_main, x2d, w_a.astype(BF16), w_b.astype(BF16),
                                        w_out.astype(BF16), ln1_g[None, :], ln1_b[None, :], w_r, b_r)

    tb = MOE_TB
    n_blocks = (T * TOP_K) // tb + N_EXPERTS
    x_pad, pstart, bexp, nused = _dispatch(counts[:, 0], route_i, x1, tb, n_blocks)
    y_pad = _experts(bexp, nused, x_pad, w_gate, w_up, w_down, tb)
    out = _combine(pstart, route_i, x1, route_w.T, ln2_g[None, :], ln2_b[None, :], y_pad)
    return out.reshape(B, S, D)


def kernel(x, w_in, b_in, conv_w, conv_b, mlstm_norm_g, lambda_q1, lambda_k1, lambda_q2, lambda_k2,
           diff_norm_g, w_a, w_b, w_out, ln1_g, ln1_b, w_grp, b_grp, w_exp, b_exp,
           w_gate, w_up, w_down, ln2_g, ln2_b):
    for l in range(DEPTH):
        lambda_init = 0.8 - 0.6 * math.exp(-0.3 * l)
        x = _layer(x, w_in[l], b_in[l], conv_w[l], conv_b[l], mlstm_norm_g[l], lambda_q1[l], lambda_k1[l],
                   lambda_q2[l], lambda_k2[l], diff_norm_g[l], w_a[l], w_b[l], w_out[l], ln1_g[l], ln1_b[l],
                   w_grp[l], b_grp[l], w_exp[l], b_exp[l], w_gate[l], w_up[l], w_down[l], ln2_g[l], ln2_b[l],
                   lambda_init)
    return x
```

```python
import functools
import math

import jax
import jax.numpy as jnp
from jax import lax
from jax.experimental import pallas as pl
from jax.experimental.pallas import tpu as pltpu

F32 = jnp.float32
BF16 = jnp.bfloat16
I32 = jnp.int32

D_MODEL = 2048
A_HEADS = 4
A_HEAD_DIM = 256
A_WIDTH = A_HEADS * A_HEAD_DIM
CONV_WIDTH = 4
B_HEADS = 8
B_HEAD_DIM = 64
B_V_DIM = 2 * B_HEAD_DIM
B_WIDTH = B_HEADS * B_V_DIM
ATTN_CHUNK = 64
N_GROUPS = 4
EXPERTS_PER_GROUP = 8
N_EXPERTS = N_GROUPS * EXPERTS_PER_GROUP
TOP_K = 2
D_EXPERT = 512
DEPTH = 1
DEEPNORM_ALPHA = (2 * DEPTH) ** 0.25
LN_EPS = 1e-5
NEG_BIG = -1e30
LOG2_E = 1.4426950408889634

LANES = 128
SUBLANES = 8
VMEM_LIMIT_BYTES = 56 * 1024 * 1024

COL_GA = 0
COL_GB = COL_GA + D_MODEL
COL_QA = COL_GB + D_MODEL
COL_KA = COL_QA + A_WIDTH
COL_VA = COL_KA + A_WIDTH
COL_OA = COL_VA + A_WIDTH
COL_QB = COL_OA + A_WIDTH
COL_KB = COL_QB + B_WIDTH
COL_VB = COL_KB + B_WIDTH
N_MAIN = COL_VB + B_WIDTH

PROJ_TM, PROJ_TN = 1024, 1024
MLSTM_CHUNK = 256
ATTN_TQ = 512
MIX_TM = 256
MOE_TB = 256
DISPATCH_TM = 256
COMBINE_TM = 256


def _cparams(sem, vmem=VMEM_LIMIT_BYTES):
    return pltpu.CompilerParams(dimension_semantics=sem, vmem_limit_bytes=vmem)


def _sigmoid(x):
    return 1.0 / (1.0 + jnp.exp(-x))


U32 = jnp.uint32


def _pack_halves(x):
    n = x.shape[1] // 2
    lo = lax.bitcast_convert_type(x[:, :n].astype(BF16).astype(F32), U32)
    hi = lax.bitcast_convert_type(x[:, n:].astype(BF16).astype(F32), U32)
    return (lo >> 16) | hi


def _unpack_halves(u):
    lo = lax.bitcast_convert_type(u << 16, F32)
    hi = lax.bitcast_convert_type(u & jnp.uint32(0xFFFF0000), F32)
    return lo, hi


def _split_bf16(a):
    hi = a.astype(BF16)
    return hi, (a - hi.astype(F32)).astype(BF16)


def _dot_3pass(a, b):
    a_hi, a_lo = _split_bf16(a)
    b_hi, b_lo = _split_bf16(b)
    dot = functools.partial(jnp.dot, preferred_element_type=F32)
    return dot(a_hi, b_hi) + (dot(a_hi, b_lo) + dot(a_lo, b_hi))


def _proj_kernel(x_ref, w_ref, b_ref, wg_ref, bg_ref, z_ref, zg_ref, xb_ref):
    @pl.when(pl.program_id(1) == 0)
    def _():
        x = x_ref[...]
        xb_ref[...] = x.astype(BF16)
        zg_ref[...] = _dot_3pass(x, wg_ref[...]) + bg_ref[...]

    acc = jnp.dot(xb_ref[...], w_ref[...], preferred_element_type=F32)
    z_ref[...] = (acc + b_ref[...]).astype(z_ref.dtype)


def _in_projection(x2d, w_main, b_main, w_gate, b_gate):
    T, K = x2d.shape
    N = w_main.shape[1]
    tm, tn = min(PROJ_TM, T), PROJ_TN
    return pl.pallas_call(
        _proj_kernel,
        grid=(T // tm, N // tn),
        in_specs=[
            pl.BlockSpec((tm, K), lambda i, j: (i, 0)),
            pl.BlockSpec((K, tn), lambda i, j: (0, j)),
            pl.BlockSpec((1, tn), lambda i, j: (0, j)),
            pl.BlockSpec((K, LANES), lambda i, j: (0, 0)),
            pl.BlockSpec((1, LANES), lambda i, j: (0, 0)),
        ],
        out_specs=[
            pl.BlockSpec((tm, tn), lambda i, j: (i, j)),
            pl.BlockSpec((tm, LANES), lambda i, j: (i, 0)),
        ],
        out_shape=[
            jax.ShapeDtypeStruct((T, N), BF16),
            jax.ShapeDtypeStruct((T, LANES), F32),
        ],
        scratch_shapes=[pltpu.VMEM((tm, K), BF16)],
        compiler_params=_cparams(("parallel", "arbitrary")),
        name="in_projection",
    )(x2d, w_main, b_main, w_gate, b_gate)


def _gate_prep_kernel(g_ref, o_ref):
    g = g_ref[...]
    L = g.shape[1]
    lf = jnp.minimum(g, 0.0) - jnp.log(1.0 + jnp.exp(-jnp.abs(g)))
    r = lax.broadcasted_iota(I32, (L, L), 0)
    c = lax.broadcasted_iota(I32, (L, L), 1)
    tri = (r <= c).astype(F32)
    b = jnp.dot(lf, tri, preferred_element_type=F32, precision=lax.Precision.HIGHEST)
    row = lax.broadcasted_iota(I32, g.shape, 0)
    o_ref[...] = jnp.where(row < A_HEADS, g, b)


def _gate_prep(g_rows, L):
    R, T = g_rows.shape
    return pl.pallas_call(
        _gate_prep_kernel,
        grid=(T // L,),
        in_specs=[pl.BlockSpec((R, L), lambda i: (0, i))],
        out_specs=pl.BlockSpec((R, L), lambda i: (0, i)),
        out_shape=jax.ShapeDtypeStruct((R, T), F32),
        compiler_params=_cparams(("parallel",)),
        name="gate_prep",
    )(g_rows)


def _mlstm_kernel(q_ref, k_ref, v_ref, o_ref, gcol_ref, grow_ref, cw_ref, cb_ref, ng_ref, out_ref,
                  c_ref, n_ref, m_ref, qcar_ref, kcar_ref):
    ci = pl.program_id(1)
    L = q_ref.shape[0]
    dh = A_HEAD_DIM

    @pl.when(ci == 0)
    def _():
        c_ref[...] = jnp.zeros_like(c_ref)
        n_ref[...] = jnp.zeros_like(n_ref)
        m_ref[...] = jnp.zeros_like(m_ref)
        qcar_ref[...] = jnp.zeros_like(qcar_ref)
        kcar_ref[...] = jnp.zeros_like(kcar_ref)

    def conv_silu(u_ref, car_ref, lo, wcol):
        u = u_ref[:, lo:lo + dh].astype(F32)
        ext = jnp.concatenate([car_ref[:, lo:lo + dh], u], axis=0)
        w = cw_ref[:, wcol:wcol + dh]
        y = cb_ref[:, wcol:wcol + dh] + w[3:4, :] * u
        for j in range(CONV_WIDTH - 1):
            off = SUBLANES - (CONV_WIDTH - 1) + j
            y = y + w[j:j + 1, :] * ext[off:off + L, :]
        car_ref[:, lo:lo + dh] = u[L - SUBLANES:, :]
        return y * _sigmoid(y)

    r = lax.broadcasted_iota(I32, (L, L), 0)
    c = lax.broadcasted_iota(I32, (L, L), 1)
    causal = r >= c
    gcol = gcol_ref[...]
    grow = grow_ref[...]

    for h in range(A_HEADS):
        lo = h * dh
        q = conv_silu(q_ref, qcar_ref, lo, lo)
        k = conv_silu(k_ref, kcar_ref, lo, A_WIDTH + lo) * (dh ** -0.5)
        v_b = v_ref[:, lo:lo + dh]
        q_b = q.astype(BF16)
        k_b = k.astype(BF16)

        i_col, b_col = gcol[:, h:h + 1], gcol[:, A_HEADS + h:A_HEADS + h + 1]
        i_row, b_row = grow[h:h + 1, :], grow[A_HEADS + h:A_HEADS + h + 1, :]
        b_last = b_row[:, L - 1:L]
        m_prev = m_ref[h]

        dmat = jnp.where(causal, b_col - b_row + i_row, NEG_BIG)
        inter = b_col + m_prev
        m_t = jnp.maximum(inter, jnp.max(dmat, axis=-1, keepdims=True))
        s = lax.dot_general(q_b, k_b, (((1,), (1,)), ((), ())), preferred_element_type=F32)
        w_intra = jnp.exp(dmat - m_t) * s
        w_inter = jnp.exp(inter - m_t)
        qc = jnp.dot(q_b, c_ref[h].astype(BF16), preferred_element_type=F32)
        num = w_inter * qc + jnp.dot(w_intra.astype(BF16), v_b, preferred_element_type=F32)
        qn = jnp.sum(q * n_ref[h], axis=-1, keepdims=True)
        den = w_inter * qn + jnp.sum(w_intra, axis=-1, keepdims=True)
        hh = num / jnp.maximum(jnp.abs(den), jnp.exp(-m_t))

        g_col = b_last - b_col + i_col
        m_new = jnp.maximum(b_last + m_prev, jnp.max(g_col, axis=0, keepdims=True))
        decay = jnp.exp(b_last + m_prev - m_new)
        kw = jnp.exp(g_col - m_new) * k
        c_ref[h] = decay * c_ref[h] + jnp.dot(kw.T.astype(BF16), v_b, preferred_element_type=F32)
        n_ref[h] = decay * n_ref[h] + jnp.sum(kw, axis=0, keepdims=True)
        m_ref[h] = m_new

        y = _sigmoid(o_ref[:, lo:lo + dh].astype(F32)) * hh
        mu = jnp.mean(y, axis=-1, keepdims=True)
        yc = y - mu
        var = jnp.mean(yc * yc, axis=-1, keepdims=True)
        out_ref[:, lo:lo + dh] = (yc * lax.rsqrt(var + LN_EPS) * ng_ref[:, lo:lo + dh]).astype(out_ref.dtype)


def _mlstm(z_main, gcol, grow, conv_w, conv_b, norm_g, B, S):
    L = min(MLSTM_CHUNK, S)
    nc = S // L
    dh = A_HEAD_DIM
    H = A_HEADS
    W = A_WIDTH
    row = lambda b, c: b * nc + c
    full = lambda shape: pl.BlockSpec(shape, lambda b, c: (0, 0))
    return pl.pallas_call(
        _mlstm_kernel,
        grid=(B, nc),
        in_specs=[
            pl.BlockSpec((L, W), lambda b, c: (row(b, c), COL_QA // W)),
            pl.BlockSpec((L, W), lambda b, c: (row(b, c), COL_KA // W)),
            pl.BlockSpec((L, W), lambda b, c: (row(b, c), COL_VA // W)),
            pl.BlockSpec((L, W), lambda b, c: (row(b, c), COL_OA // W)),
            pl.BlockSpec((L, SUBLANES), lambda b, c: (row(b, c), 0)),
            pl.BlockSpec((SUBLANES, L), lambda b, c: (0, row(b, c))),
            full((CONV_WIDTH, 2 * W)), full((1, 2 * W)), full((1, W)),
        ],
        out_specs=pl.BlockSpec((L, W), lambda b, c: (row(b, c), 0)),
        out_shape=jax.ShapeDtypeStruct((B * S, W), BF16),
        scratch_shapes=[
            pltpu.VMEM((H, dh, dh), F32),
            pltpu.VMEM((H, 1, dh), F32),
            pltpu.VMEM((H, 1, 1), F32),
            pltpu.VMEM((SUBLANES, W), F32),
            pltpu.VMEM((SUBLANES, W), F32),
        ],
        compiler_params=_cparams(("parallel", "arbitrary")),
        name="mlstm",
    )(z_main, z_main, z_main, z_main, gcol, grow, conv_w, conv_b, norm_g)


def _lane_tiles(t):
    return [t[:, i * LANES:(i + 1) * LANES] for i in range(t.shape[1] // LANES)]


def _attn_kernel(slope_ref, q_ref, k_ref, kb_ref, v_ref, lq1_ref, lk1_ref, lq2_ref, lk2_ref, g_ref, o_ref,
                 s_ref, mx_ref, ls_ref, acc_ref, corr_ref, *, lambda_init):
    h = pl.program_id(1)
    qi = pl.program_id(2)
    tq = q_ref.shape[0]
    tk = tq
    slope = slope_ref[h]
    nt = (((1,), (1,)), ((), ()))

    lane = lax.broadcasted_iota(I32, (tq, B_V_DIM), 1)
    q = q_ref[...] * (B_HEAD_DIM ** -0.5)
    zero = jnp.zeros_like(q)
    bias_cols = jnp.where(lane < 2, 1.0, 0.0).astype(BF16)
    qa = (jnp.concatenate([jnp.where(lane < B_HEAD_DIM, q, zero), bias_cols], axis=1),
          jnp.concatenate([jnp.where(lane >= B_HEAD_DIM, q, zero), bias_cols], axis=1))

    def scores(j):
        start = pl.multiple_of(j * tk, tk)
        ka = jnp.concatenate([k_ref[pl.ds(start, tk), :], kb_ref[pl.ds(start, tk), :]], axis=1)
        return [lax.dot_general(qa[n], ka, nt, preferred_element_type=F32) for n in range(2)]

    def keep(j, n, t):
        t = t * LOG2_E
        s_ref[n, j] = t
        mx_ref[n] = functools.reduce(jnp.maximum, _lane_tiles(t), mx_ref[n])

    mx_ref[...] = jnp.full(mx_ref.shape, NEG_BIG, F32)

    def pass_a(j):
        t = scores(j)
        for n in range(2):
            keep(j, n, t[n])

    def pass_a_pair(jj, carry):
        pass_a(2 * jj)
        pass_a(2 * jj + 1)
        return carry
    lax.fori_loop(0, qi // 2, pass_a_pair, 0)

    @pl.when(qi % 2 == 1)
    def _():
        pass_a(qi - 1)

    @pl.when(qi == 0)
    def _():
        r = lax.broadcasted_iota(I32, (tq, tk), 0)
        c = lax.broadcasted_iota(I32, (tq, tk), 1)
        ahead = jnp.where(c <= r, 0.0, (r - c).astype(F32) * (2.0 * slope))
        corr_ref[...] = jnp.where((c // ATTN_CHUNK) <= (r // ATTN_CHUNK), ahead, NEG_BIG)

    t = scores(qi)
    for n in range(2):
        keep(qi, n, t[n] + corr_ref[...])

    for n in range(2):
        m = jnp.max(mx_ref[n], axis=-1, keepdims=True)
        mx_ref[n] = jnp.broadcast_to(m, (tq, LANES))
    ls_ref[...] = jnp.zeros_like(ls_ref)
    acc_ref[...] = jnp.zeros_like(acc_ref)

    def pass_b(j, carry):
        start = pl.multiple_of(j * tk, tk)
        vj = v_ref[pl.ds(start, tk), :]
        for n in range(2):
            mb = mx_ref[n]
            ps = [jnp.exp2(t - mb) for t in _lane_tiles(s_ref[n, j])]
            ls_ref[n] += functools.reduce(jnp.add, ps)
            p = jnp.concatenate(ps, axis=1).astype(BF16)
            acc_ref[n] += jnp.dot(p, vj, preferred_element_type=F32)
        return carry
    lax.fori_loop(0, qi + 1, pass_b, 0)

    lam = (jnp.exp(jnp.sum(lq1_ref[...] * lk1_ref[...], axis=-1, keepdims=True))
           - jnp.exp(jnp.sum(lq2_ref[...] * lk2_ref[...], axis=-1, keepdims=True)) + lambda_init)
    l0 = jnp.sum(ls_ref[0], axis=-1, keepdims=True)
    l1 = jnp.sum(ls_ref[1], axis=-1, keepdims=True)
    o = acc_ref[0] / l0 - lam * (acc_ref[1] / l1)
    ms = jnp.mean(o * o, axis=-1, keepdims=True)
    o_ref[...] = (o * lax.rsqrt(ms + LN_EPS) * g_ref[...] * (1.0 - lambda_init)).astype(o_ref.dtype)


def _diff_attention(z_main, lq1, lk1, lq2, lk2, norm_g, B, S, lambda_init):
    tq = min(ATTN_TQ, S)
    nq = S // tq
    H = B_HEADS
    dv = B_V_DIM
    slopes = 2.0 ** (-8.0 * jnp.arange(1, H + 1, dtype=F32) / H)
    assert S <= 256 * 256
    pos = lax.broadcasted_iota(I32, (H, S, dv), 1)
    col = lax.broadcasted_iota(I32, (H, S, dv), 2)
    within = pos % 256
    kbias = jnp.where(col == 0, within, jnp.where(col == 1, pos - within, 0)).astype(F32)
    kbias = (kbias * slopes[:, None, None]).astype(BF16)
    small = pl.BlockSpec((1, B_HEAD_DIM), lambda b, h, i: (0, 0))
    return pl.pallas_call(
        functools.partial(_attn_kernel, lambda_init=lambda_init),
        grid=(B, H, nq),
        in_specs=[
            pl.BlockSpec(memory_space=pltpu.SMEM),
            pl.BlockSpec((tq, dv), lambda b, h, i: (b * nq + i, COL_QB // dv + h)),
            pl.BlockSpec((S, dv), lambda b, h, i: (b, COL_KB // dv + h)),
            pl.BlockSpec((None, S, dv), lambda b, h, i: (h, 0, 0)),
            pl.BlockSpec((S, dv), lambda b, h, i: (b, COL_VB // dv + h)),
            small, small, small, small,
            pl.BlockSpec((1, dv), lambda b, h, i: (0, 0)),
        ],
        out_specs=pl.BlockSpec((tq, dv), lambda b, h, i: (b * nq + i, h)),
        out_shape=jax.ShapeDtypeStruct((B * S, B_WIDTH), BF16),
        scratch_shapes=[
            pltpu.VMEM((2, nq, tq, tq), F32),
            pltpu.VMEM((2, tq, LANES), F32),
            pltpu.VMEM((2, tq, LANES), F32),
            pltpu.VMEM((2, tq, dv), F32),
            pltpu.VMEM((tq, tq), F32),
        ],
        compiler_params=_cparams(("parallel", "parallel", "arbitrary")),
        name="diff_attention",
    )(slopes, z_main, z_main, kbias, z_main, lq1, lk1, lq2, lk2, norm_g)


def _mix_kernel(ha_ref, ob_ref, ga_ref, gb_ref, x_ref, wa_ref, wb_ref, wo_ref, g1_ref, b1_ref, wr_ref, br_ref,
                x1_ref, x1p_ref, ri_ref, rw_ref, cnt_ref, carry_ref):
    i = pl.program_id(0)
    tm = x_ref.shape[0]

    @pl.when(i == 0)
    def _():
        carry_ref[...] = jnp.zeros_like(carry_ref)

    ya = jnp.dot(ha_ref[...], wa_ref[...], preferred_element_type=F32)
    yb = jnp.dot(ob_ref[...], wb_ref[...], preferred_element_type=F32)
    merged = _sigmoid(ga_ref[...].astype(F32)) * ya + _sigmoid(gb_ref[...].astype(F32)) * yb
    mix = jnp.dot(merged.astype(BF16), wo_ref[...], preferred_element_type=F32)
    pre = DEEPNORM_ALPHA * x_ref[...] + mix
    mu = jnp.mean(pre, axis=-1, keepdims=True)
    pc = pre - mu
    var = jnp.mean(pc * pc, axis=-1, keepdims=True)
    x1 = pc * lax.rsqrt(var + LN_EPS) * g1_ref[...] + b1_ref[...]
    x1_ref[...] = x1
    x1p_ref[...] = _pack_halves(x1)

    logits = _dot_3pass(x1, wr_ref[...]) + br_ref[...]
    lt = logits.T
    row8 = lax.broadcasted_iota(I32, (SUBLANES, tm), 0).astype(F32)
    gl = jnp.where(row8 < N_GROUPS, lt[0:SUBLANES, :], NEG_BIG)
    gmax = jnp.max(gl, axis=0, keepdims=True)
    gsel = jnp.min(jnp.where(gl == gmax, row8, float(SUBLANES)), axis=0, keepdims=True)
    gprob = 1.0 / jnp.sum(jnp.exp(gl - gmax), axis=0, keepdims=True)
    ing = lt[SUBLANES:2 * SUBLANES, :]
    for g in range(1, N_GROUPS):
        ing = jnp.where(gsel == g, lt[(g + 1) * SUBLANES:(g + 2) * SUBLANES, :], ing)
    v0 = jnp.max(ing, axis=0, keepdims=True)
    i0 = jnp.min(jnp.where(ing == v0, row8, float(SUBLANES)), axis=0, keepdims=True)
    ing2 = jnp.where(row8 == i0, -jnp.inf, ing)
    v1 = jnp.max(ing2, axis=0, keepdims=True)
    i1 = jnp.min(jnp.where(ing2 == v1, row8, float(SUBLANES)), axis=0, keepdims=True)
    ex = jnp.exp(v1 - v0)
    inv = 1.0 / (1.0 + ex)
    w0 = gprob * inv
    w1 = gprob * (ex * inv)
    e0 = (gsel * EXPERTS_PER_GROUP + i0).astype(I32)
    e1 = (gsel * EXPERTS_PER_GROUP + i1).astype(I32)

    rowe = lax.broadcasted_iota(I32, (N_EXPERTS, tm), 0)
    is0 = rowe == e0
    is1 = rowe == e1
    oh = jnp.where(is0, 1.0, 0.0) + jnp.where(is1, 1.0, 0.0)
    tr = lax.broadcasted_iota(I32, (tm, tm), 0)
    tc = lax.broadcasted_iota(I32, (tm, tm), 1)
    su = jnp.where(tr < tc, 1.0, 0.0).astype(BF16)
    tot = jnp.dot(oh.astype(BF16), su, preferred_element_type=F32) + carry_ref[...]
    rank0 = jnp.sum(jnp.where(is0, tot, 0.0), axis=0, keepdims=True)
    rank1 = jnp.sum(jnp.where(is1, tot, 0.0), axis=0, keepdims=True)
    carry = carry_ref[...] + jnp.sum(oh, axis=1, keepdims=True)
    carry_ref[...] = carry
    cnt_ref[...] = jnp.broadcast_to(carry, cnt_ref.shape).astype(I32)

    zi = jnp.zeros((1, tm), I32)
    ri_ref[...] = jnp.concatenate([e0, e1, rank0.astype(I32), rank1.astype(I32), zi, zi, zi, zi], axis=0)
    zf = jnp.zeros((1, tm), F32)
    rw_ref[...] = jnp.concatenate([w0, w1, zf, zf, zf, zf, zf, zf], axis=0)


def _mix(h_a, o_b, z_main, x2d, w_a, w_b, w_out, ln_g, ln_b, w_r, b_r):
    T, D = x2d.shape
    tm = min(MIX_TM, T)
    const = lambda shape: pl.BlockSpec(shape, lambda i: (0, 0), pipeline_mode=pl.Buffered(1))
    return pl.pallas_call(
        _mix_kernel,
        grid=(T // tm,),
        in_specs=[
            pl.BlockSpec((tm, A_WIDTH), lambda i: (i, 0)),
            pl.BlockSpec((tm, B_WIDTH), lambda i: (i, 0)),
            pl.BlockSpec((tm, D), lambda i: (i, COL_GA // D)),
            pl.BlockSpec((tm, D), lambda i: (i, COL_GB // D)),
            pl.BlockSpec((tm, D), lambda i: (i, 0)),
            const((A_WIDTH, D)), const((B_WIDTH, D)), const((D, D)),
            const((1, D)), const((1, D)), const((D, LANES)), const((1, LANES)),
        ],
        out_specs=[
            pl.BlockSpec((tm, D), lambda i: (i, 0)),
            pl.BlockSpec((tm, D // 2), lambda i: (i, 0)),
            pl.BlockSpec((SUBLANES, tm), lambda i: (0, i)),
            pl.BlockSpec((SUBLANES, tm), lambda i: (0, i)),
            pl.BlockSpec((N_EXPERTS, LANES), lambda i: (0, 0)),
        ],
        out_shape=[
            jax.ShapeDtypeStruct((T, D), F32),
            jax.ShapeDtypeStruct((T, D // 2), U32),
            jax.ShapeDtypeStruct((SUBLANES, T), I32),
            jax.ShapeDtypeStruct((SUBLANES, T), F32),
            jax.ShapeDtypeStruct((N_EXPERTS, LANES), I32),
        ],
        scratch_shapes=[pltpu.VMEM((N_EXPERTS, 1), F32)],
        compiler_params=_cparams(("arbitrary",)),
        name="mix_ln1_router",
    )(h_a, o_b, z_main, z_main, x2d, w_a, w_b, w_out, ln_g, ln_b, w_r, b_r)


def _row_copy(src_ref, src_row, dst_ref, dst_row, sem):
    return pltpu.make_async_copy(src_ref.at[pl.ds(src_row, 1), :], dst_ref.at[pl.ds(dst_row, 1), :], sem)


def _rows_wait(src_ref, dst_ref, rows, sem):
    pltpu.make_async_copy(src_ref.at[pl.ds(0, rows), :], dst_ref.at[pl.ds(0, rows), :], sem).wait()


def _dispatch_kernel(cnt_ref, ri_ref, x1_ref, xpad_ref, pstart_ref, bexp_ref, nused_ref,
                     pst_ref, zero_ref, sem, zsem, *, tb, n_blocks, tm):
    i = pl.program_id(0)
    last = pl.num_programs(0) - 1

    @pl.when(i == 0)
    def _():
        def fill(b, carry):
            bexp_ref[b] = 0
            return carry
        lax.fori_loop(0, n_blocks, fill, 0)

        def per_expert(e, blk):
            pst_ref[e] = blk * tb
            pstart_ref[e] = blk * tb
            nb = (cnt_ref[e] + (tb - 1)) // tb

            def mark(b, carry):
                bexp_ref[blk + b] = e
                return carry
            lax.fori_loop(0, nb, mark, 0)
            return blk + nb
        nused = lax.fori_loop(0, N_EXPERTS, per_expert, 0)
        nused_ref[0] = nused
        pst_ref[N_EXPERTS] = nused

    def issue(t, carry):
        for k in range(TOP_K):
            pos = pst_ref[ri_ref[k, t]] + ri_ref[TOP_K + k, t]
            _row_copy(x1_ref, t, xpad_ref, pos, sem).start()
        return carry
    lax.fori_loop(0, tm, issue, 0, unroll=8)
    for k in range(TOP_K):
        _rows_wait(x1_ref, xpad_ref, tm, sem)

    @pl.when(i == last)
    def _():
        zero_ref[...] = jnp.zeros_like(zero_ref)

        def per_expert(e, carry):
            cnt = cnt_ref[e]
            first = pst_ref[e] + cnt
            npad = ((cnt + (tb - 1)) // tb) * tb - cnt

            def start(p, c2):
                _row_copy(zero_ref, 0, xpad_ref, first + p, zsem).start()
                return c2
            lax.fori_loop(0, npad, start, 0)

            def wait(p, c2):
                _row_copy(zero_ref, 0, xpad_ref, 0, zsem).wait()
                return c2
            lax.fori_loop(0, npad, wait, 0)
            return carry
        lax.fori_loop(0, N_EXPERTS, per_expert, 0)

        def block_copy(b):
            return pltpu.make_async_copy(zero_ref, xpad_ref.at[pl.ds(pl.multiple_of(b * tb, tb), tb), :], zsem)

        def start_block(b, carry):
            block_copy(b).start()
            return carry
        lax.fori_loop(pst_ref[N_EXPERTS], n_blocks, start_block, 0)

        def wait_block(b, carry):
            block_copy(b).wait()
            return carry
        lax.fori_loop(pst_ref[N_EXPERTS], n_blocks, wait_block, 0)


def _dispatch(counts, route_i, x1, tb, n_blocks):
    T, D = x1.shape
    tm = min(DISPATCH_TM, T)
    smem_full = pl.BlockSpec(memory_space=pltpu.SMEM)
    return pl.pallas_call(
        functools.partial(_dispatch_kernel, tb=tb, n_blocks=n_blocks, tm=tm),
        grid=(T // tm,),
        in_specs=[
            smem_full,
            pl.BlockSpec((SUBLANES, tm), lambda i: (0, i), memory_space=pltpu.SMEM),
            pl.BlockSpec((tm, D), lambda i: (i, 0)),
        ],
        out_specs=[
            pl.BlockSpec(memory_space=pl.ANY),
            smem_full, smem_full, smem_full,
        ],
        out_shape=[
            jax.ShapeDtypeStruct((n_blocks * tb, D), x1.dtype),
            jax.ShapeDtypeStruct((N_EXPERTS,), I32),
            jax.ShapeDtypeStruct((n_blocks,), I32),
            jax.ShapeDtypeStruct((1,), I32),
        ],
        scratch_shapes=[
            pltpu.SMEM((N_EXPERTS + 1,), I32),
            pltpu.VMEM((tb, D), x1.dtype),
            pltpu.SemaphoreType.DMA(()),
            pltpu.SemaphoreType.DMA(()),
        ],
        compiler_params=_cparams(("arbitrary",)),
        name="moe_dispatch",
    )(counts, route_i, x1)


def _expert_kernel(bexp_ref, nused_ref, x_ref, wg_ref, wu_ref, wd_ref, y_ref, wgb_ref, wub_ref, wdb_ref):
    b = pl.program_id(0)
    used = b < nused_ref[0]
    new_expert = jnp.logical_or(b == 0, bexp_ref[b] != bexp_ref[jnp.maximum(b - 1, 0)])

    @pl.when(jnp.logical_and(used, new_expert))
    def _():
        wgb_ref[...] = wg_ref[...].astype(BF16)
        wub_ref[...] = wu_ref[...].astype(BF16)
        wdb_ref[...] = wd_ref[...].astype(BF16)

    @pl.when(used)
    def _():
        half = x_ref.shape[1]
        x_lo, x_hi = (t.astype(BF16) for t in _unpack_halves(x_ref[...]))
        dot = functools.partial(jnp.dot, preferred_element_type=F32)
        gate = dot(x_lo, wgb_ref[:half, :]) + dot(x_hi, wgb_ref[half:, :])
        up = dot(x_lo, wub_ref[:half, :]) + dot(x_hi, wub_ref[half:, :])
        hid = (gate * _sigmoid(gate) * up).astype(BF16)
        y_ref[...] = _pack_halves(dot(hid, wdb_ref[...]))

    @pl.when(jnp.logical_not(used))
    def _():
        y_ref[...] = jnp.zeros_like(y_ref)


def _experts(bexp, nused, x_pad, w_gate, w_up, w_down, tb):
    P, Dh = x_pad.shape
    D = 2 * Dh
    n_blocks = P // tb
    F = w_gate.shape[-1]
    blk = lambda b, be, nu: jnp.maximum(jnp.minimum(b, nu[0] - 1), 0)
    exp = lambda b, be, nu: be[blk(b, be, nu)]
    return pl.pallas_call(
        _expert_kernel,
        grid_spec=pltpu.PrefetchScalarGridSpec(
            num_scalar_prefetch=2,
            grid=(n_blocks,),
            in_specs=[
                pl.BlockSpec((tb, Dh), lambda b, be, nu: (blk(b, be, nu), 0)),
                pl.BlockSpec((None, D, F), lambda b, be, nu: (exp(b, be, nu), 0, 0)),
                pl.BlockSpec((None, D, F), lambda b, be, nu: (exp(b, be, nu), 0, 0)),
                pl.BlockSpec((None, F, D), lambda b, be, nu: (exp(b, be, nu), 0, 0)),
            ],
            out_specs=pl.BlockSpec((tb, Dh), lambda b, be, nu: (b, 0)),
            scratch_shapes=[pltpu.VMEM((D, F), BF16), pltpu.VMEM((D, F), BF16), pltpu.VMEM((F, D), BF16)],
        ),
        out_shape=jax.ShapeDtypeStruct((P, Dh), U32),
        compiler_params=_cparams(("arbitrary",)),
        name="moe_experts",
    )(bexp, nused, x_pad, w_gate, w_up, w_down)


def _combine_kernel(pstart_ref, ri_ref, rin_ref, x1_ref, rw_ref, g_ref, b_ref, ypad_ref, o_ref, ybuf_ref, sem):
    i = pl.program_id(0)
    tm = x1_ref.shape[0]
    slot = i % 2

    def gather(route_ref, s):
        def issue(t, carry):
            for k in range(TOP_K):
                pos = pstart_ref[route_ref[k, t]] + route_ref[TOP_K + k, t]
                _row_copy(ypad_ref, pos, ybuf_ref.at[s, k], t, sem.at[s]).start()
            return carry
        lax.fori_loop(0, tm, issue, 0, unroll=8)

    @pl.when(i == 0)
    def _():
        gather(ri_ref, 0)

    @pl.when(i + 1 < pl.num_programs(0))
    def _():
        gather(rin_ref, 1 - slot)

    for k in range(TOP_K):
        _rows_wait(ypad_ref, ybuf_ref.at[slot, k], tm, sem.at[slot])

    rw = rw_ref[...]
    y0 = _unpack_halves(ybuf_ref[slot, 0])
    y1 = _unpack_halves(ybuf_ref[slot, 1])
    ffn = jnp.concatenate([y0[h] * rw[:, 0:1] + y1[h] * rw[:, 1:2] for h in range(2)], axis=1)
    pre = DEEPNORM_ALPHA * x1_ref[...] + ffn
    mu = jnp.mean(pre, axis=-1, keepdims=True)
    pc = pre - mu
    var = jnp.mean(pc * pc, axis=-1, keepdims=True)
    o_ref[...] = pc * lax.rsqrt(var + LN_EPS) * g_ref[...] + b_ref[...]


def _combine(pstart, route_i, x1, rw_col, ln_g, ln_b, y_pad):
    T, D = x1.shape
    tm = min(COMBINE_TM, T)
    n = T // tm
    return pl.pallas_call(
        _combine_kernel,
        grid=(n,),
        in_specs=[
            pl.BlockSpec(memory_space=pltpu.SMEM),
            pl.BlockSpec((SUBLANES, tm), lambda i: (0, i), memory_space=pltpu.SMEM),
            pl.BlockSpec((SUBLANES, tm), lambda i: (0, jnp.minimum(i + 1, n - 1)), memory_space=pltpu.SMEM),
            pl.BlockSpec((tm, D), lambda i: (i, 0)),
            pl.BlockSpec((tm, SUBLANES), lambda i: (i, 0)),
            pl.BlockSpec((1, D), lambda i: (0, 0)),
            pl.BlockSpec((1, D), lambda i: (0, 0)),
            pl.BlockSpec(memory_space=pl.ANY),
        ],
        out_specs=pl.BlockSpec((tm, D), lambda i: (i, 0)),
        out_shape=jax.ShapeDtypeStruct((T, D), F32),
        scratch_shapes=[
            pltpu.VMEM((2, TOP_K, tm, D // 2), U32),
            pltpu.SemaphoreType.DMA((2,)),
        ],
        compiler_params=_cparams(("arbitrary",)),
        name="moe_combine_ln2",
    )(pstart, route_i, route_i, x1, rw_col, ln_g, ln_b, y_pad)


def _rearrange_in_proj(w_in, b_in):
    offs = [0]
    for cwidth in (A_WIDTH, A_WIDTH, A_WIDTH, A_WIDTH, A_HEADS, A_HEADS, B_WIDTH, B_WIDTH, B_WIDTH, D_MODEL, D_MODEL):
        offs.append(offs[-1] + cwidth)
    main_cols = [(9, 11), (0, 4), (6, 9)]
    w_main = jnp.concatenate([w_in[:, offs[a]:offs[b]] for a, b in main_cols], axis=1).astype(BF16)
    b_main = jnp.concatenate([b_in[offs[a]:offs[b]] for a, b in main_cols])[None, :]
    ngate = 2 * A_HEADS
    w_gate = jnp.pad(w_in[:, offs[4]:offs[6]], ((0, 0), (0, LANES - ngate)))
    b_gate = jnp.pad(b_in[offs[4]:offs[6]], (0, LANES - ngate))[None, :]
    return w_main, b_main, w_gate, b_gate


def _layer(x, w_in, b_in, conv_w, conv_b, norm_a_g, lq1, lk1, lq2, lk2, norm_b_g, w_a, w_b, w_out,
           ln1_g, ln1_b, w_grp, b_grp, w_exp, b_exp, w_gate, w_up, w_down, ln2_g, ln2_b, lambda_init):
    B, S, D = x.shape
    T = B * S
    x2d = x.reshape(T, D)

    w_main, b_main, w_g, b_g = _rearrange_in_proj(w_in, b_in)
    z_main, z_gate = _in_projection(x2d, w_main, b_main, w_g, b_g)

    L = min(MLSTM_CHUNK, S)
    gp = _gate_prep(z_gate[:, :2 * A_HEADS].T, L)
    h_a = _mlstm(z_main, gp.T, gp, conv_w, conv_b[None, :], norm_a_g[None, :], B, S)
    o_b = _diff_attention(z_main, lq1[None, :], lk1[None, :], lq2[None, :], lk2[None, :],
                          norm_b_g[None, :], B, S, lambda_init)

    w_r = jnp.zeros((D, LANES), F32).at[:, :N_GROUPS].set(w_grp).at[:, SUBLANES:SUBLANES + N_EXPERTS].set(w_exp)
    b_r = jnp.zeros((LANES,), F32).at[:N_GROUPS].set(b_grp).at[SUBLANES:SUBLANES + N_EXPERTS].set(b_exp)[None, :]
    x1, x1_packed, route_i, route_w, counts = _mix(h_a, o_b, z_main, x2d, w_a.astype(BF16), w_b.astype(BF16),
                                                   w_out.astype(BF16), ln1_g[None, :], ln1_b[None, :], w_r, b_r)

    tb = MOE_TB
    n_blocks = (T * TOP_K) // tb + N_EXPERTS
    x_pad, pstart, bexp, nused = _dispatch(counts[:, 0], route_i, x1_packed, tb, n_blocks)
    y_pad = _experts(bexp, nused, x_pad, w_gate, w_up, w_down, tb)
    out = _combine(pstart, route_i, x1, route_w.T, ln2_g[None, :], ln2_b[None, :], y_pad)
    return out.reshape(B, S, D)


def kernel(x, w_in, b_in, conv_w, conv_b, mlstm_norm_g, lambda_q1, lambda_k1, lambda_q2, lambda_k2,
           diff_norm_g, w_a, w_b, w_out, ln1_g, ln1_b, w_grp, b_grp, w_exp, b_exp,
           w_gate, w_up, w_down, ln2_g, ln2_b):
    for l in range(DEPTH):
        lambda_init = 0.8 - 0.6 * math.exp(-0.3 * l)
        x = _layer(x, w_in[l], b_in[l], conv_w[l], conv_b[l], mlstm_norm_g[l], lambda_q1[l], lambda_k1[l],
                   lambda_q2[l], lambda_k2[l], diff_norm_g[l], w_a[l], w_b[l], w_out[l], ln1_g[l], ln1_b[l],
                   w_grp[l], b_grp[l], w_exp[l], b_exp[l], w_gate[l], w_up[l], w_down[l], ln2_g[l], ln2_b[l],
                   lambda_init)
    return x
```

```python
import functools
import math

import jax
import jax.numpy as jnp
from jax import lax
from jax.experimental import pallas as pl
from jax.experimental.pallas import tpu as pltpu

F32 = jnp.float32
BF16 = jnp.bfloat16
I32 = jnp.int32

D_MODEL = 2048
A_HEADS = 4
A_HEAD_DIM = 256
A_WIDTH = A_HEADS * A_HEAD_DIM
CONV_WIDTH = 4
B_HEADS = 8
B_HEAD_DIM = 64
B_V_DIM = 2 * B_HEAD_DIM
B_WIDTH = B_HEADS * B_V_DIM
ATTN_CHUNK = 64
N_GROUPS = 4
EXPERTS_PER_GROUP = 8
N_EXPERTS = N_GROUPS * EXPERTS_PER_GROUP
TOP_K = 2
D_EXPERT = 512
DEPTH = 1
DEEPNORM_ALPHA = (2 * DEPTH) ** 0.25
LN_EPS = 1e-5
NEG_BIG = -1e30
LOG2_E = 1.4426950408889634

LANES = 128
SUBLANES = 8
VMEM_LIMIT_BYTES = 56 * 1024 * 1024

COL_GA = 0
COL_GB = COL_GA + D_MODEL
COL_QA = COL_GB + D_MODEL
COL_KA = COL_QA + A_WIDTH
COL_VA = COL_KA + A_WIDTH
COL_OA = COL_VA + A_WIDTH
COL_QB = COL_OA + A_WIDTH
COL_KB = COL_QB + B_WIDTH
COL_VB = COL_KB + B_WIDTH
N_MAIN = COL_VB + B_WIDTH

PROJ_TM, PROJ_TN = 1024, 1024
MLSTM_CHUNK = 256
ATTN_TQ = 512
MIX_TM = 256
MOE_TB = 256
DISPATCH_TM = 256
COMBINE_TM = 256


def _cparams(sem, vmem=VMEM_LIMIT_BYTES):
    return pltpu.CompilerParams(dimension_semantics=sem, vmem_limit_bytes=vmem)


def _sigmoid(x):
    return 1.0 / (1.0 + jnp.exp(-x))


U32 = jnp.uint32


def _pack_halves(x):
    n = x.shape[1] // 2
    lo = lax.bitcast_convert_type(x[:, :n].astype(BF16).astype(F32), U32)
    hi = lax.bitcast_convert_type(x[:, n:].astype(BF16).astype(F32), U32)
    return (lo >> 16) | hi


def _unpack_halves(u):
    lo = lax.bitcast_convert_type(u << 16, F32)
    hi = lax.bitcast_convert_type(u & jnp.uint32(0xFFFF0000), F32)
    return lo, hi


def _split_bf16(a):
    hi = a.astype(BF16)
    return hi, (a - hi.astype(F32)).astype(BF16)


def _dot_3pass(a, b):
    a_hi, a_lo = _split_bf16(a)
    b_hi, b_lo = _split_bf16(b)
    dot = functools.partial(jnp.dot, preferred_element_type=F32)
    return dot(a_hi, b_hi) + (dot(a_hi, b_lo) + dot(a_lo, b_hi))


def _proj_kernel(x_ref, w_ref, b_ref, wg_ref, bg_ref, z_ref, zg_ref, xb_ref):
    @pl.when(pl.program_id(1) == 0)
    def _():
        x = x_ref[...]
        xb_ref[...] = x.astype(BF16)
        zg_ref[...] = _dot_3pass(x, wg_ref[...]) + bg_ref[...]

    acc = jnp.dot(xb_ref[...], w_ref[...], preferred_element_type=F32)
    z_ref[...] = (acc + b_ref[...]).astype(z_ref.dtype)


def _in_projection(x2d, w_main, b_main, w_gate, b_gate):
    T, K = x2d.shape
    N = w_main.shape[1]
    tm, tn = min(PROJ_TM, T), PROJ_TN
    return pl.pallas_call(
        _proj_kernel,
        grid=(T // tm, N // tn),
        in_specs=[
            pl.BlockSpec((tm, K), lambda i, j: (i, 0)),
            pl.BlockSpec((K, tn), lambda i, j: (0, j)),
            pl.BlockSpec((1, tn), lambda i, j: (0, j)),
            pl.BlockSpec((K, LANES), lambda i, j: (0, 0)),
            pl.BlockSpec((1, LANES), lambda i, j: (0, 0)),
        ],
        out_specs=[
            pl.BlockSpec((tm, tn), lambda i, j: (i, j)),
            pl.BlockSpec((tm, LANES), lambda i, j: (i, 0)),
        ],
        out_shape=[
            jax.ShapeDtypeStruct((T, N), BF16),
            jax.ShapeDtypeStruct((T, LANES), F32),
        ],
        scratch_shapes=[pltpu.VMEM((tm, K), BF16)],
        compiler_params=_cparams(("parallel", "arbitrary")),
        name="in_projection",
    )(x2d, w_main, b_main, w_gate, b_gate)


def _gate_prep_kernel(g_ref, o_ref):
    g = g_ref[...]
    L = g.shape[1]
    lf = jnp.minimum(g, 0.0) - jnp.log(1.0 + jnp.exp(-jnp.abs(g)))
    r = lax.broadcasted_iota(I32, (L, L), 0)
    c = lax.broadcasted_iota(I32, (L, L), 1)
    tri = (r <= c).astype(F32)
    b = jnp.dot(lf, tri, preferred_element_type=F32, precision=lax.Precision.HIGHEST)
    row = lax.broadcasted_iota(I32, g.shape, 0)
    o_ref[...] = jnp.where(row < A_HEADS, g, b)


def _gate_prep(g_rows, L):
    R, T = g_rows.shape
    return pl.pallas_call(
        _gate_prep_kernel,
        grid=(T // L,),
        in_specs=[pl.BlockSpec((R, L), lambda i: (0, i))],
        out_specs=pl.BlockSpec((R, L), lambda i: (0, i)),
        out_shape=jax.ShapeDtypeStruct((R, T), F32),
        compiler_params=_cparams(("parallel",)),
        name="gate_prep",
    )(g_rows)


def _mlstm_kernel(q_ref, k_ref, v_ref, o_ref, gcol_ref, grow_ref, cw_ref, cb_ref, ng_ref, out_ref,
                  c_ref, n_ref, m_ref, qcar_ref, kcar_ref):
    ci = pl.program_id(1)
    L = q_ref.shape[0]
    dh = A_HEAD_DIM

    @pl.when(ci == 0)
    def _():
        c_ref[...] = jnp.zeros_like(c_ref)
        n_ref[...] = jnp.zeros_like(n_ref)
        m_ref[...] = jnp.zeros_like(m_ref)
        qcar_ref[...] = jnp.zeros_like(qcar_ref)
        kcar_ref[...] = jnp.zeros_like(kcar_ref)

    def conv_silu(u_ref, car_ref, lo, wcol):
        u = u_ref[:, lo:lo + dh].astype(F32)
        ext = jnp.concatenate([car_ref[:, lo:lo + dh], u], axis=0)
        w = cw_ref[:, wcol:wcol + dh]
        y = cb_ref[:, wcol:wcol + dh] + w[3:4, :] * u
        for j in range(CONV_WIDTH - 1):
            off = SUBLANES - (CONV_WIDTH - 1) + j
            y = y + w[j:j + 1, :] * ext[off:off + L, :]
        car_ref[:, lo:lo + dh] = u[L - SUBLANES:, :]
        return y * _sigmoid(y)

    r = lax.broadcasted_iota(I32, (L, L), 0)
    c = lax.broadcasted_iota(I32, (L, L), 1)
    causal = r >= c
    gcol = gcol_ref[...]
    grow = grow_ref[...]

    for h in range(A_HEADS):
        lo = h * dh
        q = conv_silu(q_ref, qcar_ref, lo, lo)
        k = conv_silu(k_ref, kcar_ref, lo, A_WIDTH + lo) * (dh ** -0.5)
        v_b = v_ref[:, lo:lo + dh]
        q_b = q.astype(BF16)
        k_b = k.astype(BF16)

        i_col, b_col = gcol[:, h:h + 1], gcol[:, A_HEADS + h:A_HEADS + h + 1]
        i_row, b_row = grow[h:h + 1, :], grow[A_HEADS + h:A_HEADS + h + 1, :]
        b_last = b_row[:, L - 1:L]
        m_prev = m_ref[h]

        dmat = jnp.where(causal, b_col - b_row + i_row, NEG_BIG)
        inter = b_col + m_prev
        m_t = jnp.maximum(inter, jnp.max(dmat, axis=-1, keepdims=True))
        s = lax.dot_general(q_b, k_b, (((1,), (1,)), ((), ())), preferred_element_type=F32)
        w_intra = jnp.exp(dmat - m_t) * s
        w_inter = jnp.exp(inter - m_t)
        qc = jnp.dot(q_b, c_ref[h].astype(BF16), preferred_element_type=F32)
        num = w_inter * qc + jnp.dot(w_intra.astype(BF16), v_b, preferred_element_type=F32)
        qn = jnp.sum(q * n_ref[h], axis=-1, keepdims=True)
        den = w_inter * qn + jnp.sum(w_intra, axis=-1, keepdims=True)
        hh = num / jnp.maximum(jnp.abs(den), jnp.exp(-m_t))

        g_col = b_last - b_col + i_col
        m_new = jnp.maximum(b_last + m_prev, jnp.max(g_col, axis=0, keepdims=True))
        decay = jnp.exp(b_last + m_prev - m_new)
        kw = jnp.exp(g_col - m_new) * k
        c_ref[h] = decay * c_ref[h] + jnp.dot(kw.T.astype(BF16), v_b, preferred_element_type=F32)
        n_ref[h] = decay * n_ref[h] + jnp.sum(kw, axis=0, keepdims=True)
        m_ref[h] = m_new

        y = _sigmoid(o_ref[:, lo:lo + dh].astype(F32)) * hh
        mu = jnp.mean(y, axis=-1, keepdims=True)
        yc = y - mu
        var = jnp.mean(yc * yc, axis=-1, keepdims=True)
        out_ref[:, lo:lo + dh] = (yc * lax.rsqrt(var + LN_EPS) * ng_ref[:, lo:lo + dh]).astype(out_ref.dtype)


def _mlstm(z_main, gcol, grow, conv_w, conv_b, norm_g, B, S):
    L = min(MLSTM_CHUNK, S)
    nc = S // L
    dh = A_HEAD_DIM
    H = A_HEADS
    W = A_WIDTH
    row = lambda b, c: b * nc + c
    full = lambda shape: pl.BlockSpec(shape, lambda b, c: (0, 0))
    return pl.pallas_call(
        _mlstm_kernel,
        grid=(B, nc),
        in_specs=[
            pl.BlockSpec((L, W), lambda b, c: (row(b, c), COL_QA // W)),
            pl.BlockSpec((L, W), lambda b, c: (row(b, c), COL_KA // W)),
            pl.BlockSpec((L, W), lambda b, c: (row(b, c), COL_VA // W)),
            pl.BlockSpec((L, W), lambda b, c: (row(b, c), COL_OA // W)),
            pl.BlockSpec((L, SUBLANES), lambda b, c: (row(b, c), 0)),
            pl.BlockSpec((SUBLANES, L), lambda b, c: (0, row(b, c))),
            full((CONV_WIDTH, 2 * W)), full((1, 2 * W)), full((1, W)),
        ],
        out_specs=pl.BlockSpec((L, W), lambda b, c: (row(b, c), 0)),
        out_shape=jax.ShapeDtypeStruct((B * S, W), BF16),
        scratch_shapes=[
            pltpu.VMEM((H, dh, dh), F32),
            pltpu.VMEM((H, 1, dh), F32),
            pltpu.VMEM((H, 1, 1), F32),
            pltpu.VMEM((SUBLANES, W), F32),
            pltpu.VMEM((SUBLANES, W), F32),
        ],
        compiler_params=_cparams(("parallel", "arbitrary")),
        name="mlstm",
    )(z_main, z_main, z_main, z_main, gcol, grow, conv_w, conv_b, norm_g)


def _lane_tiles(t):
    return [t[:, i * LANES:(i + 1) * LANES] for i in range(t.shape[1] // LANES)]


def _attn_kernel(slope_ref, q_ref, k_ref, kb_ref, v_ref, lq1_ref, lk1_ref, lq2_ref, lk2_ref, g_ref, o_ref,
                 s_ref, mx_ref, ls_ref, acc_ref, corr_ref, *, lambda_init):
    h = pl.program_id(1)
    qi = pl.program_id(2)
    tq = q_ref.shape[0]
    tk = tq
    slope = slope_ref[h]
    nt = (((1,), (1,)), ((), ()))

    lane = lax.broadcasted_iota(I32, (tq, B_V_DIM), 1)
    q = q_ref[...] * (B_HEAD_DIM ** -0.5)
    zero = jnp.zeros_like(q)
    bias_cols = jnp.where(lane < 2, 1.0, 0.0).astype(BF16)
    qa = (jnp.concatenate([jnp.where(lane < B_HEAD_DIM, q, zero), bias_cols], axis=1),
          jnp.concatenate([jnp.where(lane >= B_HEAD_DIM, q, zero), bias_cols], axis=1))

    def scores(j):
        start = pl.multiple_of(j * tk, tk)
        ka = jnp.concatenate([k_ref[pl.ds(start, tk), :], kb_ref[pl.ds(start, tk), :]], axis=1)
        return [lax.dot_general(qa[n], ka, nt, preferred_element_type=F32) for n in range(2)]

    def keep(j, n, t):
        t = t * LOG2_E
        s_ref[n, j] = t
        mx_ref[n] = functools.reduce(jnp.maximum, _lane_tiles(t), mx_ref[n])

    mx_ref[...] = jnp.full(mx_ref.shape, NEG_BIG, F32)

    def pass_a(j):
        t = scores(j)
        for n in range(2):
            keep(j, n, t[n])

    def pass_a_pair(jj, carry):
        pass_a(2 * jj)
        pass_a(2 * jj + 1)
        return carry
    lax.fori_loop(0, qi // 2, pass_a_pair, 0)

    @pl.when(qi % 2 == 1)
    def _():
        pass_a(qi - 1)

    @pl.when(qi == 0)
    def _():
        r = lax.broadcasted_iota(I32, (tq, tk), 0)
        c = lax.broadcasted_iota(I32, (tq, tk), 1)
        ahead = jnp.where(c <= r, 0.0, (r - c).astype(F32) * (2.0 * slope))
        corr_ref[...] = jnp.where((c // ATTN_CHUNK) <= (r // ATTN_CHUNK), ahead, NEG_BIG)

    t = scores(qi)
    for n in range(2):
        keep(qi, n, t[n] + corr_ref[...])

    for n in range(2):
        m = jnp.max(mx_ref[n], axis=-1, keepdims=True)
        mx_ref[n] = jnp.broadcast_to(m, (tq, LANES))
    ls_ref[...] = jnp.zeros_like(ls_ref)
    acc_ref[...] = jnp.zeros_like(acc_ref)

    def pass_b(j, carry):
        start = pl.multiple_of(j * tk, tk)
        vj = v_ref[pl.ds(start, tk), :]
        for n in range(2):
            mb = mx_ref[n]
            ps = [jnp.exp2(t - mb) for t in _lane_tiles(s_ref[n, j])]
            ls_ref[n] += functools.reduce(jnp.add, ps)
            p = jnp.concatenate(ps, axis=1).astype(BF16)
            acc_ref[n] += jnp.dot(p, vj, preferred_element_type=F32)
        return carry
    lax.fori_loop(0, qi + 1, pass_b, 0)

    lam = (jnp.exp(jnp.sum(lq1_ref[...] * lk1_ref[...], axis=-1, keepdims=True))
           - jnp.exp(jnp.sum(lq2_ref[...] * lk2_ref[...], axis=-1, keepdims=True)) + lambda_init)
    l0 = jnp.sum(ls_ref[0], axis=-1, keepdims=True)
    l1 = jnp.sum(ls_ref[1], axis=-1, keepdims=True)
    o = acc_ref[0] / l0 - lam * (acc_ref[1] / l1)
    ms = jnp.mean(o * o, axis=-1, keepdims=True)
    o_ref[...] = (o * lax.rsqrt(ms + LN_EPS) * g_ref[...] * (1.0 - lambda_init)).astype(o_ref.dtype)


def _diff_attention(z_main, lq1, lk1, lq2, lk2, norm_g, B, S, lambda_init):
    tq = min(ATTN_TQ, S)
    nq = S // tq
    H = B_HEADS
    dv = B_V_DIM
    slopes = 2.0 ** (-8.0 * jnp.arange(1, H + 1, dtype=F32) / H)
    assert S <= 256 * 256
    pos = lax.broadcasted_iota(I32, (H, S, dv), 1)
    col = lax.broadcasted_iota(I32, (H, S, dv), 2)
    within = pos % 256
    kbias = jnp.where(col == 0, within, jnp.where(col == 1, pos - within, 0)).astype(F32)
    kbias = (kbias * slopes[:, None, None]).astype(BF16)
    small = pl.BlockSpec((1, B_HEAD_DIM), lambda b, h, i: (0, 0))
    return pl.pallas_call(
        functools.partial(_attn_kernel, lambda_init=lambda_init),
        grid=(B, H, nq),
        in_specs=[
            pl.BlockSpec(memory_space=pltpu.SMEM),
            pl.BlockSpec((tq, dv), lambda b, h, i: (b * nq + i, COL_QB // dv + h)),
            pl.BlockSpec((S, dv), lambda b, h, i: (b, COL_KB // dv + h)),
            pl.BlockSpec((None, S, dv), lambda b, h, i: (h, 0, 0)),
            pl.BlockSpec((S, dv), lambda b, h, i: (b, COL_VB // dv + h)),
            small, small, small, small,
            pl.BlockSpec((1, dv), lambda b, h, i: (0, 0)),
        ],
        out_specs=pl.BlockSpec((tq, dv), lambda b, h, i: (b * nq + i, h)),
        out_shape=jax.ShapeDtypeStruct((B * S, B_WIDTH), BF16),
        scratch_shapes=[
            pltpu.VMEM((2, nq, tq, tq), F32),
            pltpu.VMEM((2, tq, LANES), F32),
            pltpu.VMEM((2, tq, LANES), F32),
            pltpu.VMEM((2, tq, dv), F32),
            pltpu.VMEM((tq, tq), F32),
        ],
        compiler_params=_cparams(("parallel", "parallel", "arbitrary")),
        name="diff_attention",
    )(slopes, z_main, z_main, kbias, z_main, lq1, lk1, lq2, lk2, norm_g)


def _mix_kernel(ha_ref, ob_ref, ga_ref, gb_ref, x_ref, wa_ref, wb_ref, wo_ref, g1_ref, b1_ref, wr_ref, br_ref,
                x1_ref, x1p_ref, ri_ref, rw_ref, cnt_ref, carry_ref):
    i = pl.program_id(0)
    tm = x_ref.shape[0]

    @pl.when(i == 0)
    def _():
        carry_ref[...] = jnp.zeros_like(carry_ref)

    ya = jnp.dot(ha_ref[...], wa_ref[...], preferred_element_type=F32)
    yb = jnp.dot(ob_ref[...], wb_ref[...], preferred_element_type=F32)
    merged = _sigmoid(ga_ref[...].astype(F32)) * ya + _sigmoid(gb_ref[...].astype(F32)) * yb
    mix = jnp.dot(merged.astype(BF16), wo_ref[...], preferred_element_type=F32)
    pre = DEEPNORM_ALPHA * x_ref[...] + mix
    mu = jnp.mean(pre, axis=-1, keepdims=True)
    pc = pre - mu
    var = jnp.mean(pc * pc, axis=-1, keepdims=True)
    x1 = pc * lax.rsqrt(var + LN_EPS) * g1_ref[...] + b1_ref[...]
    x1_ref[...] = x1
    x1p_ref[...] = _pack_halves(x1)

    logits = _dot_3pass(x1, wr_ref[...]) + br_ref[...]
    lt = logits.T
    row8 = lax.broadcasted_iota(I32, (SUBLANES, tm), 0).astype(F32)
    gl = jnp.where(row8 < N_GROUPS, lt[0:SUBLANES, :], NEG_BIG)
    gmax = jnp.max(gl, axis=0, keepdims=True)
    gsel = jnp.min(jnp.where(gl == gmax, row8, float(SUBLANES)), axis=0, keepdims=True)
    gprob = 1.0 / jnp.sum(jnp.exp(gl - gmax), axis=0, keepdims=True)
    ing = lt[SUBLANES:2 * SUBLANES, :]
    for g in range(1, N_GROUPS):
        ing = jnp.where(gsel == g, lt[(g + 1) * SUBLANES:(g + 2) * SUBLANES, :], ing)
    v0 = jnp.max(ing, axis=0, keepdims=True)
    i0 = jnp.min(jnp.where(ing == v0, row8, float(SUBLANES)), axis=0, keepdims=True)
    ing2 = jnp.where(row8 == i0, -jnp.inf, ing)
    v1 = jnp.max(ing2, axis=0, keepdims=True)
    i1 = jnp.min(jnp.where(ing2 == v1, row8, float(SUBLANES)), axis=0, keepdims=True)
    ex = jnp.exp(v1 - v0)
    inv = 1.0 / (1.0 + ex)
    w0 = gprob * inv
    w1 = gprob * (ex * inv)
    e0 = (gsel * EXPERTS_PER_GROUP + i0).astype(I32)
    e1 = (gsel * EXPERTS_PER_GROUP + i1).astype(I32)

    rowe = lax.broadcasted_iota(I32, (N_EXPERTS, tm), 0)
    is0 = rowe == e0
    is1 = rowe == e1
    oh = jnp.where(is0, 1.0, 0.0) + jnp.where(is1, 1.0, 0.0)
    tr = lax.broadcasted_iota(I32, (tm, tm), 0)
    tc = lax.broadcasted_iota(I32, (tm, tm), 1)
    su = jnp.where(tr < tc, 1.0, 0.0).astype(BF16)
    tot = jnp.dot(oh.astype(BF16), su, preferred_element_type=F32) + carry_ref[...]
    rank0 = jnp.sum(jnp.where(is0, tot, 0.0), axis=0, keepdims=True)
    rank1 = jnp.sum(jnp.where(is1, tot, 0.0), axis=0, keepdims=True)
    carry = carry_ref[...] + jnp.sum(oh, axis=1, keepdims=True)
    carry_ref[...] = carry
    cnt_ref[...] = jnp.broadcast_to(carry, cnt_ref.shape).astype(I32)

    zi = jnp.zeros((1, tm), I32)
    ri_ref[...] = jnp.concatenate([e0, e1, rank0.astype(I32), rank1.astype(I32), zi, zi, zi, zi], axis=0)
    zf = jnp.zeros((1, tm), F32)
    rw_ref[...] = jnp.concatenate([w0, w1, zf, zf, zf, zf, zf, zf], axis=0)


def _mix(h_a, o_b, z_main, x2d, w_a, w_b, w_out, ln_g, ln_b, w_r, b_r):
    T, D = x2d.shape
    tm = min(MIX_TM, T)
    const = lambda shape: pl.BlockSpec(shape, lambda i: (0, 0), pipeline_mode=pl.Buffered(1))
    return pl.pallas_call(
        _mix_kernel,
        grid=(T // tm,),
        in_specs=[
            pl.BlockSpec((tm, A_WIDTH), lambda i: (i, 0)),
            pl.BlockSpec((tm, B_WIDTH), lambda i: (i, 0)),
            pl.BlockSpec((tm, D), lambda i: (i, COL_GA // D)),
            pl.BlockSpec((tm, D), lambda i: (i, COL_GB // D)),
            pl.BlockSpec((tm, D), lambda i: (i, 0)),
            const((A_WIDTH, D)), const((B_WIDTH, D)), const((D, D)),
            const((1, D)), const((1, D)), const((D, LANES)), const((1, LANES)),
        ],
        out_specs=[
            pl.BlockSpec((tm, D), lambda i: (i, 0)),
            pl.BlockSpec((tm, D // 2), lambda i: (i, 0)),
            pl.BlockSpec((SUBLANES, tm), lambda i: (0, i)),
            pl.BlockSpec((SUBLANES, tm), lambda i: (0, i)),
            pl.BlockSpec((N_EXPERTS, LANES), lambda i: (0, 0)),
        ],
        out_shape=[
            jax.ShapeDtypeStruct((T, D), F32),
            jax.ShapeDtypeStruct((T, D // 2), U32),
            jax.ShapeDtypeStruct((SUBLANES, T), I32),
            jax.ShapeDtypeStruct((SUBLANES, T), F32),
            jax.ShapeDtypeStruct((N_EXPERTS, LANES), I32),
        ],
        scratch_shapes=[pltpu.VMEM((N_EXPERTS, 1), F32)],
        compiler_params=_cparams(("arbitrary",)),
        name="mix_ln1_router",
    )(h_a, o_b, z_main, z_main, x2d, w_a, w_b, w_out, ln_g, ln_b, w_r, b_r)


def _row_copy(src_ref, src_row, dst_ref, dst_row, sem):
    return pltpu.make_async_copy(src_ref.at[pl.ds(src_row, 1), :], dst_ref.at[pl.ds(dst_row, 1), :], sem)


def _rows_wait(src_ref, dst_ref, rows, sem):
    pltpu.make_async_copy(src_ref.at[pl.ds(0, rows), :], dst_ref.at[pl.ds(0, rows), :], sem).wait()


def _dispatch_kernel(cnt_ref, ri_ref, x1_ref, xpad_ref, pstart_ref, bexp_ref, nused_ref,
                     pst_ref, zero_ref, sem, zsem, *, tb, n_blocks, tm):
    i = pl.program_id(0)
    last = pl.num_programs(0) - 1

    @pl.when(i == 0)
    def _():
        def fill(b, carry):
            bexp_ref[b] = 0
            return carry
        lax.fori_loop(0, n_blocks, fill, 0)

        def per_expert(e, blk):
            pst_ref[e] = blk * tb
            pstart_ref[e] = blk * tb
            nb = (cnt_ref[e] + (tb - 1)) // tb

            def mark(b, carry):
                bexp_ref[blk + b] = e
                return carry
            lax.fori_loop(0, nb, mark, 0)
            return blk + nb
        nused = lax.fori_loop(0, N_EXPERTS, per_expert, 0)
        nused_ref[0] = nused
        pst_ref[N_EXPERTS] = nused

    def issue(t, carry):
        for k in range(TOP_K):
            pos = pst_ref[ri_ref[k, t]] + ri_ref[TOP_K + k, t]
            _row_copy(x1_ref, t, xpad_ref, pos, sem).start()
        return carry
    lax.fori_loop(0, tm, issue, 0, unroll=8)
    for k in range(TOP_K):
        _rows_wait(x1_ref, xpad_ref, tm, sem)

    @pl.when(i == last)
    def _():
        zero_ref[...] = jnp.zeros_like(zero_ref)

        def per_expert(e, carry):
            cnt = cnt_ref[e]
            first = pst_ref[e] + cnt
            npad = ((cnt + (tb - 1)) // tb) * tb - cnt

            def start(p, c2):
                _row_copy(zero_ref, 0, xpad_ref, first + p, zsem).start()
                return c2
            lax.fori_loop(0, npad, start, 0)

            def wait(p, c2):
                _row_copy(zero_ref, 0, xpad_ref, 0, zsem).wait()
                return c2
            lax.fori_loop(0, npad, wait, 0)
            return carry
        lax.fori_loop(0, N_EXPERTS, per_expert, 0)

        def block_copy(b):
            return pltpu.make_async_copy(zero_ref, xpad_ref.at[pl.ds(pl.multiple_of(b * tb, tb), tb), :], zsem)

        def start_block(b, carry):
            block_copy(b).start()
            return carry
        lax.fori_loop(pst_ref[N_EXPERTS], n_blocks, start_block, 0)

        def wait_block(b, carry):
            block_copy(b).wait()
            return carry
        lax.fori_loop(pst_ref[N_EXPERTS], n_blocks, wait_block, 0)


def _dispatch(counts, route_i, x1, tb, n_blocks):
    T, D = x1.shape
    tm = min(DISPATCH_TM, T)
    smem_full = pl.BlockSpec(memory_space=pltpu.SMEM)
    return pl.pallas_call(
        functools.partial(_dispatch_kernel, tb=tb, n_blocks=n_blocks, tm=tm),
        grid=(T // tm,),
        in_specs=[
            smem_full,
            pl.BlockSpec((SUBLANES, tm), lambda i: (0, i), memory_space=pltpu.SMEM),
            pl.BlockSpec((tm, D), lambda i: (i, 0)),
        ],
        out_specs=[
            pl.BlockSpec(memory_space=pl.ANY),
            smem_full, smem_full, smem_full,
        ],
        out_shape=[
            jax.ShapeDtypeStruct((n_blocks * tb, D), x1.dtype),
            jax.ShapeDtypeStruct((N_EXPERTS,), I32),
            jax.ShapeDtypeStruct((n_blocks,), I32),
            jax.ShapeDtypeStruct((1,), I32),
        ],
        scratch_shapes=[
            pltpu.SMEM((N_EXPERTS + 1,), I32),
            pltpu.VMEM((tb, D), x1.dtype),
            pltpu.SemaphoreType.DMA(()),
            pltpu.SemaphoreType.DMA(()),
        ],
        compiler_params=_cparams(("arbitrary",)),
        name="moe_dispatch",
    )(counts, route_i, x1)


def _expert_kernel(bexp_ref, nused_ref, x_ref, wg_hbm, wu_hbm, wd_hbm, y_ref,
                   wgf_ref, wuf_ref, wdf_ref, wgb_ref, wub_ref, wdb_ref, sem, run_ref):
    b = pl.program_id(0)
    nused = nused_ref[0]
    used = b < nused
    e = bexp_ref[b]
    new_expert = jnp.logical_or(b == 0, e != bexp_ref[jnp.maximum(b - 1, 0)])
    streams = ((wg_hbm, wgf_ref), (wu_hbm, wuf_ref), (wd_hbm, wdf_ref))

    def weight_copies(expert, slot):
        return [pltpu.make_async_copy(w_hbm.at[expert], wf_ref.at[slot], sem.at[slot, j])
                for j, (w_hbm, wf_ref) in enumerate(streams)]

    @pl.when(jnp.logical_and(used, b == 0))
    def _():
        run_ref[0] = 0
        for cp in weight_copies(e, 0):
            cp.start()

    @pl.when(jnp.logical_and(used, new_expert))
    def _():
        slot = run_ref[0] % 2
        nxt = lax.while_loop(lambda j: jnp.logical_and(j < nused, bexp_ref[jnp.minimum(j, nused - 1)] == e),
                             lambda j: j + 1, b + 1)

        @pl.when(nxt < nused)
        def _():
            for cp in weight_copies(bexp_ref[jnp.minimum(nxt, nused - 1)], 1 - slot):
                cp.start()

        for cp in weight_copies(e, slot):
            cp.wait()
        wgb_ref[...] = wgf_ref[slot].astype(BF16)
        wub_ref[...] = wuf_ref[slot].astype(BF16)
        wdb_ref[...] = wdf_ref[slot].astype(BF16)
        run_ref[0] = run_ref[0] + 1

    @pl.when(used)
    def _():
        half = x_ref.shape[1]
        x_lo, x_hi = (t.astype(BF16) for t in _unpack_halves(x_ref[...]))
        dot = functools.partial(jnp.dot, preferred_element_type=F32)
        gate = dot(x_lo, wgb_ref[:half, :]) + dot(x_hi, wgb_ref[half:, :])
        up = dot(x_lo, wub_ref[:half, :]) + dot(x_hi, wub_ref[half:, :])
        hid = (gate * _sigmoid(gate) * up).astype(BF16)
        y_ref[...] = _pack_halves(dot(hid, wdb_ref[...]))

    @pl.when(jnp.logical_not(used))
    def _():
        y_ref[...] = jnp.zeros_like(y_ref)


def _experts(bexp, nused, x_pad, w_gate, w_up, w_down, tb):
    P, Dh = x_pad.shape
    D = 2 * Dh
    n_blocks = P // tb
    F = w_gate.shape[-1]
    blk = lambda b, be, nu: jnp.maximum(jnp.minimum(b, nu[0] - 1), 0)
    hbm = pl.BlockSpec(memory_space=pl.ANY)
    return pl.pallas_call(
        _expert_kernel,
        grid_spec=pltpu.PrefetchScalarGridSpec(
            num_scalar_prefetch=2,
            grid=(n_blocks,),
            in_specs=[pl.BlockSpec((tb, Dh), lambda b, be, nu: (blk(b, be, nu), 0)), hbm, hbm, hbm],
            out_specs=pl.BlockSpec((tb, Dh), lambda b, be, nu: (b, 0)),
            scratch_shapes=[
                pltpu.VMEM((2, D, F), F32), pltpu.VMEM((2, D, F), F32), pltpu.VMEM((2, F, D), F32),
                pltpu.VMEM((D, F), BF16), pltpu.VMEM((D, F), BF16), pltpu.VMEM((F, D), BF16),
                pltpu.SemaphoreType.DMA((2, 3)),
                pltpu.SMEM((1,), I32),
            ],
        ),
        out_shape=jax.ShapeDtypeStruct((P, Dh), U32),
        compiler_params=_cparams(("arbitrary",)),
        name="moe_experts",
    )(bexp, nused, x_pad, w_gate, w_up, w_down)


def _combine_kernel(pstart_ref, ri_ref, rin_ref, x1_ref, rw_ref, g_ref, b_ref, ypad_ref, o_ref, ybuf_ref, sem):
    i = pl.program_id(0)
    tm = x1_ref.shape[0]
    slot = i % 2

    def gather(route_ref, s):
        def issue(t, carry):
            for k in range(TOP_K):
                pos = pstart_ref[route_ref[k, t]] + route_ref[TOP_K + k, t]
                _row_copy(ypad_ref, pos, ybuf_ref.at[s, k], t, sem.at[s]).start()
            return carry
        lax.fori_loop(0, tm, issue, 0, unroll=8)

    @pl.when(i == 0)
    def _():
        gather(ri_ref, 0)

    @pl.when(i + 1 < pl.num_programs(0))
    def _():
        gather(rin_ref, 1 - slot)

    for k in range(TOP_K):
        _rows_wait(ypad_ref, ybuf_ref.at[slot, k], tm, sem.at[slot])

    rw = rw_ref[...]
    y0 = _unpack_halves(ybuf_ref[slot, 0])
    y1 = _unpack_halves(ybuf_ref[slot, 1])
    ffn = jnp.concatenate([y0[h] * rw[:, 0:1] + y1[h] * rw[:, 1:2] for h in range(2)], axis=1)
    pre = DEEPNORM_ALPHA * x1_ref[...] + ffn
    mu = jnp.mean(pre, axis=-1, keepdims=True)
    pc = pre - mu
    var = jnp.mean(pc * pc, axis=-1, keepdims=True)
    o_ref[...] = pc * lax.rsqrt(var + LN_EPS) * g_ref[...] + b_ref[...]


def _combine(pstart, route_i, x1, rw_col, ln_g, ln_b, y_pad):
    T, D = x1.shape
    tm = min(COMBINE_TM, T)
    n = T // tm
    return pl.pallas_call(
        _combine_kernel,
        grid=(n,),
        in_specs=[
            pl.BlockSpec(memory_space=pltpu.SMEM),
            pl.BlockSpec((SUBLANES, tm), lambda i: (0, i), memory_space=pltpu.SMEM),
            pl.BlockSpec((SUBLANES, tm), lambda i: (0, jnp.minimum(i + 1, n - 1)), memory_space=pltpu.SMEM),
            pl.BlockSpec((tm, D), lambda i: (i, 0)),
            pl.BlockSpec((tm, SUBLANES), lambda i: (i, 0)),
            pl.BlockSpec((1, D), lambda i: (0, 0)),
            pl.BlockSpec((1, D), lambda i: (0, 0)),
            pl.BlockSpec(memory_space=pl.ANY),
        ],
        out_specs=pl.BlockSpec((tm, D), lambda i: (i, 0)),
        out_shape=jax.ShapeDtypeStruct((T, D), F32),
        scratch_shapes=[
            pltpu.VMEM((2, TOP_K, tm, D // 2), U32),
            pltpu.SemaphoreType.DMA((2,)),
        ],
        compiler_params=_cparams(("arbitrary",)),
        name="moe_combine_ln2",
    )(pstart, route_i, route_i, x1, rw_col, ln_g, ln_b, y_pad)


N_GATE_COLS = 2 * A_HEADS
SRC_MLSTM = 0
SRC_GATES = 4 * A_WIDTH
SRC_DIFF = SRC_GATES + N_GATE_COLS
SRC_MERGE = SRC_DIFF + 3 * B_WIDTH
PREP_TN = 512


def _wprep_kernel(a_ref, b_ref, o_ref):
    jb = pl.program_id(0)
    merge_blocks = (2 * D_MODEL) // PREP_TN
    mlstm_blocks = (4 * A_WIDTH) // PREP_TN
    aligned = jnp.logical_and(jb >= merge_blocks, jb < merge_blocks + mlstm_blocks)

    @pl.when(aligned)
    def _():
        o_ref[...] = a_ref[...].astype(BF16)

    @pl.when(jnp.logical_not(aligned))
    def _():
        a = a_ref[...]
        shifted = jnp.concatenate([a[:, N_GATE_COLS:], b_ref[:, :N_GATE_COLS]], axis=1)
        o_ref[...] = shifted.astype(BF16)


def _rearrange_in_proj(w_in, b_in):
    K, n_in = w_in.shape
    tn = PREP_TN
    merge_blocks = (2 * D_MODEL) // tn
    mlstm_blocks = (4 * A_WIDTH) // tn
    assert SRC_MERGE % tn == N_GATE_COLS and SRC_DIFF % tn == N_GATE_COLS and SRC_MLSTM % tn == 0

    def a_index(jb):
        merge = SRC_MERGE // tn + jb
        mlstm = SRC_MLSTM // tn + (jb - merge_blocks)
        diff = SRC_DIFF // tn + (jb - merge_blocks - mlstm_blocks)
        return jnp.where(jb < merge_blocks, merge, jnp.where(jb < merge_blocks + mlstm_blocks, mlstm, diff))

    w_main = pl.pallas_call(
        _wprep_kernel,
        grid=(N_MAIN // tn,),
        in_specs=[
            pl.BlockSpec((K, tn), lambda jb: (0, a_index(jb))),
            pl.BlockSpec((K, LANES), lambda jb: (0, (a_index(jb) + 1) * (tn // LANES))),
        ],
        out_specs=pl.BlockSpec((K, tn), lambda jb: (0, jb)),
        out_shape=jax.ShapeDtypeStruct((K, N_MAIN), BF16),
        compiler_params=_cparams(("parallel",)),
        name="w_in_prep",
    )(w_in, w_in)
    b_main = jnp.concatenate([b_in[SRC_MERGE:], b_in[SRC_MLSTM:SRC_GATES], b_in[SRC_DIFF:SRC_MERGE]])[None, :]
    w_gate = jnp.pad(w_in[:, SRC_GATES:SRC_DIFF], ((0, 0), (0, LANES - N_GATE_COLS)))
    b_gate = jnp.pad(b_in[SRC_GATES:SRC_DIFF], (0, LANES - N_GATE_COLS))[None, :]
    return w_main, b_main, w_gate, b_gate


def _layer(x, w_in, b_in, conv_w, conv_b, norm_a_g, lq1, lk1, lq2, lk2, norm_b_g, w_a, w_b, w_out,
           ln1_g, ln1_b, w_grp, b_grp, w_exp, b_exp, w_gate, w_up, w_down, ln2_g, ln2_b, lambda_init):
    B, S, D = x.shape
    T = B * S
    x2d = x.reshape(T, D)

    w_main, b_main, w_g, b_g = _rearrange_in_proj(w_in, b_in)
    z_main, z_gate = _in_projection(x2d, w_main, b_main, w_g, b_g)

    L = min(MLSTM_CHUNK, S)
    gp = _gate_prep(z_gate[:, :2 * A_HEADS].T, L)
    h_a = _mlstm(z_main, gp.T, gp, conv_w, conv_b[None, :], norm_a_g[None, :], B, S)
    o_b = _diff_attention(z_main, lq1[None, :], lk1[None, :], lq2[None, :], lk2[None, :],
                          norm_b_g[None, :], B, S, lambda_init)

    w_r = jnp.zeros((D, LANES), F32).at[:, :N_GROUPS].set(w_grp).at[:, SUBLANES:SUBLANES + N_EXPERTS].set(w_exp)
    b_r = jnp.zeros((LANES,), F32).at[:N_GROUPS].set(b_grp).at[SUBLANES:SUBLANES + N_EXPERTS].set(b_exp)[None, :]
    x1, x1_packed, route_i, route_w, counts = _mix(h_a, o_b, z_main, x2d, w_a.astype(BF16), w_b.astype(BF16),
                                                   w_out.astype(BF16), ln1_g[None, :], ln1_b[None, :], w_r, b_r)

    tb = MOE_TB
    n_blocks = (T * TOP_K) // tb + N_EXPERTS
    x_pad, pstart, bexp, nused = _dispatch(counts[:, 0], route_i, x1_packed, tb, n_blocks)
    y_pad = _experts(bexp, nused, x_pad, w_gate, w_up, w_down, tb)
    out = _combine(pstart, route_i, x1, route_w.T, ln2_g[None, :], ln2_b[None, :], y_pad)
    return out.reshape(B, S, D)


def kernel(x, w_in, b_in, conv_w, conv_b, mlstm_norm_g, lambda_q1, lambda_k1, lambda_q2, lambda_k2,
           diff_norm_g, w_a, w_b, w_out, ln1_g, ln1_b, w_grp, b_grp, w_exp, b_exp,
           w_gate, w_up, w_down, ln2_g, ln2_b):
    for l in range(DEPTH):
        lambda_init = 0.8 - 0.6 * math.exp(-0.3 * l)
        x = _layer(x, w_in[l], b_in[l], conv_w[l], conv_b[l], mlstm_norm_g[l], lambda_q1[l], lambda_k1[l],
                   lambda_q2[l], lambda_k2[l], diff_norm_g[l], w_a[l], w_b[l], w_out[l], ln1_g[l], ln1_b[l],
                   w_grp[l], b_grp[l], w_exp[l], b_exp[l], w_gate[l], w_up[l], w_down[l], ln2_g[l], ln2_b[l],
                   lambda_init)
    return x
```

```python
import functools
import math

import jax
import jax.numpy as jnp
from jax import lax
from jax.experimental import pallas as pl
from jax.experimental.pallas import tpu as pltpu

F32 = jnp.float32
BF16 = jnp.bfloat16
I32 = jnp.int32

D_MODEL = 2048
A_HEADS = 4
A_HEAD_DIM = 256
A_WIDTH = A_HEADS * A_HEAD_DIM
CONV_WIDTH = 4
B_HEADS = 8
B_HEAD_DIM = 64
B_V_DIM = 2 * B_HEAD_DIM
B_WIDTH = B_HEADS * B_V_DIM
ATTN_CHUNK = 64
N_GROUPS = 4
EXPERTS_PER_GROUP = 8
N_EXPERTS = N_GROUPS * EXPERTS_PER_GROUP
TOP_K = 2
D_EXPERT = 512
DEPTH = 1
DEEPNORM_ALPHA = (2 * DEPTH) ** 0.25
LN_EPS = 1e-5
NEG_BIG = -1e30
LOG2_E = 1.4426950408889634

LANES = 128
SUBLANES = 8
VMEM_LIMIT_BYTES = 56 * 1024 * 1024

COL_GA = 0
COL_GB = COL_GA + D_MODEL
COL_QA = COL_GB + D_MODEL
COL_KA = COL_QA + A_WIDTH
COL_VA = COL_KA + A_WIDTH
COL_OA = COL_VA + A_WIDTH
COL_QB = COL_OA + A_WIDTH
COL_KB = COL_QB + B_WIDTH
COL_VB = COL_KB + B_WIDTH
N_MAIN = COL_VB + B_WIDTH

PROJ_TM, PROJ_TN = 1024, 1024
MLSTM_CHUNK = 256
ATTN_TQ = 512
MIX_TM = 256
MOE_TB = 256
DISPATCH_TM = 256
COMBINE_TM = 256


def _cparams(sem, vmem=VMEM_LIMIT_BYTES):
    return pltpu.CompilerParams(dimension_semantics=sem, vmem_limit_bytes=vmem)


def _sigmoid(x):
    return 1.0 / (1.0 + jnp.exp(-x))


U32 = jnp.uint32


def _pack_halves(x):
    n = x.shape[1] // 2
    lo = lax.bitcast_convert_type(x[:, :n].astype(BF16).astype(F32), U32)
    hi = lax.bitcast_convert_type(x[:, n:].astype(BF16).astype(F32), U32)
    return (lo >> 16) | hi


def _unpack_halves(u):
    lo = lax.bitcast_convert_type(u << 16, F32)
    hi = lax.bitcast_convert_type(u & jnp.uint32(0xFFFF0000), F32)
    return lo, hi


def _split_bf16(a):
    hi = a.astype(BF16)
    return hi, (a - hi.astype(F32)).astype(BF16)


def _dot_3pass(a, b):
    a_hi, a_lo = _split_bf16(a)
    b_hi, b_lo = _split_bf16(b)
    dot = functools.partial(jnp.dot, preferred_element_type=F32)
    return dot(a_hi, b_hi) + (dot(a_hi, b_lo) + dot(a_lo, b_hi))


def _proj_kernel(x_ref, w_ref, b_ref, wg_ref, bg_ref, z_ref, zg_ref, xb_ref):
    @pl.when(pl.program_id(1) == 0)
    def _():
        x = x_ref[...]
        xb_ref[...] = x.astype(BF16)
        zg_ref[...] = _dot_3pass(x, wg_ref[...]) + bg_ref[...]

    acc = jnp.dot(xb_ref[...], w_ref[...], preferred_element_type=F32)
    z_ref[...] = (acc + b_ref[...]).astype(z_ref.dtype)


def _in_projection(x2d, w_main, b_main, w_gate, b_gate):
    T, K = x2d.shape
    N = w_main.shape[1]
    tm, tn = min(PROJ_TM, T), PROJ_TN
    return pl.pallas_call(
        _proj_kernel,
        grid=(T // tm, N // tn),
        in_specs=[
            pl.BlockSpec((tm, K), lambda i, j: (i, 0)),
            pl.BlockSpec((K, tn), lambda i, j: (0, j)),
            pl.BlockSpec((1, tn), lambda i, j: (0, j)),
            pl.BlockSpec((K, LANES), lambda i, j: (0, 0)),
            pl.BlockSpec((1, LANES), lambda i, j: (0, 0)),
        ],
        out_specs=[
            pl.BlockSpec((tm, tn), lambda i, j: (i, j)),
            pl.BlockSpec((tm, LANES), lambda i, j: (i, 0)),
        ],
        out_shape=[
            jax.ShapeDtypeStruct((T, N), BF16),
            jax.ShapeDtypeStruct((T, LANES), F32),
        ],
        scratch_shapes=[pltpu.VMEM((tm, K), BF16)],
        compiler_params=_cparams(("parallel", "arbitrary")),
        name="in_projection",
    )(x2d, w_main, b_main, w_gate, b_gate)


def _gate_prep_kernel(g_ref, o_ref):
    g = g_ref[...]
    L = g.shape[1]
    lf = jnp.minimum(g, 0.0) - jnp.log(1.0 + jnp.exp(-jnp.abs(g)))
    r = lax.broadcasted_iota(I32, (L, L), 0)
    c = lax.broadcasted_iota(I32, (L, L), 1)
    tri = (r <= c).astype(F32)
    b = jnp.dot(lf, tri, preferred_element_type=F32, precision=lax.Precision.HIGHEST)
    row = lax.broadcasted_iota(I32, g.shape, 0)
    o_ref[...] = jnp.where(row < A_HEADS, g, b)


def _gate_prep(g_rows, L):
    R, T = g_rows.shape
    return pl.pallas_call(
        _gate_prep_kernel,
        grid=(T // L,),
        in_specs=[pl.BlockSpec((R, L), lambda i: (0, i))],
        out_specs=pl.BlockSpec((R, L), lambda i: (0, i)),
        out_shape=jax.ShapeDtypeStruct((R, T), F32),
        compiler_params=_cparams(("parallel",)),
        name="gate_prep",
    )(g_rows)


def _mlstm_kernel(q_ref, k_ref, v_ref, o_ref, gcol_ref, grow_ref, cw_ref, cb_ref, ng_ref, out_ref,
                  c_ref, n_ref, m_ref, qcar_ref, kcar_ref):
    ci = pl.program_id(1)
    L = q_ref.shape[0]
    dh = A_HEAD_DIM

    @pl.when(ci == 0)
    def _():
        c_ref[...] = jnp.zeros_like(c_ref)
        n_ref[...] = jnp.zeros_like(n_ref)
        m_ref[...] = jnp.zeros_like(m_ref)
        qcar_ref[...] = jnp.zeros_like(qcar_ref)
        kcar_ref[...] = jnp.zeros_like(kcar_ref)

    def conv_silu(u_ref, car_ref, lo, wcol):
        u = u_ref[:, lo:lo + dh].astype(F32)
        ext = jnp.concatenate([car_ref[:, lo:lo + dh], u], axis=0)
        w = cw_ref[:, wcol:wcol + dh]
        y = cb_ref[:, wcol:wcol + dh] + w[3:4, :] * u
        for j in range(CONV_WIDTH - 1):
            off = SUBLANES - (CONV_WIDTH - 1) + j
            y = y + w[j:j + 1, :] * ext[off:off + L, :]
        car_ref[:, lo:lo + dh] = u[L - SUBLANES:, :]
        return y * _sigmoid(y)

    r = lax.broadcasted_iota(I32, (L, L), 0)
    c = lax.broadcasted_iota(I32, (L, L), 1)
    causal = r >= c
    gcol = gcol_ref[...]
    grow = grow_ref[...]

    for h in range(A_HEADS):
        lo = h * dh
        q = conv_silu(q_ref, qcar_ref, lo, lo)
        k = conv_silu(k_ref, kcar_ref, lo, A_WIDTH + lo) * (dh ** -0.5)
        v_b = v_ref[:, lo:lo + dh]
        q_b = q.astype(BF16)
        k_b = k.astype(BF16)

        i_col, b_col = gcol[:, h:h + 1], gcol[:, A_HEADS + h:A_HEADS + h + 1]
        i_row, b_row = grow[h:h + 1, :], grow[A_HEADS + h:A_HEADS + h + 1, :]
        b_last = b_row[:, L - 1:L]
        m_prev = m_ref[h]

        dmat = jnp.where(causal, b_col - b_row + i_row, NEG_BIG)
        inter = b_col + m_prev
        m_t = jnp.maximum(inter, jnp.max(dmat, axis=-1, keepdims=True))
        s = lax.dot_general(q_b, k_b, (((1,), (1,)), ((), ())), preferred_element_type=F32)
        w_intra = jnp.exp(dmat - m_t) * s
        w_inter = jnp.exp(inter - m_t)
        qc = jnp.dot(q_b, c_ref[h].astype(BF16), preferred_element_type=F32)
        num = w_inter * qc + jnp.dot(w_intra.astype(BF16), v_b, preferred_element_type=F32)
        qn = jnp.sum(q * n_ref[h], axis=-1, keepdims=True)
        den = w_inter * qn + jnp.sum(w_intra, axis=-1, keepdims=True)
        hh = num / jnp.maximum(jnp.abs(den), jnp.exp(-m_t))

        g_col = b_last - b_col + i_col
        m_new = jnp.maximum(b_last + m_prev, jnp.max(g_col, axis=0, keepdims=True))
        decay = jnp.exp(b_last + m_prev - m_new)
        kw = jnp.exp(g_col - m_new) * k
        c_ref[h] = decay * c_ref[h] + jnp.dot(kw.T.astype(BF16), v_b, preferred_element_type=F32)
        n_ref[h] = decay * n_ref[h] + jnp.sum(kw, axis=0, keepdims=True)
        m_ref[h] = m_new

        y = _sigmoid(o_ref[:, lo:lo + dh].astype(F32)) * hh
        mu = jnp.mean(y, axis=-1, keepdims=True)
        yc = y - mu
        var = jnp.mean(yc * yc, axis=-1, keepdims=True)
        out_ref[:, lo:lo + dh] = (yc * lax.rsqrt(var + LN_EPS) * ng_ref[:, lo:lo + dh]).astype(out_ref.dtype)


def _mlstm(z_main, gcol, grow, conv_w, conv_b, norm_g, B, S):
    L = min(MLSTM_CHUNK, S)
    nc = S // L
    dh = A_HEAD_DIM
    H = A_HEADS
    W = A_WIDTH
    row = lambda b, c: b * nc + c
    full = lambda shape: pl.BlockSpec(shape, lambda b, c: (0, 0))
    return pl.pallas_call(
        _mlstm_kernel,
        grid=(B, nc),
        in_specs=[
            pl.BlockSpec((L, W), lambda b, c: (row(b, c), COL_QA // W)),
            pl.BlockSpec((L, W), lambda b, c: (row(b, c), COL_KA // W)),
            pl.BlockSpec((L, W), lambda b, c: (row(b, c), COL_VA // W)),
            pl.BlockSpec((L, W), lambda b, c: (row(b, c), COL_OA // W)),
            pl.BlockSpec((L, SUBLANES), lambda b, c: (row(b, c), 0)),
            pl.BlockSpec((SUBLANES, L), lambda b, c: (0, row(b, c))),
            full((CONV_WIDTH, 2 * W)), full((1, 2 * W)), full((1, W)),
        ],
        out_specs=pl.BlockSpec((L, W), lambda b, c: (row(b, c), 0)),
        out_shape=jax.ShapeDtypeStruct((B * S, W), BF16),
        scratch_shapes=[
            pltpu.VMEM((H, dh, dh), F32),
            pltpu.VMEM((H, 1, dh), F32),
            pltpu.VMEM((H, 1, 1), F32),
            pltpu.VMEM((SUBLANES, W), F32),
            pltpu.VMEM((SUBLANES, W), F32),
        ],
        compiler_params=_cparams(("parallel", "arbitrary")),
        name="mlstm",
    )(z_main, z_main, z_main, z_main, gcol, grow, conv_w, conv_b, norm_g)


def _lane_tiles(t):
    return [t[:, i * LANES:(i + 1) * LANES] for i in range(t.shape[1] // LANES)]


def _attn_kernel(slope_ref, q_ref, k_ref, kb_ref, v_ref, lq1_ref, lk1_ref, lq2_ref, lk2_ref, g_ref, o_ref,
                 s_ref, mx_ref, ls_ref, acc_ref, corr_ref, *, lambda_init):
    h = pl.program_id(1)
    qi = pl.program_id(2)
    tq = q_ref.shape[0]
    tk = tq
    slope = slope_ref[h]
    nt = (((1,), (1,)), ((), ()))

    lane = lax.broadcasted_iota(I32, (tq, B_V_DIM), 1)
    q = q_ref[...] * (B_HEAD_DIM ** -0.5)
    zero = jnp.zeros_like(q)
    bias_cols = jnp.where(lane < 2, 1.0, 0.0).astype(BF16)
    qa = (jnp.concatenate([jnp.where(lane < B_HEAD_DIM, q, zero), bias_cols], axis=1),
          jnp.concatenate([jnp.where(lane >= B_HEAD_DIM, q, zero), bias_cols], axis=1))

    def scores(j):
        start = pl.multiple_of(j * tk, tk)
        ka = jnp.concatenate([k_ref[pl.ds(start, tk), :], kb_ref[pl.ds(start, tk), :]], axis=1)
        return [lax.dot_general(qa[n], ka, nt, preferred_element_type=F32) for n in range(2)]

    def keep(j, n, t):
        t = t * LOG2_E
        s_ref[n, j] = t
        mx_ref[n] = functools.reduce(jnp.maximum, _lane_tiles(t), mx_ref[n])

    mx_ref[...] = jnp.full(mx_ref.shape, NEG_BIG, F32)

    def pass_a(j):
        t = scores(j)
        for n in range(2):
            keep(j, n, t[n])

    def pass_a_pair(jj, carry):
        pass_a(2 * jj)
        pass_a(2 * jj + 1)
        return carry
    lax.fori_loop(0, qi // 2, pass_a_pair, 0)

    @pl.when(qi % 2 == 1)
    def _():
        pass_a(qi - 1)

    @pl.when(qi == 0)
    def _():
        r = lax.broadcasted_iota(I32, (tq, tk), 0)
        c = lax.broadcasted_iota(I32, (tq, tk), 1)
        ahead = jnp.where(c <= r, 0.0, (r - c).astype(F32) * (2.0 * slope))
        corr_ref[...] = jnp.where((c // ATTN_CHUNK) <= (r // ATTN_CHUNK), ahead, NEG_BIG)

    t = scores(qi)
    for n in range(2):
        keep(qi, n, t[n] + corr_ref[...])

    for n in range(2):
        m = jnp.max(mx_ref[n], axis=-1, keepdims=True)
        mx_ref[n] = jnp.broadcast_to(m, (tq, LANES))
    ls_ref[...] = jnp.zeros_like(ls_ref)
    acc_ref[...] = jnp.zeros_like(acc_ref)

    def pass_b(j, carry):
        start = pl.multiple_of(j * tk, tk)
        vj = v_ref[pl.ds(start, tk), :]
        for n in range(2):
            mb = mx_ref[n]
            ps = [jnp.exp2(t - mb) for t in _lane_tiles(s_ref[n, j])]
            ls_ref[n] += functools.reduce(jnp.add, ps)
            p = jnp.concatenate(ps, axis=1).astype(BF16)
            acc_ref[n] += jnp.dot(p, vj, preferred_element_type=F32)
        return carry
    lax.fori_loop(0, qi + 1, pass_b, 0)

    lam = (jnp.exp(jnp.sum(lq1_ref[...] * lk1_ref[...], axis=-1, keepdims=True))
           - jnp.exp(jnp.sum(lq2_ref[...] * lk2_ref[...], axis=-1, keepdims=True)) + lambda_init)
    l0 = jnp.sum(ls_ref[0], axis=-1, keepdims=True)
    l1 = jnp.sum(ls_ref[1], axis=-1, keepdims=True)
    o = acc_ref[0] / l0 - lam * (acc_ref[1] / l1)
    ms = jnp.mean(o * o, axis=-1, keepdims=True)
    o_ref[...] = (o * lax.rsqrt(ms + LN_EPS) * g_ref[...] * (1.0 - lambda_init)).astype(o_ref.dtype)


def _diff_attention(z_main, lq1, lk1, lq2, lk2, norm_g, B, S, lambda_init):
    tq = min(ATTN_TQ, S)
    nq = S // tq
    H = B_HEADS
    dv = B_V_DIM
    slopes = 2.0 ** (-8.0 * jnp.arange(1, H + 1, dtype=F32) / H)
    assert S <= 256 * 256
    pos = lax.broadcasted_iota(I32, (H, S, dv), 1)
    col = lax.broadcasted_iota(I32, (H, S, dv), 2)
    within = pos % 256
    kbias = jnp.where(col == 0, within, jnp.where(col == 1, pos - within, 0)).astype(F32)
    kbias = (kbias * slopes[:, None, None]).astype(BF16)
    small = pl.BlockSpec((1, B_HEAD_DIM), lambda b, h, i: (0, 0))
    return pl.pallas_call(
        functools.partial(_attn_kernel, lambda_init=lambda_init),
        grid=(B, H, nq),
        in_specs=[
            pl.BlockSpec(memory_space=pltpu.SMEM),
            pl.BlockSpec((tq, dv), lambda b, h, i: (b * nq + i, COL_QB // dv + h)),
            pl.BlockSpec((S, dv), lambda b, h, i: (b, COL_KB // dv + h)),
            pl.BlockSpec((None, S, dv), lambda b, h, i: (h, 0, 0)),
            pl.BlockSpec((S, dv), lambda b, h, i: (b, COL_VB // dv + h)),
            small, small, small, small,
            pl.BlockSpec((1, dv), lambda b, h, i: (0, 0)),
        ],
        out_specs=pl.BlockSpec((tq, dv), lambda b, h, i: (b * nq + i, h)),
        out_shape=jax.ShapeDtypeStruct((B * S, B_WIDTH), BF16),
        scratch_shapes=[
            pltpu.VMEM((2, nq, tq, tq), F32),
            pltpu.VMEM((2, tq, LANES), F32),
            pltpu.VMEM((2, tq, LANES), F32),
            pltpu.VMEM((2, tq, dv), F32),
            pltpu.VMEM((tq, tq), F32),
        ],
        compiler_params=_cparams(("parallel", "parallel", "arbitrary")),
        name="diff_attention",
    )(slopes, z_main, z_main, kbias, z_main, lq1, lk1, lq2, lk2, norm_g)


def _mix_kernel(ha_ref, ob_ref, ga_ref, gb_ref, x_ref, wa_ref, wb_ref, wo_ref, g1_ref, b1_ref, wr_ref, br_ref,
                x1_ref, x1p_ref, ri_ref, rw_ref, cnt_ref, carry_ref):
    i = pl.program_id(0)
    tm = x_ref.shape[0]

    @pl.when(i == 0)
    def _():
        carry_ref[...] = jnp.zeros_like(carry_ref)

    ya = jnp.dot(ha_ref[...], wa_ref[...], preferred_element_type=F32)
    yb = jnp.dot(ob_ref[...], wb_ref[...], preferred_element_type=F32)
    merged = _sigmoid(ga_ref[...].astype(F32)) * ya + _sigmoid(gb_ref[...].astype(F32)) * yb
    mix = jnp.dot(merged.astype(BF16), wo_ref[...], preferred_element_type=F32)
    pre = DEEPNORM_ALPHA * x_ref[...] + mix
    mu = jnp.mean(pre, axis=-1, keepdims=True)
    pc = pre - mu
    var = jnp.mean(pc * pc, axis=-1, keepdims=True)
    x1 = pc * lax.rsqrt(var + LN_EPS) * g1_ref[...] + b1_ref[...]
    x1_ref[...] = x1
    x1p_ref[...] = _pack_halves(x1)

    logits = _dot_3pass(x1, wr_ref[...]) + br_ref[...]
    lt = logits.T
    row8 = lax.broadcasted_iota(I32, (SUBLANES, tm), 0).astype(F32)
    gl = jnp.where(row8 < N_GROUPS, lt[0:SUBLANES, :], NEG_BIG)
    gmax = jnp.max(gl, axis=0, keepdims=True)
    gsel = jnp.min(jnp.where(gl == gmax, row8, float(SUBLANES)), axis=0, keepdims=True)
    gprob = 1.0 / jnp.sum(jnp.exp(gl - gmax), axis=0, keepdims=True)
    ing = lt[SUBLANES:2 * SUBLANES, :]
    for g in range(1, N_GROUPS):
        ing = jnp.where(gsel == g, lt[(g + 1) * SUBLANES:(g + 2) * SUBLANES, :], ing)
    v0 = jnp.max(ing, axis=0, keepdims=True)
    i0 = jnp.min(jnp.where(ing == v0, row8, float(SUBLANES)), axis=0, keepdims=True)
    ing2 = jnp.where(row8 == i0, -jnp.inf, ing)
    v1 = jnp.max(ing2, axis=0, keepdims=True)
    i1 = jnp.min(jnp.where(ing2 == v1, row8, float(SUBLANES)), axis=0, keepdims=True)
    ex = jnp.exp(v1 - v0)
    inv = 1.0 / (1.0 + ex)
    w0 = gprob * inv
    w1 = gprob * (ex * inv)
    e0 = (gsel * EXPERTS_PER_GROUP + i0).astype(I32)
    e1 = (gsel * EXPERTS_PER_GROUP + i1).astype(I32)

    rowe = lax.broadcasted_iota(I32, (N_EXPERTS, tm), 0)
    is0 = rowe == e0
    is1 = rowe == e1
    oh = jnp.where(is0, 1.0, 0.0) + jnp.where(is1, 1.0, 0.0)
    tr = lax.broadcasted_iota(I32, (tm, tm), 0)
    tc = lax.broadcasted_iota(I32, (tm, tm), 1)
    su = jnp.where(tr < tc, 1.0, 0.0).astype(BF16)
    tot = jnp.dot(oh.astype(BF16), su, preferred_element_type=F32) + carry_ref[...]
    rank0 = jnp.sum(jnp.where(is0, tot, 0.0), axis=0, keepdims=True)
    rank1 = jnp.sum(jnp.where(is1, tot, 0.0), axis=0, keepdims=True)
    carry = carry_ref[...] + jnp.sum(oh, axis=1, keepdims=True)
    carry_ref[...] = carry
    cnt_ref[...] = jnp.broadcast_to(carry, cnt_ref.shape).astype(I32)

    zi = jnp.zeros((1, tm), I32)
    ri_ref[...] = jnp.concatenate([e0, e1, rank0.astype(I32), rank1.astype(I32), zi, zi, zi, zi], axis=0)
    zf = jnp.zeros((1, tm), F32)
    rw_ref[...] = jnp.concatenate([w0, w1, zf, zf, zf, zf, zf, zf], axis=0)


def _mix(h_a, o_b, z_main, x2d, w_a, w_b, w_out, ln_g, ln_b, w_r, b_r):
    T, D = x2d.shape
    tm = min(MIX_TM, T)
    const = lambda shape: pl.BlockSpec(shape, lambda i: (0, 0), pipeline_mode=pl.Buffered(1))
    return pl.pallas_call(
        _mix_kernel,
        grid=(T // tm,),
        in_specs=[
            pl.BlockSpec((tm, A_WIDTH), lambda i: (i, 0)),
            pl.BlockSpec((tm, B_WIDTH), lambda i: (i, 0)),
            pl.BlockSpec((tm, D), lambda i: (i, COL_GA // D)),
            pl.BlockSpec((tm, D), lambda i: (i, COL_GB // D)),
            pl.BlockSpec((tm, D), lambda i: (i, 0)),
            const((A_WIDTH, D)), const((B_WIDTH, D)), const((D, D)),
            const((1, D)), const((1, D)), const((D, LANES)), const((1, LANES)),
        ],
        out_specs=[
            pl.BlockSpec((tm, D), lambda i: (i, 0)),
            pl.BlockSpec((tm, D // 2), lambda i: (i, 0)),
            pl.BlockSpec((SUBLANES, tm), lambda i: (0, i)),
            pl.BlockSpec((SUBLANES, tm), lambda i: (0, i)),
            pl.BlockSpec((N_EXPERTS, LANES), lambda i: (0, 0)),
        ],
        out_shape=[
            jax.ShapeDtypeStruct((T, D), F32),
            jax.ShapeDtypeStruct((T, D // 2), U32),
            jax.ShapeDtypeStruct((SUBLANES, T), I32),
            jax.ShapeDtypeStruct((SUBLANES, T), F32),
            jax.ShapeDtypeStruct((N_EXPERTS, LANES), I32),
        ],
        scratch_shapes=[pltpu.VMEM((N_EXPERTS, 1), F32)],
        compiler_params=_cparams(("arbitrary",)),
        name="mix_ln1_router",
    )(h_a, o_b, z_main, z_main, x2d, w_a, w_b, w_out, ln_g, ln_b, w_r, b_r)


def _row_copy(src_ref, src_row, dst_ref, dst_row, sem):
    return pltpu.make_async_copy(src_ref.at[pl.ds(src_row, 1), :], dst_ref.at[pl.ds(dst_row, 1), :], sem)


def _rows_wait(src_ref, dst_ref, rows, sem):
    pltpu.make_async_copy(src_ref.at[pl.ds(0, rows), :], dst_ref.at[pl.ds(0, rows), :], sem).wait()


def _dispatch_kernel(cnt_ref, ri_ref, x1_ref, xpad_ref, pstart_ref, bexp_ref, nused_ref,
                     pst_ref, zero_ref, sem, zsem, *, tb, n_blocks, tm):
    i = pl.program_id(0)
    last = pl.num_programs(0) - 1

    @pl.when(i == 0)
    def _():
        def fill(b, carry):
            bexp_ref[b] = 0
            return carry
        lax.fori_loop(0, n_blocks, fill, 0)

        def per_expert(e, blk):
            pst_ref[e] = blk * tb
            pstart_ref[e] = blk * tb
            nb = (cnt_ref[e] + (tb - 1)) // tb

            def mark(b, carry):
                bexp_ref[blk + b] = e
                return carry
            lax.fori_loop(0, nb, mark, 0)
            return blk + nb
        nused = lax.fori_loop(0, N_EXPERTS, per_expert, 0)
        nused_ref[0] = nused
        pst_ref[N_EXPERTS] = nused

    def issue(t, carry):
        for k in range(TOP_K):
            pos = pst_ref[ri_ref[k, t]] + ri_ref[TOP_K + k, t]
            _row_copy(x1_ref, t, xpad_ref, pos, sem).start()
        return carry
    lax.fori_loop(0, tm, issue, 0, unroll=8)
    for k in range(TOP_K):
        _rows_wait(x1_ref, xpad_ref, tm, sem)

    @pl.when(i == last)
    def _():
        zero_ref[...] = jnp.zeros_like(zero_ref)

        def per_expert(e, carry):
            cnt = cnt_ref[e]
            first = pst_ref[e] + cnt
            npad = ((cnt + (tb - 1)) // tb) * tb - cnt

            def start(p, c2):
                _row_copy(zero_ref, 0, xpad_ref, first + p, zsem).start()
                return c2
            lax.fori_loop(0, npad, start, 0)

            def wait(p, c2):
                _row_copy(zero_ref, 0, xpad_ref, 0, zsem).wait()
                return c2
            lax.fori_loop(0, npad, wait, 0)
            return carry
        lax.fori_loop(0, N_EXPERTS, per_expert, 0)

        def block_copy(b):
            return pltpu.make_async_copy(zero_ref, xpad_ref.at[pl.ds(pl.multiple_of(b * tb, tb), tb), :], zsem)

        def start_block(b, carry):
            block_copy(b).start()
            return carry
        lax.fori_loop(pst_ref[N_EXPERTS], n_blocks, start_block, 0)

        def wait_block(b, carry):
            block_copy(b).wait()
            return carry
        lax.fori_loop(pst_ref[N_EXPERTS], n_blocks, wait_block, 0)


def _dispatch(counts, route_i, x1, tb, n_blocks):
    T, D = x1.shape
    tm = min(DISPATCH_TM, T)
    smem_full = pl.BlockSpec(memory_space=pltpu.SMEM)
    return pl.pallas_call(
        functools.partial(_dispatch_kernel, tb=tb, n_blocks=n_blocks, tm=tm),
        grid=(T // tm,),
        in_specs=[
            smem_full,
            pl.BlockSpec((SUBLANES, tm), lambda i: (0, i), memory_space=pltpu.SMEM),
            pl.BlockSpec((tm, D), lambda i: (i, 0)),
        ],
        out_specs=[
            pl.BlockSpec(memory_space=pl.ANY),
            smem_full, smem_full, smem_full,
        ],
        out_shape=[
            jax.ShapeDtypeStruct((n_blocks * tb, D), x1.dtype),
            jax.ShapeDtypeStruct((N_EXPERTS,), I32),
            jax.ShapeDtypeStruct((n_blocks,), I32),
            jax.ShapeDtypeStruct((1,), I32),
        ],
        scratch_shapes=[
            pltpu.SMEM((N_EXPERTS + 1,), I32),
            pltpu.VMEM((tb, D), x1.dtype),
            pltpu.SemaphoreType.DMA(()),
            pltpu.SemaphoreType.DMA(()),
        ],
        compiler_params=_cparams(("arbitrary",)),
        name="moe_dispatch",
    )(counts, route_i, x1)


def _expert_kernel(bexp_ref, nused_ref, x_ref, wg_hbm, wu_hbm, wd_hbm, y_ref,
                   wgf_ref, wuf_ref, wdf_ref, wgb_ref, wub_ref, wdb_ref, sem, run_ref):
    b = pl.program_id(0)
    nused = nused_ref[0]
    used = b < nused
    e = bexp_ref[b]
    new_expert = jnp.logical_or(b == 0, e != bexp_ref[jnp.maximum(b - 1, 0)])
    streams = ((wg_hbm, wgf_ref), (wu_hbm, wuf_ref), (wd_hbm, wdf_ref))

    def weight_copies(expert, slot):
        return [pltpu.make_async_copy(w_hbm.at[expert], wf_ref.at[slot], sem.at[slot, j])
                for j, (w_hbm, wf_ref) in enumerate(streams)]

    @pl.when(jnp.logical_and(used, b == 0))
    def _():
        run_ref[0] = 0
        for cp in weight_copies(e, 0):
            cp.start()

    @pl.when(jnp.logical_and(used, new_expert))
    def _():
        slot = run_ref[0] % 2
        nxt = lax.while_loop(lambda j: jnp.logical_and(j < nused, bexp_ref[jnp.minimum(j, nused - 1)] == e),
                             lambda j: j + 1, b + 1)

        @pl.when(nxt < nused)
        def _():
            for cp in weight_copies(bexp_ref[jnp.minimum(nxt, nused - 1)], 1 - slot):
                cp.start()

        for cp in weight_copies(e, slot):
            cp.wait()
        wgb_ref[...] = wgf_ref[slot].astype(BF16)
        wub_ref[...] = wuf_ref[slot].astype(BF16)
        wdb_ref[...] = wdf_ref[slot].astype(BF16)
        run_ref[0] = run_ref[0] + 1

    @pl.when(used)
    def _():
        half = x_ref.shape[1]
        x_lo, x_hi = (t.astype(BF16) for t in _unpack_halves(x_ref[...]))
        dot = functools.partial(jnp.dot, preferred_element_type=F32)
        gate = dot(x_lo, wgb_ref[:half, :]) + dot(x_hi, wgb_ref[half:, :])
        up = dot(x_lo, wub_ref[:half, :]) + dot(x_hi, wub_ref[half:, :])
        hid = (gate * _sigmoid(gate) * up).astype(BF16)
        y_ref[...] = _pack_halves(dot(hid, wdb_ref[...]))

    @pl.when(jnp.logical_not(used))
    def _():
        y_ref[...] = jnp.zeros_like(y_ref)


def _experts(bexp, nused, x_pad, w_gate, w_up, w_down, tb):
    P, Dh = x_pad.shape
    D = 2 * Dh
    n_blocks = P // tb
    F = w_gate.shape[-1]
    blk = lambda b, be, nu: jnp.maximum(jnp.minimum(b, nu[0] - 1), 0)
    hbm = pl.BlockSpec(memory_space=pl.ANY)
    return pl.pallas_call(
        _expert_kernel,
        grid_spec=pltpu.PrefetchScalarGridSpec(
            num_scalar_prefetch=2,
            grid=(n_blocks,),
            in_specs=[pl.BlockSpec((tb, Dh), lambda b, be, nu: (blk(b, be, nu), 0)), hbm, hbm, hbm],
            out_specs=pl.BlockSpec((tb, Dh), lambda b, be, nu: (b, 0)),
            scratch_shapes=[
                pltpu.VMEM((2, D, F), F32), pltpu.VMEM((2, D, F), F32), pltpu.VMEM((2, F, D), F32),
                pltpu.VMEM((D, F), BF16), pltpu.VMEM((D, F), BF16), pltpu.VMEM((F, D), BF16),
                pltpu.SemaphoreType.DMA((2, 3)),
                pltpu.SMEM((1,), I32),
            ],
        ),
        out_shape=jax.ShapeDtypeStruct((P, Dh), U32),
        compiler_params=_cparams(("arbitrary",)),
        name="moe_experts",
    )(bexp, nused, x_pad, w_gate, w_up, w_down)


def _combine_kernel(pstart_ref, ri_ref, rin_ref, x1_ref, rw_ref, g_ref, b_ref, ypad_ref, o_ref, ybuf_ref, sem):
    i = pl.program_id(0)
    tm = x1_ref.shape[0]
    slot = i % 2

    def gather(route_ref, s):
        def issue(t, carry):
            for k in range(TOP_K):
                pos = pstart_ref[route_ref[k, t]] + route_ref[TOP_K + k, t]
                _row_copy(ypad_ref, pos, ybuf_ref.at[s, k], t, sem.at[s]).start()
            return carry
        lax.fori_loop(0, tm, issue, 0, unroll=8)

    @pl.when(i == 0)
    def _():
        gather(ri_ref, 0)

    @pl.when(i + 1 < pl.num_programs(0))
    def _():
        gather(rin_ref, 1 - slot)

    for k in range(TOP_K):
        _rows_wait(ypad_ref, ybuf_ref.at[slot, k], tm, sem.at[slot])

    rw = rw_ref[...]
    y0 = _unpack_halves(ybuf_ref[slot, 0])
    y1 = _unpack_halves(ybuf_ref[slot, 1])
    ffn = jnp.concatenate([y0[h] * rw[:, 0:1] + y1[h] * rw[:, 1:2] for h in range(2)], axis=1)
    pre = DEEPNORM_ALPHA * x1_ref[...] + ffn
    mu = jnp.mean(pre, axis=-1, keepdims=True)
    pc = pre - mu
    var = jnp.mean(pc * pc, axis=-1, keepdims=True)
    o_ref[...] = pc * lax.rsqrt(var + LN_EPS) * g_ref[...] + b_ref[...]


def _combine(pstart, route_i, x1, rw_col, ln_g, ln_b, y_pad):
    T, D = x1.shape
    tm = min(COMBINE_TM, T)
    n = T // tm
    return pl.pallas_call(
        _combine_kernel,
        grid=(n,),
        in_specs=[
            pl.BlockSpec(memory_space=pltpu.SMEM),
            pl.BlockSpec((SUBLANES, tm), lambda i: (0, i), memory_space=pltpu.SMEM),
            pl.BlockSpec((SUBLANES, tm), lambda i: (0, jnp.minimum(i + 1, n - 1)), memory_space=pltpu.SMEM),
            pl.BlockSpec((tm, D), lambda i: (i, 0)),
            pl.BlockSpec((tm, SUBLANES), lambda i: (i, 0)),
            pl.BlockSpec((1, D), lambda i: (0, 0)),
            pl.BlockSpec((1, D), lambda i: (0, 0)),
            pl.BlockSpec(memory_space=pl.ANY),
        ],
        out_specs=pl.BlockSpec((tm, D), lambda i: (i, 0)),
        out_shape=jax.ShapeDtypeStruct((T, D), F32),
        scratch_shapes=[
            pltpu.VMEM((2, TOP_K, tm, D // 2), U32),
            pltpu.SemaphoreType.DMA((2,)),
        ],
        compiler_params=_cparams(("arbitrary",)),
        name="moe_combine_ln2",
    )(pstart, route_i, route_i, x1, rw_col, ln_g, ln_b, y_pad)


N_GATE_COLS = 2 * A_HEADS
SRC_MLSTM = 0
SRC_GATES = 4 * A_WIDTH
SRC_DIFF = SRC_GATES + N_GATE_COLS
SRC_MERGE = SRC_DIFF + 3 * B_WIDTH
PREP_TN = 512


def _wprep_kernel(a_ref, o_ref):
    o_ref[...] = a_ref[...].T.astype(BF16)


def _wgate_kernel(a_ref, o_ref):
    a = a_ref[...]
    rows = jnp.concatenate([a, jnp.zeros((LANES - a.shape[0], a.shape[1]), F32)], axis=0)
    o_ref[...] = rows.T


def _rearrange_in_proj(w_in, b_in):
    K, n_in = w_in.shape
    tn = PREP_TN
    merge_blocks = (2 * D_MODEL) // tn
    mlstm_blocks = (4 * A_WIDTH) // tn
    assert SRC_MERGE % SUBLANES == 0 and SRC_DIFF % SUBLANES == 0

    def src_row(jb):
        s = SUBLANES
        merge = SRC_MERGE // s + (tn // s) * jb
        mlstm = SRC_MLSTM // s + (tn // s) * (jb - merge_blocks)
        diff = SRC_DIFF // s + (tn // s) * (jb - merge_blocks - mlstm_blocks)
        return s * jnp.where(jb < merge_blocks, merge, jnp.where(jb < merge_blocks + mlstm_blocks, mlstm, diff))

    w_t = jnp.swapaxes(w_in, 0, 1)
    w_main = pl.pallas_call(
        _wprep_kernel,
        grid=(N_MAIN // tn,),
        in_specs=[pl.BlockSpec((pl.Element(tn), pl.Element(K)), lambda jb: (src_row(jb), 0))],
        out_specs=pl.BlockSpec((K, tn), lambda jb: (0, jb)),
        out_shape=jax.ShapeDtypeStruct((K, N_MAIN), BF16),
        compiler_params=_cparams(("parallel",)),
        name="w_in_prep",
    )(w_t)
    b_main = jnp.concatenate([b_in[SRC_MERGE:], b_in[SRC_MLSTM:SRC_GATES], b_in[SRC_DIFF:SRC_MERGE]])[None, :]
    w_gate = pl.pallas_call(
        _wgate_kernel,
        grid=(1,),
        in_specs=[pl.BlockSpec((pl.Element(N_GATE_COLS), pl.Element(K)), lambda i: (SRC_GATES, 0))],
        out_specs=pl.BlockSpec((K, LANES), lambda i: (0, 0)),
        out_shape=jax.ShapeDtypeStruct((K, LANES), F32),
        name="w_gate_prep",
    )(w_t)
    b_gate = jnp.pad(b_in[SRC_GATES:SRC_DIFF], (0, LANES - N_GATE_COLS))[None, :]
    return w_main, b_main, w_gate, b_gate


def _layer(x, w_in, b_in, conv_w, conv_b, norm_a_g, lq1, lk1, lq2, lk2, norm_b_g, w_a, w_b, w_out,
           ln1_g, ln1_b, w_grp, b_grp, w_exp, b_exp, w_gate, w_up, w_down, ln2_g, ln2_b, lambda_init):
    B, S, D = x.shape
    T = B * S
    x2d = x.reshape(T, D)

    w_main, b_main, w_g, b_g = _rearrange_in_proj(w_in, b_in)
    z_main, z_gate = _in_projection(x2d, w_main, b_main, w_g, b_g)

    L = min(MLSTM_CHUNK, S)
    gp = _gate_prep(z_gate[:, :2 * A_HEADS].T, L)
    h_a = _mlstm(z_main, gp.T, gp, conv_w, conv_b[None, :], norm_a_g[None, :], B, S)
    o_b = _diff_attention(z_main, lq1[None, :], lk1[None, :], lq2[None, :], lk2[None, :],
                          norm_b_g[None, :], B, S, lambda_init)

    w_r = jnp.zeros((D, LANES), F32).at[:, :N_GROUPS].set(w_grp).at[:, SUBLANES:SUBLANES + N_EXPERTS].set(w_exp)
    b_r = jnp.zeros((LANES,), F32).at[:N_GROUPS].set(b_grp).at[SUBLANES:SUBLANES + N_EXPERTS].set(b_exp)[None, :]
    x1, x1_packed, route_i, route_w, counts = _mix(h_a, o_b, z_main, x2d, w_a.astype(BF16), w_b.astype(BF16),
                                                   w_out.astype(BF16), ln1_g[None, :], ln1_b[None, :], w_r, b_r)

    tb = MOE_TB
    n_blocks = (T * TOP_K) // tb + N_EXPERTS
    x_pad, pstart, bexp, nused = _dispatch(counts[:, 0], route_i, x1_packed, tb, n_blocks)
    y_pad = _experts(bexp, nused, x_pad, w_gate, w_up, w_down, tb)
    out = _combine(pstart, route_i, x1, route_w.T, ln2_g[None, :], ln2_b[None, :], y_pad)
    return out.reshape(B, S, D)


def kernel(x, w_in, b_in, conv_w, conv_b, mlstm_norm_g, lambda_q1, lambda_k1, lambda_q2, lambda_k2,
           diff_norm_g, w_a, w_b, w_out, ln1_g, ln1_b, w_grp, b_grp, w_exp, b_exp,
           w_gate, w_up, w_down, ln2_g, ln2_b):
    for l in range(DEPTH):
        lambda_init = 0.8 - 0.6 * math.exp(-0.3 * l)
        x = _layer(x, w_in[l], b_in[l], conv_w[l], conv_b[l], mlstm_norm_g[l], lambda_q1[l], lambda_k1[l],
                   lambda_q2[l], lambda_k2[l], diff_norm_g[l], w_a[l], w_b[l], w_out[l], ln1_g[l], ln1_b[l],
                   w_grp[l], b_grp[l], w_exp[l], b_exp[l], w_gate[l], w_up[l], w_down[l], ln2_g[l], ln2_b[l],
                   lambda_init)
    return x
```

```python
import functools
import math

import jax
import jax.numpy as jnp
from jax import lax
from jax.experimental import pallas as pl
from jax.experimental.pallas import tpu as pltpu

F32 = jnp.float32
BF16 = jnp.bfloat16
I32 = jnp.int32

D_MODEL = 2048
A_HEADS = 4
A_HEAD_DIM = 256
A_WIDTH = A_HEADS * A_HEAD_DIM
CONV_WIDTH = 4
B_HEADS = 8
B_HEAD_DIM = 64
B_V_DIM = 2 * B_HEAD_DIM
B_WIDTH = B_HEADS * B_V_DIM
ATTN_CHUNK = 64
N_GROUPS = 4
EXPERTS_PER_GROUP = 8
N_EXPERTS = N_GROUPS * EXPERTS_PER_GROUP
TOP_K = 2
D_EXPERT = 512
DEPTH = 1
DEEPNORM_ALPHA = (2 * DEPTH) ** 0.25
LN_EPS = 1e-5
NEG_BIG = -1e30
LOG2_E = 1.4426950408889634

LANES = 128
SUBLANES = 8
VMEM_LIMIT_BYTES = 56 * 1024 * 1024

COL_GA = 0
COL_GB = COL_GA + D_MODEL
COL_QA = COL_GB + D_MODEL
COL_KA = COL_QA + A_WIDTH
COL_VA = COL_KA + A_WIDTH
COL_OA = COL_VA + A_WIDTH
COL_QB = COL_OA + A_WIDTH
COL_KB = COL_QB + B_WIDTH
COL_VB = COL_KB + B_WIDTH
N_MAIN = COL_VB + B_WIDTH

PROJ_TM, PROJ_TN = 1024, 1024
MLSTM_CHUNK = 256
ATTN_TQ = 512
MIX_TM = 256
MOE_TB = 256
DISPATCH_TM = 256
COMBINE_TM = 256


def _cparams(sem, vmem=VMEM_LIMIT_BYTES):
    return pltpu.CompilerParams(dimension_semantics=sem, vmem_limit_bytes=vmem)


def _sigmoid(x):
    return 1.0 / (1.0 + jnp.exp(-x))


U32 = jnp.uint32


def _pack_halves(x):
    n = x.shape[1] // 2
    lo = lax.bitcast_convert_type(x[:, :n].astype(BF16).astype(F32), U32)
    hi = lax.bitcast_convert_type(x[:, n:].astype(BF16).astype(F32), U32)
    return (lo >> 16) | hi


def _unpack_halves(u):
    lo = lax.bitcast_convert_type(u << 16, F32)
    hi = lax.bitcast_convert_type(u & jnp.uint32(0xFFFF0000), F32)
    return lo, hi


TILE_WORDS = SUBLANES * LANES


def _rows_to_tiles(ref, v):
    rows = v.shape[0]
    for j in range(SUBLANES):
        ref[pl.ds(j, rows, stride=SUBLANES), :] = v[:, j * LANES:(j + 1) * LANES]


def _tiles_to_rows(ref):
    rows = ref.shape[0] // SUBLANES
    return jnp.concatenate([ref[pl.ds(j, rows, stride=SUBLANES), :] for j in range(SUBLANES)], axis=1)


def _split_bf16(a):
    hi = a.astype(BF16)
    return hi, (a - hi.astype(F32)).astype(BF16)


def _dot_3pass(a, b):
    a_hi, a_lo = _split_bf16(a)
    b_hi, b_lo = _split_bf16(b)
    dot = functools.partial(jnp.dot, preferred_element_type=F32)
    return dot(a_hi, b_hi) + (dot(a_hi, b_lo) + dot(a_lo, b_hi))


def _proj_kernel(x_ref, w_ref, b_ref, wg_ref, bg_ref, z_ref, zg_ref, xb_ref):
    @pl.when(pl.program_id(1) == 0)
    def _():
        x = x_ref[...]
        xb_ref[...] = x.astype(BF16)
        zg_ref[...] = _dot_3pass(x, wg_ref[...]) + bg_ref[...]

    acc = jnp.dot(xb_ref[...], w_ref[...], preferred_element_type=F32)
    z_ref[...] = (acc + b_ref[...]).astype(z_ref.dtype)


def _in_projection(x2d, w_main, b_main, w_gate, b_gate):
    T, K = x2d.shape
    N = w_main.shape[1]
    tm, tn = min(PROJ_TM, T), PROJ_TN
    return pl.pallas_call(
        _proj_kernel,
        grid=(T // tm, N // tn),
        in_specs=[
            pl.BlockSpec((tm, K), lambda i, j: (i, 0)),
            pl.BlockSpec((K, tn), lambda i, j: (0, j)),
            pl.BlockSpec((1, tn), lambda i, j: (0, j)),
            pl.BlockSpec((K, LANES), lambda i, j: (0, 0)),
            pl.BlockSpec((1, LANES), lambda i, j: (0, 0)),
        ],
        out_specs=[
            pl.BlockSpec((tm, tn), lambda i, j: (i, j)),
            pl.BlockSpec((tm, LANES), lambda i, j: (i, 0)),
        ],
        out_shape=[
            jax.ShapeDtypeStruct((T, N), BF16),
            jax.ShapeDtypeStruct((T, LANES), F32),
        ],
        scratch_shapes=[pltpu.VMEM((tm, K), BF16)],
        compiler_params=_cparams(("parallel", "arbitrary")),
        name="in_projection",
    )(x2d, w_main, b_main, w_gate, b_gate)


def _gate_prep_kernel(g_ref, o_ref, *, L):
    r = lax.broadcasted_iota(I32, (L, L), 0)
    c = lax.broadcasted_iota(I32, (L, L), 1)
    tri = (r <= c).astype(F32)
    row = lax.broadcasted_iota(I32, (g_ref.shape[0], L), 0)
    for ch in range(g_ref.shape[1] // L):
        g = g_ref[:, ch * L:(ch + 1) * L]
        lf = jnp.minimum(g, 0.0) - jnp.log(1.0 + jnp.exp(-jnp.abs(g)))
        b = jnp.dot(lf, tri, preferred_element_type=F32, precision=lax.Precision.HIGHEST)
        o_ref[:, ch * L:(ch + 1) * L] = jnp.where(row < A_HEADS, g, b)


def _gate_prep(g_rows, L):
    R, T = g_rows.shape
    per_step = math.gcd(T // L, 8)
    W = L * per_step
    return pl.pallas_call(
        functools.partial(_gate_prep_kernel, L=L),
        grid=(T // W,),
        in_specs=[pl.BlockSpec((R, W), lambda i: (0, i))],
        out_specs=pl.BlockSpec((R, W), lambda i: (0, i)),
        out_shape=jax.ShapeDtypeStruct((R, T), F32),
        compiler_params=_cparams(("parallel",)),
        name="gate_prep",
    )(g_rows)


def _mlstm_kernel(q_ref, k_ref, v_ref, o_ref, gcol_ref, grow_ref, cw_ref, cb_ref, ng_ref, out_ref,
                  c_ref, n_ref, m_ref, qcar_ref, kcar_ref):
    ci = pl.program_id(1)
    L = q_ref.shape[0]
    dh = A_HEAD_DIM

    @pl.when(ci == 0)
    def _():
        c_ref[...] = jnp.zeros_like(c_ref)
        n_ref[...] = jnp.zeros_like(n_ref)
        m_ref[...] = jnp.zeros_like(m_ref)
        qcar_ref[...] = jnp.zeros_like(qcar_ref)
        kcar_ref[...] = jnp.zeros_like(kcar_ref)

    def conv_silu(u_ref, car_ref, lo, wcol):
        u = u_ref[:, lo:lo + dh].astype(F32)
        ext = jnp.concatenate([car_ref[:, lo:lo + dh], u], axis=0)
        w = cw_ref[:, wcol:wcol + dh]
        y = cb_ref[:, wcol:wcol + dh] + w[3:4, :] * u
        for j in range(CONV_WIDTH - 1):
            off = SUBLANES - (CONV_WIDTH - 1) + j
            y = y + w[j:j + 1, :] * ext[off:off + L, :]
        car_ref[:, lo:lo + dh] = u[L - SUBLANES:, :]
        return y * _sigmoid(y)

    r = lax.broadcasted_iota(I32, (L, L), 0)
    c = lax.broadcasted_iota(I32, (L, L), 1)
    causal = r >= c
    gcol = gcol_ref[...]
    grow = grow_ref[...]

    for h in range(A_HEADS):
        lo = h * dh
        q = conv_silu(q_ref, qcar_ref, lo, lo)
        k = conv_silu(k_ref, kcar_ref, lo, A_WIDTH + lo) * (dh ** -0.5)
        v_b = v_ref[:, lo:lo + dh]
        q_b = q.astype(BF16)
        k_b = k.astype(BF16)

        i_col, b_col = gcol[:, h:h + 1], gcol[:, A_HEADS + h:A_HEADS + h + 1]
        i_row, b_row = grow[h:h + 1, :], grow[A_HEADS + h:A_HEADS + h + 1, :]
        b_last = b_row[:, L - 1:L]
        m_prev = m_ref[h]

        dmat = jnp.where(causal, b_col - b_row + i_row, NEG_BIG)
        inter = b_col + m_prev
        m_t = jnp.maximum(inter, jnp.max(dmat, axis=-1, keepdims=True))
        s = lax.dot_general(q_b, k_b, (((1,), (1,)), ((), ())), preferred_element_type=F32)
        w_intra = jnp.exp(dmat - m_t) * s
        w_inter = jnp.exp(inter - m_t)
        qc = jnp.dot(q_b, c_ref[h].astype(BF16), preferred_element_type=F32)
        num = w_inter * qc + jnp.dot(w_intra.astype(BF16), v_b, preferred_element_type=F32)
        qn = jnp.sum(q * n_ref[h], axis=-1, keepdims=True)
        den = w_inter * qn + jnp.sum(w_intra, axis=-1, keepdims=True)
        hh = num / jnp.maximum(jnp.abs(den), jnp.exp(-m_t))

        g_col = b_last - b_col + i_col
        m_new = jnp.maximum(b_last + m_prev, jnp.max(g_col, axis=0, keepdims=True))
        decay = jnp.exp(b_last + m_prev - m_new)
        kw = jnp.exp(g_col - m_new) * k
        c_ref[h] = decay * c_ref[h] + jnp.dot(kw.T.astype(BF16), v_b, preferred_element_type=F32)
        n_ref[h] = decay * n_ref[h] + jnp.sum(kw, axis=0, keepdims=True)
        m_ref[h] = m_new

        y = _sigmoid(o_ref[:, lo:lo + dh].astype(F32)) * hh
        mu = jnp.mean(y, axis=-1, keepdims=True)
        yc = y - mu
        var = jnp.mean(yc * yc, axis=-1, keepdims=True)
        out_ref[:, lo:lo + dh] = (yc * lax.rsqrt(var + LN_EPS) * ng_ref[:, lo:lo + dh]).astype(out_ref.dtype)


def _mlstm(z_main, gcol, grow, conv_w, conv_b, norm_g, B, S):
    L = min(MLSTM_CHUNK, S)
    nc = S // L
    dh = A_HEAD_DIM
    H = A_HEADS
    W = A_WIDTH
    row = lambda b, c: b * nc + c
    full = lambda shape: pl.BlockSpec(shape, lambda b, c: (0, 0))
    return pl.pallas_call(
        _mlstm_kernel,
        grid=(B, nc),
        in_specs=[
            pl.BlockSpec((L, W), lambda b, c: (row(b, c), COL_QA // W)),
            pl.BlockSpec((L, W), lambda b, c: (row(b, c), COL_KA // W)),
            pl.BlockSpec((L, W), lambda b, c: (row(b, c), COL_VA // W)),
            pl.BlockSpec((L, W), lambda b, c: (row(b, c), COL_OA // W)),
            pl.BlockSpec((L, SUBLANES), lambda b, c: (row(b, c), 0)),
            pl.BlockSpec((SUBLANES, L), lambda b, c: (0, row(b, c))),
            full((CONV_WIDTH, 2 * W)), full((1, 2 * W)), full((1, W)),
        ],
        out_specs=pl.BlockSpec((L, W), lambda b, c: (row(b, c), 0)),
        out_shape=jax.ShapeDtypeStruct((B * S, W), BF16),
        scratch_shapes=[
            pltpu.VMEM((H, dh, dh), F32),
            pltpu.VMEM((H, 1, dh), F32),
            pltpu.VMEM((H, 1, 1), F32),
            pltpu.VMEM((SUBLANES, W), F32),
            pltpu.VMEM((SUBLANES, W), F32),
        ],
        compiler_params=_cparams(("parallel", "arbitrary")),
        name="mlstm",
    )(z_main, z_main, z_main, z_main, gcol, grow, conv_w, conv_b, norm_g)


def _lane_tiles(t):
    return [t[:, i * LANES:(i + 1) * LANES] for i in range(t.shape[1] // LANES)]


def _attn_kernel(slope_ref, q_ref, k_ref, kb_ref, v_ref, lq1_ref, lk1_ref, lq2_ref, lk2_ref, g_ref, o_ref,
                 s_ref, mx_ref, ls_ref, acc_ref, corr_ref, *, lambda_init):
    h = pl.program_id(1)
    qi = pl.program_id(2)
    tq = q_ref.shape[0]
    tk = tq
    slope = slope_ref[h]
    nt = (((1,), (1,)), ((), ()))

    lane = lax.broadcasted_iota(I32, (tq, B_V_DIM), 1)
    q = q_ref[...] * (B_HEAD_DIM ** -0.5)
    zero = jnp.zeros_like(q)
    bias_cols = jnp.where(lane < 2, 1.0, 0.0).astype(BF16)
    qa = (jnp.concatenate([jnp.where(lane < B_HEAD_DIM, q, zero), bias_cols], axis=1),
          jnp.concatenate([jnp.where(lane >= B_HEAD_DIM, q, zero), bias_cols], axis=1))

    def scores(j):
        start = pl.multiple_of(j * tk, tk)
        ka = jnp.concatenate([k_ref[pl.ds(start, tk), :], kb_ref[pl.ds(start, tk), :]], axis=1)
        return [lax.dot_general(qa[n], ka, nt, preferred_element_type=F32) for n in range(2)]

    def keep(j, n, t):
        t = t * LOG2_E
        s_ref[n, j] = t
        mx_ref[n] = functools.reduce(jnp.maximum, _lane_tiles(t), mx_ref[n])

    mx_ref[...] = jnp.full(mx_ref.shape, NEG_BIG, F32)

    def pass_a(j):
        t = scores(j)
        for n in range(2):
            keep(j, n, t[n])

    def pass_a_pair(jj, carry):
        pass_a(2 * jj)
        pass_a(2 * jj + 1)
        return carry
    lax.fori_loop(0, qi // 2, pass_a_pair, 0)

    @pl.when(qi % 2 == 1)
    def _():
        pass_a(qi - 1)

    @pl.when(qi == 0)
    def _():
        r = lax.broadcasted_iota(I32, (tq, tk), 0)
        c = lax.broadcasted_iota(I32, (tq, tk), 1)
        ahead = jnp.where(c <= r, 0.0, (r - c).astype(F32) * (2.0 * slope))
        corr_ref[...] = jnp.where((c // ATTN_CHUNK) <= (r // ATTN_CHUNK), ahead, NEG_BIG)

    t = scores(qi)
    for n in range(2):
        keep(qi, n, t[n] + corr_ref[...])

    for n in range(2):
        m = jnp.max(mx_ref[n], axis=-1, keepdims=True)
        mx_ref[n] = jnp.broadcast_to(m, (tq, LANES))
    ls_ref[...] = jnp.zeros_like(ls_ref)
    acc_ref[...] = jnp.zeros_like(acc_ref)

    def pass_b(j, carry):
        start = pl.multiple_of(j * tk, tk)
        vj = v_ref[pl.ds(start, tk), :]
        for n in range(2):
            mb = mx_ref[n]
            ps = [jnp.exp2(t - mb) for t in _lane_tiles(s_ref[n, j])]
            ls_ref[n] += functools.reduce(jnp.add, ps)
            p = jnp.concatenate(ps, axis=1).astype(BF16)
            acc_ref[n] += jnp.dot(p, vj, preferred_element_type=F32)
        return carry
    lax.fori_loop(0, qi + 1, pass_b, 0)

    lam = (jnp.exp(jnp.sum(lq1_ref[...] * lk1_ref[...], axis=-1, keepdims=True))
           - jnp.exp(jnp.sum(lq2_ref[...] * lk2_ref[...], axis=-1, keepdims=True)) + lambda_init)
    l0 = jnp.sum(ls_ref[0], axis=-1, keepdims=True)
    l1 = jnp.sum(ls_ref[1], axis=-1, keepdims=True)
    o = acc_ref[0] / l0 - lam * (acc_ref[1] / l1)
    ms = jnp.mean(o * o, axis=-1, keepdims=True)
    o_ref[...] = (o * lax.rsqrt(ms + LN_EPS) * g_ref[...] * (1.0 - lambda_init)).astype(o_ref.dtype)


def _diff_attention(z_main, lq1, lk1, lq2, lk2, norm_g, B, S, lambda_init):
    tq = min(ATTN_TQ, S)
    nq = S // tq
    H = B_HEADS
    dv = B_V_DIM
    slopes = 2.0 ** (-8.0 * jnp.arange(1, H + 1, dtype=F32) / H)
    assert S <= 256 * 256
    pos = lax.broadcasted_iota(I32, (H, S, dv), 1)
    col = lax.broadcasted_iota(I32, (H, S, dv), 2)
    within = pos % 256
    kbias = jnp.where(col == 0, within, jnp.where(col == 1, pos - within, 0)).astype(F32)
    kbias = (kbias * slopes[:, None, None]).astype(BF16)
    small = pl.BlockSpec((1, B_HEAD_DIM), lambda b, h, i: (0, 0))
    return pl.pallas_call(
        functools.partial(_attn_kernel, lambda_init=lambda_init),
        grid=(B, H, nq),
        in_specs=[
            pl.BlockSpec(memory_space=pltpu.SMEM),
            pl.BlockSpec((tq, dv), lambda b, h, i: (b * nq + i, COL_QB // dv + h)),
            pl.BlockSpec((S, dv), lambda b, h, i: (b, COL_KB // dv + h)),
            pl.BlockSpec((None, S, dv), lambda b, h, i: (h, 0, 0)),
            pl.BlockSpec((S, dv), lambda b, h, i: (b, COL_VB // dv + h)),
            small, small, small, small,
            pl.BlockSpec((1, dv), lambda b, h, i: (0, 0)),
        ],
        out_specs=pl.BlockSpec((tq, dv), lambda b, h, i: (b * nq + i, h)),
        out_shape=jax.ShapeDtypeStruct((B * S, B_WIDTH), BF16),
        scratch_shapes=[
            pltpu.VMEM((2, nq, tq, tq), F32),
            pltpu.VMEM((2, tq, LANES), F32),
            pltpu.VMEM((2, tq, LANES), F32),
            pltpu.VMEM((2, tq, dv), F32),
            pltpu.VMEM((tq, tq), F32),
        ],
        compiler_params=_cparams(("parallel", "parallel", "arbitrary")),
        name="diff_attention",
    )(slopes, z_main, z_main, kbias, z_main, lq1, lk1, lq2, lk2, norm_g)


def _mix_kernel(ha_ref, ob_ref, ga_ref, gb_ref, x_ref, wa_ref, wb_ref, wo_ref, g1_ref, b1_ref, wr_ref, br_ref,
                x1_ref, x1p_ref, ri_ref, rw_ref, cnt_ref, carry_ref):
    i = pl.program_id(0)
    tm = x_ref.shape[0]

    @pl.when(i == 0)
    def _():
        carry_ref[...] = jnp.zeros_like(carry_ref)

    ya = jnp.dot(ha_ref[...], wa_ref[...], preferred_element_type=F32)
    yb = jnp.dot(ob_ref[...], wb_ref[...], preferred_element_type=F32)
    merged = _sigmoid(ga_ref[...].astype(F32)) * ya + _sigmoid(gb_ref[...].astype(F32)) * yb
    mix = jnp.dot(merged.astype(BF16), wo_ref[...], preferred_element_type=F32)
    pre = DEEPNORM_ALPHA * x_ref[...] + mix
    mu = jnp.mean(pre, axis=-1, keepdims=True)
    pc = pre - mu
    var = jnp.mean(pc * pc, axis=-1, keepdims=True)
    x1 = pc * lax.rsqrt(var + LN_EPS) * g1_ref[...] + b1_ref[...]
    x1_ref[...] = x1
    _rows_to_tiles(x1p_ref, _pack_halves(x1))

    logits = _dot_3pass(x1, wr_ref[...]) + br_ref[...]
    lt = logits.T
    row8 = lax.broadcasted_iota(I32, (SUBLANES, tm), 0).astype(F32)
    gl = jnp.where(row8 < N_GROUPS, lt[0:SUBLANES, :], NEG_BIG)
    gmax = jnp.max(gl, axis=0, keepdims=True)
    gsel = jnp.min(jnp.where(gl == gmax, row8, float(SUBLANES)), axis=0, keepdims=True)
    gprob = 1.0 / jnp.sum(jnp.exp(gl - gmax), axis=0, keepdims=True)
    ing = lt[SUBLANES:2 * SUBLANES, :]
    for g in range(1, N_GROUPS):
        ing = jnp.where(gsel == g, lt[(g + 1) * SUBLANES:(g + 2) * SUBLANES, :], ing)
    v0 = jnp.max(ing, axis=0, keepdims=True)
    i0 = jnp.min(jnp.where(ing == v0, row8, float(SUBLANES)), axis=0, keepdims=True)
    ing2 = jnp.where(row8 == i0, -jnp.inf, ing)
    v1 = jnp.max(ing2, axis=0, keepdims=True)
    i1 = jnp.min(jnp.where(ing2 == v1, row8, float(SUBLANES)), axis=0, keepdims=True)
    ex = jnp.exp(v1 - v0)
    inv = 1.0 / (1.0 + ex)
    w0 = gprob * inv
    w1 = gprob * (ex * inv)
    e0 = (gsel * EXPERTS_PER_GROUP + i0).astype(I32)
    e1 = (gsel * EXPERTS_PER_GROUP + i1).astype(I32)

    rowe = lax.broadcasted_iota(I32, (N_EXPERTS, tm), 0)
    is0 = rowe == e0
    is1 = rowe == e1
    oh = jnp.where(is0, 1.0, 0.0) + jnp.where(is1, 1.0, 0.0)
    tr = lax.broadcasted_iota(I32, (tm, tm), 0)
    tc = lax.broadcasted_iota(I32, (tm, tm), 1)
    su = jnp.where(tr < tc, 1.0, 0.0).astype(BF16)
    tot = jnp.dot(oh.astype(BF16), su, preferred_element_type=F32) + carry_ref[...]
    rank0 = jnp.sum(jnp.where(is0, tot, 0.0), axis=0, keepdims=True)
    rank1 = jnp.sum(jnp.where(is1, tot, 0.0), axis=0, keepdims=True)
    carry = carry_ref[...] + jnp.sum(oh, axis=1, keepdims=True)
    carry_ref[...] = carry
    cnt_ref[...] = jnp.broadcast_to(carry, cnt_ref.shape).astype(I32)

    zi = jnp.zeros((1, tm), I32)
    ri_ref[...] = jnp.concatenate([e0, e1, rank0.astype(I32), rank1.astype(I32), zi, zi, zi, zi], axis=0)
    zf = jnp.zeros((1, tm), F32)
    rw_ref[...] = jnp.concatenate([w0, w1, zf, zf, zf, zf, zf, zf], axis=0)


def _mix(h_a, o_b, z_main, x2d, w_a, w_b, w_out, ln_g, ln_b, w_r, b_r):
    T, D = x2d.shape
    tm = min(MIX_TM, T)
    const = lambda shape: pl.BlockSpec(shape, lambda i: (0, 0), pipeline_mode=pl.Buffered(1))
    return pl.pallas_call(
        _mix_kernel,
        grid=(T // tm,),
        in_specs=[
            pl.BlockSpec((tm, A_WIDTH), lambda i: (i, 0)),
            pl.BlockSpec((tm, B_WIDTH), lambda i: (i, 0)),
            pl.BlockSpec((tm, D), lambda i: (i, COL_GA // D)),
            pl.BlockSpec((tm, D), lambda i: (i, COL_GB // D)),
            pl.BlockSpec((tm, D), lambda i: (i, 0)),
            const((A_WIDTH, D)), const((B_WIDTH, D)), const((D, D)),
            const((1, D)), const((1, D)), const((D, LANES)), const((1, LANES)),
        ],
        out_specs=[
            pl.BlockSpec((tm, D), lambda i: (i, 0)),
            pl.BlockSpec((tm * SUBLANES, LANES), lambda i: (i, 0)),
            pl.BlockSpec((SUBLANES, tm), lambda i: (0, i)),
            pl.BlockSpec((SUBLANES, tm), lambda i: (0, i)),
            pl.BlockSpec((N_EXPERTS, LANES), lambda i: (0, 0)),
        ],
        out_shape=[
            jax.ShapeDtypeStruct((T, D), F32),
            jax.ShapeDtypeStruct((T * SUBLANES, LANES), U32),
            jax.ShapeDtypeStruct((SUBLANES, T), I32),
            jax.ShapeDtypeStruct((SUBLANES, T), F32),
            jax.ShapeDtypeStruct((N_EXPERTS, LANES), I32),
        ],
        scratch_shapes=[pltpu.VMEM((N_EXPERTS, 1), F32)],
        compiler_params=_cparams(("arbitrary",)),
        name="mix_ln1_router",
    )(h_a, o_b, z_main, z_main, x2d, w_a, w_b, w_out, ln_g, ln_b, w_r, b_r)


def _row_copy(src_ref, src_row, dst_ref, dst_row, sem):
    return pltpu.make_async_copy(src_ref.at[pl.ds(src_row * SUBLANES, SUBLANES)],
                                 dst_ref.at[pl.ds(dst_row * SUBLANES, SUBLANES)], sem)


def _rows_wait(src_ref, dst_ref, rows, sem):
    n = rows * SUBLANES
    pltpu.make_async_copy(src_ref.at[pl.ds(0, n)], dst_ref.at[pl.ds(0, n)], sem).wait()


def _dispatch_kernel(cnt_ref, ri_ref, x1_ref, xpad_ref, pstart_ref, bexp_ref, nused_ref,
                     pst_ref, zero_ref, sem, zsem, *, tb, n_blocks, tm):
    i = pl.program_id(0)
    last = pl.num_programs(0) - 1

    @pl.when(i == 0)
    def _():
        def fill(b, carry):
            bexp_ref[b] = 0
            return carry
        lax.fori_loop(0, n_blocks, fill, 0)

        def per_expert(e, blk):
            pst_ref[e] = blk * tb
            pstart_ref[e] = blk * tb
            nb = (cnt_ref[e] + (tb - 1)) // tb

            def mark(b, carry):
                bexp_ref[blk + b] = e
                return carry
            lax.fori_loop(0, nb, mark, 0)
            return blk + nb
        nused = lax.fori_loop(0, N_EXPERTS, per_expert, 0)
        nused_ref[0] = nused
        pst_ref[N_EXPERTS] = nused

    def issue(t, carry):
        for k in range(TOP_K):
            pos = pst_ref[ri_ref[k, t]] + ri_ref[TOP_K + k, t]
            _row_copy(x1_ref, t, xpad_ref, pos, sem).start()
        return carry
    lax.fori_loop(0, tm, issue, 0, unroll=8)
    for k in range(TOP_K):
        _rows_wait(x1_ref, xpad_ref, tm, sem)

    @pl.when(i == last)
    def _():
        zero_ref[...] = jnp.zeros_like(zero_ref)

        def per_expert(e, carry):
            cnt = cnt_ref[e]
            first = pst_ref[e] + cnt
            npad = ((cnt + (tb - 1)) // tb) * tb - cnt

            def start(p, c2):
                _row_copy(zero_ref, 0, xpad_ref, first + p, zsem).start()
                return c2
            lax.fori_loop(0, npad, start, 0)

            def wait(p, c2):
                _row_copy(zero_ref, 0, xpad_ref, 0, zsem).wait()
                return c2
            lax.fori_loop(0, npad, wait, 0)
            return carry
        lax.fori_loop(0, N_EXPERTS, per_expert, 0)

        def block_copy(b):
            n = tb * SUBLANES
            return pltpu.make_async_copy(zero_ref, xpad_ref.at[pl.ds(pl.multiple_of(b * n, n), n)], zsem)

        def start_block(b, carry):
            block_copy(b).start()
            return carry
        lax.fori_loop(pst_ref[N_EXPERTS], n_blocks, start_block, 0)

        def wait_block(b, carry):
            block_copy(b).wait()
            return carry
        lax.fori_loop(pst_ref[N_EXPERTS], n_blocks, wait_block, 0)


def _dispatch(counts, route_i, x1, tb, n_blocks):
    T = x1.shape[0] // SUBLANES
    tm = min(DISPATCH_TM, T)
    smem_full = pl.BlockSpec(memory_space=pltpu.SMEM)
    return pl.pallas_call(
        functools.partial(_dispatch_kernel, tb=tb, n_blocks=n_blocks, tm=tm),
        grid=(T // tm,),
        in_specs=[
            smem_full,
            pl.BlockSpec((SUBLANES, tm), lambda i: (0, i), memory_space=pltpu.SMEM),
            pl.BlockSpec((tm * SUBLANES, LANES), lambda i: (i, 0)),
        ],
        out_specs=[
            pl.BlockSpec(memory_space=pl.ANY),
            smem_full, smem_full, smem_full,
        ],
        out_shape=[
            jax.ShapeDtypeStruct((n_blocks * tb * SUBLANES, LANES), x1.dtype),
            jax.ShapeDtypeStruct((N_EXPERTS,), I32),
            jax.ShapeDtypeStruct((n_blocks,), I32),
            jax.ShapeDtypeStruct((1,), I32),
        ],
        scratch_shapes=[
            pltpu.SMEM((N_EXPERTS + 1,), I32),
            pltpu.VMEM((tb * SUBLANES, LANES), x1.dtype),
            pltpu.SemaphoreType.DMA(()),
            pltpu.SemaphoreType.DMA(()),
        ],
        compiler_params=_cparams(("arbitrary",)),
        name="moe_dispatch",
    )(counts, route_i, x1)


def _expert_kernel(bexp_ref, nused_ref, x_ref, wg_hbm, wu_hbm, wd_hbm, y_ref,
                   wgf_ref, wuf_ref, wdf_ref, wgb_ref, wub_ref, wdb_ref, sem, run_ref):
    b = pl.program_id(0)
    nused = nused_ref[0]
    used = b < nused
    e = bexp_ref[b]
    new_expert = jnp.logical_or(b == 0, e != bexp_ref[jnp.maximum(b - 1, 0)])
    streams = ((wg_hbm, wgf_ref), (wu_hbm, wuf_ref), (wd_hbm, wdf_ref))

    def weight_copies(expert, slot):
        return [pltpu.make_async_copy(w_hbm.at[expert], wf_ref.at[slot], sem.at[slot, j])
                for j, (w_hbm, wf_ref) in enumerate(streams)]

    @pl.when(jnp.logical_and(used, b == 0))
    def _():
        run_ref[0] = 0
        for cp in weight_copies(e, 0):
            cp.start()

    @pl.when(jnp.logical_and(used, new_expert))
    def _():
        slot = run_ref[0] % 2
        nxt = lax.while_loop(lambda j: jnp.logical_and(j < nused, bexp_ref[jnp.minimum(j, nused - 1)] == e),
                             lambda j: j + 1, b + 1)

        @pl.when(nxt < nused)
        def _():
            for cp in weight_copies(bexp_ref[jnp.minimum(nxt, nused - 1)], 1 - slot):
                cp.start()

        for cp in weight_copies(e, slot):
            cp.wait()
        wgb_ref[...] = wgf_ref[slot].astype(BF16)
        wub_ref[...] = wuf_ref[slot].astype(BF16)
        wdb_ref[...] = wdf_ref[slot].astype(BF16)
        run_ref[0] = run_ref[0] + 1

    @pl.when(used)
    def _():
        half = SUBLANES * LANES
        x_lo, x_hi = (t.astype(BF16) for t in _unpack_halves(_tiles_to_rows(x_ref)))
        dot = functools.partial(jnp.dot, preferred_element_type=F32)
        gate = dot(x_lo, wgb_ref[:half, :]) + dot(x_hi, wgb_ref[half:, :])
        up = dot(x_lo, wub_ref[:half, :]) + dot(x_hi, wub_ref[half:, :])
        hid = (gate * _sigmoid(gate) * up).astype(BF16)
        _rows_to_tiles(y_ref, _pack_halves(dot(hid, wdb_ref[...])))

    @pl.when(jnp.logical_not(used))
    def _():
        y_ref[...] = jnp.zeros_like(y_ref)


def _experts(bexp, nused, x_pad, w_gate, w_up, w_down, tb):
    P = x_pad.shape[0] // SUBLANES
    D = 2 * TILE_WORDS
    n_blocks = P // tb
    rows = (tb * SUBLANES, LANES)
    F = w_gate.shape[-1]
    blk = lambda b, be, nu: jnp.maximum(jnp.minimum(b, nu[0] - 1), 0)
    hbm = pl.BlockSpec(memory_space=pl.ANY)
    return pl.pallas_call(
        _expert_kernel,
        grid_spec=pltpu.PrefetchScalarGridSpec(
            num_scalar_prefetch=2,
            grid=(n_blocks,),
            in_specs=[pl.BlockSpec(rows, lambda b, be, nu: (blk(b, be, nu), 0)), hbm, hbm, hbm],
            out_specs=pl.BlockSpec(rows, lambda b, be, nu: (b, 0)),
            scratch_shapes=[
                pltpu.VMEM((2, D, F), F32), pltpu.VMEM((2, D, F), F32), pltpu.VMEM((2, F, D), F32),
                pltpu.VMEM((D, F), BF16), pltpu.VMEM((D, F), BF16), pltpu.VMEM((F, D), BF16),
                pltpu.SemaphoreType.DMA((2, 3)),
                pltpu.SMEM((1,), I32),
            ],
        ),
        out_shape=jax.ShapeDtypeStruct(x_pad.shape, U32),
        compiler_params=_cparams(("arbitrary",)),
        name="moe_experts",
    )(bexp, nused, x_pad, w_gate, w_up, w_down)


def _combine_kernel(pstart_ref, ri_ref, rin_ref, x1_ref, rw_ref, g_ref, b_ref, ypad_ref, o_ref, ybuf_ref, sem):
    i = pl.program_id(0)
    tm = x1_ref.shape[0]
    slot = i % 2

    def gather(route_ref, s):
        def issue(t, carry):
            for k in range(TOP_K):
                pos = pstart_ref[route_ref[k, t]] + route_ref[TOP_K + k, t]
                _row_copy(ypad_ref, pos, ybuf_ref.at[s, k], t, sem.at[s]).start()
            return carry
        lax.fori_loop(0, tm, issue, 0, unroll=8)

    @pl.when(i == 0)
    def _():
        gather(ri_ref, 0)

    @pl.when(i + 1 < pl.num_programs(0))
    def _():
        gather(rin_ref, 1 - slot)

    for k in range(TOP_K):
        _rows_wait(ypad_ref, ybuf_ref.at[slot, k], tm, sem.at[slot])

    rw = rw_ref[...]
    y0 = _unpack_halves(_tiles_to_rows(ybuf_ref.at[slot, 0]))
    y1 = _unpack_halves(_tiles_to_rows(ybuf_ref.at[slot, 1]))
    ffn = jnp.concatenate([y0[h] * rw[:, 0:1] + y1[h] * rw[:, 1:2] for h in range(2)], axis=1)
    pre = DEEPNORM_ALPHA * x1_ref[...] + ffn
    mu = jnp.mean(pre, axis=-1, keepdims=True)
    pc = pre - mu
    var = jnp.mean(pc * pc, axis=-1, keepdims=True)
    o_ref[...] = pc * lax.rsqrt(var + LN_EPS) * g_ref[...] + b_ref[...]


def _combine(pstart, route_i, x1, rw_col, ln_g, ln_b, y_pad):
    T, D = x1.shape
    tm = min(COMBINE_TM, T)
    n = T // tm
    return pl.pallas_call(
        _combine_kernel,
        grid=(n,),
        in_specs=[
            pl.BlockSpec(memory_space=pltpu.SMEM),
            pl.BlockSpec((SUBLANES, tm), lambda i: (0, i), memory_space=pltpu.SMEM),
            pl.BlockSpec((SUBLANES, tm), lambda i: (0, jnp.minimum(i + 1, n - 1)), memory_space=pltpu.SMEM),
            pl.BlockSpec((tm, D), lambda i: (i, 0)),
            pl.BlockSpec((tm, SUBLANES), lambda i: (i, 0)),
            pl.BlockSpec((1, D), lambda i: (0, 0)),
            pl.BlockSpec((1, D), lambda i: (0, 0)),
            pl.BlockSpec(memory_space=pl.ANY),
        ],
        out_specs=pl.BlockSpec((tm, D), lambda i: (i, 0)),
        out_shape=jax.ShapeDtypeStruct((T, D), F32),
        scratch_shapes=[
            pltpu.VMEM((2, TOP_K, tm * SUBLANES, LANES), U32),
            pltpu.SemaphoreType.DMA((2,)),
        ],
        compiler_params=_cparams(("arbitrary",)),
        name="moe_combine_ln2",
    )(pstart, route_i, route_i, x1, rw_col, ln_g, ln_b, y_pad)


N_GATE_COLS = 2 * A_HEADS
SRC_MLSTM = 0
SRC_GATES = 4 * A_WIDTH
SRC_DIFF = SRC_GATES + N_GATE_COLS
SRC_MERGE = SRC_DIFF + 3 * B_WIDTH
PREP_TN = 512


def _wprep_kernel(a_ref, o_ref):
    o_ref[...] = a_ref[...].T.astype(BF16)


def _wgate_kernel(a_ref, o_ref):
    a = a_ref[...]
    rows = jnp.concatenate([a, jnp.zeros((LANES - a.shape[0], a.shape[1]), F32)], axis=0)
    o_ref[...] = rows.T


def _rearrange_in_proj(w_in, b_in):
    K, n_in = w_in.shape
    tn = PREP_TN
    merge_blocks = (2 * D_MODEL) // tn
    mlstm_blocks = (4 * A_WIDTH) // tn
    assert SRC_MERGE % SUBLANES == 0 and SRC_DIFF % SUBLANES == 0

    def src_row(jb):
        s = SUBLANES
        merge = SRC_MERGE // s + (tn // s) * jb
        mlstm = SRC_MLSTM // s + (tn // s) * (jb - merge_blocks)
        diff = SRC_DIFF // s + (tn // s) * (jb - merge_blocks - mlstm_blocks)
        return s * jnp.where(jb < merge_blocks, merge, jnp.where(jb < merge_blocks + mlstm_blocks, mlstm, diff))

    w_t = jnp.swapaxes(w_in, 0, 1)
    w_main = pl.pallas_call(
        _wprep_kernel,
        grid=(N_MAIN // tn,),
        in_specs=[pl.BlockSpec((pl.Element(tn), pl.Element(K)), lambda jb: (src_row(jb), 0))],
        out_specs=pl.BlockSpec((K, tn), lambda jb: (0, jb)),
        out_shape=jax.ShapeDtypeStruct((K, N_MAIN), BF16),
        compiler_params=_cparams(("parallel",)),
        name="w_in_prep",
    )(w_t)
    b_main = jnp.concatenate([b_in[SRC_MERGE:], b_in[SRC_MLSTM:SRC_GATES], b_in[SRC_DIFF:SRC_MERGE]])[None, :]
    w_gate = pl.pallas_call(
        _wgate_kernel,
        grid=(1,),
        in_specs=[pl.BlockSpec((pl.Element(N_GATE_COLS), pl.Element(K)), lambda i: (SRC_GATES, 0))],
        out_specs=pl.BlockSpec((K, LANES), lambda i: (0, 0)),
        out_shape=jax.ShapeDtypeStruct((K, LANES), F32),
        name="w_gate_prep",
    )(w_t)
    b_gate = jnp.pad(b_in[SRC_GATES:SRC_DIFF], (0, LANES - N_GATE_COLS))[None, :]
    return w_main, b_main, w_gate, b_gate


def _layer(x, w_in, b_in, conv_w, conv_b, norm_a_g, lq1, lk1, lq2, lk2, norm_b_g, w_a, w_b, w_out,
           ln1_g, ln1_b, w_grp, b_grp, w_exp, b_exp, w_gate, w_up, w_down, ln2_g, ln2_b, lambda_init):
    B, S, D = x.shape
    T = B * S
    x2d = x.reshape(T, D)

    w_main, b_main, w_g, b_g = _rearrange_in_proj(w_in, b_in)
    z_main, z_gate = _in_projection(x2d, w_main, b_main, w_g, b_g)

    L = min(MLSTM_CHUNK, S)
    gp = _gate_prep(z_gate[:, :2 * A_HEADS].T, L)
    h_a = _mlstm(z_main, gp.T, gp, conv_w, conv_b[None, :], norm_a_g[None, :], B, S)
    o_b = _diff_attention(z_main, lq1[None, :], lk1[None, :], lq2[None, :], lk2[None, :],
                          norm_b_g[None, :], B, S, lambda_init)

    w_r = jnp.zeros((D, LANES), F32).at[:, :N_GROUPS].set(w_grp).at[:, SUBLANES:SUBLANES + N_EXPERTS].set(w_exp)
    b_r = jnp.zeros((LANES,), F32).at[:N_GROUPS].set(b_grp).at[SUBLANES:SUBLANES + N_EXPERTS].set(b_exp)[None, :]
    x1, x1_packed, route_i, route_w, counts = _mix(h_a, o_b, z_main, x2d, w_a.astype(BF16), w_b.astype(BF16),
                                                   w_out.astype(BF16), ln1_g[None, :], ln1_b[None, :], w_r, b_r)

    tb = MOE_TB
    n_blocks = (T * TOP_K) // tb + N_EXPERTS
    x_pad, pstart, bexp, nused = _dispatch(counts[:, 0], route_i, x1_packed, tb, n_blocks)
    y_pad = _experts(bexp, nused, x_pad, w_gate, w_up, w_down, tb)
    out = _combine(pstart, route_i, x1, route_w.T, ln2_g[None, :], ln2_b[None, :], y_pad)
    return out.reshape(B, S, D)


def kernel(x, w_in, b_in, conv_w, conv_b, mlstm_norm_g, lambda_q1, lambda_k1, lambda_q2, lambda_k2,
           diff_norm_g, w_a, w_b, w_out, ln1_g, ln1_b, w_grp, b_grp, w_exp, b_exp,
           w_gate, w_up, w_down, ln2_g, ln2_b):
    for l in range(DEPTH):
        lambda_init = 0.8 - 0.6 * math.exp(-0.3 * l)
        x = _layer(x, w_in[l], b_in[l], conv_w[l], conv_b[l], mlstm_norm_g[l], lambda_q1[l], lambda_k1[l],
                   lambda_q2[l], lambda_k2[l], diff_norm_g[l], w_a[l], w_b[l], w_out[l], ln1_g[l], ln1_b[l],
                   w_grp[l], b_grp[l], w_exp[l], b_exp[l], w_gate[l], w_up[l], w_down[l], ln2_g[l], ln2_b[l],
                   lambda_init)
    return x
```

```python
import functools
import math

import jax
import jax.numpy as jnp
from jax import lax
from jax.experimental import pallas as pl
from jax.experimental.pallas import tpu as pltpu

F32 = jnp.float32
BF16 = jnp.bfloat16
I32 = jnp.int32

D_MODEL = 2048
A_HEADS = 4
A_HEAD_DIM = 256
A_WIDTH = A_HEADS * A_HEAD_DIM
CONV_WIDTH = 4
B_HEADS = 8
B_HEAD_DIM = 64
B_V_DIM = 2 * B_HEAD_DIM
B_WIDTH = B_HEADS * B_V_DIM
ATTN_CHUNK = 64
N_GROUPS = 4
EXPERTS_PER_GROUP = 8
N_EXPERTS = N_GROUPS * EXPERTS_PER_GROUP
TOP_K = 2
D_EXPERT = 512
DEPTH = 1
DEEPNORM_ALPHA = (2 * DEPTH) ** 0.25
LN_EPS = 1e-5
NEG_BIG = -1e30
LOG2_E = 1.4426950408889634

LANES = 128
SUBLANES = 8
VMEM_LIMIT_BYTES = 56 * 1024 * 1024

COL_GA = 0
COL_GB = COL_GA + D_MODEL
COL_QA = COL_GB + D_MODEL
COL_KA = COL_QA + A_WIDTH
COL_VA = COL_KA + A_WIDTH
COL_OA = COL_VA + A_WIDTH
COL_QB = COL_OA + A_WIDTH
COL_KB = COL_QB + B_WIDTH
COL_VB = COL_KB + B_WIDTH
N_MAIN = COL_VB + B_WIDTH

PROJ_TM, PROJ_TN = 1024, 1024
MLSTM_CHUNK = 256
ATTN_TQ = 512
MIX_TM = 256
MOE_TB = 256
DISPATCH_TM = 256
COMBINE_TM = 256


def _cparams(sem, vmem=VMEM_LIMIT_BYTES):
    return pltpu.CompilerParams(dimension_semantics=sem, vmem_limit_bytes=vmem)


def _sigmoid(x):
    return 1.0 / (1.0 + jnp.exp(-x))


U32 = jnp.uint32


def _pack_halves(x):
    n = x.shape[1] // 2
    lo = lax.bitcast_convert_type(x[:, :n].astype(BF16).astype(F32), U32)
    hi = lax.bitcast_convert_type(x[:, n:].astype(BF16).astype(F32), U32)
    return (lo >> 16) | hi


def _unpack_halves(u):
    lo = lax.bitcast_convert_type(u << 16, F32)
    hi = lax.bitcast_convert_type(u & jnp.uint32(0xFFFF0000), F32)
    return lo, hi


TILE_WORDS = SUBLANES * LANES


def _rows_to_tiles(ref, v):
    rows = v.shape[0]
    for j in range(SUBLANES):
        ref[pl.ds(j, rows, stride=SUBLANES), :] = v[:, j * LANES:(j + 1) * LANES]


def _tiles_to_rows(ref):
    rows = ref.shape[0] // SUBLANES
    return jnp.concatenate([ref[pl.ds(j, rows, stride=SUBLANES), :] for j in range(SUBLANES)], axis=1)


def _split_bf16(a):
    hi = a.astype(BF16)
    return hi, (a - hi.astype(F32)).astype(BF16)


def _dot_3pass(a, b):
    a_hi, a_lo = _split_bf16(a)
    b_hi, b_lo = _split_bf16(b)
    dot = functools.partial(jnp.dot, preferred_element_type=F32)
    return dot(a_hi, b_hi) + (dot(a_hi, b_lo) + dot(a_lo, b_hi))


def _proj_kernel(x_ref, w_ref, b_ref, wg_ref, bg_ref, z_ref, zg_ref, xb_ref):
    @pl.when(pl.program_id(1) == 0)
    def _():
        x = x_ref[...]
        xb_ref[...] = x.astype(BF16)
        zg_ref[...] = _dot_3pass(x, wg_ref[...]) + bg_ref[...]

    acc = jnp.dot(xb_ref[...], w_ref[...], preferred_element_type=F32)
    z_ref[...] = (acc + b_ref[...]).astype(z_ref.dtype)


def _in_projection(x2d, w_main, b_main, w_gate, b_gate):
    T, K = x2d.shape
    N = w_main.shape[1]
    tm, tn = min(PROJ_TM, T), PROJ_TN
    return pl.pallas_call(
        _proj_kernel,
        grid=(T // tm, N // tn),
        in_specs=[
            pl.BlockSpec((tm, K), lambda i, j: (i, 0)),
            pl.BlockSpec((K, tn), lambda i, j: (0, j)),
            pl.BlockSpec((1, tn), lambda i, j: (0, j)),
            pl.BlockSpec((K, LANES), lambda i, j: (0, 0)),
            pl.BlockSpec((1, LANES), lambda i, j: (0, 0)),
        ],
        out_specs=[
            pl.BlockSpec((tm, tn), lambda i, j: (i, j)),
            pl.BlockSpec((tm, LANES), lambda i, j: (i, 0)),
        ],
        out_shape=[
            jax.ShapeDtypeStruct((T, N), BF16),
            jax.ShapeDtypeStruct((T, LANES), F32),
        ],
        scratch_shapes=[pltpu.VMEM((tm, K), BF16)],
        compiler_params=_cparams(("parallel", "arbitrary")),
        name="in_projection",
    )(x2d, w_main, b_main, w_gate, b_gate)


def _gate_prep_kernel(g_ref, o_ref, *, L):
    r = lax.broadcasted_iota(I32, (L, L), 0)
    c = lax.broadcasted_iota(I32, (L, L), 1)
    tri = (r <= c).astype(F32)
    row = lax.broadcasted_iota(I32, (g_ref.shape[0], L), 0)
    for ch in range(g_ref.shape[1] // L):
        g = g_ref[:, ch * L:(ch + 1) * L]
        lf = jnp.minimum(g, 0.0) - jnp.log(1.0 + jnp.exp(-jnp.abs(g)))
        b = jnp.dot(lf, tri, preferred_element_type=F32, precision=lax.Precision.HIGHEST)
        o_ref[:, ch * L:(ch + 1) * L] = jnp.where(row < A_HEADS, g, b)


def _gate_prep(g_rows, L):
    R, T = g_rows.shape
    per_step = math.gcd(T // L, 8)
    W = L * per_step
    return pl.pallas_call(
        functools.partial(_gate_prep_kernel, L=L),
        grid=(T // W,),
        in_specs=[pl.BlockSpec((R, W), lambda i: (0, i))],
        out_specs=pl.BlockSpec((R, W), lambda i: (0, i)),
        out_shape=jax.ShapeDtypeStruct((R, T), F32),
        compiler_params=_cparams(("parallel",)),
        name="gate_prep",
    )(g_rows)


def _mlstm_kernel(q_ref, k_ref, v_ref, o_ref, gcol_ref, grow_ref, cw_ref, cb_ref, ng_ref, out_ref,
                  c_ref, n_ref, m_ref, qcar_ref, kcar_ref):
    ci = pl.program_id(1)
    L = q_ref.shape[0]
    dh = A_HEAD_DIM

    @pl.when(ci == 0)
    def _():
        c_ref[...] = jnp.zeros_like(c_ref)
        n_ref[...] = jnp.zeros_like(n_ref)
        m_ref[...] = jnp.zeros_like(m_ref)
        qcar_ref[...] = jnp.zeros_like(qcar_ref)
        kcar_ref[...] = jnp.zeros_like(kcar_ref)

    def conv_silu(u_ref, car_ref, lo, wcol):
        u = u_ref[:, lo:lo + dh].astype(F32)
        ext = jnp.concatenate([car_ref[:, lo:lo + dh], u], axis=0)
        w = cw_ref[:, wcol:wcol + dh]
        y = cb_ref[:, wcol:wcol + dh] + w[3:4, :] * u
        for j in range(CONV_WIDTH - 1):
            off = SUBLANES - (CONV_WIDTH - 1) + j
            y = y + w[j:j + 1, :] * ext[off:off + L, :]
        car_ref[:, lo:lo + dh] = u[L - SUBLANES:, :]
        return y * _sigmoid(y)

    r = lax.broadcasted_iota(I32, (L, L), 0)
    c = lax.broadcasted_iota(I32, (L, L), 1)
    causal = r >= c
    gcol = gcol_ref[...]
    grow = grow_ref[...]

    for h in range(A_HEADS):
        lo = h * dh
        q = conv_silu(q_ref, qcar_ref, lo, lo)
        k = conv_silu(k_ref, kcar_ref, lo, A_WIDTH + lo) * (dh ** -0.5)
        v_b = v_ref[:, lo:lo + dh]
        q_b = q.astype(BF16)
        k_b = k.astype(BF16)

        i_col, b_col = gcol[:, h:h + 1], gcol[:, A_HEADS + h:A_HEADS + h + 1]
        i_row, b_row = grow[h:h + 1, :], grow[A_HEADS + h:A_HEADS + h + 1, :]
        b_last = b_row[:, L - 1:L]
        m_prev = m_ref[h]

        dmat = jnp.where(causal, b_col - b_row + i_row, NEG_BIG)
        inter = b_col + m_prev
        m_t = jnp.maximum(inter, jnp.max(dmat, axis=-1, keepdims=True))
        s = lax.dot_general(q_b, k_b, (((1,), (1,)), ((), ())), preferred_element_type=F32)
        w_intra = jnp.exp(dmat - m_t) * s
        w_inter = jnp.exp(inter - m_t)
        qc = jnp.dot(q_b, c_ref[h].astype(BF16), preferred_element_type=F32)
        num = w_inter * qc + jnp.dot(w_intra.astype(BF16), v_b, preferred_element_type=F32)
        qn = jnp.sum(q * n_ref[h], axis=-1, keepdims=True)
        den = w_inter * qn + jnp.sum(w_intra, axis=-1, keepdims=True)
        hh = num / jnp.maximum(jnp.abs(den), jnp.exp(-m_t))

        g_col = b_last - b_col + i_col
        m_new = jnp.maximum(b_last + m_prev, jnp.max(g_col, axis=0, keepdims=True))
        decay = jnp.exp(b_last + m_prev - m_new)
        kw = jnp.exp(g_col - m_new) * k
        c_ref[h] = decay * c_ref[h] + jnp.dot(kw.T.astype(BF16), v_b, preferred_element_type=F32)
        n_ref[h] = decay * n_ref[h] + jnp.sum(kw, axis=0, keepdims=True)
        m_ref[h] = m_new

        y = _sigmoid(o_ref[:, lo:lo + dh].astype(F32)) * hh
        mu = jnp.mean(y, axis=-1, keepdims=True)
        yc = y - mu
        var = jnp.mean(yc * yc, axis=-1, keepdims=True)
        out_ref[:, lo:lo + dh] = (yc * lax.rsqrt(var + LN_EPS) * ng_ref[:, lo:lo + dh]).astype(out_ref.dtype)


def _mlstm(z_main, gcol, grow, conv_w, conv_b, norm_g, B, S):
    L = min(MLSTM_CHUNK, S)
    nc = S // L
    dh = A_HEAD_DIM
    H = A_HEADS
    W = A_WIDTH
    row = lambda b, c: b * nc + c
    full = lambda shape: pl.BlockSpec(shape, lambda b, c: (0, 0))
    return pl.pallas_call(
        _mlstm_kernel,
        grid=(B, nc),
        in_specs=[
            pl.BlockSpec((L, W), lambda b, c: (row(b, c), COL_QA // W)),
            pl.BlockSpec((L, W), lambda b, c: (row(b, c), COL_KA // W)),
            pl.BlockSpec((L, W), lambda b, c: (row(b, c), COL_VA // W)),
            pl.BlockSpec((L, W), lambda b, c: (row(b, c), COL_OA // W)),
            pl.BlockSpec((L, SUBLANES), lambda b, c: (row(b, c), 0)),
            pl.BlockSpec((SUBLANES, L), lambda b, c: (0, row(b, c))),
            full((CONV_WIDTH, 2 * W)), full((1, 2 * W)), full((1, W)),
        ],
        out_specs=pl.BlockSpec((L, W), lambda b, c: (row(b, c), 0)),
        out_shape=jax.ShapeDtypeStruct((B * S, W), BF16),
        scratch_shapes=[
            pltpu.VMEM((H, dh, dh), F32),
            pltpu.VMEM((H, 1, dh), F32),
            pltpu.VMEM((H, 1, 1), F32),
            pltpu.VMEM((SUBLANES, W), F32),
            pltpu.VMEM((SUBLANES, W), F32),
        ],
        compiler_params=_cparams(("parallel", "arbitrary")),
        name="mlstm",
    )(z_main, z_main, z_main, z_main, gcol, grow, conv_w, conv_b, norm_g)


def _lane_tiles(t):
    return [t[:, i * LANES:(i + 1) * LANES] for i in range(t.shape[1] // LANES)]


def _attn_kernel(slope_ref, q_ref, k_ref, kb_ref, v_ref, lq1_ref, lk1_ref, lq2_ref, lk2_ref, g_ref, o_ref,
                 s_ref, mx_ref, ls_ref, acc_ref, corr_ref, *, lambda_init):
    h = pl.program_id(1)
    qi = pl.program_id(2)
    tq = q_ref.shape[0]
    tk = tq
    slope = slope_ref[h]
    nt = (((1,), (1,)), ((), ()))

    lane = lax.broadcasted_iota(I32, (tq, B_V_DIM), 1)
    q = q_ref[...] * (B_HEAD_DIM ** -0.5)
    zero = jnp.zeros_like(q)
    bias_cols = jnp.where(lane < 2, 1.0, 0.0).astype(BF16)
    qa = (jnp.concatenate([jnp.where(lane < B_HEAD_DIM, q, zero), bias_cols], axis=1),
          jnp.concatenate([jnp.where(lane >= B_HEAD_DIM, q, zero), bias_cols], axis=1))

    def scores(j):
        start = pl.multiple_of(j * tk, tk)
        ka = jnp.concatenate([k_ref[pl.ds(start, tk), :], kb_ref[pl.ds(start, tk), :]], axis=1)
        return [lax.dot_general(qa[n], ka, nt, preferred_element_type=F32) for n in range(2)]

    def keep(j, n, t):
        t = t * LOG2_E
        s_ref[n, j] = t
        mx_ref[n] = functools.reduce(jnp.maximum, _lane_tiles(t), mx_ref[n])

    mx_ref[...] = jnp.full(mx_ref.shape, NEG_BIG, F32)

    def pass_a(j):
        t = scores(j)
        for n in range(2):
            keep(j, n, t[n])

    def pass_a_pair(jj, carry):
        pass_a(2 * jj)
        pass_a(2 * jj + 1)
        return carry
    lax.fori_loop(0, qi // 2, pass_a_pair, 0)

    @pl.when(qi % 2 == 1)
    def _():
        pass_a(qi - 1)

    @pl.when(qi == 0)
    def _():
        r = lax.broadcasted_iota(I32, (tq, tk), 0)
        c = lax.broadcasted_iota(I32, (tq, tk), 1)
        ahead = jnp.where(c <= r, 0.0, (r - c).astype(F32) * (2.0 * slope))
        corr_ref[...] = jnp.where((c // ATTN_CHUNK) <= (r // ATTN_CHUNK), ahead, NEG_BIG)

    t = scores(qi)
    for n in range(2):
        keep(qi, n, t[n] + corr_ref[...])

    for n in range(2):
        m = jnp.max(mx_ref[n], axis=-1, keepdims=True)
        mx_ref[n] = jnp.broadcast_to(m, (tq, LANES))
    ls_ref[...] = jnp.zeros_like(ls_ref)
    acc_ref[...] = jnp.zeros_like(acc_ref)

    def pass_b(j, carry):
        start = pl.multiple_of(j * tk, tk)
        vj = v_ref[pl.ds(start, tk), :]
        for n in range(2):
            mb = mx_ref[n]
            ps = [jnp.exp2(t - mb) for t in _lane_tiles(s_ref[n, j])]
            ls_ref[n] += functools.reduce(jnp.add, ps)
            p = jnp.concatenate(ps, axis=1).astype(BF16)
            acc_ref[n] += jnp.dot(p, vj, preferred_element_type=F32)
        return carry
    lax.fori_loop(0, qi + 1, pass_b, 0)

    lam = (jnp.exp(jnp.sum(lq1_ref[...] * lk1_ref[...], axis=-1, keepdims=True))
           - jnp.exp(jnp.sum(lq2_ref[...] * lk2_ref[...], axis=-1, keepdims=True)) + lambda_init)
    l0 = jnp.sum(ls_ref[0], axis=-1, keepdims=True)
    l1 = jnp.sum(ls_ref[1], axis=-1, keepdims=True)
    o = acc_ref[0] / l0 - lam * (acc_ref[1] / l1)
    ms = jnp.mean(o * o, axis=-1, keepdims=True)
    o_ref[...] = (o * lax.rsqrt(ms + LN_EPS) * g_ref[...] * (1.0 - lambda_init)).astype(o_ref.dtype)


def _diff_attention(z_main, lq1, lk1, lq2, lk2, norm_g, B, S, lambda_init):
    tq = min(ATTN_TQ, S)
    nq = S // tq
    H = B_HEADS
    dv = B_V_DIM
    slopes = 2.0 ** (-8.0 * jnp.arange(1, H + 1, dtype=F32) / H)
    assert S <= 256 * 256
    pos = lax.broadcasted_iota(I32, (H, S, dv), 1)
    col = lax.broadcasted_iota(I32, (H, S, dv), 2)
    within = pos % 256
    kbias = jnp.where(col == 0, within, jnp.where(col == 1, pos - within, 0)).astype(F32)
    kbias = (kbias * slopes[:, None, None]).astype(BF16)
    small = pl.BlockSpec((1, B_HEAD_DIM), lambda b, h, i: (0, 0))
    return pl.pallas_call(
        functools.partial(_attn_kernel, lambda_init=lambda_init),
        grid=(B, H, nq),
        in_specs=[
            pl.BlockSpec(memory_space=pltpu.SMEM),
            pl.BlockSpec((tq, dv), lambda b, h, i: (b * nq + i, COL_QB // dv + h)),
            pl.BlockSpec((S, dv), lambda b, h, i: (b, COL_KB // dv + h)),
            pl.BlockSpec((None, S, dv), lambda b, h, i: (h, 0, 0)),
            pl.BlockSpec((S, dv), lambda b, h, i: (b, COL_VB // dv + h)),
            small, small, small, small,
            pl.BlockSpec((1, dv), lambda b, h, i: (0, 0)),
        ],
        out_specs=pl.BlockSpec((tq, dv), lambda b, h, i: (b * nq + i, h)),
        out_shape=jax.ShapeDtypeStruct((B * S, B_WIDTH), BF16),
        scratch_shapes=[
            pltpu.VMEM((2, nq, tq, tq), F32),
            pltpu.VMEM((2, tq, LANES), F32),
            pltpu.VMEM((2, tq, LANES), F32),
            pltpu.VMEM((2, tq, dv), F32),
            pltpu.VMEM((tq, tq), F32),
        ],
        compiler_params=_cparams(("parallel", "parallel", "arbitrary")),
        name="diff_attention",
    )(slopes, z_main, z_main, kbias, z_main, lq1, lk1, lq2, lk2, norm_g)


def _mix_kernel(ha_ref, ob_ref, ga_ref, gb_ref, x_ref, wa_ref, wb_ref, wo_ref, g1_ref, b1_ref, wr_ref, br_ref,
                x1_ref, x1p_ref, ri_ref, rw_ref, cnt_ref, carry_ref):
    i = pl.program_id(0)
    tm = x_ref.shape[0]

    @pl.when(i == 0)
    def _():
        carry_ref[...] = jnp.zeros_like(carry_ref)

    ya = jnp.dot(ha_ref[...], wa_ref[...], preferred_element_type=F32)
    yb = jnp.dot(ob_ref[...], wb_ref[...], preferred_element_type=F32)
    merged = _sigmoid(ga_ref[...].astype(F32)) * ya + _sigmoid(gb_ref[...].astype(F32)) * yb
    mix = jnp.dot(merged.astype(BF16), wo_ref[...], preferred_element_type=F32)
    pre = DEEPNORM_ALPHA * x_ref[...] + mix
    mu = jnp.mean(pre, axis=-1, keepdims=True)
    pc = pre - mu
    var = jnp.mean(pc * pc, axis=-1, keepdims=True)
    x1 = pc * lax.rsqrt(var + LN_EPS) * g1_ref[...] + b1_ref[...]
    x1_ref[...] = x1
    _rows_to_tiles(x1p_ref, _pack_halves(x1))

    logits = _dot_3pass(x1, wr_ref[...]) + br_ref[...]
    lt = logits.T
    row8 = lax.broadcasted_iota(I32, (SUBLANES, tm), 0).astype(F32)
    gl = jnp.where(row8 < N_GROUPS, lt[0:SUBLANES, :], NEG_BIG)
    gmax = jnp.max(gl, axis=0, keepdims=True)
    gsel = jnp.min(jnp.where(gl == gmax, row8, float(SUBLANES)), axis=0, keepdims=True)
    gprob = 1.0 / jnp.sum(jnp.exp(gl - gmax), axis=0, keepdims=True)
    ing = lt[SUBLANES:2 * SUBLANES, :]
    for g in range(1, N_GROUPS):
        ing = jnp.where(gsel == g, lt[(g + 1) * SUBLANES:(g + 2) * SUBLANES, :], ing)
    v0 = jnp.max(ing, axis=0, keepdims=True)
    i0 = jnp.min(jnp.where(ing == v0, row8, float(SUBLANES)), axis=0, keepdims=True)
    ing2 = jnp.where(row8 == i0, -jnp.inf, ing)
    v1 = jnp.max(ing2, axis=0, keepdims=True)
    i1 = jnp.min(jnp.where(ing2 == v1, row8, float(SUBLANES)), axis=0, keepdims=True)
    ex = jnp.exp(v1 - v0)
    inv = 1.0 / (1.0 + ex)
    w0 = gprob * inv
    w1 = gprob * (ex * inv)
    e0 = (gsel * EXPERTS_PER_GROUP + i0).astype(I32)
    e1 = (gsel * EXPERTS_PER_GROUP + i1).astype(I32)

    rowe = lax.broadcasted_iota(I32, (N_EXPERTS, tm), 0)
    is0 = rowe == e0
    is1 = rowe == e1
    oh = jnp.where(is0, 1.0, 0.0) + jnp.where(is1, 1.0, 0.0)
    tr = lax.broadcasted_iota(I32, (tm, tm), 0)
    tc = lax.broadcasted_iota(I32, (tm, tm), 1)
    su = jnp.where(tr < tc, 1.0, 0.0).astype(BF16)
    tot = jnp.dot(oh.astype(BF16), su, preferred_element_type=F32) + carry_ref[...]
    rank0 = jnp.sum(jnp.where(is0, tot, 0.0), axis=0, keepdims=True)
    rank1 = jnp.sum(jnp.where(is1, tot, 0.0), axis=0, keepdims=True)
    carry = carry_ref[...] + jnp.sum(oh, axis=1, keepdims=True)
    carry_ref[...] = carry
    cnt_ref[...] = jnp.broadcast_to(carry, cnt_ref.shape).astype(I32)

    zi = jnp.zeros((1, tm), I32)
    ri_ref[...] = jnp.concatenate([e0, e1, rank0.astype(I32), rank1.astype(I32), zi, zi, zi, zi], axis=0)
    zf = jnp.zeros((1, tm), F32)
    rw_ref[...] = jnp.concatenate([w0, w1, zf, zf, zf, zf, zf, zf], axis=0)


def _mix(h_a, o_b, z_main, x2d, w_a, w_b, w_out, ln_g, ln_b, w_r, b_r):
    T, D = x2d.shape
    tm = min(MIX_TM, T)
    const = lambda shape: pl.BlockSpec(shape, lambda i: (0, 0), pipeline_mode=pl.Buffered(1))
    return pl.pallas_call(
        _mix_kernel,
        grid=(T // tm,),
        in_specs=[
            pl.BlockSpec((tm, A_WIDTH), lambda i: (i, 0)),
            pl.BlockSpec((tm, B_WIDTH), lambda i: (i, 0)),
            pl.BlockSpec((tm, D), lambda i: (i, COL_GA // D)),
            pl.BlockSpec((tm, D), lambda i: (i, COL_GB // D)),
            pl.BlockSpec((tm, D), lambda i: (i, 0)),
            const((A_WIDTH, D)), const((B_WIDTH, D)), const((D, D)),
            const((1, D)), const((1, D)), const((D, LANES)), const((1, LANES)),
        ],
        out_specs=[
            pl.BlockSpec((tm, D), lambda i: (i, 0)),
            pl.BlockSpec((tm * SUBLANES, LANES), lambda i: (i, 0)),
            pl.BlockSpec((SUBLANES, tm), lambda i: (0, i)),
            pl.BlockSpec((SUBLANES, tm), lambda i: (0, i)),
            pl.BlockSpec((N_EXPERTS, LANES), lambda i: (0, 0)),
        ],
        out_shape=[
            jax.ShapeDtypeStruct((T, D), F32),
            jax.ShapeDtypeStruct((T * SUBLANES, LANES), U32),
            jax.ShapeDtypeStruct((SUBLANES, T), I32),
            jax.ShapeDtypeStruct((SUBLANES, T), F32),
            jax.ShapeDtypeStruct((N_EXPERTS, LANES), I32),
        ],
        scratch_shapes=[pltpu.VMEM((N_EXPERTS, 1), F32)],
        compiler_params=_cparams(("arbitrary",)),
        name="mix_ln1_router",
    )(h_a, o_b, z_main, z_main, x2d, w_a, w_b, w_out, ln_g, ln_b, w_r, b_r)


def _row_copy(src_ref, src_row, dst_ref, dst_row, sem):
    return pltpu.make_async_copy(src_ref.at[pl.ds(src_row * SUBLANES, SUBLANES)],
                                 dst_ref.at[pl.ds(dst_row * SUBLANES, SUBLANES)], sem)


def _rows_wait(src_ref, dst_ref, rows, sem):
    n = rows * SUBLANES
    pltpu.make_async_copy(src_ref.at[pl.ds(0, n)], dst_ref.at[pl.ds(0, n)], sem).wait()


def _dispatch_kernel(cnt_ref, ri_ref, x1_ref, xpad_ref, pstart_ref, bexp_ref, nused_ref,
                     pst_ref, zero_ref, sem, zsem, *, tb, n_blocks, tm):
    i = pl.program_id(0)
    last = pl.num_programs(0) - 1

    @pl.when(i == 0)
    def _():
        def fill(b, carry):
            bexp_ref[b] = 0
            return carry
        lax.fori_loop(0, n_blocks, fill, 0)

        def per_expert(e, blk):
            pst_ref[e] = blk * tb
            pstart_ref[e] = blk * tb
            nb = (cnt_ref[e] + (tb - 1)) // tb

            def mark(b, carry):
                bexp_ref[blk + b] = e
                return carry
            lax.fori_loop(0, nb, mark, 0)
            return blk + nb
        nused = lax.fori_loop(0, N_EXPERTS, per_expert, 0)
        nused_ref[0] = nused
        pst_ref[N_EXPERTS] = nused

    def issue(t, carry):
        for k in range(TOP_K):
            pos = pst_ref[ri_ref[k, t]] + ri_ref[TOP_K + k, t]
            _row_copy(x1_ref, t, xpad_ref, pos, sem).start()
        return carry
    lax.fori_loop(0, tm, issue, 0, unroll=8)
    for k in range(TOP_K):
        _rows_wait(x1_ref, xpad_ref, tm, sem)

    @pl.when(i == last)
    def _():
        zero_ref[...] = jnp.zeros_like(zero_ref)

        def per_expert(e, carry):
            cnt = cnt_ref[e]
            first = pst_ref[e] + cnt
            npad = ((cnt + (tb - 1)) // tb) * tb - cnt

            def start(p, c2):
                _row_copy(zero_ref, 0, xpad_ref, first + p, zsem).start()
                return c2
            lax.fori_loop(0, npad, start, 0)

            def wait(p, c2):
                _row_copy(zero_ref, 0, xpad_ref, 0, zsem).wait()
                return c2
            lax.fori_loop(0, npad, wait, 0)
            return carry
        lax.fori_loop(0, N_EXPERTS, per_expert, 0)

        def block_copy(b):
            n = tb * SUBLANES
            return pltpu.make_async_copy(zero_ref, xpad_ref.at[pl.ds(pl.multiple_of(b * n, n), n)], zsem)

        def start_block(b, carry):
            block_copy(b).start()
            return carry
        lax.fori_loop(pst_ref[N_EXPERTS], n_blocks, start_block, 0)

        def wait_block(b, carry):
            block_copy(b).wait()
            return carry
        lax.fori_loop(pst_ref[N_EXPERTS], n_blocks, wait_block, 0)


def _dispatch(counts, route_i, x1, tb, n_blocks):
    T = x1.shape[0] // SUBLANES
    tm = min(DISPATCH_TM, T)
    smem_full = pl.BlockSpec(memory_space=pltpu.SMEM)
    return pl.pallas_call(
        functools.partial(_dispatch_kernel, tb=tb, n_blocks=n_blocks, tm=tm),
        grid=(T // tm,),
        in_specs=[
            smem_full,
            pl.BlockSpec((SUBLANES, tm), lambda i: (0, i), memory_space=pltpu.SMEM),
            pl.BlockSpec((tm * SUBLANES, LANES), lambda i: (i, 0)),
        ],
        out_specs=[
            pl.BlockSpec(memory_space=pl.ANY),
            smem_full, smem_full, smem_full,
        ],
        out_shape=[
            jax.ShapeDtypeStruct((n_blocks * tb * SUBLANES, LANES), x1.dtype),
            jax.ShapeDtypeStruct((N_EXPERTS,), I32),
            jax.ShapeDtypeStruct((n_blocks,), I32),
            jax.ShapeDtypeStruct((1,), I32),
        ],
        scratch_shapes=[
            pltpu.SMEM((N_EXPERTS + 1,), I32),
            pltpu.VMEM((tb * SUBLANES, LANES), x1.dtype),
            pltpu.SemaphoreType.DMA(()),
            pltpu.SemaphoreType.DMA(()),
        ],
        compiler_params=_cparams(("arbitrary",)),
        name="moe_dispatch",
    )(counts, route_i, x1)


def _expert_kernel(bexp_ref, nused_ref, x_ref, wg_hbm, wu_hbm, wd_hbm, y_ref,
                   wgf_ref, wuf_ref, wdf_ref, wgb_ref, wub_ref, wdb_ref, sem, run_ref):
    b = pl.program_id(0)
    nused = nused_ref[0]
    used = b < nused
    e = bexp_ref[b]
    new_expert = jnp.logical_or(b == 0, e != bexp_ref[jnp.maximum(b - 1, 0)])
    streams = ((wg_hbm, wgf_ref), (wu_hbm, wuf_ref), (wd_hbm, wdf_ref))

    def weight_copies(expert, slot):
        return [pltpu.make_async_copy(w_hbm.at[expert], wf_ref.at[slot], sem.at[slot, j])
                for j, (w_hbm, wf_ref) in enumerate(streams)]

    @pl.when(jnp.logical_and(used, b == 0))
    def _():
        run_ref[0] = 0
        for cp in weight_copies(e, 0):
            cp.start()

    @pl.when(jnp.logical_and(used, new_expert))
    def _():
        slot = run_ref[0] % 2
        nxt = lax.while_loop(lambda j: jnp.logical_and(j < nused, bexp_ref[jnp.minimum(j, nused - 1)] == e),
                             lambda j: j + 1, b + 1)

        @pl.when(nxt < nused)
        def _():
            for cp in weight_copies(bexp_ref[jnp.minimum(nxt, nused - 1)], 1 - slot):
                cp.start()

        for cp in weight_copies(e, slot):
            cp.wait()
        wgb_ref[...] = wgf_ref[slot].astype(BF16)
        wub_ref[...] = wuf_ref[slot].astype(BF16)
        wdb_ref[...] = wdf_ref[slot].astype(BF16)
        run_ref[0] = run_ref[0] + 1

    @pl.when(used)
    def _():
        half = SUBLANES * LANES
        x_lo, x_hi = (t.astype(BF16) for t in _unpack_halves(_tiles_to_rows(x_ref)))
        dot = functools.partial(jnp.dot, preferred_element_type=F32)
        gate = dot(x_lo, wgb_ref[:half, :]) + dot(x_hi, wgb_ref[half:, :])
        up = dot(x_lo, wub_ref[:half, :]) + dot(x_hi, wub_ref[half:, :])
        hid = (gate * _sigmoid(gate) * up).astype(BF16)
        _rows_to_tiles(y_ref, _pack_halves(dot(hid, wdb_ref[...])))

    @pl.when(jnp.logical_not(used))
    def _():
        y_ref[...] = jnp.zeros_like(y_ref)


def _experts(bexp, nused, x_pad, w_gate, w_up, w_down, tb):
    P = x_pad.shape[0] // SUBLANES
    D = 2 * TILE_WORDS
    n_blocks = P // tb
    rows = (tb * SUBLANES, LANES)
    F = w_gate.shape[-1]
    blk = lambda b, be, nu: jnp.maximum(jnp.minimum(b, nu[0] - 1), 0)
    hbm = pl.BlockSpec(memory_space=pl.ANY)
    return pl.pallas_call(
        _expert_kernel,
        grid_spec=pltpu.PrefetchScalarGridSpec(
            num_scalar_prefetch=2,
            grid=(n_blocks,),
            in_specs=[pl.BlockSpec(rows, lambda b, be, nu: (blk(b, be, nu), 0)), hbm, hbm, hbm],
            out_specs=pl.BlockSpec(rows, lambda b, be, nu: (b, 0)),
            scratch_shapes=[
                pltpu.VMEM((2, D, F), F32), pltpu.VMEM((2, D, F), F32), pltpu.VMEM((2, F, D), F32),
                pltpu.VMEM((D, F), BF16), pltpu.VMEM((D, F), BF16), pltpu.VMEM((F, D), BF16),
                pltpu.SemaphoreType.DMA((2, 3)),
                pltpu.SMEM((1,), I32),
            ],
        ),
        out_shape=jax.ShapeDtypeStruct(x_pad.shape, U32),
        compiler_params=_cparams(("arbitrary",)),
        name="moe_experts",
    )(bexp, nused, x_pad, w_gate, w_up, w_down)


COMBINE_SLOTS = 3


def _combine_kernel(pstart_ref, ri_ref, r1_ref, r2_ref, x1_ref, rw_ref, g_ref, b_ref, ypad_ref, o_ref, ybuf_ref, sem):
    i = pl.program_id(0)
    n = pl.num_programs(0)
    tm = x1_ref.shape[0]
    slot = i % COMBINE_SLOTS

    def gather_loop(route_ref, s):
        def issue(t, carry):
            for k in range(TOP_K):
                pos = pstart_ref[route_ref[k, t]] + route_ref[TOP_K + k, t]
                _row_copy(ypad_ref, pos, ybuf_ref.at[s, k], t, sem.at[s]).start()
            return carry
        lax.fori_loop(0, tm, issue, 0, unroll=8)

    def wait_slot(s):
        for k in range(TOP_K):
            _rows_wait(ypad_ref, ybuf_ref.at[s, k], tm, sem.at[s])

    @pl.when(i == 0)
    def _():
        gather_loop(ri_ref, 0)
        gather_loop(r1_ref, 1)

    wait_slot(slot)

    rw = rw_ref[...]
    y0 = _unpack_halves(_tiles_to_rows(ybuf_ref.at[slot, 0]))
    y1 = _unpack_halves(_tiles_to_rows(ybuf_ref.at[slot, 1]))
    ffn = jnp.concatenate([y0[h] * rw[:, 0:1] + y1[h] * rw[:, 1:2] for h in range(2)], axis=1)
    pre = DEEPNORM_ALPHA * x1_ref[...] + ffn
    mu = jnp.mean(pre, axis=-1, keepdims=True)
    pc = pre - mu
    var = jnp.mean(pc * pc, axis=-1, keepdims=True)
    o_ref[...] = pc * lax.rsqrt(var + LN_EPS) * g_ref[...] + b_ref[...]

    nxt = (i + 2) % COMBINE_SLOTS
    for t in range(tm):
        for k in range(TOP_K):
            pos = pstart_ref[r2_ref[k, t]] + r2_ref[TOP_K + k, t]
            _row_copy(ypad_ref, pos, ybuf_ref.at[nxt, k], t, sem.at[nxt]).start()

    @pl.when(i == n - 1)
    def _():
        wait_slot((i + 1) % COMBINE_SLOTS)
        wait_slot(nxt)


def _combine(pstart, route_i, x1, rw_col, ln_g, ln_b, y_pad):
    T, D = x1.shape
    tm = min(COMBINE_TM, T)
    n = T // tm
    return pl.pallas_call(
        _combine_kernel,
        grid=(n,),
        in_specs=[
            pl.BlockSpec(memory_space=pltpu.SMEM),
            pl.BlockSpec((SUBLANES, tm), lambda i: (0, i), memory_space=pltpu.SMEM),
            pl.BlockSpec((SUBLANES, tm), lambda i: (0, jnp.minimum(i + 1, n - 1)), memory_space=pltpu.SMEM),
            pl.BlockSpec((SUBLANES, tm), lambda i: (0, jnp.minimum(i + 2, n - 1)), memory_space=pltpu.SMEM),
            pl.BlockSpec((tm, D), lambda i: (i, 0)),
            pl.BlockSpec((tm, SUBLANES), lambda i: (i, 0)),
            pl.BlockSpec((1, D), lambda i: (0, 0)),
            pl.BlockSpec((1, D), lambda i: (0, 0)),
            pl.BlockSpec(memory_space=pl.ANY),
        ],
        out_specs=pl.BlockSpec((tm, D), lambda i: (i, 0)),
        out_shape=jax.ShapeDtypeStruct((T, D), F32),
        scratch_shapes=[
            pltpu.VMEM((COMBINE_SLOTS, TOP_K, tm * SUBLANES, LANES), U32),
            pltpu.SemaphoreType.DMA((COMBINE_SLOTS,)),
        ],
        compiler_params=_cparams(("arbitrary",)),
        name="moe_combine_ln2",
    )(pstart, route_i, route_i, route_i, x1, rw_col, ln_g, ln_b, y_pad)


N_GATE_COLS = 2 * A_HEADS
SRC_MLSTM = 0
SRC_GATES = 4 * A_WIDTH
SRC_DIFF = SRC_GATES + N_GATE_COLS
SRC_MERGE = SRC_DIFF + 3 * B_WIDTH
PREP_TN = 512


def _wprep_kernel(a_ref, o_ref):
    o_ref[...] = a_ref[...].T.astype(BF16)


def _wgate_kernel(a_ref, o_ref):
    a = a_ref[...]
    rows = jnp.concatenate([a, jnp.zeros((LANES - a.shape[0], a.shape[1]), F32)], axis=0)
    o_ref[...] = rows.T


def _rearrange_in_proj(w_in, b_in):
    K, n_in = w_in.shape
    tn = PREP_TN
    merge_blocks = (2 * D_MODEL) // tn
    mlstm_blocks = (4 * A_WIDTH) // tn
    assert SRC_MERGE % SUBLANES == 0 and SRC_DIFF % SUBLANES == 0

    def src_row(jb):
        s = SUBLANES
        merge = SRC_MERGE // s + (tn // s) * jb
        mlstm = SRC_MLSTM // s + (tn // s) * (jb - merge_blocks)
        diff = SRC_DIFF // s + (tn // s) * (jb - merge_blocks - mlstm_blocks)
        return s * jnp.where(jb < merge_blocks, merge, jnp.where(jb < merge_blocks + mlstm_blocks, mlstm, diff))

    w_t = jnp.swapaxes(w_in, 0, 1)
    w_main = pl.pallas_call(
        _wprep_kernel,
        grid=(N_MAIN // tn,),
        in_specs=[pl.BlockSpec((pl.Element(tn), pl.Element(K)), lambda jb: (src_row(jb), 0))],
        out_specs=pl.BlockSpec((K, tn), lambda jb: (0, jb)),
        out_shape=jax.ShapeDtypeStruct((K, N_MAIN), BF16),
        compiler_params=_cparams(("parallel",)),
        name="w_in_prep",
    )(w_t)
    b_main = jnp.concatenate([b_in[SRC_MERGE:], b_in[SRC_MLSTM:SRC_GATES], b_in[SRC_DIFF:SRC_MERGE]])[None, :]
    w_gate = pl.pallas_call(
        _wgate_kernel,
        grid=(1,),
        in_specs=[pl.BlockSpec((pl.Element(N_GATE_COLS), pl.Element(K)), lambda i: (SRC_GATES, 0))],
        out_specs=pl.BlockSpec((K, LANES), lambda i: (0, 0)),
        out_shape=jax.ShapeDtypeStruct((K, LANES), F32),
        name="w_gate_prep",
    )(w_t)
    b_gate = jnp.pad(b_in[SRC_GATES:SRC_DIFF], (0, LANES - N_GATE_COLS))[None, :]
    return w_main, b_main, w_gate, b_gate


def _layer(x, w_in, b_in, conv_w, conv_b, norm_a_g, lq1, lk1, lq2, lk2, norm_b_g, w_a, w_b, w_out,
           ln1_g, ln1_b, w_grp, b_grp, w_exp, b_exp, w_gate, w_up, w_down, ln2_g, ln2_b, lambda_init):
    B, S, D = x.shape
    T = B * S
    x2d = x.reshape(T, D)

    w_main, b_main, w_g, b_g = _rearrange_in_proj(w_in, b_in)
    z_main, z_gate = _in_projection(x2d, w_main, b_main, w_g, b_g)

    L = min(MLSTM_CHUNK, S)
    gp = _gate_prep(z_gate[:, :2 * A_HEADS].T, L)
    h_a = _mlstm(z_main, gp.T, gp, conv_w, conv_b[None, :], norm_a_g[None, :], B, S)
    o_b = _diff_attention(z_main, lq1[None, :], lk1[None, :], lq2[None, :], lk2[None, :],
                          norm_b_g[None, :], B, S, lambda_init)

    w_r = jnp.zeros((D, LANES), F32).at[:, :N_GROUPS].set(w_grp).at[:, SUBLANES:SUBLANES + N_EXPERTS].set(w_exp)
    b_r = jnp.zeros((LANES,), F32).at[:N_GROUPS].set(b_grp).at[SUBLANES:SUBLANES + N_EXPERTS].set(b_exp)[None, :]
    x1, x1_packed, route_i, route_w, counts = _mix(h_a, o_b, z_main, x2d, w_a.astype(BF16), w_b.astype(BF16),
                                                   w_out.astype(BF16), ln1_g[None, :], ln1_b[None, :], w_r, b_r)

    tb = MOE_TB
    n_blocks = (T * TOP_K) // tb + N_EXPERTS
    x_pad, pstart, bexp, nused = _dispatch(counts[:, 0], route_i, x1_packed, tb, n_blocks)
    y_pad = _experts(bexp, nused, x_pad, w_gate, w_up, w_down, tb)
    out = _combine(pstart, route_i, x1, route_w.T, ln2_g[None, :], ln2_b[None, :], y_pad)
    return out.reshape(B, S, D)


def kernel(x, w_in, b_in, conv_w, conv_b, mlstm_norm_g, lambda_q1, lambda_k1, lambda_q2, lambda_k2,
           diff_norm_g, w_a, w_b, w_out, ln1_g, ln1_b, w_grp, b_grp, w_exp, b_exp,
           w_gate, w_up, w_down, ln2_g, ln2_b):
    for l in range(DEPTH):
        lambda_init = 0.8 - 0.6 * math.exp(-0.3 * l)
        x = _layer(x, w_in[l], b_in[l], conv_w[l], conv_b[l], mlstm_norm_g[l], lambda_q1[l], lambda_k1[l],
                   lambda_q2[l], lambda_k2[l], diff_norm_g[l], w_a[l], w_b[l], w_out[l], ln1_g[l], ln1_b[l],
                   w_grp[l], b_grp[l], w_exp[l], b_exp[l], w_gate[l], w_up[l], w_down[l], ln2_g[l], ln2_b[l],
                   lambda_init)
    return x
```

```python
import functools
import math

import jax
import jax.numpy as jnp
from jax import lax
from jax.experimental import pallas as pl
from jax.experimental.pallas import tpu as pltpu

F32 = jnp.float32
BF16 = jnp.bfloat16
I32 = jnp.int32

D_MODEL = 2048
A_HEADS = 4
A_HEAD_DIM = 256
A_WIDTH = A_HEADS * A_HEAD_DIM
CONV_WIDTH = 4
B_HEADS = 8
B_HEAD_DIM = 64
B_V_DIM = 2 * B_HEAD_DIM
B_WIDTH = B_HEADS * B_V_DIM
ATTN_CHUNK = 64
N_GROUPS = 4
EXPERTS_PER_GROUP = 8
N_EXPERTS = N_GROUPS * EXPERTS_PER_GROUP
TOP_K = 2
D_EXPERT = 512
DEPTH = 1
DEEPNORM_ALPHA = (2 * DEPTH) ** 0.25
LN_EPS = 1e-5
NEG_BIG = -1e30
LOG2_E = 1.4426950408889634

LANES = 128
SUBLANES = 8
VMEM_LIMIT_BYTES = 56 * 1024 * 1024

COL_GA = 0
COL_GB = COL_GA + D_MODEL
COL_QA = COL_GB + D_MODEL
COL_KA = COL_QA + A_WIDTH
COL_VA = COL_KA + A_WIDTH
COL_OA = COL_VA + A_WIDTH
COL_QB = COL_OA + A_WIDTH
COL_KB = COL_QB + B_WIDTH
COL_VB = COL_KB + B_WIDTH
N_MAIN = COL_VB + B_WIDTH

PROJ_TM, PROJ_TN = 1024, 1024
MLSTM_CHUNK = 256
ATTN_TQ = 512
MIX_TM = 256
MOE_TB = 256
DISPATCH_TM = 256
COMBINE_TM = 256


def _cparams(sem, vmem=VMEM_LIMIT_BYTES):
    return pltpu.CompilerParams(dimension_semantics=sem, vmem_limit_bytes=vmem)


def _sigmoid(x):
    return 1.0 / (1.0 + jnp.exp(-x))


U32 = jnp.uint32


def _pack_halves(x):
    n = x.shape[1] // 2
    lo = lax.bitcast_convert_type(x[:, :n].astype(BF16).astype(F32), U32)
    hi = lax.bitcast_convert_type(x[:, n:].astype(BF16).astype(F32), U32)
    return (lo >> 16) | hi


def _unpack_halves(u):
    lo = lax.bitcast_convert_type(u << 16, F32)
    hi = lax.bitcast_convert_type(u & jnp.uint32(0xFFFF0000), F32)
    return lo, hi


TILE_WORDS = SUBLANES * LANES


def _rows_to_tiles(ref, v):
    rows = v.shape[0]
    for j in range(SUBLANES):
        ref[pl.ds(j, rows, stride=SUBLANES), :] = v[:, j * LANES:(j + 1) * LANES]


def _tiles_to_rows(ref):
    rows = ref.shape[0] // SUBLANES
    return jnp.concatenate([ref[pl.ds(j, rows, stride=SUBLANES), :] for j in range(SUBLANES)], axis=1)


def _split_bf16(a):
    hi = a.astype(BF16)
    return hi, (a - hi.astype(F32)).astype(BF16)


def _dot_3pass(a, b):
    a_hi, a_lo = _split_bf16(a)
    b_hi, b_lo = _split_bf16(b)
    dot = functools.partial(jnp.dot, preferred_element_type=F32)
    return dot(a_hi, b_hi) + (dot(a_hi, b_lo) + dot(a_lo, b_hi))


def _proj_kernel(x_ref, w_ref, b_ref, wg_ref, bg_ref, z_ref, zg_ref, xb_ref):
    @pl.when(pl.program_id(1) == 0)
    def _():
        x = x_ref[...]
        xb_ref[...] = x.astype(BF16)
        zg_ref[...] = _dot_3pass(x, wg_ref[...]) + bg_ref[...]

    acc = jnp.dot(xb_ref[...], w_ref[...], preferred_element_type=F32)
    z_ref[...] = (acc + b_ref[...]).astype(z_ref.dtype)


def _in_projection(x2d, w_main, b_main, w_gate, b_gate):
    T, K = x2d.shape
    N = w_main.shape[1]
    tm, tn = min(PROJ_TM, T), PROJ_TN
    return pl.pallas_call(
        _proj_kernel,
        grid=(T // tm, N // tn),
        in_specs=[
            pl.BlockSpec((tm, K), lambda i, j: (i, 0)),
            pl.BlockSpec((K, tn), lambda i, j: (0, j)),
            pl.BlockSpec((1, tn), lambda i, j: (0, j)),
            pl.BlockSpec((K, LANES), lambda i, j: (0, 0)),
            pl.BlockSpec((1, LANES), lambda i, j: (0, 0)),
        ],
        out_specs=[
            pl.BlockSpec((tm, tn), lambda i, j: (i, j)),
            pl.BlockSpec((tm, LANES), lambda i, j: (i, 0)),
        ],
        out_shape=[
            jax.ShapeDtypeStruct((T, N), BF16),
            jax.ShapeDtypeStruct((T, LANES), F32),
        ],
        scratch_shapes=[pltpu.VMEM((tm, K), BF16)],
        compiler_params=_cparams(("parallel", "arbitrary")),
        name="in_projection",
    )(x2d, w_main, b_main, w_gate, b_gate)


def _gate_prep_kernel(g_ref, o_ref, *, L):
    r = lax.broadcasted_iota(I32, (L, L), 0)
    c = lax.broadcasted_iota(I32, (L, L), 1)
    tri = (r <= c).astype(F32)
    row = lax.broadcasted_iota(I32, (g_ref.shape[0], L), 0)
    for ch in range(g_ref.shape[1] // L):
        g = g_ref[:, ch * L:(ch + 1) * L]
        lf = jnp.minimum(g, 0.0) - jnp.log(1.0 + jnp.exp(-jnp.abs(g)))
        b = jnp.dot(lf, tri, preferred_element_type=F32, precision=lax.Precision.HIGHEST)
        o_ref[:, ch * L:(ch + 1) * L] = jnp.where(row < A_HEADS, g, b)


def _gate_prep(g_rows, L):
    R, T = g_rows.shape
    per_step = math.gcd(T // L, 8)
    W = L * per_step
    return pl.pallas_call(
        functools.partial(_gate_prep_kernel, L=L),
        grid=(T // W,),
        in_specs=[pl.BlockSpec((R, W), lambda i: (0, i))],
        out_specs=pl.BlockSpec((R, W), lambda i: (0, i)),
        out_shape=jax.ShapeDtypeStruct((R, T), F32),
        compiler_params=_cparams(("parallel",)),
        name="gate_prep",
    )(g_rows)


def _mlstm_kernel(q_ref, k_ref, v_ref, o_ref, gcol_ref, grow_ref, cw_ref, cb_ref, ng_ref, out_ref,
                  c_ref, n_ref, m_ref, qcar_ref, kcar_ref):
    ci = pl.program_id(1)
    L = q_ref.shape[0]
    dh = A_HEAD_DIM

    @pl.when(ci == 0)
    def _():
        c_ref[...] = jnp.zeros_like(c_ref)
        n_ref[...] = jnp.zeros_like(n_ref)
        m_ref[...] = jnp.zeros_like(m_ref)
        qcar_ref[...] = jnp.zeros_like(qcar_ref)
        kcar_ref[...] = jnp.zeros_like(kcar_ref)

    def conv_silu(u_ref, car_ref, lo, wcol):
        u = u_ref[:, lo:lo + dh].astype(F32)
        ext = jnp.concatenate([car_ref[:, lo:lo + dh], u], axis=0)
        w = cw_ref[:, wcol:wcol + dh]
        y = cb_ref[:, wcol:wcol + dh] + w[3:4, :] * u
        for j in range(CONV_WIDTH - 1):
            off = SUBLANES - (CONV_WIDTH - 1) + j
            y = y + w[j:j + 1, :] * ext[off:off + L, :]
        car_ref[:, lo:lo + dh] = u[L - SUBLANES:, :]
        return y * _sigmoid(y)

    r = lax.broadcasted_iota(I32, (L, L), 0)
    c = lax.broadcasted_iota(I32, (L, L), 1)
    causal = r >= c
    gcol = gcol_ref[...]
    grow = grow_ref[...]

    for h in range(A_HEADS):
        lo = h * dh
        q = conv_silu(q_ref, qcar_ref, lo, lo)
        k = conv_silu(k_ref, kcar_ref, lo, A_WIDTH + lo) * (dh ** -0.5)
        v_b = v_ref[:, lo:lo + dh]
        q_b = q.astype(BF16)
        k_b = k.astype(BF16)

        i_col, b_col = gcol[:, h:h + 1], gcol[:, A_HEADS + h:A_HEADS + h + 1]
        i_row, b_row = grow[h:h + 1, :], grow[A_HEADS + h:A_HEADS + h + 1, :]
        b_last = b_row[:, L - 1:L]
        m_prev = m_ref[h]

        dmat = jnp.where(causal, b_col - b_row + i_row, NEG_BIG)
        inter = b_col + m_prev
        m_t = jnp.maximum(inter, jnp.max(dmat, axis=-1, keepdims=True))
        s = lax.dot_general(q_b, k_b, (((1,), (1,)), ((), ())), preferred_element_type=F32)
        w_intra = jnp.exp(dmat - m_t) * s
        w_inter = jnp.exp(inter - m_t)
        qc = jnp.dot(q_b, c_ref[h].astype(BF16), preferred_element_type=F32)
        num = w_inter * qc + jnp.dot(w_intra.astype(BF16), v_b, preferred_element_type=F32)
        qn = jnp.sum(q * n_ref[h], axis=-1, keepdims=True)
        den = w_inter * qn + jnp.sum(w_intra, axis=-1, keepdims=True)
        hh = num / jnp.maximum(jnp.abs(den), jnp.exp(-m_t))

        g_col = b_last - b_col + i_col
        m_new = jnp.maximum(b_last + m_prev, jnp.max(g_col, axis=0, keepdims=True))
        decay = jnp.exp(b_last + m_prev - m_new)
        kw = jnp.exp(g_col - m_new) * k
        c_ref[h] = decay * c_ref[h] + jnp.dot(kw.T.astype(BF16), v_b, preferred_element_type=F32)
        n_ref[h] = decay * n_ref[h] + jnp.sum(kw, axis=0, keepdims=True)
        m_ref[h] = m_new

        y = _sigmoid(o_ref[:, lo:lo + dh].astype(F32)) * hh
        mu = jnp.mean(y, axis=-1, keepdims=True)
        yc = y - mu
        var = jnp.mean(yc * yc, axis=-1, keepdims=True)
        out_ref[:, lo:lo + dh] = (yc * lax.rsqrt(var + LN_EPS) * ng_ref[:, lo:lo + dh]).astype(out_ref.dtype)


def _mlstm(z_main, gcol, grow, conv_w, conv_b, norm_g, B, S):
    L = min(MLSTM_CHUNK, S)
    nc = S // L
    dh = A_HEAD_DIM
    H = A_HEADS
    W = A_WIDTH
    row = lambda b, c: b * nc + c
    full = lambda shape: pl.BlockSpec(shape, lambda b, c: (0, 0))
    return pl.pallas_call(
        _mlstm_kernel,
        grid=(B, nc),
        in_specs=[
            pl.BlockSpec((L, W), lambda b, c: (row(b, c), COL_QA // W)),
            pl.BlockSpec((L, W), lambda b, c: (row(b, c), COL_KA // W)),
            pl.BlockSpec((L, W), lambda b, c: (row(b, c), COL_VA // W)),
            pl.BlockSpec((L, W), lambda b, c: (row(b, c), COL_OA // W)),
            pl.BlockSpec((L, SUBLANES), lambda b, c: (row(b, c), 0)),
            pl.BlockSpec((SUBLANES, L), lambda b, c: (0, row(b, c))),
            full((CONV_WIDTH, 2 * W)), full((1, 2 * W)), full((1, W)),
        ],
        out_specs=pl.BlockSpec((L, W), lambda b, c: (row(b, c), 0)),
        out_shape=jax.ShapeDtypeStruct((B * S, W), BF16),
        scratch_shapes=[
            pltpu.VMEM((H, dh, dh), F32),
            pltpu.VMEM((H, 1, dh), F32),
            pltpu.VMEM((H, 1, 1), F32),
            pltpu.VMEM((SUBLANES, W), F32),
            pltpu.VMEM((SUBLANES, W), F32),
        ],
        compiler_params=_cparams(("parallel", "arbitrary")),
        name="mlstm",
    )(z_main, z_main, z_main, z_main, gcol, grow, conv_w, conv_b, norm_g)


def _lane_tiles(t):
    return [t[:, i * LANES:(i + 1) * LANES] for i in range(t.shape[1] // LANES)]


def _attn_kernel(slope_ref, q_ref, k_ref, kb_ref, v_ref, lq1_ref, lk1_ref, lq2_ref, lk2_ref, g_ref, o_ref,
                 s_ref, mx_ref, ls_ref, acc_ref, corr_ref, *, lambda_init):
    h = pl.program_id(1)
    qi = pl.program_id(2)
    tq = q_ref.shape[0]
    tk = tq
    slope = slope_ref[h]
    nt = (((1,), (1,)), ((), ()))

    lane = lax.broadcasted_iota(I32, (tq, B_V_DIM), 1)
    q = q_ref[...] * (B_HEAD_DIM ** -0.5)
    zero = jnp.zeros_like(q)
    bias_cols = jnp.where(lane < 2, 1.0, 0.0).astype(BF16)
    qa = (jnp.concatenate([jnp.where(lane < B_HEAD_DIM, q, zero), bias_cols], axis=1),
          jnp.concatenate([jnp.where(lane >= B_HEAD_DIM, q, zero), bias_cols], axis=1))

    def scores(j):
        start = pl.multiple_of(j * tk, tk)
        ka = jnp.concatenate([k_ref[pl.ds(start, tk), :], kb_ref[pl.ds(start, tk), :]], axis=1)
        return [lax.dot_general(qa[n], ka, nt, preferred_element_type=F32) for n in range(2)]

    def keep(j, n, t):
        t = t * LOG2_E
        s_ref[n, j] = t
        mx_ref[n] = functools.reduce(jnp.maximum, _lane_tiles(t), mx_ref[n])

    mx_ref[...] = jnp.full(mx_ref.shape, NEG_BIG, F32)

    def pass_a(j):
        t = scores(j)
        for n in range(2):
            keep(j, n, t[n])

    def pass_a_pair(jj, carry):
        pass_a(2 * jj)
        pass_a(2 * jj + 1)
        return carry
    lax.fori_loop(0, qi // 2, pass_a_pair, 0)

    @pl.when(qi % 2 == 1)
    def _():
        pass_a(qi - 1)

    @pl.when(qi == 0)
    def _():
        r = lax.broadcasted_iota(I32, (tq, tk), 0)
        c = lax.broadcasted_iota(I32, (tq, tk), 1)
        ahead = jnp.where(c <= r, 0.0, (r - c).astype(F32) * (2.0 * slope))
        corr_ref[...] = jnp.where((c // ATTN_CHUNK) <= (r // ATTN_CHUNK), ahead, NEG_BIG)

    t = scores(qi)
    for n in range(2):
        keep(qi, n, t[n] + corr_ref[...])

    for n in range(2):
        m = jnp.max(mx_ref[n], axis=-1, keepdims=True)
        mx_ref[n] = jnp.broadcast_to(m, (tq, LANES))
    ls_ref[...] = jnp.zeros_like(ls_ref)
    acc_ref[...] = jnp.zeros_like(acc_ref)

    def pass_b(j, carry):
        start = pl.multiple_of(j * tk, tk)
        vj = v_ref[pl.ds(start, tk), :]
        for n in range(2):
            mb = mx_ref[n]
            ps = [jnp.exp2(t - mb) for t in _lane_tiles(s_ref[n, j])]
            ls_ref[n] += functools.reduce(jnp.add, ps)
            p = jnp.concatenate(ps, axis=1).astype(BF16)
            acc_ref[n] += jnp.dot(p, vj, preferred_element_type=F32)
        return carry
    lax.fori_loop(0, qi + 1, pass_b, 0)

    lam = (jnp.exp(jnp.sum(lq1_ref[...] * lk1_ref[...], axis=-1, keepdims=True))
           - jnp.exp(jnp.sum(lq2_ref[...] * lk2_ref[...], axis=-1, keepdims=True)) + lambda_init)
    l0 = jnp.sum(ls_ref[0], axis=-1, keepdims=True)
    l1 = jnp.sum(ls_ref[1], axis=-1, keepdims=True)
    o = acc_ref[0] / l0 - lam * (acc_ref[1] / l1)
    ms = jnp.mean(o * o, axis=-1, keepdims=True)
    o_ref[...] = (o * lax.rsqrt(ms + LN_EPS) * g_ref[...] * (1.0 - lambda_init)).astype(o_ref.dtype)


def _diff_attention(z_main, lq1, lk1, lq2, lk2, norm_g, B, S, lambda_init):
    tq = min(ATTN_TQ, S)
    nq = S // tq
    H = B_HEADS
    dv = B_V_DIM
    slopes = 2.0 ** (-8.0 * jnp.arange(1, H + 1, dtype=F32) / H)
    assert S <= 256 * 256
    pos = lax.broadcasted_iota(I32, (H, S, dv), 1)
    col = lax.broadcasted_iota(I32, (H, S, dv), 2)
    within = pos % 256
    kbias = jnp.where(col == 0, within, jnp.where(col == 1, pos - within, 0)).astype(F32)
    kbias = (kbias * slopes[:, None, None]).astype(BF16)
    small = pl.BlockSpec((1, B_HEAD_DIM), lambda b, h, i: (0, 0))
    return pl.pallas_call(
        functools.partial(_attn_kernel, lambda_init=lambda_init),
        grid=(B, H, nq),
        in_specs=[
            pl.BlockSpec(memory_space=pltpu.SMEM),
            pl.BlockSpec((tq, dv), lambda b, h, i: (b * nq + i, COL_QB // dv + h)),
            pl.BlockSpec((S, dv), lambda b, h, i: (b, COL_KB // dv + h)),
            pl.BlockSpec((None, S, dv), lambda b, h, i: (h, 0, 0)),
            pl.BlockSpec((S, dv), lambda b, h, i: (b, COL_VB // dv + h)),
            small, small, small, small,
            pl.BlockSpec((1, dv), lambda b, h, i: (0, 0)),
        ],
        out_specs=pl.BlockSpec((tq, dv), lambda b, h, i: (b * nq + i, h)),
        out_shape=jax.ShapeDtypeStruct((B * S, B_WIDTH), BF16),
        scratch_shapes=[
            pltpu.VMEM((2, nq, tq, tq), F32),
            pltpu.VMEM((2, tq, LANES), F32),
            pltpu.VMEM((2, tq, LANES), F32),
            pltpu.VMEM((2, tq, dv), F32),
            pltpu.VMEM((tq, tq), F32),
        ],
        compiler_params=_cparams(("parallel", "parallel", "arbitrary")),
        name="diff_attention",
    )(slopes, z_main, z_main, kbias, z_main, lq1, lk1, lq2, lk2, norm_g)


def _mix_kernel(ha_ref, ob_ref, ga_ref, gb_ref, x_ref, wa_ref, wb_ref, wo_ref, g1_ref, b1_ref, wr_ref, br_ref,
                x1_ref, x1p_ref, ri_ref, rw_ref, cnt_ref, carry_ref):
    i = pl.program_id(0)
    tm = x_ref.shape[0]

    @pl.when(i == 0)
    def _():
        carry_ref[...] = jnp.zeros_like(carry_ref)

    ya = jnp.dot(ha_ref[...], wa_ref[...], preferred_element_type=F32)
    yb = jnp.dot(ob_ref[...], wb_ref[...], preferred_element_type=F32)
    merged = _sigmoid(ga_ref[...].astype(F32)) * ya + _sigmoid(gb_ref[...].astype(F32)) * yb
    mix = jnp.dot(merged.astype(BF16), wo_ref[...], preferred_element_type=F32)
    pre = DEEPNORM_ALPHA * x_ref[...] + mix
    mu = jnp.mean(pre, axis=-1, keepdims=True)
    pc = pre - mu
    var = jnp.mean(pc * pc, axis=-1, keepdims=True)
    x1 = pc * lax.rsqrt(var + LN_EPS) * g1_ref[...] + b1_ref[...]
    x1_ref[...] = x1
    _rows_to_tiles(x1p_ref, _pack_halves(x1))

    logits = _dot_3pass(x1, wr_ref[...]) + br_ref[...]
    lt = logits.T
    row8 = lax.broadcasted_iota(I32, (SUBLANES, tm), 0).astype(F32)
    gl = jnp.where(row8 < N_GROUPS, lt[0:SUBLANES, :], NEG_BIG)
    gmax = jnp.max(gl, axis=0, keepdims=True)
    gsel = jnp.min(jnp.where(gl == gmax, row8, float(SUBLANES)), axis=0, keepdims=True)
    gprob = 1.0 / jnp.sum(jnp.exp(gl - gmax), axis=0, keepdims=True)
    ing = lt[SUBLANES:2 * SUBLANES, :]
    for g in range(1, N_GROUPS):
        ing = jnp.where(gsel == g, lt[(g + 1) * SUBLANES:(g + 2) * SUBLANES, :], ing)
    v0 = jnp.max(ing, axis=0, keepdims=True)
    i0 = jnp.min(jnp.where(ing == v0, row8, float(SUBLANES)), axis=0, keepdims=True)
    ing2 = jnp.where(row8 == i0, -jnp.inf, ing)
    v1 = jnp.max(ing2, axis=0, keepdims=True)
    i1 = jnp.min(jnp.where(ing2 == v1, row8, float(SUBLANES)), axis=0, keepdims=True)
    ex = jnp.exp(v1 - v0)
    inv = 1.0 / (1.0 + ex)
    w0 = gprob * inv
    w1 = gprob * (ex * inv)
    e0 = (gsel * EXPERTS_PER_GROUP + i0).astype(I32)
    e1 = (gsel * EXPERTS_PER_GROUP + i1).astype(I32)

    rowe = lax.broadcasted_iota(I32, (N_EXPERTS, tm), 0)
    is0 = rowe == e0
    is1 = rowe == e1
    oh = jnp.where(is0, 1.0, 0.0) + jnp.where(is1, 1.0, 0.0)
    tr = lax.broadcasted_iota(I32, (tm, tm), 0)
    tc = lax.broadcasted_iota(I32, (tm, tm), 1)
    su = jnp.where(tr < tc, 1.0, 0.0).astype(BF16)
    tot = jnp.dot(oh.astype(BF16), su, preferred_element_type=F32) + carry_ref[...]
    rank0 = jnp.sum(jnp.where(is0, tot, 0.0), axis=0, keepdims=True)
    rank1 = jnp.sum(jnp.where(is1, tot, 0.0), axis=0, keepdims=True)
    carry = carry_ref[...] + jnp.sum(oh, axis=1, keepdims=True)
    carry_ref[...] = carry
    cnt_ref[...] = jnp.broadcast_to(carry, cnt_ref.shape).astype(I32)

    zi = jnp.zeros((1, tm), I32)
    ri_ref[...] = jnp.concatenate([e0, e1, rank0.astype(I32), rank1.astype(I32), zi, zi, zi, zi], axis=0)
    zf = jnp.zeros((1, tm), F32)
    rw_ref[...] = jnp.concatenate([w0, w1, zf, zf, zf, zf, zf, zf], axis=0)


def _mix(h_a, o_b, z_main, x2d, w_a, w_b, w_out, ln_g, ln_b, w_r, b_r):
    T, D = x2d.shape
    tm = min(MIX_TM, T)
    const = lambda shape: pl.BlockSpec(shape, lambda i: (0, 0), pipeline_mode=pl.Buffered(1))
    return pl.pallas_call(
        _mix_kernel,
        grid=(T // tm,),
        in_specs=[
            pl.BlockSpec((tm, A_WIDTH), lambda i: (i, 0)),
            pl.BlockSpec((tm, B_WIDTH), lambda i: (i, 0)),
            pl.BlockSpec((tm, D), lambda i: (i, COL_GA // D)),
            pl.BlockSpec((tm, D), lambda i: (i, COL_GB // D)),
            pl.BlockSpec((tm, D), lambda i: (i, 0)),
            const((A_WIDTH, D)), const((B_WIDTH, D)), const((D, D)),
            const((1, D)), const((1, D)), const((D, LANES)), const((1, LANES)),
        ],
        out_specs=[
            pl.BlockSpec((tm, D), lambda i: (i, 0)),
            pl.BlockSpec((tm * SUBLANES, LANES), lambda i: (i, 0)),
            pl.BlockSpec((SUBLANES, tm), lambda i: (0, i)),
            pl.BlockSpec((SUBLANES, tm), lambda i: (0, i)),
            pl.BlockSpec((N_EXPERTS, LANES), lambda i: (0, 0)),
        ],
        out_shape=[
            jax.ShapeDtypeStruct((T, D), F32),
            jax.ShapeDtypeStruct((T * SUBLANES, LANES), U32),
            jax.ShapeDtypeStruct((SUBLANES, T), I32),
            jax.ShapeDtypeStruct((SUBLANES, T), F32),
            jax.ShapeDtypeStruct((N_EXPERTS, LANES), I32),
        ],
        scratch_shapes=[pltpu.VMEM((N_EXPERTS, 1), F32)],
        compiler_params=_cparams(("arbitrary",)),
        name="mix_ln1_router",
    )(h_a, o_b, z_main, z_main, x2d, w_a, w_b, w_out, ln_g, ln_b, w_r, b_r)


def _row_copy(src_ref, src_row, dst_ref, dst_row, sem):
    return pltpu.make_async_copy(src_ref.at[pl.ds(src_row * SUBLANES, SUBLANES)],
                                 dst_ref.at[pl.ds(dst_row * SUBLANES, SUBLANES)], sem)


def _rows_wait(src_ref, dst_ref, rows, sem):
    n = rows * SUBLANES
    pltpu.make_async_copy(src_ref.at[pl.ds(0, n)], dst_ref.at[pl.ds(0, n)], sem).wait()


DISPATCH_SLOTS = 3


def _dispatch_kernel(cnt_ref, ri_ref, x1_ref, xpad_ref, pstart_ref, bexp_ref, nused_ref,
                     pst_ref, zero_ref, xbuf_ref, sem, lsem, zsem, *, tb, n_blocks, tm):
    i = pl.program_id(0)
    last = pl.num_programs(0) - 1

    @pl.when(i == 0)
    def _():
        def fill(b, carry):
            bexp_ref[b] = 0
            return carry
        lax.fori_loop(0, n_blocks, fill, 0)

        def per_expert(e, blk):
            pst_ref[e] = blk * tb
            pstart_ref[e] = blk * tb
            nb = (cnt_ref[e] + (tb - 1)) // tb

            def mark(b, carry):
                bexp_ref[blk + b] = e
                return carry
            lax.fori_loop(0, nb, mark, 0)
            return blk + nb
        nused = lax.fori_loop(0, N_EXPERTS, per_expert, 0)
        nused_ref[0] = nused
        pst_ref[N_EXPERTS] = nused

    rows = tm * SUBLANES
    slot = i % DISPATCH_SLOTS

    def tile_load(tile, s):
        src = x1_ref.at[pl.ds(pl.multiple_of(tile * rows, rows), rows)]
        return pltpu.make_async_copy(src, xbuf_ref.at[s], lsem.at[s])

    def wait_rows(s):
        for k in range(TOP_K):
            _rows_wait(xbuf_ref.at[s], xpad_ref, tm, sem.at[s])

    @pl.when(i == 0)
    def _():
        tile_load(0, 0).start()

        @pl.when(last >= 1)
        def _():
            tile_load(1, 1).start()

    tile_load(i, slot).wait()
    for t in range(tm):
        for k in range(TOP_K):
            pos = pst_ref[ri_ref[k, t]] + ri_ref[TOP_K + k, t]
            _row_copy(xbuf_ref.at[slot], t, xpad_ref, pos, sem.at[slot]).start()

    prev = (i + DISPATCH_SLOTS - 1) % DISPATCH_SLOTS

    @pl.when(i >= 1)
    def _():
        wait_rows(prev)

    @pl.when(i + 2 <= last)
    def _():
        tile_load(i + 2, prev).start()

    @pl.when(i == last)
    def _():
        wait_rows(slot)
        zero_ref[...] = jnp.zeros_like(zero_ref)

        def per_expert(e, carry):
            cnt = cnt_ref[e]
            first = pst_ref[e] + cnt
            npad = ((cnt + (tb - 1)) // tb) * tb - cnt

            def start(p, c2):
                _row_copy(zero_ref, 0, xpad_ref, first + p, zsem).start()
                return c2
            lax.fori_loop(0, npad, start, 0)

            def wait(p, c2):
                _row_copy(zero_ref, 0, xpad_ref, 0, zsem).wait()
                return c2
            lax.fori_loop(0, npad, wait, 0)
            return carry
        lax.fori_loop(0, N_EXPERTS, per_expert, 0)

        def block_copy(b):
            n = tb * SUBLANES
            return pltpu.make_async_copy(zero_ref, xpad_ref.at[pl.ds(pl.multiple_of(b * n, n), n)], zsem)

        def start_block(b, carry):
            block_copy(b).start()
            return carry
        lax.fori_loop(pst_ref[N_EXPERTS], n_blocks, start_block, 0)

        def wait_block(b, carry):
            block_copy(b).wait()
            return carry
        lax.fori_loop(pst_ref[N_EXPERTS], n_blocks, wait_block, 0)


def _dispatch(counts, route_i, x1, tb, n_blocks):
    T = x1.shape[0] // SUBLANES
    tm = min(DISPATCH_TM, T)
    smem_full = pl.BlockSpec(memory_space=pltpu.SMEM)
    return pl.pallas_call(
        functools.partial(_dispatch_kernel, tb=tb, n_blocks=n_blocks, tm=tm),
        grid=(T // tm,),
        in_specs=[
            smem_full,
            pl.BlockSpec((SUBLANES, tm), lambda i: (0, i), memory_space=pltpu.SMEM),
            pl.BlockSpec(memory_space=pl.ANY),
        ],
        out_specs=[
            pl.BlockSpec(memory_space=pl.ANY),
            smem_full, smem_full, smem_full,
        ],
        out_shape=[
            jax.ShapeDtypeStruct((n_blocks * tb * SUBLANES, LANES), x1.dtype),
            jax.ShapeDtypeStruct((N_EXPERTS,), I32),
            jax.ShapeDtypeStruct((n_blocks,), I32),
            jax.ShapeDtypeStruct((1,), I32),
        ],
        scratch_shapes=[
            pltpu.SMEM((N_EXPERTS + 1,), I32),
            pltpu.VMEM((tb * SUBLANES, LANES), x1.dtype),
            pltpu.VMEM((DISPATCH_SLOTS, tm * SUBLANES, LANES), x1.dtype),
            pltpu.SemaphoreType.DMA((DISPATCH_SLOTS,)),
            pltpu.SemaphoreType.DMA((DISPATCH_SLOTS,)),
            pltpu.SemaphoreType.DMA(()),
        ],
        compiler_params=_cparams(("arbitrary",)),
        name="moe_dispatch",
    )(counts, route_i, x1)


def _expert_kernel(bexp_ref, nused_ref, x_ref, wg_hbm, wu_hbm, wd_hbm, y_ref,
                   wgf_ref, wuf_ref, wdf_ref, wgb_ref, wub_ref, wdb_ref, sem, run_ref):
    b = pl.program_id(0)
    nused = nused_ref[0]
    used = b < nused
    e = bexp_ref[b]
    new_expert = jnp.logical_or(b == 0, e != bexp_ref[jnp.maximum(b - 1, 0)])
    streams = ((wg_hbm, wgf_ref), (wu_hbm, wuf_ref), (wd_hbm, wdf_ref))

    def weight_copies(expert, slot):
        return [pltpu.make_async_copy(w_hbm.at[expert], wf_ref.at[slot], sem.at[slot, j])
                for j, (w_hbm, wf_ref) in enumerate(streams)]

    @pl.when(jnp.logical_and(used, b == 0))
    def _():
        run_ref[0] = 0
        for cp in weight_copies(e, 0):
            cp.start()

    @pl.when(jnp.logical_and(used, new_expert))
    def _():
        slot = run_ref[0] % 2
        nxt = lax.while_loop(lambda j: jnp.logical_and(j < nused, bexp_ref[jnp.minimum(j, nused - 1)] == e),
                             lambda j: j + 1, b + 1)

        @pl.when(nxt < nused)
        def _():
            for cp in weight_copies(bexp_ref[jnp.minimum(nxt, nused - 1)], 1 - slot):
                cp.start()

        for cp in weight_copies(e, slot):
            cp.wait()
        wgb_ref[...] = wgf_ref[slot].astype(BF16)
        wub_ref[...] = wuf_ref[slot].astype(BF16)
        wdb_ref[...] = wdf_ref[slot].astype(BF16)
        run_ref[0] = run_ref[0] + 1

    @pl.when(used)
    def _():
        half = SUBLANES * LANES
        x_lo, x_hi = (t.astype(BF16) for t in _unpack_halves(_tiles_to_rows(x_ref)))
        dot = functools.partial(jnp.dot, preferred_element_type=F32)
        gate = dot(x_lo, wgb_ref[:half, :]) + dot(x_hi, wgb_ref[half:, :])
        up = dot(x_lo, wub_ref[:half, :]) + dot(x_hi, wub_ref[half:, :])
        hid = (gate * _sigmoid(gate) * up).astype(BF16)
        _rows_to_tiles(y_ref, _pack_halves(dot(hid, wdb_ref[...])))

    @pl.when(jnp.logical_not(used))
    def _():
        y_ref[...] = jnp.zeros_like(y_ref)


def _experts(bexp, nused, x_pad, w_gate, w_up, w_down, tb):
    P = x_pad.shape[0] // SUBLANES
    D = 2 * TILE_WORDS
    n_blocks = P // tb
    rows = (tb * SUBLANES, LANES)
    F = w_gate.shape[-1]
    blk = lambda b, be, nu: jnp.maximum(jnp.minimum(b, nu[0] - 1), 0)
    hbm = pl.BlockSpec(memory_space=pl.ANY)
    return pl.pallas_call(
        _expert_kernel,
        grid_spec=pltpu.PrefetchScalarGridSpec(
            num_scalar_prefetch=2,
            grid=(n_blocks,),
            in_specs=[pl.BlockSpec(rows, lambda b, be, nu: (blk(b, be, nu), 0)), hbm, hbm, hbm],
            out_specs=pl.BlockSpec(rows, lambda b, be, nu: (b, 0)),
            scratch_shapes=[
                pltpu.VMEM((2, D, F), F32), pltpu.VMEM((2, D, F), F32), pltpu.VMEM((2, F, D), F32),
                pltpu.VMEM((D, F), BF16), pltpu.VMEM((D, F), BF16), pltpu.VMEM((F, D), BF16),
                pltpu.SemaphoreType.DMA((2, 3)),
                pltpu.SMEM((1,), I32),
            ],
        ),
        out_shape=jax.ShapeDtypeStruct(x_pad.shape, U32),
        compiler_params=_cparams(("arbitrary",)),
        name="moe_experts",
    )(bexp, nused, x_pad, w_gate, w_up, w_down)


COMBINE_SLOTS = 3


def _combine_kernel(pstart_ref, ri_ref, r1_ref, r2_ref, x1_ref, rw_ref, g_ref, b_ref, ypad_ref, o_ref, ybuf_ref, sem):
    i = pl.program_id(0)
    n = pl.num_programs(0)
    tm = x1_ref.shape[0]
    slot = i % COMBINE_SLOTS

    def gather_loop(route_ref, s):
        def issue(t, carry):
            for k in range(TOP_K):
                pos = pstart_ref[route_ref[k, t]] + route_ref[TOP_K + k, t]
                _row_copy(ypad_ref, pos, ybuf_ref.at[s, k], t, sem.at[s]).start()
            return carry
        lax.fori_loop(0, tm, issue, 0, unroll=8)

    def wait_slot(s):
        for k in range(TOP_K):
            _rows_wait(ypad_ref, ybuf_ref.at[s, k], tm, sem.at[s])

    @pl.when(i == 0)
    def _():
        gather_loop(ri_ref, 0)
        gather_loop(r1_ref, 1)

    wait_slot(slot)

    rw = rw_ref[...]
    y0 = _unpack_halves(_tiles_to_rows(ybuf_ref.at[slot, 0]))
    y1 = _unpack_halves(_tiles_to_rows(ybuf_ref.at[slot, 1]))
    ffn = jnp.concatenate([y0[h] * rw[:, 0:1] + y1[h] * rw[:, 1:2] for h in range(2)], axis=1)
    pre = DEEPNORM_ALPHA * x1_ref[...] + ffn
    mu = jnp.mean(pre, axis=-1, keepdims=True)
    pc = pre - mu
    var = jnp.mean(pc * pc, axis=-1, keepdims=True)
    o_ref[...] = pc * lax.rsqrt(var + LN_EPS) * g_ref[...] + b_ref[...]

    nxt = (i + 2) % COMBINE_SLOTS
    for t in range(tm):
        for k in range(TOP_K):
            pos = pstart_ref[r2_ref[k, t]] + r2_ref[TOP_K + k, t]
            _row_copy(ypad_ref, pos, ybuf_ref.at[nxt, k], t, sem.at[nxt]).start()

    @pl.when(i == n - 1)
    def _():
        wait_slot((i + 1) % COMBINE_SLOTS)
        wait_slot(nxt)


def _combine(pstart, route_i, x1, rw_col, ln_g, ln_b, y_pad):
    T, D = x1.shape
    tm = min(COMBINE_TM, T)
    n = T // tm
    return pl.pallas_call(
        _combine_kernel,
        grid=(n,),
        in_specs=[
            pl.BlockSpec(memory_space=pltpu.SMEM),
            pl.BlockSpec((SUBLANES, tm), lambda i: (0, i), memory_space=pltpu.SMEM),
            pl.BlockSpec((SUBLANES, tm), lambda i: (0, jnp.minimum(i + 1, n - 1)), memory_space=pltpu.SMEM),
            pl.BlockSpec((SUBLANES, tm), lambda i: (0, jnp.minimum(i + 2, n - 1)), memory_space=pltpu.SMEM),
            pl.BlockSpec((tm, D), lambda i: (i, 0)),
            pl.BlockSpec((tm, SUBLANES), lambda i: (i, 0)),
            pl.BlockSpec((1, D), lambda i: (0, 0)),
            pl.BlockSpec((1, D), lambda i: (0, 0)),
            pl.BlockSpec(memory_space=pl.ANY),
        ],
        out_specs=pl.BlockSpec((tm, D), lambda i: (i, 0)),
        out_shape=jax.ShapeDtypeStruct((T, D), F32),
        scratch_shapes=[
            pltpu.VMEM((COMBINE_SLOTS, TOP_K, tm * SUBLANES, LANES), U32),
            pltpu.SemaphoreType.DMA((COMBINE_SLOTS,)),
        ],
        compiler_params=_cparams(("arbitrary",)),
        name="moe_combine_ln2",
    )(pstart, route_i, route_i, route_i, x1, rw_col, ln_g, ln_b, y_pad)


N_GATE_COLS = 2 * A_HEADS
SRC_MLSTM = 0
SRC_GATES = 4 * A_WIDTH
SRC_DIFF = SRC_GATES + N_GATE_COLS
SRC_MERGE = SRC_DIFF + 3 * B_WIDTH
PREP_TN = 512


def _wprep_kernel(a_ref, o_ref):
    o_ref[...] = a_ref[...].T.astype(BF16)


def _wgate_kernel(a_ref, o_ref):
    a = a_ref[...]
    rows = jnp.concatenate([a, jnp.zeros((LANES - a.shape[0], a.shape[1]), F32)], axis=0)
    o_ref[...] = rows.T


def _rearrange_in_proj(w_in, b_in):
    K, n_in = w_in.shape
    tn = PREP_TN
    merge_blocks = (2 * D_MODEL) // tn
    mlstm_blocks = (4 * A_WIDTH) // tn
    assert SRC_MERGE % SUBLANES == 0 and SRC_DIFF % SUBLANES == 0

    def src_row(jb):
        s = SUBLANES
        merge = SRC_MERGE // s + (tn // s) * jb
        mlstm = SRC_MLSTM // s + (tn // s) * (jb - merge_blocks)
        diff = SRC_DIFF // s + (tn // s) * (jb - merge_blocks - mlstm_blocks)
        return s * jnp.where(jb < merge_blocks, merge, jnp.where(jb < merge_blocks + mlstm_blocks, mlstm, diff))

    w_t = jnp.swapaxes(w_in, 0, 1)
    w_main = pl.pallas_call(
        _wprep_kernel,
        grid=(N_MAIN // tn,),
        in_specs=[pl.BlockSpec((pl.Element(tn), pl.Element(K)), lambda jb: (src_row(jb), 0))],
        out_specs=pl.BlockSpec((K, tn), lambda jb: (0, jb)),
        out_shape=jax.ShapeDtypeStruct((K, N_MAIN), BF16),
        compiler_params=_cparams(("parallel",)),
        name="w_in_prep",
    )(w_t)
    b_main = jnp.concatenate([b_in[SRC_MERGE:], b_in[SRC_MLSTM:SRC_GATES], b_in[SRC_DIFF:SRC_MERGE]])[None, :]
    w_gate = pl.pallas_call(
        _wgate_kernel,
        grid=(1,),
        in_specs=[pl.BlockSpec((pl.Element(N_GATE_COLS), pl.Element(K)), lambda i: (SRC_GATES, 0))],
        out_specs=pl.BlockSpec((K, LANES), lambda i: (0, 0)),
        out_shape=jax.ShapeDtypeStruct((K, LANES), F32),
        name="w_gate_prep",
    )(w_t)
    b_gate = jnp.pad(b_in[SRC_GATES:SRC_DIFF], (0, LANES - N_GATE_COLS))[None, :]
    return w_main, b_main, w_gate, b_gate


def _layer(x, w_in, b_in, conv_w, conv_b, norm_a_g, lq1, lk1, lq2, lk2, norm_b_g, w_a, w_b, w_out,
           ln1_g, ln1_b, w_grp, b_grp, w_exp, b_exp, w_gate, w_up, w_down, ln2_g, ln2_b, lambda_init):
    B, S, D = x.shape
    T = B * S
    x2d = x.reshape(T, D)

    w_main, b_main, w_g, b_g = _rearrange_in_proj(w_in, b_in)
    z_main, z_gate = _in_projection(x2d, w_main, b_main, w_g, b_g)

    L = min(MLSTM_CHUNK, S)
    gp = _gate_prep(z_gate[:, :2 * A_HEADS].T, L)
    h_a = _mlstm(z_main, gp.T, gp, conv_w, conv_b[None, :], norm_a_g[None, :], B, S)
    o_b = _diff_attention(z_main, lq1[None, :], lk1[None, :], lq2[None, :], lk2[None, :],
                          norm_b_g[None, :], B, S, lambda_init)

    w_r = jnp.zeros((D, LANES), F32).at[:, :N_GROUPS].set(w_grp).at[:, SUBLANES:SUBLANES + N_EXPERTS].set(w_exp)
    b_r = jnp.zeros((LANES,), F32).at[:N_GROUPS].set(b_grp).at[SUBLANES:SUBLANES + N_EXPERTS].set(b_exp)[None, :]
    x1, x1_packed, route_i, route_w, counts = _mix(h_a, o_b, z_main, x2d, w_a.astype(BF16), w_b.astype(BF16),
                                                   w_out.astype(BF16), ln1_g[None, :], ln1_b[None, :], w_r, b_r)

    tb = MOE_TB
    n_blocks = (T * TOP_K) // tb + N_EXPERTS
    x_pad, pstart, bexp, nused = _dispatch(counts[:, 0], route_i, x1_packed, tb, n_blocks)
    y_pad = _experts(bexp, nused, x_pad, w_gate, w_up, w_down, tb)
    out = _combine(pstart, route_i, x1, route_w.T, ln2_g[None, :], ln2_b[None, :], y_pad)
    return out.reshape(B, S, D)


def kernel(x, w_in, b_in, conv_w, conv_b, mlstm_norm_g, lambda_q1, lambda_k1, lambda_q2, lambda_k2,
           diff_norm_g, w_a, w_b, w_out, ln1_g, ln1_b, w_grp, b_grp, w_exp, b_exp,
           w_gate, w_up, w_down, ln2_g, ln2_b):
    for l in range(DEPTH):
        lambda_init = 0.8 - 0.6 * math.exp(-0.3 * l)
        x = _layer(x, w_in[l], b_in[l], conv_w[l], conv_b[l], mlstm_norm_g[l], lambda_q1[l], lambda_k1[l],
                   lambda_q2[l], lambda_k2[l], diff_norm_g[l], w_a[l], w_b[l], w_out[l], ln1_g[l], ln1_b[l],
                   w_grp[l], b_grp[l], w_exp[l], b_exp[l], w_gate[l], w_up[l], w_down[l], ln2_g[l], ln2_b[l],
                   lambda_init)
    return x
```

```python
import functools
import math

import jax
import jax.numpy as jnp
from jax import lax
from jax.experimental import pallas as pl
from jax.experimental.pallas import tpu as pltpu

F32 = jnp.float32
BF16 = jnp.bfloat16
I32 = jnp.int32

D_MODEL = 2048
A_HEADS = 4
A_HEAD_DIM = 256
A_WIDTH = A_HEADS * A_HEAD_DIM
CONV_WIDTH = 4
B_HEADS = 8
B_HEAD_DIM = 64
B_V_DIM = 2 * B_HEAD_DIM
B_WIDTH = B_HEADS * B_V_DIM
ATTN_CHUNK = 64
N_GROUPS = 4
EXPERTS_PER_GROUP = 8
N_EXPERTS = N_GROUPS * EXPERTS_PER_GROUP
TOP_K = 2
D_EXPERT = 512
DEPTH = 1
DEEPNORM_ALPHA = (2 * DEPTH) ** 0.25
LN_EPS = 1e-5
NEG_BIG = -1e30
LOG2_E = 1.4426950408889634

LANES = 128
SUBLANES = 8
VMEM_LIMIT_BYTES = 56 * 1024 * 1024

COL_GA = 0
COL_GB = COL_GA + D_MODEL
COL_QA = COL_GB + D_MODEL
COL_KA = COL_QA + A_WIDTH
COL_VA = COL_KA + A_WIDTH
COL_OA = COL_VA + A_WIDTH
COL_QB = COL_OA + A_WIDTH
COL_KB = COL_QB + B_WIDTH
COL_VB = COL_KB + B_WIDTH
N_MAIN = COL_VB + B_WIDTH

PROJ_TM, PROJ_TN = 1024, 1024
MLSTM_CHUNK = 256
ATTN_TQ = 512
MIX_TM = 256
MOE_TB = 256
DISPATCH_TM = 256
COMBINE_TM = 256


def _cparams(sem, vmem=VMEM_LIMIT_BYTES):
    return pltpu.CompilerParams(dimension_semantics=sem, vmem_limit_bytes=vmem)


def _sigmoid(x):
    return 1.0 / (1.0 + jnp.exp(-x))


U32 = jnp.uint32


def _pack_halves(x):
    n = x.shape[1] // 2
    lo = lax.bitcast_convert_type(x[:, :n].astype(BF16).astype(F32), U32)
    hi = lax.bitcast_convert_type(x[:, n:].astype(BF16).astype(F32), U32)
    return (lo >> 16) | hi


def _unpack_halves(u):
    lo = lax.bitcast_convert_type(u << 16, F32)
    hi = lax.bitcast_convert_type(u & jnp.uint32(0xFFFF0000), F32)
    return lo, hi


TILE_WORDS = SUBLANES * LANES


def _rows_to_tiles(ref, v):
    rows = v.shape[0]
    for j in range(SUBLANES):
        ref[pl.ds(j, rows, stride=SUBLANES), :] = v[:, j * LANES:(j + 1) * LANES]


def _tiles_to_rows(ref):
    rows = ref.shape[0] // SUBLANES
    return jnp.concatenate([ref[pl.ds(j, rows, stride=SUBLANES), :] for j in range(SUBLANES)], axis=1)


def _split_bf16(a):
    hi = a.astype(BF16)
    return hi, (a - hi.astype(F32)).astype(BF16)


def _dot_3pass(a, b):
    a_hi, a_lo = _split_bf16(a)
    b_hi, b_lo = _split_bf16(b)
    dot = functools.partial(jnp.dot, preferred_element_type=F32)
    return dot(a_hi, b_hi) + (dot(a_hi, b_lo) + dot(a_lo, b_hi))


def _proj_kernel(x_ref, w_ref, b_ref, wg_ref, bg_ref, z_ref, zg_ref, xb_ref):
    @pl.when(pl.program_id(1) == 0)
    def _():
        x = x_ref[...]
        xb_ref[...] = x.astype(BF16)
        zg_ref[...] = _dot_3pass(x, wg_ref[...]) + bg_ref[...]

    acc = jnp.dot(xb_ref[...], w_ref[...], preferred_element_type=F32)
    z_ref[...] = (acc + b_ref[...]).astype(z_ref.dtype)


def _in_projection(x2d, w_main, b_main, w_gate, b_gate):
    T, K = x2d.shape
    N = w_main.shape[1]
    tm, tn = min(PROJ_TM, T), PROJ_TN
    return pl.pallas_call(
        _proj_kernel,
        grid=(T // tm, N // tn),
        in_specs=[
            pl.BlockSpec((tm, K), lambda i, j: (i, 0)),
            pl.BlockSpec((K, tn), lambda i, j: (0, j)),
            pl.BlockSpec((1, tn), lambda i, j: (0, j)),
            pl.BlockSpec((K, LANES), lambda i, j: (0, 0)),
            pl.BlockSpec((1, LANES), lambda i, j: (0, 0)),
        ],
        out_specs=[
            pl.BlockSpec((tm, tn), lambda i, j: (i, j)),
            pl.BlockSpec((tm, LANES), lambda i, j: (i, 0)),
        ],
        out_shape=[
            jax.ShapeDtypeStruct((T, N), BF16),
            jax.ShapeDtypeStruct((T, LANES), F32),
        ],
        scratch_shapes=[pltpu.VMEM((tm, K), BF16)],
        compiler_params=_cparams(("parallel", "arbitrary")),
        name="in_projection",
    )(x2d, w_main, b_main, w_gate, b_gate)


def _gate_prep_kernel(g_ref, o_ref, *, L):
    r = lax.broadcasted_iota(I32, (L, L), 0)
    c = lax.broadcasted_iota(I32, (L, L), 1)
    tri = (r <= c).astype(F32)
    row = lax.broadcasted_iota(I32, (g_ref.shape[0], L), 0)
    for ch in range(g_ref.shape[1] // L):
        g = g_ref[:, ch * L:(ch + 1) * L]
        lf = jnp.minimum(g, 0.0) - jnp.log(1.0 + jnp.exp(-jnp.abs(g)))
        b = jnp.dot(lf, tri, preferred_element_type=F32, precision=lax.Precision.HIGHEST)
        o_ref[:, ch * L:(ch + 1) * L] = jnp.where(row < A_HEADS, g, b)


def _gate_prep(g_rows, L):
    R, T = g_rows.shape
    per_step = math.gcd(T // L, 8)
    W = L * per_step
    return pl.pallas_call(
        functools.partial(_gate_prep_kernel, L=L),
        grid=(T // W,),
        in_specs=[pl.BlockSpec((R, W), lambda i: (0, i))],
        out_specs=pl.BlockSpec((R, W), lambda i: (0, i)),
        out_shape=jax.ShapeDtypeStruct((R, T), F32),
        compiler_params=_cparams(("parallel",)),
        name="gate_prep",
    )(g_rows)


def _mlstm_kernel(q_ref, k_ref, v_ref, o_ref, gcol_ref, grow_ref, cw_ref, cb_ref, ng_ref, out_ref,
                  c_ref, n_ref, m_ref, qcar_ref, kcar_ref):
    ci = pl.program_id(1)
    L = q_ref.shape[0]
    dh = A_HEAD_DIM

    @pl.when(ci == 0)
    def _():
        c_ref[...] = jnp.zeros_like(c_ref)
        n_ref[...] = jnp.zeros_like(n_ref)
        m_ref[...] = jnp.zeros_like(m_ref)
        qcar_ref[...] = jnp.zeros_like(qcar_ref)
        kcar_ref[...] = jnp.zeros_like(kcar_ref)

    def conv_silu(u_ref, car_ref, lo, wcol):
        u = u_ref[:, lo:lo + dh].astype(F32)
        ext = jnp.concatenate([car_ref[:, lo:lo + dh], u], axis=0)
        w = cw_ref[:, wcol:wcol + dh]
        y = cb_ref[:, wcol:wcol + dh] + w[3:4, :] * u
        for j in range(CONV_WIDTH - 1):
            off = SUBLANES - (CONV_WIDTH - 1) + j
            y = y + w[j:j + 1, :] * ext[off:off + L, :]
        car_ref[:, lo:lo + dh] = u[L - SUBLANES:, :]
        return y * _sigmoid(y)

    r = lax.broadcasted_iota(I32, (L, L), 0)
    c = lax.broadcasted_iota(I32, (L, L), 1)
    causal = r >= c
    gcol = gcol_ref[...]
    grow = grow_ref[...]

    for h in range(A_HEADS):
        lo = h * dh
        q = conv_silu(q_ref, qcar_ref, lo, lo)
        k = conv_silu(k_ref, kcar_ref, lo, A_WIDTH + lo) * (dh ** -0.5)
        v_b = v_ref[:, lo:lo + dh]
        q_b = q.astype(BF16)
        k_b = k.astype(BF16)

        i_col, b_col = gcol[:, h:h + 1], gcol[:, A_HEADS + h:A_HEADS + h + 1]
        i_row, b_row = grow[h:h + 1, :], grow[A_HEADS + h:A_HEADS + h + 1, :]
        b_last = b_row[:, L - 1:L]
        m_prev = m_ref[h]

        dmat = jnp.where(causal, b_col - b_row + i_row, NEG_BIG)
        inter = b_col + m_prev
        m_t = jnp.maximum(inter, jnp.max(dmat, axis=-1, keepdims=True))
        s = lax.dot_general(q_b, k_b, (((1,), (1,)), ((), ())), preferred_element_type=F32)
        w_intra = jnp.exp(dmat - m_t) * s
        w_inter = jnp.exp(inter - m_t)
        qc = jnp.dot(q_b, c_ref[h].astype(BF16), preferred_element_type=F32)
        num = w_inter * qc + jnp.dot(w_intra.astype(BF16), v_b, preferred_element_type=F32)
        qn = jnp.sum(q * n_ref[h], axis=-1, keepdims=True)
        den = w_inter * qn + jnp.sum(w_intra, axis=-1, keepdims=True)
        hh = num / jnp.maximum(jnp.abs(den), jnp.exp(-m_t))

        g_col = b_last - b_col + i_col
        m_new = jnp.maximum(b_last + m_prev, jnp.max(g_col, axis=0, keepdims=True))
        decay = jnp.exp(b_last + m_prev - m_new)
        kw = jnp.exp(g_col - m_new) * k
        c_ref[h] = decay * c_ref[h] + jnp.dot(kw.T.astype(BF16), v_b, preferred_element_type=F32)
        n_ref[h] = decay * n_ref[h] + jnp.sum(kw, axis=0, keepdims=True)
        m_ref[h] = m_new

        y = _sigmoid(o_ref[:, lo:lo + dh].astype(F32)) * hh
        mu = jnp.mean(y, axis=-1, keepdims=True)
        yc = y - mu
        var = jnp.mean(yc * yc, axis=-1, keepdims=True)
        out_ref[:, lo:lo + dh] = (yc * lax.rsqrt(var + LN_EPS) * ng_ref[:, lo:lo + dh]).astype(out_ref.dtype)


def _mlstm(z_main, gcol, grow, conv_w, conv_b, norm_g, B, S):
    L = min(MLSTM_CHUNK, S)
    nc = S // L
    dh = A_HEAD_DIM
    H = A_HEADS
    W = A_WIDTH
    row = lambda b, c: b * nc + c
    full = lambda shape: pl.BlockSpec(shape, lambda b, c: (0, 0))
    return pl.pallas_call(
        _mlstm_kernel,
        grid=(B, nc),
        in_specs=[
            pl.BlockSpec((L, W), lambda b, c: (row(b, c), COL_QA // W)),
            pl.BlockSpec((L, W), lambda b, c: (row(b, c), COL_KA // W)),
            pl.BlockSpec((L, W), lambda b, c: (row(b, c), COL_VA // W)),
            pl.BlockSpec((L, W), lambda b, c: (row(b, c), COL_OA // W)),
            pl.BlockSpec((L, SUBLANES), lambda b, c: (row(b, c), 0)),
            pl.BlockSpec((SUBLANES, L), lambda b, c: (0, row(b, c))),
            full((CONV_WIDTH, 2 * W)), full((1, 2 * W)), full((1, W)),
        ],
        out_specs=pl.BlockSpec((L, W), lambda b, c: (row(b, c), 0)),
        out_shape=jax.ShapeDtypeStruct((B * S, W), BF16),
        scratch_shapes=[
            pltpu.VMEM((H, dh, dh), F32),
            pltpu.VMEM((H, 1, dh), F32),
            pltpu.VMEM((H, 1, 1), F32),
            pltpu.VMEM((SUBLANES, W), F32),
            pltpu.VMEM((SUBLANES, W), F32),
        ],
        compiler_params=_cparams(("parallel", "arbitrary")),
        name="mlstm",
    )(z_main, z_main, z_main, z_main, gcol, grow, conv_w, conv_b, norm_g)


def _lane_tiles(t):
    return [t[:, i * LANES:(i + 1) * LANES] for i in range(t.shape[1] // LANES)]


def _attn_kernel(slope_ref, q_ref, k_ref, kb_ref, v_ref, lq1_ref, lk1_ref, lq2_ref, lk2_ref, g_ref, o_ref,
                 s_ref, mx_ref, ls_ref, acc_ref, corr_ref, *, lambda_init):
    h = pl.program_id(1)
    qi = pl.program_id(2)
    tq = q_ref.shape[0]
    tk = tq
    slope = slope_ref[h]
    nt = (((1,), (1,)), ((), ()))

    lane = lax.broadcasted_iota(I32, (tq, B_V_DIM), 1)
    q = q_ref[...] * (B_HEAD_DIM ** -0.5)
    zero = jnp.zeros_like(q)
    bias_cols = jnp.where(lane < 2, 1.0, 0.0).astype(BF16)
    qa = (jnp.concatenate([jnp.where(lane < B_HEAD_DIM, q, zero), bias_cols], axis=1),
          jnp.concatenate([jnp.where(lane >= B_HEAD_DIM, q, zero), bias_cols], axis=1))

    def scores(j):
        start = pl.multiple_of(j * tk, tk)
        ka = jnp.concatenate([k_ref[pl.ds(start, tk), :], kb_ref[pl.ds(start, tk), :]], axis=1)
        return [lax.dot_general(qa[n], ka, nt, preferred_element_type=F32) for n in range(2)]

    def keep(j, n, t):
        t = t * LOG2_E
        s_ref[n, j] = t
        mx_ref[n] = functools.reduce(jnp.maximum, _lane_tiles(t), mx_ref[n])

    mx_ref[...] = jnp.full(mx_ref.shape, NEG_BIG, F32)

    def pass_a(j):
        t = scores(j)
        for n in range(2):
            keep(j, n, t[n])

    def pass_a_pair(jj, carry):
        pass_a(2 * jj)
        pass_a(2 * jj + 1)
        return carry
    lax.fori_loop(0, qi // 2, pass_a_pair, 0)

    @pl.when(qi % 2 == 1)
    def _():
        pass_a(qi - 1)

    @pl.when(qi == 0)
    def _():
        r = lax.broadcasted_iota(I32, (tq, tk), 0)
        c = lax.broadcasted_iota(I32, (tq, tk), 1)
        ahead = jnp.where(c <= r, 0.0, (r - c).astype(F32) * (2.0 * slope))
        corr_ref[...] = jnp.where((c // ATTN_CHUNK) <= (r // ATTN_CHUNK), ahead, NEG_BIG)

    t = scores(qi)
    for n in range(2):
        keep(qi, n, t[n] + corr_ref[...])

    for n in range(2):
        m = jnp.max(mx_ref[n], axis=-1, keepdims=True)
        mx_ref[n] = jnp.broadcast_to(m, (tq, LANES))
    ls_ref[...] = jnp.zeros_like(ls_ref)
    acc_ref[...] = jnp.zeros_like(acc_ref)

    def pass_b(j, carry):
        start = pl.multiple_of(j * tk, tk)
        vj = v_ref[pl.ds(start, tk), :]
        for n in range(2):
            mb = mx_ref[n]
            ps = [jnp.exp2(t - mb) for t in _lane_tiles(s_ref[n, j])]
            ls_ref[n] += functools.reduce(jnp.add, ps)
            p = jnp.concatenate(ps, axis=1).astype(BF16)
            acc_ref[n] += jnp.dot(p, vj, preferred_element_type=F32)
        return carry
    lax.fori_loop(0, qi + 1, pass_b, 0)

    lam = (jnp.exp(jnp.sum(lq1_ref[...] * lk1_ref[...], axis=-1, keepdims=True))
           - jnp.exp(jnp.sum(lq2_ref[...] * lk2_ref[...], axis=-1, keepdims=True)) + lambda_init)
    l0 = jnp.sum(ls_ref[0], axis=-1, keepdims=True)
    l1 = jnp.sum(ls_ref[1], axis=-1, keepdims=True)
    o = acc_ref[0] / l0 - lam * (acc_ref[1] / l1)
    ms = jnp.mean(o * o, axis=-1, keepdims=True)
    o_ref[...] = (o * lax.rsqrt(ms + LN_EPS) * g_ref[...] * (1.0 - lambda_init)).astype(o_ref.dtype)


def _diff_attention(z_main, lq1, lk1, lq2, lk2, norm_g, B, S, lambda_init):
    tq = min(ATTN_TQ, S)
    nq = S // tq
    H = B_HEADS
    dv = B_V_DIM
    slopes = 2.0 ** (-8.0 * jnp.arange(1, H + 1, dtype=F32) / H)
    assert S <= 256 * 256
    pos = lax.broadcasted_iota(I32, (H, S, dv), 1)
    col = lax.broadcasted_iota(I32, (H, S, dv), 2)
    within = pos % 256
    kbias = jnp.where(col == 0, within, jnp.where(col == 1, pos - within, 0)).astype(F32)
    kbias = (kbias * slopes[:, None, None]).astype(BF16)
    small = pl.BlockSpec((1, B_HEAD_DIM), lambda b, h, i: (0, 0))
    return pl.pallas_call(
        functools.partial(_attn_kernel, lambda_init=lambda_init),
        grid=(B, H, nq),
        in_specs=[
            pl.BlockSpec(memory_space=pltpu.SMEM),
            pl.BlockSpec((tq, dv), lambda b, h, i: (b * nq + i, COL_QB // dv + h)),
            pl.BlockSpec((S, dv), lambda b, h, i: (b, COL_KB // dv + h)),
            pl.BlockSpec((None, S, dv), lambda b, h, i: (h, 0, 0)),
            pl.BlockSpec((S, dv), lambda b, h, i: (b, COL_VB // dv + h)),
            small, small, small, small,
            pl.BlockSpec((1, dv), lambda b, h, i: (0, 0)),
        ],
        out_specs=pl.BlockSpec((tq, dv), lambda b, h, i: (b * nq + i, h)),
        out_shape=jax.ShapeDtypeStruct((B * S, B_WIDTH), BF16),
        scratch_shapes=[
            pltpu.VMEM((2, nq, tq, tq), F32),
            pltpu.VMEM((2, tq, LANES), F32),
            pltpu.VMEM((2, tq, LANES), F32),
            pltpu.VMEM((2, tq, dv), F32),
            pltpu.VMEM((tq, tq), F32),
        ],
        compiler_params=_cparams(("parallel", "parallel", "arbitrary")),
        name="diff_attention",
    )(slopes, z_main, z_main, kbias, z_main, lq1, lk1, lq2, lk2, norm_g)


def _mix_kernel(ha_ref, ob_ref, ga_ref, gb_ref, x_ref, wa_ref, wb_ref, wo_ref, g1_ref, b1_ref, wr_ref, br_ref,
                x1_ref, x1p_ref, ri_ref, rw_ref, cnt_ref, carry_ref):
    i = pl.program_id(0)
    tm = x_ref.shape[0]

    @pl.when(i == 0)
    def _():
        carry_ref[...] = jnp.zeros_like(carry_ref)

    ya = jnp.dot(ha_ref[...], wa_ref[...], preferred_element_type=F32)
    yb = jnp.dot(ob_ref[...], wb_ref[...], preferred_element_type=F32)
    merged = _sigmoid(ga_ref[...].astype(F32)) * ya + _sigmoid(gb_ref[...].astype(F32)) * yb
    mix = jnp.dot(merged.astype(BF16), wo_ref[...], preferred_element_type=F32)
    pre = DEEPNORM_ALPHA * x_ref[...] + mix
    mu = jnp.mean(pre, axis=-1, keepdims=True)
    pc = pre - mu
    var = jnp.mean(pc * pc, axis=-1, keepdims=True)
    x1 = pc * lax.rsqrt(var + LN_EPS) * g1_ref[...] + b1_ref[...]
    x1_ref[...] = x1
    _rows_to_tiles(x1p_ref, _pack_halves(x1))

    logits = _dot_3pass(x1, wr_ref[...]) + br_ref[...]
    lt = logits.T
    row8 = lax.broadcasted_iota(I32, (SUBLANES, tm), 0).astype(F32)
    gl = jnp.where(row8 < N_GROUPS, lt[0:SUBLANES, :], NEG_BIG)
    gmax = jnp.max(gl, axis=0, keepdims=True)
    gsel = jnp.min(jnp.where(gl == gmax, row8, float(SUBLANES)), axis=0, keepdims=True)
    gprob = 1.0 / jnp.sum(jnp.exp(gl - gmax), axis=0, keepdims=True)
    ing = lt[SUBLANES:2 * SUBLANES, :]
    for g in range(1, N_GROUPS):
        ing = jnp.where(gsel == g, lt[(g + 1) * SUBLANES:(g + 2) * SUBLANES, :], ing)
    v0 = jnp.max(ing, axis=0, keepdims=True)
    i0 = jnp.min(jnp.where(ing == v0, row8, float(SUBLANES)), axis=0, keepdims=True)
    ing2 = jnp.where(row8 == i0, -jnp.inf, ing)
    v1 = jnp.max(ing2, axis=0, keepdims=True)
    i1 = jnp.min(jnp.where(ing2 == v1, row8, float(SUBLANES)), axis=0, keepdims=True)
    ex = jnp.exp(v1 - v0)
    inv = 1.0 / (1.0 + ex)
    w0 = gprob * inv
    w1 = gprob * (ex * inv)
    e0 = (gsel * EXPERTS_PER_GROUP + i0).astype(I32)
    e1 = (gsel * EXPERTS_PER_GROUP + i1).astype(I32)

    rowe = lax.broadcasted_iota(I32, (N_EXPERTS, tm), 0)
    is0 = rowe == e0
    is1 = rowe == e1
    oh = jnp.where(is0, 1.0, 0.0) + jnp.where(is1, 1.0, 0.0)
    tr = lax.broadcasted_iota(I32, (tm, tm), 0)
    tc = lax.broadcasted_iota(I32, (tm, tm), 1)
    su = jnp.where(tr < tc, 1.0, 0.0).astype(BF16)
    tot = jnp.dot(oh.astype(BF16), su, preferred_element_type=F32) + carry_ref[...]
    rank0 = jnp.sum(jnp.where(is0, tot, 0.0), axis=0, keepdims=True)
    rank1 = jnp.sum(jnp.where(is1, tot, 0.0), axis=0, keepdims=True)
    carry = carry_ref[...] + jnp.sum(oh, axis=1, keepdims=True)
    carry_ref[...] = carry
    cnt_ref[...] = jnp.broadcast_to(carry, cnt_ref.shape).astype(I32)

    zi = jnp.zeros((1, tm), I32)
    ri_ref[...] = jnp.concatenate([e0, e1, rank0.astype(I32), rank1.astype(I32), zi, zi, zi, zi], axis=0)
    zf = jnp.zeros((1, tm), F32)
    rw_ref[...] = jnp.concatenate([w0, w1, zf, zf, zf, zf, zf, zf], axis=0)


def _mix(h_a, o_b, z_main, x2d, w_a, w_b, w_out, ln_g, ln_b, w_r, b_r):
    T, D = x2d.shape
    tm = min(MIX_TM, T)
    const = lambda shape: pl.BlockSpec(shape, lambda i: (0, 0), pipeline_mode=pl.Buffered(1))
    return pl.pallas_call(
        _mix_kernel,
        grid=(T // tm,),
        in_specs=[
            pl.BlockSpec((tm, A_WIDTH), lambda i: (i, 0)),
            pl.BlockSpec((tm, B_WIDTH), lambda i: (i, 0)),
            pl.BlockSpec((tm, D), lambda i: (i, COL_GA // D)),
            pl.BlockSpec((tm, D), lambda i: (i, COL_GB // D)),
            pl.BlockSpec((tm, D), lambda i: (i, 0)),
            const((A_WIDTH, D)), const((B_WIDTH, D)), const((D, D)),
            const((1, D)), const((1, D)), const((D, LANES)), const((1, LANES)),
        ],
        out_specs=[
            pl.BlockSpec((tm, D), lambda i: (i, 0)),
            pl.BlockSpec((tm * SUBLANES, LANES), lambda i: (i, 0)),
            pl.BlockSpec((SUBLANES, tm), lambda i: (0, i)),
            pl.BlockSpec((SUBLANES, tm), lambda i: (0, i)),
            pl.BlockSpec((N_EXPERTS, LANES), lambda i: (0, 0)),
        ],
        out_shape=[
            jax.ShapeDtypeStruct((T, D), F32),
            jax.ShapeDtypeStruct((T * SUBLANES, LANES), U32),
            jax.ShapeDtypeStruct((SUBLANES, T), I32),
            jax.ShapeDtypeStruct((SUBLANES, T), F32),
            jax.ShapeDtypeStruct((N_EXPERTS, LANES), I32),
        ],
        scratch_shapes=[pltpu.VMEM((N_EXPERTS, 1), F32)],
        compiler_params=_cparams(("arbitrary",)),
        name="mix_ln1_router",
    )(h_a, o_b, z_main, z_main, x2d, w_a, w_b, w_out, ln_g, ln_b, w_r, b_r)


def _row_copy(src_ref, src_row, dst_ref, dst_row, sem):
    return pltpu.make_async_copy(src_ref.at[pl.ds(src_row * SUBLANES, SUBLANES)],
                                 dst_ref.at[pl.ds(dst_row * SUBLANES, SUBLANES)], sem)


def _rows_wait(src_ref, dst_ref, rows, sem):
    n = rows * SUBLANES
    pltpu.make_async_copy(src_ref.at[pl.ds(0, n)], dst_ref.at[pl.ds(0, n)], sem).wait()


DISPATCH_SLOTS = 3


def _dispatch_kernel(cnt_ref, ri_ref, x1_ref, xpad_ref, pstart_ref, bexp_ref, nused_ref,
                     pst_ref, zero_ref, xbuf_ref, sem, lsem, zsem, *, tb, n_blocks, tm):
    i = pl.program_id(0)
    last = pl.num_programs(0) - 1

    @pl.when(i == 0)
    def _():
        def fill(b, carry):
            bexp_ref[b] = 0
            return carry
        lax.fori_loop(0, n_blocks, fill, 0)

        def per_expert(e, blk):
            pst_ref[e] = blk * tb
            pstart_ref[e] = blk * tb
            nb = (cnt_ref[e] + (tb - 1)) // tb

            def mark(b, carry):
                bexp_ref[blk + b] = e
                return carry
            lax.fori_loop(0, nb, mark, 0)
            return blk + nb
        nused = lax.fori_loop(0, N_EXPERTS, per_expert, 0)
        nused_ref[0] = nused
        pst_ref[N_EXPERTS] = nused

    rows = tm * SUBLANES
    slot = i % DISPATCH_SLOTS

    def tile_load(tile, s):
        src = x1_ref.at[pl.ds(pl.multiple_of(tile * rows, rows), rows)]
        return pltpu.make_async_copy(src, xbuf_ref.at[s], lsem.at[s])

    def wait_rows(s):
        for k in range(TOP_K):
            _rows_wait(xbuf_ref.at[s], xpad_ref, tm, sem.at[s])

    @pl.when(i == 0)
    def _():
        tile_load(0, 0).start()

        @pl.when(last >= 1)
        def _():
            tile_load(1, 1).start()

    tile_load(i, slot).wait()
    for t in range(tm):
        for k in range(TOP_K):
            pos = pst_ref[ri_ref[k, t]] + ri_ref[TOP_K + k, t]
            _row_copy(xbuf_ref.at[slot], t, xpad_ref, pos, sem.at[slot]).start(priority=k)

    prev = (i + DISPATCH_SLOTS - 1) % DISPATCH_SLOTS

    @pl.when(i >= 1)
    def _():
        wait_rows(prev)

    @pl.when(i + 2 <= last)
    def _():
        tile_load(i + 2, prev).start()

    @pl.when(i == last)
    def _():
        wait_rows(slot)
        zero_ref[...] = jnp.zeros_like(zero_ref)

        def per_expert(e, carry):
            cnt = cnt_ref[e]
            first = pst_ref[e] + cnt
            npad = ((cnt + (tb - 1)) // tb) * tb - cnt

            def start(p, c2):
                _row_copy(zero_ref, 0, xpad_ref, first + p, zsem).start()
                return c2
            lax.fori_loop(0, npad, start, 0)

            def wait(p, c2):
                _row_copy(zero_ref, 0, xpad_ref, 0, zsem).wait()
                return c2
            lax.fori_loop(0, npad, wait, 0)
            return carry
        lax.fori_loop(0, N_EXPERTS, per_expert, 0)

        def block_copy(b):
            n = tb * SUBLANES
            return pltpu.make_async_copy(zero_ref, xpad_ref.at[pl.ds(pl.multiple_of(b * n, n), n)], zsem)

        def start_block(b, carry):
            block_copy(b).start()
            return carry
        lax.fori_loop(pst_ref[N_EXPERTS], n_blocks, start_block, 0)

        def wait_block(b, carry):
            block_copy(b).wait()
            return carry
        lax.fori_loop(pst_ref[N_EXPERTS], n_blocks, wait_block, 0)


def _dispatch(counts, route_i, x1, tb, n_blocks):
    T = x1.shape[0] // SUBLANES
    tm = min(DISPATCH_TM, T)
    smem_full = pl.BlockSpec(memory_space=pltpu.SMEM)
    return pl.pallas_call(
        functools.partial(_dispatch_kernel, tb=tb, n_blocks=n_blocks, tm=tm),
        grid=(T // tm,),
        in_specs=[
            smem_full,
            pl.BlockSpec((SUBLANES, tm), lambda i: (0, i), memory_space=pltpu.SMEM),
            pl.BlockSpec(memory_space=pl.ANY),
        ],
        out_specs=[
            pl.BlockSpec(memory_space=pl.ANY),
            smem_full, smem_full, smem_full,
        ],
        out_shape=[
            jax.ShapeDtypeStruct((n_blocks * tb * SUBLANES, LANES), x1.dtype),
            jax.ShapeDtypeStruct((N_EXPERTS,), I32),
            jax.ShapeDtypeStruct((n_blocks,), I32),
            jax.ShapeDtypeStruct((1,), I32),
        ],
        scratch_shapes=[
            pltpu.SMEM((N_EXPERTS + 1,), I32),
            pltpu.VMEM((tb * SUBLANES, LANES), x1.dtype),
            pltpu.VMEM((DISPATCH_SLOTS, tm * SUBLANES, LANES), x1.dtype),
            pltpu.SemaphoreType.DMA((DISPATCH_SLOTS,)),
            pltpu.SemaphoreType.DMA((DISPATCH_SLOTS,)),
            pltpu.SemaphoreType.DMA(()),
        ],
        compiler_params=_cparams(("arbitrary",)),
        name="moe_dispatch",
    )(counts, route_i, x1)


def _expert_kernel(bexp_ref, nused_ref, x_ref, wg_hbm, wu_hbm, wd_hbm, y_ref,
                   wgf_ref, wuf_ref, wdf_ref, wgb_ref, wub_ref, wdb_ref, sem, run_ref):
    b = pl.program_id(0)
    nused = nused_ref[0]
    used = b < nused
    e = bexp_ref[b]
    new_expert = jnp.logical_or(b == 0, e != bexp_ref[jnp.maximum(b - 1, 0)])
    streams = ((wg_hbm, wgf_ref), (wu_hbm, wuf_ref), (wd_hbm, wdf_ref))

    def weight_copies(expert, slot):
        return [pltpu.make_async_copy(w_hbm.at[expert], wf_ref.at[slot], sem.at[slot, j])
                for j, (w_hbm, wf_ref) in enumerate(streams)]

    @pl.when(jnp.logical_and(used, b == 0))
    def _():
        run_ref[0] = 0
        for cp in weight_copies(e, 0):
            cp.start()

    @pl.when(jnp.logical_and(used, new_expert))
    def _():
        slot = run_ref[0] % 2
        nxt = lax.while_loop(lambda j: jnp.logical_and(j < nused, bexp_ref[jnp.minimum(j, nused - 1)] == e),
                             lambda j: j + 1, b + 1)

        @pl.when(nxt < nused)
        def _():
            for cp in weight_copies(bexp_ref[jnp.minimum(nxt, nused - 1)], 1 - slot):
                cp.start()

        for cp in weight_copies(e, slot):
            cp.wait()
        wgb_ref[...] = wgf_ref[slot].astype(BF16)
        wub_ref[...] = wuf_ref[slot].astype(BF16)
        wdb_ref[...] = wdf_ref[slot].astype(BF16)
        run_ref[0] = run_ref[0] + 1

    @pl.when(used)
    def _():
        half = SUBLANES * LANES
        x_lo, x_hi = (t.astype(BF16) for t in _unpack_halves(_tiles_to_rows(x_ref)))
        dot = functools.partial(jnp.dot, preferred_element_type=F32)
        gate = dot(x_lo, wgb_ref[:half, :]) + dot(x_hi, wgb_ref[half:, :])
        up = dot(x_lo, wub_ref[:half, :]) + dot(x_hi, wub_ref[half:, :])
        hid = (gate * _sigmoid(gate) * up).astype(BF16)
        _rows_to_tiles(y_ref, _pack_halves(dot(hid, wdb_ref[...])))

    @pl.when(jnp.logical_not(used))
    def _():
        y_ref[...] = jnp.zeros_like(y_ref)


def _experts(bexp, nused, x_pad, w_gate, w_up, w_down, tb):
    P = x_pad.shape[0] // SUBLANES
    D = 2 * TILE_WORDS
    n_blocks = P // tb
    rows = (tb * SUBLANES, LANES)
    F = w_gate.shape[-1]
    blk = lambda b, be, nu: jnp.maximum(jnp.minimum(b, nu[0] - 1), 0)
    hbm = pl.BlockSpec(memory_space=pl.ANY)
    return pl.pallas_call(
        _expert_kernel,
        grid_spec=pltpu.PrefetchScalarGridSpec(
            num_scalar_prefetch=2,
            grid=(n_blocks,),
            in_specs=[pl.BlockSpec(rows, lambda b, be, nu: (blk(b, be, nu), 0)), hbm, hbm, hbm],
            out_specs=pl.BlockSpec(rows, lambda b, be, nu: (b, 0)),
            scratch_shapes=[
                pltpu.VMEM((2, D, F), F32), pltpu.VMEM((2, D, F), F32), pltpu.VMEM((2, F, D), F32),
                pltpu.VMEM((D, F), BF16), pltpu.VMEM((D, F), BF16), pltpu.VMEM((F, D), BF16),
                pltpu.SemaphoreType.DMA((2, 3)),
                pltpu.SMEM((1,), I32),
            ],
        ),
        out_shape=jax.ShapeDtypeStruct(x_pad.shape, U32),
        compiler_params=_cparams(("arbitrary",)),
        name="moe_experts",
    )(bexp, nused, x_pad, w_gate, w_up, w_down)


COMBINE_SLOTS = 3


def _combine_kernel(pstart_ref, ri_ref, r1_ref, r2_ref, x1_ref, rw_ref, g_ref, b_ref, ypad_ref, o_ref, ybuf_ref, sem):
    i = pl.program_id(0)
    n = pl.num_programs(0)
    tm = x1_ref.shape[0]
    slot = i % COMBINE_SLOTS

    def gather_loop(route_ref, s):
        def issue(t, carry):
            for k in range(TOP_K):
                pos = pstart_ref[route_ref[k, t]] + route_ref[TOP_K + k, t]
                _row_copy(ypad_ref, pos, ybuf_ref.at[s, k], t, sem.at[s]).start(priority=k)
            return carry
        lax.fori_loop(0, tm, issue, 0, unroll=8)

    def wait_slot(s):
        for k in range(TOP_K):
            _rows_wait(ypad_ref, ybuf_ref.at[s, k], tm, sem.at[s])

    @pl.when(i == 0)
    def _():
        gather_loop(ri_ref, 0)
        gather_loop(r1_ref, 1)

    wait_slot(slot)

    rw = rw_ref[...]
    y0 = _unpack_halves(_tiles_to_rows(ybuf_ref.at[slot, 0]))
    y1 = _unpack_halves(_tiles_to_rows(ybuf_ref.at[slot, 1]))
    ffn = jnp.concatenate([y0[h] * rw[:, 0:1] + y1[h] * rw[:, 1:2] for h in range(2)], axis=1)
    pre = DEEPNORM_ALPHA * x1_ref[...] + ffn
    mu = jnp.mean(pre, axis=-1, keepdims=True)
    pc = pre - mu
    var = jnp.mean(pc * pc, axis=-1, keepdims=True)
    o_ref[...] = pc * lax.rsqrt(var + LN_EPS) * g_ref[...] + b_ref[...]

    nxt = (i + 2) % COMBINE_SLOTS
    for t in range(tm):
        for k in range(TOP_K):
            pos = pstart_ref[r2_ref[k, t]] + r2_ref[TOP_K + k, t]
            _row_copy(ypad_ref, pos, ybuf_ref.at[nxt, k], t, sem.at[nxt]).start(priority=k)

    @pl.when(i == n - 1)
    def _():
        wait_slot((i + 1) % COMBINE_SLOTS)
        wait_slot(nxt)


def _combine(pstart, route_i, x1, rw_col, ln_g, ln_b, y_pad):
    T, D = x1.shape
    tm = min(COMBINE_TM, T)
    n = T // tm
    return pl.pallas_call(
        _combine_kernel,
        grid=(n,),
        in_specs=[
            pl.BlockSpec(memory_space=pltpu.SMEM),
            pl.BlockSpec((SUBLANES, tm), lambda i: (0, i), memory_space=pltpu.SMEM),
            pl.BlockSpec((SUBLANES, tm), lambda i: (0, jnp.minimum(i + 1, n - 1)), memory_space=pltpu.SMEM),
            pl.BlockSpec((SUBLANES, tm), lambda i: (0, jnp.minimum(i + 2, n - 1)), memory_space=pltpu.SMEM),
            pl.BlockSpec((tm, D), lambda i: (i, 0)),
            pl.BlockSpec((tm, SUBLANES), lambda i: (i, 0)),
            pl.BlockSpec((1, D), lambda i: (0, 0)),
            pl.BlockSpec((1, D), lambda i: (0, 0)),
            pl.BlockSpec(memory_space=pl.ANY),
        ],
        out_specs=pl.BlockSpec((tm, D), lambda i: (i, 0)),
        out_shape=jax.ShapeDtypeStruct((T, D), F32),
        scratch_shapes=[
            pltpu.VMEM((COMBINE_SLOTS, TOP_K, tm * SUBLANES, LANES), U32),
            pltpu.SemaphoreType.DMA((COMBINE_SLOTS,)),
        ],
        compiler_params=_cparams(("arbitrary",)),
        name="moe_combine_ln2",
    )(pstart, route_i, route_i, route_i, x1, rw_col, ln_g, ln_b, y_pad)


N_GATE_COLS = 2 * A_HEADS
SRC_MLSTM = 0
SRC_GATES = 4 * A_WIDTH
SRC_DIFF = SRC_GATES + N_GATE_COLS
SRC_MERGE = SRC_DIFF + 3 * B_WIDTH
PREP_TN = 512


def _wprep_kernel(a_ref, o_ref):
    o_ref[...] = a_ref[...].T.astype(BF16)


def _wgate_kernel(a_ref, o_ref):
    a = a_ref[...]
    rows = jnp.concatenate([a, jnp.zeros((LANES - a.shape[0], a.shape[1]), F32)], axis=0)
    o_ref[...] = rows.T


def _rearrange_in_proj(w_in, b_in):
    K, n_in = w_in.shape
    tn = PREP_TN
    merge_blocks = (2 * D_MODEL) // tn
    mlstm_blocks = (4 * A_WIDTH) // tn
    assert SRC_MERGE % SUBLANES == 0 and SRC_DIFF % SUBLANES == 0

    def src_row(jb):
        s = SUBLANES
        merge = SRC_MERGE // s + (tn // s) * jb
        mlstm = SRC_MLSTM // s + (tn // s) * (jb - merge_blocks)
        diff = SRC_DIFF // s + (tn // s) * (jb - merge_blocks - mlstm_blocks)
        return s * jnp.where(jb < merge_blocks, merge, jnp.where(jb < merge_blocks + mlstm_blocks, mlstm, diff))

    w_t = jnp.swapaxes(w_in, 0, 1)
    w_main = pl.pallas_call(
        _wprep_kernel,
        grid=(N_MAIN // tn,),
        in_specs=[pl.BlockSpec((pl.Element(tn), pl.Element(K)), lambda jb: (src_row(jb), 0))],
        out_specs=pl.BlockSpec((K, tn), lambda jb: (0, jb)),
        out_shape=jax.ShapeDtypeStruct((K, N_MAIN), BF16),
        compiler_params=_cparams(("parallel",)),
        name="w_in_prep",
    )(w_t)
    b_main = jnp.concatenate([b_in[SRC_MERGE:], b_in[SRC_MLSTM:SRC_GATES], b_in[SRC_DIFF:SRC_MERGE]])[None, :]
    w_gate = pl.pallas_call(
        _wgate_kernel,
        grid=(1,),
        in_specs=[pl.BlockSpec((pl.Element(N_GATE_COLS), pl.Element(K)), lambda i: (SRC_GATES, 0))],
        out_specs=pl.BlockSpec((K, LANES), lambda i: (0, 0)),
        out_shape=jax.ShapeDtypeStruct((K, LANES), F32),
        name="w_gate_prep",
    )(w_t)
    b_gate = jnp.pad(b_in[SRC_GATES:SRC_DIFF], (0, LANES - N_GATE_COLS))[None, :]
    return w_main, b_main, w_gate, b_gate


def _layer(x, w_in, b_in, conv_w, conv_b, norm_a_g, lq1, lk1, lq2, lk2, norm_b_g, w_a, w_b, w_out,
           ln1_g, ln1_b, w_grp, b_grp, w_exp, b_exp, w_gate, w_up, w_down, ln2_g, ln2_b, lambda_init):
    B, S, D = x.shape
    T = B * S
    x2d = x.reshape(T, D)

    w_main, b_main, w_g, b_g = _rearrange_in_proj(w_in, b_in)
    z_main, z_gate = _in_projection(x2d, w_main, b_main, w_g, b_g)

    L = min(MLSTM_CHUNK, S)
    gp = _gate_prep(z_gate[:, :2 * A_HEADS].T, L)
    h_a = _mlstm(z_main, gp.T, gp, conv_w, conv_b[None, :], norm_a_g[None, :], B, S)
    o_b = _diff_attention(z_main, lq1[None, :], lk1[None, :], lq2[None, :], lk2[None, :],
                          norm_b_g[None, :], B, S, lambda_init)

    w_r = jnp.zeros((D, LANES), F32).at[:, :N_GROUPS].set(w_grp).at[:, SUBLANES:SUBLANES + N_EXPERTS].set(w_exp)
    b_r = jnp.zeros((LANES,), F32).at[:N_GROUPS].set(b_grp).at[SUBLANES:SUBLANES + N_EXPERTS].set(b_exp)[None, :]
    x1, x1_packed, route_i, route_w, counts = _mix(h_a, o_b, z_main, x2d, w_a.astype(BF16), w_b.astype(BF16),
                                                   w_out.astype(BF16), ln1_g[None, :], ln1_b[None, :], w_r, b_r)

    tb = MOE_TB
    n_blocks = (T * TOP_K) // tb + N_EXPERTS
    x_pad, pstart, bexp, nused = _dispatch(counts[:, 0], route_i, x1_packed, tb, n_blocks)
    y_pad = _experts(bexp, nused, x_pad, w_gate, w_up, w_down, tb)
    out = _combine(pstart, route_i, x1, route_w.T, ln2_g[None, :], ln2_b[None, :], y_pad)
    return out.reshape(B, S, D)


def kernel(x, w_in, b_in, conv_w, conv_b, mlstm_norm_g, lambda_q1, lambda_k1, lambda_q2, lambda_k2,
           diff_norm_g, w_a, w_b, w_out, ln1_g, ln1_b, w_grp, b_grp, w_exp, b_exp,
           w_gate, w_up, w_down, ln2_g, ln2_b):
    for l in range(DEPTH):
        lambda_init = 0.8 - 0.6 * math.exp(-0.3 * l)
        x = _layer(x, w_in[l], b_in[l], conv_w[l], conv_b[l], mlstm_norm_g[l], lambda_q1[l], lambda_k1[l],
                   lambda_q2[l], lambda_k2[l], diff_norm_g[l], w_a[l], w_b[l], w_out[l], ln1_g[l], ln1_b[l],
                   w_grp[l], b_grp[l], w_exp[l], b_exp[l], w_gate[l], w_up[l], w_down[l], ln2_g[l], ln2_b[l],
                   lambda_init)
    return x
```

```python
import functools
import math

import jax
import jax.numpy as jnp
from jax import lax
from jax.experimental import pallas as pl
from jax.experimental.pallas import tpu as pltpu

F32 = jnp.float32
BF16 = jnp.bfloat16
I32 = jnp.int32

D_MODEL = 2048
A_HEADS = 4
A_HEAD_DIM = 256
A_WIDTH = A_HEADS * A_HEAD_DIM
CONV_WIDTH = 4
B_HEADS = 8
B_HEAD_DIM = 64
B_V_DIM = 2 * B_HEAD_DIM
B_WIDTH = B_HEADS * B_V_DIM
ATTN_CHUNK = 64
N_GROUPS = 4
EXPERTS_PER_GROUP = 8
N_EXPERTS = N_GROUPS * EXPERTS_PER_GROUP
TOP_K = 2
D_EXPERT = 512
DEPTH = 1
DEEPNORM_ALPHA = (2 * DEPTH) ** 0.25
LN_EPS = 1e-5
NEG_BIG = -1e30
LOG2_E = 1.4426950408889634

LANES = 128
SUBLANES = 8
VMEM_LIMIT_BYTES = 56 * 1024 * 1024

COL_GA = 0
COL_GB = COL_GA + D_MODEL
COL_QA = COL_GB + D_MODEL
COL_KA = COL_QA + A_WIDTH
COL_VA = COL_KA + A_WIDTH
COL_OA = COL_VA + A_WIDTH
COL_QB = COL_OA + A_WIDTH
COL_KB = COL_QB + B_WIDTH
COL_VB = COL_KB + B_WIDTH
N_MAIN = COL_VB + B_WIDTH

PROJ_TM, PROJ_TN = 1024, 1024
MLSTM_CHUNK = 256
ATTN_TQ = 512
MIX_TM = 256
MOE_TB = 256
DISPATCH_TM = 256
COMBINE_TM = 256


def _cparams(sem, vmem=VMEM_LIMIT_BYTES):
    return pltpu.CompilerParams(dimension_semantics=sem, vmem_limit_bytes=vmem)


def _sigmoid(x):
    return 1.0 / (1.0 + jnp.exp(-x))


U32 = jnp.uint32


def _pack_halves(x):
    n = x.shape[1] // 2
    lo = lax.bitcast_convert_type(x[:, :n].astype(BF16).astype(F32), U32)
    hi = lax.bitcast_convert_type(x[:, n:].astype(BF16).astype(F32), U32)
    return (lo >> 16) | hi


def _unpack_halves(u):
    lo = lax.bitcast_convert_type(u << 16, F32)
    hi = lax.bitcast_convert_type(u & jnp.uint32(0xFFFF0000), F32)
    return lo, hi


TILE_WORDS = SUBLANES * LANES


def _rows_to_tiles(ref, v):
    rows = v.shape[0]
    for j in range(SUBLANES):
        ref[pl.ds(j, rows, stride=SUBLANES), :] = v[:, j * LANES:(j + 1) * LANES]


def _tiles_to_rows(ref):
    rows = ref.shape[0] // SUBLANES
    return jnp.concatenate([ref[pl.ds(j, rows, stride=SUBLANES), :] for j in range(SUBLANES)], axis=1)


def _split_bf16(a):
    hi = a.astype(BF16)
    return hi, (a - hi.astype(F32)).astype(BF16)


def _hi_lo_columns(w):
    return jnp.concatenate(_split_bf16(w), axis=1)


def _dot_3pass(a, w_hl):
    a_hi, a_lo = _split_bf16(a)
    n = w_hl.shape[1] // 2
    dot = functools.partial(jnp.dot, preferred_element_type=F32)
    r = dot(a_hi, w_hl)
    return r[:, :n] + (r[:, n:] + dot(a_lo, w_hl[:, :n]))


def _proj_kernel(x_ref, w_ref, b_ref, wg_ref, bg_ref, z_ref, zg_ref, xb_ref):
    @pl.when(pl.program_id(1) == 0)
    def _():
        x = x_ref[...]
        xb_ref[...] = x.astype(BF16)
        zg_ref[...] = _dot_3pass(x, wg_ref[...]) + bg_ref[...]

    acc = jnp.dot(xb_ref[...], w_ref[...], preferred_element_type=F32)
    z_ref[...] = (acc + b_ref[...]).astype(z_ref.dtype)


def _in_projection(x2d, w_main, b_main, w_gate, b_gate):
    T, K = x2d.shape
    N = w_main.shape[1]
    tm, tn = min(PROJ_TM, T), PROJ_TN
    return pl.pallas_call(
        _proj_kernel,
        grid=(T // tm, N // tn),
        in_specs=[
            pl.BlockSpec((tm, K), lambda i, j: (i, 0)),
            pl.BlockSpec((K, tn), lambda i, j: (0, j)),
            pl.BlockSpec((1, tn), lambda i, j: (0, j)),
            pl.BlockSpec((K, 2 * LANES), lambda i, j: (0, 0)),
            pl.BlockSpec((1, LANES), lambda i, j: (0, 0)),
        ],
        out_specs=[
            pl.BlockSpec((tm, tn), lambda i, j: (i, j)),
            pl.BlockSpec((tm, LANES), lambda i, j: (i, 0)),
        ],
        out_shape=[
            jax.ShapeDtypeStruct((T, N), BF16),
            jax.ShapeDtypeStruct((T, LANES), F32),
        ],
        scratch_shapes=[pltpu.VMEM((tm, K), BF16)],
        compiler_params=_cparams(("parallel", "arbitrary")),
        name="in_projection",
    )(x2d, w_main, b_main, w_gate, b_gate)


def _gate_prep_kernel(g_ref, o_ref, *, L):
    r = lax.broadcasted_iota(I32, (L, L), 0)
    c = lax.broadcasted_iota(I32, (L, L), 1)
    tri = (r <= c).astype(F32)
    row = lax.broadcasted_iota(I32, (g_ref.shape[0], L), 0)
    for ch in range(g_ref.shape[1] // L):
        g = g_ref[:, ch * L:(ch + 1) * L]
        lf = jnp.minimum(g, 0.0) - jnp.log(1.0 + jnp.exp(-jnp.abs(g)))
        b = jnp.dot(lf, tri, preferred_element_type=F32, precision=lax.Precision.HIGHEST)
        o_ref[:, ch * L:(ch + 1) * L] = jnp.where(row < A_HEADS, g, b)


def _gate_prep(g_rows, L):
    R, T = g_rows.shape
    per_step = math.gcd(T // L, 8)
    W = L * per_step
    return pl.pallas_call(
        functools.partial(_gate_prep_kernel, L=L),
        grid=(T // W,),
        in_specs=[pl.BlockSpec((R, W), lambda i: (0, i))],
        out_specs=pl.BlockSpec((R, W), lambda i: (0, i)),
        out_shape=jax.ShapeDtypeStruct((R, T), F32),
        compiler_params=_cparams(("parallel",)),
        name="gate_prep",
    )(g_rows)


def _mlstm_kernel(q_ref, k_ref, v_ref, o_ref, gcol_ref, grow_ref, cw_ref, cb_ref, ng_ref, out_ref,
                  c_ref, n_ref, m_ref, qcar_ref, kcar_ref):
    ci = pl.program_id(1)
    L = q_ref.shape[0]
    dh = A_HEAD_DIM

    @pl.when(ci == 0)
    def _():
        c_ref[...] = jnp.zeros_like(c_ref)
        n_ref[...] = jnp.zeros_like(n_ref)
        m_ref[...] = jnp.zeros_like(m_ref)
        qcar_ref[...] = jnp.zeros_like(qcar_ref)
        kcar_ref[...] = jnp.zeros_like(kcar_ref)

    def conv_silu(u_ref, car_ref, lo, wcol):
        u = u_ref[:, lo:lo + dh].astype(F32)
        ext = jnp.concatenate([car_ref[:, lo:lo + dh], u], axis=0)
        w = cw_ref[:, wcol:wcol + dh]
        y = cb_ref[:, wcol:wcol + dh] + w[3:4, :] * u
        for j in range(CONV_WIDTH - 1):
            off = SUBLANES - (CONV_WIDTH - 1) + j
            y = y + w[j:j + 1, :] * ext[off:off + L, :]
        car_ref[:, lo:lo + dh] = u[L - SUBLANES:, :]
        return y * _sigmoid(y)

    r = lax.broadcasted_iota(I32, (L, L), 0)
    c = lax.broadcasted_iota(I32, (L, L), 1)
    causal = r >= c
    gcol = gcol_ref[...]
    grow = grow_ref[...]

    for h in range(A_HEADS):
        lo = h * dh
        q = conv_silu(q_ref, qcar_ref, lo, lo)
        k = conv_silu(k_ref, kcar_ref, lo, A_WIDTH + lo) * (dh ** -0.5)
        v_b = v_ref[:, lo:lo + dh]
        q_b = q.astype(BF16)
        k_b = k.astype(BF16)

        i_col, b_col = gcol[:, h:h + 1], gcol[:, A_HEADS + h:A_HEADS + h + 1]
        i_row, b_row = grow[h:h + 1, :], grow[A_HEADS + h:A_HEADS + h + 1, :]
        b_last = b_row[:, L - 1:L]
        m_prev = m_ref[h]

        dmat = jnp.where(causal, b_col - b_row + i_row, NEG_BIG)
        inter = b_col + m_prev
        m_t = jnp.maximum(inter, jnp.max(dmat, axis=-1, keepdims=True))
        s = lax.dot_general(q_b, k_b, (((1,), (1,)), ((), ())), preferred_element_type=F32)
        w_intra = jnp.exp(dmat - m_t) * s
        w_inter = jnp.exp(inter - m_t)
        qc = jnp.dot(q_b, c_ref[h].astype(BF16), preferred_element_type=F32)
        num = w_inter * qc + jnp.dot(w_intra.astype(BF16), v_b, preferred_element_type=F32)
        qn = jnp.sum(q * n_ref[h], axis=-1, keepdims=True)
        den = w_inter * qn + jnp.sum(w_intra, axis=-1, keepdims=True)
        hh = num / jnp.maximum(jnp.abs(den), jnp.exp(-m_t))

        g_col = b_last - b_col + i_col
        m_new = jnp.maximum(b_last + m_prev, jnp.max(g_col, axis=0, keepdims=True))
        decay = jnp.exp(b_last + m_prev - m_new)
        kw = jnp.exp(g_col - m_new) * k
        c_ref[h] = decay * c_ref[h] + jnp.dot(kw.T.astype(BF16), v_b, preferred_element_type=F32)
        n_ref[h] = decay * n_ref[h] + jnp.sum(kw, axis=0, keepdims=True)
        m_ref[h] = m_new

        y = _sigmoid(o_ref[:, lo:lo + dh].astype(F32)) * hh
        mu = jnp.mean(y, axis=-1, keepdims=True)
        yc = y - mu
        var = jnp.mean(yc * yc, axis=-1, keepdims=True)
        out_ref[:, lo:lo + dh] = (yc * lax.rsqrt(var + LN_EPS) * ng_ref[:, lo:lo + dh]).astype(out_ref.dtype)


def _mlstm(z_main, gcol, grow, conv_w, conv_b, norm_g, B, S):
    L = min(MLSTM_CHUNK, S)
    nc = S // L
    dh = A_HEAD_DIM
    H = A_HEADS
    W = A_WIDTH
    row = lambda b, c: b * nc + c
    full = lambda shape: pl.BlockSpec(shape, lambda b, c: (0, 0))
    return pl.pallas_call(
        _mlstm_kernel,
        grid=(B, nc),
        in_specs=[
            pl.BlockSpec((L, W), lambda b, c: (row(b, c), COL_QA // W)),
            pl.BlockSpec((L, W), lambda b, c: (row(b, c), COL_KA // W)),
            pl.BlockSpec((L, W), lambda b, c: (row(b, c), COL_VA // W)),
            pl.BlockSpec((L, W), lambda b, c: (row(b, c), COL_OA // W)),
            pl.BlockSpec((L, SUBLANES), lambda b, c: (row(b, c), 0)),
            pl.BlockSpec((SUBLANES, L), lambda b, c: (0, row(b, c))),
            full((CONV_WIDTH, 2 * W)), full((1, 2 * W)), full((1, W)),
        ],
        out_specs=pl.BlockSpec((L, W), lambda b, c: (row(b, c), 0)),
        out_shape=jax.ShapeDtypeStruct((B * S, W), BF16),
        scratch_shapes=[
            pltpu.VMEM((H, dh, dh), F32),
            pltpu.VMEM((H, 1, dh), F32),
            pltpu.VMEM((H, 1, 1), F32),
            pltpu.VMEM((SUBLANES, W), F32),
            pltpu.VMEM((SUBLANES, W), F32),
        ],
        compiler_params=_cparams(("parallel", "arbitrary")),
        name="mlstm",
    )(z_main, z_main, z_main, z_main, gcol, grow, conv_w, conv_b, norm_g)


def _lane_tiles(t):
    return [t[:, i * LANES:(i + 1) * LANES] for i in range(t.shape[1] // LANES)]


def _attn_kernel(slope_ref, q_ref, k_ref, kb_ref, v_ref, lq1_ref, lk1_ref, lq2_ref, lk2_ref, g_ref, o_ref,
                 s_ref, mx_ref, ls_ref, acc_ref, corr_ref, *, lambda_init):
    h = pl.program_id(1)
    qi = pl.program_id(2)
    tq = q_ref.shape[0]
    tk = tq
    slope = slope_ref[h]
    nt = (((1,), (1,)), ((), ()))

    lane = lax.broadcasted_iota(I32, (tq, B_V_DIM), 1)
    q = q_ref[...] * (B_HEAD_DIM ** -0.5)
    zero = jnp.zeros_like(q)
    bias_cols = jnp.where(lane < 2, 1.0, 0.0).astype(BF16)
    qa = (jnp.concatenate([jnp.where(lane < B_HEAD_DIM, q, zero), bias_cols], axis=1),
          jnp.concatenate([jnp.where(lane >= B_HEAD_DIM, q, zero), bias_cols], axis=1))

    def scores(j):
        start = pl.multiple_of(j * tk, tk)
        ka = jnp.concatenate([k_ref[pl.ds(start, tk), :], kb_ref[pl.ds(start, tk), :]], axis=1)
        return [lax.dot_general(qa[n], ka, nt, preferred_element_type=F32) for n in range(2)]

    def keep(j, n, t):
        t = t * LOG2_E
        s_ref[n, j] = t
        mx_ref[n] = functools.reduce(jnp.maximum, _lane_tiles(t), mx_ref[n])

    mx_ref[...] = jnp.full(mx_ref.shape, NEG_BIG, F32)

    def pass_a(j):
        t = scores(j)
        for n in range(2):
            keep(j, n, t[n])

    def pass_a_pair(jj, carry):
        pass_a(2 * jj)
        pass_a(2 * jj + 1)
        return carry
    lax.fori_loop(0, qi // 2, pass_a_pair, 0)

    @pl.when(qi % 2 == 1)
    def _():
        pass_a(qi - 1)

    @pl.when(qi == 0)
    def _():
        r = lax.broadcasted_iota(I32, (tq, tk), 0)
        c = lax.broadcasted_iota(I32, (tq, tk), 1)
        ahead = jnp.where(c <= r, 0.0, (r - c).astype(F32) * (2.0 * slope))
        corr_ref[...] = jnp.where((c // ATTN_CHUNK) <= (r // ATTN_CHUNK), ahead, NEG_BIG)

    hq = tq // 2
    dstart = pl.multiple_of(qi * tk, tk)
    ka = jnp.concatenate([k_ref[pl.ds(dstart, tk), :], kb_ref[pl.ds(dstart, tk), :]], axis=1)
    masked = jnp.full((hq, tk - hq), NEG_BIG, F32)
    for n in range(2):
        top = lax.dot_general(qa[n][:hq], ka[:hq], nt, preferred_element_type=F32)
        bot = lax.dot_general(qa[n][hq:], ka, nt, preferred_element_type=F32)
        t = jnp.concatenate([jnp.concatenate([top, masked], axis=1), bot], axis=0)
        keep(qi, n, t + corr_ref[...])

    for n in range(2):
        m = jnp.max(mx_ref[n], axis=-1, keepdims=True)
        mx_ref[n] = jnp.broadcast_to(m, (tq, LANES))
    ls_ref[...] = jnp.zeros_like(ls_ref)
    acc_ref[...] = jnp.zeros_like(acc_ref)

    def weigh(n, rows, s_tile, v_tile):
        mb = mx_ref[n, rows, :]
        ps = [jnp.exp2(t - mb) for t in _lane_tiles(s_tile)]
        ls_ref[n, rows, :] += functools.reduce(jnp.add, ps)
        p = jnp.concatenate(ps, axis=1).astype(BF16)
        acc_ref[n, rows, :] += jnp.dot(p, v_tile, preferred_element_type=F32)

    def pass_b(j, carry):
        start = pl.multiple_of(j * tk, tk)
        vj = v_ref[pl.ds(start, tk), :]
        for n in range(2):
            weigh(n, slice(None), s_ref[n, j], vj)
        return carry
    lax.fori_loop(0, qi, pass_b, 0)

    for n in range(2):
        weigh(n, slice(0, hq), s_ref[n, qi, :hq, :hq], v_ref[pl.ds(dstart, hq), :])
        weigh(n, slice(hq, tq), s_ref[n, qi, hq:, :], v_ref[pl.ds(dstart, tk), :])

    lam = (jnp.exp(jnp.sum(lq1_ref[...] * lk1_ref[...], axis=-1, keepdims=True))
           - jnp.exp(jnp.sum(lq2_ref[...] * lk2_ref[...], axis=-1, keepdims=True)) + lambda_init)
    l0 = jnp.sum(ls_ref[0], axis=-1, keepdims=True)
    l1 = jnp.sum(ls_ref[1], axis=-1, keepdims=True)
    o = acc_ref[0] / l0 - lam * (acc_ref[1] / l1)
    ms = jnp.mean(o * o, axis=-1, keepdims=True)
    o_ref[...] = (o * lax.rsqrt(ms + LN_EPS) * g_ref[...] * (1.0 - lambda_init)).astype(o_ref.dtype)


def _diff_attention(z_main, lq1, lk1, lq2, lk2, norm_g, B, S, lambda_init):
    tq = min(ATTN_TQ, S)
    nq = S // tq
    H = B_HEADS
    dv = B_V_DIM
    slopes = 2.0 ** (-8.0 * jnp.arange(1, H + 1, dtype=F32) / H)
    assert S <= 256 * 256
    pos = lax.broadcasted_iota(I32, (H, S, dv), 1)
    col = lax.broadcasted_iota(I32, (H, S, dv), 2)
    within = pos % 256
    kbias = jnp.where(col == 0, within, jnp.where(col == 1, pos - within, 0)).astype(F32)
    kbias = (kbias * slopes[:, None, None]).astype(BF16)
    small = pl.BlockSpec((1, B_HEAD_DIM), lambda b, h, i: (0, 0))
    return pl.pallas_call(
        functools.partial(_attn_kernel, lambda_init=lambda_init),
        grid=(B, H, nq),
        in_specs=[
            pl.BlockSpec(memory_space=pltpu.SMEM),
            pl.BlockSpec((tq, dv), lambda b, h, i: (b * nq + i, COL_QB // dv + h)),
            pl.BlockSpec((S, dv), lambda b, h, i: (b, COL_KB // dv + h)),
            pl.BlockSpec((None, S, dv), lambda b, h, i: (h, 0, 0)),
            pl.BlockSpec((S, dv), lambda b, h, i: (b, COL_VB // dv + h)),
            small, small, small, small,
            pl.BlockSpec((1, dv), lambda b, h, i: (0, 0)),
        ],
        out_specs=pl.BlockSpec((tq, dv), lambda b, h, i: (b * nq + i, h)),
        out_shape=jax.ShapeDtypeStruct((B * S, B_WIDTH), BF16),
        scratch_shapes=[
            pltpu.VMEM((2, nq, tq, tq), F32),
            pltpu.VMEM((2, tq, LANES), F32),
            pltpu.VMEM((2, tq, LANES), F32),
            pltpu.VMEM((2, tq, dv), F32),
            pltpu.VMEM((tq, tq), F32),
        ],
        compiler_params=_cparams(("parallel", "parallel", "arbitrary")),
        name="diff_attention",
    )(slopes, z_main, z_main, kbias, z_main, lq1, lk1, lq2, lk2, norm_g)


def _mix_kernel(ha_ref, ob_ref, ga_ref, gb_ref, x_ref, wa_ref, wb_ref, wo_ref, g1_ref, b1_ref, wr_ref, br_ref,
                x1_ref, x1p_ref, ri_ref, rw_ref, cnt_ref, carry_ref):
    i = pl.program_id(0)
    tm = x_ref.shape[0]

    @pl.when(i == 0)
    def _():
        carry_ref[...] = jnp.zeros_like(carry_ref)

    ya = jnp.dot(ha_ref[...], wa_ref[...], preferred_element_type=F32)
    yb = jnp.dot(ob_ref[...], wb_ref[...], preferred_element_type=F32)
    merged = _sigmoid(ga_ref[...].astype(F32)) * ya + _sigmoid(gb_ref[...].astype(F32)) * yb
    mix = jnp.dot(merged.astype(BF16), wo_ref[...], preferred_element_type=F32)
    pre = DEEPNORM_ALPHA * x_ref[...] + mix
    mu = jnp.mean(pre, axis=-1, keepdims=True)
    pc = pre - mu
    var = jnp.mean(pc * pc, axis=-1, keepdims=True)
    x1 = pc * lax.rsqrt(var + LN_EPS) * g1_ref[...] + b1_ref[...]
    x1_ref[...] = x1
    _rows_to_tiles(x1p_ref, _pack_halves(x1))

    logits = _dot_3pass(x1, wr_ref[...]) + br_ref[...]
    lt = logits.T
    row8 = lax.broadcasted_iota(I32, (SUBLANES, tm), 0).astype(F32)
    gl = jnp.where(row8 < N_GROUPS, lt[0:SUBLANES, :], NEG_BIG)
    gmax = jnp.max(gl, axis=0, keepdims=True)
    gsel = jnp.min(jnp.where(gl == gmax, row8, float(SUBLANES)), axis=0, keepdims=True)
    gprob = 1.0 / jnp.sum(jnp.exp(gl - gmax), axis=0, keepdims=True)
    ing = lt[SUBLANES:2 * SUBLANES, :]
    for g in range(1, N_GROUPS):
        ing = jnp.where(gsel == g, lt[(g + 1) * SUBLANES:(g + 2) * SUBLANES, :], ing)
    v0 = jnp.max(ing, axis=0, keepdims=True)
    i0 = jnp.min(jnp.where(ing == v0, row8, float(SUBLANES)), axis=0, keepdims=True)
    ing2 = jnp.where(row8 == i0, -jnp.inf, ing)
    v1 = jnp.max(ing2, axis=0, keepdims=True)
    i1 = jnp.min(jnp.where(ing2 == v1, row8, float(SUBLANES)), axis=0, keepdims=True)
    ex = jnp.exp(v1 - v0)
    inv = 1.0 / (1.0 + ex)
    w0 = gprob * inv
    w1 = gprob * (ex * inv)
    e0 = (gsel * EXPERTS_PER_GROUP + i0).astype(I32)
    e1 = (gsel * EXPERTS_PER_GROUP + i1).astype(I32)

    rowe = lax.broadcasted_iota(I32, (N_EXPERTS, tm), 0)
    is0 = rowe == e0
    is1 = rowe == e1
    oh = jnp.where(is0, 1.0, 0.0) + jnp.where(is1, 1.0, 0.0)
    tr = lax.broadcasted_iota(I32, (tm, tm), 0)
    tc = lax.broadcasted_iota(I32, (tm, tm), 1)
    su = jnp.where(tr < tc, 1.0, 0.0).astype(BF16)
    tot = jnp.dot(oh.astype(BF16), su, preferred_element_type=F32) + carry_ref[...]
    rank0 = jnp.sum(jnp.where(is0, tot, 0.0), axis=0, keepdims=True)
    rank1 = jnp.sum(jnp.where(is1, tot, 0.0), axis=0, keepdims=True)
    carry = carry_ref[...] + jnp.sum(oh, axis=1, keepdims=True)
    carry_ref[...] = carry
    cnt_ref[...] = jnp.broadcast_to(carry, cnt_ref.shape).astype(I32)

    zi = jnp.zeros((1, tm), I32)
    ri_ref[...] = jnp.concatenate([e0, e1, rank0.astype(I32), rank1.astype(I32), zi, zi, zi, zi], axis=0)
    zf = jnp.zeros((1, tm), F32)
    rw_ref[...] = jnp.concatenate([w0, w1, zf, zf, zf, zf, zf, zf], axis=0)


def _mix(h_a, o_b, z_main, x2d, w_a, w_b, w_out, ln_g, ln_b, w_r, b_r):
    T, D = x2d.shape
    tm = min(MIX_TM, T)
    const = lambda shape: pl.BlockSpec(shape, lambda i: (0, 0), pipeline_mode=pl.Buffered(1))
    return pl.pallas_call(
        _mix_kernel,
        grid=(T // tm,),
        in_specs=[
            pl.BlockSpec((tm, A_WIDTH), lambda i: (i, 0)),
            pl.BlockSpec((tm, B_WIDTH), lambda i: (i, 0)),
            pl.BlockSpec((tm, D), lambda i: (i, COL_GA // D)),
            pl.BlockSpec((tm, D), lambda i: (i, COL_GB // D)),
            pl.BlockSpec((tm, D), lambda i: (i, 0)),
            const((A_WIDTH, D)), const((B_WIDTH, D)), const((D, D)),
            const((1, D)), const((1, D)), const((D, 2 * LANES)), const((1, LANES)),
        ],
        out_specs=[
            pl.BlockSpec((tm, D), lambda i: (i, 0)),
            pl.BlockSpec((tm * SUBLANES, LANES), lambda i: (i, 0)),
            pl.BlockSpec((SUBLANES, tm), lambda i: (0, i)),
            pl.BlockSpec((SUBLANES, tm), lambda i: (0, i)),
            pl.BlockSpec((N_EXPERTS, LANES), lambda i: (0, 0)),
        ],
        out_shape=[
            jax.ShapeDtypeStruct((T, D), F32),
            jax.ShapeDtypeStruct((T * SUBLANES, LANES), U32),
            jax.ShapeDtypeStruct((SUBLANES, T), I32),
            jax.ShapeDtypeStruct((SUBLANES, T), F32),
            jax.ShapeDtypeStruct((N_EXPERTS, LANES), I32),
        ],
        scratch_shapes=[pltpu.VMEM((N_EXPERTS, 1), F32)],
        compiler_params=_cparams(("arbitrary",)),
        name="mix_ln1_router",
    )(h_a, o_b, z_main, z_main, x2d, w_a, w_b, w_out, ln_g, ln_b, w_r, b_r)


def _row_copy(src_ref, src_row, dst_ref, dst_row, sem):
    return pltpu.make_async_copy(src_ref.at[pl.ds(src_row * SUBLANES, SUBLANES)],
                                 dst_ref.at[pl.ds(dst_row * SUBLANES, SUBLANES)], sem)


def _rows_wait(src_ref, dst_ref, rows, sem):
    n = rows * SUBLANES
    pltpu.make_async_copy(src_ref.at[pl.ds(0, n)], dst_ref.at[pl.ds(0, n)], sem).wait()


DISPATCH_SLOTS = 3


def _dispatch_kernel(cnt_ref, ri_ref, x1_ref, xpad_ref, pstart_ref, bexp_ref, nused_ref,
                     pst_ref, zero_ref, xbuf_ref, sem, lsem, zsem, *, tb, n_blocks, tm):
    i = pl.program_id(0)
    last = pl.num_programs(0) - 1

    @pl.when(i == 0)
    def _():
        def fill(b, carry):
            bexp_ref[b] = 0
            return carry
        lax.fori_loop(0, n_blocks, fill, 0)

        def per_expert(e, blk):
            pst_ref[e] = blk * tb
            pstart_ref[e] = blk * tb
            nb = (cnt_ref[e] + (tb - 1)) // tb

            def mark(b, carry):
                bexp_ref[blk + b] = e
                return carry
            lax.fori_loop(0, nb, mark, 0)
            return blk + nb
        nused = lax.fori_loop(0, N_EXPERTS, per_expert, 0)
        nused_ref[0] = nused
        pst_ref[N_EXPERTS] = nused

    rows = tm * SUBLANES
    slot = i % DISPATCH_SLOTS

    def tile_load(tile, s):
        src = x1_ref.at[pl.ds(pl.multiple_of(tile * rows, rows), rows)]
        return pltpu.make_async_copy(src, xbuf_ref.at[s], lsem.at[s])

    def wait_rows(s):
        for k in range(TOP_K):
            _rows_wait(xbuf_ref.at[s], xpad_ref, tm, sem.at[s])

    @pl.when(i == 0)
    def _():
        tile_load(0, 0).start()

        @pl.when(last >= 1)
        def _():
            tile_load(1, 1).start()

    tile_load(i, slot).wait()
    for t in range(tm):
        for k in range(TOP_K):
            pos = pst_ref[ri_ref[k, t]] + ri_ref[TOP_K + k, t]
            _row_copy(xbuf_ref.at[slot], t, xpad_ref, pos, sem.at[slot]).start(priority=k)

    prev = (i + DISPATCH_SLOTS - 1) % DISPATCH_SLOTS

    @pl.when(i >= 1)
    def _():
        wait_rows(prev)

    @pl.when(i + 2 <= last)
    def _():
        tile_load(i + 2, prev).start()

    @pl.when(i == last)
    def _():
        wait_rows(slot)
        zero_ref[...] = jnp.zeros_like(zero_ref)

        def per_expert(e, carry):
            cnt = cnt_ref[e]
            first = pst_ref[e] + cnt
            npad = ((cnt + (tb - 1)) // tb) * tb - cnt

            def start(p, c2):
                _row_copy(zero_ref, 0, xpad_ref, first + p, zsem).start()
                return c2
            lax.fori_loop(0, npad, start, 0)

            def wait(p, c2):
                _row_copy(zero_ref, 0, xpad_ref, 0, zsem).wait()
                return c2
            lax.fori_loop(0, npad, wait, 0)
            return carry
        lax.fori_loop(0, N_EXPERTS, per_expert, 0)

        def block_copy(b):
            n = tb * SUBLANES
            return pltpu.make_async_copy(zero_ref, xpad_ref.at[pl.ds(pl.multiple_of(b * n, n), n)], zsem)

        def start_block(b, carry):
            block_copy(b).start()
            return carry
        lax.fori_loop(pst_ref[N_EXPERTS], n_blocks, start_block, 0)

        def wait_block(b, carry):
            block_copy(b).wait()
            return carry
        lax.fori_loop(pst_ref[N_EXPERTS], n_blocks, wait_block, 0)


def _dispatch(counts, route_i, x1, tb, n_blocks):
    T = x1.shape[0] // SUBLANES
    tm = min(DISPATCH_TM, T)
    smem_full = pl.BlockSpec(memory_space=pltpu.SMEM)
    return pl.pallas_call(
        functools.partial(_dispatch_kernel, tb=tb, n_blocks=n_blocks, tm=tm),
        grid=(T // tm,),
        in_specs=[
            smem_full,
            pl.BlockSpec((SUBLANES, tm), lambda i: (0, i), memory_space=pltpu.SMEM),
            pl.BlockSpec(memory_space=pl.ANY),
        ],
        out_specs=[
            pl.BlockSpec(memory_space=pl.ANY),
            smem_full, smem_full, smem_full,
        ],
        out_shape=[
            jax.ShapeDtypeStruct((n_blocks * tb * SUBLANES, LANES), x1.dtype),
            jax.ShapeDtypeStruct((N_EXPERTS,), I32),
            jax.ShapeDtypeStruct((n_blocks,), I32),
            jax.ShapeDtypeStruct((1,), I32),
        ],
        scratch_shapes=[
            pltpu.SMEM((N_EXPERTS + 1,), I32),
            pltpu.VMEM((tb * SUBLANES, LANES), x1.dtype),
            pltpu.VMEM((DISPATCH_SLOTS, tm * SUBLANES, LANES), x1.dtype),
            pltpu.SemaphoreType.DMA((DISPATCH_SLOTS,)),
            pltpu.SemaphoreType.DMA((DISPATCH_SLOTS,)),
            pltpu.SemaphoreType.DMA(()),
        ],
        compiler_params=_cparams(("arbitrary",)),
        name="moe_dispatch",
    )(counts, route_i, x1)


def _expert_kernel(bexp_ref, nused_ref, x_ref, wg_hbm, wu_hbm, wd_hbm, y_ref,
                   wgf_ref, wuf_ref, wdf_ref, wgb_ref, wub_ref, wdb_ref, sem, run_ref):
    b = pl.program_id(0)
    nused = nused_ref[0]
    used = b < nused
    e = bexp_ref[b]
    new_expert = jnp.logical_or(b == 0, e != bexp_ref[jnp.maximum(b - 1, 0)])
    streams = ((wg_hbm, wgf_ref), (wu_hbm, wuf_ref), (wd_hbm, wdf_ref))

    def weight_copies(expert, slot):
        return [pltpu.make_async_copy(w_hbm.at[expert], wf_ref.at[slot], sem.at[slot, j])
                for j, (w_hbm, wf_ref) in enumerate(streams)]

    @pl.when(jnp.logical_and(used, b == 0))
    def _():
        run_ref[0] = 0
        for cp in weight_copies(e, 0):
            cp.start()

    @pl.when(jnp.logical_and(used, new_expert))
    def _():
        slot = run_ref[0] % 2
        nxt = lax.while_loop(lambda j: jnp.logical_and(j < nused, bexp_ref[jnp.minimum(j, nused - 1)] == e),
                             lambda j: j + 1, b + 1)

        @pl.when(nxt < nused)
        def _():
            for cp in weight_copies(bexp_ref[jnp.minimum(nxt, nused - 1)], 1 - slot):
                cp.start()

        for cp in weight_copies(e, slot):
            cp.wait()
        wgb_ref[...] = wgf_ref[slot].astype(BF16)
        wub_ref[...] = wuf_ref[slot].astype(BF16)
        wdb_ref[...] = wdf_ref[slot].astype(BF16)
        run_ref[0] = run_ref[0] + 1

    @pl.when(used)
    def _():
        half = SUBLANES * LANES
        x_lo, x_hi = (t.astype(BF16) for t in _unpack_halves(_tiles_to_rows(x_ref)))
        dot = functools.partial(jnp.dot, preferred_element_type=F32)
        gate = dot(x_lo, wgb_ref[:half, :]) + dot(x_hi, wgb_ref[half:, :])
        up = dot(x_lo, wub_ref[:half, :]) + dot(x_hi, wub_ref[half:, :])
        hid = (gate * _sigmoid(gate) * up).astype(BF16)
        _rows_to_tiles(y_ref, _pack_halves(dot(hid, wdb_ref[...])))

    @pl.when(jnp.logical_not(used))
    def _():
        y_ref[...] = jnp.zeros_like(y_ref)


def _experts(bexp, nused, x_pad, w_gate, w_up, w_down, tb):
    P = x_pad.shape[0] // SUBLANES
    D = 2 * TILE_WORDS
    n_blocks = P // tb
    rows = (tb * SUBLANES, LANES)
    F = w_gate.shape[-1]
    blk = lambda b, be, nu: jnp.maximum(jnp.minimum(b, nu[0] - 1), 0)
    hbm = pl.BlockSpec(memory_space=pl.ANY)
    return pl.pallas_call(
        _expert_kernel,
        grid_spec=pltpu.PrefetchScalarGridSpec(
            num_scalar_prefetch=2,
            grid=(n_blocks,),
            in_specs=[pl.BlockSpec(rows, lambda b, be, nu: (blk(b, be, nu), 0)), hbm, hbm, hbm],
            out_specs=pl.BlockSpec(rows, lambda b, be, nu: (b, 0)),
            scratch_shapes=[
                pltpu.VMEM((2, D, F), F32), pltpu.VMEM((2, D, F), F32), pltpu.VMEM((2, F, D), F32),
                pltpu.VMEM((D, F), BF16), pltpu.VMEM((D, F), BF16), pltpu.VMEM((F, D), BF16),
                pltpu.SemaphoreType.DMA((2, 3)),
                pltpu.SMEM((1,), I32),
            ],
        ),
        out_shape=jax.ShapeDtypeStruct(x_pad.shape, U32),
        compiler_params=_cparams(("arbitrary",)),
        name="moe_experts",
    )(bexp, nused, x_pad, w_gate, w_up, w_down)


COMBINE_SLOTS = 3


def _combine_kernel(pstart_ref, ri_ref, r1_ref, r2_ref, x1_ref, rw_ref, g_ref, b_ref, ypad_ref, o_ref, ybuf_ref, sem):
    i = pl.program_id(0)
    n = pl.num_programs(0)
    tm = x1_ref.shape[0]
    slot = i % COMBINE_SLOTS

    def gather_loop(route_ref, s):
        def issue(t, carry):
            for k in range(TOP_K):
                pos = pstart_ref[route_ref[k, t]] + route_ref[TOP_K + k, t]
                _row_copy(ypad_ref, pos, ybuf_ref.at[s, k], t, sem.at[s]).start(priority=k)
            return carry
        lax.fori_loop(0, tm, issue, 0, unroll=8)

    def wait_slot(s):
        for k in range(TOP_K):
            _rows_wait(ypad_ref, ybuf_ref.at[s, k], tm, sem.at[s])

    @pl.when(i == 0)
    def _():
        gather_loop(ri_ref, 0)
        gather_loop(r1_ref, 1)

    wait_slot(slot)

    rw = rw_ref[...]
    y0 = _unpack_halves(_tiles_to_rows(ybuf_ref.at[slot, 0]))
    y1 = _unpack_halves(_tiles_to_rows(ybuf_ref.at[slot, 1]))
    ffn = jnp.concatenate([y0[h] * rw[:, 0:1] + y1[h] * rw[:, 1:2] for h in range(2)], axis=1)
    pre = DEEPNORM_ALPHA * x1_ref[...] + ffn
    mu = jnp.mean(pre, axis=-1, keepdims=True)
    pc = pre - mu
    var = jnp.mean(pc * pc, axis=-1, keepdims=True)
    o_ref[...] = pc * lax.rsqrt(var + LN_EPS) * g_ref[...] + b_ref[...]

    nxt = (i + 2) % COMBINE_SLOTS
    for t in range(tm):
        for k in range(TOP_K):
            pos = pstart_ref[r2_ref[k, t]] + r2_ref[TOP_K + k, t]
            _row_copy(ypad_ref, pos, ybuf_ref.at[nxt, k], t, sem.at[nxt]).start(priority=k)

    @pl.when(i == n - 1)
    def _():
        wait_slot((i + 1) % COMBINE_SLOTS)
        wait_slot(nxt)


def _combine(pstart, route_i, x1, rw_col, ln_g, ln_b, y_pad):
    T, D = x1.shape
    tm = min(COMBINE_TM, T)
    n = T // tm
    return pl.pallas_call(
        _combine_kernel,
        grid=(n,),
        in_specs=[
            pl.BlockSpec(memory_space=pltpu.SMEM),
            pl.BlockSpec((SUBLANES, tm), lambda i: (0, i), memory_space=pltpu.SMEM),
            pl.BlockSpec((SUBLANES, tm), lambda i: (0, jnp.minimum(i + 1, n - 1)), memory_space=pltpu.SMEM),
            pl.BlockSpec((SUBLANES, tm), lambda i: (0, jnp.minimum(i + 2, n - 1)), memory_space=pltpu.SMEM),
            pl.BlockSpec((tm, D), lambda i: (i, 0)),
            pl.BlockSpec((tm, SUBLANES), lambda i: (i, 0)),
            pl.BlockSpec((1, D), lambda i: (0, 0)),
            pl.BlockSpec((1, D), lambda i: (0, 0)),
            pl.BlockSpec(memory_space=pl.ANY),
        ],
        out_specs=pl.BlockSpec((tm, D), lambda i: (i, 0)),
        out_shape=jax.ShapeDtypeStruct((T, D), F32),
        scratch_shapes=[
            pltpu.VMEM((COMBINE_SLOTS, TOP_K, tm * SUBLANES, LANES), U32),
            pltpu.SemaphoreType.DMA((COMBINE_SLOTS,)),
        ],
        compiler_params=_cparams(("arbitrary",)),
        name="moe_combine_ln2",
    )(pstart, route_i, route_i, route_i, x1, rw_col, ln_g, ln_b, y_pad)


N_GATE_COLS = 2 * A_HEADS
SRC_MLSTM = 0
SRC_GATES = 4 * A_WIDTH
SRC_DIFF = SRC_GATES + N_GATE_COLS
SRC_MERGE = SRC_DIFF + 3 * B_WIDTH
PREP_TN = 512


def _wprep_kernel(a_ref, o_ref):
    o_ref[...] = a_ref[...].T.astype(BF16)


def _wgate_kernel(a_ref, o_ref):
    a = a_ref[...]
    rows = jnp.concatenate([a, jnp.zeros((LANES - a.shape[0], a.shape[1]), F32)], axis=0)
    o_ref[...] = rows.T


def _rearrange_in_proj(w_in, b_in):
    K, n_in = w_in.shape
    tn = PREP_TN
    merge_blocks = (2 * D_MODEL) // tn
    mlstm_blocks = (4 * A_WIDTH) // tn
    assert SRC_MERGE % SUBLANES == 0 and SRC_DIFF % SUBLANES == 0

    def src_row(jb):
        s = SUBLANES
        merge = SRC_MERGE // s + (tn // s) * jb
        mlstm = SRC_MLSTM // s + (tn // s) * (jb - merge_blocks)
        diff = SRC_DIFF // s + (tn // s) * (jb - merge_blocks - mlstm_blocks)
        return s * jnp.where(jb < merge_blocks, merge, jnp.where(jb < merge_blocks + mlstm_blocks, mlstm, diff))

    w_t = jnp.swapaxes(w_in, 0, 1)
    w_main = pl.pallas_call(
        _wprep_kernel,
        grid=(N_MAIN // tn,),
        in_specs=[pl.BlockSpec((pl.Element(tn), pl.Element(K)), lambda jb: (src_row(jb), 0))],
        out_specs=pl.BlockSpec((K, tn), lambda jb: (0, jb)),
        out_shape=jax.ShapeDtypeStruct((K, N_MAIN), BF16),
        compiler_params=_cparams(("parallel",)),
        name="w_in_prep",
    )(w_t)
    b_main = jnp.concatenate([b_in[SRC_MERGE:], b_in[SRC_MLSTM:SRC_GATES], b_in[SRC_DIFF:SRC_MERGE]])[None, :]
    w_gate = pl.pallas_call(
        _wgate_kernel,
        grid=(1,),
        in_specs=[pl.BlockSpec((pl.Element(N_GATE_COLS), pl.Element(K)), lambda i: (SRC_GATES, 0))],
        out_specs=pl.BlockSpec((K, LANES), lambda i: (0, 0)),
        out_shape=jax.ShapeDtypeStruct((K, LANES), F32),
        name="w_gate_prep",
    )(w_t)
    b_gate = jnp.pad(b_in[SRC_GATES:SRC_DIFF], (0, LANES - N_GATE_COLS))[None, :]
    return w_main, b_main, _hi_lo_columns(w_gate), b_gate


def _layer(x, w_in, b_in, conv_w, conv_b, norm_a_g, lq1, lk1, lq2, lk2, norm_b_g, w_a, w_b, w_out,
           ln1_g, ln1_b, w_grp, b_grp, w_exp, b_exp, w_gate, w_up, w_down, ln2_g, ln2_b, lambda_init):
    B, S, D = x.shape
    T = B * S
    x2d = x.reshape(T, D)

    w_main, b_main, w_g, b_g = _rearrange_in_proj(w_in, b_in)
    z_main, z_gate = _in_projection(x2d, w_main, b_main, w_g, b_g)

    L = min(MLSTM_CHUNK, S)
    gp = _gate_prep(z_gate[:, :2 * A_HEADS].T, L)
    h_a = _mlstm(z_main, gp.T, gp, conv_w, conv_b[None, :], norm_a_g[None, :], B, S)
    o_b = _diff_attention(z_main, lq1[None, :], lk1[None, :], lq2[None, :], lk2[None, :],
                          norm_b_g[None, :], B, S, lambda_init)

    w_r = jnp.zeros((D, LANES), F32).at[:, :N_GROUPS].set(w_grp).at[:, SUBLANES:SUBLANES + N_EXPERTS].set(w_exp)
    b_r = jnp.zeros((LANES,), F32).at[:N_GROUPS].set(b_grp).at[SUBLANES:SUBLANES + N_EXPERTS].set(b_exp)[None, :]
    x1, x1_packed, route_i, route_w, counts = _mix(h_a, o_b, z_main, x2d, w_a.astype(BF16), w_b.astype(BF16),
                                                   w_out.astype(BF16), ln1_g[None, :], ln1_b[None, :],
                                                   _hi_lo_columns(w_r), b_r)

    tb = MOE_TB
    n_blocks = (T * TOP_K) // tb + N_EXPERTS
    x_pad, pstart, bexp, nused = _dispatch(counts[:, 0], route_i, x1_packed, tb, n_blocks)
    y_pad = _experts(bexp, nused, x_pad, w_gate, w_up, w_down, tb)
    out = _combine(pstart, route_i, x1, route_w.T, ln2_g[None, :], ln2_b[None, :], y_pad)
    return out.reshape(B, S, D)


def kernel(x, w_in, b_in, conv_w, conv_b, mlstm_norm_g, lambda_q1, lambda_k1, lambda_q2, lambda_k2,
           diff_norm_g, w_a, w_b, w_out, ln1_g, ln1_b, w_grp, b_grp, w_exp, b_exp,
           w_gate, w_up, w_down, ln2_g, ln2_b):
    for l in range(DEPTH):
        lambda_init = 0.8 - 0.6 * math.exp(-0.3 * l)
        x = _layer(x, w_in[l], b_in[l], conv_w[l], conv_b[l], mlstm_norm_g[l], lambda_q1[l], lambda_k1[l],
                   lambda_q2[l], lambda_k2[l], diff_norm_g[l], w_a[l], w_b[l], w_out[l], ln1_g[l], ln1_b[l],
                   w_grp[l], b_grp[l], w_exp[l], b_exp[l], w_gate[l], w_up[l], w_down[l], ln2_g[l], ln2_b[l],
                   lambda_init)
    return x
```

```python
import functools
import math

import jax
import jax.numpy as jnp
from jax import lax
from jax.experimental import pallas as pl
from jax.experimental.pallas import tpu as pltpu

F32 = jnp.float32
BF16 = jnp.bfloat16
I32 = jnp.int32

D_MODEL = 2048
A_HEADS = 4
A_HEAD_DIM = 256
A_WIDTH = A_HEADS * A_HEAD_DIM
CONV_WIDTH = 4
B_HEADS = 8
B_HEAD_DIM = 64
B_V_DIM = 2 * B_HEAD_DIM
B_WIDTH = B_HEADS * B_V_DIM
ATTN_CHUNK = 64
N_GROUPS = 4
EXPERTS_PER_GROUP = 8
N_EXPERTS = N_GROUPS * EXPERTS_PER_GROUP
TOP_K = 2
D_EXPERT = 512
DEPTH = 1
DEEPNORM_ALPHA = (2 * DEPTH) ** 0.25
LN_EPS = 1e-5
NEG_BIG = -1e30
LOG2_E = 1.4426950408889634

LANES = 128
SUBLANES = 8
VMEM_LIMIT_BYTES = 56 * 1024 * 1024

COL_GA = 0
COL_GB = COL_GA + D_MODEL
COL_QA = COL_GB + D_MODEL
COL_KA = COL_QA + A_WIDTH
COL_VA = COL_KA + A_WIDTH
COL_OA = COL_VA + A_WIDTH
COL_QB = COL_OA + A_WIDTH
COL_KB = COL_QB + B_WIDTH
COL_VB = COL_KB + B_WIDTH
N_MAIN = COL_VB + B_WIDTH

PROJ_TM, PROJ_TN = 1024, 1024
MLSTM_CHUNK = 512
ATTN_TQ = 512
MIX_TM = 256
MOE_TB = 256
DISPATCH_TM = 256
COMBINE_TM = 256


def _cparams(sem, vmem=VMEM_LIMIT_BYTES):
    return pltpu.CompilerParams(dimension_semantics=sem, vmem_limit_bytes=vmem)


def _sigmoid(x):
    return 1.0 / (1.0 + jnp.exp(-x))


U32 = jnp.uint32


def _pack_halves(x):
    n = x.shape[1] // 2
    lo = lax.bitcast_convert_type(x[:, :n].astype(BF16).astype(F32), U32)
    hi = lax.bitcast_convert_type(x[:, n:].astype(BF16).astype(F32), U32)
    return (lo >> 16) | hi


def _unpack_halves(u):
    lo = lax.bitcast_convert_type(u << 16, F32)
    hi = lax.bitcast_convert_type(u & jnp.uint32(0xFFFF0000), F32)
    return lo, hi


TILE_WORDS = SUBLANES * LANES


def _rows_to_tiles(ref, v):
    rows = v.shape[0]
    for j in range(SUBLANES):
        ref[pl.ds(j, rows, stride=SUBLANES), :] = v[:, j * LANES:(j + 1) * LANES]


def _tiles_to_rows(ref):
    rows = ref.shape[0] // SUBLANES
    return jnp.concatenate([ref[pl.ds(j, rows, stride=SUBLANES), :] for j in range(SUBLANES)], axis=1)


def _split_bf16(a):
    hi = a.astype(BF16)
    return hi, (a - hi.astype(F32)).astype(BF16)


def _hi_lo_columns(w):
    return jnp.concatenate(_split_bf16(w), axis=1)


def _dot_3pass(a, w_hl):
    a_hi, a_lo = _split_bf16(a)
    n = w_hl.shape[1] // 2
    dot = functools.partial(jnp.dot, preferred_element_type=F32)
    r = dot(a_hi, w_hl)
    return r[:, :n] + (r[:, n:] + dot(a_lo, w_hl[:, :n]))


def _proj_kernel(x_ref, w_ref, b_ref, wg_ref, bg_ref, z_ref, zg_ref, xb_ref):
    @pl.when(pl.program_id(1) == 0)
    def _():
        x = x_ref[...]
        xb_ref[...] = x.astype(BF16)
        zg_ref[...] = _dot_3pass(x, wg_ref[...]) + bg_ref[...]

    acc = jnp.dot(xb_ref[...], w_ref[...], preferred_element_type=F32)
    z_ref[...] = (acc + b_ref[...]).astype(z_ref.dtype)


def _in_projection(x2d, w_main, b_main, w_gate, b_gate):
    T, K = x2d.shape
    N = w_main.shape[1]
    tm, tn = min(PROJ_TM, T), PROJ_TN
    return pl.pallas_call(
        _proj_kernel,
        grid=(T // tm, N // tn),
        in_specs=[
            pl.BlockSpec((tm, K), lambda i, j: (i, 0)),
            pl.BlockSpec((K, tn), lambda i, j: (0, j)),
            pl.BlockSpec((1, tn), lambda i, j: (0, j)),
            pl.BlockSpec((K, 2 * LANES), lambda i, j: (0, 0)),
            pl.BlockSpec((1, LANES), lambda i, j: (0, 0)),
        ],
        out_specs=[
            pl.BlockSpec((tm, tn), lambda i, j: (i, j)),
            pl.BlockSpec((tm, LANES), lambda i, j: (i, 0)),
        ],
        out_shape=[
            jax.ShapeDtypeStruct((T, N), BF16),
            jax.ShapeDtypeStruct((T, LANES), F32),
        ],
        scratch_shapes=[pltpu.VMEM((tm, K), BF16)],
        compiler_params=_cparams(("parallel", "arbitrary")),
        name="in_projection",
    )(x2d, w_main, b_main, w_gate, b_gate)


def _gate_prep_kernel(g_ref, o_ref, *, L):
    r = lax.broadcasted_iota(I32, (L, L), 0)
    c = lax.broadcasted_iota(I32, (L, L), 1)
    tri = (r <= c).astype(F32)
    row = lax.broadcasted_iota(I32, (g_ref.shape[0], L), 0)
    for ch in range(g_ref.shape[1] // L):
        g = g_ref[:, ch * L:(ch + 1) * L]
        lf = jnp.minimum(g, 0.0) - jnp.log(1.0 + jnp.exp(-jnp.abs(g)))
        b = jnp.dot(lf, tri, preferred_element_type=F32, precision=lax.Precision.HIGHEST)
        o_ref[:, ch * L:(ch + 1) * L] = jnp.where(row < A_HEADS, g, b)


def _gate_prep(g_rows, L):
    R, T = g_rows.shape
    per_step = math.gcd(T // L, 8)
    W = L * per_step
    return pl.pallas_call(
        functools.partial(_gate_prep_kernel, L=L),
        grid=(T // W,),
        in_specs=[pl.BlockSpec((R, W), lambda i: (0, i))],
        out_specs=pl.BlockSpec((R, W), lambda i: (0, i)),
        out_shape=jax.ShapeDtypeStruct((R, T), F32),
        compiler_params=_cparams(("parallel",)),
        name="gate_prep",
    )(g_rows)


def _mlstm_kernel(q_ref, k_ref, v_ref, o_ref, gcol_ref, grow_ref, cw_ref, cb_ref, ng_ref, out_ref,
                  c_ref, n_ref, m_ref, qcar_ref, kcar_ref):
    ci = pl.program_id(1)
    L = q_ref.shape[0]
    dh = A_HEAD_DIM

    @pl.when(ci == 0)
    def _():
        c_ref[...] = jnp.zeros_like(c_ref)
        n_ref[...] = jnp.zeros_like(n_ref)
        m_ref[...] = jnp.zeros_like(m_ref)
        qcar_ref[...] = jnp.zeros_like(qcar_ref)
        kcar_ref[...] = jnp.zeros_like(kcar_ref)

    def conv_silu(u_ref, car_ref, lo, wcol):
        u = u_ref[:, lo:lo + dh].astype(F32)
        ext = jnp.concatenate([car_ref[:, lo:lo + dh], u], axis=0)
        w = cw_ref[:, wcol:wcol + dh]
        y = cb_ref[:, wcol:wcol + dh] + w[3:4, :] * u
        for j in range(CONV_WIDTH - 1):
            off = SUBLANES - (CONV_WIDTH - 1) + j
            y = y + w[j:j + 1, :] * ext[off:off + L, :]
        car_ref[:, lo:lo + dh] = u[L - SUBLANES:, :]
        return y * _sigmoid(y)

    r = lax.broadcasted_iota(I32, (L, L), 0)
    c = lax.broadcasted_iota(I32, (L, L), 1)
    causal = r >= c
    gcol = gcol_ref[...]
    grow = grow_ref[...]

    for h in range(A_HEADS):
        lo = h * dh
        q = conv_silu(q_ref, qcar_ref, lo, lo)
        k = conv_silu(k_ref, kcar_ref, lo, A_WIDTH + lo) * (dh ** -0.5)
        v_b = v_ref[:, lo:lo + dh]
        q_b = q.astype(BF16)
        k_b = k.astype(BF16)

        i_col, b_col = gcol[:, h:h + 1], gcol[:, A_HEADS + h:A_HEADS + h + 1]
        i_row, b_row = grow[h:h + 1, :], grow[A_HEADS + h:A_HEADS + h + 1, :]
        b_last = b_row[:, L - 1:L]
        m_prev = m_ref[h]

        dmat = jnp.where(causal, b_col - b_row + i_row, NEG_BIG)
        inter = b_col + m_prev
        m_t = jnp.maximum(inter, jnp.max(dmat, axis=-1, keepdims=True))
        s = lax.dot_general(q_b, k_b, (((1,), (1,)), ((), ())), preferred_element_type=F32)
        w_intra = jnp.exp(dmat - m_t) * s
        w_inter = jnp.exp(inter - m_t)
        qc = jnp.dot(q_b, c_ref[h].astype(BF16), preferred_element_type=F32)
        num = w_inter * qc + jnp.dot(w_intra.astype(BF16), v_b, preferred_element_type=F32)
        qn = jnp.sum(q * n_ref[h], axis=-1, keepdims=True)
        den = w_inter * qn + jnp.sum(w_intra, axis=-1, keepdims=True)
        hh = num / jnp.maximum(jnp.abs(den), jnp.exp(-m_t))

        g_col = b_last - b_col + i_col
        m_new = jnp.maximum(b_last + m_prev, jnp.max(g_col, axis=0, keepdims=True))
        decay = jnp.exp(b_last + m_prev - m_new)
        kw = jnp.exp(g_col - m_new) * k
        c_ref[h] = decay * c_ref[h] + jnp.dot(kw.T.astype(BF16), v_b, preferred_element_type=F32)
        n_ref[h] = decay * n_ref[h] + jnp.sum(kw, axis=0, keepdims=True)
        m_ref[h] = m_new

        y = _sigmoid(o_ref[:, lo:lo + dh].astype(F32)) * hh
        mu = jnp.mean(y, axis=-1, keepdims=True)
        yc = y - mu
        var = jnp.mean(yc * yc, axis=-1, keepdims=True)
        out_ref[:, lo:lo + dh] = (yc * lax.rsqrt(var + LN_EPS) * ng_ref[:, lo:lo + dh]).astype(out_ref.dtype)


def _mlstm(z_main, gcol, grow, conv_w, conv_b, norm_g, B, S):
    L = min(MLSTM_CHUNK, S)
    nc = S // L
    dh = A_HEAD_DIM
    H = A_HEADS
    W = A_WIDTH
    row = lambda b, c: b * nc + c
    full = lambda shape: pl.BlockSpec(shape, lambda b, c: (0, 0))
    return pl.pallas_call(
        _mlstm_kernel,
        grid=(B, nc),
        in_specs=[
            pl.BlockSpec((L, W), lambda b, c: (row(b, c), COL_QA // W)),
            pl.BlockSpec((L, W), lambda b, c: (row(b, c), COL_KA // W)),
            pl.BlockSpec((L, W), lambda b, c: (row(b, c), COL_VA // W)),
            pl.BlockSpec((L, W), lambda b, c: (row(b, c), COL_OA // W)),
            pl.BlockSpec((L, SUBLANES), lambda b, c: (row(b, c), 0)),
            pl.BlockSpec((SUBLANES, L), lambda b, c: (0, row(b, c))),
            full((CONV_WIDTH, 2 * W)), full((1, 2 * W)), full((1, W)),
        ],
        out_specs=pl.BlockSpec((L, W), lambda b, c: (row(b, c), 0)),
        out_shape=jax.ShapeDtypeStruct((B * S, W), BF16),
        scratch_shapes=[
            pltpu.VMEM((H, dh, dh), F32),
            pltpu.VMEM((H, 1, dh), F32),
            pltpu.VMEM((H, 1, 1), F32),
            pltpu.VMEM((SUBLANES, W), F32),
            pltpu.VMEM((SUBLANES, W), F32),
        ],
        compiler_params=_cparams(("parallel", "arbitrary")),
        name="mlstm",
    )(z_main, z_main, z_main, z_main, gcol, grow, conv_w, conv_b, norm_g)


def _lane_tiles(t):
    return [t[:, i * LANES:(i + 1) * LANES] for i in range(t.shape[1] // LANES)]


def _attn_kernel(slope_ref, q_ref, k_ref, kb_ref, v_ref, lq1_ref, lk1_ref, lq2_ref, lk2_ref, g_ref, o_ref,
                 s_ref, mx_ref, ls_ref, acc_ref, corr_ref, *, lambda_init):
    h = pl.program_id(1)
    qi = pl.program_id(2)
    tq = q_ref.shape[0]
    tk = tq
    slope = slope_ref[h]
    nt = (((1,), (1,)), ((), ()))

    lane = lax.broadcasted_iota(I32, (tq, B_V_DIM), 1)
    q = q_ref[...] * (B_HEAD_DIM ** -0.5)
    zero = jnp.zeros_like(q)
    bias_cols = jnp.where(lane < 2, 1.0, 0.0).astype(BF16)
    qa = (jnp.concatenate([jnp.where(lane < B_HEAD_DIM, q, zero), bias_cols], axis=1),
          jnp.concatenate([jnp.where(lane >= B_HEAD_DIM, q, zero), bias_cols], axis=1))

    def scores(j):
        start = pl.multiple_of(j * tk, tk)
        ka = jnp.concatenate([k_ref[pl.ds(start, tk), :], kb_ref[pl.ds(start, tk), :]], axis=1)
        return [lax.dot_general(qa[n], ka, nt, preferred_element_type=F32) for n in range(2)]

    def keep(j, n, t):
        t = t * LOG2_E
        s_ref[n, j] = t
        mx_ref[n] = functools.reduce(jnp.maximum, _lane_tiles(t), mx_ref[n])

    mx_ref[...] = jnp.full(mx_ref.shape, NEG_BIG, F32)

    def pass_a(j):
        t = scores(j)
        for n in range(2):
            keep(j, n, t[n])

    def pass_a_pair(jj, carry):
        pass_a(2 * jj)
        pass_a(2 * jj + 1)
        return carry
    lax.fori_loop(0, qi // 2, pass_a_pair, 0)

    @pl.when(qi % 2 == 1)
    def _():
        pass_a(qi - 1)

    @pl.when(qi == 0)
    def _():
        r = lax.broadcasted_iota(I32, (tq, tk), 0)
        c = lax.broadcasted_iota(I32, (tq, tk), 1)
        ahead = jnp.where(c <= r, 0.0, (r - c).astype(F32) * (2.0 * slope))
        corr_ref[...] = jnp.where((c // ATTN_CHUNK) <= (r // ATTN_CHUNK), ahead, NEG_BIG)

    hq = tq // 2
    dstart = pl.multiple_of(qi * tk, tk)
    ka = jnp.concatenate([k_ref[pl.ds(dstart, tk), :], kb_ref[pl.ds(dstart, tk), :]], axis=1)
    masked = jnp.full((hq, tk - hq), NEG_BIG, F32)
    for n in range(2):
        top = lax.dot_general(qa[n][:hq], ka[:hq], nt, preferred_element_type=F32)
        bot = lax.dot_general(qa[n][hq:], ka, nt, preferred_element_type=F32)
        t = jnp.concatenate([jnp.concatenate([top, masked], axis=1), bot], axis=0)
        keep(qi, n, t + corr_ref[...])

    for n in range(2):
        m = jnp.max(mx_ref[n], axis=-1, keepdims=True)
        mx_ref[n] = jnp.broadcast_to(m, (tq, LANES))
    ls_ref[...] = jnp.zeros_like(ls_ref)
    acc_ref[...] = jnp.zeros_like(acc_ref)

    def weigh(n, rows, s_tile, v_tile):
        mb = mx_ref[n, rows, :]
        ps = [jnp.exp2(t - mb) for t in _lane_tiles(s_tile)]
        ls_ref[n, rows, :] += functools.reduce(jnp.add, ps)
        p = jnp.concatenate(ps, axis=1).astype(BF16)
        acc_ref[n, rows, :] += jnp.dot(p, v_tile, preferred_element_type=F32)

    def pass_b(j):
        start = pl.multiple_of(j * tk, tk)
        vj = v_ref[pl.ds(start, tk), :]
        for n in range(2):
            weigh(n, slice(None), s_ref[n, j], vj)

    def pass_b_pair(jj, carry):
        pass_b(2 * jj)
        pass_b(2 * jj + 1)
        return carry
    lax.fori_loop(0, qi // 2, pass_b_pair, 0)

    @pl.when(qi % 2 == 1)
    def _():
        pass_b(qi - 1)

    for n in range(2):
        weigh(n, slice(0, hq), s_ref[n, qi, :hq, :hq], v_ref[pl.ds(dstart, hq), :])
        weigh(n, slice(hq, tq), s_ref[n, qi, hq:, :], v_ref[pl.ds(dstart, tk), :])

    lam = (jnp.exp(jnp.sum(lq1_ref[...] * lk1_ref[...], axis=-1, keepdims=True))
           - jnp.exp(jnp.sum(lq2_ref[...] * lk2_ref[...], axis=-1, keepdims=True)) + lambda_init)
    l0 = jnp.sum(ls_ref[0], axis=-1, keepdims=True)
    l1 = jnp.sum(ls_ref[1], axis=-1, keepdims=True)
    o = acc_ref[0] / l0 - lam * (acc_ref[1] / l1)
    ms = jnp.mean(o * o, axis=-1, keepdims=True)
    o_ref[...] = (o * lax.rsqrt(ms + LN_EPS) * g_ref[...] * (1.0 - lambda_init)).astype(o_ref.dtype)


def _diff_attention(z_main, lq1, lk1, lq2, lk2, norm_g, B, S, lambda_init):
    tq = min(ATTN_TQ, S)
    nq = S // tq
    H = B_HEADS
    dv = B_V_DIM
    slopes = 2.0 ** (-8.0 * jnp.arange(1, H + 1, dtype=F32) / H)
    assert S <= 256 * 256
    pos = lax.broadcasted_iota(I32, (H, S, dv), 1)
    col = lax.broadcasted_iota(I32, (H, S, dv), 2)
    within = pos % 256
    kbias = jnp.where(col == 0, within, jnp.where(col == 1, pos - within, 0)).astype(F32)
    kbias = (kbias * slopes[:, None, None]).astype(BF16)
    small = pl.BlockSpec((1, B_HEAD_DIM), lambda b, h, i: (0, 0))
    return pl.pallas_call(
        functools.partial(_attn_kernel, lambda_init=lambda_init),
        grid=(B, H, nq),
        in_specs=[
            pl.BlockSpec(memory_space=pltpu.SMEM),
            pl.BlockSpec((tq, dv), lambda b, h, i: (b * nq + i, COL_QB // dv + h)),
            pl.BlockSpec((S, dv), lambda b, h, i: (b, COL_KB // dv + h)),
            pl.BlockSpec((None, S, dv), lambda b, h, i: (h, 0, 0)),
            pl.BlockSpec((S, dv), lambda b, h, i: (b, COL_VB // dv + h)),
            small, small, small, small,
            pl.BlockSpec((1, dv), lambda b, h, i: (0, 0)),
        ],
        out_specs=pl.BlockSpec((tq, dv), lambda b, h, i: (b * nq + i, h)),
        out_shape=jax.ShapeDtypeStruct((B * S, B_WIDTH), BF16),
        scratch_shapes=[
            pltpu.VMEM((2, nq, tq, tq), F32),
            pltpu.VMEM((2, tq, LANES), F32),
            pltpu.VMEM((2, tq, LANES), F32),
            pltpu.VMEM((2, tq, dv), F32),
            pltpu.VMEM((tq, tq), F32),
        ],
        compiler_params=_cparams(("parallel", "parallel", "arbitrary")),
        name="diff_attention",
    )(slopes, z_main, z_main, kbias, z_main, lq1, lk1, lq2, lk2, norm_g)


def _mix_kernel(ha_ref, ob_ref, ga_ref, gb_ref, x_ref, wa_ref, wb_ref, wo_ref, g1_ref, b1_ref, wr_ref, br_ref,
                x1_ref, x1p_ref, ri_ref, rw_ref, cnt_ref, carry_ref):
    i = pl.program_id(0)
    tm = x_ref.shape[0]

    @pl.when(i == 0)
    def _():
        carry_ref[...] = jnp.zeros_like(carry_ref)

    ya = jnp.dot(ha_ref[...], wa_ref[...], preferred_element_type=F32)
    yb = jnp.dot(ob_ref[...], wb_ref[...], preferred_element_type=F32)
    merged = _sigmoid(ga_ref[...].astype(F32)) * ya + _sigmoid(gb_ref[...].astype(F32)) * yb
    mix = jnp.dot(merged.astype(BF16), wo_ref[...], preferred_element_type=F32)
    pre = DEEPNORM_ALPHA * x_ref[...] + mix
    mu = jnp.mean(pre, axis=-1, keepdims=True)
    pc = pre - mu
    var = jnp.mean(pc * pc, axis=-1, keepdims=True)
    x1 = pc * lax.rsqrt(var + LN_EPS) * g1_ref[...] + b1_ref[...]
    x1_ref[...] = x1
    _rows_to_tiles(x1p_ref, _pack_halves(x1))

    logits = _dot_3pass(x1, wr_ref[...]) + br_ref[...]
    lt = logits.T
    row8 = lax.broadcasted_iota(I32, (SUBLANES, tm), 0).astype(F32)
    gl = jnp.where(row8 < N_GROUPS, lt[0:SUBLANES, :], NEG_BIG)
    gmax = jnp.max(gl, axis=0, keepdims=True)
    gsel = jnp.min(jnp.where(gl == gmax, row8, float(SUBLANES)), axis=0, keepdims=True)
    gprob = 1.0 / jnp.sum(jnp.exp(gl - gmax), axis=0, keepdims=True)
    ing = lt[SUBLANES:2 * SUBLANES, :]
    for g in range(1, N_GROUPS):
        ing = jnp.where(gsel == g, lt[(g + 1) * SUBLANES:(g + 2) * SUBLANES, :], ing)
    v0 = jnp.max(ing, axis=0, keepdims=True)
    i0 = jnp.min(jnp.where(ing == v0, row8, float(SUBLANES)), axis=0, keepdims=True)
    ing2 = jnp.where(row8 == i0, -jnp.inf, ing)
    v1 = jnp.max(ing2, axis=0, keepdims=True)
    i1 = jnp.min(jnp.where(ing2 == v1, row8, float(SUBLANES)), axis=0, keepdims=True)
    ex = jnp.exp(v1 - v0)
    inv = 1.0 / (1.0 + ex)
    w0 = gprob * inv
    w1 = gprob * (ex * inv)
    e0 = (gsel * EXPERTS_PER_GROUP + i0).astype(I32)
    e1 = (gsel * EXPERTS_PER_GROUP + i1).astype(I32)

    rowe = lax.broadcasted_iota(I32, (N_EXPERTS, tm), 0)
    is0 = rowe == e0
    is1 = rowe == e1
    oh = jnp.where(is0, 1.0, 0.0) + jnp.where(is1, 1.0, 0.0)
    tr = lax.broadcasted_iota(I32, (tm, tm), 0)
    tc = lax.broadcasted_iota(I32, (tm, tm), 1)
    su = jnp.where(tr < tc, 1.0, 0.0).astype(BF16)
    tot = jnp.dot(oh.astype(BF16), su, preferred_element_type=F32) + carry_ref[...]
    rank0 = jnp.sum(jnp.where(is0, tot, 0.0), axis=0, keepdims=True)
    rank1 = jnp.sum(jnp.where(is1, tot, 0.0), axis=0, keepdims=True)
    carry = carry_ref[...] + jnp.sum(oh, axis=1, keepdims=True)
    carry_ref[...] = carry
    cnt_ref[...] = jnp.broadcast_to(carry, cnt_ref.shape).astype(I32)

    zi = jnp.zeros((1, tm), I32)
    ri_ref[...] = jnp.concatenate([e0, e1, rank0.astype(I32), rank1.astype(I32), zi, zi, zi, zi], axis=0)
    zf = jnp.zeros((1, tm), F32)
    rw_ref[...] = jnp.concatenate([w0, w1, zf, zf, zf, zf, zf, zf], axis=0)


def _mix(h_a, o_b, z_main, x2d, w_a, w_b, w_out, ln_g, ln_b, w_r, b_r):
    T, D = x2d.shape
    tm = min(MIX_TM, T)
    const = lambda shape: pl.BlockSpec(shape, lambda i: (0, 0), pipeline_mode=pl.Buffered(1))
    return pl.pallas_call(
        _mix_kernel,
        grid=(T // tm,),
        in_specs=[
            pl.BlockSpec((tm, A_WIDTH), lambda i: (i, 0)),
            pl.BlockSpec((tm, B_WIDTH), lambda i: (i, 0)),
            pl.BlockSpec((tm, D), lambda i: (i, COL_GA // D)),
            pl.BlockSpec((tm, D), lambda i: (i, COL_GB // D)),
            pl.BlockSpec((tm, D), lambda i: (i, 0)),
            const((A_WIDTH, D)), const((B_WIDTH, D)), const((D, D)),
            const((1, D)), const((1, D)), const((D, 2 * LANES)), const((1, LANES)),
        ],
        out_specs=[
            pl.BlockSpec((tm, D), lambda i: (i, 0)),
            pl.BlockSpec((tm * SUBLANES, LANES), lambda i: (i, 0)),
            pl.BlockSpec((SUBLANES, tm), lambda i: (0, i)),
            pl.BlockSpec((SUBLANES, tm), lambda i: (0, i)),
            pl.BlockSpec((N_EXPERTS, LANES), lambda i: (0, 0)),
        ],
        out_shape=[
            jax.ShapeDtypeStruct((T, D), F32),
            jax.ShapeDtypeStruct((T * SUBLANES, LANES), U32),
            jax.ShapeDtypeStruct((SUBLANES, T), I32),
            jax.ShapeDtypeStruct((SUBLANES, T), F32),
            jax.ShapeDtypeStruct((N_EXPERTS, LANES), I32),
        ],
        scratch_shapes=[pltpu.VMEM((N_EXPERTS, 1), F32)],
        compiler_params=_cparams(("arbitrary",)),
        name="mix_ln1_router",
    )(h_a, o_b, z_main, z_main, x2d, w_a, w_b, w_out, ln_g, ln_b, w_r, b_r)


def _row_copy(src_ref, src_row, dst_ref, dst_row, sem):
    return pltpu.make_async_copy(src_ref.at[pl.ds(src_row * SUBLANES, SUBLANES)],
                                 dst_ref.at[pl.ds(dst_row * SUBLANES, SUBLANES)], sem)


def _rows_wait(src_ref, dst_ref, rows, sem):
    n = rows * SUBLANES
    pltpu.make_async_copy(src_ref.at[pl.ds(0, n)], dst_ref.at[pl.ds(0, n)], sem).wait()


DISPATCH_SLOTS = 3


def _dispatch_kernel(cnt_ref, ri_ref, x1_ref, xpad_ref, pstart_ref, bexp_ref, nused_ref,
                     pst_ref, zero_ref, xbuf_ref, sem, lsem, zsem, *, tb, n_blocks, tm):
    i = pl.program_id(0)
    last = pl.num_programs(0) - 1

    @pl.when(i == 0)
    def _():
        def fill(b, carry):
            bexp_ref[b] = 0
            return carry
        lax.fori_loop(0, n_blocks, fill, 0)

        def per_expert(e, blk):
            pst_ref[e] = blk * tb
            pstart_ref[e] = blk * tb
            nb = (cnt_ref[e] + (tb - 1)) // tb

            def mark(b, carry):
                bexp_ref[blk + b] = e
                return carry
            lax.fori_loop(0, nb, mark, 0)
            return blk + nb
        nused = lax.fori_loop(0, N_EXPERTS, per_expert, 0)
        nused_ref[0] = nused
        pst_ref[N_EXPERTS] = nused

    rows = tm * SUBLANES
    slot = i % DISPATCH_SLOTS

    def tile_load(tile, s):
        src = x1_ref.at[pl.ds(pl.multiple_of(tile * rows, rows), rows)]
        return pltpu.make_async_copy(src, xbuf_ref.at[s], lsem.at[s])

    def wait_rows(s):
        for k in range(TOP_K):
            _rows_wait(xbuf_ref.at[s], xpad_ref, tm, sem.at[s])

    @pl.when(i == 0)
    def _():
        tile_load(0, 0).start()

        @pl.when(last >= 1)
        def _():
            tile_load(1, 1).start()

    tile_load(i, slot).wait()
    for t in range(tm):
        for k in range(TOP_K):
            pos = pst_ref[ri_ref[k, t]] + ri_ref[TOP_K + k, t]
            _row_copy(xbuf_ref.at[slot], t, xpad_ref, pos, sem.at[slot]).start(priority=k)

    prev = (i + DISPATCH_SLOTS - 1) % DISPATCH_SLOTS

    @pl.when(i >= 1)
    def _():
        wait_rows(prev)

    @pl.when(i + 2 <= last)
    def _():
        tile_load(i + 2, prev).start()

    @pl.when(i == last)
    def _():
        wait_rows(slot)
        zero_ref[...] = jnp.zeros_like(zero_ref)

        def per_expert(e, carry):
            cnt = cnt_ref[e]
            first = pst_ref[e] + cnt
            npad = ((cnt + (tb - 1)) // tb) * tb - cnt

            def start(p, c2):
                _row_copy(zero_ref, 0, xpad_ref, first + p, zsem).start()
                return c2
            lax.fori_loop(0, npad, start, 0)

            def wait(p, c2):
                _row_copy(zero_ref, 0, xpad_ref, 0, zsem).wait()
                return c2
            lax.fori_loop(0, npad, wait, 0)
            return carry
        lax.fori_loop(0, N_EXPERTS, per_expert, 0)

        def block_copy(b):
            n = tb * SUBLANES
            return pltpu.make_async_copy(zero_ref, xpad_ref.at[pl.ds(pl.multiple_of(b * n, n), n)], zsem)

        def start_block(b, carry):
            block_copy(b).start()
            return carry
        lax.fori_loop(pst_ref[N_EXPERTS], n_blocks, start_block, 0)

        def wait_block(b, carry):
            block_copy(b).wait()
            return carry
        lax.fori_loop(pst_ref[N_EXPERTS], n_blocks, wait_block, 0)


def _dispatch(counts, route_i, x1, tb, n_blocks):
    T = x1.shape[0] // SUBLANES
    tm = min(DISPATCH_TM, T)
    smem_full = pl.BlockSpec(memory_space=pltpu.SMEM)
    return pl.pallas_call(
        functools.partial(_dispatch_kernel, tb=tb, n_blocks=n_blocks, tm=tm),
        grid=(T // tm,),
        in_specs=[
            smem_full,
            pl.BlockSpec((SUBLANES, tm), lambda i: (0, i), memory_space=pltpu.SMEM),
            pl.BlockSpec(memory_space=pl.ANY),
        ],
        out_specs=[
            pl.BlockSpec(memory_space=pl.ANY),
            smem_full, smem_full, smem_full,
        ],
        out_shape=[
            jax.ShapeDtypeStruct((n_blocks * tb * SUBLANES, LANES), x1.dtype),
            jax.ShapeDtypeStruct((N_EXPERTS,), I32),
            jax.ShapeDtypeStruct((n_blocks,), I32),
            jax.ShapeDtypeStruct((1,), I32),
        ],
        scratch_shapes=[
            pltpu.SMEM((N_EXPERTS + 1,), I32),
            pltpu.VMEM((tb * SUBLANES, LANES), x1.dtype),
            pltpu.VMEM((DISPATCH_SLOTS, tm * SUBLANES, LANES), x1.dtype),
            pltpu.SemaphoreType.DMA((DISPATCH_SLOTS,)),
            pltpu.SemaphoreType.DMA((DISPATCH_SLOTS,)),
            pltpu.SemaphoreType.DMA(()),
        ],
        compiler_params=_cparams(("arbitrary",)),
        name="moe_dispatch",
    )(counts, route_i, x1)


def _expert_kernel(bexp_ref, nused_ref, x_ref, wg_hbm, wu_hbm, wd_hbm, y_ref,
                   wgf_ref, wuf_ref, wdf_ref, wgb_ref, wub_ref, wdb_ref, sem, run_ref):
    b = pl.program_id(0)
    nused = nused_ref[0]
    used = b < nused
    e = bexp_ref[b]
    new_expert = jnp.logical_or(b == 0, e != bexp_ref[jnp.maximum(b - 1, 0)])
    streams = ((wg_hbm, wgf_ref), (wu_hbm, wuf_ref), (wd_hbm, wdf_ref))

    def weight_copies(expert, slot):
        return [pltpu.make_async_copy(w_hbm.at[expert], wf_ref.at[slot], sem.at[slot, j])
                for j, (w_hbm, wf_ref) in enumerate(streams)]

    @pl.when(jnp.logical_and(used, b == 0))
    def _():
        run_ref[0] = 0
        for cp in weight_copies(e, 0):
            cp.start()

    @pl.when(jnp.logical_and(used, new_expert))
    def _():
        slot = run_ref[0] % 2
        nxt = lax.while_loop(lambda j: jnp.logical_and(j < nused, bexp_ref[jnp.minimum(j, nused - 1)] == e),
                             lambda j: j + 1, b + 1)

        @pl.when(nxt < nused)
        def _():
            for cp in weight_copies(bexp_ref[jnp.minimum(nxt, nused - 1)], 1 - slot):
                cp.start()

        for cp in weight_copies(e, slot):
            cp.wait()
        wgb_ref[...] = wgf_ref[slot].astype(BF16)
        wub_ref[...] = wuf_ref[slot].astype(BF16)
        wdb_ref[...] = wdf_ref[slot].astype(BF16)
        run_ref[0] = run_ref[0] + 1

    @pl.when(used)
    def _():
        half = SUBLANES * LANES
        x_lo, x_hi = (t.astype(BF16) for t in _unpack_halves(_tiles_to_rows(x_ref)))
        dot = functools.partial(jnp.dot, preferred_element_type=F32)
        gate = dot(x_lo, wgb_ref[:half, :]) + dot(x_hi, wgb_ref[half:, :])
        up = dot(x_lo, wub_ref[:half, :]) + dot(x_hi, wub_ref[half:, :])
        hid = (gate * _sigmoid(gate) * up).astype(BF16)
        _rows_to_tiles(y_ref, _pack_halves(dot(hid, wdb_ref[...])))

    @pl.when(jnp.logical_not(used))
    def _():
        y_ref[...] = jnp.zeros_like(y_ref)


def _experts(bexp, nused, x_pad, w_gate, w_up, w_down, tb):
    P = x_pad.shape[0] // SUBLANES
    D = 2 * TILE_WORDS
    n_blocks = P // tb
    rows = (tb * SUBLANES, LANES)
    F = w_gate.shape[-1]
    blk = lambda b, be, nu: jnp.maximum(jnp.minimum(b, nu[0] - 1), 0)
    hbm = pl.BlockSpec(memory_space=pl.ANY)
    return pl.pallas_call(
        _expert_kernel,
        grid_spec=pltpu.PrefetchScalarGridSpec(
            num_scalar_prefetch=2,
            grid=(n_blocks,),
            in_specs=[pl.BlockSpec(rows, lambda b, be, nu: (blk(b, be, nu), 0)), hbm, hbm, hbm],
            out_specs=pl.BlockSpec(rows, lambda b, be, nu: (b, 0)),
            scratch_shapes=[
                pltpu.VMEM((2, D, F), F32), pltpu.VMEM((2, D, F), F32), pltpu.VMEM((2, F, D), F32),
                pltpu.VMEM((D, F), BF16), pltpu.VMEM((D, F), BF16), pltpu.VMEM((F, D), BF16),
                pltpu.SemaphoreType.DMA((2, 3)),
                pltpu.SMEM((1,), I32),
            ],
        ),
        out_shape=jax.ShapeDtypeStruct(x_pad.shape, U32),
        compiler_params=_cparams(("arbitrary",)),
        name="moe_experts",
    )(bexp, nused, x_pad, w_gate, w_up, w_down)


COMBINE_SLOTS = 3


def _combine_kernel(pstart_ref, ri_ref, r1_ref, r2_ref, x1_ref, rw_ref, g_ref, b_ref, ypad_ref, o_ref, ybuf_ref, sem):
    i = pl.program_id(0)
    n = pl.num_programs(0)
    tm = x1_ref.shape[0]
    slot = i % COMBINE_SLOTS

    def gather_loop(route_ref, s):
        def issue(t, carry):
            for k in range(TOP_K):
                pos = pstart_ref[route_ref[k, t]] + route_ref[TOP_K + k, t]
                _row_copy(ypad_ref, pos, ybuf_ref.at[s, k], t, sem.at[s]).start(priority=k)
            return carry
        lax.fori_loop(0, tm, issue, 0, unroll=8)

    def wait_slot(s):
        for k in range(TOP_K):
            _rows_wait(ypad_ref, ybuf_ref.at[s, k], tm, sem.at[s])

    @pl.when(i == 0)
    def _():
        gather_loop(ri_ref, 0)
        gather_loop(r1_ref, 1)

    wait_slot(slot)

    rw = rw_ref[...]
    y0 = _unpack_halves(_tiles_to_rows(ybuf_ref.at[slot, 0]))
    y1 = _unpack_halves(_tiles_to_rows(ybuf_ref.at[slot, 1]))
    ffn = jnp.concatenate([y0[h] * rw[:, 0:1] + y1[h] * rw[:, 1:2] for h in range(2)], axis=1)
    pre = DEEPNORM_ALPHA * x1_ref[...] + ffn
    mu = jnp.mean(pre, axis=-1, keepdims=True)
    pc = pre - mu
    var = jnp.mean(pc * pc, axis=-1, keepdims=True)
    o_ref[...] = pc * lax.rsqrt(var + LN_EPS) * g_ref[...] + b_ref[...]

    nxt = (i + 2) % COMBINE_SLOTS
    for t in range(tm):
        for k in range(TOP_K):
            pos = pstart_ref[r2_ref[k, t]] + r2_ref[TOP_K + k, t]
            _row_copy(ypad_ref, pos, ybuf_ref.at[nxt, k], t, sem.at[nxt]).start(priority=k)

    @pl.when(i == n - 1)
    def _():
        wait_slot((i + 1) % COMBINE_SLOTS)
        wait_slot(nxt)


def _combine(pstart, route_i, x1, rw_col, ln_g, ln_b, y_pad):
    T, D = x1.shape
    tm = min(COMBINE_TM, T)
    n = T // tm
    return pl.pallas_call(
        _combine_kernel,
        grid=(n,),
        in_specs=[
            pl.BlockSpec(memory_space=pltpu.SMEM),
            pl.BlockSpec((SUBLANES, tm), lambda i: (0, i), memory_space=pltpu.SMEM),
            pl.BlockSpec((SUBLANES, tm), lambda i: (0, jnp.minimum(i + 1, n - 1)), memory_space=pltpu.SMEM),
            pl.BlockSpec((SUBLANES, tm), lambda i: (0, jnp.minimum(i + 2, n - 1)), memory_space=pltpu.SMEM),
            pl.BlockSpec((tm, D), lambda i: (i, 0)),
            pl.BlockSpec((tm, SUBLANES), lambda i: (i, 0)),
            pl.BlockSpec((1, D), lambda i: (0, 0)),
            pl.BlockSpec((1, D), lambda i: (0, 0)),
            pl.BlockSpec(memory_space=pl.ANY),
        ],
        out_specs=pl.BlockSpec((tm, D), lambda i: (i, 0)),
        out_shape=jax.ShapeDtypeStruct((T, D), F32),
        scratch_shapes=[
            pltpu.VMEM((COMBINE_SLOTS, TOP_K, tm * SUBLANES, LANES), U32),
            pltpu.SemaphoreType.DMA((COMBINE_SLOTS,)),
        ],
        compiler_params=_cparams(("arbitrary",)),
        name="moe_combine_ln2",
    )(pstart, route_i, route_i, route_i, x1, rw_col, ln_g, ln_b, y_pad)


N_GATE_COLS = 2 * A_HEADS
SRC_MLSTM = 0
SRC_GATES = 4 * A_WIDTH
SRC_DIFF = SRC_GATES + N_GATE_COLS
SRC_MERGE = SRC_DIFF + 3 * B_WIDTH
PREP_TN = 512


def _wprep_kernel(a_ref, o_ref):
    o_ref[...] = a_ref[...].T.astype(BF16)


def _wgate_kernel(a_ref, o_ref):
    a = a_ref[...]
    rows = jnp.concatenate([a, jnp.zeros((LANES - a.shape[0], a.shape[1]), F32)], axis=0)
    o_ref[...] = rows.T


def _rearrange_in_proj(w_in, b_in):
    K, n_in = w_in.shape
    tn = PREP_TN
    merge_blocks = (2 * D_MODEL) // tn
    mlstm_blocks = (4 * A_WIDTH) // tn
    assert SRC_MERGE % SUBLANES == 0 and SRC_DIFF % SUBLANES == 0

    def src_row(jb):
        s = SUBLANES
        merge = SRC_MERGE // s + (tn // s) * jb
        mlstm = SRC_MLSTM // s + (tn // s) * (jb - merge_blocks)
        diff = SRC_DIFF // s + (tn // s) * (jb - merge_blocks - mlstm_blocks)
        return s * jnp.where(jb < merge_blocks, merge, jnp.where(jb < merge_blocks + mlstm_blocks, mlstm, diff))

    w_t = jnp.swapaxes(w_in, 0, 1)
    w_main = pl.pallas_call(
        _wprep_kernel,
        grid=(N_MAIN // tn,),
        in_specs=[pl.BlockSpec((pl.Element(tn), pl.Element(K)), lambda jb: (src_row(jb), 0))],
        out_specs=pl.BlockSpec((K, tn), lambda jb: (0, jb)),
        out_shape=jax.ShapeDtypeStruct((K, N_MAIN), BF16),
        compiler_params=_cparams(("parallel",)),
        name="w_in_prep",
    )(w_t)
    b_main = jnp.concatenate([b_in[SRC_MERGE:], b_in[SRC_MLSTM:SRC_GATES], b_in[SRC_DIFF:SRC_MERGE]])[None, :]
    w_gate = pl.pallas_call(
        _wgate_kernel,
        grid=(1,),
        in_specs=[pl.BlockSpec((pl.Element(N_GATE_COLS), pl.Element(K)), lambda i: (SRC_GATES, 0))],
        out_specs=pl.BlockSpec((K, LANES), lambda i: (0, 0)),
        out_shape=jax.ShapeDtypeStruct((K, LANES), F32),
        name="w_gate_prep",
    )(w_t)
    b_gate = jnp.pad(b_in[SRC_GATES:SRC_DIFF], (0, LANES - N_GATE_COLS))[None, :]
    return w_main, b_main, _hi_lo_columns(w_gate), b_gate


def _layer(x, w_in, b_in, conv_w, conv_b, norm_a_g, lq1, lk1, lq2, lk2, norm_b_g, w_a, w_b, w_out,
           ln1_g, ln1_b, w_grp, b_grp, w_exp, b_exp, w_gate, w_up, w_down, ln2_g, ln2_b, lambda_init):
    B, S, D = x.shape
    T = B * S
    x2d = x.reshape(T, D)

    w_main, b_main, w_g, b_g = _rearrange_in_proj(w_in, b_in)
    z_main, z_gate = _in_projection(x2d, w_main, b_main, w_g, b_g)

    L = min(MLSTM_CHUNK, S)
    gp = _gate_prep(z_gate[:, :2 * A_HEADS].T, L)
    h_a = _mlstm(z_main, gp.T, gp, conv_w, conv_b[None, :], norm_a_g[None, :], B, S)
    o_b = _diff_attention(z_main, lq1[None, :], lk1[None, :], lq2[None, :], lk2[None, :],
                          norm_b_g[None, :], B, S, lambda_init)

    w_r = jnp.zeros((D, LANES), F32).at[:, :N_GROUPS].set(w_grp).at[:, SUBLANES:SUBLANES + N_EXPERTS].set(w_exp)
    b_r = jnp.zeros((LANES,), F32).at[:N_GROUPS].set(b_grp).at[SUBLANES:SUBLANES + N_EXPERTS].set(b_exp)[None, :]
    x1, x1_packed, route_i, route_w, counts = _mix(h_a, o_b, z_main, x2d, w_a.astype(BF16), w_b.astype(BF16),
                                                   w_out.astype(BF16), ln1_g[None, :], ln1_b[None, :],
                                                   _hi_lo_columns(w_r), b_r)

    tb = MOE_TB
    n_blocks = (T * TOP_K) // tb + N_EXPERTS
    x_pad, pstart, bexp, nused = _dispatch(counts[:, 0], route_i, x1_packed, tb, n_blocks)
    y_pad = _experts(bexp, nused, x_pad, w_gate, w_up, w_down, tb)
    out = _combine(pstart, route_i, x1, route_w.T, ln2_g[None, :], ln2_b[None, :], y_pad)
    return out.reshape(B, S, D)


def kernel(x, w_in, b_in, conv_w, conv_b, mlstm_norm_g, lambda_q1, lambda_k1, lambda_q2, lambda_k2,
           diff_norm_g, w_a, w_b, w_out, ln1_g, ln1_b, w_grp, b_grp, w_exp, b_exp,
           w_gate, w_up, w_down, ln2_g, ln2_b):
    for l in range(DEPTH):
        lambda_init = 0.8 - 0.6 * math.exp(-0.3 * l)
        x = _layer(x, w_in[l], b_in[l], conv_w[l], conv_b[l], mlstm_norm_g[l], lambda_q1[l], lambda_k1[l],
                   lambda_q2[l], lambda_k2[l], diff_norm_g[l], w_a[l], w_b[l], w_out[l], ln1_g[l], ln1_b[l],
                   w_grp[l], b_grp[l], w_exp[l], b_exp[l], w_gate[l], w_up[l], w_down[l], ln2_g[l], ln2_b[l],
                   lambda_init)
    return x
```

```python
import functools
import math

import jax
import jax.numpy as jnp
from jax import lax
from jax.experimental import pallas as pl
from jax.experimental.pallas import tpu as pltpu

F32 = jnp.float32
BF16 = jnp.bfloat16
I32 = jnp.int32
U32 = jnp.uint32

D_MODEL = 2048
A_HEADS = 4
A_HEAD_DIM = 256
A_WIDTH = A_HEADS * A_HEAD_DIM
CONV_WIDTH = 4
B_HEADS = 8
B_HEAD_DIM = 64
B_V_DIM = 2 * B_HEAD_DIM
B_WIDTH = B_HEADS * B_V_DIM
ATTN_CHUNK = 64
N_GROUPS = 4
EXPERTS_PER_GROUP = 8
N_EXPERTS = N_GROUPS * EXPERTS_PER_GROUP
TOP_K = 2
D_EXPERT = 512
DEPTH = 1
DEEPNORM_ALPHA = (2 * DEPTH) ** 0.25
LN_EPS = 1e-5
NEG_INF = float("-inf")
LOG2_E = 1.4426950408889634

LANES = 128
SUBLANES = 8
VMEM_LIMIT_BYTES = 56 * 1024 * 1024

COL_GA = 0
COL_GB = COL_GA + D_MODEL
COL_QA = COL_GB + D_MODEL
COL_KA = COL_QA + A_WIDTH
COL_VA = COL_KA + A_WIDTH
COL_OA = COL_VA + A_WIDTH
COL_QB = COL_OA + A_WIDTH
COL_KB = COL_QB + B_WIDTH
COL_VB = COL_KB + B_WIDTH
N_MAIN = COL_VB + B_WIDTH

PROJ_TM, PROJ_TN = 1024, 1024
MLSTM_CHUNK = 512
ATTN_TQ = 512
MIX_TM = 256
MOE_TB = 256
DISPATCH_TM = 256
COMBINE_TM = 256


def _cparams(sem, vmem=VMEM_LIMIT_BYTES):
    return pltpu.CompilerParams(dimension_semantics=sem, vmem_limit_bytes=vmem)


def _sigmoid(x):
    return 1.0 / (1.0 + jnp.exp(-x))


def _pack_halves(x):
    n = x.shape[1] // 2
    lo = lax.bitcast_convert_type(x[:, :n].astype(BF16).astype(F32), U32)
    hi = lax.bitcast_convert_type(x[:, n:].astype(BF16).astype(F32), U32)
    return (lo >> 16) | hi


def _unpack_halves(u):
    lo = lax.bitcast_convert_type(u << 16, F32)
    hi = lax.bitcast_convert_type(u & jnp.uint32(0xFFFF0000), F32)
    return lo, hi


TILE_WORDS = SUBLANES * LANES


def _rows_to_tiles(ref, v):
    rows = v.shape[0]
    for j in range(SUBLANES):
        ref[pl.ds(j, rows, stride=SUBLANES), :] = v[:, j * LANES:(j + 1) * LANES]


def _tiles_to_rows(ref):
    rows = ref.shape[0] // SUBLANES
    return jnp.concatenate([ref[pl.ds(j, rows, stride=SUBLANES), :] for j in range(SUBLANES)], axis=1)


def _split_bf16(a):
    hi = a.astype(BF16)
    return hi, (a - hi.astype(F32)).astype(BF16)


def _hi_lo_columns(w):
    return jnp.concatenate(_split_bf16(w), axis=1)


def _dot_3pass(a, w_hl):
    a_hi, a_lo = _split_bf16(a)
    n = w_hl.shape[1] // 2
    dot = functools.partial(jnp.dot, preferred_element_type=F32)
    r = dot(a_hi, w_hl)
    return r[:, :n] + (r[:, n:] + dot(a_lo, w_hl[:, :n]))


def _proj_kernel(x_ref, w_ref, b_ref, wg_ref, bg_ref, z_ref, zg_ref, xb_ref):
    @pl.when(pl.program_id(1) == 0)
    def _():
        x = x_ref[...]
        xb_ref[...] = x.astype(BF16)
        zg_ref[...] = _dot_3pass(x, wg_ref[...]) + bg_ref[...]

    acc = jnp.dot(xb_ref[...], w_ref[...], preferred_element_type=F32)
    z_ref[...] = (acc + b_ref[...]).astype(z_ref.dtype)


def _in_projection(x2d, w_main, b_main, w_gate, b_gate):
    T, K = x2d.shape
    N = w_main.shape[1]
    tm, tn = min(PROJ_TM, T), PROJ_TN
    return pl.pallas_call(
        _proj_kernel,
        grid=(T // tm, N // tn),
        in_specs=[
            pl.BlockSpec((tm, K), lambda i, j: (i, 0)),
            pl.BlockSpec((K, tn), lambda i, j: (0, j)),
            pl.BlockSpec((1, tn), lambda i, j: (0, j)),
            pl.BlockSpec((K, 2 * LANES), lambda i, j: (0, 0)),
            pl.BlockSpec((1, LANES), lambda i, j: (0, 0)),
        ],
        out_specs=[
            pl.BlockSpec((tm, tn), lambda i, j: (i, j)),
            pl.BlockSpec((tm, LANES), lambda i, j: (i, 0)),
        ],
        out_shape=[
            jax.ShapeDtypeStruct((T, N), BF16),
            jax.ShapeDtypeStruct((T, LANES), F32),
        ],
        scratch_shapes=[pltpu.VMEM((tm, K), BF16)],
        compiler_params=_cparams(("parallel", "arbitrary")),
        name="in_projection",
    )(x2d, w_main, b_main, w_gate, b_gate)


def _gate_prep_kernel(g_ref, o_ref, *, L):
    r = lax.broadcasted_iota(I32, (L, L), 0)
    c = lax.broadcasted_iota(I32, (L, L), 1)
    tri = (r <= c).astype(F32)
    row = lax.broadcasted_iota(I32, (g_ref.shape[0], L), 0)
    for ch in range(g_ref.shape[1] // L):
        g = g_ref[:, ch * L:(ch + 1) * L]
        lf = jnp.minimum(g, 0.0) - jnp.log(1.0 + jnp.exp(-jnp.abs(g)))
        b = jnp.dot(lf, tri, preferred_element_type=F32, precision=lax.Precision.HIGHEST)
        o_ref[:, ch * L:(ch + 1) * L] = jnp.where(row < A_HEADS, g, b)


def _gate_prep(g_rows, L):
    R, T = g_rows.shape
    per_step = math.gcd(T // L, 8)
    W = L * per_step
    return pl.pallas_call(
        functools.partial(_gate_prep_kernel, L=L),
        grid=(T // W,),
        in_specs=[pl.BlockSpec((R, W), lambda i: (0, i))],
        out_specs=pl.BlockSpec((R, W), lambda i: (0, i)),
        out_shape=jax.ShapeDtypeStruct((R, T), F32),
        compiler_params=_cparams(("parallel",)),
        name="gate_prep",
    )(g_rows)


def _mlstm_kernel(q_ref, k_ref, v_ref, o_ref, gcol_ref, grow_ref, cw_ref, cb_ref, ng_ref, out_ref,
                  c_ref, n_ref, m_ref, qcar_ref, kcar_ref):
    ci = pl.program_id(1)
    L = q_ref.shape[0]
    dh = A_HEAD_DIM

    @pl.when(ci == 0)
    def _():
        c_ref[...] = jnp.zeros_like(c_ref)
        n_ref[...] = jnp.zeros_like(n_ref)
        m_ref[...] = jnp.zeros_like(m_ref)
        qcar_ref[...] = jnp.zeros_like(qcar_ref)
        kcar_ref[...] = jnp.zeros_like(kcar_ref)

    def conv_silu(u_ref, car_ref, lo, wcol):
        u = u_ref[:, lo:lo + dh].astype(F32)
        ext = jnp.concatenate([car_ref[:, lo:lo + dh], u], axis=0)
        w = cw_ref[:, wcol:wcol + dh]
        y = cb_ref[:, wcol:wcol + dh] + w[3:4, :] * u
        for j in range(CONV_WIDTH - 1):
            off = SUBLANES - (CONV_WIDTH - 1) + j
            y = y + w[j:j + 1, :] * ext[off:off + L, :]
        car_ref[:, lo:lo + dh] = u[L - SUBLANES:, :]
        return y * _sigmoid(y)

    r = lax.broadcasted_iota(I32, (L, L), 0)
    c = lax.broadcasted_iota(I32, (L, L), 1)
    causal = r >= c
    gcol = gcol_ref[...]
    grow = grow_ref[...]

    for h in range(A_HEADS):
        lo = h * dh
        q = conv_silu(q_ref, qcar_ref, lo, lo)
        k = conv_silu(k_ref, kcar_ref, lo, A_WIDTH + lo) * (dh ** -0.5)
        v_b = v_ref[:, lo:lo + dh]
        q_b = q.astype(BF16)
        k_b = k.astype(BF16)

        i_col, b_col = gcol[:, h:h + 1], gcol[:, A_HEADS + h:A_HEADS + h + 1]
        i_row, b_row = grow[h:h + 1, :], grow[A_HEADS + h:A_HEADS + h + 1, :]
        b_last = b_row[:, L - 1:L]
        m_prev = m_ref[h]

        dmat = jnp.where(causal, b_col - b_row + i_row, NEG_INF)
        inter = b_col + m_prev
        m_t = jnp.maximum(inter, jnp.max(dmat, axis=-1, keepdims=True))
        s = lax.dot_general(q_b, k_b, (((1,), (1,)), ((), ())), preferred_element_type=F32)
        w_intra = jnp.exp(dmat - m_t) * s
        w_inter = jnp.exp(inter - m_t)
        qc = jnp.dot(q_b, c_ref[h].astype(BF16), preferred_element_type=F32)
        num = w_inter * qc + jnp.dot(w_intra.astype(BF16), v_b, preferred_element_type=F32)
        qn = jnp.sum(q * n_ref[h], axis=-1, keepdims=True)
        den = w_inter * qn + jnp.sum(w_intra, axis=-1, keepdims=True)
        hh = num / jnp.maximum(jnp.abs(den), jnp.exp(-m_t))

        g_col = b_last - b_col + i_col
        m_new = jnp.maximum(b_last + m_prev, jnp.max(g_col, axis=0, keepdims=True))
        decay = jnp.exp(b_last + m_prev - m_new)
        kw = jnp.exp(g_col - m_new) * k
        c_ref[h] = decay * c_ref[h] + jnp.dot(kw.T.astype(BF16), v_b, preferred_element_type=F32)
        n_ref[h] = decay * n_ref[h] + jnp.sum(kw, axis=0, keepdims=True)
        m_ref[h] = m_new

        y = _sigmoid(o_ref[:, lo:lo + dh].astype(F32)) * hh
        mu = jnp.mean(y, axis=-1, keepdims=True)
        yc = y - mu
        var = jnp.mean(yc * yc, axis=-1, keepdims=True)
        out_ref[:, lo:lo + dh] = (yc * lax.rsqrt(var + LN_EPS) * ng_ref[:, lo:lo + dh]).astype(out_ref.dtype)


def _mlstm(z_main, gcol, grow, conv_w, conv_b, norm_g, B, S):
    L = min(MLSTM_CHUNK, S)
    nc = S // L
    dh = A_HEAD_DIM
    H = A_HEADS
    W = A_WIDTH
    row = lambda b, c: b * nc + c
    full = lambda shape: pl.BlockSpec(shape, lambda b, c: (0, 0))
    return pl.pallas_call(
        _mlstm_kernel,
        grid=(B, nc),
        in_specs=[
            pl.BlockSpec((L, W), lambda b, c: (row(b, c), COL_QA // W)),
            pl.BlockSpec((L, W), lambda b, c: (row(b, c), COL_KA // W)),
            pl.BlockSpec((L, W), lambda b, c: (row(b, c), COL_VA // W)),
            pl.BlockSpec((L, W), lambda b, c: (row(b, c), COL_OA // W)),
            pl.BlockSpec((L, SUBLANES), lambda b, c: (row(b, c), 0)),
            pl.BlockSpec((SUBLANES, L), lambda b, c: (0, row(b, c))),
            full((CONV_WIDTH, 2 * W)), full((1, 2 * W)), full((1, W)),
        ],
        out_specs=pl.BlockSpec((L, W), lambda b, c: (row(b, c), 0)),
        out_shape=jax.ShapeDtypeStruct((B * S, W), BF16),
        scratch_shapes=[
            pltpu.VMEM((H, dh, dh), F32),
            pltpu.VMEM((H, 1, dh), F32),
            pltpu.VMEM((H, 1, 1), F32),
            pltpu.VMEM((SUBLANES, W), F32),
            pltpu.VMEM((SUBLANES, W), F32),
        ],
        compiler_params=_cparams(("parallel", "arbitrary")),
        name="mlstm",
    )(z_main, z_main, z_main, z_main, gcol, grow, conv_w, conv_b, norm_g)


def _lane_tiles(t):
    return [t[:, i * LANES:(i + 1) * LANES] for i in range(t.shape[1] // LANES)]


def _attn_kernel(slope_ref, q_ref, k_ref, kb_ref, v_ref, lq1_ref, lk1_ref, lq2_ref, lk2_ref, g_ref, o_ref,
                 s_ref, mx_ref, ls_ref, acc_ref, corr_ref, *, lambda_init):
    h = pl.program_id(1)
    qi = pl.program_id(2)
    tq = q_ref.shape[0]
    tk = tq
    slope = slope_ref[h]
    nt = (((1,), (1,)), ((), ()))

    lane = lax.broadcasted_iota(I32, (tq, B_V_DIM), 1)
    q = q_ref[...] * (B_HEAD_DIM ** -0.5)
    zero = jnp.zeros_like(q)
    bias_cols = jnp.where(lane < 2, 1.0, 0.0).astype(BF16)
    qa = (jnp.concatenate([jnp.where(lane < B_HEAD_DIM, q, zero), bias_cols], axis=1),
          jnp.concatenate([jnp.where(lane >= B_HEAD_DIM, q, zero), bias_cols], axis=1))

    def scores(j):
        start = pl.multiple_of(j * tk, tk)
        ka = jnp.concatenate([k_ref[pl.ds(start, tk), :], kb_ref[pl.ds(start, tk), :]], axis=1)
        return [lax.dot_general(qa[n], ka, nt, preferred_element_type=F32) for n in range(2)]

    def keep(j, n, t):
        t = t * LOG2_E
        s_ref[n, j] = t
        mx_ref[n] = functools.reduce(jnp.maximum, _lane_tiles(t), mx_ref[n])

    mx_ref[...] = jnp.full(mx_ref.shape, NEG_INF, F32)

    def pass_a(j):
        t = scores(j)
        for n in range(2):
            keep(j, n, t[n])

    def pass_a_pair(jj, carry):
        pass_a(2 * jj)
        pass_a(2 * jj + 1)
        return carry
    lax.fori_loop(0, qi // 2, pass_a_pair, 0)

    @pl.when(qi % 2 == 1)
    def _():
        pass_a(qi - 1)

    @pl.when(qi == 0)
    def _():
        r = lax.broadcasted_iota(I32, (tq, tk), 0)
        c = lax.broadcasted_iota(I32, (tq, tk), 1)
        ahead = jnp.where(c <= r, 0.0, (r - c).astype(F32) * (2.0 * slope))
        corr_ref[...] = jnp.where((c // ATTN_CHUNK) <= (r // ATTN_CHUNK), ahead, NEG_INF)

    hq = tq // 2
    dstart = pl.multiple_of(qi * tk, tk)
    ka = jnp.concatenate([k_ref[pl.ds(dstart, tk), :], kb_ref[pl.ds(dstart, tk), :]], axis=1)
    masked = jnp.full((hq, tk - hq), NEG_INF, F32)
    for n in range(2):
        top = lax.dot_general(qa[n][:hq], ka[:hq], nt, preferred_element_type=F32)
        bot = lax.dot_general(qa[n][hq:], ka, nt, preferred_element_type=F32)
        t = jnp.concatenate([jnp.concatenate([top, masked], axis=1), bot], axis=0)
        keep(qi, n, t + corr_ref[...])

    for n in range(2):
        m = jnp.max(mx_ref[n], axis=-1, keepdims=True)
        mx_ref[n] = jnp.broadcast_to(m, (tq, LANES))
    ls_ref[...] = jnp.zeros_like(ls_ref)
    acc_ref[...] = jnp.zeros_like(acc_ref)

    def weigh(n, rows, s_tile, v_tile):
        mb = mx_ref[n, rows, :]
        ps = [jnp.exp2(t - mb) for t in _lane_tiles(s_tile)]
        ls_ref[n, rows, :] += functools.reduce(jnp.add, ps)
        p = jnp.concatenate(ps, axis=1).astype(BF16)
        acc_ref[n, rows, :] += jnp.dot(p, v_tile, preferred_element_type=F32)

    def pass_b(j):
        start = pl.multiple_of(j * tk, tk)
        vj = v_ref[pl.ds(start, tk), :]
        for n in range(2):
            weigh(n, slice(None), s_ref[n, j], vj)

    def pass_b_pair(jj, carry):
        pass_b(2 * jj)
        pass_b(2 * jj + 1)
        return carry
    lax.fori_loop(0, qi // 2, pass_b_pair, 0)

    @pl.when(qi % 2 == 1)
    def _():
        pass_b(qi - 1)

    for n in range(2):
        weigh(n, slice(0, hq), s_ref[n, qi, :hq, :hq], v_ref[pl.ds(dstart, hq), :])
        weigh(n, slice(hq, tq), s_ref[n, qi, hq:, :], v_ref[pl.ds(dstart, tk), :])

    lam = (jnp.exp(jnp.sum(lq1_ref[...] * lk1_ref[...], axis=-1, keepdims=True))
           - jnp.exp(jnp.sum(lq2_ref[...] * lk2_ref[...], axis=-1, keepdims=True)) + lambda_init)
    l0 = jnp.sum(ls_ref[0], axis=-1, keepdims=True)
    l1 = jnp.sum(ls_ref[1], axis=-1, keepdims=True)
    o = acc_ref[0] / l0 - lam * (acc_ref[1] / l1)
    ms = jnp.mean(o * o, axis=-1, keepdims=True)
    o_ref[...] = (o * lax.rsqrt(ms + LN_EPS) * g_ref[...] * (1.0 - lambda_init)).astype(o_ref.dtype)


def _diff_attention(z_main, lq1, lk1, lq2, lk2, norm_g, B, S, lambda_init):
    tq = min(ATTN_TQ, S)
    nq = S // tq
    H = B_HEADS
    dv = B_V_DIM
    slopes = 2.0 ** (-8.0 * jnp.arange(1, H + 1, dtype=F32) / H)
    assert S <= 256 * 256
    pos = lax.broadcasted_iota(I32, (H, S, dv), 1)
    col = lax.broadcasted_iota(I32, (H, S, dv), 2)
    within = pos % 256
    kbias = jnp.where(col == 0, within, jnp.where(col == 1, pos - within, 0)).astype(F32)
    kbias = (kbias * slopes[:, None, None]).astype(BF16)
    small = pl.BlockSpec((1, B_HEAD_DIM), lambda b, h, i: (0, 0))
    return pl.pallas_call(
        functools.partial(_attn_kernel, lambda_init=lambda_init),
        grid=(B, H, nq),
        in_specs=[
            pl.BlockSpec(memory_space=pltpu.SMEM),
            pl.BlockSpec((tq, dv), lambda b, h, i: (b * nq + i, COL_QB // dv + h)),
            pl.BlockSpec((S, dv), lambda b, h, i: (b, COL_KB // dv + h)),
            pl.BlockSpec((None, S, dv), lambda b, h, i: (h, 0, 0)),
            pl.BlockSpec((S, dv), lambda b, h, i: (b, COL_VB // dv + h)),
            small, small, small, small,
            pl.BlockSpec((1, dv), lambda b, h, i: (0, 0)),
        ],
        out_specs=pl.BlockSpec((tq, dv), lambda b, h, i: (b * nq + i, h)),
        out_shape=jax.ShapeDtypeStruct((B * S, B_WIDTH), BF16),
        scratch_shapes=[
            pltpu.VMEM((2, nq, tq, tq), F32),
            pltpu.VMEM((2, tq, LANES), F32),
            pltpu.VMEM((2, tq, LANES), F32),
            pltpu.VMEM((2, tq, dv), F32),
            pltpu.VMEM((tq, tq), F32),
        ],
        compiler_params=_cparams(("parallel", "parallel", "arbitrary")),
        name="diff_attention",
    )(slopes, z_main, z_main, kbias, z_main, lq1, lk1, lq2, lk2, norm_g)


def _mix_kernel(ha_ref, ob_ref, ga_ref, gb_ref, x_ref, wa_ref, wb_ref, wo_ref, g1_ref, b1_ref, wr_ref, br_ref,
                x1_ref, x1p_ref, ri_ref, rw_ref, cnt_ref, carry_ref):
    i = pl.program_id(0)
    tm = x_ref.shape[0]

    @pl.when(i == 0)
    def _():
        carry_ref[...] = jnp.zeros_like(carry_ref)

    ya = jnp.dot(ha_ref[...], wa_ref[...], preferred_element_type=F32)
    yb = jnp.dot(ob_ref[...], wb_ref[...], preferred_element_type=F32)
    merged = _sigmoid(ga_ref[...].astype(F32)) * ya + _sigmoid(gb_ref[...].astype(F32)) * yb
    mix = jnp.dot(merged.astype(BF16), wo_ref[...], preferred_element_type=F32)
    pre = DEEPNORM_ALPHA * x_ref[...] + mix
    mu = jnp.mean(pre, axis=-1, keepdims=True)
    pc = pre - mu
    var = jnp.mean(pc * pc, axis=-1, keepdims=True)
    x1 = pc * lax.rsqrt(var + LN_EPS) * g1_ref[...] + b1_ref[...]
    x1_ref[...] = x1
    _rows_to_tiles(x1p_ref, _pack_halves(x1))

    logits = _dot_3pass(x1, wr_ref[...]) + br_ref[...]
    lt = logits.T
    row8 = lax.broadcasted_iota(I32, (SUBLANES, tm), 0).astype(F32)
    gl = jnp.where(row8 < N_GROUPS, lt[0:SUBLANES, :], NEG_INF)
    gmax = jnp.max(gl, axis=0, keepdims=True)
    gsel = jnp.min(jnp.where(gl == gmax, row8, float(SUBLANES)), axis=0, keepdims=True)
    gprob = 1.0 / jnp.sum(jnp.exp(gl - gmax), axis=0, keepdims=True)
    ing = lt[SUBLANES:2 * SUBLANES, :]
    for g in range(1, N_GROUPS):
        ing = jnp.where(gsel == g, lt[(g + 1) * SUBLANES:(g + 2) * SUBLANES, :], ing)
    v0 = jnp.max(ing, axis=0, keepdims=True)
    i0 = jnp.min(jnp.where(ing == v0, row8, float(SUBLANES)), axis=0, keepdims=True)
    ing2 = jnp.where(row8 == i0, -jnp.inf, ing)
    v1 = jnp.max(ing2, axis=0, keepdims=True)
    i1 = jnp.min(jnp.where(ing2 == v1, row8, float(SUBLANES)), axis=0, keepdims=True)
    ex = jnp.exp(v1 - v0)
    inv = 1.0 / (1.0 + ex)
    w0 = gprob * inv
    w1 = gprob * (ex * inv)
    e0 = (gsel * EXPERTS_PER_GROUP + i0).astype(I32)
    e1 = (gsel * EXPERTS_PER_GROUP + i1).astype(I32)

    rowe = lax.broadcasted_iota(I32, (N_EXPERTS, tm), 0)
    is0 = rowe == e0
    is1 = rowe == e1
    oh = jnp.where(is0, 1.0, 0.0) + jnp.where(is1, 1.0, 0.0)
    tr = lax.broadcasted_iota(I32, (tm, tm), 0)
    tc = lax.broadcasted_iota(I32, (tm, tm), 1)
    su = jnp.where(tr < tc, 1.0, 0.0).astype(BF16)
    tot = jnp.dot(oh.astype(BF16), su, preferred_element_type=F32) + carry_ref[...]
    rank0 = jnp.sum(jnp.where(is0, tot, 0.0), axis=0, keepdims=True)
    rank1 = jnp.sum(jnp.where(is1, tot, 0.0), axis=0, keepdims=True)
    carry = carry_ref[...] + jnp.sum(oh, axis=1, keepdims=True)
    carry_ref[...] = carry
    cnt_ref[...] = jnp.broadcast_to(carry, cnt_ref.shape).astype(I32)

    zi = jnp.zeros((1, tm), I32)
    ri_ref[...] = jnp.concatenate([e0, e1, rank0.astype(I32), rank1.astype(I32), zi, zi, zi, zi], axis=0)
    zf = jnp.zeros((1, tm), F32)
    rw_ref[...] = jnp.concatenate([w0, w1, zf, zf, zf, zf, zf, zf], axis=0)


def _mix(h_a, o_b, z_main, x2d, w_a, w_b, w_out, ln_g, ln_b, w_r, b_r):
    T, D = x2d.shape
    tm = min(MIX_TM, T)
    const = lambda shape: pl.BlockSpec(shape, lambda i: (0, 0), pipeline_mode=pl.Buffered(1))
    return pl.pallas_call(
        _mix_kernel,
        grid=(T // tm,),
        in_specs=[
            pl.BlockSpec((tm, A_WIDTH), lambda i: (i, 0)),
            pl.BlockSpec((tm, B_WIDTH), lambda i: (i, 0)),
            pl.BlockSpec((tm, D), lambda i: (i, COL_GA // D)),
            pl.BlockSpec((tm, D), lambda i: (i, COL_GB // D)),
            pl.BlockSpec((tm, D), lambda i: (i, 0)),
            const((A_WIDTH, D)), const((B_WIDTH, D)), const((D, D)),
            const((1, D)), const((1, D)), const((D, 2 * LANES)), const((1, LANES)),
        ],
        out_specs=[
            pl.BlockSpec((tm, D), lambda i: (i, 0)),
            pl.BlockSpec((tm * SUBLANES, LANES), lambda i: (i, 0)),
            pl.BlockSpec((SUBLANES, tm), lambda i: (0, i)),
            pl.BlockSpec((SUBLANES, tm), lambda i: (0, i)),
            pl.BlockSpec((N_EXPERTS, LANES), lambda i: (0, 0)),
        ],
        out_shape=[
            jax.ShapeDtypeStruct((T, D), F32),
            jax.ShapeDtypeStruct((T * SUBLANES, LANES), U32),
            jax.ShapeDtypeStruct((SUBLANES, T), I32),
            jax.ShapeDtypeStruct((SUBLANES, T), F32),
            jax.ShapeDtypeStruct((N_EXPERTS, LANES), I32),
        ],
        scratch_shapes=[pltpu.VMEM((N_EXPERTS, 1), F32)],
        compiler_params=_cparams(("arbitrary",)),
        name="mix_ln1_router",
    )(h_a, o_b, z_main, z_main, x2d, w_a, w_b, w_out, ln_g, ln_b, w_r, b_r)


def _row_copy(src_ref, src_row, dst_ref, dst_row, sem):
    return pltpu.make_async_copy(src_ref.at[pl.ds(src_row * SUBLANES, SUBLANES)],
                                 dst_ref.at[pl.ds(dst_row * SUBLANES, SUBLANES)], sem)


def _rows_wait(src_ref, dst_ref, rows, sem):
    n = rows * SUBLANES
    pltpu.make_async_copy(src_ref.at[pl.ds(0, n)], dst_ref.at[pl.ds(0, n)], sem).wait()


DISPATCH_SLOTS = 3


def _dispatch_kernel(cnt_ref, ri_ref, x1_ref, xpad_ref, pstart_ref, bexp_ref, nused_ref,
                     pst_ref, zero_ref, xbuf_ref, sem, lsem, zsem, *, tb, n_blocks, tm):
    i = pl.program_id(0)
    last = pl.num_programs(0) - 1

    @pl.when(i == 0)
    def _():
        def fill(b, carry):
            bexp_ref[b] = 0
            return carry
        lax.fori_loop(0, n_blocks, fill, 0)

        def per_expert(e, blk):
            pst_ref[e] = blk * tb
            pstart_ref[e] = blk * tb
            nb = (cnt_ref[e] + (tb - 1)) // tb

            def mark(b, carry):
                bexp_ref[blk + b] = e
                return carry
            lax.fori_loop(0, nb, mark, 0)
            return blk + nb
        nused = lax.fori_loop(0, N_EXPERTS, per_expert, 0)
        nused_ref[0] = nused
        pst_ref[N_EXPERTS] = nused

    rows = tm * SUBLANES
    slot = i % DISPATCH_SLOTS

    def tile_load(tile, s):
        src = x1_ref.at[pl.ds(pl.multiple_of(tile * rows, rows), rows)]
        return pltpu.make_async_copy(src, xbuf_ref.at[s], lsem.at[s])

    def wait_rows(s):
        for k in range(TOP_K):
            _rows_wait(xbuf_ref.at[s], xpad_ref, tm, sem.at[s])

    @pl.when(i == 0)
    def _():
        tile_load(0, 0).start()

        @pl.when(last >= 1)
        def _():
            tile_load(1, 1).start()

    tile_load(i, slot).wait()
    for t in range(tm):
        for k in range(TOP_K):
            pos = pst_ref[ri_ref[k, t]] + ri_ref[TOP_K + k, t]
            _row_copy(xbuf_ref.at[slot], t, xpad_ref, pos, sem.at[slot]).start(priority=k)

    prev = (i + DISPATCH_SLOTS - 1) % DISPATCH_SLOTS

    @pl.when(i >= 1)
    def _():
        wait_rows(prev)

    @pl.when(i + 2 <= last)
    def _():
        tile_load(i + 2, prev).start()

    @pl.when(i == last)
    def _():
        wait_rows(slot)
        zero_ref[...] = jnp.zeros_like(zero_ref)

        def per_expert(e, carry):
            cnt = cnt_ref[e]
            first = pst_ref[e] + cnt
            npad = ((cnt + (tb - 1)) // tb) * tb - cnt

            def start(p, c2):
                _row_copy(zero_ref, 0, xpad_ref, first + p, zsem).start()
                return c2
            lax.fori_loop(0, npad, start, 0)

            def wait(p, c2):
                _row_copy(zero_ref, 0, xpad_ref, 0, zsem).wait()
                return c2
            lax.fori_loop(0, npad, wait, 0)
            return carry
        lax.fori_loop(0, N_EXPERTS, per_expert, 0)

        def block_copy(b):
            n = tb * SUBLANES
            return pltpu.make_async_copy(zero_ref, xpad_ref.at[pl.ds(pl.multiple_of(b * n, n), n)], zsem)

        def start_block(b, carry):
            block_copy(b).start()
            return carry
        lax.fori_loop(pst_ref[N_EXPERTS], n_blocks, start_block, 0)

        def wait_block(b, carry):
            block_copy(b).wait()
            return carry
        lax.fori_loop(pst_ref[N_EXPERTS], n_blocks, wait_block, 0)


def _dispatch(counts, route_i, x1, tb, n_blocks):
    T = x1.shape[0] // SUBLANES
    tm = min(DISPATCH_TM, T)
    smem_full = pl.BlockSpec(memory_space=pltpu.SMEM)
    return pl.pallas_call(
        functools.partial(_dispatch_kernel, tb=tb, n_blocks=n_blocks, tm=tm),
        grid=(T // tm,),
        in_specs=[
            smem_full,
            pl.BlockSpec((SUBLANES, tm), lambda i: (0, i), memory_space=pltpu.SMEM),
            pl.BlockSpec(memory_space=pl.ANY),
        ],
        out_specs=[
            pl.BlockSpec(memory_space=pl.ANY),
            smem_full, smem_full, smem_full,
        ],
        out_shape=[
            jax.ShapeDtypeStruct((n_blocks * tb * SUBLANES, LANES), x1.dtype),
            jax.ShapeDtypeStruct((N_EXPERTS,), I32),
            jax.ShapeDtypeStruct((n_blocks,), I32),
            jax.ShapeDtypeStruct((1,), I32),
        ],
        scratch_shapes=[
            pltpu.SMEM((N_EXPERTS + 1,), I32),
            pltpu.VMEM((tb * SUBLANES, LANES), x1.dtype),
            pltpu.VMEM((DISPATCH_SLOTS, tm * SUBLANES, LANES), x1.dtype),
            pltpu.SemaphoreType.DMA((DISPATCH_SLOTS,)),
            pltpu.SemaphoreType.DMA((DISPATCH_SLOTS,)),
            pltpu.SemaphoreType.DMA(()),
        ],
        compiler_params=_cparams(("arbitrary",)),
        name="moe_dispatch",
    )(counts, route_i, x1)


def _expert_kernel(bexp_ref, nused_ref, x_ref, wg_hbm, wu_hbm, wd_hbm, y_ref,
                   wgf_ref, wuf_ref, wdf_ref, wgb_ref, wub_ref, wdb_ref, sem, run_ref):
    b = pl.program_id(0)
    nused = nused_ref[0]
    used = b < nused
    e = bexp_ref[b]
    new_expert = jnp.logical_or(b == 0, e != bexp_ref[jnp.maximum(b - 1, 0)])
    streams = ((wg_hbm, wgf_ref), (wu_hbm, wuf_ref), (wd_hbm, wdf_ref))

    def weight_copies(expert, slot):
        return [pltpu.make_async_copy(w_hbm.at[expert], wf_ref.at[slot], sem.at[slot, j])
                for j, (w_hbm, wf_ref) in enumerate(streams)]

    @pl.when(jnp.logical_and(used, b == 0))
    def _():
        run_ref[0] = 0
        for cp in weight_copies(e, 0):
            cp.start()

    @pl.when(jnp.logical_and(used, new_expert))
    def _():
        slot = run_ref[0] % 2
        nxt = lax.while_loop(lambda j: jnp.logical_and(j < nused, bexp_ref[jnp.minimum(j, nused - 1)] == e),
                             lambda j: j + 1, b + 1)

        @pl.when(nxt < nused)
        def _():
            for cp in weight_copies(bexp_ref[jnp.minimum(nxt, nused - 1)], 1 - slot):
                cp.start()

        for cp in weight_copies(e, slot):
            cp.wait()
        wgb_ref[...] = wgf_ref[slot].astype(BF16)
        wub_ref[...] = wuf_ref[slot].astype(BF16)
        wdb_ref[...] = wdf_ref[slot].astype(BF16)
        run_ref[0] = run_ref[0] + 1

    @pl.when(used)
    def _():
        half = SUBLANES * LANES
        x_lo, x_hi = (t.astype(BF16) for t in _unpack_halves(_tiles_to_rows(x_ref)))
        dot = functools.partial(jnp.dot, preferred_element_type=F32)
        gate = dot(x_lo, wgb_ref[:half, :]) + dot(x_hi, wgb_ref[half:, :])
        up = dot(x_lo, wub_ref[:half, :]) + dot(x_hi, wub_ref[half:, :])
        hid = (gate * _sigmoid(gate) * up).astype(BF16)
        _rows_to_tiles(y_ref, _pack_halves(dot(hid, wdb_ref[...])))

    @pl.when(jnp.logical_not(used))
    def _():
        y_ref[...] = jnp.zeros_like(y_ref)


def _experts(bexp, nused, x_pad, w_gate, w_up, w_down, tb):
    P = x_pad.shape[0] // SUBLANES
    D = 2 * TILE_WORDS
    n_blocks = P // tb
    rows = (tb * SUBLANES, LANES)
    F = w_gate.shape[-1]
    blk = lambda b, be, nu: jnp.maximum(jnp.minimum(b, nu[0] - 1), 0)
    hbm = pl.BlockSpec(memory_space=pl.ANY)
    return pl.pallas_call(
        _expert_kernel,
        grid_spec=pltpu.PrefetchScalarGridSpec(
            num_scalar_prefetch=2,
            grid=(n_blocks,),
            in_specs=[pl.BlockSpec(rows, lambda b, be, nu: (blk(b, be, nu), 0)), hbm, hbm, hbm],
            out_specs=pl.BlockSpec(rows, lambda b, be, nu: (b, 0)),
            scratch_shapes=[
                pltpu.VMEM((2, D, F), F32), pltpu.VMEM((2, D, F), F32), pltpu.VMEM((2, F, D), F32),
                pltpu.VMEM((D, F), BF16), pltpu.VMEM((D, F), BF16), pltpu.VMEM((F, D), BF16),
                pltpu.SemaphoreType.DMA((2, 3)),
                pltpu.SMEM((1,), I32),
            ],
        ),
        out_shape=jax.ShapeDtypeStruct(x_pad.shape, U32),
        compiler_params=_cparams(("arbitrary",)),
        name="moe_experts",
    )(bexp, nused, x_pad, w_gate, w_up, w_down)


COMBINE_SLOTS = 3


def _combine_kernel(pstart_ref, ri_ref, r1_ref, r2_ref, x1_ref, rw_ref, g_ref, b_ref, ypad_ref, o_ref, ybuf_ref, sem):
    i = pl.program_id(0)
    n = pl.num_programs(0)
    tm = x1_ref.shape[0]
    slot = i % COMBINE_SLOTS

    def gather_loop(route_ref, s):
        def issue(t, carry):
            for k in range(TOP_K):
                pos = pstart_ref[route_ref[k, t]] + route_ref[TOP_K + k, t]
                _row_copy(ypad_ref, pos, ybuf_ref.at[s, k], t, sem.at[s]).start(priority=k)
            return carry
        lax.fori_loop(0, tm, issue, 0, unroll=8)

    def wait_slot(s):
        for k in range(TOP_K):
            _rows_wait(ypad_ref, ybuf_ref.at[s, k], tm, sem.at[s])

    @pl.when(i == 0)
    def _():
        gather_loop(ri_ref, 0)
        gather_loop(r1_ref, 1)

    wait_slot(slot)

    rw = rw_ref[...]
    y0 = _unpack_halves(_tiles_to_rows(ybuf_ref.at[slot, 0]))
    y1 = _unpack_halves(_tiles_to_rows(ybuf_ref.at[slot, 1]))
    ffn = jnp.concatenate([y0[h] * rw[:, 0:1] + y1[h] * rw[:, 1:2] for h in range(2)], axis=1)
    pre = DEEPNORM_ALPHA * x1_ref[...] + ffn
    mu = jnp.mean(pre, axis=-1, keepdims=True)
    pc = pre - mu
    var = jnp.mean(pc * pc, axis=-1, keepdims=True)
    o_ref[...] = pc * lax.rsqrt(var + LN_EPS) * g_ref[...] + b_ref[...]

    nxt = (i + 2) % COMBINE_SLOTS
    for t in range(tm):
        for k in range(TOP_K):
            pos = pstart_ref[r2_ref[k, t]] + r2_ref[TOP_K + k, t]
            _row_copy(ypad_ref, pos, ybuf_ref.at[nxt, k], t, sem.at[nxt]).start(priority=k)

    @pl.when(i == n - 1)
    def _():
        wait_slot((i + 1) % COMBINE_SLOTS)
        wait_slot(nxt)


def _combine(pstart, route_i, x1, rw_col, ln_g, ln_b, y_pad):
    T, D = x1.shape
    tm = min(COMBINE_TM, T)
    n = T // tm
    return pl.pallas_call(
        _combine_kernel,
        grid=(n,),
        in_specs=[
            pl.BlockSpec(memory_space=pltpu.SMEM),
            pl.BlockSpec((SUBLANES, tm), lambda i: (0, i), memory_space=pltpu.SMEM),
            pl.BlockSpec((SUBLANES, tm), lambda i: (0, jnp.minimum(i + 1, n - 1)), memory_space=pltpu.SMEM),
            pl.BlockSpec((SUBLANES, tm), lambda i: (0, jnp.minimum(i + 2, n - 1)), memory_space=pltpu.SMEM),
            pl.BlockSpec((tm, D), lambda i: (i, 0)),
            pl.BlockSpec((tm, SUBLANES), lambda i: (i, 0)),
            pl.BlockSpec((1, D), lambda i: (0, 0)),
            pl.BlockSpec((1, D), lambda i: (0, 0)),
            pl.BlockSpec(memory_space=pl.ANY),
        ],
        out_specs=pl.BlockSpec((tm, D), lambda i: (i, 0)),
        out_shape=jax.ShapeDtypeStruct((T, D), F32),
        scratch_shapes=[
            pltpu.VMEM((COMBINE_SLOTS, TOP_K, tm * SUBLANES, LANES), U32),
            pltpu.SemaphoreType.DMA((COMBINE_SLOTS,)),
        ],
        compiler_params=_cparams(("arbitrary",)),
        name="moe_combine_ln2",
    )(pstart, route_i, route_i, route_i, x1, rw_col, ln_g, ln_b, y_pad)


N_GATE_COLS = 2 * A_HEADS
SRC_MLSTM = 0
SRC_GATES = 4 * A_WIDTH
SRC_DIFF = SRC_GATES + N_GATE_COLS
SRC_MERGE = SRC_DIFF + 3 * B_WIDTH
PREP_TN = 512


def _wprep_kernel(a_ref, o_ref):
    o_ref[...] = a_ref[...].T.astype(BF16)


def _wgate_kernel(a_ref, o_ref):
    a = a_ref[...]
    rows = jnp.concatenate([a, jnp.zeros((LANES - a.shape[0], a.shape[1]), F32)], axis=0)
    o_ref[...] = rows.T


def _rearrange_in_proj(w_in, b_in):
    K, n_in = w_in.shape
    tn = PREP_TN
    merge_blocks = (2 * D_MODEL) // tn
    mlstm_blocks = (4 * A_WIDTH) // tn
    assert SRC_MERGE % SUBLANES == 0 and SRC_DIFF % SUBLANES == 0

    def src_row(jb):
        s = SUBLANES
        merge = SRC_MERGE // s + (tn // s) * jb
        mlstm = SRC_MLSTM // s + (tn // s) * (jb - merge_blocks)
        diff = SRC_DIFF // s + (tn // s) * (jb - merge_blocks - mlstm_blocks)
        return s * jnp.where(jb < merge_blocks, merge, jnp.where(jb < merge_blocks + mlstm_blocks, mlstm, diff))

    w_t = jnp.swapaxes(w_in, 0, 1)
    w_main = pl.pallas_call(
        _wprep_kernel,
        grid=(N_MAIN // tn,),
        in_specs=[pl.BlockSpec((pl.Element(tn), pl.Element(K)), lambda jb: (src_row(jb), 0))],
        out_specs=pl.BlockSpec((K, tn), lambda jb: (0, jb)),
        out_shape=jax.ShapeDtypeStruct((K, N_MAIN), BF16),
        compiler_params=_cparams(("parallel",)),
        name="w_in_prep",
    )(w_t)
    b_main = jnp.concatenate([b_in[SRC_MERGE:], b_in[SRC_MLSTM:SRC_GATES], b_in[SRC_DIFF:SRC_MERGE]])[None, :]
    w_gate = pl.pallas_call(
        _wgate_kernel,
        grid=(1,),
        in_specs=[pl.BlockSpec((pl.Element(N_GATE_COLS), pl.Element(K)), lambda i: (SRC_GATES, 0))],
        out_specs=pl.BlockSpec((K, LANES), lambda i: (0, 0)),
        out_shape=jax.ShapeDtypeStruct((K, LANES), F32),
        name="w_gate_prep",
    )(w_t)
    b_gate = jnp.pad(b_in[SRC_GATES:SRC_DIFF], (0, LANES - N_GATE_COLS))[None, :]
    return w_main, b_main, _hi_lo_columns(w_gate), b_gate


def _layer(x, w_in, b_in, conv_w, conv_b, norm_a_g, lq1, lk1, lq2, lk2, norm_b_g, w_a, w_b, w_out,
           ln1_g, ln1_b, w_grp, b_grp, w_exp, b_exp, w_gate, w_up, w_down, ln2_g, ln2_b, lambda_init):
    B, S, D = x.shape
    T = B * S
    x2d = x.reshape(T, D)

    w_main, b_main, w_g, b_g = _rearrange_in_proj(w_in, b_in)
    z_main, z_gate = _in_projection(x2d, w_main, b_main, w_g, b_g)

    L = min(MLSTM_CHUNK, S)
    gp = _gate_prep(z_gate[:, :2 * A_HEADS].T, L)
    h_a = _mlstm(z_main, gp.T, gp, conv_w, conv_b[None, :], norm_a_g[None, :], B, S)
    o_b = _diff_attention(z_main, lq1[None, :], lk1[None, :], lq2[None, :], lk2[None, :],
                          norm_b_g[None, :], B, S, lambda_init)

    w_r = jnp.zeros((D, LANES), F32).at[:, :N_GROUPS].set(w_grp).at[:, SUBLANES:SUBLANES + N_EXPERTS].set(w_exp)
    b_r = jnp.zeros((LANES,), F32).at[:N_GROUPS].set(b_grp).at[SUBLANES:SUBLANES + N_EXPERTS].set(b_exp)[None, :]
    x1, x1_packed, route_i, route_w, counts = _mix(h_a, o_b, z_main, x2d, w_a.astype(BF16), w_b.astype(BF16),
                                                   w_out.astype(BF16), ln1_g[None, :], ln1_b[None, :],
                                                   _hi_lo_columns(w_r), b_r)

    tb = MOE_TB
    n_blocks = (T * TOP_K) // tb + N_EXPERTS
    x_pad, pstart, bexp, nused = _dispatch(counts[:, 0], route_i, x1_packed, tb, n_blocks)
    y_pad = _experts(bexp, nused, x_pad, w_gate, w_up, w_down, tb)
    out = _combine(pstart, route_i, x1, route_w.T, ln2_g[None, :], ln2_b[None, :], y_pad)
    return out.reshape(B, S, D)


def kernel(x, w_in, b_in, conv_w, conv_b, mlstm_norm_g, lambda_q1, lambda_k1, lambda_q2, lambda_k2,
           diff_norm_g, w_a, w_b, w_out, ln1_g, ln1_b, w_grp, b_grp, w_exp, b_exp,
           w_gate, w_up, w_down, ln2_g, ln2_b):
    for l in range(DEPTH):
        lambda_init = 0.8 - 0.6 * math.exp(-0.3 * l)
        x = _layer(x, w_in[l], b_in[l], conv_w[l], conv_b[l], mlstm_norm_g[l], lambda_q1[l], lambda_k1[l],
                   lambda_q2[l], lambda_k2[l], diff_norm_g[l], w_a[l], w_b[l], w_out[l], ln1_g[l], ln1_b[l],
                   w_grp[l], b_grp[l], w_exp[l], b_exp[l], w_gate[l], w_up[l], w_down[l], ln2_g[l], ln2_b[l],
                   lambda_init)
    return x
```

```python
import functools
import math

import jax
import jax.numpy as jnp
from jax import lax
from jax.experimental import pallas as pl
from jax.experimental.pallas import tpu as pltpu

F32 = jnp.float32
BF16 = jnp.bfloat16
I32 = jnp.int32
U32 = jnp.uint32

D_MODEL = 2048
A_HEADS = 4
A_HEAD_DIM = 256
A_WIDTH = A_HEADS * A_HEAD_DIM
CONV_WIDTH = 4
B_HEADS = 8
B_HEAD_DIM = 64
B_V_DIM = 2 * B_HEAD_DIM
B_WIDTH = B_HEADS * B_V_DIM
ATTN_CHUNK = 64
N_GROUPS = 4
EXPERTS_PER_GROUP = 8
N_EXPERTS = N_GROUPS * EXPERTS_PER_GROUP
TOP_K = 2
D_EXPERT = 512
DEPTH = 1
DEEPNORM_ALPHA = (2 * DEPTH) ** 0.25
LN_EPS = 1e-5
NEG_INF = float("-inf")
LOG2_E = 1.4426950408889634

LANES = 128
SUBLANES = 8
VMEM_LIMIT_BYTES = 56 * 1024 * 1024

COL_GA = 0
COL_GB = COL_GA + D_MODEL
COL_QA = COL_GB + D_MODEL
COL_KA = COL_QA + A_WIDTH
COL_VA = COL_KA + A_WIDTH
COL_OA = COL_VA + A_WIDTH
COL_QB = COL_OA + A_WIDTH
COL_KB = COL_QB + B_WIDTH
COL_VB = COL_KB + B_WIDTH
N_MAIN = COL_VB + B_WIDTH

PROJ_TM, PROJ_TN = 1024, 1024
MLSTM_CHUNK = 512
ATTN_TQ = 1024
MIX_TM = 256
MOE_TB = 256
DISPATCH_TM = 256
COMBINE_TM = 256


def _cparams(sem, vmem=VMEM_LIMIT_BYTES):
    return pltpu.CompilerParams(dimension_semantics=sem, vmem_limit_bytes=vmem)


def _sigmoid(x):
    return 1.0 / (1.0 + jnp.exp(-x))


def _pack_halves(x):
    n = x.shape[1] // 2
    lo = lax.bitcast_convert_type(x[:, :n].astype(BF16).astype(F32), U32)
    hi = lax.bitcast_convert_type(x[:, n:].astype(BF16).astype(F32), U32)
    return (lo >> 16) | hi


def _unpack_halves(u):
    lo = lax.bitcast_convert_type(u << 16, F32)
    hi = lax.bitcast_convert_type(u & jnp.uint32(0xFFFF0000), F32)
    return lo, hi


TILE_WORDS = SUBLANES * LANES


def _rows_to_tiles(ref, v):
    rows = v.shape[0]
    for j in range(SUBLANES):
        ref[pl.ds(j, rows, stride=SUBLANES), :] = v[:, j * LANES:(j + 1) * LANES]


def _tiles_to_rows(ref):
    rows = ref.shape[0] // SUBLANES
    return jnp.concatenate([ref[pl.ds(j, rows, stride=SUBLANES), :] for j in range(SUBLANES)], axis=1)


def _split_bf16(a):
    hi = a.astype(BF16)
    return hi, (a - hi.astype(F32)).astype(BF16)


def _hi_lo_columns(w):
    return jnp.concatenate(_split_bf16(w), axis=1)


def _dot_3pass(a, w_hl):
    a_hi, a_lo = _split_bf16(a)
    n = w_hl.shape[1] // 2
    dot = functools.partial(jnp.dot, preferred_element_type=F32)
    r = dot(a_hi, w_hl)
    return r[:, :n] + (r[:, n:] + dot(a_lo, w_hl[:, :n]))


def _proj_kernel(x_ref, w_ref, b_ref, wg_ref, bg_ref, z_ref, zg_ref, xb_ref):
    @pl.when(pl.program_id(1) == 0)
    def _():
        x = x_ref[...]
        xb_ref[...] = x.astype(BF16)
        zg_ref[...] = _dot_3pass(x, wg_ref[...]) + bg_ref[...]

    acc = jnp.dot(xb_ref[...], w_ref[...], preferred_element_type=F32)
    z_ref[...] = (acc + b_ref[...]).astype(z_ref.dtype)


def _in_projection(x2d, w_main, b_main, w_gate, b_gate):
    T, K = x2d.shape
    N = w_main.shape[1]
    tm, tn = min(PROJ_TM, T), PROJ_TN
    return pl.pallas_call(
        _proj_kernel,
        grid=(T // tm, N // tn),
        in_specs=[
            pl.BlockSpec((tm, K), lambda i, j: (i, 0)),
            pl.BlockSpec((K, tn), lambda i, j: (0, j)),
            pl.BlockSpec((1, tn), lambda i, j: (0, j)),
            pl.BlockSpec((K, 2 * LANES), lambda i, j: (0, 0)),
            pl.BlockSpec((1, LANES), lambda i, j: (0, 0)),
        ],
        out_specs=[
            pl.BlockSpec((tm, tn), lambda i, j: (i, j)),
            pl.BlockSpec((tm, LANES), lambda i, j: (i, 0)),
        ],
        out_shape=[
            jax.ShapeDtypeStruct((T, N), BF16),
            jax.ShapeDtypeStruct((T, LANES), F32),
        ],
        scratch_shapes=[pltpu.VMEM((tm, K), BF16)],
        compiler_params=_cparams(("parallel", "arbitrary")),
        name="in_projection",
    )(x2d, w_main, b_main, w_gate, b_gate)


def _gate_prep_kernel(g_ref, o_ref, *, L):
    r = lax.broadcasted_iota(I32, (L, L), 0)
    c = lax.broadcasted_iota(I32, (L, L), 1)
    tri = (r <= c).astype(F32)
    row = lax.broadcasted_iota(I32, (g_ref.shape[0], L), 0)
    for ch in range(g_ref.shape[1] // L):
        g = g_ref[:, ch * L:(ch + 1) * L]
        lf = jnp.minimum(g, 0.0) - jnp.log(1.0 + jnp.exp(-jnp.abs(g)))
        b = jnp.dot(lf, tri, preferred_element_type=F32, precision=lax.Precision.HIGHEST)
        o_ref[:, ch * L:(ch + 1) * L] = jnp.where(row < A_HEADS, g, b)


def _gate_prep(g_rows, L):
    R, T = g_rows.shape
    per_step = math.gcd(T // L, 8)
    W = L * per_step
    return pl.pallas_call(
        functools.partial(_gate_prep_kernel, L=L),
        grid=(T // W,),
        in_specs=[pl.BlockSpec((R, W), lambda i: (0, i))],
        out_specs=pl.BlockSpec((R, W), lambda i: (0, i)),
        out_shape=jax.ShapeDtypeStruct((R, T), F32),
        compiler_params=_cparams(("parallel",)),
        name="gate_prep",
    )(g_rows)


def _mlstm_kernel(q_ref, k_ref, v_ref, o_ref, gcol_ref, grow_ref, cw_ref, cb_ref, ng_ref, out_ref,
                  c_ref, n_ref, m_ref, qcar_ref, kcar_ref):
    ci = pl.program_id(1)
    L = q_ref.shape[0]
    dh = A_HEAD_DIM

    @pl.when(ci == 0)
    def _():
        c_ref[...] = jnp.zeros_like(c_ref)
        n_ref[...] = jnp.zeros_like(n_ref)
        m_ref[...] = jnp.zeros_like(m_ref)
        qcar_ref[...] = jnp.zeros_like(qcar_ref)
        kcar_ref[...] = jnp.zeros_like(kcar_ref)

    def conv_silu(u_ref, car_ref, lo, wcol):
        u = u_ref[:, lo:lo + dh].astype(F32)
        ext = jnp.concatenate([car_ref[:, lo:lo + dh], u], axis=0)
        w = cw_ref[:, wcol:wcol + dh]
        y = cb_ref[:, wcol:wcol + dh] + w[3:4, :] * u
        for j in range(CONV_WIDTH - 1):
            off = SUBLANES - (CONV_WIDTH - 1) + j
            y = y + w[j:j + 1, :] * ext[off:off + L, :]
        car_ref[:, lo:lo + dh] = u[L - SUBLANES:, :]
        return y * _sigmoid(y)

    r = lax.broadcasted_iota(I32, (L, L), 0)
    c = lax.broadcasted_iota(I32, (L, L), 1)
    causal = r >= c
    gcol = gcol_ref[...]
    grow = grow_ref[...]

    for h in range(A_HEADS):
        lo = h * dh
        q = conv_silu(q_ref, qcar_ref, lo, lo)
        k = conv_silu(k_ref, kcar_ref, lo, A_WIDTH + lo) * (dh ** -0.5)
        v_b = v_ref[:, lo:lo + dh]
        q_b = q.astype(BF16)
        k_b = k.astype(BF16)

        i_col, b_col = gcol[:, h:h + 1], gcol[:, A_HEADS + h:A_HEADS + h + 1]
        i_row, b_row = grow[h:h + 1, :], grow[A_HEADS + h:A_HEADS + h + 1, :]
        b_last = b_row[:, L - 1:L]
        m_prev = m_ref[h]

        dmat = jnp.where(causal, b_col - b_row + i_row, NEG_INF)
        inter = b_col + m_prev
        m_t = jnp.maximum(inter, jnp.max(dmat, axis=-1, keepdims=True))
        s = lax.dot_general(q_b, k_b, (((1,), (1,)), ((), ())), preferred_element_type=F32)
        w_intra = jnp.exp(dmat - m_t) * s
        w_inter = jnp.exp(inter - m_t)
        qc = jnp.dot(q_b, c_ref[h].astype(BF16), preferred_element_type=F32)
        num = w_inter * qc + jnp.dot(w_intra.astype(BF16), v_b, preferred_element_type=F32)
        qn = jnp.sum(q * n_ref[h], axis=-1, keepdims=True)
        den = w_inter * qn + jnp.sum(w_intra, axis=-1, keepdims=True)
        hh = num / jnp.maximum(jnp.abs(den), jnp.exp(-m_t))

        g_col = b_last - b_col + i_col
        m_new = jnp.maximum(b_last + m_prev, jnp.max(g_col, axis=0, keepdims=True))
        decay = jnp.exp(b_last + m_prev - m_new)
        kw = jnp.exp(g_col - m_new) * k
        c_ref[h] = decay * c_ref[h] + jnp.dot(kw.T.astype(BF16), v_b, preferred_element_type=F32)
        n_ref[h] = decay * n_ref[h] + jnp.sum(kw, axis=0, keepdims=True)
        m_ref[h] = m_new

        y = _sigmoid(o_ref[:, lo:lo + dh].astype(F32)) * hh
        mu = jnp.mean(y, axis=-1, keepdims=True)
        yc = y - mu
        var = jnp.mean(yc * yc, axis=-1, keepdims=True)
        out_ref[:, lo:lo + dh] = (yc * lax.rsqrt(var + LN_EPS) * ng_ref[:, lo:lo + dh]).astype(out_ref.dtype)


def _mlstm(z_main, gcol, grow, conv_w, conv_b, norm_g, B, S):
    L = min(MLSTM_CHUNK, S)
    nc = S // L
    dh = A_HEAD_DIM
    H = A_HEADS
    W = A_WIDTH
    row = lambda b, c: b * nc + c
    full = lambda shape: pl.BlockSpec(shape, lambda b, c: (0, 0))
    return pl.pallas_call(
        _mlstm_kernel,
        grid=(B, nc),
        in_specs=[
            pl.BlockSpec((L, W), lambda b, c: (row(b, c), COL_QA // W)),
            pl.BlockSpec((L, W), lambda b, c: (row(b, c), COL_KA // W)),
            pl.BlockSpec((L, W), lambda b, c: (row(b, c), COL_VA // W)),
            pl.BlockSpec((L, W), lambda b, c: (row(b, c), COL_OA // W)),
            pl.BlockSpec((L, SUBLANES), lambda b, c: (row(b, c), 0)),
            pl.BlockSpec((SUBLANES, L), lambda b, c: (0, row(b, c))),
            full((CONV_WIDTH, 2 * W)), full((1, 2 * W)), full((1, W)),
        ],
        out_specs=pl.BlockSpec((L, W), lambda b, c: (row(b, c), 0)),
        out_shape=jax.ShapeDtypeStruct((B * S, W), BF16),
        scratch_shapes=[
            pltpu.VMEM((H, dh, dh), F32),
            pltpu.VMEM((H, 1, dh), F32),
            pltpu.VMEM((H, 1, 1), F32),
            pltpu.VMEM((SUBLANES, W), F32),
            pltpu.VMEM((SUBLANES, W), F32),
        ],
        compiler_params=_cparams(("parallel", "arbitrary")),
        name="mlstm",
    )(z_main, z_main, z_main, z_main, gcol, grow, conv_w, conv_b, norm_g)


def _lane_tiles(t):
    return [t[:, i * LANES:(i + 1) * LANES] for i in range(t.shape[1] // LANES)]


def _attn_kernel(slope_ref, q_ref, k_ref, kb_ref, v_ref, lq1_ref, lk1_ref, lq2_ref, lk2_ref, g_ref, o_ref,
                 s_ref, mx_ref, ls_ref, acc_ref, corr_ref, *, lambda_init):
    h = pl.program_id(1)
    qi = pl.program_id(2)
    tq = q_ref.shape[0]
    tk = tq
    slope = slope_ref[h]
    nt = (((1,), (1,)), ((), ()))

    lane = lax.broadcasted_iota(I32, (tq, B_V_DIM), 1)
    q = q_ref[...] * (B_HEAD_DIM ** -0.5)
    zero = jnp.zeros_like(q)
    bias_cols = jnp.where(lane < 2, 1.0, 0.0).astype(BF16)
    qa = (jnp.concatenate([jnp.where(lane < B_HEAD_DIM, q, zero), bias_cols], axis=1),
          jnp.concatenate([jnp.where(lane >= B_HEAD_DIM, q, zero), bias_cols], axis=1))

    def scores(j):
        start = pl.multiple_of(j * tk, tk)
        ka = jnp.concatenate([k_ref[pl.ds(start, tk), :], kb_ref[pl.ds(start, tk), :]], axis=1)
        return [lax.dot_general(qa[n], ka, nt, preferred_element_type=F32) for n in range(2)]

    def keep(j, n, t):
        t = t * LOG2_E
        s_ref[n, j] = t
        mx_ref[n] = functools.reduce(jnp.maximum, _lane_tiles(t), mx_ref[n])

    mx_ref[...] = jnp.full(mx_ref.shape, NEG_INF, F32)

    def pass_a(j):
        t = scores(j)
        for n in range(2):
            keep(j, n, t[n])

    def pass_a_pair(jj, carry):
        pass_a(2 * jj)
        pass_a(2 * jj + 1)
        return carry
    lax.fori_loop(0, qi // 2, pass_a_pair, 0)

    @pl.when(qi % 2 == 1)
    def _():
        pass_a(qi - 1)

    @pl.when(qi == 0)
    def _():
        r = lax.broadcasted_iota(I32, (tq, tk), 0)
        c = lax.broadcasted_iota(I32, (tq, tk), 1)
        ahead = jnp.where(c <= r, 0.0, (r - c).astype(F32) * (2.0 * slope))
        corr_ref[...] = jnp.where((c // ATTN_CHUNK) <= (r // ATTN_CHUNK), ahead, NEG_INF)

    hq = tq // 2
    dstart = pl.multiple_of(qi * tk, tk)
    ka = jnp.concatenate([k_ref[pl.ds(dstart, tk), :], kb_ref[pl.ds(dstart, tk), :]], axis=1)
    masked = jnp.full((hq, tk - hq), NEG_INF, F32)
    for n in range(2):
        top = lax.dot_general(qa[n][:hq], ka[:hq], nt, preferred_element_type=F32)
        bot = lax.dot_general(qa[n][hq:], ka, nt, preferred_element_type=F32)
        t = jnp.concatenate([jnp.concatenate([top, masked], axis=1), bot], axis=0)
        keep(qi, n, t + corr_ref[...])

    for n in range(2):
        m = jnp.max(mx_ref[n], axis=-1, keepdims=True)
        mx_ref[n] = jnp.broadcast_to(m, (tq, LANES))
    ls_ref[...] = jnp.zeros_like(ls_ref)
    acc_ref[...] = jnp.zeros_like(acc_ref)

    def weigh(n, rows, s_tile, v_tile):
        mb = mx_ref[n, rows, :]
        ps = [jnp.exp2(t - mb) for t in _lane_tiles(s_tile)]
        ls_ref[n, rows, :] += functools.reduce(jnp.add, ps)
        p = jnp.concatenate(ps, axis=1).astype(BF16)
        acc_ref[n, rows, :] += jnp.dot(p, v_tile, preferred_element_type=F32)

    def pass_b(j):
        start = pl.multiple_of(j * tk, tk)
        vj = v_ref[pl.ds(start, tk), :]
        for n in range(2):
            weigh(n, slice(None), s_ref[n, j], vj)

    def pass_b_pair(jj, carry):
        pass_b(2 * jj)
        pass_b(2 * jj + 1)
        return carry
    lax.fori_loop(0, qi // 2, pass_b_pair, 0)

    @pl.when(qi % 2 == 1)
    def _():
        pass_b(qi - 1)

    for n in range(2):
        weigh(n, slice(0, hq), s_ref[n, qi, :hq, :hq], v_ref[pl.ds(dstart, hq), :])
        weigh(n, slice(hq, tq), s_ref[n, qi, hq:, :], v_ref[pl.ds(dstart, tk), :])

    lam = (jnp.exp(jnp.sum(lq1_ref[...] * lk1_ref[...], axis=-1, keepdims=True))
           - jnp.exp(jnp.sum(lq2_ref[...] * lk2_ref[...], axis=-1, keepdims=True)) + lambda_init)
    l0 = jnp.sum(ls_ref[0], axis=-1, keepdims=True)
    l1 = jnp.sum(ls_ref[1], axis=-1, keepdims=True)
    o = acc_ref[0] / l0 - lam * (acc_ref[1] / l1)
    ms = jnp.mean(o * o, axis=-1, keepdims=True)
    o_ref[...] = (o * lax.rsqrt(ms + LN_EPS) * g_ref[...] * (1.0 - lambda_init)).astype(o_ref.dtype)


def _diff_attention(z_main, lq1, lk1, lq2, lk2, norm_g, B, S, lambda_init):
    tq = min(ATTN_TQ, S)
    nq = S // tq
    H = B_HEADS
    dv = B_V_DIM
    slopes = 2.0 ** (-8.0 * jnp.arange(1, H + 1, dtype=F32) / H)
    assert S <= 256 * 256
    pos = lax.broadcasted_iota(I32, (H, S, dv), 1)
    col = lax.broadcasted_iota(I32, (H, S, dv), 2)
    within = pos % 256
    kbias = jnp.where(col == 0, within, jnp.where(col == 1, pos - within, 0)).astype(F32)
    kbias = (kbias * slopes[:, None, None]).astype(BF16)
    small = pl.BlockSpec((1, B_HEAD_DIM), lambda b, h, i: (0, 0))
    return pl.pallas_call(
        functools.partial(_attn_kernel, lambda_init=lambda_init),
        grid=(B, H, nq),
        in_specs=[
            pl.BlockSpec(memory_space=pltpu.SMEM),
            pl.BlockSpec((tq, dv), lambda b, h, i: (b * nq + i, COL_QB // dv + h)),
            pl.BlockSpec((S, dv), lambda b, h, i: (b, COL_KB // dv + h)),
            pl.BlockSpec((None, S, dv), lambda b, h, i: (h, 0, 0)),
            pl.BlockSpec((S, dv), lambda b, h, i: (b, COL_VB // dv + h)),
            small, small, small, small,
            pl.BlockSpec((1, dv), lambda b, h, i: (0, 0)),
        ],
        out_specs=pl.BlockSpec((tq, dv), lambda b, h, i: (b * nq + i, h)),
        out_shape=jax.ShapeDtypeStruct((B * S, B_WIDTH), BF16),
        scratch_shapes=[
            pltpu.VMEM((2, nq, tq, tq), F32),
            pltpu.VMEM((2, tq, LANES), F32),
            pltpu.VMEM((2, tq, LANES), F32),
            pltpu.VMEM((2, tq, dv), F32),
            pltpu.VMEM((tq, tq), F32),
        ],
        compiler_params=_cparams(("parallel", "parallel", "arbitrary")),
        name="diff_attention",
    )(slopes, z_main, z_main, kbias, z_main, lq1, lk1, lq2, lk2, norm_g)


def _mix_kernel(ha_ref, ob_ref, ga_ref, gb_ref, x_ref, wa_ref, wb_ref, wo_ref, g1_ref, b1_ref, wr_ref, br_ref,
                x1_ref, x1p_ref, ri_ref, rw_ref, cnt_ref, carry_ref):
    i = pl.program_id(0)
    tm = x_ref.shape[0]

    @pl.when(i == 0)
    def _():
        carry_ref[...] = jnp.zeros_like(carry_ref)

    ya = jnp.dot(ha_ref[...], wa_ref[...], preferred_element_type=F32)
    yb = jnp.dot(ob_ref[...], wb_ref[...], preferred_element_type=F32)
    merged = _sigmoid(ga_ref[...].astype(F32)) * ya + _sigmoid(gb_ref[...].astype(F32)) * yb
    mix = jnp.dot(merged.astype(BF16), wo_ref[...], preferred_element_type=F32)
    pre = DEEPNORM_ALPHA * x_ref[...] + mix
    mu = jnp.mean(pre, axis=-1, keepdims=True)
    pc = pre - mu
    var = jnp.mean(pc * pc, axis=-1, keepdims=True)
    x1 = pc * lax.rsqrt(var + LN_EPS) * g1_ref[...] + b1_ref[...]
    x1_ref[...] = x1
    _rows_to_tiles(x1p_ref, _pack_halves(x1))

    logits = _dot_3pass(x1, wr_ref[...]) + br_ref[...]
    lt = logits.T
    row8 = lax.broadcasted_iota(I32, (SUBLANES, tm), 0).astype(F32)
    gl = jnp.where(row8 < N_GROUPS, lt[0:SUBLANES, :], NEG_INF)
    gmax = jnp.max(gl, axis=0, keepdims=True)
    gsel = jnp.min(jnp.where(gl == gmax, row8, float(SUBLANES)), axis=0, keepdims=True)
    gprob = 1.0 / jnp.sum(jnp.exp(gl - gmax), axis=0, keepdims=True)
    ing = lt[SUBLANES:2 * SUBLANES, :]
    for g in range(1, N_GROUPS):
        ing = jnp.where(gsel == g, lt[(g + 1) * SUBLANES:(g + 2) * SUBLANES, :], ing)
    v0 = jnp.max(ing, axis=0, keepdims=True)
    i0 = jnp.min(jnp.where(ing == v0, row8, float(SUBLANES)), axis=0, keepdims=True)
    ing2 = jnp.where(row8 == i0, -jnp.inf, ing)
    v1 = jnp.max(ing2, axis=0, keepdims=True)
    i1 = jnp.min(jnp.where(ing2 == v1, row8, float(SUBLANES)), axis=0, keepdims=True)
    ex = jnp.exp(v1 - v0)
    inv = 1.0 / (1.0 + ex)
    w0 = gprob * inv
    w1 = gprob * (ex * inv)
    e0 = (gsel * EXPERTS_PER_GROUP + i0).astype(I32)
    e1 = (gsel * EXPERTS_PER_GROUP + i1).astype(I32)

    rowe = lax.broadcasted_iota(I32, (N_EXPERTS, tm), 0)
    is0 = rowe == e0
    is1 = rowe == e1
    oh = jnp.where(is0, 1.0, 0.0) + jnp.where(is1, 1.0, 0.0)
    tr = lax.broadcasted_iota(I32, (tm, tm), 0)
    tc = lax.broadcasted_iota(I32, (tm, tm), 1)
    su = jnp.where(tr < tc, 1.0, 0.0).astype(BF16)
    tot = jnp.dot(oh.astype(BF16), su, preferred_element_type=F32) + carry_ref[...]
    rank0 = jnp.sum(jnp.where(is0, tot, 0.0), axis=0, keepdims=True)
    rank1 = jnp.sum(jnp.where(is1, tot, 0.0), axis=0, keepdims=True)
    carry = carry_ref[...] + jnp.sum(oh, axis=1, keepdims=True)
    carry_ref[...] = carry
    cnt_ref[...] = jnp.broadcast_to(carry, cnt_ref.shape).astype(I32)

    zi = jnp.zeros((1, tm), I32)
    ri_ref[...] = jnp.concatenate([e0, e1, rank0.astype(I32), rank1.astype(I32), zi, zi, zi, zi], axis=0)
    zf = jnp.zeros((1, tm), F32)
    rw_ref[...] = jnp.concatenate([w0, w1, zf, zf, zf, zf, zf, zf], axis=0)


def _mix(h_a, o_b, z_main, x2d, w_a, w_b, w_out, ln_g, ln_b, w_r, b_r):
    T, D = x2d.shape
    tm = min(MIX_TM, T)
    const = lambda shape: pl.BlockSpec(shape, lambda i: (0, 0), pipeline_mode=pl.Buffered(1))
    return pl.pallas_call(
        _mix_kernel,
        grid=(T // tm,),
        in_specs=[
            pl.BlockSpec((tm, A_WIDTH), lambda i: (i, 0)),
            pl.BlockSpec((tm, B_WIDTH), lambda i: (i, 0)),
            pl.BlockSpec((tm, D), lambda i: (i, COL_GA // D)),
            pl.BlockSpec((tm, D), lambda i: (i, COL_GB // D)),
            pl.BlockSpec((tm, D), lambda i: (i, 0)),
            const((A_WIDTH, D)), const((B_WIDTH, D)), const((D, D)),
            const((1, D)), const((1, D)), const((D, 2 * LANES)), const((1, LANES)),
        ],
        out_specs=[
            pl.BlockSpec((tm, D), lambda i: (i, 0)),
            pl.BlockSpec((tm * SUBLANES, LANES), lambda i: (i, 0)),
            pl.BlockSpec((SUBLANES, tm), lambda i: (0, i)),
            pl.BlockSpec((SUBLANES, tm), lambda i: (0, i)),
            pl.BlockSpec((N_EXPERTS, LANES), lambda i: (0, 0)),
        ],
        out_shape=[
            jax.ShapeDtypeStruct((T, D), F32),
            jax.ShapeDtypeStruct((T * SUBLANES, LANES), U32),
            jax.ShapeDtypeStruct((SUBLANES, T), I32),
            jax.ShapeDtypeStruct((SUBLANES, T), F32),
            jax.ShapeDtypeStruct((N_EXPERTS, LANES), I32),
        ],
        scratch_shapes=[pltpu.VMEM((N_EXPERTS, 1), F32)],
        compiler_params=_cparams(("arbitrary",)),
        name="mix_ln1_router",
    )(h_a, o_b, z_main, z_main, x2d, w_a, w_b, w_out, ln_g, ln_b, w_r, b_r)


def _row_copy(src_ref, src_row, dst_ref, dst_row, sem):
    return pltpu.make_async_copy(src_ref.at[pl.ds(src_row * SUBLANES, SUBLANES)],
                                 dst_ref.at[pl.ds(dst_row * SUBLANES, SUBLANES)], sem)


def _rows_wait(src_ref, dst_ref, rows, sem):
    n = rows * SUBLANES
    pltpu.make_async_copy(src_ref.at[pl.ds(0, n)], dst_ref.at[pl.ds(0, n)], sem).wait()


DISPATCH_SLOTS = 3


def _dispatch_kernel(cnt_ref, ri_ref, x1_ref, xpad_ref, pstart_ref, bexp_ref, nused_ref,
                     pst_ref, zero_ref, xbuf_ref, sem, lsem, zsem, *, tb, n_blocks, tm):
    i = pl.program_id(0)
    last = pl.num_programs(0) - 1

    @pl.when(i == 0)
    def _():
        def fill(b, carry):
            bexp_ref[b] = 0
            return carry
        lax.fori_loop(0, n_blocks, fill, 0)

        def per_expert(e, blk):
            pst_ref[e] = blk * tb
            pstart_ref[e] = blk * tb
            nb = (cnt_ref[e] + (tb - 1)) // tb

            def mark(b, carry):
                bexp_ref[blk + b] = e
                return carry
            lax.fori_loop(0, nb, mark, 0)
            return blk + nb
        nused = lax.fori_loop(0, N_EXPERTS, per_expert, 0)
        nused_ref[0] = nused
        pst_ref[N_EXPERTS] = nused

    rows = tm * SUBLANES
    slot = i % DISPATCH_SLOTS

    def tile_load(tile, s):
        src = x1_ref.at[pl.ds(pl.multiple_of(tile * rows, rows), rows)]
        return pltpu.make_async_copy(src, xbuf_ref.at[s], lsem.at[s])

    def wait_rows(s):
        for k in range(TOP_K):
            _rows_wait(xbuf_ref.at[s], xpad_ref, tm, sem.at[s])

    @pl.when(i == 0)
    def _():
        tile_load(0, 0).start()

        @pl.when(last >= 1)
        def _():
            tile_load(1, 1).start()

    tile_load(i, slot).wait()
    for t in range(tm):
        for k in range(TOP_K):
            pos = pst_ref[ri_ref[k, t]] + ri_ref[TOP_K + k, t]
            _row_copy(xbuf_ref.at[slot], t, xpad_ref, pos, sem.at[slot]).start(priority=k)

    prev = (i + DISPATCH_SLOTS - 1) % DISPATCH_SLOTS

    @pl.when(i >= 1)
    def _():
        wait_rows(prev)

    @pl.when(i + 2 <= last)
    def _():
        tile_load(i + 2, prev).start()

    @pl.when(i == last)
    def _():
        wait_rows(slot)
        zero_ref[...] = jnp.zeros_like(zero_ref)

        def per_expert(e, carry):
            cnt = cnt_ref[e]
            first = pst_ref[e] + cnt
            npad = ((cnt + (tb - 1)) // tb) * tb - cnt

            def start(p, c2):
                _row_copy(zero_ref, 0, xpad_ref, first + p, zsem).start()
                return c2
            lax.fori_loop(0, npad, start, 0)

            def wait(p, c2):
                _row_copy(zero_ref, 0, xpad_ref, 0, zsem).wait()
                return c2
            lax.fori_loop(0, npad, wait, 0)
            return carry
        lax.fori_loop(0, N_EXPERTS, per_expert, 0)

        def block_copy(b):
            n = tb * SUBLANES
            return pltpu.make_async_copy(zero_ref, xpad_ref.at[pl.ds(pl.multiple_of(b * n, n), n)], zsem)

        def start_block(b, carry):
            block_copy(b).start()
            return carry
        lax.fori_loop(pst_ref[N_EXPERTS], n_blocks, start_block, 0)

        def wait_block(b, carry):
            block_copy(b).wait()
            return carry
        lax.fori_loop(pst_ref[N_EXPERTS], n_blocks, wait_block, 0)


def _dispatch(counts, route_i, x1, tb, n_blocks):
    T = x1.shape[0] // SUBLANES
    tm = min(DISPATCH_TM, T)
    smem_full = pl.BlockSpec(memory_space=pltpu.SMEM)
    return pl.pallas_call(
        functools.partial(_dispatch_kernel, tb=tb, n_blocks=n_blocks, tm=tm),
        grid=(T // tm,),
        in_specs=[
            smem_full,
            pl.BlockSpec((SUBLANES, tm), lambda i: (0, i), memory_space=pltpu.SMEM),
            pl.BlockSpec(memory_space=pl.ANY),
        ],
        out_specs=[
            pl.BlockSpec(memory_space=pl.ANY),
            smem_full, smem_full, smem_full,
        ],
        out_shape=[
            jax.ShapeDtypeStruct((n_blocks * tb * SUBLANES, LANES), x1.dtype),
            jax.ShapeDtypeStruct((N_EXPERTS,), I32),
            jax.ShapeDtypeStruct((n_blocks,), I32),
            jax.ShapeDtypeStruct((1,), I32),
        ],
        scratch_shapes=[
            pltpu.SMEM((N_EXPERTS + 1,), I32),
            pltpu.VMEM((tb * SUBLANES, LANES), x1.dtype),
            pltpu.VMEM((DISPATCH_SLOTS, tm * SUBLANES, LANES), x1.dtype),
            pltpu.SemaphoreType.DMA((DISPATCH_SLOTS,)),
            pltpu.SemaphoreType.DMA((DISPATCH_SLOTS,)),
            pltpu.SemaphoreType.DMA(()),
        ],
        compiler_params=_cparams(("arbitrary",)),
        name="moe_dispatch",
    )(counts, route_i, x1)


def _expert_kernel(bexp_ref, nused_ref, x_ref, wg_hbm, wu_hbm, wd_hbm, y_ref,
                   wgf_ref, wuf_ref, wdf_ref, wgb_ref, wub_ref, wdb_ref, sem, run_ref):
    b = pl.program_id(0)
    nused = nused_ref[0]
    used = b < nused
    e = bexp_ref[b]
    new_expert = jnp.logical_or(b == 0, e != bexp_ref[jnp.maximum(b - 1, 0)])
    streams = ((wg_hbm, wgf_ref), (wu_hbm, wuf_ref), (wd_hbm, wdf_ref))

    def weight_copies(expert, slot):
        return [pltpu.make_async_copy(w_hbm.at[expert], wf_ref.at[slot], sem.at[slot, j])
                for j, (w_hbm, wf_ref) in enumerate(streams)]

    @pl.when(jnp.logical_and(used, b == 0))
    def _():
        run_ref[0] = 0
        for cp in weight_copies(e, 0):
            cp.start()

    @pl.when(jnp.logical_and(used, new_expert))
    def _():
        slot = run_ref[0] % 2
        nxt = lax.while_loop(lambda j: jnp.logical_and(j < nused, bexp_ref[jnp.minimum(j, nused - 1)] == e),
                             lambda j: j + 1, b + 1)

        @pl.when(nxt < nused)
        def _():
            for cp in weight_copies(bexp_ref[jnp.minimum(nxt, nused - 1)], 1 - slot):
                cp.start()

        for cp in weight_copies(e, slot):
            cp.wait()
        wgb_ref[...] = wgf_ref[slot].astype(BF16)
        wub_ref[...] = wuf_ref[slot].astype(BF16)
        wdb_ref[...] = wdf_ref[slot].astype(BF16)
        run_ref[0] = run_ref[0] + 1

    @pl.when(used)
    def _():
        half = SUBLANES * LANES
        x_lo, x_hi = (t.astype(BF16) for t in _unpack_halves(_tiles_to_rows(x_ref)))
        dot = functools.partial(jnp.dot, preferred_element_type=F32)
        gate = dot(x_lo, wgb_ref[:half, :]) + dot(x_hi, wgb_ref[half:, :])
        up = dot(x_lo, wub_ref[:half, :]) + dot(x_hi, wub_ref[half:, :])
        hid = (gate * _sigmoid(gate) * up).astype(BF16)
        _rows_to_tiles(y_ref, _pack_halves(dot(hid, wdb_ref[...])))

    @pl.when(jnp.logical_not(used))
    def _():
        y_ref[...] = jnp.zeros_like(y_ref)


def _experts(bexp, nused, x_pad, w_gate, w_up, w_down, tb):
    P = x_pad.shape[0] // SUBLANES
    D = 2 * TILE_WORDS
    n_blocks = P // tb
    rows = (tb * SUBLANES, LANES)
    F = w_gate.shape[-1]
    blk = lambda b, be, nu: jnp.maximum(jnp.minimum(b, nu[0] - 1), 0)
    hbm = pl.BlockSpec(memory_space=pl.ANY)
    return pl.pallas_call(
        _expert_kernel,
        grid_spec=pltpu.PrefetchScalarGridSpec(
            num_scalar_prefetch=2,
            grid=(n_blocks,),
            in_specs=[pl.BlockSpec(rows, lambda b, be, nu: (blk(b, be, nu), 0)), hbm, hbm, hbm],
            out_specs=pl.BlockSpec(rows, lambda b, be, nu: (b, 0)),
            scratch_shapes=[
                pltpu.VMEM((2, D, F), F32), pltpu.VMEM((2, D, F), F32), pltpu.VMEM((2, F, D), F32),
                pltpu.VMEM((D, F), BF16), pltpu.VMEM((D, F), BF16), pltpu.VMEM((F, D), BF16),
                pltpu.SemaphoreType.DMA((2, 3)),
                pltpu.SMEM((1,), I32),
            ],
        ),
        out_shape=jax.ShapeDtypeStruct(x_pad.shape, U32),
        compiler_params=_cparams(("arbitrary",)),
        name="moe_experts",
    )(bexp, nused, x_pad, w_gate, w_up, w_down)


COMBINE_SLOTS = 3


def _combine_kernel(pstart_ref, ri_ref, r1_ref, r2_ref, x1_ref, rw_ref, g_ref, b_ref, ypad_ref, o_ref, ybuf_ref, sem):
    i = pl.program_id(0)
    n = pl.num_programs(0)
    tm = x1_ref.shape[0]
    slot = i % COMBINE_SLOTS

    def gather_loop(route_ref, s):
        def issue(t, carry):
            for k in range(TOP_K):
                pos = pstart_ref[route_ref[k, t]] + route_ref[TOP_K + k, t]
                _row_copy(ypad_ref, pos, ybuf_ref.at[s, k], t, sem.at[s]).start(priority=k)
            return carry
        lax.fori_loop(0, tm, issue, 0, unroll=8)

    def wait_slot(s):
        for k in range(TOP_K):
            _rows_wait(ypad_ref, ybuf_ref.at[s, k], tm, sem.at[s])

    @pl.when(i == 0)
    def _():
        gather_loop(ri_ref, 0)
        gather_loop(r1_ref, 1)

    wait_slot(slot)

    rw = rw_ref[...]
    y0 = _unpack_halves(_tiles_to_rows(ybuf_ref.at[slot, 0]))
    y1 = _unpack_halves(_tiles_to_rows(ybuf_ref.at[slot, 1]))
    ffn = jnp.concatenate([y0[h] * rw[:, 0:1] + y1[h] * rw[:, 1:2] for h in range(2)], axis=1)
    pre = DEEPNORM_ALPHA * x1_ref[...] + ffn
    mu = jnp.mean(pre, axis=-1, keepdims=True)
    pc = pre - mu
    var = jnp.mean(pc * pc, axis=-1, keepdims=True)
    o_ref[...] = pc * lax.rsqrt(var + LN_EPS) * g_ref[...] + b_ref[...]

    nxt = (i + 2) % COMBINE_SLOTS
    for t in range(tm):
        for k in range(TOP_K):
            pos = pstart_ref[r2_ref[k, t]] + r2_ref[TOP_K + k, t]
            _row_copy(ypad_ref, pos, ybuf_ref.at[nxt, k], t, sem.at[nxt]).start(priority=k)

    @pl.when(i == n - 1)
    def _():
        wait_slot((i + 1) % COMBINE_SLOTS)
        wait_slot(nxt)


def _combine(pstart, route_i, x1, rw_col, ln_g, ln_b, y_pad):
    T, D = x1.shape
    tm = min(COMBINE_TM, T)
    n = T // tm
    return pl.pallas_call(
        _combine_kernel,
        grid=(n,),
        in_specs=[
            pl.BlockSpec(memory_space=pltpu.SMEM),
            pl.BlockSpec((SUBLANES, tm), lambda i: (0, i), memory_space=pltpu.SMEM),
            pl.BlockSpec((SUBLANES, tm), lambda i: (0, jnp.minimum(i + 1, n - 1)), memory_space=pltpu.SMEM),
            pl.BlockSpec((SUBLANES, tm), lambda i: (0, jnp.minimum(i + 2, n - 1)), memory_space=pltpu.SMEM),
            pl.BlockSpec((tm, D), lambda i: (i, 0)),
            pl.BlockSpec((tm, SUBLANES), lambda i: (i, 0)),
            pl.BlockSpec((1, D), lambda i: (0, 0)),
            pl.BlockSpec((1, D), lambda i: (0, 0)),
            pl.BlockSpec(memory_space=pl.ANY),
        ],
        out_specs=pl.BlockSpec((tm, D), lambda i: (i, 0)),
        out_shape=jax.ShapeDtypeStruct((T, D), F32),
        scratch_shapes=[
            pltpu.VMEM((COMBINE_SLOTS, TOP_K, tm * SUBLANES, LANES), U32),
            pltpu.SemaphoreType.DMA((COMBINE_SLOTS,)),
        ],
        compiler_params=_cparams(("arbitrary",)),
        name="moe_combine_ln2",
    )(pstart, route_i, route_i, route_i, x1, rw_col, ln_g, ln_b, y_pad)


N_GATE_COLS = 2 * A_HEADS
SRC_MLSTM = 0
SRC_GATES = 4 * A_WIDTH
SRC_DIFF = SRC_GATES + N_GATE_COLS
SRC_MERGE = SRC_DIFF + 3 * B_WIDTH
PREP_TN = 512


def _wprep_kernel(a_ref, o_ref):
    o_ref[...] = a_ref[...].T.astype(BF16)


def _wgate_kernel(a_ref, o_ref):
    a = a_ref[...]
    rows = jnp.concatenate([a, jnp.zeros((LANES - a.shape[0], a.shape[1]), F32)], axis=0)
    o_ref[...] = rows.T


def _rearrange_in_proj(w_in, b_in):
    K, n_in = w_in.shape
    tn = PREP_TN
    merge_blocks = (2 * D_MODEL) // tn
    mlstm_blocks = (4 * A_WIDTH) // tn
    assert SRC_MERGE % SUBLANES == 0 and SRC_DIFF % SUBLANES == 0

    def src_row(jb):
        s = SUBLANES
        merge = SRC_MERGE // s + (tn // s) * jb
        mlstm = SRC_MLSTM // s + (tn // s) * (jb - merge_blocks)
        diff = SRC_DIFF // s + (tn // s) * (jb - merge_blocks - mlstm_blocks)
        return s * jnp.where(jb < merge_blocks, merge, jnp.where(jb < merge_blocks + mlstm_blocks, mlstm, diff))

    w_t = jnp.swapaxes(w_in, 0, 1)
    w_main = pl.pallas_call(
        _wprep_kernel,
        grid=(N_MAIN // tn,),
        in_specs=[pl.BlockSpec((pl.Element(tn), pl.Element(K)), lambda jb: (src_row(jb), 0))],
        out_specs=pl.BlockSpec((K, tn), lambda jb: (0, jb)),
        out_shape=jax.ShapeDtypeStruct((K, N_MAIN), BF16),
        compiler_params=_cparams(("parallel",)),
        name="w_in_prep",
    )(w_t)
    b_main = jnp.concatenate([b_in[SRC_MERGE:], b_in[SRC_MLSTM:SRC_GATES], b_in[SRC_DIFF:SRC_MERGE]])[None, :]
    w_gate = pl.pallas_call(
        _wgate_kernel,
        grid=(1,),
        in_specs=[pl.BlockSpec((pl.Element(N_GATE_COLS), pl.Element(K)), lambda i: (SRC_GATES, 0))],
        out_specs=pl.BlockSpec((K, LANES), lambda i: (0, 0)),
        out_shape=jax.ShapeDtypeStruct((K, LANES), F32),
        name="w_gate_prep",
    )(w_t)
    b_gate = jnp.pad(b_in[SRC_GATES:SRC_DIFF], (0, LANES - N_GATE_COLS))[None, :]
    return w_main, b_main, _hi_lo_columns(w_gate), b_gate


def _layer(x, w_in, b_in, conv_w, conv_b, norm_a_g, lq1, lk1, lq2, lk2, norm_b_g, w_a, w_b, w_out,
           ln1_g, ln1_b, w_grp, b_grp, w_exp, b_exp, w_gate, w_up, w_down, ln2_g, ln2_b, lambda_init):
    B, S, D = x.shape
    T = B * S
    x2d = x.reshape(T, D)

    w_main, b_main, w_g, b_g = _rearrange_in_proj(w_in, b_in)
    z_main, z_gate = _in_projection(x2d, w_main, b_main, w_g, b_g)

    L = min(MLSTM_CHUNK, S)
    gp = _gate_prep(z_gate[:, :2 * A_HEADS].T, L)
    h_a = _mlstm(z_main, gp.T, gp, conv_w, conv_b[None, :], norm_a_g[None, :], B, S)
    o_b = _diff_attention(z_main, lq1[None, :], lk1[None, :], lq2[None, :], lk2[None, :],
                          norm_b_g[None, :], B, S, lambda_init)

    w_r = jnp.zeros((D, LANES), F32).at[:, :N_GROUPS].set(w_grp).at[:, SUBLANES:SUBLANES + N_EXPERTS].set(w_exp)
    b_r = jnp.zeros((LANES,), F32).at[:N_GROUPS].set(b_grp).at[SUBLANES:SUBLANES + N_EXPERTS].set(b_exp)[None, :]
    x1, x1_packed, route_i, route_w, counts = _mix(h_a, o_b, z_main, x2d, w_a.astype(BF16), w_b.astype(BF16),
                                                   w_out.astype(BF16), ln1_g[None, :], ln1_b[None, :],
                                                   _hi_lo_columns(w_r), b_r)

    tb = MOE_TB
    n_blocks = (T * TOP_K) // tb + N_EXPERTS
    x_pad, pstart, bexp, nused = _dispatch(counts[:, 0], route_i, x1_packed, tb, n_blocks)
    y_pad = _experts(bexp, nused, x_pad, w_gate, w_up, w_down, tb)
    out = _combine(pstart, route_i, x1, route_w.T, ln2_g[None, :], ln2_b[None, :], y_pad)
    return out.reshape(B, S, D)


def kernel(x, w_in, b_in, conv_w, conv_b, mlstm_norm_g, lambda_q1, lambda_k1, lambda_q2, lambda_k2,
           diff_norm_g, w_a, w_b, w_out, ln1_g, ln1_b, w_grp, b_grp, w_exp, b_exp,
           w_gate, w_up, w_down, ln2_g, ln2_b):
    for l in range(DEPTH):
        lambda_init = 0.8 - 0.6 * math.exp(-0.3 * l)
        x = _layer(x, w_in[l], b_in[l], conv_w[l], conv_b[l], mlstm_norm_g[l], lambda_q1[l], lambda_k1[l],
                   lambda_q2[l], lambda_k2[l], diff_norm_g[l], w_a[l], w_b[l], w_out[l], ln1_g[l], ln1_b[l],
                   w_grp[l], b_grp[l], w_exp[l], b_exp[l], w_gate[l], w_up[l], w_down[l], ln2_g[l], ln2_b[l],
                   lambda_init)
    return x
```

```python
import functools
import math

import jax
import jax.numpy as jnp
from jax import lax
from jax.experimental import pallas as pl
from jax.experimental.pallas import tpu as pltpu

F32 = jnp.float32
BF16 = jnp.bfloat16
I32 = jnp.int32
U32 = jnp.uint32

D_MODEL = 2048
A_HEADS = 4
A_HEAD_DIM = 256
A_WIDTH = A_HEADS * A_HEAD_DIM
CONV_WIDTH = 4
B_HEADS = 8
B_HEAD_DIM = 64
B_V_DIM = 2 * B_HEAD_DIM
B_WIDTH = B_HEADS * B_V_DIM
ATTN_CHUNK = 64
N_GROUPS = 4
EXPERTS_PER_GROUP = 8
N_EXPERTS = N_GROUPS * EXPERTS_PER_GROUP
TOP_K = 2
D_EXPERT = 512
DEPTH = 1
DEEPNORM_ALPHA = (2 * DEPTH) ** 0.25
LN_EPS = 1e-5
NEG_INF = float("-inf")
LOG2_E = 1.4426950408889634

LANES = 128
SUBLANES = 8
VMEM_LIMIT_BYTES = 56 * 1024 * 1024

COL_GA = 0
COL_GB = COL_GA + D_MODEL
COL_QA = COL_GB + D_MODEL
COL_KA = COL_QA + A_WIDTH
COL_VA = COL_KA + A_WIDTH
COL_OA = COL_VA + A_WIDTH
COL_QB = COL_OA + A_WIDTH
COL_KB = COL_QB + B_WIDTH
COL_VB = COL_KB + B_WIDTH
N_MAIN = COL_VB + B_WIDTH

PROJ_TM, PROJ_TN = 1024, 1024
MLSTM_CHUNK = 512
ATTN_TQ = 1024
ATTN_DIAG_GROUPS = 4
MIX_TM = 256
MOE_TB = 256
DISPATCH_TM = 256
COMBINE_TM = 256


def _cparams(sem, vmem=VMEM_LIMIT_BYTES):
    return pltpu.CompilerParams(dimension_semantics=sem, vmem_limit_bytes=vmem)


def _sigmoid(x):
    return 1.0 / (1.0 + jnp.exp(-x))


def _pack_halves(x):
    n = x.shape[1] // 2
    lo = lax.bitcast_convert_type(x[:, :n].astype(BF16).astype(F32), U32)
    hi = lax.bitcast_convert_type(x[:, n:].astype(BF16).astype(F32), U32)
    return (lo >> 16) | hi


def _unpack_halves(u):
    lo = lax.bitcast_convert_type(u << 16, F32)
    hi = lax.bitcast_convert_type(u & jnp.uint32(0xFFFF0000), F32)
    return lo, hi


TILE_WORDS = SUBLANES * LANES


def _rows_to_tiles(ref, v):
    rows = v.shape[0]
    for j in range(SUBLANES):
        ref[pl.ds(j, rows, stride=SUBLANES), :] = v[:, j * LANES:(j + 1) * LANES]


def _tiles_to_rows(ref):
    rows = ref.shape[0] // SUBLANES
    return jnp.concatenate([ref[pl.ds(j, rows, stride=SUBLANES), :] for j in range(SUBLANES)], axis=1)


def _split_bf16(a):
    hi = a.astype(BF16)
    return hi, (a - hi.astype(F32)).astype(BF16)


def _hi_lo_columns(w):
    return jnp.concatenate(_split_bf16(w), axis=1)


def _dot_3pass(a, w_hl):
    a_hi, a_lo = _split_bf16(a)
    n = w_hl.shape[1] // 2
    dot = functools.partial(jnp.dot, preferred_element_type=F32)
    r = dot(a_hi, w_hl)
    return r[:, :n] + (r[:, n:] + dot(a_lo, w_hl[:, :n]))


def _proj_kernel(x_ref, w_ref, b_ref, wg_ref, bg_ref, z_ref, zg_ref, xb_ref):
    @pl.when(pl.program_id(1) == 0)
    def _():
        x = x_ref[...]
        xb_ref[...] = x.astype(BF16)
        zg_ref[...] = _dot_3pass(x, wg_ref[...]) + bg_ref[...]

    acc = jnp.dot(xb_ref[...], w_ref[...], preferred_element_type=F32)
    z_ref[...] = (acc + b_ref[...]).astype(z_ref.dtype)


def _in_projection(x2d, w_main, b_main, w_gate, b_gate):
    T, K = x2d.shape
    N = w_main.shape[1]
    tm, tn = min(PROJ_TM, T), PROJ_TN
    return pl.pallas_call(
        _proj_kernel,
        grid=(T // tm, N // tn),
        in_specs=[
            pl.BlockSpec((tm, K), lambda i, j: (i, 0)),
            pl.BlockSpec((K, tn), lambda i, j: (0, j)),
            pl.BlockSpec((1, tn), lambda i, j: (0, j)),
            pl.BlockSpec((K, 2 * LANES), lambda i, j: (0, 0)),
            pl.BlockSpec((1, LANES), lambda i, j: (0, 0)),
        ],
        out_specs=[
            pl.BlockSpec((tm, tn), lambda i, j: (i, j)),
            pl.BlockSpec((tm, LANES), lambda i, j: (i, 0)),
        ],
        out_shape=[
            jax.ShapeDtypeStruct((T, N), BF16),
            jax.ShapeDtypeStruct((T, LANES), F32),
        ],
        scratch_shapes=[pltpu.VMEM((tm, K), BF16)],
        compiler_params=_cparams(("parallel", "arbitrary")),
        name="in_projection",
    )(x2d, w_main, b_main, w_gate, b_gate)


def _gate_prep_kernel(g_ref, o_ref, *, L):
    r = lax.broadcasted_iota(I32, (L, L), 0)
    c = lax.broadcasted_iota(I32, (L, L), 1)
    tri = (r <= c).astype(F32)
    row = lax.broadcasted_iota(I32, (g_ref.shape[0], L), 0)
    for ch in range(g_ref.shape[1] // L):
        g = g_ref[:, ch * L:(ch + 1) * L]
        lf = jnp.minimum(g, 0.0) - jnp.log(1.0 + jnp.exp(-jnp.abs(g)))
        b = jnp.dot(lf, tri, preferred_element_type=F32, precision=lax.Precision.HIGHEST)
        o_ref[:, ch * L:(ch + 1) * L] = jnp.where(row < A_HEADS, g, b)


def _gate_prep(g_rows, L):
    R, T = g_rows.shape
    per_step = math.gcd(T // L, 8)
    W = L * per_step
    return pl.pallas_call(
        functools.partial(_gate_prep_kernel, L=L),
        grid=(T // W,),
        in_specs=[pl.BlockSpec((R, W), lambda i: (0, i))],
        out_specs=pl.BlockSpec((R, W), lambda i: (0, i)),
        out_shape=jax.ShapeDtypeStruct((R, T), F32),
        compiler_params=_cparams(("parallel",)),
        name="gate_prep",
    )(g_rows)


def _mlstm_kernel(q_ref, k_ref, v_ref, o_ref, gcol_ref, grow_ref, cw_ref, cb_ref, ng_ref, out_ref,
                  c_ref, n_ref, m_ref, qcar_ref, kcar_ref):
    ci = pl.program_id(1)
    L = q_ref.shape[0]
    dh = A_HEAD_DIM

    @pl.when(ci == 0)
    def _():
        c_ref[...] = jnp.zeros_like(c_ref)
        n_ref[...] = jnp.zeros_like(n_ref)
        m_ref[...] = jnp.zeros_like(m_ref)
        qcar_ref[...] = jnp.zeros_like(qcar_ref)
        kcar_ref[...] = jnp.zeros_like(kcar_ref)

    def conv_silu(u_ref, car_ref, lo, wcol):
        u = u_ref[:, lo:lo + dh].astype(F32)
        ext = jnp.concatenate([car_ref[:, lo:lo + dh], u], axis=0)
        w = cw_ref[:, wcol:wcol + dh]
        y = cb_ref[:, wcol:wcol + dh] + w[3:4, :] * u
        for j in range(CONV_WIDTH - 1):
            off = SUBLANES - (CONV_WIDTH - 1) + j
            y = y + w[j:j + 1, :] * ext[off:off + L, :]
        car_ref[:, lo:lo + dh] = u[L - SUBLANES:, :]
        return y * _sigmoid(y)

    r = lax.broadcasted_iota(I32, (L, L), 0)
    c = lax.broadcasted_iota(I32, (L, L), 1)
    causal = r >= c
    gcol = gcol_ref[...]
    grow = grow_ref[...]

    for h in range(A_HEADS):
        lo = h * dh
        q = conv_silu(q_ref, qcar_ref, lo, lo)
        k = conv_silu(k_ref, kcar_ref, lo, A_WIDTH + lo) * (dh ** -0.5)
        v_b = v_ref[:, lo:lo + dh]
        q_b = q.astype(BF16)
        k_b = k.astype(BF16)

        i_col, b_col = gcol[:, h:h + 1], gcol[:, A_HEADS + h:A_HEADS + h + 1]
        i_row, b_row = grow[h:h + 1, :], grow[A_HEADS + h:A_HEADS + h + 1, :]
        b_last = b_row[:, L - 1:L]
        m_prev = m_ref[h]

        dmat = jnp.where(causal, b_col - b_row + i_row, NEG_INF)
        inter = b_col + m_prev
        m_t = jnp.maximum(inter, jnp.max(dmat, axis=-1, keepdims=True))
        s = lax.dot_general(q_b, k_b, (((1,), (1,)), ((), ())), preferred_element_type=F32)
        w_intra = jnp.exp(dmat - m_t) * s
        w_inter = jnp.exp(inter - m_t)
        qc = jnp.dot(q_b, c_ref[h].astype(BF16), preferred_element_type=F32)
        num = w_inter * qc + jnp.dot(w_intra.astype(BF16), v_b, preferred_element_type=F32)
        qn = jnp.sum(q * n_ref[h], axis=-1, keepdims=True)
        den = w_inter * qn + jnp.sum(w_intra, axis=-1, keepdims=True)
        hh = num / jnp.maximum(jnp.abs(den), jnp.exp(-m_t))

        g_col = b_last - b_col + i_col
        m_new = jnp.maximum(b_last + m_prev, jnp.max(g_col, axis=0, keepdims=True))
        decay = jnp.exp(b_last + m_prev - m_new)
        kw = jnp.exp(g_col - m_new) * k
        c_ref[h] = decay * c_ref[h] + jnp.dot(kw.T.astype(BF16), v_b, preferred_element_type=F32)
        n_ref[h] = decay * n_ref[h] + jnp.sum(kw, axis=0, keepdims=True)
        m_ref[h] = m_new

        y = _sigmoid(o_ref[:, lo:lo + dh].astype(F32)) * hh
        mu = jnp.mean(y, axis=-1, keepdims=True)
        yc = y - mu
        var = jnp.mean(yc * yc, axis=-1, keepdims=True)
        out_ref[:, lo:lo + dh] = (yc * lax.rsqrt(var + LN_EPS) * ng_ref[:, lo:lo + dh]).astype(out_ref.dtype)


def _mlstm(z_main, gcol, grow, conv_w, conv_b, norm_g, B, S):
    L = min(MLSTM_CHUNK, S)
    nc = S // L
    dh = A_HEAD_DIM
    H = A_HEADS
    W = A_WIDTH
    row = lambda b, c: b * nc + c
    full = lambda shape: pl.BlockSpec(shape, lambda b, c: (0, 0))
    return pl.pallas_call(
        _mlstm_kernel,
        grid=(B, nc),
        in_specs=[
            pl.BlockSpec((L, W), lambda b, c: (row(b, c), COL_QA // W)),
            pl.BlockSpec((L, W), lambda b, c: (row(b, c), COL_KA // W)),
            pl.BlockSpec((L, W), lambda b, c: (row(b, c), COL_VA // W)),
            pl.BlockSpec((L, W), lambda b, c: (row(b, c), COL_OA // W)),
            pl.BlockSpec((L, SUBLANES), lambda b, c: (row(b, c), 0)),
            pl.BlockSpec((SUBLANES, L), lambda b, c: (0, row(b, c))),
            full((CONV_WIDTH, 2 * W)), full((1, 2 * W)), full((1, W)),
        ],
        out_specs=pl.BlockSpec((L, W), lambda b, c: (row(b, c), 0)),
        out_shape=jax.ShapeDtypeStruct((B * S, W), BF16),
        scratch_shapes=[
            pltpu.VMEM((H, dh, dh), F32),
            pltpu.VMEM((H, 1, dh), F32),
            pltpu.VMEM((H, 1, 1), F32),
            pltpu.VMEM((SUBLANES, W), F32),
            pltpu.VMEM((SUBLANES, W), F32),
        ],
        compiler_params=_cparams(("parallel", "arbitrary")),
        name="mlstm",
    )(z_main, z_main, z_main, z_main, gcol, grow, conv_w, conv_b, norm_g)


def _lane_tiles(t):
    return [t[:, i * LANES:(i + 1) * LANES] for i in range(t.shape[1] // LANES)]


def _attn_kernel(slope_ref, q_ref, k_ref, kb_ref, v_ref, lq1_ref, lk1_ref, lq2_ref, lk2_ref, g_ref, o_ref,
                 s_ref, mx_ref, acc_ref, corr_ref, *, lambda_init):
    h = pl.program_id(1)
    qi = pl.program_id(2)
    tq = q_ref.shape[0]
    tk = tq
    slope = slope_ref[h]
    nt = (((1,), (1,)), ((), ()))

    lane = lax.broadcasted_iota(I32, (tq, B_V_DIM), 1)
    q = q_ref[...] * (B_HEAD_DIM ** -0.5)
    zero = jnp.zeros_like(q)
    bias_cols = jnp.where(lane < 2, 1.0, 0.0).astype(BF16)
    qa = (jnp.concatenate([jnp.where(lane < B_HEAD_DIM, q, zero), bias_cols], axis=1),
          jnp.concatenate([jnp.where(lane >= B_HEAD_DIM, q, zero), bias_cols], axis=1))

    def scores(j):
        start = pl.multiple_of(j * tk, tk)
        ka = jnp.concatenate([k_ref[pl.ds(start, tk), :], kb_ref[pl.ds(start, tk), :]], axis=1)
        return [lax.dot_general(qa[n], ka, nt, preferred_element_type=F32) for n in range(2)]

    def keep(j, n, t):
        t = t * LOG2_E
        s_ref[n, j] = t
        mx_ref[n] = functools.reduce(jnp.maximum, _lane_tiles(t), mx_ref[n])

    mx_ref[...] = jnp.full(mx_ref.shape, NEG_INF, F32)

    def pass_a(j):
        t = scores(j)
        for n in range(2):
            keep(j, n, t[n])

    def pass_a_pair(jj, carry):
        pass_a(2 * jj)
        pass_a(2 * jj + 1)
        return carry
    lax.fori_loop(0, qi // 2, pass_a_pair, 0)

    @pl.when(qi % 2 == 1)
    def _():
        pass_a(qi - 1)

    @pl.when(qi == 0)
    def _():
        r = lax.broadcasted_iota(I32, (tq, tk), 0)
        c = lax.broadcasted_iota(I32, (tq, tk), 1)
        ahead = jnp.where(c <= r, 0.0, (r - c).astype(F32) * (2.0 * slope))
        corr_ref[...] = jnp.where((c // ATTN_CHUNK) <= (r // ATTN_CHUNK), ahead, NEG_INF)

    rg = tq // ATTN_DIAG_GROUPS
    dstart = pl.multiple_of(qi * tk, tk)
    ka = jnp.concatenate([k_ref[pl.ds(dstart, tk), :], kb_ref[pl.ds(dstart, tk), :]], axis=1)
    for n in range(2):
        bands = []
        for g in range(ATTN_DIAG_GROUPS):
            kend = (g + 1) * rg
            band = lax.dot_general(qa[n][g * rg:kend], ka[:kend], nt, preferred_element_type=F32)
            if kend < tk:
                band = jnp.concatenate([band, jnp.full((rg, tk - kend), NEG_INF, F32)], axis=1)
            bands.append(band)
        keep(qi, n, jnp.concatenate(bands, axis=0) + corr_ref[...])

    for n in range(2):
        m = jnp.max(mx_ref[n], axis=-1, keepdims=True)
        mx_ref[n] = jnp.broadcast_to(m, (tq, LANES))
    acc_ref[...] = jnp.zeros_like(acc_ref)

    def weigh(n, rows, s_tile, v_tile):
        mb = mx_ref[n, rows, :]
        p = jnp.concatenate([jnp.exp2(t - mb) for t in _lane_tiles(s_tile)], axis=1).astype(BF16)
        v_ones = jnp.concatenate([v_tile, jnp.ones((v_tile.shape[0], LANES), BF16)], axis=1)
        acc_ref[n, rows, :] += jnp.dot(p, v_ones, preferred_element_type=F32)

    def pass_b(j, carry):
        start = pl.multiple_of(j * tk, tk)
        vj = v_ref[pl.ds(start, tk), :]
        for n in range(2):
            weigh(n, slice(None), s_ref[n, j], vj)
        return carry
    lax.fori_loop(0, qi, pass_b, 0)

    for n in range(2):
        for g in range(ATTN_DIAG_GROUPS):
            kend = (g + 1) * rg
            weigh(n, slice(g * rg, kend), s_ref[n, qi, g * rg:kend, :kend], v_ref[pl.ds(dstart, kend), :])

    lam = (jnp.exp(jnp.sum(lq1_ref[...] * lk1_ref[...], axis=-1, keepdims=True))
           - jnp.exp(jnp.sum(lq2_ref[...] * lk2_ref[...], axis=-1, keepdims=True)) + lambda_init)
    dv = B_V_DIM
    a0, a1 = acc_ref[0], acc_ref[1]
    o = a0[:, :dv] / a0[:, dv:dv + 1] - lam * (a1[:, :dv] / a1[:, dv:dv + 1])
    ms = jnp.mean(o * o, axis=-1, keepdims=True)
    o_ref[...] = (o * lax.rsqrt(ms + LN_EPS) * g_ref[...] * (1.0 - lambda_init)).astype(o_ref.dtype)


def _diff_attention(z_main, lq1, lk1, lq2, lk2, norm_g, B, S, lambda_init):
    tq = min(ATTN_TQ, S)
    nq = S // tq
    H = B_HEADS
    dv = B_V_DIM
    slopes = 2.0 ** (-8.0 * jnp.arange(1, H + 1, dtype=F32) / H)
    assert S <= 256 * 256
    pos = lax.broadcasted_iota(I32, (H, S, dv), 1)
    col = lax.broadcasted_iota(I32, (H, S, dv), 2)
    within = pos % 256
    kbias = jnp.where(col == 0, within, jnp.where(col == 1, pos - within, 0)).astype(F32)
    kbias = (kbias * slopes[:, None, None]).astype(BF16)
    small = pl.BlockSpec((1, B_HEAD_DIM), lambda b, h, i: (0, 0))
    return pl.pallas_call(
        functools.partial(_attn_kernel, lambda_init=lambda_init),
        grid=(B, H, nq),
        in_specs=[
            pl.BlockSpec(memory_space=pltpu.SMEM),
            pl.BlockSpec((tq, dv), lambda b, h, i: (b * nq + i, COL_QB // dv + h)),
            pl.BlockSpec((S, dv), lambda b, h, i: (b, COL_KB // dv + h)),
            pl.BlockSpec((None, S, dv), lambda b, h, i: (h, 0, 0)),
            pl.BlockSpec((S, dv), lambda b, h, i: (b, COL_VB // dv + h)),
            small, small, small, small,
            pl.BlockSpec((1, dv), lambda b, h, i: (0, 0)),
        ],
        out_specs=pl.BlockSpec((tq, dv), lambda b, h, i: (b * nq + i, h)),
        out_shape=jax.ShapeDtypeStruct((B * S, B_WIDTH), BF16),
        scratch_shapes=[
            pltpu.VMEM((2, nq, tq, tq), F32),
            pltpu.VMEM((2, tq, LANES), F32),
            pltpu.VMEM((2, tq, dv + LANES), F32),
            pltpu.VMEM((tq, tq), F32),
        ],
        compiler_params=_cparams(("parallel", "parallel", "arbitrary")),
        name="diff_attention",
    )(slopes, z_main, z_main, kbias, z_main, lq1, lk1, lq2, lk2, norm_g)


def _mix_kernel(ha_ref, ob_ref, ga_ref, gb_ref, x_ref, wa_ref, wb_ref, wo_ref, g1_ref, b1_ref, wr_ref, br_ref,
                x1_ref, x1p_ref, ri_ref, rw_ref, cnt_ref, carry_ref):
    i = pl.program_id(0)
    tm = x_ref.shape[0]

    @pl.when(i == 0)
    def _():
        carry_ref[...] = jnp.zeros_like(carry_ref)

    ya = jnp.dot(ha_ref[...], wa_ref[...], preferred_element_type=F32)
    yb = jnp.dot(ob_ref[...], wb_ref[...], preferred_element_type=F32)
    merged = _sigmoid(ga_ref[...].astype(F32)) * ya + _sigmoid(gb_ref[...].astype(F32)) * yb
    mix = jnp.dot(merged.astype(BF16), wo_ref[...], preferred_element_type=F32)
    pre = DEEPNORM_ALPHA * x_ref[...] + mix
    mu = jnp.mean(pre, axis=-1, keepdims=True)
    pc = pre - mu
    var = jnp.mean(pc * pc, axis=-1, keepdims=True)
    x1 = pc * lax.rsqrt(var + LN_EPS) * g1_ref[...] + b1_ref[...]
    x1_ref[...] = x1
    _rows_to_tiles(x1p_ref, _pack_halves(x1))

    logits = _dot_3pass(x1, wr_ref[...]) + br_ref[...]
    lt = logits.T
    row8 = lax.broadcasted_iota(I32, (SUBLANES, tm), 0).astype(F32)
    gl = jnp.where(row8 < N_GROUPS, lt[0:SUBLANES, :], NEG_INF)
    gmax = jnp.max(gl, axis=0, keepdims=True)
    gsel = jnp.min(jnp.where(gl == gmax, row8, float(SUBLANES)), axis=0, keepdims=True)
    gprob = 1.0 / jnp.sum(jnp.exp(gl - gmax), axis=0, keepdims=True)
    ing = lt[SUBLANES:2 * SUBLANES, :]
    for g in range(1, N_GROUPS):
        ing = jnp.where(gsel == g, lt[(g + 1) * SUBLANES:(g + 2) * SUBLANES, :], ing)
    v0 = jnp.max(ing, axis=0, keepdims=True)
    i0 = jnp.min(jnp.where(ing == v0, row8, float(SUBLANES)), axis=0, keepdims=True)
    ing2 = jnp.where(row8 == i0, -jnp.inf, ing)
    v1 = jnp.max(ing2, axis=0, keepdims=True)
    i1 = jnp.min(jnp.where(ing2 == v1, row8, float(SUBLANES)), axis=0, keepdims=True)
    ex = jnp.exp(v1 - v0)
    inv = 1.0 / (1.0 + ex)
    w0 = gprob * inv
    w1 = gprob * (ex * inv)
    e0 = (gsel * EXPERTS_PER_GROUP + i0).astype(I32)
    e1 = (gsel * EXPERTS_PER_GROUP + i1).astype(I32)

    rowe = lax.broadcasted_iota(I32, (N_EXPERTS, tm), 0)
    is0 = rowe == e0
    is1 = rowe == e1
    oh = jnp.where(is0, 1.0, 0.0) + jnp.where(is1, 1.0, 0.0)
    tr = lax.broadcasted_iota(I32, (tm, tm), 0)
    tc = lax.broadcasted_iota(I32, (tm, tm), 1)
    su = jnp.where(tr < tc, 1.0, 0.0).astype(BF16)
    tot = jnp.dot(oh.astype(BF16), su, preferred_element_type=F32) + carry_ref[...]
    rank0 = jnp.sum(jnp.where(is0, tot, 0.0), axis=0, keepdims=True)
    rank1 = jnp.sum(jnp.where(is1, tot, 0.0), axis=0, keepdims=True)
    carry = carry_ref[...] + jnp.sum(oh, axis=1, keepdims=True)
    carry_ref[...] = carry
    cnt_ref[...] = jnp.broadcast_to(carry, cnt_ref.shape).astype(I32)

    zi = jnp.zeros((1, tm), I32)
    ri_ref[...] = jnp.concatenate([e0, e1, rank0.astype(I32), rank1.astype(I32), zi, zi, zi, zi], axis=0)
    zf = jnp.zeros((1, tm), F32)
    rw_ref[...] = jnp.concatenate([w0, w1, zf, zf, zf, zf, zf, zf], axis=0)


def _mix(h_a, o_b, z_main, x2d, w_a, w_b, w_out, ln_g, ln_b, w_r, b_r):
    T, D = x2d.shape
    tm = min(MIX_TM, T)
    const = lambda shape: pl.BlockSpec(shape, lambda i: (0, 0), pipeline_mode=pl.Buffered(1))
    return pl.pallas_call(
        _mix_kernel,
        grid=(T // tm,),
        in_specs=[
            pl.BlockSpec((tm, A_WIDTH), lambda i: (i, 0)),
            pl.BlockSpec((tm, B_WIDTH), lambda i: (i, 0)),
            pl.BlockSpec((tm, D), lambda i: (i, COL_GA // D)),
            pl.BlockSpec((tm, D), lambda i: (i, COL_GB // D)),
            pl.BlockSpec((tm, D), lambda i: (i, 0)),
            const((A_WIDTH, D)), const((B_WIDTH, D)), const((D, D)),
            const((1, D)), const((1, D)), const((D, 2 * LANES)), const((1, LANES)),
        ],
        out_specs=[
            pl.BlockSpec((tm, D), lambda i: (i, 0)),
            pl.BlockSpec((tm * SUBLANES, LANES), lambda i: (i, 0)),
            pl.BlockSpec((SUBLANES, tm), lambda i: (0, i)),
            pl.BlockSpec((SUBLANES, tm), lambda i: (0, i)),
            pl.BlockSpec((N_EXPERTS, LANES), lambda i: (0, 0)),
        ],
        out_shape=[
            jax.ShapeDtypeStruct((T, D), F32),
            jax.ShapeDtypeStruct((T * SUBLANES, LANES), U32),
            jax.ShapeDtypeStruct((SUBLANES, T), I32),
            jax.ShapeDtypeStruct((SUBLANES, T), F32),
            jax.ShapeDtypeStruct((N_EXPERTS, LANES), I32),
        ],
        scratch_shapes=[pltpu.VMEM((N_EXPERTS, 1), F32)],
        compiler_params=_cparams(("arbitrary",)),
        name="mix_ln1_router",
    )(h_a, o_b, z_main, z_main, x2d, w_a, w_b, w_out, ln_g, ln_b, w_r, b_r)


def _row_copy(src_ref, src_row, dst_ref, dst_row, sem):
    return pltpu.make_async_copy(src_ref.at[pl.ds(src_row * SUBLANES, SUBLANES)],
                                 dst_ref.at[pl.ds(dst_row * SUBLANES, SUBLANES)], sem)


def _rows_wait(src_ref, dst_ref, rows, sem):
    n = rows * SUBLANES
    pltpu.make_async_copy(src_ref.at[pl.ds(0, n)], dst_ref.at[pl.ds(0, n)], sem).wait()


DISPATCH_SLOTS = 3


def _dispatch_kernel(cnt_ref, ri_ref, x1_ref, xpad_ref, pstart_ref, bexp_ref, nused_ref,
                     pst_ref, zero_ref, xbuf_ref, sem, lsem, zsem, *, tb, n_blocks, tm):
    i = pl.program_id(0)
    last = pl.num_programs(0) - 1

    @pl.when(i == 0)
    def _():
        def fill(b, carry):
            bexp_ref[b] = 0
            return carry
        lax.fori_loop(0, n_blocks, fill, 0)

        def per_expert(e, blk):
            pst_ref[e] = blk * tb
            pstart_ref[e] = blk * tb
            nb = (cnt_ref[e] + (tb - 1)) // tb

            def mark(b, carry):
                bexp_ref[blk + b] = e
                return carry
            lax.fori_loop(0, nb, mark, 0)
            return blk + nb
        nused = lax.fori_loop(0, N_EXPERTS, per_expert, 0)
        nused_ref[0] = nused
        pst_ref[N_EXPERTS] = nused

    rows = tm * SUBLANES
    slot = i % DISPATCH_SLOTS

    def tile_load(tile, s):
        src = x1_ref.at[pl.ds(pl.multiple_of(tile * rows, rows), rows)]
        return pltpu.make_async_copy(src, xbuf_ref.at[s], lsem.at[s])

    def wait_rows(s):
        for k in range(TOP_K):
            _rows_wait(xbuf_ref.at[s], xpad_ref, tm, sem.at[s])

    @pl.when(i == 0)
    def _():
        tile_load(0, 0).start()

        @pl.when(last >= 1)
        def _():
            tile_load(1, 1).start()

    tile_load(i, slot).wait()
    for t in range(tm):
        for k in range(TOP_K):
            pos = pst_ref[ri_ref[k, t]] + ri_ref[TOP_K + k, t]
            _row_copy(xbuf_ref.at[slot], t, xpad_ref, pos, sem.at[slot]).start(priority=k)

    prev = (i + DISPATCH_SLOTS - 1) % DISPATCH_SLOTS

    @pl.when(i >= 1)
    def _():
        wait_rows(prev)

    @pl.when(i + 2 <= last)
    def _():
        tile_load(i + 2, prev).start()

    @pl.when(i == last)
    def _():
        wait_rows(slot)
        zero_ref[...] = jnp.zeros_like(zero_ref)

        def per_expert(e, carry):
            cnt = cnt_ref[e]
            first = pst_ref[e] + cnt
            npad = ((cnt + (tb - 1)) // tb) * tb - cnt

            def start(p, c2):
                _row_copy(zero_ref, 0, xpad_ref, first + p, zsem).start()
                return c2
            lax.fori_loop(0, npad, start, 0)

            def wait(p, c2):
                _row_copy(zero_ref, 0, xpad_ref, 0, zsem).wait()
                return c2
            lax.fori_loop(0, npad, wait, 0)
            return carry
        lax.fori_loop(0, N_EXPERTS, per_expert, 0)

        def block_copy(b):
            n = tb * SUBLANES
            return pltpu.make_async_copy(zero_ref, xpad_ref.at[pl.ds(pl.multiple_of(b * n, n), n)], zsem)

        def start_block(b, carry):
            block_copy(b).start()
            return carry
        lax.fori_loop(pst_ref[N_EXPERTS], n_blocks, start_block, 0)

        def wait_block(b, carry):
            block_copy(b).wait()
            return carry
        lax.fori_loop(pst_ref[N_EXPERTS], n_blocks, wait_block, 0)


def _dispatch(counts, route_i, x1, tb, n_blocks):
    T = x1.shape[0] // SUBLANES
    tm = min(DISPATCH_TM, T)
    smem_full = pl.BlockSpec(memory_space=pltpu.SMEM)
    return pl.pallas_call(
        functools.partial(_dispatch_kernel, tb=tb, n_blocks=n_blocks, tm=tm),
        grid=(T // tm,),
        in_specs=[
            smem_full,
            pl.BlockSpec((SUBLANES, tm), lambda i: (0, i), memory_space=pltpu.SMEM),
            pl.BlockSpec(memory_space=pl.ANY),
        ],
        out_specs=[
            pl.BlockSpec(memory_space=pl.ANY),
            smem_full, smem_full, smem_full,
        ],
        out_shape=[
            jax.ShapeDtypeStruct((n_blocks * tb * SUBLANES, LANES), x1.dtype),
            jax.ShapeDtypeStruct((N_EXPERTS,), I32),
            jax.ShapeDtypeStruct((n_blocks,), I32),
            jax.ShapeDtypeStruct((1,), I32),
        ],
        scratch_shapes=[
            pltpu.SMEM((N_EXPERTS + 1,), I32),
            pltpu.VMEM((tb * SUBLANES, LANES), x1.dtype),
            pltpu.VMEM((DISPATCH_SLOTS, tm * SUBLANES, LANES), x1.dtype),
            pltpu.SemaphoreType.DMA((DISPATCH_SLOTS,)),
            pltpu.SemaphoreType.DMA((DISPATCH_SLOTS,)),
            pltpu.SemaphoreType.DMA(()),
        ],
        compiler_params=_cparams(("arbitrary",)),
        name="moe_dispatch",
    )(counts, route_i, x1)


def _expert_kernel(bexp_ref, nused_ref, x_ref, wg_hbm, wu_hbm, wd_hbm, y_ref,
                   wgf_ref, wuf_ref, wdf_ref, wgb_ref, wub_ref, wdb_ref, sem, run_ref):
    b = pl.program_id(0)
    nused = nused_ref[0]
    used = b < nused
    e = bexp_ref[b]
    new_expert = jnp.logical_or(b == 0, e != bexp_ref[jnp.maximum(b - 1, 0)])
    streams = ((wg_hbm, wgf_ref), (wu_hbm, wuf_ref), (wd_hbm, wdf_ref))

    def weight_copies(expert, slot):
        return [pltpu.make_async_copy(w_hbm.at[expert], wf_ref.at[slot], sem.at[slot, j])
                for j, (w_hbm, wf_ref) in enumerate(streams)]

    @pl.when(jnp.logical_and(used, b == 0))
    def _():
        run_ref[0] = 0
        for cp in weight_copies(e, 0):
            cp.start()

    @pl.when(jnp.logical_and(used, new_expert))
    def _():
        slot = run_ref[0] % 2
        nxt = lax.while_loop(lambda j: jnp.logical_and(j < nused, bexp_ref[jnp.minimum(j, nused - 1)] == e),
                             lambda j: j + 1, b + 1)

        @pl.when(nxt < nused)
        def _():
            for cp in weight_copies(bexp_ref[jnp.minimum(nxt, nused - 1)], 1 - slot):
                cp.start()

        for cp in weight_copies(e, slot):
            cp.wait()
        wgb_ref[...] = wgf_ref[slot].astype(BF16)
        wub_ref[...] = wuf_ref[slot].astype(BF16)
        wdb_ref[...] = wdf_ref[slot].astype(BF16)
        run_ref[0] = run_ref[0] + 1

    @pl.when(used)
    def _():
        half = SUBLANES * LANES
        x_lo, x_hi = (t.astype(BF16) for t in _unpack_halves(_tiles_to_rows(x_ref)))
        dot = functools.partial(jnp.dot, preferred_element_type=F32)
        gate = dot(x_lo, wgb_ref[:half, :]) + dot(x_hi, wgb_ref[half:, :])
        up = dot(x_lo, wub_ref[:half, :]) + dot(x_hi, wub_ref[half:, :])
        hid = (gate * _sigmoid(gate) * up).astype(BF16)
        _rows_to_tiles(y_ref, _pack_halves(dot(hid, wdb_ref[...])))

    @pl.when(jnp.logical_not(used))
    def _():
        y_ref[...] = jnp.zeros_like(y_ref)


def _experts(bexp, nused, x_pad, w_gate, w_up, w_down, tb):
    P = x_pad.shape[0] // SUBLANES
    D = 2 * TILE_WORDS
    n_blocks = P // tb
    rows = (tb * SUBLANES, LANES)
    F = w_gate.shape[-1]
    blk = lambda b, be, nu: jnp.maximum(jnp.minimum(b, nu[0] - 1), 0)
    hbm = pl.BlockSpec(memory_space=pl.ANY)
    return pl.pallas_call(
        _expert_kernel,
        grid_spec=pltpu.PrefetchScalarGridSpec(
            num_scalar_prefetch=2,
            grid=(n_blocks,),
            in_specs=[pl.BlockSpec(rows, lambda b, be, nu: (blk(b, be, nu), 0)), hbm, hbm, hbm],
            out_specs=pl.BlockSpec(rows, lambda b, be, nu: (b, 0)),
            scratch_shapes=[
                pltpu.VMEM((2, D, F), F32), pltpu.VMEM((2, D, F), F32), pltpu.VMEM((2, F, D), F32),
                pltpu.VMEM((D, F), BF16), pltpu.VMEM((D, F), BF16), pltpu.VMEM((F, D), BF16),
                pltpu.SemaphoreType.DMA((2, 3)),
                pltpu.SMEM((1,), I32),
            ],
        ),
        out_shape=jax.ShapeDtypeStruct(x_pad.shape, U32),
        compiler_params=_cparams(("arbitrary",)),
        name="moe_experts",
    )(bexp, nused, x_pad, w_gate, w_up, w_down)


COMBINE_SLOTS = 3


def _combine_kernel(pstart_ref, ri_ref, r1_ref, r2_ref, x1_ref, rw_ref, g_ref, b_ref, ypad_ref, o_ref, ybuf_ref, sem):
    i = pl.program_id(0)
    n = pl.num_programs(0)
    tm = x1_ref.shape[0]
    slot = i % COMBINE_SLOTS

    def gather_loop(route_ref, s):
        def issue(t, carry):
            for k in range(TOP_K):
                pos = pstart_ref[route_ref[k, t]] + route_ref[TOP_K + k, t]
                _row_copy(ypad_ref, pos, ybuf_ref.at[s, k], t, sem.at[s]).start(priority=k)
            return carry
        lax.fori_loop(0, tm, issue, 0, unroll=8)

    def wait_slot(s):
        for k in range(TOP_K):
            _rows_wait(ypad_ref, ybuf_ref.at[s, k], tm, sem.at[s])

    @pl.when(i == 0)
    def _():
        gather_loop(ri_ref, 0)
        gather_loop(r1_ref, 1)

    wait_slot(slot)

    rw = rw_ref[...]
    y0 = _unpack_halves(_tiles_to_rows(ybuf_ref.at[slot, 0]))
    y1 = _unpack_halves(_tiles_to_rows(ybuf_ref.at[slot, 1]))
    ffn = jnp.concatenate([y0[h] * rw[:, 0:1] + y1[h] * rw[:, 1:2] for h in range(2)], axis=1)
    pre = DEEPNORM_ALPHA * x1_ref[...] + ffn
    mu = jnp.mean(pre, axis=-1, keepdims=True)
    pc = pre - mu
    var = jnp.mean(pc * pc, axis=-1, keepdims=True)
    o_ref[...] = pc * lax.rsqrt(var + LN_EPS) * g_ref[...] + b_ref[...]

    nxt = (i + 2) % COMBINE_SLOTS
    for t in range(tm):
        for k in range(TOP_K):
            pos = pstart_ref[r2_ref[k, t]] + r2_ref[TOP_K + k, t]
            _row_copy(ypad_ref, pos, ybuf_ref.at[nxt, k], t, sem.at[nxt]).start(priority=k)

    @pl.when(i == n - 1)
    def _():
        wait_slot((i + 1) % COMBINE_SLOTS)
        wait_slot(nxt)


def _combine(pstart, route_i, x1, rw_col, ln_g, ln_b, y_pad):
    T, D = x1.shape
    tm = min(COMBINE_TM, T)
    n = T // tm
    return pl.pallas_call(
        _combine_kernel,
        grid=(n,),
        in_specs=[
            pl.BlockSpec(memory_space=pltpu.SMEM),
            pl.BlockSpec((SUBLANES, tm), lambda i: (0, i), memory_space=pltpu.SMEM),
            pl.BlockSpec((SUBLANES, tm), lambda i: (0, jnp.minimum(i + 1, n - 1)), memory_space=pltpu.SMEM),
            pl.BlockSpec((SUBLANES, tm), lambda i: (0, jnp.minimum(i + 2, n - 1)), memory_space=pltpu.SMEM),
            pl.BlockSpec((tm, D), lambda i: (i, 0)),
            pl.BlockSpec((tm, SUBLANES), lambda i: (i, 0)),
            pl.BlockSpec((1, D), lambda i: (0, 0)),
            pl.BlockSpec((1, D), lambda i: (0, 0)),
            pl.BlockSpec(memory_space=pl.ANY),
        ],
        out_specs=pl.BlockSpec((tm, D), lambda i: (i, 0)),
        out_shape=jax.ShapeDtypeStruct((T, D), F32),
        scratch_shapes=[
            pltpu.VMEM((COMBINE_SLOTS, TOP_K, tm * SUBLANES, LANES), U32),
            pltpu.SemaphoreType.DMA((COMBINE_SLOTS,)),
        ],
        compiler_params=_cparams(("arbitrary",)),
        name="moe_combine_ln2",
    )(pstart, route_i, route_i, route_i, x1, rw_col, ln_g, ln_b, y_pad)


N_GATE_COLS = 2 * A_HEADS
SRC_MLSTM = 0
SRC_GATES = 4 * A_WIDTH
SRC_DIFF = SRC_GATES + N_GATE_COLS
SRC_MERGE = SRC_DIFF + 3 * B_WIDTH
PREP_TN = 512


def _wprep_kernel(a_ref, o_ref):
    o_ref[...] = a_ref[...].T.astype(BF16)


def _wgate_kernel(a_ref, o_ref):
    a = a_ref[...]
    rows = jnp.concatenate([a, jnp.zeros((LANES - a.shape[0], a.shape[1]), F32)], axis=0)
    o_ref[...] = rows.T


def _rearrange_in_proj(w_in, b_in):
    K, n_in = w_in.shape
    tn = PREP_TN
    merge_blocks = (2 * D_MODEL) // tn
    mlstm_blocks = (4 * A_WIDTH) // tn
    assert SRC_MERGE % SUBLANES == 0 and SRC_DIFF % SUBLANES == 0

    def src_row(jb):
        s = SUBLANES
        merge = SRC_MERGE // s + (tn // s) * jb
        mlstm = SRC_MLSTM // s + (tn // s) * (jb - merge_blocks)
        diff = SRC_DIFF // s + (tn // s) * (jb - merge_blocks - mlstm_blocks)
        return s * jnp.where(jb < merge_blocks, merge, jnp.where(jb < merge_blocks + mlstm_blocks, mlstm, diff))

    w_t = jnp.swapaxes(w_in, 0, 1)
    w_main = pl.pallas_call(
        _wprep_kernel,
        grid=(N_MAIN // tn,),
        in_specs=[pl.BlockSpec((pl.Element(tn), pl.Element(K)), lambda jb: (src_row(jb), 0))],
        out_specs=pl.BlockSpec((K, tn), lambda jb: (0, jb)),
        out_shape=jax.ShapeDtypeStruct((K, N_MAIN), BF16),
        compiler_params=_cparams(("parallel",)),
        name="w_in_prep",
    )(w_t)
    b_main = jnp.concatenate([b_in[SRC_MERGE:], b_in[SRC_MLSTM:SRC_GATES], b_in[SRC_DIFF:SRC_MERGE]])[None, :]
    w_gate = pl.pallas_call(
        _wgate_kernel,
        grid=(1,),
        in_specs=[pl.BlockSpec((pl.Element(N_GATE_COLS), pl.Element(K)), lambda i: (SRC_GATES, 0))],
        out_specs=pl.BlockSpec((K, LANES), lambda i: (0, 0)),
        out_shape=jax.ShapeDtypeStruct((K, LANES), F32),
        name="w_gate_prep",
    )(w_t)
    b_gate = jnp.pad(b_in[SRC_GATES:SRC_DIFF], (0, LANES - N_GATE_COLS))[None, :]
    return w_main, b_main, _hi_lo_columns(w_gate), b_gate


def _layer(x, w_in, b_in, conv_w, conv_b, norm_a_g, lq1, lk1, lq2, lk2, norm_b_g, w_a, w_b, w_out,
           ln1_g, ln1_b, w_grp, b_grp, w_exp, b_exp, w_gate, w_up, w_down, ln2_g, ln2_b, lambda_init):
    B, S, D = x.shape
    T = B * S
    x2d = x.reshape(T, D)

    w_main, b_main, w_g, b_g = _rearrange_in_proj(w_in, b_in)
    z_main, z_gate = _in_projection(x2d, w_main, b_main, w_g, b_g)

    L = min(MLSTM_CHUNK, S)
    gp = _gate_prep(z_gate[:, :2 * A_HEADS].T, L)
    h_a = _mlstm(z_main, gp.T, gp, conv_w, conv_b[None, :], norm_a_g[None, :], B, S)
    o_b = _diff_attention(z_main, lq1[None, :], lk1[None, :], lq2[None, :], lk2[None, :],
                          norm_b_g[None, :], B, S, lambda_init)

    w_r = jnp.zeros((D, LANES), F32).at[:, :N_GROUPS].set(w_grp).at[:, SUBLANES:SUBLANES + N_EXPERTS].set(w_exp)
    b_r = jnp.zeros((LANES,), F32).at[:N_GROUPS].set(b_grp).at[SUBLANES:SUBLANES + N_EXPERTS].set(b_exp)[None, :]
    x1, x1_packed, route_i, route_w, counts = _mix(h_a, o_b, z_main, x2d, w_a.astype(BF16), w_b.astype(BF16),
                                                   w_out.astype(BF16), ln1_g[None, :], ln1_b[None, :],
                                                   _hi_lo_columns(w_r), b_r)

    tb = MOE_TB
    n_blocks = (T * TOP_K) // tb + N_EXPERTS
    x_pad, pstart, bexp, nused = _dispatch(counts[:, 0], route_i, x1_packed, tb, n_blocks)
    y_pad = _experts(bexp, nused, x_pad, w_gate, w_up, w_down, tb)
    out = _combine(pstart, route_i, x1, route_w.T, ln2_g[None, :], ln2_b[None, :], y_pad)
    return out.reshape(B, S, D)


def kernel(x, w_in, b_in, conv_w, conv_b, mlstm_norm_g, lambda_q1, lambda_k1, lambda_q2, lambda_k2,
           diff_norm_g, w_a, w_b, w_out, ln1_g, ln1_b, w_grp, b_grp, w_exp, b_exp,
           w_gate, w_up, w_down, ln2_g, ln2_b):
    for l in range(DEPTH):
        lambda_init = 0.8 - 0.6 * math.exp(-0.3 * l)
        x = _layer(x, w_in[l], b_in[l], conv_w[l], conv_b[l], mlstm_norm_g[l], lambda_q1[l], lambda_k1[l],
                   lambda_q2[l], lambda_k2[l], diff_norm_g[l], w_a[l], w_b[l], w_out[l], ln1_g[l], ln1_b[l],
                   w_grp[l], b_grp[l], w_exp[l], b_exp[l], w_gate[l], w_up[l], w_down[l], ln2_g[l], ln2_b[l],
                   lambda_init)
    return x
```

```python
import functools
import math

import jax
import jax.numpy as jnp
from jax import lax
from jax.experimental import pallas as pl
from jax.experimental.pallas import tpu as pltpu

F32 = jnp.float32
BF16 = jnp.bfloat16
I32 = jnp.int32
U32 = jnp.uint32

D_MODEL = 2048
A_HEADS = 4
A_HEAD_DIM = 256
A_WIDTH = A_HEADS * A_HEAD_DIM
CONV_WIDTH = 4
B_HEADS = 8
B_HEAD_DIM = 64
B_V_DIM = 2 * B_HEAD_DIM
B_WIDTH = B_HEADS * B_V_DIM
ATTN_CHUNK = 64
N_GROUPS = 4
EXPERTS_PER_GROUP = 8
N_EXPERTS = N_GROUPS * EXPERTS_PER_GROUP
TOP_K = 2
D_EXPERT = 512
DEPTH = 1
DEEPNORM_ALPHA = (2 * DEPTH) ** 0.25
LN_EPS = 1e-5
NEG_INF = float("-inf")
LOG2_E = 1.4426950408889634

LANES = 128
SUBLANES = 8
VMEM_LIMIT_BYTES = 56 * 1024 * 1024

COL_GA = 0
COL_GB = COL_GA + D_MODEL
COL_QA = COL_GB + D_MODEL
COL_KA = COL_QA + A_WIDTH
COL_VA = COL_KA + A_WIDTH
COL_OA = COL_VA + A_WIDTH
COL_QB = COL_OA + A_WIDTH
COL_KB = COL_QB + B_WIDTH
COL_VB = COL_KB + B_WIDTH
N_MAIN = COL_VB + B_WIDTH

PROJ_TM, PROJ_TN = 1024, 1024
MLSTM_CHUNK = 512
ATTN_TQ = 1024
ATTN_DIAG_GROUPS = 4
MIX_TM = 256
MOE_TB = 256
DISPATCH_TM = 256
COMBINE_TM = 256


def _cparams(sem, vmem=VMEM_LIMIT_BYTES):
    return pltpu.CompilerParams(dimension_semantics=sem, vmem_limit_bytes=vmem)


def _sigmoid(x):
    return 1.0 / (1.0 + jnp.exp(-x))


def _pack_halves(x):
    n = x.shape[1] // 2
    lo = lax.bitcast_convert_type(x[:, :n].astype(BF16).astype(F32), U32)
    hi = lax.bitcast_convert_type(x[:, n:].astype(BF16).astype(F32), U32)
    return (lo >> 16) | hi


def _unpack_halves(u):
    lo = lax.bitcast_convert_type(u << 16, F32)
    hi = lax.bitcast_convert_type(u & jnp.uint32(0xFFFF0000), F32)
    return lo, hi


TILE_WORDS = SUBLANES * LANES


def _rows_to_tiles(ref, v):
    rows = v.shape[0]
    for j in range(SUBLANES):
        ref[pl.ds(j, rows, stride=SUBLANES), :] = v[:, j * LANES:(j + 1) * LANES]


def _tiles_to_rows(ref):
    rows = ref.shape[0] // SUBLANES
    return jnp.concatenate([ref[pl.ds(j, rows, stride=SUBLANES), :] for j in range(SUBLANES)], axis=1)


def _split_bf16(a):
    hi = a.astype(BF16)
    return hi, (a - hi.astype(F32)).astype(BF16)


def _hi_lo_columns(w):
    return jnp.concatenate(_split_bf16(w), axis=1)


def _dot_3pass(a, w_hl):
    a_hi, a_lo = _split_bf16(a)
    n = w_hl.shape[1] // 2
    dot = functools.partial(jnp.dot, preferred_element_type=F32)
    r = dot(a_hi, w_hl)
    return r[:, :n] + (r[:, n:] + dot(a_lo, w_hl[:, :n]))


def _proj_kernel(x_ref, w_ref, b_ref, wg_ref, bg_ref, z_ref, zg_ref, xb_ref):
    @pl.when(pl.program_id(1) == 0)
    def _():
        x = x_ref[...]
        xb_ref[...] = x.astype(BF16)
        zg_ref[...] = _dot_3pass(x, wg_ref[...]) + bg_ref[...]

    acc = jnp.dot(xb_ref[...], w_ref[...], preferred_element_type=F32)
    z_ref[...] = (acc + b_ref[...]).astype(z_ref.dtype)


def _in_projection(x2d, w_main, b_main, w_gate, b_gate):
    T, K = x2d.shape
    N = w_main.shape[1]
    tm, tn = min(PROJ_TM, T), PROJ_TN
    return pl.pallas_call(
        _proj_kernel,
        grid=(T // tm, N // tn),
        in_specs=[
            pl.BlockSpec((tm, K), lambda i, j: (i, 0)),
            pl.BlockSpec((K, tn), lambda i, j: (0, j)),
            pl.BlockSpec((1, tn), lambda i, j: (0, j)),
            pl.BlockSpec((K, 2 * LANES), lambda i, j: (0, 0)),
            pl.BlockSpec((1, LANES), lambda i, j: (0, 0)),
        ],
        out_specs=[
            pl.BlockSpec((tm, tn), lambda i, j: (i, j)),
            pl.BlockSpec((tm, LANES), lambda i, j: (i, 0)),
        ],
        out_shape=[
            jax.ShapeDtypeStruct((T, N), BF16),
            jax.ShapeDtypeStruct((T, LANES), F32),
        ],
        scratch_shapes=[pltpu.VMEM((tm, K), BF16)],
        compiler_params=_cparams(("parallel", "arbitrary")),
        name="in_projection",
    )(x2d, w_main, b_main, w_gate, b_gate)


def _gate_prep_kernel(g_ref, o_ref, *, L):
    r = lax.broadcasted_iota(I32, (L, L), 0)
    c = lax.broadcasted_iota(I32, (L, L), 1)
    tri = (r <= c).astype(F32)
    row = lax.broadcasted_iota(I32, (g_ref.shape[0], L), 0)
    for ch in range(g_ref.shape[1] // L):
        g = g_ref[:, ch * L:(ch + 1) * L]
        lf = jnp.minimum(g, 0.0) - jnp.log(1.0 + jnp.exp(-jnp.abs(g)))
        b = jnp.dot(lf, tri, preferred_element_type=F32, precision=lax.Precision.HIGHEST)
        o_ref[:, ch * L:(ch + 1) * L] = jnp.where(row < A_HEADS, g, b)


def _gate_prep(g_rows, L):
    R, T = g_rows.shape
    per_step = math.gcd(T // L, 8)
    W = L * per_step
    return pl.pallas_call(
        functools.partial(_gate_prep_kernel, L=L),
        grid=(T // W,),
        in_specs=[pl.BlockSpec((R, W), lambda i: (0, i))],
        out_specs=pl.BlockSpec((R, W), lambda i: (0, i)),
        out_shape=jax.ShapeDtypeStruct((R, T), F32),
        compiler_params=_cparams(("parallel",)),
        name="gate_prep",
    )(g_rows)


def _mlstm_kernel(q_ref, k_ref, v_ref, o_ref, gcol_ref, grow_ref, cw_ref, cb_ref, ng_ref, out_ref,
                  c_ref, n_ref, m_ref, qcar_ref, kcar_ref):
    ci = pl.program_id(1)
    L = q_ref.shape[0]
    dh = A_HEAD_DIM

    @pl.when(ci == 0)
    def _():
        c_ref[...] = jnp.zeros_like(c_ref)
        n_ref[...] = jnp.zeros_like(n_ref)
        m_ref[...] = jnp.zeros_like(m_ref)
        qcar_ref[...] = jnp.zeros_like(qcar_ref)
        kcar_ref[...] = jnp.zeros_like(kcar_ref)

    def conv_silu(u_ref, car_ref, lo, wcol):
        u = u_ref[:, lo:lo + dh].astype(F32)
        ext = jnp.concatenate([car_ref[:, lo:lo + dh], u], axis=0)
        w = cw_ref[:, wcol:wcol + dh]
        y = cb_ref[:, wcol:wcol + dh] + w[3:4, :] * u
        for j in range(CONV_WIDTH - 1):
            off = SUBLANES - (CONV_WIDTH - 1) + j
            y = y + w[j:j + 1, :] * ext[off:off + L, :]
        car_ref[:, lo:lo + dh] = u[L - SUBLANES:, :]
        return y * _sigmoid(y)

    r = lax.broadcasted_iota(I32, (L, L), 0)
    c = lax.broadcasted_iota(I32, (L, L), 1)
    causal = r >= c
    gcol = gcol_ref[...]
    grow = grow_ref[...]

    for h in range(A_HEADS):
        lo = h * dh
        q = conv_silu(q_ref, qcar_ref, lo, lo)
        k = conv_silu(k_ref, kcar_ref, lo, A_WIDTH + lo) * (dh ** -0.5)
        v_b = v_ref[:, lo:lo + dh]
        q_b = q.astype(BF16)
        k_b = k.astype(BF16)

        i_col, b_col = gcol[:, h:h + 1], gcol[:, A_HEADS + h:A_HEADS + h + 1]
        i_row, b_row = grow[h:h + 1, :], grow[A_HEADS + h:A_HEADS + h + 1, :]
        b_last = b_row[:, L - 1:L]
        m_prev = m_ref[h]

        dmat = jnp.where(causal, b_col - b_row + i_row, NEG_INF)
        inter = b_col + m_prev
        m_t = jnp.maximum(inter, jnp.max(dmat, axis=-1, keepdims=True))
        s = lax.dot_general(q_b, k_b, (((1,), (1,)), ((), ())), preferred_element_type=F32)
        w_intra = jnp.exp(dmat - m_t) * s
        w_inter = jnp.exp(inter - m_t)
        qc = jnp.dot(q_b, c_ref[h].astype(BF16), preferred_element_type=F32)
        num = w_inter * qc + jnp.dot(w_intra.astype(BF16), v_b, preferred_element_type=F32)
        qn = jnp.sum(q * n_ref[h], axis=-1, keepdims=True)
        den = w_inter * qn + jnp.sum(w_intra, axis=-1, keepdims=True)
        hh = num / jnp.maximum(jnp.abs(den), jnp.exp(-m_t))

        g_col = b_last - b_col + i_col
        m_new = jnp.maximum(b_last + m_prev, jnp.max(g_col, axis=0, keepdims=True))
        decay = jnp.exp(b_last + m_prev - m_new)
        kw = jnp.exp(g_col - m_new) * k
        c_ref[h] = decay * c_ref[h] + jnp.dot(kw.T.astype(BF16), v_b, preferred_element_type=F32)
        n_ref[h] = decay * n_ref[h] + jnp.sum(kw, axis=0, keepdims=True)
        m_ref[h] = m_new

        y = _sigmoid(o_ref[:, lo:lo + dh].astype(F32)) * hh
        mu = jnp.mean(y, axis=-1, keepdims=True)
        yc = y - mu
        var = jnp.mean(yc * yc, axis=-1, keepdims=True)
        out_ref[:, lo:lo + dh] = (yc * lax.rsqrt(var + LN_EPS) * ng_ref[:, lo:lo + dh]).astype(out_ref.dtype)


def _mlstm(z_main, gcol, grow, conv_w, conv_b, norm_g, B, S):
    L = min(MLSTM_CHUNK, S)
    nc = S // L
    dh = A_HEAD_DIM
    H = A_HEADS
    W = A_WIDTH
    row = lambda b, c: b * nc + c
    full = lambda shape: pl.BlockSpec(shape, lambda b, c: (0, 0))
    return pl.pallas_call(
        _mlstm_kernel,
        grid=(B, nc),
        in_specs=[
            pl.BlockSpec((L, W), lambda b, c: (row(b, c), COL_QA // W)),
            pl.BlockSpec((L, W), lambda b, c: (row(b, c), COL_KA // W)),
            pl.BlockSpec((L, W), lambda b, c: (row(b, c), COL_VA // W)),
            pl.BlockSpec((L, W), lambda b, c: (row(b, c), COL_OA // W)),
            pl.BlockSpec((L, SUBLANES), lambda b, c: (row(b, c), 0)),
            pl.BlockSpec((SUBLANES, L), lambda b, c: (0, row(b, c))),
            full((CONV_WIDTH, 2 * W)), full((1, 2 * W)), full((1, W)),
        ],
        out_specs=pl.BlockSpec((L, W), lambda b, c: (row(b, c), 0)),
        out_shape=jax.ShapeDtypeStruct((B * S, W), BF16),
        scratch_shapes=[
            pltpu.VMEM((H, dh, dh), F32),
            pltpu.VMEM((H, 1, dh), F32),
            pltpu.VMEM((H, 1, 1), F32),
            pltpu.VMEM((SUBLANES, W), F32),
            pltpu.VMEM((SUBLANES, W), F32),
        ],
        compiler_params=_cparams(("parallel", "arbitrary")),
        name="mlstm",
    )(z_main, z_main, z_main, z_main, gcol, grow, conv_w, conv_b, norm_g)


def _lane_tiles(t):
    return [t[:, i * LANES:(i + 1) * LANES] for i in range(t.shape[1] // LANES)]


def _attn_kernel(slope_ref, q_ref, k_ref, kb_ref, v_ref, lq1_ref, lk1_ref, lq2_ref, lk2_ref, g_ref, o_ref,
                 s_ref, mx_ref, acc_ref, corr_ref, *, lambda_init):
    h = pl.program_id(1)
    qi = pl.program_id(2)
    tq = q_ref.shape[0]
    tk = tq
    slope = slope_ref[h]
    nt = (((1,), (1,)), ((), ()))

    lane = lax.broadcasted_iota(I32, (tq, B_V_DIM), 1)
    q = q_ref[...] * (B_HEAD_DIM ** -0.5)
    zero = jnp.zeros_like(q)
    bias_cols = jnp.where(lane < 2, 1.0, 0.0).astype(BF16)
    qa = (jnp.concatenate([jnp.where(lane < B_HEAD_DIM, q, zero), bias_cols], axis=1),
          jnp.concatenate([jnp.where(lane >= B_HEAD_DIM, q, zero), bias_cols], axis=1))

    def scores(j):
        start = pl.multiple_of(j * tk, tk)
        ka = jnp.concatenate([k_ref[pl.ds(start, tk), :], kb_ref[pl.ds(start, tk), :]], axis=1)
        return [lax.dot_general(qa[n], ka, nt, preferred_element_type=F32) for n in range(2)]

    def keep(j, n, t):
        t = t * LOG2_E
        s_ref[n, j] = t
        mx_ref[n] = functools.reduce(jnp.maximum, _lane_tiles(t), mx_ref[n])

    mx_ref[...] = jnp.full(mx_ref.shape, NEG_INF, F32)

    def pass_a(j):
        t = scores(j)
        for n in range(2):
            keep(j, n, t[n])

    def pass_a_pair(jj, carry):
        pass_a(2 * jj)
        pass_a(2 * jj + 1)
        return carry
    lax.fori_loop(0, qi // 2, pass_a_pair, 0)

    @pl.when(qi % 2 == 1)
    def _():
        pass_a(qi - 1)

    @pl.when(qi == 0)
    def _():
        r = lax.broadcasted_iota(I32, (tq, tk), 0)
        c = lax.broadcasted_iota(I32, (tq, tk), 1)
        ahead = jnp.where(c <= r, 0.0, (r - c).astype(F32) * (2.0 * slope))
        corr_ref[...] = jnp.where((c // ATTN_CHUNK) <= (r // ATTN_CHUNK), ahead, NEG_INF)

    rg = tq // ATTN_DIAG_GROUPS
    dstart = pl.multiple_of(qi * tk, tk)
    ka = jnp.concatenate([k_ref[pl.ds(dstart, tk), :], kb_ref[pl.ds(dstart, tk), :]], axis=1)
    for n in range(2):
        bands = []
        for g in range(ATTN_DIAG_GROUPS):
            kend = (g + 1) * rg
            band = lax.dot_general(qa[n][g * rg:kend], ka[:kend], nt, preferred_element_type=F32)
            if kend < tk:
                band = jnp.concatenate([band, jnp.full((rg, tk - kend), NEG_INF, F32)], axis=1)
            bands.append(band)
        keep(qi, n, jnp.concatenate(bands, axis=0) + corr_ref[...])

    for n in range(2):
        m = jnp.max(mx_ref[n], axis=-1, keepdims=True)
        mx_ref[n] = jnp.broadcast_to(m, (tq, LANES))
    acc_ref[...] = jnp.zeros_like(acc_ref)

    def weigh(n, rows, s_tile, v_tile):
        mb = mx_ref[n, rows, :]
        p = jnp.concatenate([jnp.exp2(t - mb) for t in _lane_tiles(s_tile)], axis=1).astype(BF16)
        v_ones = jnp.concatenate([v_tile, jnp.ones((v_tile.shape[0], LANES), BF16)], axis=1)
        acc_ref[n, rows, :] += jnp.dot(p, v_ones, preferred_element_type=F32)

    def pass_b(j, carry):
        start = pl.multiple_of(j * tk, tk)
        vj = v_ref[pl.ds(start, tk), :]
        for n in range(2):
            weigh(n, slice(None), s_ref[n, j], vj)
        return carry
    lax.fori_loop(0, qi, pass_b, 0)

    for n in range(2):
        for g in range(ATTN_DIAG_GROUPS):
            kend = (g + 1) * rg
            weigh(n, slice(g * rg, kend), s_ref[n, qi, g * rg:kend, :kend], v_ref[pl.ds(dstart, kend), :])

    lam = (jnp.exp(jnp.sum(lq1_ref[...] * lk1_ref[...], axis=-1, keepdims=True))
           - jnp.exp(jnp.sum(lq2_ref[...] * lk2_ref[...], axis=-1, keepdims=True)) + lambda_init)
    dv = B_V_DIM
    a0, a1 = acc_ref[0], acc_ref[1]
    o = a0[:, :dv] / a0[:, dv:] - lam * (a1[:, :dv] / a1[:, dv:])
    ms = jnp.mean(o * o, axis=-1, keepdims=True)
    o_ref[...] = (o * lax.rsqrt(ms + LN_EPS) * g_ref[...] * (1.0 - lambda_init)).astype(o_ref.dtype)


def _diff_attention(z_main, lq1, lk1, lq2, lk2, norm_g, B, S, lambda_init):
    tq = min(ATTN_TQ, S)
    nq = S // tq
    H = B_HEADS
    dv = B_V_DIM
    slopes = 2.0 ** (-8.0 * jnp.arange(1, H + 1, dtype=F32) / H)
    assert S <= 256 * 256
    pos = lax.broadcasted_iota(I32, (H, S, dv), 1)
    col = lax.broadcasted_iota(I32, (H, S, dv), 2)
    within = pos % 256
    kbias = jnp.where(col == 0, within, jnp.where(col == 1, pos - within, 0)).astype(F32)
    kbias = (kbias * slopes[:, None, None]).astype(BF16)
    small = pl.BlockSpec((1, B_HEAD_DIM), lambda b, h, i: (0, 0))
    return pl.pallas_call(
        functools.partial(_attn_kernel, lambda_init=lambda_init),
        grid=(B, H, nq),
        in_specs=[
            pl.BlockSpec(memory_space=pltpu.SMEM),
            pl.BlockSpec((tq, dv), lambda b, h, i: (b * nq + i, COL_QB // dv + h)),
            pl.BlockSpec((S, dv), lambda b, h, i: (b, COL_KB // dv + h)),
            pl.BlockSpec((None, S, dv), lambda b, h, i: (h, 0, 0)),
            pl.BlockSpec((S, dv), lambda b, h, i: (b, COL_VB // dv + h)),
            small, small, small, small,
            pl.BlockSpec((1, dv), lambda b, h, i: (0, 0)),
        ],
        out_specs=pl.BlockSpec((tq, dv), lambda b, h, i: (b * nq + i, h)),
        out_shape=jax.ShapeDtypeStruct((B * S, B_WIDTH), BF16),
        scratch_shapes=[
            pltpu.VMEM((2, nq, tq, tq), F32),
            pltpu.VMEM((2, tq, LANES), F32),
            pltpu.VMEM((2, tq, dv + LANES), F32),
            pltpu.VMEM((tq, tq), F32),
        ],
        compiler_params=_cparams(("parallel", "parallel", "arbitrary")),
        name="diff_attention",
    )(slopes, z_main, z_main, kbias, z_main, lq1, lk1, lq2, lk2, norm_g)


def _mix_kernel(ha_ref, ob_ref, ga_ref, gb_ref, x_ref, wa_ref, wb_ref, wo_ref, g1_ref, b1_ref, wr_ref, br_ref,
                x1_ref, x1p_ref, ri_ref, rw_ref, cnt_ref, carry_ref):
    i = pl.program_id(0)
    tm = x_ref.shape[0]

    @pl.when(i == 0)
    def _():
        carry_ref[...] = jnp.zeros_like(carry_ref)

    ya = jnp.dot(ha_ref[...], wa_ref[...], preferred_element_type=F32)
    yb = jnp.dot(ob_ref[...], wb_ref[...], preferred_element_type=F32)
    merged = _sigmoid(ga_ref[...].astype(F32)) * ya + _sigmoid(gb_ref[...].astype(F32)) * yb
    mix = jnp.dot(merged.astype(BF16), wo_ref[...], preferred_element_type=F32)
    pre = DEEPNORM_ALPHA * x_ref[...] + mix
    mu = jnp.mean(pre, axis=-1, keepdims=True)
    pc = pre - mu
    var = jnp.mean(pc * pc, axis=-1, keepdims=True)
    x1 = pc * lax.rsqrt(var + LN_EPS) * g1_ref[...] + b1_ref[...]
    x1_ref[...] = x1
    _rows_to_tiles(x1p_ref, _pack_halves(x1))

    logits = _dot_3pass(x1, wr_ref[...]) + br_ref[...]
    lt = logits.T
    row8 = lax.broadcasted_iota(I32, (SUBLANES, tm), 0).astype(F32)
    gl = jnp.where(row8 < N_GROUPS, lt[0:SUBLANES, :], NEG_INF)
    gmax = jnp.max(gl, axis=0, keepdims=True)
    gsel = jnp.min(jnp.where(gl == gmax, row8, float(SUBLANES)), axis=0, keepdims=True)
    gprob = 1.0 / jnp.sum(jnp.exp(gl - gmax), axis=0, keepdims=True)
    ing = lt[SUBLANES:2 * SUBLANES, :]
    for g in range(1, N_GROUPS):
        ing = jnp.where(gsel == g, lt[(g + 1) * SUBLANES:(g + 2) * SUBLANES, :], ing)
    v0 = jnp.max(ing, axis=0, keepdims=True)
    i0 = jnp.min(jnp.where(ing == v0, row8, float(SUBLANES)), axis=0, keepdims=True)
    ing2 = jnp.where(row8 == i0, -jnp.inf, ing)
    v1 = jnp.max(ing2, axis=0, keepdims=True)
    i1 = jnp.min(jnp.where(ing2 == v1, row8, float(SUBLANES)), axis=0, keepdims=True)
    ex = jnp.exp(v1 - v0)
    inv = 1.0 / (1.0 + ex)
    w0 = gprob * inv
    w1 = gprob * (ex * inv)
    e0 = (gsel * EXPERTS_PER_GROUP + i0).astype(I32)
    e1 = (gsel * EXPERTS_PER_GROUP + i1).astype(I32)

    rowe = lax.broadcasted_iota(I32, (N_EXPERTS, tm), 0)
    is0 = rowe == e0
    is1 = rowe == e1
    oh = jnp.where(is0, 1.0, 0.0) + jnp.where(is1, 1.0, 0.0)
    tr = lax.broadcasted_iota(I32, (tm, tm), 0)
    tc = lax.broadcasted_iota(I32, (tm, tm), 1)
    su = jnp.where(tr < tc, 1.0, 0.0).astype(BF16)
    tot = jnp.dot(oh.astype(BF16), su, preferred_element_type=F32) + carry_ref[...]
    rank0 = jnp.sum(jnp.where(is0, tot, 0.0), axis=0, keepdims=True)
    rank1 = jnp.sum(jnp.where(is1, tot, 0.0), axis=0, keepdims=True)
    carry = carry_ref[...] + jnp.sum(oh, axis=1, keepdims=True)
    carry_ref[...] = carry
    cnt_ref[...] = jnp.broadcast_to(carry, cnt_ref.shape).astype(I32)

    zi = jnp.zeros((1, tm), I32)
    ri_ref[...] = jnp.concatenate([e0, e1, rank0.astype(I32), rank1.astype(I32), zi, zi, zi, zi], axis=0)
    zf = jnp.zeros((1, tm), F32)
    rw_ref[...] = jnp.concatenate([w0, w1, zf, zf, zf, zf, zf, zf], axis=0)


def _mix(h_a, o_b, z_main, x2d, w_a, w_b, w_out, ln_g, ln_b, w_r, b_r):
    T, D = x2d.shape
    tm = min(MIX_TM, T)
    const = lambda shape: pl.BlockSpec(shape, lambda i: (0, 0), pipeline_mode=pl.Buffered(1))
    return pl.pallas_call(
        _mix_kernel,
        grid=(T // tm,),
        in_specs=[
            pl.BlockSpec((tm, A_WIDTH), lambda i: (i, 0)),
            pl.BlockSpec((tm, B_WIDTH), lambda i: (i, 0)),
            pl.BlockSpec((tm, D), lambda i: (i, COL_GA // D)),
            pl.BlockSpec((tm, D), lambda i: (i, COL_GB // D)),
            pl.BlockSpec((tm, D), lambda i: (i, 0)),
            const((A_WIDTH, D)), const((B_WIDTH, D)), const((D, D)),
            const((1, D)), const((1, D)), const((D, 2 * LANES)), const((1, LANES)),
        ],
        out_specs=[
            pl.BlockSpec((tm, D), lambda i: (i, 0)),
            pl.BlockSpec((tm * SUBLANES, LANES), lambda i: (i, 0)),
            pl.BlockSpec((SUBLANES, tm), lambda i: (0, i)),
            pl.BlockSpec((SUBLANES, tm), lambda i: (0, i)),
            pl.BlockSpec((N_EXPERTS, LANES), lambda i: (0, 0)),
        ],
        out_shape=[
            jax.ShapeDtypeStruct((T, D), F32),
            jax.ShapeDtypeStruct((T * SUBLANES, LANES), U32),
            jax.ShapeDtypeStruct((SUBLANES, T), I32),
            jax.ShapeDtypeStruct((SUBLANES, T), F32),
            jax.ShapeDtypeStruct((N_EXPERTS, LANES), I32),
        ],
        scratch_shapes=[pltpu.VMEM((N_EXPERTS, 1), F32)],
        compiler_params=_cparams(("arbitrary",)),
        name="mix_ln1_router",
    )(h_a, o_b, z_main, z_main, x2d, w_a, w_b, w_out, ln_g, ln_b, w_r, b_r)


def _row_copy(src_ref, src_row, dst_ref, dst_row, sem):
    return pltpu.make_async_copy(src_ref.at[pl.ds(src_row * SUBLANES, SUBLANES)],
                                 dst_ref.at[pl.ds(dst_row * SUBLANES, SUBLANES)], sem)


def _rows_wait(src_ref, dst_ref, rows, sem):
    n = rows * SUBLANES
    pltpu.make_async_copy(src_ref.at[pl.ds(0, n)], dst_ref.at[pl.ds(0, n)], sem).wait()


DISPATCH_SLOTS = 3


def _dispatch_kernel(cnt_ref, ri_ref, x1_ref, xpad_ref, pstart_ref, bexp_ref, nused_ref,
                     pst_ref, zero_ref, xbuf_ref, sem, lsem, zsem, *, tb, n_blocks, tm):
    i = pl.program_id(0)
    last = pl.num_programs(0) - 1

    @pl.when(i == 0)
    def _():
        def fill(b, carry):
            bexp_ref[b] = 0
            return carry
        lax.fori_loop(0, n_blocks, fill, 0)

        def per_expert(e, blk):
            pst_ref[e] = blk * tb
            pstart_ref[e] = blk * tb
            nb = (cnt_ref[e] + (tb - 1)) // tb

            def mark(b, carry):
                bexp_ref[blk + b] = e
                return carry
            lax.fori_loop(0, nb, mark, 0)
            return blk + nb
        nused = lax.fori_loop(0, N_EXPERTS, per_expert, 0)
        nused_ref[0] = nused
        pst_ref[N_EXPERTS] = nused

    rows = tm * SUBLANES
    slot = i % DISPATCH_SLOTS

    def tile_load(tile, s):
        src = x1_ref.at[pl.ds(pl.multiple_of(tile * rows, rows), rows)]
        return pltpu.make_async_copy(src, xbuf_ref.at[s], lsem.at[s])

    def wait_rows(s):
        for k in range(TOP_K):
            _rows_wait(xbuf_ref.at[s], xpad_ref, tm, sem.at[s])

    @pl.when(i == 0)
    def _():
        tile_load(0, 0).start()

        @pl.when(last >= 1)
        def _():
            tile_load(1, 1).start()

    tile_load(i, slot).wait()
    for t in range(tm):
        for k in range(TOP_K):
            pos = pst_ref[ri_ref[k, t]] + ri_ref[TOP_K + k, t]
            _row_copy(xbuf_ref.at[slot], t, xpad_ref, pos, sem.at[slot]).start(priority=k)

    prev = (i + DISPATCH_SLOTS - 1) % DISPATCH_SLOTS

    @pl.when(i >= 1)
    def _():
        wait_rows(prev)

    @pl.when(i + 2 <= last)
    def _():
        tile_load(i + 2, prev).start()

    @pl.when(i == last)
    def _():
        wait_rows(slot)
        zero_ref[...] = jnp.zeros_like(zero_ref)

        def per_expert(e, carry):
            cnt = cnt_ref[e]
            first = pst_ref[e] + cnt
            npad = ((cnt + (tb - 1)) // tb) * tb - cnt

            def start(p, c2):
                _row_copy(zero_ref, 0, xpad_ref, first + p, zsem).start()
                return c2
            lax.fori_loop(0, npad, start, 0)

            def wait(p, c2):
                _row_copy(zero_ref, 0, xpad_ref, 0, zsem).wait()
                return c2
            lax.fori_loop(0, npad, wait, 0)
            return carry
        lax.fori_loop(0, N_EXPERTS, per_expert, 0)

        def block_copy(b):
            n = tb * SUBLANES
            return pltpu.make_async_copy(zero_ref, xpad_ref.at[pl.ds(pl.multiple_of(b * n, n), n)], zsem)

        def start_block(b, carry):
            block_copy(b).start()
            return carry
        lax.fori_loop(pst_ref[N_EXPERTS], n_blocks, start_block, 0)

        def wait_block(b, carry):
            block_copy(b).wait()
            return carry
        lax.fori_loop(pst_ref[N_EXPERTS], n_blocks, wait_block, 0)


def _dispatch(counts, route_i, x1, tb, n_blocks):
    T = x1.shape[0] // SUBLANES
    tm = min(DISPATCH_TM, T)
    smem_full = pl.BlockSpec(memory_space=pltpu.SMEM)
    return pl.pallas_call(
        functools.partial(_dispatch_kernel, tb=tb, n_blocks=n_blocks, tm=tm),
        grid=(T // tm,),
        in_specs=[
            smem_full,
            pl.BlockSpec((SUBLANES, tm), lambda i: (0, i), memory_space=pltpu.SMEM),
            pl.BlockSpec(memory_space=pl.ANY),
        ],
        out_specs=[
            pl.BlockSpec(memory_space=pl.ANY),
            smem_full, smem_full, smem_full,
        ],
        out_shape=[
            jax.ShapeDtypeStruct((n_blocks * tb * SUBLANES, LANES), x1.dtype),
            jax.ShapeDtypeStruct((N_EXPERTS,), I32),
            jax.ShapeDtypeStruct((n_blocks,), I32),
            jax.ShapeDtypeStruct((1,), I32),
        ],
        scratch_shapes=[
            pltpu.SMEM((N_EXPERTS + 1,), I32),
            pltpu.VMEM((tb * SUBLANES, LANES), x1.dtype),
            pltpu.VMEM((DISPATCH_SLOTS, tm * SUBLANES, LANES), x1.dtype),
            pltpu.SemaphoreType.DMA((DISPATCH_SLOTS,)),
            pltpu.SemaphoreType.DMA((DISPATCH_SLOTS,)),
            pltpu.SemaphoreType.DMA(()),
        ],
        compiler_params=_cparams(("arbitrary",)),
        name="moe_dispatch",
    )(counts, route_i, x1)


def _expert_kernel(bexp_ref, nused_ref, x_ref, wg_hbm, wu_hbm, wd_hbm, y_ref,
                   wgf_ref, wuf_ref, wdf_ref, wgb_ref, wub_ref, wdb_ref, sem, run_ref):
    b = pl.program_id(0)
    nused = nused_ref[0]
    used = b < nused
    e = bexp_ref[b]
    new_expert = jnp.logical_or(b == 0, e != bexp_ref[jnp.maximum(b - 1, 0)])
    streams = ((wg_hbm, wgf_ref), (wu_hbm, wuf_ref), (wd_hbm, wdf_ref))

    def weight_copies(expert, slot):
        return [pltpu.make_async_copy(w_hbm.at[expert], wf_ref.at[slot], sem.at[slot, j])
                for j, (w_hbm, wf_ref) in enumerate(streams)]

    @pl.when(jnp.logical_and(used, b == 0))
    def _():
        run_ref[0] = 0
        for cp in weight_copies(e, 0):
            cp.start()

    @pl.when(jnp.logical_and(used, new_expert))
    def _():
        slot = run_ref[0] % 2
        nxt = lax.while_loop(lambda j: jnp.logical_and(j < nused, bexp_ref[jnp.minimum(j, nused - 1)] == e),
                             lambda j: j + 1, b + 1)

        @pl.when(nxt < nused)
        def _():
            for cp in weight_copies(bexp_ref[jnp.minimum(nxt, nused - 1)], 1 - slot):
                cp.start()

        for cp in weight_copies(e, slot):
            cp.wait()
        wgb_ref[...] = wgf_ref[slot].astype(BF16)
        wub_ref[...] = wuf_ref[slot].astype(BF16)
        wdb_ref[...] = wdf_ref[slot].astype(BF16)
        run_ref[0] = run_ref[0] + 1

    @pl.when(used)
    def _():
        half = SUBLANES * LANES
        x_lo, x_hi = (t.astype(BF16) for t in _unpack_halves(_tiles_to_rows(x_ref)))
        dot = functools.partial(jnp.dot, preferred_element_type=F32)
        gate = dot(x_lo, wgb_ref[:half, :]) + dot(x_hi, wgb_ref[half:, :])
        up = dot(x_lo, wub_ref[:half, :]) + dot(x_hi, wub_ref[half:, :])
        hid = (gate * _sigmoid(gate) * up).astype(BF16)
        _rows_to_tiles(y_ref, _pack_halves(dot(hid, wdb_ref[...])))

    @pl.when(jnp.logical_not(used))
    def _():
        y_ref[...] = jnp.zeros_like(y_ref)


def _experts(bexp, nused, x_pad, w_gate, w_up, w_down, tb):
    P = x_pad.shape[0] // SUBLANES
    D = 2 * TILE_WORDS
    n_blocks = P // tb
    rows = (tb * SUBLANES, LANES)
    F = w_gate.shape[-1]
    blk = lambda b, be, nu: jnp.maximum(jnp.minimum(b, nu[0] - 1), 0)
    hbm = pl.BlockSpec(memory_space=pl.ANY)
    return pl.pallas_call(
        _expert_kernel,
        grid_spec=pltpu.PrefetchScalarGridSpec(
            num_scalar_prefetch=2,
            grid=(n_blocks,),
            in_specs=[pl.BlockSpec(rows, lambda b, be, nu: (blk(b, be, nu), 0)), hbm, hbm, hbm],
            out_specs=pl.BlockSpec(rows, lambda b, be, nu: (b, 0)),
            scratch_shapes=[
                pltpu.VMEM((2, D, F), F32), pltpu.VMEM((2, D, F), F32), pltpu.VMEM((2, F, D), F32),
                pltpu.VMEM((D, F), BF16), pltpu.VMEM((D, F), BF16), pltpu.VMEM((F, D), BF16),
                pltpu.SemaphoreType.DMA((2, 3)),
                pltpu.SMEM((1,), I32),
            ],
        ),
        out_shape=jax.ShapeDtypeStruct(x_pad.shape, U32),
        compiler_params=_cparams(("arbitrary",)),
        name="moe_experts",
    )(bexp, nused, x_pad, w_gate, w_up, w_down)


COMBINE_SLOTS = 3


def _combine_kernel(pstart_ref, ri_ref, r1_ref, r2_ref, x1_ref, rw_ref, g_ref, b_ref, ypad_ref, o_ref, ybuf_ref, sem):
    i = pl.program_id(0)
    n = pl.num_programs(0)
    tm = x1_ref.shape[0]
    slot = i % COMBINE_SLOTS

    def gather_loop(route_ref, s):
        def issue(t, carry):
            for k in range(TOP_K):
                pos = pstart_ref[route_ref[k, t]] + route_ref[TOP_K + k, t]
                _row_copy(ypad_ref, pos, ybuf_ref.at[s, k], t, sem.at[s]).start(priority=k)
            return carry
        lax.fori_loop(0, tm, issue, 0, unroll=8)

    def wait_slot(s):
        for k in range(TOP_K):
            _rows_wait(ypad_ref, ybuf_ref.at[s, k], tm, sem.at[s])

    @pl.when(i == 0)
    def _():
        gather_loop(ri_ref, 0)
        gather_loop(r1_ref, 1)

    wait_slot(slot)

    rw = rw_ref[...]
    y0 = _unpack_halves(_tiles_to_rows(ybuf_ref.at[slot, 0]))
    y1 = _unpack_halves(_tiles_to_rows(ybuf_ref.at[slot, 1]))
    ffn = jnp.concatenate([y0[h] * rw[:, 0:1] + y1[h] * rw[:, 1:2] for h in range(2)], axis=1)
    pre = DEEPNORM_ALPHA * x1_ref[...] + ffn
    mu = jnp.mean(pre, axis=-1, keepdims=True)
    pc = pre - mu
    var = jnp.mean(pc * pc, axis=-1, keepdims=True)
    o_ref[...] = pc * lax.rsqrt(var + LN_EPS) * g_ref[...] + b_ref[...]

    nxt = (i + 2) % COMBINE_SLOTS
    for t in range(tm):
        for k in range(TOP_K):
            pos = pstart_ref[r2_ref[k, t]] + r2_ref[TOP_K + k, t]
            _row_copy(ypad_ref, pos, ybuf_ref.at[nxt, k], t, sem.at[nxt]).start(priority=k)

    @pl.when(i == n - 1)
    def _():
        wait_slot((i + 1) % COMBINE_SLOTS)
        wait_slot(nxt)


def _combine(pstart, route_i, x1, rw_col, ln_g, ln_b, y_pad):
    T, D = x1.shape
    tm = min(COMBINE_TM, T)
    n = T // tm
    return pl.pallas_call(
        _combine_kernel,
        grid=(n,),
        in_specs=[
            pl.BlockSpec(memory_space=pltpu.SMEM),
            pl.BlockSpec((SUBLANES, tm), lambda i: (0, i), memory_space=pltpu.SMEM),
            pl.BlockSpec((SUBLANES, tm), lambda i: (0, jnp.minimum(i + 1, n - 1)), memory_space=pltpu.SMEM),
            pl.BlockSpec((SUBLANES, tm), lambda i: (0, jnp.minimum(i + 2, n - 1)), memory_space=pltpu.SMEM),
            pl.BlockSpec((tm, D), lambda i: (i, 0)),
            pl.BlockSpec((tm, SUBLANES), lambda i: (i, 0)),
            pl.BlockSpec((1, D), lambda i: (0, 0)),
            pl.BlockSpec((1, D), lambda i: (0, 0)),
            pl.BlockSpec(memory_space=pl.ANY),
        ],
        out_specs=pl.BlockSpec((tm, D), lambda i: (i, 0)),
        out_shape=jax.ShapeDtypeStruct((T, D), F32),
        scratch_shapes=[
            pltpu.VMEM((COMBINE_SLOTS, TOP_K, tm * SUBLANES, LANES), U32),
            pltpu.SemaphoreType.DMA((COMBINE_SLOTS,)),
        ],
        compiler_params=_cparams(("arbitrary",)),
        name="moe_combine_ln2",
    )(pstart, route_i, route_i, route_i, x1, rw_col, ln_g, ln_b, y_pad)


N_GATE_COLS = 2 * A_HEADS
SRC_MLSTM = 0
SRC_GATES = 4 * A_WIDTH
SRC_DIFF = SRC_GATES + N_GATE_COLS
SRC_MERGE = SRC_DIFF + 3 * B_WIDTH
PREP_TN = 512


def _wprep_kernel(a_ref, o_ref):
    o_ref[...] = a_ref[...].T.astype(BF16)


def _wgate_kernel(a_ref, o_ref):
    a = a_ref[...]
    rows = jnp.concatenate([a, jnp.zeros((LANES - a.shape[0], a.shape[1]), F32)], axis=0)
    o_ref[...] = rows.T


def _rearrange_in_proj(w_in, b_in):
    K, n_in = w_in.shape
    tn = PREP_TN
    merge_blocks = (2 * D_MODEL) // tn
    mlstm_blocks = (4 * A_WIDTH) // tn
    assert SRC_MERGE % SUBLANES == 0 and SRC_DIFF % SUBLANES == 0

    def src_row(jb):
        s = SUBLANES
        merge = SRC_MERGE // s + (tn // s) * jb
        mlstm = SRC_MLSTM // s + (tn // s) * (jb - merge_blocks)
        diff = SRC_DIFF // s + (tn // s) * (jb - merge_blocks - mlstm_blocks)
        return s * jnp.where(jb < merge_blocks, merge, jnp.where(jb < merge_blocks + mlstm_blocks, mlstm, diff))

    w_t = jnp.swapaxes(w_in, 0, 1)
    w_main = pl.pallas_call(
        _wprep_kernel,
        grid=(N_MAIN // tn,),
        in_specs=[pl.BlockSpec((pl.Element(tn), pl.Element(K)), lambda jb: (src_row(jb), 0))],
        out_specs=pl.BlockSpec((K, tn), lambda jb: (0, jb)),
        out_shape=jax.ShapeDtypeStruct((K, N_MAIN), BF16),
        compiler_params=_cparams(("parallel",)),
        name="w_in_prep",
    )(w_t)
    b_main = jnp.concatenate([b_in[SRC_MERGE:], b_in[SRC_MLSTM:SRC_GATES], b_in[SRC_DIFF:SRC_MERGE]])[None, :]
    w_gate = pl.pallas_call(
        _wgate_kernel,
        grid=(1,),
        in_specs=[pl.BlockSpec((pl.Element(N_GATE_COLS), pl.Element(K)), lambda i: (SRC_GATES, 0))],
        out_specs=pl.BlockSpec((K, LANES), lambda i: (0, 0)),
        out_shape=jax.ShapeDtypeStruct((K, LANES), F32),
        name="w_gate_prep",
    )(w_t)
    b_gate = jnp.pad(b_in[SRC_GATES:SRC_DIFF], (0, LANES - N_GATE_COLS))[None, :]
    return w_main, b_main, _hi_lo_columns(w_gate), b_gate


def _layer(x, w_in, b_in, conv_w, conv_b, norm_a_g, lq1, lk1, lq2, lk2, norm_b_g, w_a, w_b, w_out,
           ln1_g, ln1_b, w_grp, b_grp, w_exp, b_exp, w_gate, w_up, w_down, ln2_g, ln2_b, lambda_init):
    B, S, D = x.shape
    T = B * S
    x2d = x.reshape(T, D)

    w_main, b_main, w_g, b_g = _rearrange_in_proj(w_in, b_in)
    z_main, z_gate = _in_projection(x2d, w_main, b_main, w_g, b_g)

    L = min(MLSTM_CHUNK, S)
    gp = _gate_prep(z_gate[:, :2 * A_HEADS].T, L)
    h_a = _mlstm(z_main, gp.T, gp, conv_w, conv_b[None, :], norm_a_g[None, :], B, S)
    o_b = _diff_attention(z_main, lq1[None, :], lk1[None, :], lq2[None, :], lk2[None, :],
                          norm_b_g[None, :], B, S, lambda_init)

    w_r = jnp.zeros((D, LANES), F32).at[:, :N_GROUPS].set(w_grp).at[:, SUBLANES:SUBLANES + N_EXPERTS].set(w_exp)
    b_r = jnp.zeros((LANES,), F32).at[:N_GROUPS].set(b_grp).at[SUBLANES:SUBLANES + N_EXPERTS].set(b_exp)[None, :]
    x1, x1_packed, route_i, route_w, counts = _mix(h_a, o_b, z_main, x2d, w_a.astype(BF16), w_b.astype(BF16),
                                                   w_out.astype(BF16), ln1_g[None, :], ln1_b[None, :],
                                                   _hi_lo_columns(w_r), b_r)

    tb = MOE_TB
    n_blocks = (T * TOP_K) // tb + N_EXPERTS
    x_pad, pstart, bexp, nused = _dispatch(counts[:, 0], route_i, x1_packed, tb, n_blocks)
    y_pad = _experts(bexp, nused, x_pad, w_gate, w_up, w_down, tb)
    out = _combine(pstart, route_i, x1, route_w.T, ln2_g[None, :], ln2_b[None, :], y_pad)
    return out.reshape(B, S, D)


def kernel(x, w_in, b_in, conv_w, conv_b, mlstm_norm_g, lambda_q1, lambda_k1, lambda_q2, lambda_k2,
           diff_norm_g, w_a, w_b, w_out, ln1_g, ln1_b, w_grp, b_grp, w_exp, b_exp,
           w_gate, w_up, w_down, ln2_g, ln2_b):
    for l in range(DEPTH):
        lambda_init = 0.8 - 0.6 * math.exp(-0.3 * l)
        x = _layer(x, w_in[l], b_in[l], conv_w[l], conv_b[l], mlstm_norm_g[l], lambda_q1[l], lambda_k1[l],
                   lambda_q2[l], lambda_k2[l], diff_norm_g[l], w_a[l], w_b[l], w_out[l], ln1_g[l], ln1_b[l],
                   w_grp[l], b_grp[l], w_exp[l], b_exp[l], w_gate[l], w_up[l], w_down[l], ln2_g[l], ln2_b[l],
                   lambda_init)
    return x
```

```python
import functools
import math

import jax
import jax.numpy as jnp
from jax import lax
from jax.experimental import pallas as pl
from jax.experimental.pallas import tpu as pltpu

F32 = jnp.float32
BF16 = jnp.bfloat16
I32 = jnp.int32
U32 = jnp.uint32

D_MODEL = 2048
A_HEADS = 4
A_HEAD_DIM = 256
A_WIDTH = A_HEADS * A_HEAD_DIM
CONV_WIDTH = 4
B_HEADS = 8
B_HEAD_DIM = 64
B_V_DIM = 2 * B_HEAD_DIM
B_WIDTH = B_HEADS * B_V_DIM
ATTN_CHUNK = 64
N_GROUPS = 4
EXPERTS_PER_GROUP = 8
N_EXPERTS = N_GROUPS * EXPERTS_PER_GROUP
TOP_K = 2
D_EXPERT = 512
DEPTH = 1
DEEPNORM_ALPHA = (2 * DEPTH) ** 0.25
LN_EPS = 1e-5
NEG_INF = float("-inf")
LOG2_E = 1.4426950408889634

LANES = 128
SUBLANES = 8
VMEM_LIMIT_BYTES = 56 * 1024 * 1024

COL_GA = 0
COL_GB = COL_GA + D_MODEL
COL_QA = COL_GB + D_MODEL
COL_KA = COL_QA + A_WIDTH
COL_VA = COL_KA + A_WIDTH
COL_OA = COL_VA + A_WIDTH
COL_QB = COL_OA + A_WIDTH
COL_KB = COL_QB + B_WIDTH
COL_VB = COL_KB + B_WIDTH
N_MAIN = COL_VB + B_WIDTH

PROJ_TM, PROJ_TN = 1024, 1024
MLSTM_CHUNK = 512
ATTN_TQ = 1024
ATTN_DIAG_GROUPS = 4
MIX_TM = 256
MOE_TB = 256
DISPATCH_TM = 256
COMBINE_TM = 256


def _cparams(sem, vmem=VMEM_LIMIT_BYTES):
    return pltpu.CompilerParams(dimension_semantics=sem, vmem_limit_bytes=vmem)


def _sigmoid(x):
    return 1.0 / (1.0 + jnp.exp(-x))


def _pack_halves(x):
    n = x.shape[1] // 2
    lo = lax.bitcast_convert_type(x[:, :n].astype(BF16).astype(F32), U32)
    hi = lax.bitcast_convert_type(x[:, n:].astype(BF16).astype(F32), U32)
    return (lo >> 16) | hi


def _unpack_halves(u):
    lo = lax.bitcast_convert_type(u << 16, F32)
    hi = lax.bitcast_convert_type(u & jnp.uint32(0xFFFF0000), F32)
    return lo, hi


TILE_WORDS = SUBLANES * LANES


def _rows_to_tiles(ref, v):
    rows = v.shape[0]
    for j in range(SUBLANES):
        ref[pl.ds(j, rows, stride=SUBLANES), :] = v[:, j * LANES:(j + 1) * LANES]


def _tiles_to_rows(ref):
    rows = ref.shape[0] // SUBLANES
    return jnp.concatenate([ref[pl.ds(j, rows, stride=SUBLANES), :] for j in range(SUBLANES)], axis=1)


def _split_bf16(a):
    hi = a.astype(BF16)
    return hi, (a - hi.astype(F32)).astype(BF16)


def _hi_lo_columns(w):
    return jnp.concatenate(_split_bf16(w), axis=1)


def _dot_3pass(a, w_hl):
    a_hi, a_lo = _split_bf16(a)
    n = w_hl.shape[1] // 2
    dot = functools.partial(jnp.dot, preferred_element_type=F32)
    r = dot(a_hi, w_hl)
    return r[:, :n] + (r[:, n:] + dot(a_lo, w_hl[:, :n]))


def _proj_kernel(x_ref, w_ref, b_ref, wg_ref, bg_ref, z_ref, zg_ref, xb_ref):
    @pl.when(pl.program_id(1) == 0)
    def _():
        x = x_ref[...]
        xb_ref[...] = x.astype(BF16)
        zg_ref[...] = _dot_3pass(x, wg_ref[...]) + bg_ref[...]

    acc = jnp.dot(xb_ref[...], w_ref[...], preferred_element_type=F32)
    z_ref[...] = (acc + b_ref[...]).astype(z_ref.dtype)


def _in_projection(x2d, w_main, b_main, w_gate, b_gate):
    T, K = x2d.shape
    N = w_main.shape[1]
    tm, tn = min(PROJ_TM, T), PROJ_TN
    return pl.pallas_call(
        _proj_kernel,
        grid=(T // tm, N // tn),
        in_specs=[
            pl.BlockSpec((tm, K), lambda i, j: (i, 0)),
            pl.BlockSpec((K, tn), lambda i, j: (0, j)),
            pl.BlockSpec((1, tn), lambda i, j: (0, j)),
            pl.BlockSpec((K, 2 * LANES), lambda i, j: (0, 0)),
            pl.BlockSpec((1, LANES), lambda i, j: (0, 0)),
        ],
        out_specs=[
            pl.BlockSpec((tm, tn), lambda i, j: (i, j)),
            pl.BlockSpec((tm, LANES), lambda i, j: (i, 0)),
        ],
        out_shape=[
            jax.ShapeDtypeStruct((T, N), BF16),
            jax.ShapeDtypeStruct((T, LANES), F32),
        ],
        scratch_shapes=[pltpu.VMEM((tm, K), BF16)],
        compiler_params=_cparams(("parallel", "arbitrary")),
        name="in_projection",
    )(x2d, w_main, b_main, w_gate, b_gate)


def _gate_prep_kernel(g_ref, o_ref, *, L):
    r = lax.broadcasted_iota(I32, (L, L), 0)
    c = lax.broadcasted_iota(I32, (L, L), 1)
    tri = (r <= c).astype(F32)
    row = lax.broadcasted_iota(I32, (g_ref.shape[0], L), 0)
    for ch in range(g_ref.shape[1] // L):
        g = g_ref[:, ch * L:(ch + 1) * L]
        lf = jnp.minimum(g, 0.0) - jnp.log(1.0 + jnp.exp(-jnp.abs(g)))
        b = jnp.dot(lf, tri, preferred_element_type=F32, precision=lax.Precision.HIGHEST)
        o_ref[:, ch * L:(ch + 1) * L] = jnp.where(row < A_HEADS, g, b)


def _gate_prep(g_rows, L):
    R, T = g_rows.shape
    per_step = math.gcd(T // L, 8)
    W = L * per_step
    return pl.pallas_call(
        functools.partial(_gate_prep_kernel, L=L),
        grid=(T // W,),
        in_specs=[pl.BlockSpec((R, W), lambda i: (0, i))],
        out_specs=pl.BlockSpec((R, W), lambda i: (0, i)),
        out_shape=jax.ShapeDtypeStruct((R, T), F32),
        compiler_params=_cparams(("parallel",)),
        name="gate_prep",
    )(g_rows)


def _mlstm_kernel(q_ref, k_ref, v_ref, o_ref, gcol_ref, grow_ref, cw_ref, cb_ref, ng_ref, out_ref,
                  c_ref, n_ref, m_ref, qcar_ref, kcar_ref):
    ci = pl.program_id(1)
    L = q_ref.shape[0]
    dh = A_HEAD_DIM

    @pl.when(ci == 0)
    def _():
        c_ref[...] = jnp.zeros_like(c_ref)
        n_ref[...] = jnp.zeros_like(n_ref)
        m_ref[...] = jnp.zeros_like(m_ref)
        qcar_ref[...] = jnp.zeros_like(qcar_ref)
        kcar_ref[...] = jnp.zeros_like(kcar_ref)

    def conv_silu(u_ref, car_ref, lo, wcol):
        u = u_ref[:, lo:lo + dh].astype(F32)
        ext = jnp.concatenate([car_ref[:, lo:lo + dh], u], axis=0)
        w = cw_ref[:, wcol:wcol + dh]
        y = cb_ref[:, wcol:wcol + dh] + w[3:4, :] * u
        for j in range(CONV_WIDTH - 1):
            off = SUBLANES - (CONV_WIDTH - 1) + j
            y = y + w[j:j + 1, :] * ext[off:off + L, :]
        car_ref[:, lo:lo + dh] = u[L - SUBLANES:, :]
        return y * _sigmoid(y)

    r = lax.broadcasted_iota(I32, (L, L), 0)
    c = lax.broadcasted_iota(I32, (L, L), 1)
    causal = r >= c
    gcol = gcol_ref[...]
    grow = grow_ref[...]

    for h in range(A_HEADS):
        lo = h * dh
        q = conv_silu(q_ref, qcar_ref, lo, lo)
        k = conv_silu(k_ref, kcar_ref, lo, A_WIDTH + lo) * (dh ** -0.5)
        v_b = v_ref[:, lo:lo + dh]
        q_b = q.astype(BF16)
        k_b = k.astype(BF16)

        i_col, b_col = gcol[:, h:h + 1], gcol[:, A_HEADS + h:A_HEADS + h + 1]
        i_row, b_row = grow[h:h + 1, :], grow[A_HEADS + h:A_HEADS + h + 1, :]
        b_last = b_row[:, L - 1:L]
        m_prev = m_ref[h]

        dmat = jnp.where(causal, b_col - b_row + i_row, NEG_INF)
        inter = b_col + m_prev
        m_t = jnp.maximum(inter, jnp.max(dmat, axis=-1, keepdims=True))
        s = lax.dot_general(q_b, k_b, (((1,), (1,)), ((), ())), preferred_element_type=F32)
        w_intra = jnp.exp(dmat - m_t) * s
        w_inter = jnp.exp(inter - m_t)
        qc = jnp.dot(q_b, c_ref[h].astype(BF16), preferred_element_type=F32)
        num = w_inter * qc + jnp.dot(w_intra.astype(BF16), v_b, preferred_element_type=F32)
        qn = jnp.sum(q * n_ref[h], axis=-1, keepdims=True)
        den = w_inter * qn + jnp.sum(w_intra, axis=-1, keepdims=True)
        hh = num / jnp.maximum(jnp.abs(den), jnp.exp(-m_t))

        g_col = b_last - b_col + i_col
        m_new = jnp.maximum(b_last + m_prev, jnp.max(g_col, axis=0, keepdims=True))
        decay = jnp.exp(b_last + m_prev - m_new)
        kw = jnp.exp(g_col - m_new) * k
        c_ref[h] = decay * c_ref[h] + jnp.dot(kw.T.astype(BF16), v_b, preferred_element_type=F32)
        n_ref[h] = decay * n_ref[h] + jnp.sum(kw, axis=0, keepdims=True)
        m_ref[h] = m_new

        y = _sigmoid(o_ref[:, lo:lo + dh].astype(F32)) * hh
        mu = jnp.mean(y, axis=-1, keepdims=True)
        yc = y - mu
        var = jnp.mean(yc * yc, axis=-1, keepdims=True)
        out_ref[:, lo:lo + dh] = (yc * lax.rsqrt(var + LN_EPS) * ng_ref[:, lo:lo + dh]).astype(out_ref.dtype)


def _mlstm(z_main, gcol, grow, conv_w, conv_b, norm_g, B, S):
    L = min(MLSTM_CHUNK, S)
    nc = S // L
    dh = A_HEAD_DIM
    H = A_HEADS
    W = A_WIDTH
    row = lambda b, c: b * nc + c
    full = lambda shape: pl.BlockSpec(shape, lambda b, c: (0, 0))
    return pl.pallas_call(
        _mlstm_kernel,
        grid=(B, nc),
        in_specs=[
            pl.BlockSpec((L, W), lambda b, c: (row(b, c), COL_QA // W)),
            pl.BlockSpec((L, W), lambda b, c: (row(b, c), COL_KA // W)),
            pl.BlockSpec((L, W), lambda b, c: (row(b, c), COL_VA // W)),
            pl.BlockSpec((L, W), lambda b, c: (row(b, c), COL_OA // W)),
            pl.BlockSpec((L, SUBLANES), lambda b, c: (row(b, c), 0)),
            pl.BlockSpec((SUBLANES, L), lambda b, c: (0, row(b, c))),
            full((CONV_WIDTH, 2 * W)), full((1, 2 * W)), full((1, W)),
        ],
        out_specs=pl.BlockSpec((L, W), lambda b, c: (row(b, c), 0)),
        out_shape=jax.ShapeDtypeStruct((B * S, W), BF16),
        scratch_shapes=[
            pltpu.VMEM((H, dh, dh), F32),
            pltpu.VMEM((H, 1, dh), F32),
            pltpu.VMEM((H, 1, 1), F32),
            pltpu.VMEM((SUBLANES, W), F32),
            pltpu.VMEM((SUBLANES, W), F32),
        ],
        compiler_params=_cparams(("parallel", "arbitrary")),
        name="mlstm",
    )(z_main, z_main, z_main, z_main, gcol, grow, conv_w, conv_b, norm_g)


def _lane_tiles(t):
    return [t[:, i * LANES:(i + 1) * LANES] for i in range(t.shape[1] // LANES)]


def _attn_kernel(slope_ref, q_ref, k_ref, kb_ref, v_ref, lq1_ref, lk1_ref, lq2_ref, lk2_ref, g_ref, o_ref,
                 s_ref, mx_ref, acc_ref, corr_ref, *, lambda_init):
    h = pl.program_id(1)
    qi = pl.program_id(2)
    tq = q_ref.shape[0]
    tk = tq
    slope = slope_ref[h]
    nt = (((1,), (1,)), ((), ()))

    lane = lax.broadcasted_iota(I32, (tq, B_V_DIM), 1)
    q = q_ref[...] * (B_HEAD_DIM ** -0.5)
    zero = jnp.zeros_like(q)
    bias_cols = jnp.where(lane < 2, 1.0, 0.0).astype(BF16)
    qa = (jnp.concatenate([jnp.where(lane < B_HEAD_DIM, q, zero), bias_cols], axis=1),
          jnp.concatenate([jnp.where(lane >= B_HEAD_DIM, q, zero), bias_cols], axis=1))

    def scores(j):
        start = pl.multiple_of(j * tk, tk)
        ka = jnp.concatenate([k_ref[pl.ds(start, tk), :], kb_ref[pl.ds(start, tk), :]], axis=1)
        return [lax.dot_general(qa[n], ka, nt, preferred_element_type=F32) for n in range(2)]

    def keep(j, n, t, rows=slice(None)):
        t = t * LOG2_E
        s_ref[n, j, rows, :t.shape[1]] = t
        mx_ref[n, rows, :] = functools.reduce(jnp.maximum, _lane_tiles(t), mx_ref[n, rows, :])

    mx_ref[...] = jnp.full(mx_ref.shape, NEG_INF, F32)

    def pass_a(j):
        t = scores(j)
        for n in range(2):
            keep(j, n, t[n])

    def pass_a_pair(jj, carry):
        pass_a(2 * jj)
        pass_a(2 * jj + 1)
        return carry
    lax.fori_loop(0, qi // 2, pass_a_pair, 0)

    @pl.when(qi % 2 == 1)
    def _():
        pass_a(qi - 1)

    rg = tq // ATTN_DIAG_GROUPS

    @pl.when(qi == 0)
    def _():
        r = lax.broadcasted_iota(I32, (rg, rg), 0)
        c = lax.broadcasted_iota(I32, (rg, rg), 1)
        ahead = jnp.where(c <= r, 0.0, (r - c).astype(F32) * (2.0 * slope))
        corr_ref[...] = jnp.where((c // ATTN_CHUNK) <= (r // ATTN_CHUNK), ahead, NEG_INF)

    dstart = pl.multiple_of(qi * tk, tk)
    ka = jnp.concatenate([k_ref[pl.ds(dstart, tk), :], kb_ref[pl.ds(dstart, tk), :]], axis=1)
    for n in range(2):
        for g in range(ATTN_DIAG_GROUPS):
            kbeg, kend = g * rg, (g + 1) * rg
            band = lax.dot_general(qa[n][kbeg:kend], ka[:kend], nt, preferred_element_type=F32)
            square = band[:, kbeg:] + corr_ref[...]
            band = square if g == 0 else jnp.concatenate([band[:, :kbeg], square], axis=1)
            keep(qi, n, band, rows=slice(kbeg, kend))

    for n in range(2):
        m = jnp.max(mx_ref[n], axis=-1, keepdims=True)
        mx_ref[n] = jnp.broadcast_to(m, (tq, LANES))
    acc_ref[...] = jnp.zeros_like(acc_ref)

    def weigh(n, rows, s_tile, v_tile):
        mb = mx_ref[n, rows, :]
        p = jnp.concatenate([jnp.exp2(t - mb) for t in _lane_tiles(s_tile)], axis=1).astype(BF16)
        v_ones = jnp.concatenate([v_tile, jnp.ones((v_tile.shape[0], LANES), BF16)], axis=1)
        acc_ref[n, rows, :] += jnp.dot(p, v_ones, preferred_element_type=F32)

    def pass_b(j, carry):
        start = pl.multiple_of(j * tk, tk)
        vj = v_ref[pl.ds(start, tk), :]
        for n in range(2):
            weigh(n, slice(None), s_ref[n, j], vj)
        return carry
    lax.fori_loop(0, qi, pass_b, 0)

    for n in range(2):
        for g in range(ATTN_DIAG_GROUPS):
            kend = (g + 1) * rg
            weigh(n, slice(g * rg, kend), s_ref[n, qi, g * rg:kend, :kend], v_ref[pl.ds(dstart, kend), :])

    lam = (jnp.exp(jnp.sum(lq1_ref[...] * lk1_ref[...], axis=-1, keepdims=True))
           - jnp.exp(jnp.sum(lq2_ref[...] * lk2_ref[...], axis=-1, keepdims=True)) + lambda_init)
    dv = B_V_DIM
    a0, a1 = acc_ref[0], acc_ref[1]
    o = a0[:, :dv] / a0[:, dv:] - lam * (a1[:, :dv] / a1[:, dv:])
    ms = jnp.mean(o * o, axis=-1, keepdims=True)
    o_ref[...] = (o * lax.rsqrt(ms + LN_EPS) * g_ref[...] * (1.0 - lambda_init)).astype(o_ref.dtype)


def _diff_attention(z_main, lq1, lk1, lq2, lk2, norm_g, B, S, lambda_init):
    tq = min(ATTN_TQ, S)
    nq = S // tq
    H = B_HEADS
    dv = B_V_DIM
    slopes = 2.0 ** (-8.0 * jnp.arange(1, H + 1, dtype=F32) / H)
    assert S <= 256 * 256
    pos = lax.broadcasted_iota(I32, (H, S, dv), 1)
    col = lax.broadcasted_iota(I32, (H, S, dv), 2)
    within = pos % 256
    kbias = jnp.where(col == 0, within, jnp.where(col == 1, pos - within, 0)).astype(F32)
    kbias = (kbias * slopes[:, None, None]).astype(BF16)
    small = pl.BlockSpec((1, B_HEAD_DIM), lambda b, h, i: (0, 0))
    return pl.pallas_call(
        functools.partial(_attn_kernel, lambda_init=lambda_init),
        grid=(B, H, nq),
        in_specs=[
            pl.BlockSpec(memory_space=pltpu.SMEM),
            pl.BlockSpec((tq, dv), lambda b, h, i: (b * nq + i, COL_QB // dv + h)),
            pl.BlockSpec((S, dv), lambda b, h, i: (b, COL_KB // dv + h)),
            pl.BlockSpec((None, S, dv), lambda b, h, i: (h, 0, 0)),
            pl.BlockSpec((S, dv), lambda b, h, i: (b, COL_VB // dv + h)),
            small, small, small, small,
            pl.BlockSpec((1, dv), lambda b, h, i: (0, 0)),
        ],
        out_specs=pl.BlockSpec((tq, dv), lambda b, h, i: (b * nq + i, h)),
        out_shape=jax.ShapeDtypeStruct((B * S, B_WIDTH), BF16),
        scratch_shapes=[
            pltpu.VMEM((2, nq, tq, tq), F32),
            pltpu.VMEM((2, tq, LANES), F32),
            pltpu.VMEM((2, tq, dv + LANES), F32),
            pltpu.VMEM((tq // ATTN_DIAG_GROUPS, tq // ATTN_DIAG_GROUPS), F32),
        ],
        compiler_params=_cparams(("parallel", "parallel", "arbitrary")),
        name="diff_attention",
    )(slopes, z_main, z_main, kbias, z_main, lq1, lk1, lq2, lk2, norm_g)


def _mix_kernel(ha_ref, ob_ref, ga_ref, gb_ref, x_ref, wa_ref, wb_ref, wo_ref, g1_ref, b1_ref, wr_ref, br_ref,
                x1_ref, x1p_ref, ri_ref, rw_ref, cnt_ref, carry_ref):
    i = pl.program_id(0)
    tm = x_ref.shape[0]

    @pl.when(i == 0)
    def _():
        carry_ref[...] = jnp.zeros_like(carry_ref)

    ya = jnp.dot(ha_ref[...], wa_ref[...], preferred_element_type=F32)
    yb = jnp.dot(ob_ref[...], wb_ref[...], preferred_element_type=F32)
    merged = _sigmoid(ga_ref[...].astype(F32)) * ya + _sigmoid(gb_ref[...].astype(F32)) * yb
    mix = jnp.dot(merged.astype(BF16), wo_ref[...], preferred_element_type=F32)
    pre = DEEPNORM_ALPHA * x_ref[...] + mix
    mu = jnp.mean(pre, axis=-1, keepdims=True)
    pc = pre - mu
    var = jnp.mean(pc * pc, axis=-1, keepdims=True)
    x1 = pc * lax.rsqrt(var + LN_EPS) * g1_ref[...] + b1_ref[...]
    x1_ref[...] = x1
    _rows_to_tiles(x1p_ref, _pack_halves(x1))

    logits = _dot_3pass(x1, wr_ref[...]) + br_ref[...]
    lt = logits.T
    row8 = lax.broadcasted_iota(I32, (SUBLANES, tm), 0).astype(F32)
    gl = jnp.where(row8 < N_GROUPS, lt[0:SUBLANES, :], NEG_INF)
    gmax = jnp.max(gl, axis=0, keepdims=True)
    gsel = jnp.min(jnp.where(gl == gmax, row8, float(SUBLANES)), axis=0, keepdims=True)
    gprob = 1.0 / jnp.sum(jnp.exp(gl - gmax), axis=0, keepdims=True)
    ing = lt[SUBLANES:2 * SUBLANES, :]
    for g in range(1, N_GROUPS):
        ing = jnp.where(gsel == g, lt[(g + 1) * SUBLANES:(g + 2) * SUBLANES, :], ing)
    v0 = jnp.max(ing, axis=0, keepdims=True)
    i0 = jnp.min(jnp.where(ing == v0, row8, float(SUBLANES)), axis=0, keepdims=True)
    ing2 = jnp.where(row8 == i0, -jnp.inf, ing)
    v1 = jnp.max(ing2, axis=0, keepdims=True)
    i1 = jnp.min(jnp.where(ing2 == v1, row8, float(SUBLANES)), axis=0, keepdims=True)
    ex = jnp.exp(v1 - v0)
    inv = 1.0 / (1.0 + ex)
    w0 = gprob * inv
    w1 = gprob * (ex * inv)
    e0 = (gsel * EXPERTS_PER_GROUP + i0).astype(I32)
    e1 = (gsel * EXPERTS_PER_GROUP + i1).astype(I32)

    rowe = lax.broadcasted_iota(I32, (N_EXPERTS, tm), 0)
    is0 = rowe == e0
    is1 = rowe == e1
    oh = jnp.where(is0, 1.0, 0.0) + jnp.where(is1, 1.0, 0.0)
    tr = lax.broadcasted_iota(I32, (tm, tm), 0)
    tc = lax.broadcasted_iota(I32, (tm, tm), 1)
    su = jnp.where(tr < tc, 1.0, 0.0).astype(BF16)
    tot = jnp.dot(oh.astype(BF16), su, preferred_element_type=F32) + carry_ref[...]
    rank0 = jnp.sum(jnp.where(is0, tot, 0.0), axis=0, keepdims=True)
    rank1 = jnp.sum(jnp.where(is1, tot, 0.0), axis=0, keepdims=True)
    carry = carry_ref[...] + jnp.sum(oh, axis=1, keepdims=True)
    carry_ref[...] = carry
    cnt_ref[...] = jnp.broadcast_to(carry, cnt_ref.shape).astype(I32)

    zi = jnp.zeros((1, tm), I32)
    ri_ref[...] = jnp.concatenate([e0, e1, rank0.astype(I32), rank1.astype(I32), zi, zi, zi, zi], axis=0)
    zf = jnp.zeros((1, tm), F32)
    rw_ref[...] = jnp.concatenate([w0, w1, zf, zf, zf, zf, zf, zf], axis=0)


def _mix(h_a, o_b, z_main, x2d, w_a, w_b, w_out, ln_g, ln_b, w_r, b_r):
    T, D = x2d.shape
    tm = min(MIX_TM, T)
    const = lambda shape: pl.BlockSpec(shape, lambda i: (0, 0), pipeline_mode=pl.Buffered(1))
    return pl.pallas_call(
        _mix_kernel,
        grid=(T // tm,),
        in_specs=[
            pl.BlockSpec((tm, A_WIDTH), lambda i: (i, 0)),
            pl.BlockSpec((tm, B_WIDTH), lambda i: (i, 0)),
            pl.BlockSpec((tm, D), lambda i: (i, COL_GA // D)),
            pl.BlockSpec((tm, D), lambda i: (i, COL_GB // D)),
            pl.BlockSpec((tm, D), lambda i: (i, 0)),
            const((A_WIDTH, D)), const((B_WIDTH, D)), const((D, D)),
            const((1, D)), const((1, D)), const((D, 2 * LANES)), const((1, LANES)),
        ],
        out_specs=[
            pl.BlockSpec((tm, D), lambda i: (i, 0)),
            pl.BlockSpec((tm * SUBLANES, LANES), lambda i: (i, 0)),
            pl.BlockSpec((SUBLANES, tm), lambda i: (0, i)),
            pl.BlockSpec((SUBLANES, tm), lambda i: (0, i)),
            pl.BlockSpec((N_EXPERTS, LANES), lambda i: (0, 0)),
        ],
        out_shape=[
            jax.ShapeDtypeStruct((T, D), F32),
            jax.ShapeDtypeStruct((T * SUBLANES, LANES), U32),
            jax.ShapeDtypeStruct((SUBLANES, T), I32),
            jax.ShapeDtypeStruct((SUBLANES, T), F32),
            jax.ShapeDtypeStruct((N_EXPERTS, LANES), I32),
        ],
        scratch_shapes=[pltpu.VMEM((N_EXPERTS, 1), F32)],
        compiler_params=_cparams(("arbitrary",)),
        name="mix_ln1_router",
    )(h_a, o_b, z_main, z_main, x2d, w_a, w_b, w_out, ln_g, ln_b, w_r, b_r)


def _row_copy(src_ref, src_row, dst_ref, dst_row, sem):
    return pltpu.make_async_copy(src_ref.at[pl.ds(src_row * SUBLANES, SUBLANES)],
                                 dst_ref.at[pl.ds(dst_row * SUBLANES, SUBLANES)], sem)


def _rows_wait(src_ref, dst_ref, rows, sem):
    n = rows * SUBLANES
    pltpu.make_async_copy(src_ref.at[pl.ds(0, n)], dst_ref.at[pl.ds(0, n)], sem).wait()


DISPATCH_SLOTS = 3


def _dispatch_kernel(cnt_ref, ri_ref, x1_ref, xpad_ref, pstart_ref, bexp_ref, nused_ref,
                     pst_ref, zero_ref, xbuf_ref, sem, lsem, zsem, *, tb, n_blocks, tm):
    i = pl.program_id(0)
    last = pl.num_programs(0) - 1

    @pl.when(i == 0)
    def _():
        def fill(b, carry):
            bexp_ref[b] = 0
            return carry
        lax.fori_loop(0, n_blocks, fill, 0)

        def per_expert(e, blk):
            pst_ref[e] = blk * tb
            pstart_ref[e] = blk * tb
            nb = (cnt_ref[e] + (tb - 1)) // tb

            def mark(b, carry):
                bexp_ref[blk + b] = e
                return carry
            lax.fori_loop(0, nb, mark, 0)
            return blk + nb
        nused = lax.fori_loop(0, N_EXPERTS, per_expert, 0)
        nused_ref[0] = nused
        pst_ref[N_EXPERTS] = nused

    rows = tm * SUBLANES
    slot = i % DISPATCH_SLOTS

    def tile_load(tile, s):
        src = x1_ref.at[pl.ds(pl.multiple_of(tile * rows, rows), rows)]
        return pltpu.make_async_copy(src, xbuf_ref.at[s], lsem.at[s])

    def wait_rows(s):
        for k in range(TOP_K):
            _rows_wait(xbuf_ref.at[s], xpad_ref, tm, sem.at[s])

    @pl.when(i == 0)
    def _():
        tile_load(0, 0).start()

        @pl.when(last >= 1)
        def _():
            tile_load(1, 1).start()

    tile_load(i, slot).wait()
    for t in range(tm):
        for k in range(TOP_K):
            pos = pst_ref[ri_ref[k, t]] + ri_ref[TOP_K + k, t]
            _row_copy(xbuf_ref.at[slot], t, xpad_ref, pos, sem.at[slot]).start(priority=k)

    prev = (i + DISPATCH_SLOTS - 1) % DISPATCH_SLOTS

    @pl.when(i >= 1)
    def _():
        wait_rows(prev)

    @pl.when(i + 2 <= last)
    def _():
        tile_load(i + 2, prev).start()

    @pl.when(i == last)
    def _():
        wait_rows(slot)
        zero_ref[...] = jnp.zeros_like(zero_ref)

        def per_expert(e, carry):
            cnt = cnt_ref[e]
            first = pst_ref[e] + cnt
            npad = ((cnt + (tb - 1)) // tb) * tb - cnt

            def start(p, c2):
                _row_copy(zero_ref, 0, xpad_ref, first + p, zsem).start()
                return c2
            lax.fori_loop(0, npad, start, 0)

            def wait(p, c2):
                _row_copy(zero_ref, 0, xpad_ref, 0, zsem).wait()
                return c2
            lax.fori_loop(0, npad, wait, 0)
            return carry
        lax.fori_loop(0, N_EXPERTS, per_expert, 0)

        def block_copy(b):
            n = tb * SUBLANES
            return pltpu.make_async_copy(zero_ref, xpad_ref.at[pl.ds(pl.multiple_of(b * n, n), n)], zsem)

        def start_block(b, carry):
            block_copy(b).start()
            return carry
        lax.fori_loop(pst_ref[N_EXPERTS], n_blocks, start_block, 0)

        def wait_block(b, carry):
            block_copy(b).wait()
            return carry
        lax.fori_loop(pst_ref[N_EXPERTS], n_blocks, wait_block, 0)


def _dispatch(counts, route_i, x1, tb, n_blocks):
    T = x1.shape[0] // SUBLANES
    tm = min(DISPATCH_TM, T)
    smem_full = pl.BlockSpec(memory_space=pltpu.SMEM)
    return pl.pallas_call(
        functools.partial(_dispatch_kernel, tb=tb, n_blocks=n_blocks, tm=tm),
        grid=(T // tm,),
        in_specs=[
            smem_full,
            pl.BlockSpec((SUBLANES, tm), lambda i: (0, i), memory_space=pltpu.SMEM),
            pl.BlockSpec(memory_space=pl.ANY),
        ],
        out_specs=[
            pl.BlockSpec(memory_space=pl.ANY),
            smem_full, smem_full, smem_full,
        ],
        out_shape=[
            jax.ShapeDtypeStruct((n_blocks * tb * SUBLANES, LANES), x1.dtype),
            jax.ShapeDtypeStruct((N_EXPERTS,), I32),
            jax.ShapeDtypeStruct((n_blocks,), I32),
            jax.ShapeDtypeStruct((1,), I32),
        ],
        scratch_shapes=[
            pltpu.SMEM((N_EXPERTS + 1,), I32),
            pltpu.VMEM((tb * SUBLANES, LANES), x1.dtype),
            pltpu.VMEM((DISPATCH_SLOTS, tm * SUBLANES, LANES), x1.dtype),
            pltpu.SemaphoreType.DMA((DISPATCH_SLOTS,)),
            pltpu.SemaphoreType.DMA((DISPATCH_SLOTS,)),
            pltpu.SemaphoreType.DMA(()),
        ],
        compiler_params=_cparams(("arbitrary",)),
        name="moe_dispatch",
    )(counts, route_i, x1)


def _expert_kernel(bexp_ref, nused_ref, x_ref, wg_hbm, wu_hbm, wd_hbm, y_ref,
                   wgf_ref, wuf_ref, wdf_ref, wgb_ref, wub_ref, wdb_ref, sem, run_ref):
    b = pl.program_id(0)
    nused = nused_ref[0]
    used = b < nused
    e = bexp_ref[b]
    new_expert = jnp.logical_or(b == 0, e != bexp_ref[jnp.maximum(b - 1, 0)])
    streams = ((wg_hbm, wgf_ref), (wu_hbm, wuf_ref), (wd_hbm, wdf_ref))

    def weight_copies(expert, slot):
        return [pltpu.make_async_copy(w_hbm.at[expert], wf_ref.at[slot], sem.at[slot, j])
                for j, (w_hbm, wf_ref) in enumerate(streams)]

    @pl.when(jnp.logical_and(used, b == 0))
    def _():
        run_ref[0] = 0
        for cp in weight_copies(e, 0):
            cp.start()

    @pl.when(jnp.logical_and(used, new_expert))
    def _():
        slot = run_ref[0] % 2
        nxt = lax.while_loop(lambda j: jnp.logical_and(j < nused, bexp_ref[jnp.minimum(j, nused - 1)] == e),
                             lambda j: j + 1, b + 1)

        @pl.when(nxt < nused)
        def _():
            for cp in weight_copies(bexp_ref[jnp.minimum(nxt, nused - 1)], 1 - slot):
                cp.start()

        for cp in weight_copies(e, slot):
            cp.wait()
        wgb_ref[...] = wgf_ref[slot].astype(BF16)
        wub_ref[...] = wuf_ref[slot].astype(BF16)
        wdb_ref[...] = wdf_ref[slot].astype(BF16)
        run_ref[0] = run_ref[0] + 1

    @pl.when(used)
    def _():
        half = SUBLANES * LANES
        x_lo, x_hi = (t.astype(BF16) for t in _unpack_halves(_tiles_to_rows(x_ref)))
        dot = functools.partial(jnp.dot, preferred_element_type=F32)
        gate = dot(x_lo, wgb_ref[:half, :]) + dot(x_hi, wgb_ref[half:, :])
        up = dot(x_lo, wub_ref[:half, :]) + dot(x_hi, wub_ref[half:, :])
        hid = (gate * _sigmoid(gate) * up).astype(BF16)
        _rows_to_tiles(y_ref, _pack_halves(dot(hid, wdb_ref[...])))

    @pl.when(jnp.logical_not(used))
    def _():
        y_ref[...] = jnp.zeros_like(y_ref)


def _experts(bexp, nused, x_pad, w_gate, w_up, w_down, tb):
    P = x_pad.shape[0] // SUBLANES
    D = 2 * TILE_WORDS
    n_blocks = P // tb
    rows = (tb * SUBLANES, LANES)
    F = w_gate.shape[-1]
    blk = lambda b, be, nu: jnp.maximum(jnp.minimum(b, nu[0] - 1), 0)
    hbm = pl.BlockSpec(memory_space=pl.ANY)
    return pl.pallas_call(
        _expert_kernel,
        grid_spec=pltpu.PrefetchScalarGridSpec(
            num_scalar_prefetch=2,
            grid=(n_blocks,),
            in_specs=[pl.BlockSpec(rows, lambda b, be, nu: (blk(b, be, nu), 0)), hbm, hbm, hbm],
            out_specs=pl.BlockSpec(rows, lambda b, be, nu: (b, 0)),
            scratch_shapes=[
                pltpu.VMEM((2, D, F), F32), pltpu.VMEM((2, D, F), F32), pltpu.VMEM((2, F, D), F32),
                pltpu.VMEM((D, F), BF16), pltpu.VMEM((D, F), BF16), pltpu.VMEM((F, D), BF16),
                pltpu.SemaphoreType.DMA((2, 3)),
                pltpu.SMEM((1,), I32),
            ],
        ),
        out_shape=jax.ShapeDtypeStruct(x_pad.shape, U32),
        compiler_params=_cparams(("arbitrary",)),
        name="moe_experts",
    )(bexp, nused, x_pad, w_gate, w_up, w_down)


COMBINE_SLOTS = 3


def _combine_kernel(pstart_ref, ri_ref, r1_ref, r2_ref, x1_ref, rw_ref, g_ref, b_ref, ypad_ref, o_ref, ybuf_ref, sem):
    i = pl.program_id(0)
    n = pl.num_programs(0)
    tm = x1_ref.shape[0]
    slot = i % COMBINE_SLOTS

    def gather_loop(route_ref, s):
        def issue(t, carry):
            for k in range(TOP_K):
                pos = pstart_ref[route_ref[k, t]] + route_ref[TOP_K + k, t]
                _row_copy(ypad_ref, pos, ybuf_ref.at[s, k], t, sem.at[s]).start(priority=k)
            return carry
        lax.fori_loop(0, tm, issue, 0, unroll=8)

    def wait_slot(s):
        for k in range(TOP_K):
            _rows_wait(ypad_ref, ybuf_ref.at[s, k], tm, sem.at[s])

    @pl.when(i == 0)
    def _():
        gather_loop(ri_ref, 0)
        gather_loop(r1_ref, 1)

    wait_slot(slot)

    rw = rw_ref[...]
    y0 = _unpack_halves(_tiles_to_rows(ybuf_ref.at[slot, 0]))
    y1 = _unpack_halves(_tiles_to_rows(ybuf_ref.at[slot, 1]))
    ffn = jnp.concatenate([y0[h] * rw[:, 0:1] + y1[h] * rw[:, 1:2] for h in range(2)], axis=1)
    pre = DEEPNORM_ALPHA * x1_ref[...] + ffn
    mu = jnp.mean(pre, axis=-1, keepdims=True)
    pc = pre - mu
    var = jnp.mean(pc * pc, axis=-1, keepdims=True)
    o_ref[...] = pc * lax.rsqrt(var + LN_EPS) * g_ref[...] + b_ref[...]

    nxt = (i + 2) % COMBINE_SLOTS
    for t in range(tm):
        for k in range(TOP_K):
            pos = pstart_ref[r2_ref[k, t]] + r2_ref[TOP_K + k, t]
            _row_copy(ypad_ref, pos, ybuf_ref.at[nxt, k], t, sem.at[nxt]).start(priority=k)

    @pl.when(i == n - 1)
    def _():
        wait_slot((i + 1) % COMBINE_SLOTS)
        wait_slot(nxt)


def _combine(pstart, route_i, x1, rw_col, ln_g, ln_b, y_pad):
    T, D = x1.shape
    tm = min(COMBINE_TM, T)
    n = T // tm
    return pl.pallas_call(
        _combine_kernel,
        grid=(n,),
        in_specs=[
            pl.BlockSpec(memory_space=pltpu.SMEM),
            pl.BlockSpec((SUBLANES, tm), lambda i: (0, i), memory_space=pltpu.SMEM),
            pl.BlockSpec((SUBLANES, tm), lambda i: (0, jnp.minimum(i + 1, n - 1)), memory_space=pltpu.SMEM),
            pl.BlockSpec((SUBLANES, tm), lambda i: (0, jnp.minimum(i + 2, n - 1)), memory_space=pltpu.SMEM),
            pl.BlockSpec((tm, D), lambda i: (i, 0)),
            pl.BlockSpec((tm, SUBLANES), lambda i: (i, 0)),
            pl.BlockSpec((1, D), lambda i: (0, 0)),
            pl.BlockSpec((1, D), lambda i: (0, 0)),
            pl.BlockSpec(memory_space=pl.ANY),
        ],
        out_specs=pl.BlockSpec((tm, D), lambda i: (i, 0)),
        out_shape=jax.ShapeDtypeStruct((T, D), F32),
        scratch_shapes=[
            pltpu.VMEM((COMBINE_SLOTS, TOP_K, tm * SUBLANES, LANES), U32),
            pltpu.SemaphoreType.DMA((COMBINE_SLOTS,)),
        ],
        compiler_params=_cparams(("arbitrary",)),
        name="moe_combine_ln2",
    )(pstart, route_i, route_i, route_i, x1, rw_col, ln_g, ln_b, y_pad)


N_GATE_COLS = 2 * A_HEADS
SRC_MLSTM = 0
SRC_GATES = 4 * A_WIDTH
SRC_DIFF = SRC_GATES + N_GATE_COLS
SRC_MERGE = SRC_DIFF + 3 * B_WIDTH
PREP_TN = 512


def _wprep_kernel(a_ref, o_ref):
    o_ref[...] = a_ref[...].T.astype(BF16)


def _wgate_kernel(a_ref, o_ref):
    a = a_ref[...]
    rows = jnp.concatenate([a, jnp.zeros((LANES - a.shape[0], a.shape[1]), F32)], axis=0)
    o_ref[...] = rows.T


def _rearrange_in_proj(w_in, b_in):
    K, n_in = w_in.shape
    tn = PREP_TN
    merge_blocks = (2 * D_MODEL) // tn
    mlstm_blocks = (4 * A_WIDTH) // tn
    assert SRC_MERGE % SUBLANES == 0 and SRC_DIFF % SUBLANES == 0

    def src_row(jb):
        s = SUBLANES
        merge = SRC_MERGE // s + (tn // s) * jb
        mlstm = SRC_MLSTM // s + (tn // s) * (jb - merge_blocks)
        diff = SRC_DIFF // s + (tn // s) * (jb - merge_blocks - mlstm_blocks)
        return s * jnp.where(jb < merge_blocks, merge, jnp.where(jb < merge_blocks + mlstm_blocks, mlstm, diff))

    w_t = jnp.swapaxes(w_in, 0, 1)
    w_main = pl.pallas_call(
        _wprep_kernel,
        grid=(N_MAIN // tn,),
        in_specs=[pl.BlockSpec((pl.Element(tn), pl.Element(K)), lambda jb: (src_row(jb), 0))],
        out_specs=pl.BlockSpec((K, tn), lambda jb: (0, jb)),
        out_shape=jax.ShapeDtypeStruct((K, N_MAIN), BF16),
        compiler_params=_cparams(("parallel",)),
        name="w_in_prep",
    )(w_t)
    b_main = jnp.concatenate([b_in[SRC_MERGE:], b_in[SRC_MLSTM:SRC_GATES], b_in[SRC_DIFF:SRC_MERGE]])[None, :]
    w_gate = pl.pallas_call(
        _wgate_kernel,
        grid=(1,),
        in_specs=[pl.BlockSpec((pl.Element(N_GATE_COLS), pl.Element(K)), lambda i: (SRC_GATES, 0))],
        out_specs=pl.BlockSpec((K, LANES), lambda i: (0, 0)),
        out_shape=jax.ShapeDtypeStruct((K, LANES), F32),
        name="w_gate_prep",
    )(w_t)
    b_gate = jnp.pad(b_in[SRC_GATES:SRC_DIFF], (0, LANES - N_GATE_COLS))[None, :]
    return w_main, b_main, _hi_lo_columns(w_gate), b_gate


def _layer(x, w_in, b_in, conv_w, conv_b, norm_a_g, lq1, lk1, lq2, lk2, norm_b_g, w_a, w_b, w_out,
           ln1_g, ln1_b, w_grp, b_grp, w_exp, b_exp, w_gate, w_up, w_down, ln2_g, ln2_b, lambda_init):
    B, S, D = x.shape
    T = B * S
    x2d = x.reshape(T, D)

    w_main, b_main, w_g, b_g = _rearrange_in_proj(w_in, b_in)
    z_main, z_gate = _in_projection(x2d, w_main, b_main, w_g, b_g)

    L = min(MLSTM_CHUNK, S)
    gp = _gate_prep(z_gate[:, :2 * A_HEADS].T, L)
    h_a = _mlstm(z_main, gp.T, gp, conv_w, conv_b[None, :], norm_a_g[None, :], B, S)
    o_b = _diff_attention(z_main, lq1[None, :], lk1[None, :], lq2[None, :], lk2[None, :],
                          norm_b_g[None, :], B, S, lambda_init)

    w_r = jnp.zeros((D, LANES), F32).at[:, :N_GROUPS].set(w_grp).at[:, SUBLANES:SUBLANES + N_EXPERTS].set(w_exp)
    b_r = jnp.zeros((LANES,), F32).at[:N_GROUPS].set(b_grp).at[SUBLANES:SUBLANES + N_EXPERTS].set(b_exp)[None, :]
    x1, x1_packed, route_i, route_w, counts = _mix(h_a, o_b, z_main, x2d, w_a.astype(BF16), w_b.astype(BF16),
                                                   w_out.astype(BF16), ln1_g[None, :], ln1_b[None, :],
                                                   _hi_lo_columns(w_r), b_r)

    tb = MOE_TB
    n_blocks = (T * TOP_K) // tb + N_EXPERTS
    x_pad, pstart, bexp, nused = _dispatch(counts[:, 0], route_i, x1_packed, tb, n_blocks)
    y_pad = _experts(bexp, nused, x_pad, w_gate, w_up, w_down, tb)
    out = _combine(pstart, route_i, x1, route_w.T, ln2_g[None, :], ln2_b[None, :], y_pad)
    return out.reshape(B, S, D)


def kernel(x, w_in, b_in, conv_w, conv_b, mlstm_norm_g, lambda_q1, lambda_k1, lambda_q2, lambda_k2,
           diff_norm_g, w_a, w_b, w_out, ln1_g, ln1_b, w_grp, b_grp, w_exp, b_exp,
           w_gate, w_up, w_down, ln2_g, ln2_b):
    for l in range(DEPTH):
        lambda_init = 0.8 - 0.6 * math.exp(-0.3 * l)
        x = _layer(x, w_in[l], b_in[l], conv_w[l], conv_b[l], mlstm_norm_g[l], lambda_q1[l], lambda_k1[l],
                   lambda_q2[l], lambda_k2[l], diff_norm_g[l], w_a[l], w_b[l], w_out[l], ln1_g[l], ln1_b[l],
                   w_grp[l], b_grp[l], w_exp[l], b_exp[l], w_gate[l], w_up[l], w_down[l], ln2_g[l], ln2_b[l],
                   lambda_init)
    return x
```

```python
import functools
import math

import jax
import jax.numpy as jnp
from jax import lax
from jax.experimental import pallas as pl
from jax.experimental.pallas import tpu as pltpu

F32 = jnp.float32
BF16 = jnp.bfloat16
I32 = jnp.int32
U32 = jnp.uint32

D_MODEL = 2048
A_HEADS = 4
A_HEAD_DIM = 256
A_WIDTH = A_HEADS * A_HEAD_DIM
CONV_WIDTH = 4
B_HEADS = 8
B_HEAD_DIM = 64
B_V_DIM = 2 * B_HEAD_DIM
B_WIDTH = B_HEADS * B_V_DIM
ATTN_CHUNK = 64
N_GROUPS = 4
EXPERTS_PER_GROUP = 8
N_EXPERTS = N_GROUPS * EXPERTS_PER_GROUP
TOP_K = 2
D_EXPERT = 512
DEPTH = 1
DEEPNORM_ALPHA = (2 * DEPTH) ** 0.25
LN_EPS = 1e-5
NEG_INF = float("-inf")
LOG2_E = 1.4426950408889634

LANES = 128
SUBLANES = 8
VMEM_LIMIT_BYTES = 56 * 1024 * 1024

COL_GA = 0
COL_GB = COL_GA + D_MODEL
COL_QA = COL_GB + D_MODEL
COL_KA = COL_QA + A_WIDTH
COL_VA = COL_KA + A_WIDTH
COL_OA = COL_VA + A_WIDTH
COL_QB = COL_OA + A_WIDTH
COL_KB = COL_QB + B_WIDTH
COL_VB = COL_KB + B_WIDTH
N_MAIN = COL_VB + B_WIDTH

PROJ_TM, PROJ_TN = 1024, 1024
MLSTM_CHUNK = 512
ATTN_TQ = 1024
ATTN_DIAG_GROUPS = 4
MIX_TM = 512
MOE_TB = 256
DISPATCH_TM = 256
COMBINE_TM = 256


def _cparams(sem, vmem=VMEM_LIMIT_BYTES):
    return pltpu.CompilerParams(dimension_semantics=sem, vmem_limit_bytes=vmem)


def _sigmoid(x):
    return 1.0 / (1.0 + jnp.exp(-x))


def _pack_halves(x):
    n = x.shape[1] // 2
    lo = lax.bitcast_convert_type(x[:, :n].astype(BF16).astype(F32), U32)
    hi = lax.bitcast_convert_type(x[:, n:].astype(BF16).astype(F32), U32)
    return (lo >> 16) | hi


def _unpack_halves(u):
    lo = lax.bitcast_convert_type(u << 16, F32)
    hi = lax.bitcast_convert_type(u & jnp.uint32(0xFFFF0000), F32)
    return lo, hi


TILE_WORDS = SUBLANES * LANES


def _rows_to_tiles(ref, v):
    rows = v.shape[0]
    for j in range(SUBLANES):
        ref[pl.ds(j, rows, stride=SUBLANES), :] = v[:, j * LANES:(j + 1) * LANES]


def _tiles_to_rows(ref):
    rows = ref.shape[0] // SUBLANES
    return jnp.concatenate([ref[pl.ds(j, rows, stride=SUBLANES), :] for j in range(SUBLANES)], axis=1)


def _split_bf16(a):
    hi = a.astype(BF16)
    return hi, (a - hi.astype(F32)).astype(BF16)


def _hi_lo_columns(w):
    return jnp.concatenate(_split_bf16(w), axis=1)


def _dot_3pass(a, w_hl):
    a_hi, a_lo = _split_bf16(a)
    n = w_hl.shape[1] // 2
    dot = functools.partial(jnp.dot, preferred_element_type=F32)
    r = dot(a_hi, w_hl)
    return r[:, :n] + (r[:, n:] + dot(a_lo, w_hl[:, :n]))


def _proj_kernel(x_ref, w_ref, b_ref, wg_ref, bg_ref, z_ref, zg_ref, xb_ref):
    @pl.when(pl.program_id(1) == 0)
    def _():
        x = x_ref[...]
        xb_ref[...] = x.astype(BF16)
        zg_ref[...] = _dot_3pass(x, wg_ref[...]) + bg_ref[...]

    acc = jnp.dot(xb_ref[...], w_ref[...], preferred_element_type=F32)
    z_ref[...] = (acc + b_ref[...]).astype(z_ref.dtype)


def _in_projection(x2d, w_main, b_main, w_gate, b_gate):
    T, K = x2d.shape
    N = w_main.shape[1]
    tm, tn = min(PROJ_TM, T), PROJ_TN
    return pl.pallas_call(
        _proj_kernel,
        grid=(T // tm, N // tn),
        in_specs=[
            pl.BlockSpec((tm, K), lambda i, j: (i, 0)),
            pl.BlockSpec((K, tn), lambda i, j: (0, j)),
            pl.BlockSpec((1, tn), lambda i, j: (0, j)),
            pl.BlockSpec((K, 2 * LANES), lambda i, j: (0, 0)),
            pl.BlockSpec((1, LANES), lambda i, j: (0, 0)),
        ],
        out_specs=[
            pl.BlockSpec((tm, tn), lambda i, j: (i, j)),
            pl.BlockSpec((tm, LANES), lambda i, j: (i, 0)),
        ],
        out_shape=[
            jax.ShapeDtypeStruct((T, N), BF16),
            jax.ShapeDtypeStruct((T, LANES), F32),
        ],
        scratch_shapes=[pltpu.VMEM((tm, K), BF16)],
        compiler_params=_cparams(("parallel", "arbitrary")),
        name="in_projection",
    )(x2d, w_main, b_main, w_gate, b_gate)


def _gate_prep_kernel(g_ref, o_ref, *, L):
    r = lax.broadcasted_iota(I32, (L, L), 0)
    c = lax.broadcasted_iota(I32, (L, L), 1)
    tri = (r <= c).astype(F32)
    row = lax.broadcasted_iota(I32, (g_ref.shape[0], L), 0)
    for ch in range(g_ref.shape[1] // L):
        g = g_ref[:, ch * L:(ch + 1) * L]
        lf = jnp.minimum(g, 0.0) - jnp.log(1.0 + jnp.exp(-jnp.abs(g)))
        b = jnp.dot(lf, tri, preferred_element_type=F32, precision=lax.Precision.HIGHEST)
        o_ref[:, ch * L:(ch + 1) * L] = jnp.where(row < A_HEADS, g, b)


def _gate_prep(g_rows, L):
    R, T = g_rows.shape
    per_step = math.gcd(T // L, 8)
    W = L * per_step
    return pl.pallas_call(
        functools.partial(_gate_prep_kernel, L=L),
        grid=(T // W,),
        in_specs=[pl.BlockSpec((R, W), lambda i: (0, i))],
        out_specs=pl.BlockSpec((R, W), lambda i: (0, i)),
        out_shape=jax.ShapeDtypeStruct((R, T), F32),
        compiler_params=_cparams(("parallel",)),
        name="gate_prep",
    )(g_rows)


def _mlstm_kernel(q_ref, k_ref, v_ref, o_ref, gcol_ref, grow_ref, cw_ref, cb_ref, ng_ref, out_ref,
                  c_ref, n_ref, m_ref, qcar_ref, kcar_ref):
    ci = pl.program_id(1)
    L = q_ref.shape[0]
    dh = A_HEAD_DIM

    @pl.when(ci == 0)
    def _():
        c_ref[...] = jnp.zeros_like(c_ref)
        n_ref[...] = jnp.zeros_like(n_ref)
        m_ref[...] = jnp.zeros_like(m_ref)
        qcar_ref[...] = jnp.zeros_like(qcar_ref)
        kcar_ref[...] = jnp.zeros_like(kcar_ref)

    def conv_silu(u_ref, car_ref, lo, wcol):
        u = u_ref[:, lo:lo + dh].astype(F32)
        ext = jnp.concatenate([car_ref[:, lo:lo + dh], u], axis=0)
        w = cw_ref[:, wcol:wcol + dh]
        y = cb_ref[:, wcol:wcol + dh] + w[3:4, :] * u
        for j in range(CONV_WIDTH - 1):
            off = SUBLANES - (CONV_WIDTH - 1) + j
            y = y + w[j:j + 1, :] * ext[off:off + L, :]
        car_ref[:, lo:lo + dh] = u[L - SUBLANES:, :]
        return y * _sigmoid(y)

    r = lax.broadcasted_iota(I32, (L, L), 0)
    c = lax.broadcasted_iota(I32, (L, L), 1)
    causal = r >= c
    gcol = gcol_ref[...]
    grow = grow_ref[...]

    for h in range(A_HEADS):
        lo = h * dh
        q = conv_silu(q_ref, qcar_ref, lo, lo)
        k = conv_silu(k_ref, kcar_ref, lo, A_WIDTH + lo) * (dh ** -0.5)
        v_b = v_ref[:, lo:lo + dh]
        q_b = q.astype(BF16)
        k_b = k.astype(BF16)

        i_col, b_col = gcol[:, h:h + 1], gcol[:, A_HEADS + h:A_HEADS + h + 1]
        i_row, b_row = grow[h:h + 1, :], grow[A_HEADS + h:A_HEADS + h + 1, :]
        b_last = b_row[:, L - 1:L]
        m_prev = m_ref[h]

        dmat = jnp.where(causal, b_col - b_row + i_row, NEG_INF)
        inter = b_col + m_prev
        m_t = jnp.maximum(inter, jnp.max(dmat, axis=-1, keepdims=True))
        s = lax.dot_general(q_b, k_b, (((1,), (1,)), ((), ())), preferred_element_type=F32)
        w_intra = jnp.exp(dmat - m_t) * s
        w_inter = jnp.exp(inter - m_t)
        qc = jnp.dot(q_b, c_ref[h].astype(BF16), preferred_element_type=F32)
        num = w_inter * qc + jnp.dot(w_intra.astype(BF16), v_b, preferred_element_type=F32)
        qn = jnp.sum(q * n_ref[h], axis=-1, keepdims=True)
        den = w_inter * qn + jnp.sum(w_intra, axis=-1, keepdims=True)
        hh = num / jnp.maximum(jnp.abs(den), jnp.exp(-m_t))

        g_col = b_last - b_col + i_col
        m_new = jnp.maximum(b_last + m_prev, jnp.max(g_col, axis=0, keepdims=True))
        decay = jnp.exp(b_last + m_prev - m_new)
        kw = jnp.exp(g_col - m_new) * k
        c_ref[h] = decay * c_ref[h] + jnp.dot(kw.T.astype(BF16), v_b, preferred_element_type=F32)
        n_ref[h] = decay * n_ref[h] + jnp.sum(kw, axis=0, keepdims=True)
        m_ref[h] = m_new

        y = _sigmoid(o_ref[:, lo:lo + dh].astype(F32)) * hh
        mu = jnp.mean(y, axis=-1, keepdims=True)
        yc = y - mu
        var = jnp.mean(yc * yc, axis=-1, keepdims=True)
        out_ref[:, lo:lo + dh] = (yc * lax.rsqrt(var + LN_EPS) * ng_ref[:, lo:lo + dh]).astype(out_ref.dtype)


def _mlstm(z_main, gcol, grow, conv_w, conv_b, norm_g, B, S):
    L = min(MLSTM_CHUNK, S)
    nc = S // L
    dh = A_HEAD_DIM
    H = A_HEADS
    W = A_WIDTH
    row = lambda b, c: b * nc + c
    full = lambda shape: pl.BlockSpec(shape, lambda b, c: (0, 0))
    return pl.pallas_call(
        _mlstm_kernel,
        grid=(B, nc),
        in_specs=[
            pl.BlockSpec((L, W), lambda b, c: (row(b, c), COL_QA // W)),
            pl.BlockSpec((L, W), lambda b, c: (row(b, c), COL_KA // W)),
            pl.BlockSpec((L, W), lambda b, c: (row(b, c), COL_VA // W)),
            pl.BlockSpec((L, W), lambda b, c: (row(b, c), COL_OA // W)),
            pl.BlockSpec((L, SUBLANES), lambda b, c: (row(b, c), 0)),
            pl.BlockSpec((SUBLANES, L), lambda b, c: (0, row(b, c))),
            full((CONV_WIDTH, 2 * W)), full((1, 2 * W)), full((1, W)),
        ],
        out_specs=pl.BlockSpec((L, W), lambda b, c: (row(b, c), 0)),
        out_shape=jax.ShapeDtypeStruct((B * S, W), BF16),
        scratch_shapes=[
            pltpu.VMEM((H, dh, dh), F32),
            pltpu.VMEM((H, 1, dh), F32),
            pltpu.VMEM((H, 1, 1), F32),
            pltpu.VMEM((SUBLANES, W), F32),
            pltpu.VMEM((SUBLANES, W), F32),
        ],
        compiler_params=_cparams(("parallel", "arbitrary")),
        name="mlstm",
    )(z_main, z_main, z_main, z_main, gcol, grow, conv_w, conv_b, norm_g)


def _lane_tiles(t):
    return [t[:, i * LANES:(i + 1) * LANES] for i in range(t.shape[1] // LANES)]


def _attn_kernel(slope_ref, q_ref, k_ref, kb_ref, v_ref, lq1_ref, lk1_ref, lq2_ref, lk2_ref, g_ref, o_ref,
                 s_ref, mx_ref, acc_ref, corr_ref, *, lambda_init):
    h = pl.program_id(1)
    qi = pl.program_id(2)
    tq = q_ref.shape[0]
    tk = tq
    slope = slope_ref[h]
    nt = (((1,), (1,)), ((), ()))

    lane = lax.broadcasted_iota(I32, (tq, B_V_DIM), 1)
    q = q_ref[...] * (B_HEAD_DIM ** -0.5)
    zero = jnp.zeros_like(q)
    bias_cols = jnp.where(lane < 2, 1.0, 0.0).astype(BF16)
    qa = (jnp.concatenate([jnp.where(lane < B_HEAD_DIM, q, zero), bias_cols], axis=1),
          jnp.concatenate([jnp.where(lane >= B_HEAD_DIM, q, zero), bias_cols], axis=1))

    def scores(j):
        start = pl.multiple_of(j * tk, tk)
        ka = jnp.concatenate([k_ref[pl.ds(start, tk), :], kb_ref[pl.ds(start, tk), :]], axis=1)
        return [lax.dot_general(qa[n], ka, nt, preferred_element_type=F32) for n in range(2)]

    def keep(j, n, t, rows=slice(None)):
        t = t * LOG2_E
        s_ref[n, j, rows, :t.shape[1]] = t
        mx_ref[n, rows, :] = functools.reduce(jnp.maximum, _lane_tiles(t), mx_ref[n, rows, :])

    mx_ref[...] = jnp.full(mx_ref.shape, NEG_INF, F32)

    def pass_a(j):
        t = scores(j)
        for n in range(2):
            keep(j, n, t[n])

    def pass_a_pair(jj, carry):
        pass_a(2 * jj)
        pass_a(2 * jj + 1)
        return carry
    lax.fori_loop(0, qi // 2, pass_a_pair, 0)

    @pl.when(qi % 2 == 1)
    def _():
        pass_a(qi - 1)

    rg = tq // ATTN_DIAG_GROUPS

    @pl.when(qi == 0)
    def _():
        r = lax.broadcasted_iota(I32, (rg, rg), 0)
        c = lax.broadcasted_iota(I32, (rg, rg), 1)
        ahead = jnp.where(c <= r, 0.0, (r - c).astype(F32) * (2.0 * slope))
        corr_ref[...] = jnp.where((c // ATTN_CHUNK) <= (r // ATTN_CHUNK), ahead, NEG_INF)

    dstart = pl.multiple_of(qi * tk, tk)
    ka = jnp.concatenate([k_ref[pl.ds(dstart, tk), :], kb_ref[pl.ds(dstart, tk), :]], axis=1)
    for n in range(2):
        for g in range(ATTN_DIAG_GROUPS):
            kbeg, kend = g * rg, (g + 1) * rg
            band = lax.dot_general(qa[n][kbeg:kend], ka[:kend], nt, preferred_element_type=F32)
            square = band[:, kbeg:] + corr_ref[...]
            band = square if g == 0 else jnp.concatenate([band[:, :kbeg], square], axis=1)
            keep(qi, n, band, rows=slice(kbeg, kend))

    for n in range(2):
        m = jnp.max(mx_ref[n], axis=-1, keepdims=True)
        mx_ref[n] = jnp.broadcast_to(m, (tq, LANES))
    acc_ref[...] = jnp.zeros_like(acc_ref)

    def weigh(n, rows, s_tile, v_tile):
        mb = mx_ref[n, rows, :]
        p = jnp.concatenate([jnp.exp2(t - mb) for t in _lane_tiles(s_tile)], axis=1).astype(BF16)
        v_ones = jnp.concatenate([v_tile, jnp.ones((v_tile.shape[0], LANES), BF16)], axis=1)
        acc_ref[n, rows, :] += jnp.dot(p, v_ones, preferred_element_type=F32)

    def pass_b(j, carry):
        start = pl.multiple_of(j * tk, tk)
        vj = v_ref[pl.ds(start, tk), :]
        for n in range(2):
            weigh(n, slice(None), s_ref[n, j], vj)
        return carry
    lax.fori_loop(0, qi, pass_b, 0)

    for n in range(2):
        for g in range(ATTN_DIAG_GROUPS):
            kend = (g + 1) * rg
            weigh(n, slice(g * rg, kend), s_ref[n, qi, g * rg:kend, :kend], v_ref[pl.ds(dstart, kend), :])

    lam = (jnp.exp(jnp.sum(lq1_ref[...] * lk1_ref[...], axis=-1, keepdims=True))
           - jnp.exp(jnp.sum(lq2_ref[...] * lk2_ref[...], axis=-1, keepdims=True)) + lambda_init)
    dv = B_V_DIM
    a0, a1 = acc_ref[0], acc_ref[1]
    o = a0[:, :dv] / a0[:, dv:] - lam * (a1[:, :dv] / a1[:, dv:])
    ms = jnp.mean(o * o, axis=-1, keepdims=True)
    o_ref[...] = (o * lax.rsqrt(ms + LN_EPS) * g_ref[...] * (1.0 - lambda_init)).astype(o_ref.dtype)


def _diff_attention(z_main, lq1, lk1, lq2, lk2, norm_g, B, S, lambda_init):
    tq = min(ATTN_TQ, S)
    nq = S // tq
    H = B_HEADS
    dv = B_V_DIM
    slopes = 2.0 ** (-8.0 * jnp.arange(1, H + 1, dtype=F32) / H)
    assert S <= 256 * 256
    pos = lax.broadcasted_iota(I32, (H, S, dv), 1)
    col = lax.broadcasted_iota(I32, (H, S, dv), 2)
    within = pos % 256
    kbias = jnp.where(col == 0, within, jnp.where(col == 1, pos - within, 0)).astype(F32)
    kbias = (kbias * slopes[:, None, None]).astype(BF16)
    small = pl.BlockSpec((1, B_HEAD_DIM), lambda b, h, i: (0, 0))
    return pl.pallas_call(
        functools.partial(_attn_kernel, lambda_init=lambda_init),
        grid=(B, H, nq),
        in_specs=[
            pl.BlockSpec(memory_space=pltpu.SMEM),
            pl.BlockSpec((tq, dv), lambda b, h, i: (b * nq + i, COL_QB // dv + h)),
            pl.BlockSpec((S, dv), lambda b, h, i: (b, COL_KB // dv + h)),
            pl.BlockSpec((None, S, dv), lambda b, h, i: (h, 0, 0)),
            pl.BlockSpec((S, dv), lambda b, h, i: (b, COL_VB // dv + h)),
            small, small, small, small,
            pl.BlockSpec((1, dv), lambda b, h, i: (0, 0)),
        ],
        out_specs=pl.BlockSpec((tq, dv), lambda b, h, i: (b * nq + i, h)),
        out_shape=jax.ShapeDtypeStruct((B * S, B_WIDTH), BF16),
        scratch_shapes=[
            pltpu.VMEM((2, nq, tq, tq), F32),
            pltpu.VMEM((2, tq, LANES), F32),
            pltpu.VMEM((2, tq, dv + LANES), F32),
            pltpu.VMEM((tq // ATTN_DIAG_GROUPS, tq // ATTN_DIAG_GROUPS), F32),
        ],
        compiler_params=_cparams(("parallel", "parallel", "arbitrary")),
        name="diff_attention",
    )(slopes, z_main, z_main, kbias, z_main, lq1, lk1, lq2, lk2, norm_g)


def _mix_kernel(ha_ref, ob_ref, ga_ref, gb_ref, x_ref, wa_ref, wb_ref, wo_ref, g1_ref, b1_ref, wr_ref, br_ref,
                x1_ref, x1p_ref, ri_ref, rw_ref, cnt_ref, carry_ref):
    i = pl.program_id(0)
    tm = x_ref.shape[0]

    @pl.when(i == 0)
    def _():
        carry_ref[...] = jnp.zeros_like(carry_ref)

    ya = jnp.dot(ha_ref[...], wa_ref[...], preferred_element_type=F32)
    yb = jnp.dot(ob_ref[...], wb_ref[...], preferred_element_type=F32)
    merged = _sigmoid(ga_ref[...].astype(F32)) * ya + _sigmoid(gb_ref[...].astype(F32)) * yb
    mix = jnp.dot(merged.astype(BF16), wo_ref[...], preferred_element_type=F32)
    pre = DEEPNORM_ALPHA * x_ref[...] + mix
    mu = jnp.mean(pre, axis=-1, keepdims=True)
    pc = pre - mu
    var = jnp.mean(pc * pc, axis=-1, keepdims=True)
    x1 = pc * lax.rsqrt(var + LN_EPS) * g1_ref[...] + b1_ref[...]
    x1_ref[...] = x1
    _rows_to_tiles(x1p_ref, _pack_halves(x1))

    logits = _dot_3pass(x1, wr_ref[...]) + br_ref[...]
    lt = logits.T
    row8 = lax.broadcasted_iota(I32, (SUBLANES, tm), 0).astype(F32)
    gl = jnp.where(row8 < N_GROUPS, lt[0:SUBLANES, :], NEG_INF)
    gmax = jnp.max(gl, axis=0, keepdims=True)
    gsel = jnp.min(jnp.where(gl == gmax, row8, float(SUBLANES)), axis=0, keepdims=True)
    gprob = 1.0 / jnp.sum(jnp.exp(gl - gmax), axis=0, keepdims=True)
    ing = lt[SUBLANES:2 * SUBLANES, :]
    for g in range(1, N_GROUPS):
        ing = jnp.where(gsel == g, lt[(g + 1) * SUBLANES:(g + 2) * SUBLANES, :], ing)
    v0 = jnp.max(ing, axis=0, keepdims=True)
    i0 = jnp.min(jnp.where(ing == v0, row8, float(SUBLANES)), axis=0, keepdims=True)
    ing2 = jnp.where(row8 == i0, -jnp.inf, ing)
    v1 = jnp.max(ing2, axis=0, keepdims=True)
    i1 = jnp.min(jnp.where(ing2 == v1, row8, float(SUBLANES)), axis=0, keepdims=True)
    ex = jnp.exp(v1 - v0)
    inv = 1.0 / (1.0 + ex)
    w0 = gprob * inv
    w1 = gprob * (ex * inv)
    e0 = (gsel * EXPERTS_PER_GROUP + i0).astype(I32)
    e1 = (gsel * EXPERTS_PER_GROUP + i1).astype(I32)

    rowe = lax.broadcasted_iota(I32, (N_EXPERTS, tm), 0)
    is0 = rowe == e0
    is1 = rowe == e1
    oh = jnp.where(is0, 1.0, 0.0) + jnp.where(is1, 1.0, 0.0)
    tr = lax.broadcasted_iota(I32, (tm, tm), 0)
    tc = lax.broadcasted_iota(I32, (tm, tm), 1)
    su = jnp.where(tr < tc, 1.0, 0.0).astype(BF16)
    tot = jnp.dot(oh.astype(BF16), su, preferred_element_type=F32) + carry_ref[...]
    rank0 = jnp.sum(jnp.where(is0, tot, 0.0), axis=0, keepdims=True)
    rank1 = jnp.sum(jnp.where(is1, tot, 0.0), axis=0, keepdims=True)
    carry = carry_ref[...] + jnp.sum(oh, axis=1, keepdims=True)
    carry_ref[...] = carry
    cnt_ref[...] = jnp.broadcast_to(carry, cnt_ref.shape).astype(I32)

    zi = jnp.zeros((1, tm), I32)
    ri_ref[...] = jnp.concatenate([e0, e1, rank0.astype(I32), rank1.astype(I32), zi, zi, zi, zi], axis=0)
    zf = jnp.zeros((1, tm), F32)
    rw_ref[...] = jnp.concatenate([w0, w1, zf, zf, zf, zf, zf, zf], axis=0)


def _mix(h_a, o_b, z_main, x2d, w_a, w_b, w_out, ln_g, ln_b, w_r, b_r):
    T, D = x2d.shape
    tm = min(MIX_TM, T)
    const = lambda shape: pl.BlockSpec(shape, lambda i: (0, 0), pipeline_mode=pl.Buffered(1))
    return pl.pallas_call(
        _mix_kernel,
        grid=(T // tm,),
        in_specs=[
            pl.BlockSpec((tm, A_WIDTH), lambda i: (i, 0)),
            pl.BlockSpec((tm, B_WIDTH), lambda i: (i, 0)),
            pl.BlockSpec((tm, D), lambda i: (i, COL_GA // D)),
            pl.BlockSpec((tm, D), lambda i: (i, COL_GB // D)),
            pl.BlockSpec((tm, D), lambda i: (i, 0)),
            const((A_WIDTH, D)), const((B_WIDTH, D)), const((D, D)),
            const((1, D)), const((1, D)), const((D, 2 * LANES)), const((1, LANES)),
        ],
        out_specs=[
            pl.BlockSpec((tm, D), lambda i: (i, 0)),
            pl.BlockSpec((tm * SUBLANES, LANES), lambda i: (i, 0)),
            pl.BlockSpec((SUBLANES, tm), lambda i: (0, i)),
            pl.BlockSpec((SUBLANES, tm), lambda i: (0, i)),
            pl.BlockSpec((N_EXPERTS, LANES), lambda i: (0, 0)),
        ],
        out_shape=[
            jax.ShapeDtypeStruct((T, D), F32),
            jax.ShapeDtypeStruct((T * SUBLANES, LANES), U32),
            jax.ShapeDtypeStruct((SUBLANES, T), I32),
            jax.ShapeDtypeStruct((SUBLANES, T), F32),
            jax.ShapeDtypeStruct((N_EXPERTS, LANES), I32),
        ],
        scratch_shapes=[pltpu.VMEM((N_EXPERTS, 1), F32)],
        compiler_params=_cparams(("arbitrary",)),
        name="mix_ln1_router",
    )(h_a, o_b, z_main, z_main, x2d, w_a, w_b, w_out, ln_g, ln_b, w_r, b_r)


def _row_copy(src_ref, src_row, dst_ref, dst_row, sem):
    return pltpu.make_async_copy(src_ref.at[pl.ds(src_row * SUBLANES, SUBLANES)],
                                 dst_ref.at[pl.ds(dst_row * SUBLANES, SUBLANES)], sem)


def _rows_wait(src_ref, dst_ref, rows, sem):
    n = rows * SUBLANES
    pltpu.make_async_copy(src_ref.at[pl.ds(0, n)], dst_ref.at[pl.ds(0, n)], sem).wait()


DISPATCH_SLOTS = 3


def _dispatch_kernel(cnt_ref, ri_ref, x1_ref, xpad_ref, pstart_ref, bexp_ref, nused_ref,
                     pst_ref, zero_ref, xbuf_ref, sem, lsem, zsem, *, tb, n_blocks, tm):
    i = pl.program_id(0)
    last = pl.num_programs(0) - 1

    @pl.when(i == 0)
    def _():
        def fill(b, carry):
            bexp_ref[b] = 0
            return carry
        lax.fori_loop(0, n_blocks, fill, 0)

        def per_expert(e, blk):
            pst_ref[e] = blk * tb
            pstart_ref[e] = blk * tb
            nb = (cnt_ref[e] + (tb - 1)) // tb

            def mark(b, carry):
                bexp_ref[blk + b] = e
                return carry
            lax.fori_loop(0, nb, mark, 0)
            return blk + nb
        nused = lax.fori_loop(0, N_EXPERTS, per_expert, 0)
        nused_ref[0] = nused
        pst_ref[N_EXPERTS] = nused

    rows = tm * SUBLANES
    slot = i % DISPATCH_SLOTS

    def tile_load(tile, s):
        src = x1_ref.at[pl.ds(pl.multiple_of(tile * rows, rows), rows)]
        return pltpu.make_async_copy(src, xbuf_ref.at[s], lsem.at[s])

    def wait_rows(s):
        for k in range(TOP_K):
            _rows_wait(xbuf_ref.at[s], xpad_ref, tm, sem.at[s])

    @pl.when(i == 0)
    def _():
        tile_load(0, 0).start()

        @pl.when(last >= 1)
        def _():
            tile_load(1, 1).start()

    tile_load(i, slot).wait()
    for t in range(tm):
        for k in range(TOP_K):
            pos = pst_ref[ri_ref[k, t]] + ri_ref[TOP_K + k, t]
            _row_copy(xbuf_ref.at[slot], t, xpad_ref, pos, sem.at[slot]).start(priority=k)

    prev = (i + DISPATCH_SLOTS - 1) % DISPATCH_SLOTS

    @pl.when(i >= 1)
    def _():
        wait_rows(prev)

    @pl.when(i + 2 <= last)
    def _():
        tile_load(i + 2, prev).start()

    @pl.when(i == last)
    def _():
        wait_rows(slot)
        zero_ref[...] = jnp.zeros_like(zero_ref)

        def per_expert(e, carry):
            cnt = cnt_ref[e]
            first = pst_ref[e] + cnt
            npad = ((cnt + (tb - 1)) // tb) * tb - cnt

            def start(p, c2):
                _row_copy(zero_ref, 0, xpad_ref, first + p, zsem).start()
                return c2
            lax.fori_loop(0, npad, start, 0)

            def wait(p, c2):
                _row_copy(zero_ref, 0, xpad_ref, 0, zsem).wait()
                return c2
            lax.fori_loop(0, npad, wait, 0)
            return carry
        lax.fori_loop(0, N_EXPERTS, per_expert, 0)

        def block_copy(b):
            n = tb * SUBLANES
            return pltpu.make_async_copy(zero_ref, xpad_ref.at[pl.ds(pl.multiple_of(b * n, n), n)], zsem)

        def start_block(b, carry):
            block_copy(b).start()
            return carry
        lax.fori_loop(pst_ref[N_EXPERTS], n_blocks, start_block, 0)

        def wait_block(b, carry):
            block_copy(b).wait()
            return carry
        lax.fori_loop(pst_ref[N_EXPERTS], n_blocks, wait_block, 0)


def _dispatch(counts, route_i, x1, tb, n_blocks):
    T = x1.shape[0] // SUBLANES
    tm = min(DISPATCH_TM, T)
    smem_full = pl.BlockSpec(memory_space=pltpu.SMEM)
    return pl.pallas_call(
        functools.partial(_dispatch_kernel, tb=tb, n_blocks=n_blocks, tm=tm),
        grid=(T // tm,),
        in_specs=[
            smem_full,
            pl.BlockSpec((SUBLANES, tm), lambda i: (0, i), memory_space=pltpu.SMEM),
            pl.BlockSpec(memory_space=pl.ANY),
        ],
        out_specs=[
            pl.BlockSpec(memory_space=pl.ANY),
            smem_full, smem_full, smem_full,
        ],
        out_shape=[
            jax.ShapeDtypeStruct((n_blocks * tb * SUBLANES, LANES), x1.dtype),
            jax.ShapeDtypeStruct((N_EXPERTS,), I32),
            jax.ShapeDtypeStruct((n_blocks,), I32),
            jax.ShapeDtypeStruct((1,), I32),
        ],
        scratch_shapes=[
            pltpu.SMEM((N_EXPERTS + 1,), I32),
            pltpu.VMEM((tb * SUBLANES, LANES), x1.dtype),
            pltpu.VMEM((DISPATCH_SLOTS, tm * SUBLANES, LANES), x1.dtype),
            pltpu.SemaphoreType.DMA((DISPATCH_SLOTS,)),
            pltpu.SemaphoreType.DMA((DISPATCH_SLOTS,)),
            pltpu.SemaphoreType.DMA(()),
        ],
        compiler_params=_cparams(("arbitrary",)),
        name="moe_dispatch",
    )(counts, route_i, x1)


def _expert_kernel(bexp_ref, nused_ref, x_ref, wg_hbm, wu_hbm, wd_hbm, y_ref,
                   wgf_ref, wuf_ref, wdf_ref, wgb_ref, wub_ref, wdb_ref, sem, run_ref):
    b = pl.program_id(0)
    nused = nused_ref[0]
    used = b < nused
    e = bexp_ref[b]
    new_expert = jnp.logical_or(b == 0, e != bexp_ref[jnp.maximum(b - 1, 0)])
    streams = ((wg_hbm, wgf_ref), (wu_hbm, wuf_ref), (wd_hbm, wdf_ref))

    def weight_copies(expert, slot):
        return [pltpu.make_async_copy(w_hbm.at[expert], wf_ref.at[slot], sem.at[slot, j])
                for j, (w_hbm, wf_ref) in enumerate(streams)]

    @pl.when(jnp.logical_and(used, b == 0))
    def _():
        run_ref[0] = 0
        for cp in weight_copies(e, 0):
            cp.start()

    @pl.when(jnp.logical_and(used, new_expert))
    def _():
        slot = run_ref[0] % 2
        nxt = lax.while_loop(lambda j: jnp.logical_and(j < nused, bexp_ref[jnp.minimum(j, nused - 1)] == e),
                             lambda j: j + 1, b + 1)

        @pl.when(nxt < nused)
        def _():
            for cp in weight_copies(bexp_ref[jnp.minimum(nxt, nused - 1)], 1 - slot):
                cp.start()

        for cp in weight_copies(e, slot):
            cp.wait()
        wgb_ref[...] = wgf_ref[slot].astype(BF16)
        wub_ref[...] = wuf_ref[slot].astype(BF16)
        wdb_ref[...] = wdf_ref[slot].astype(BF16)
        run_ref[0] = run_ref[0] + 1

    @pl.when(used)
    def _():
        half = SUBLANES * LANES
        x_lo, x_hi = (t.astype(BF16) for t in _unpack_halves(_tiles_to_rows(x_ref)))
        dot = functools.partial(jnp.dot, preferred_element_type=F32)
        gate = dot(x_lo, wgb_ref[:half, :]) + dot(x_hi, wgb_ref[half:, :])
        up = dot(x_lo, wub_ref[:half, :]) + dot(x_hi, wub_ref[half:, :])
        hid = (gate * _sigmoid(gate) * up).astype(BF16)
        _rows_to_tiles(y_ref, _pack_halves(dot(hid, wdb_ref[...])))

    @pl.when(jnp.logical_not(used))
    def _():
        y_ref[...] = jnp.zeros_like(y_ref)


def _experts(bexp, nused, x_pad, w_gate, w_up, w_down, tb):
    P = x_pad.shape[0] // SUBLANES
    D = 2 * TILE_WORDS
    n_blocks = P // tb
    rows = (tb * SUBLANES, LANES)
    F = w_gate.shape[-1]
    blk = lambda b, be, nu: jnp.maximum(jnp.minimum(b, nu[0] - 1), 0)
    hbm = pl.BlockSpec(memory_space=pl.ANY)
    return pl.pallas_call(
        _expert_kernel,
        grid_spec=pltpu.PrefetchScalarGridSpec(
            num_scalar_prefetch=2,
            grid=(n_blocks,),
            in_specs=[pl.BlockSpec(rows, lambda b, be, nu: (blk(b, be, nu), 0)), hbm, hbm, hbm],
            out_specs=pl.BlockSpec(rows, lambda b, be, nu: (b, 0)),
            scratch_shapes=[
                pltpu.VMEM((2, D, F), F32), pltpu.VMEM((2, D, F), F32), pltpu.VMEM((2, F, D), F32),
                pltpu.VMEM((D, F), BF16), pltpu.VMEM((D, F), BF16), pltpu.VMEM((F, D), BF16),
                pltpu.SemaphoreType.DMA((2, 3)),
                pltpu.SMEM((1,), I32),
            ],
        ),
        out_shape=jax.ShapeDtypeStruct(x_pad.shape, U32),
        compiler_params=_cparams(("arbitrary",)),
        name="moe_experts",
    )(bexp, nused, x_pad, w_gate, w_up, w_down)


COMBINE_SLOTS = 3


def _combine_kernel(pstart_ref, ri_ref, r1_ref, r2_ref, x1_ref, rw_ref, g_ref, b_ref, ypad_ref, o_ref, ybuf_ref, sem):
    i = pl.program_id(0)
    n = pl.num_programs(0)
    tm = x1_ref.shape[0]
    slot = i % COMBINE_SLOTS

    def gather_loop(route_ref, s):
        def issue(t, carry):
            for k in range(TOP_K):
                pos = pstart_ref[route_ref[k, t]] + route_ref[TOP_K + k, t]
                _row_copy(ypad_ref, pos, ybuf_ref.at[s, k], t, sem.at[s]).start(priority=k)
            return carry
        lax.fori_loop(0, tm, issue, 0, unroll=8)

    def wait_slot(s):
        for k in range(TOP_K):
            _rows_wait(ypad_ref, ybuf_ref.at[s, k], tm, sem.at[s])

    @pl.when(i == 0)
    def _():
        gather_loop(ri_ref, 0)
        gather_loop(r1_ref, 1)

    wait_slot(slot)

    rw = rw_ref[...]
    y0 = _unpack_halves(_tiles_to_rows(ybuf_ref.at[slot, 0]))
    y1 = _unpack_halves(_tiles_to_rows(ybuf_ref.at[slot, 1]))
    ffn = jnp.concatenate([y0[h] * rw[:, 0:1] + y1[h] * rw[:, 1:2] for h in range(2)], axis=1)
    pre = DEEPNORM_ALPHA * x1_ref[...] + ffn
    mu = jnp.mean(pre, axis=-1, keepdims=True)
    pc = pre - mu
    var = jnp.mean(pc * pc, axis=-1, keepdims=True)
    o_ref[...] = pc * lax.rsqrt(var + LN_EPS) * g_ref[...] + b_ref[...]

    nxt = (i + 2) % COMBINE_SLOTS
    for t in range(tm):
        for k in range(TOP_K):
            pos = pstart_ref[r2_ref[k, t]] + r2_ref[TOP_K + k, t]
            _row_copy(ypad_ref, pos, ybuf_ref.at[nxt, k], t, sem.at[nxt]).start(priority=k)

    @pl.when(i == n - 1)
    def _():
        wait_slot((i + 1) % COMBINE_SLOTS)
        wait_slot(nxt)


def _combine(pstart, route_i, x1, rw_col, ln_g, ln_b, y_pad):
    T, D = x1.shape
    tm = min(COMBINE_TM, T)
    n = T // tm
    return pl.pallas_call(
        _combine_kernel,
        grid=(n,),
        in_specs=[
            pl.BlockSpec(memory_space=pltpu.SMEM),
            pl.BlockSpec((SUBLANES, tm), lambda i: (0, i), memory_space=pltpu.SMEM),
            pl.BlockSpec((SUBLANES, tm), lambda i: (0, jnp.minimum(i + 1, n - 1)), memory_space=pltpu.SMEM),
            pl.BlockSpec((SUBLANES, tm), lambda i: (0, jnp.minimum(i + 2, n - 1)), memory_space=pltpu.SMEM),
            pl.BlockSpec((tm, D), lambda i: (i, 0)),
            pl.BlockSpec((tm, SUBLANES), lambda i: (i, 0)),
            pl.BlockSpec((1, D), lambda i: (0, 0)),
            pl.BlockSpec((1, D), lambda i: (0, 0)),
            pl.BlockSpec(memory_space=pl.ANY),
        ],
        out_specs=pl.BlockSpec((tm, D), lambda i: (i, 0)),
        out_shape=jax.ShapeDtypeStruct((T, D), F32),
        scratch_shapes=[
            pltpu.VMEM((COMBINE_SLOTS, TOP_K, tm * SUBLANES, LANES), U32),
            pltpu.SemaphoreType.DMA((COMBINE_SLOTS,)),
        ],
        compiler_params=_cparams(("arbitrary",)),
        name="moe_combine_ln2",
    )(pstart, route_i, route_i, route_i, x1, rw_col, ln_g, ln_b, y_pad)


N_GATE_COLS = 2 * A_HEADS
SRC_MLSTM = 0
SRC_GATES = 4 * A_WIDTH
SRC_DIFF = SRC_GATES + N_GATE_COLS
SRC_MERGE = SRC_DIFF + 3 * B_WIDTH
PREP_TN = 512


def _wprep_kernel(a_ref, o_ref):
    o_ref[...] = a_ref[...].T.astype(BF16)


def _wgate_kernel(a_ref, o_ref):
    a = a_ref[...]
    rows = jnp.concatenate([a, jnp.zeros((LANES - a.shape[0], a.shape[1]), F32)], axis=0)
    o_ref[...] = rows.T


def _rearrange_in_proj(w_in, b_in):
    K, n_in = w_in.shape
    tn = PREP_TN
    merge_blocks = (2 * D_MODEL) // tn
    mlstm_blocks = (4 * A_WIDTH) // tn
    assert SRC_MERGE % SUBLANES == 0 and SRC_DIFF % SUBLANES == 0

    def src_row(jb):
        s = SUBLANES
        merge = SRC_MERGE // s + (tn // s) * jb
        mlstm = SRC_MLSTM // s + (tn // s) * (jb - merge_blocks)
        diff = SRC_DIFF // s + (tn // s) * (jb - merge_blocks - mlstm_blocks)
        return s * jnp.where(jb < merge_blocks, merge, jnp.where(jb < merge_blocks + mlstm_blocks, mlstm, diff))

    w_t = jnp.swapaxes(w_in, 0, 1)
    w_main = pl.pallas_call(
        _wprep_kernel,
        grid=(N_MAIN // tn,),
        in_specs=[pl.BlockSpec((pl.Element(tn), pl.Element(K)), lambda jb: (src_row(jb), 0))],
        out_specs=pl.BlockSpec((K, tn), lambda jb: (0, jb)),
        out_shape=jax.ShapeDtypeStruct((K, N_MAIN), BF16),
        compiler_params=_cparams(("parallel",)),
        name="w_in_prep",
    )(w_t)
    b_main = jnp.concatenate([b_in[SRC_MERGE:], b_in[SRC_MLSTM:SRC_GATES], b_in[SRC_DIFF:SRC_MERGE]])[None, :]
    w_gate = pl.pallas_call(
        _wgate_kernel,
        grid=(1,),
        in_specs=[pl.BlockSpec((pl.Element(N_GATE_COLS), pl.Element(K)), lambda i: (SRC_GATES, 0))],
        out_specs=pl.BlockSpec((K, LANES), lambda i: (0, 0)),
        out_shape=jax.ShapeDtypeStruct((K, LANES), F32),
        name="w_gate_prep",
    )(w_t)
    b_gate = jnp.pad(b_in[SRC_GATES:SRC_DIFF], (0, LANES - N_GATE_COLS))[None, :]
    return w_main, b_main, _hi_lo_columns(w_gate), b_gate


def _layer(x, w_in, b_in, conv_w, conv_b, norm_a_g, lq1, lk1, lq2, lk2, norm_b_g, w_a, w_b, w_out,
           ln1_g, ln1_b, w_grp, b_grp, w_exp, b_exp, w_gate, w_up, w_down, ln2_g, ln2_b, lambda_init):
    B, S, D = x.shape
    T = B * S
    x2d = x.reshape(T, D)

    w_main, b_main, w_g, b_g = _rearrange_in_proj(w_in, b_in)
    z_main, z_gate = _in_projection(x2d, w_main, b_main, w_g, b_g)

    L = min(MLSTM_CHUNK, S)
    gp = _gate_prep(z_gate[:, :2 * A_HEADS].T, L)
    h_a = _mlstm(z_main, gp.T, gp, conv_w, conv_b[None, :], norm_a_g[None, :], B, S)
    o_b = _diff_attention(z_main, lq1[None, :], lk1[None, :], lq2[None, :], lk2[None, :],
                          norm_b_g[None, :], B, S, lambda_init)

    w_r = jnp.zeros((D, LANES), F32).at[:, :N_GROUPS].set(w_grp).at[:, SUBLANES:SUBLANES + N_EXPERTS].set(w_exp)
    b_r = jnp.zeros((LANES,), F32).at[:N_GROUPS].set(b_grp).at[SUBLANES:SUBLANES + N_EXPERTS].set(b_exp)[None, :]
    x1, x1_packed, route_i, route_w, counts = _mix(h_a, o_b, z_main, x2d, w_a.astype(BF16), w_b.astype(BF16),
                                                   w_out.astype(BF16), ln1_g[None, :], ln1_b[None, :],
                                                   _hi_lo_columns(w_r), b_r)

    tb = MOE_TB
    n_blocks = (T * TOP_K) // tb + N_EXPERTS
    x_pad, pstart, bexp, nused = _dispatch(counts[:, 0], route_i, x1_packed, tb, n_blocks)
    y_pad = _experts(bexp, nused, x_pad, w_gate, w_up, w_down, tb)
    out = _combine(pstart, route_i, x1, route_w.T, ln2_g[None, :], ln2_b[None, :], y_pad)
    return out.reshape(B, S, D)


def kernel(x, w_in, b_in, conv_w, conv_b, mlstm_norm_g, lambda_q1, lambda_k1, lambda_q2, lambda_k2,
           diff_norm_g, w_a, w_b, w_out, ln1_g, ln1_b, w_grp, b_grp, w_exp, b_exp,
           w_gate, w_up, w_down, ln2_g, ln2_b):
    for l in range(DEPTH):
        lambda_init = 0.8 - 0.6 * math.exp(-0.3 * l)
        x = _layer(x, w_in[l], b_in[l], conv_w[l], conv_b[l], mlstm_norm_g[l], lambda_q1[l], lambda_k1[l],
                   lambda_q2[l], lambda_k2[l], diff_norm_g[l], w_a[l], w_b[l], w_out[l], ln1_g[l], ln1_b[l],
                   w_grp[l], b_grp[l], w_exp[l], b_exp[l], w_gate[l], w_up[l], w_down[l], ln2_g[l], ln2_b[l],
                   lambda_init)
    return x
```

```python
import functools
import math

import jax
import jax.numpy as jnp
from jax import lax
from jax.experimental import pallas as pl
from jax.experimental.pallas import tpu as pltpu

F32 = jnp.float32
BF16 = jnp.bfloat16
I32 = jnp.int32
U32 = jnp.uint32

D_MODEL = 2048
A_HEADS = 4
A_HEAD_DIM = 256
A_WIDTH = A_HEADS * A_HEAD_DIM
CONV_WIDTH = 4
B_HEADS = 8
B_HEAD_DIM = 64
B_V_DIM = 2 * B_HEAD_DIM
B_WIDTH = B_HEADS * B_V_DIM
ATTN_CHUNK = 64
N_GROUPS = 4
EXPERTS_PER_GROUP = 8
N_EXPERTS = N_GROUPS * EXPERTS_PER_GROUP
TOP_K = 2
D_EXPERT = 512
DEPTH = 1
DEEPNORM_ALPHA = (2 * DEPTH) ** 0.25
LN_EPS = 1e-5
NEG_INF = float("-inf")
LOG2_E = 1.4426950408889634

LANES = 128
SUBLANES = 8
VMEM_LIMIT_BYTES = 56 * 1024 * 1024

COL_GA = 0
COL_GB = COL_GA + D_MODEL
COL_QA = COL_GB + D_MODEL
COL_KA = COL_QA + A_WIDTH
COL_VA = COL_KA + A_WIDTH
COL_OA = COL_VA + A_WIDTH
COL_QB = COL_OA + A_WIDTH
COL_KB = COL_QB + B_WIDTH
COL_VB = COL_KB + B_WIDTH
N_MAIN = COL_VB + B_WIDTH

PROJ_TM, PROJ_TN = 1024, 1024
MLSTM_CHUNK = 512
ATTN_TQ = 1024
ATTN_DIAG_GROUPS = 4
MIX_TM = 512
MOE_TB = 256
DISPATCH_TM = 256
COMBINE_TM = 256


def _cparams(sem, vmem=VMEM_LIMIT_BYTES):
    return pltpu.CompilerParams(dimension_semantics=sem, vmem_limit_bytes=vmem)


def _sigmoid(x):
    return 0.5 * jnp.tanh(0.5 * x) + 0.5


def _pack_halves(x):
    n = x.shape[1] // 2
    lo = lax.bitcast_convert_type(x[:, :n].astype(BF16).astype(F32), U32)
    hi = lax.bitcast_convert_type(x[:, n:].astype(BF16).astype(F32), U32)
    return (lo >> 16) | hi


def _unpack_halves(u):
    lo = lax.bitcast_convert_type(u << 16, F32)
    hi = lax.bitcast_convert_type(u & jnp.uint32(0xFFFF0000), F32)
    return lo, hi


TILE_WORDS = SUBLANES * LANES


def _rows_to_tiles(ref, v):
    rows = v.shape[0]
    for j in range(SUBLANES):
        ref[pl.ds(j, rows, stride=SUBLANES), :] = v[:, j * LANES:(j + 1) * LANES]


def _tiles_to_rows(ref):
    rows = ref.shape[0] // SUBLANES
    return jnp.concatenate([ref[pl.ds(j, rows, stride=SUBLANES), :] for j in range(SUBLANES)], axis=1)


def _split_bf16(a):
    hi = a.astype(BF16)
    return hi, (a - hi.astype(F32)).astype(BF16)


def _hi_lo_columns(w):
    return jnp.concatenate(_split_bf16(w), axis=1)


def _dot_3pass(a, w_hl):
    a_hi, a_lo = _split_bf16(a)
    n = w_hl.shape[1] // 2
    dot = functools.partial(jnp.dot, preferred_element_type=F32)
    r = dot(a_hi, w_hl)
    return r[:, :n] + (r[:, n:] + dot(a_lo, w_hl[:, :n]))


def _proj_kernel(x_ref, w_ref, b_ref, wg_ref, bg_ref, z_ref, zg_ref, xb_ref):
    @pl.when(pl.program_id(1) == 0)
    def _():
        x = x_ref[...]
        xb_ref[...] = x.astype(BF16)
        zg_ref[...] = _dot_3pass(x, wg_ref[...]) + bg_ref[...]

    acc = jnp.dot(xb_ref[...], w_ref[...], preferred_element_type=F32)
    z_ref[...] = (acc + b_ref[...]).astype(z_ref.dtype)


def _in_projection(x2d, w_main, b_main, w_gate, b_gate):
    T, K = x2d.shape
    N = w_main.shape[1]
    tm, tn = min(PROJ_TM, T), PROJ_TN
    return pl.pallas_call(
        _proj_kernel,
        grid=(T // tm, N // tn),
        in_specs=[
            pl.BlockSpec((tm, K), lambda i, j: (i, 0)),
            pl.BlockSpec((K, tn), lambda i, j: (0, j)),
            pl.BlockSpec((1, tn), lambda i, j: (0, j)),
            pl.BlockSpec((K, 2 * LANES), lambda i, j: (0, 0)),
            pl.BlockSpec((1, LANES), lambda i, j: (0, 0)),
        ],
        out_specs=[
            pl.BlockSpec((tm, tn), lambda i, j: (i, j)),
            pl.BlockSpec((tm, LANES), lambda i, j: (i, 0)),
        ],
        out_shape=[
            jax.ShapeDtypeStruct((T, N), BF16),
            jax.ShapeDtypeStruct((T, LANES), F32),
        ],
        scratch_shapes=[pltpu.VMEM((tm, K), BF16)],
        compiler_params=_cparams(("parallel", "arbitrary")),
        name="in_projection",
    )(x2d, w_main, b_main, w_gate, b_gate)


def _gate_prep_kernel(g_ref, o_ref, *, L):
    r = lax.broadcasted_iota(I32, (L, L), 0)
    c = lax.broadcasted_iota(I32, (L, L), 1)
    tri = (r <= c).astype(F32)
    row = lax.broadcasted_iota(I32, (g_ref.shape[0], L), 0)
    for ch in range(g_ref.shape[1] // L):
        g = g_ref[:, ch * L:(ch + 1) * L]
        lf = jnp.minimum(g, 0.0) - jnp.log(1.0 + jnp.exp(-jnp.abs(g)))
        b = jnp.dot(lf, tri, preferred_element_type=F32, precision=lax.Precision.HIGHEST)
        o_ref[:, ch * L:(ch + 1) * L] = jnp.where(row < A_HEADS, g, b)


def _gate_prep(g_rows, L):
    R, T = g_rows.shape
    per_step = math.gcd(T // L, 8)
    W = L * per_step
    return pl.pallas_call(
        functools.partial(_gate_prep_kernel, L=L),
        grid=(T // W,),
        in_specs=[pl.BlockSpec((R, W), lambda i: (0, i))],
        out_specs=pl.BlockSpec((R, W), lambda i: (0, i)),
        out_shape=jax.ShapeDtypeStruct((R, T), F32),
        compiler_params=_cparams(("parallel",)),
        name="gate_prep",
    )(g_rows)


def _mlstm_kernel(q_ref, k_ref, v_ref, o_ref, gcol_ref, grow_ref, cw_ref, cb_ref, ng_ref, out_ref,
                  c_ref, n_ref, m_ref, qcar_ref, kcar_ref):
    ci = pl.program_id(1)
    L = q_ref.shape[0]
    dh = A_HEAD_DIM

    @pl.when(ci == 0)
    def _():
        c_ref[...] = jnp.zeros_like(c_ref)
        n_ref[...] = jnp.zeros_like(n_ref)
        m_ref[...] = jnp.zeros_like(m_ref)
        qcar_ref[...] = jnp.zeros_like(qcar_ref)
        kcar_ref[...] = jnp.zeros_like(kcar_ref)

    def conv_silu(u_ref, car_ref, lo, wcol):
        u = u_ref[:, lo:lo + dh].astype(F32)
        ext = jnp.concatenate([car_ref[:, lo:lo + dh], u], axis=0)
        w = cw_ref[:, wcol:wcol + dh]
        y = cb_ref[:, wcol:wcol + dh] + w[3:4, :] * u
        for j in range(CONV_WIDTH - 1):
            off = SUBLANES - (CONV_WIDTH - 1) + j
            y = y + w[j:j + 1, :] * ext[off:off + L, :]
        car_ref[:, lo:lo + dh] = u[L - SUBLANES:, :]
        return y * _sigmoid(y)

    r = lax.broadcasted_iota(I32, (L, L), 0)
    c = lax.broadcasted_iota(I32, (L, L), 1)
    causal = r >= c
    gcol = gcol_ref[...]
    grow = grow_ref[...]

    for h in range(A_HEADS):
        lo = h * dh
        q = conv_silu(q_ref, qcar_ref, lo, lo)
        k = conv_silu(k_ref, kcar_ref, lo, A_WIDTH + lo) * (dh ** -0.5)
        v_b = v_ref[:, lo:lo + dh]
        q_b = q.astype(BF16)
        k_b = k.astype(BF16)

        i_col, b_col = gcol[:, h:h + 1], gcol[:, A_HEADS + h:A_HEADS + h + 1]
        i_row, b_row = grow[h:h + 1, :], grow[A_HEADS + h:A_HEADS + h + 1, :]
        b_last = b_row[:, L - 1:L]
        m_prev = m_ref[h]

        dmat = jnp.where(causal, b_col - b_row + i_row, NEG_INF)
        inter = b_col + m_prev
        m_t = jnp.maximum(inter, jnp.max(dmat, axis=-1, keepdims=True))
        s = lax.dot_general(q_b, k_b, (((1,), (1,)), ((), ())), preferred_element_type=F32)
        w_intra = jnp.exp(dmat - m_t) * s
        w_inter = jnp.exp(inter - m_t)
        qc = jnp.dot(q_b, c_ref[h].astype(BF16), preferred_element_type=F32)
        num = w_inter * qc + jnp.dot(w_intra.astype(BF16), v_b, preferred_element_type=F32)
        qn = jnp.sum(q * n_ref[h], axis=-1, keepdims=True)
        den = w_inter * qn + jnp.sum(w_intra, axis=-1, keepdims=True)
        hh = num / jnp.maximum(jnp.abs(den), jnp.exp(-m_t))

        g_col = b_last - b_col + i_col
        m_new = jnp.maximum(b_last + m_prev, jnp.max(g_col, axis=0, keepdims=True))
        decay = jnp.exp(b_last + m_prev - m_new)
        kw = jnp.exp(g_col - m_new) * k
        c_ref[h] = decay * c_ref[h] + jnp.dot(kw.T.astype(BF16), v_b, preferred_element_type=F32)
        n_ref[h] = decay * n_ref[h] + jnp.sum(kw, axis=0, keepdims=True)
        m_ref[h] = m_new

        y = _sigmoid(o_ref[:, lo:lo + dh].astype(F32)) * hh
        mu = jnp.mean(y, axis=-1, keepdims=True)
        yc = y - mu
        var = jnp.mean(yc * yc, axis=-1, keepdims=True)
        out_ref[:, lo:lo + dh] = (yc * lax.rsqrt(var + LN_EPS) * ng_ref[:, lo:lo + dh]).astype(out_ref.dtype)


def _mlstm(z_main, gcol, grow, conv_w, conv_b, norm_g, B, S):
    L = min(MLSTM_CHUNK, S)
    nc = S // L
    dh = A_HEAD_DIM
    H = A_HEADS
    W = A_WIDTH
    row = lambda b, c: b * nc + c
    full = lambda shape: pl.BlockSpec(shape, lambda b, c: (0, 0))
    return pl.pallas_call(
        _mlstm_kernel,
        grid=(B, nc),
        in_specs=[
            pl.BlockSpec((L, W), lambda b, c: (row(b, c), COL_QA // W)),
            pl.BlockSpec((L, W), lambda b, c: (row(b, c), COL_KA // W)),
            pl.BlockSpec((L, W), lambda b, c: (row(b, c), COL_VA // W)),
            pl.BlockSpec((L, W), lambda b, c: (row(b, c), COL_OA // W)),
            pl.BlockSpec((L, SUBLANES), lambda b, c: (row(b, c), 0)),
            pl.BlockSpec((SUBLANES, L), lambda b, c: (0, row(b, c))),
            full((CONV_WIDTH, 2 * W)), full((1, 2 * W)), full((1, W)),
        ],
        out_specs=pl.BlockSpec((L, W), lambda b, c: (row(b, c), 0)),
        out_shape=jax.ShapeDtypeStruct((B * S, W), BF16),
        scratch_shapes=[
            pltpu.VMEM((H, dh, dh), F32),
            pltpu.VMEM((H, 1, dh), F32),
            pltpu.VMEM((H, 1, 1), F32),
            pltpu.VMEM((SUBLANES, W), F32),
            pltpu.VMEM((SUBLANES, W), F32),
        ],
        compiler_params=_cparams(("parallel", "arbitrary")),
        name="mlstm",
    )(z_main, z_main, z_main, z_main, gcol, grow, conv_w, conv_b, norm_g)


def _lane_tiles(t):
    return [t[:, i * LANES:(i + 1) * LANES] for i in range(t.shape[1] // LANES)]


def _attn_kernel(slope_ref, q_ref, k_ref, kb_ref, v_ref, lq1_ref, lk1_ref, lq2_ref, lk2_ref, g_ref, o_ref,
                 s_ref, mx_ref, acc_ref, corr_ref, *, lambda_init):
    h = pl.program_id(1)
    qi = pl.program_id(2)
    tq = q_ref.shape[0]
    tk = tq
    slope = slope_ref[h]
    nt = (((1,), (1,)), ((), ()))

    lane = lax.broadcasted_iota(I32, (tq, B_V_DIM), 1)
    q = q_ref[...] * (B_HEAD_DIM ** -0.5)
    zero = jnp.zeros_like(q)
    bias_cols = jnp.where(lane < 2, 1.0, 0.0).astype(BF16)
    qa = (jnp.concatenate([jnp.where(lane < B_HEAD_DIM, q, zero), bias_cols], axis=1),
          jnp.concatenate([jnp.where(lane >= B_HEAD_DIM, q, zero), bias_cols], axis=1))

    def scores(j):
        start = pl.multiple_of(j * tk, tk)
        ka = jnp.concatenate([k_ref[pl.ds(start, tk), :], kb_ref[pl.ds(start, tk), :]], axis=1)
        return [lax.dot_general(qa[n], ka, nt, preferred_element_type=F32) for n in range(2)]

    def keep(j, n, t, rows=slice(None)):
        t = t * LOG2_E
        s_ref[n, j, rows, :t.shape[1]] = t
        mx_ref[n, rows, :] = functools.reduce(jnp.maximum, _lane_tiles(t), mx_ref[n, rows, :])

    mx_ref[...] = jnp.full(mx_ref.shape, NEG_INF, F32)

    def pass_a(j):
        t = scores(j)
        for n in range(2):
            keep(j, n, t[n])

    def pass_a_pair(jj, carry):
        pass_a(2 * jj)
        pass_a(2 * jj + 1)
        return carry
    lax.fori_loop(0, qi // 2, pass_a_pair, 0)

    @pl.when(qi % 2 == 1)
    def _():
        pass_a(qi - 1)

    rg = tq // ATTN_DIAG_GROUPS

    @pl.when(qi == 0)
    def _():
        r = lax.broadcasted_iota(I32, (rg, rg), 0)
        c = lax.broadcasted_iota(I32, (rg, rg), 1)
        ahead = jnp.where(c <= r, 0.0, (r - c).astype(F32) * (2.0 * slope))
        corr_ref[...] = jnp.where((c // ATTN_CHUNK) <= (r // ATTN_CHUNK), ahead, NEG_INF)

    dstart = pl.multiple_of(qi * tk, tk)
    ka = jnp.concatenate([k_ref[pl.ds(dstart, tk), :], kb_ref[pl.ds(dstart, tk), :]], axis=1)
    for n in range(2):
        for g in range(ATTN_DIAG_GROUPS):
            kbeg, kend = g * rg, (g + 1) * rg
            band = lax.dot_general(qa[n][kbeg:kend], ka[:kend], nt, preferred_element_type=F32)
            square = band[:, kbeg:] + corr_ref[...]
            band = square if g == 0 else jnp.concatenate([band[:, :kbeg], square], axis=1)
            keep(qi, n, band, rows=slice(kbeg, kend))

    for n in range(2):
        m = jnp.max(mx_ref[n], axis=-1, keepdims=True)
        mx_ref[n] = jnp.broadcast_to(m, (tq, LANES))
    acc_ref[...] = jnp.zeros_like(acc_ref)

    def weigh(n, rows, s_tile, v_tile):
        mb = mx_ref[n, rows, :]
        p = jnp.concatenate([jnp.exp2(t - mb) for t in _lane_tiles(s_tile)], axis=1).astype(BF16)
        v_ones = jnp.concatenate([v_tile, jnp.ones((v_tile.shape[0], LANES), BF16)], axis=1)
        acc_ref[n, rows, :] += jnp.dot(p, v_ones, preferred_element_type=F32)

    def pass_b(j, carry):
        start = pl.multiple_of(j * tk, tk)
        vj = v_ref[pl.ds(start, tk), :]
        for n in range(2):
            weigh(n, slice(None), s_ref[n, j], vj)
        return carry
    lax.fori_loop(0, qi, pass_b, 0)

    for n in range(2):
        for g in range(ATTN_DIAG_GROUPS):
            kend = (g + 1) * rg
            weigh(n, slice(g * rg, kend), s_ref[n, qi, g * rg:kend, :kend], v_ref[pl.ds(dstart, kend), :])

    lam = (jnp.exp(jnp.sum(lq1_ref[...] * lk1_ref[...], axis=-1, keepdims=True))
           - jnp.exp(jnp.sum(lq2_ref[...] * lk2_ref[...], axis=-1, keepdims=True)) + lambda_init)
    dv = B_V_DIM
    a0, a1 = acc_ref[0], acc_ref[1]
    o = a0[:, :dv] / a0[:, dv:] - lam * (a1[:, :dv] / a1[:, dv:])
    ms = jnp.mean(o * o, axis=-1, keepdims=True)
    o_ref[...] = (o * lax.rsqrt(ms + LN_EPS) * g_ref[...] * (1.0 - lambda_init)).astype(o_ref.dtype)


def _diff_attention(z_main, lq1, lk1, lq2, lk2, norm_g, B, S, lambda_init):
    tq = min(ATTN_TQ, S)
    nq = S // tq
    H = B_HEADS
    dv = B_V_DIM
    slopes = 2.0 ** (-8.0 * jnp.arange(1, H + 1, dtype=F32) / H)
    assert S <= 256 * 256
    pos = lax.broadcasted_iota(I32, (H, S, dv), 1)
    col = lax.broadcasted_iota(I32, (H, S, dv), 2)
    within = pos % 256
    kbias = jnp.where(col == 0, within, jnp.where(col == 1, pos - within, 0)).astype(F32)
    kbias = (kbias * slopes[:, None, None]).astype(BF16)
    small = pl.BlockSpec((1, B_HEAD_DIM), lambda b, h, i: (0, 0))
    return pl.pallas_call(
        functools.partial(_attn_kernel, lambda_init=lambda_init),
        grid=(B, H, nq),
        in_specs=[
            pl.BlockSpec(memory_space=pltpu.SMEM),
            pl.BlockSpec((tq, dv), lambda b, h, i: (b * nq + i, COL_QB // dv + h)),
            pl.BlockSpec((S, dv), lambda b, h, i: (b, COL_KB // dv + h)),
            pl.BlockSpec((None, S, dv), lambda b, h, i: (h, 0, 0)),
            pl.BlockSpec((S, dv), lambda b, h, i: (b, COL_VB // dv + h)),
            small, small, small, small,
            pl.BlockSpec((1, dv), lambda b, h, i: (0, 0)),
        ],
        out_specs=pl.BlockSpec((tq, dv), lambda b, h, i: (b * nq + i, h)),
        out_shape=jax.ShapeDtypeStruct((B * S, B_WIDTH), BF16),
        scratch_shapes=[
            pltpu.VMEM((2, nq, tq, tq), F32),
            pltpu.VMEM((2, tq, LANES), F32),
            pltpu.VMEM((2, tq, dv + LANES), F32),
            pltpu.VMEM((tq // ATTN_DIAG_GROUPS, tq // ATTN_DIAG_GROUPS), F32),
        ],
        compiler_params=_cparams(("parallel", "parallel", "arbitrary")),
        name="diff_attention",
    )(slopes, z_main, z_main, kbias, z_main, lq1, lk1, lq2, lk2, norm_g)


def _mix_kernel(ha_ref, ob_ref, ga_ref, gb_ref, x_ref, wa_ref, wb_ref, wo_ref, g1_ref, b1_ref, wr_ref, br_ref,
                x1_ref, x1p_ref, ri_ref, rw_ref, cnt_ref, carry_ref):
    i = pl.program_id(0)
    tm = x_ref.shape[0]

    @pl.when(i == 0)
    def _():
        carry_ref[...] = jnp.zeros_like(carry_ref)

    ya = jnp.dot(ha_ref[...], wa_ref[...], preferred_element_type=F32)
    yb = jnp.dot(ob_ref[...], wb_ref[...], preferred_element_type=F32)
    merged = _sigmoid(ga_ref[...].astype(F32)) * ya + _sigmoid(gb_ref[...].astype(F32)) * yb
    mix = jnp.dot(merged.astype(BF16), wo_ref[...], preferred_element_type=F32)
    pre = DEEPNORM_ALPHA * x_ref[...] + mix
    mu = jnp.mean(pre, axis=-1, keepdims=True)
    pc = pre - mu
    var = jnp.mean(pc * pc, axis=-1, keepdims=True)
    x1 = pc * lax.rsqrt(var + LN_EPS) * g1_ref[...] + b1_ref[...]
    x1_ref[...] = x1
    _rows_to_tiles(x1p_ref, _pack_halves(x1))

    logits = _dot_3pass(x1, wr_ref[...]) + br_ref[...]
    lt = logits.T
    row8 = lax.broadcasted_iota(I32, (SUBLANES, tm), 0).astype(F32)
    gl = jnp.where(row8 < N_GROUPS, lt[0:SUBLANES, :], NEG_INF)
    gmax = jnp.max(gl, axis=0, keepdims=True)
    gsel = jnp.min(jnp.where(gl == gmax, row8, float(SUBLANES)), axis=0, keepdims=True)
    gprob = 1.0 / jnp.sum(jnp.exp(gl - gmax), axis=0, keepdims=True)
    ing = lt[SUBLANES:2 * SUBLANES, :]
    for g in range(1, N_GROUPS):
        ing = jnp.where(gsel == g, lt[(g + 1) * SUBLANES:(g + 2) * SUBLANES, :], ing)
    v0 = jnp.max(ing, axis=0, keepdims=True)
    i0 = jnp.min(jnp.where(ing == v0, row8, float(SUBLANES)), axis=0, keepdims=True)
    ing2 = jnp.where(row8 == i0, -jnp.inf, ing)
    v1 = jnp.max(ing2, axis=0, keepdims=True)
    i1 = jnp.min(jnp.where(ing2 == v1, row8, float(SUBLANES)), axis=0, keepdims=True)
    ex = jnp.exp(v1 - v0)
    inv = 1.0 / (1.0 + ex)
    w0 = gprob * inv
    w1 = gprob * (ex * inv)
    e0 = (gsel * EXPERTS_PER_GROUP + i0).astype(I32)
    e1 = (gsel * EXPERTS_PER_GROUP + i1).astype(I32)

    rowe = lax.broadcasted_iota(I32, (N_EXPERTS, tm), 0)
    is0 = rowe == e0
    is1 = rowe == e1
    oh = jnp.where(is0, 1.0, 0.0) + jnp.where(is1, 1.0, 0.0)
    tr = lax.broadcasted_iota(I32, (tm, tm), 0)
    tc = lax.broadcasted_iota(I32, (tm, tm), 1)
    su = jnp.where(tr < tc, 1.0, 0.0).astype(BF16)
    tot = jnp.dot(oh.astype(BF16), su, preferred_element_type=F32) + carry_ref[...]
    rank0 = jnp.sum(jnp.where(is0, tot, 0.0), axis=0, keepdims=True)
    rank1 = jnp.sum(jnp.where(is1, tot, 0.0), axis=0, keepdims=True)
    carry = carry_ref[...] + jnp.sum(oh, axis=1, keepdims=True)
    carry_ref[...] = carry
    cnt_ref[...] = jnp.broadcast_to(carry, cnt_ref.shape).astype(I32)

    zi = jnp.zeros((1, tm), I32)
    ri_ref[...] = jnp.concatenate([e0, e1, rank0.astype(I32), rank1.astype(I32), zi, zi, zi, zi], axis=0)
    zf = jnp.zeros((1, tm), F32)
    rw_ref[...] = jnp.concatenate([w0, w1, zf, zf, zf, zf, zf, zf], axis=0)


def _mix(h_a, o_b, z_main, x2d, w_a, w_b, w_out, ln_g, ln_b, w_r, b_r):
    T, D = x2d.shape
    tm = min(MIX_TM, T)
    const = lambda shape: pl.BlockSpec(shape, lambda i: (0, 0), pipeline_mode=pl.Buffered(1))
    return pl.pallas_call(
        _mix_kernel,
        grid=(T // tm,),
        in_specs=[
            pl.BlockSpec((tm, A_WIDTH), lambda i: (i, 0)),
            pl.BlockSpec((tm, B_WIDTH), lambda i: (i, 0)),
            pl.BlockSpec((tm, D), lambda i: (i, COL_GA // D)),
            pl.BlockSpec((tm, D), lambda i: (i, COL_GB // D)),
            pl.BlockSpec((tm, D), lambda i: (i, 0)),
            const((A_WIDTH, D)), const((B_WIDTH, D)), const((D, D)),
            const((1, D)), const((1, D)), const((D, 2 * LANES)), const((1, LANES)),
        ],
        out_specs=[
            pl.BlockSpec((tm, D), lambda i: (i, 0)),
            pl.BlockSpec((tm * SUBLANES, LANES), lambda i: (i, 0)),
            pl.BlockSpec((SUBLANES, tm), lambda i: (0, i)),
            pl.BlockSpec((SUBLANES, tm), lambda i: (0, i)),
            pl.BlockSpec((N_EXPERTS, LANES), lambda i: (0, 0)),
        ],
        out_shape=[
            jax.ShapeDtypeStruct((T, D), F32),
            jax.ShapeDtypeStruct((T * SUBLANES, LANES), U32),
            jax.ShapeDtypeStruct((SUBLANES, T), I32),
            jax.ShapeDtypeStruct((SUBLANES, T), F32),
            jax.ShapeDtypeStruct((N_EXPERTS, LANES), I32),
        ],
        scratch_shapes=[pltpu.VMEM((N_EXPERTS, 1), F32)],
        compiler_params=_cparams(("arbitrary",)),
        name="mix_ln1_router",
    )(h_a, o_b, z_main, z_main, x2d, w_a, w_b, w_out, ln_g, ln_b, w_r, b_r)


def _row_copy(src_ref, src_row, dst_ref, dst_row, sem):
    return pltpu.make_async_copy(src_ref.at[pl.ds(src_row * SUBLANES, SUBLANES)],
                                 dst_ref.at[pl.ds(dst_row * SUBLANES, SUBLANES)], sem)


def _rows_wait(src_ref, dst_ref, rows, sem):
    n = rows * SUBLANES
    pltpu.make_async_copy(src_ref.at[pl.ds(0, n)], dst_ref.at[pl.ds(0, n)], sem).wait()


DISPATCH_SLOTS = 3


def _dispatch_kernel(cnt_ref, ri_ref, x1_ref, xpad_ref, pstart_ref, bexp_ref, nused_ref,
                     pst_ref, zero_ref, xbuf_ref, sem, lsem, zsem, *, tb, n_blocks, tm):
    i = pl.program_id(0)
    last = pl.num_programs(0) - 1

    @pl.when(i == 0)
    def _():
        def fill(b, carry):
            bexp_ref[b] = 0
            return carry
        lax.fori_loop(0, n_blocks, fill, 0)

        def per_expert(e, blk):
            pst_ref[e] = blk * tb
            pstart_ref[e] = blk * tb
            nb = (cnt_ref[e] + (tb - 1)) // tb

            def mark(b, carry):
                bexp_ref[blk + b] = e
                return carry
            lax.fori_loop(0, nb, mark, 0)
            return blk + nb
        nused = lax.fori_loop(0, N_EXPERTS, per_expert, 0)
        nused_ref[0] = nused
        pst_ref[N_EXPERTS] = nused

    rows = tm * SUBLANES
    slot = i % DISPATCH_SLOTS

    def tile_load(tile, s):
        src = x1_ref.at[pl.ds(pl.multiple_of(tile * rows, rows), rows)]
        return pltpu.make_async_copy(src, xbuf_ref.at[s], lsem.at[s])

    def wait_rows(s):
        for k in range(TOP_K):
            _rows_wait(xbuf_ref.at[s], xpad_ref, tm, sem.at[s])

    @pl.when(i == 0)
    def _():
        tile_load(0, 0).start()

        @pl.when(last >= 1)
        def _():
            tile_load(1, 1).start()

    tile_load(i, slot).wait()
    for t in range(tm):
        for k in range(TOP_K):
            pos = pst_ref[ri_ref[k, t]] + ri_ref[TOP_K + k, t]
            _row_copy(xbuf_ref.at[slot], t, xpad_ref, pos, sem.at[slot]).start(priority=k)

    prev = (i + DISPATCH_SLOTS - 1) % DISPATCH_SLOTS

    @pl.when(i >= 1)
    def _():
        wait_rows(prev)

    @pl.when(i + 2 <= last)
    def _():
        tile_load(i + 2, prev).start()

    @pl.when(i == last)
    def _():
        wait_rows(slot)
        zero_ref[...] = jnp.zeros_like(zero_ref)

        def per_expert(e, carry):
            cnt = cnt_ref[e]
            first = pst_ref[e] + cnt
            npad = ((cnt + (tb - 1)) // tb) * tb - cnt

            def start(p, c2):
                _row_copy(zero_ref, 0, xpad_ref, first + p, zsem).start()
                return c2
            lax.fori_loop(0, npad, start, 0)

            def wait(p, c2):
                _row_copy(zero_ref, 0, xpad_ref, 0, zsem).wait()
                return c2
            lax.fori_loop(0, npad, wait, 0)
            return carry
        lax.fori_loop(0, N_EXPERTS, per_expert, 0)

        def block_copy(b):
            n = tb * SUBLANES
            return pltpu.make_async_copy(zero_ref, xpad_ref.at[pl.ds(pl.multiple_of(b * n, n), n)], zsem)

        def start_block(b, carry):
            block_copy(b).start()
            return carry
        lax.fori_loop(pst_ref[N_EXPERTS], n_blocks, start_block, 0)

        def wait_block(b, carry):
            block_copy(b).wait()
            return carry
        lax.fori_loop(pst_ref[N_EXPERTS], n_blocks, wait_block, 0)


def _dispatch(counts, route_i, x1, tb, n_blocks):
    T = x1.shape[0] // SUBLANES
    tm = min(DISPATCH_TM, T)
    smem_full = pl.BlockSpec(memory_space=pltpu.SMEM)
    return pl.pallas_call(
        functools.partial(_dispatch_kernel, tb=tb, n_blocks=n_blocks, tm=tm),
        grid=(T // tm,),
        in_specs=[
            smem_full,
            pl.BlockSpec((SUBLANES, tm), lambda i: (0, i), memory_space=pltpu.SMEM),
            pl.BlockSpec(memory_space=pl.ANY),
        ],
        out_specs=[
            pl.BlockSpec(memory_space=pl.ANY),
            smem_full, smem_full, smem_full,
        ],
        out_shape=[
            jax.ShapeDtypeStruct((n_blocks * tb * SUBLANES, LANES), x1.dtype),
            jax.ShapeDtypeStruct((N_EXPERTS,), I32),
            jax.ShapeDtypeStruct((n_blocks,), I32),
            jax.ShapeDtypeStruct((1,), I32),
        ],
        scratch_shapes=[
            pltpu.SMEM((N_EXPERTS + 1,), I32),
            pltpu.VMEM((tb * SUBLANES, LANES), x1.dtype),
            pltpu.VMEM((DISPATCH_SLOTS, tm * SUBLANES, LANES), x1.dtype),
            pltpu.SemaphoreType.DMA((DISPATCH_SLOTS,)),
            pltpu.SemaphoreType.DMA((DISPATCH_SLOTS,)),
            pltpu.SemaphoreType.DMA(()),
        ],
        compiler_params=_cparams(("arbitrary",)),
        name="moe_dispatch",
    )(counts, route_i, x1)


def _expert_kernel(bexp_ref, nused_ref, x_ref, wg_hbm, wu_hbm, wd_hbm, y_ref,
                   wgf_ref, wuf_ref, wdf_ref, wgb_ref, wub_ref, wdb_ref, sem, run_ref):
    b = pl.program_id(0)
    nused = nused_ref[0]
    used = b < nused
    e = bexp_ref[b]
    new_expert = jnp.logical_or(b == 0, e != bexp_ref[jnp.maximum(b - 1, 0)])
    streams = ((wg_hbm, wgf_ref), (wu_hbm, wuf_ref), (wd_hbm, wdf_ref))

    def weight_copies(expert, slot):
        return [pltpu.make_async_copy(w_hbm.at[expert], wf_ref.at[slot], sem.at[slot, j])
                for j, (w_hbm, wf_ref) in enumerate(streams)]

    @pl.when(jnp.logical_and(used, b == 0))
    def _():
        run_ref[0] = 0
        for cp in weight_copies(e, 0):
            cp.start()

    @pl.when(jnp.logical_and(used, new_expert))
    def _():
        slot = run_ref[0] % 2
        nxt = lax.while_loop(lambda j: jnp.logical_and(j < nused, bexp_ref[jnp.minimum(j, nused - 1)] == e),
                             lambda j: j + 1, b + 1)

        @pl.when(nxt < nused)
        def _():
            for cp in weight_copies(bexp_ref[jnp.minimum(nxt, nused - 1)], 1 - slot):
                cp.start()

        for cp in weight_copies(e, slot):
            cp.wait()
        wgb_ref[...] = wgf_ref[slot].astype(BF16)
        wub_ref[...] = wuf_ref[slot].astype(BF16)
        wdb_ref[...] = wdf_ref[slot].astype(BF16)
        run_ref[0] = run_ref[0] + 1

    @pl.when(used)
    def _():
        half = SUBLANES * LANES
        x_lo, x_hi = (t.astype(BF16) for t in _unpack_halves(_tiles_to_rows(x_ref)))
        dot = functools.partial(jnp.dot, preferred_element_type=F32)
        gate = dot(x_lo, wgb_ref[:half, :]) + dot(x_hi, wgb_ref[half:, :])
        up = dot(x_lo, wub_ref[:half, :]) + dot(x_hi, wub_ref[half:, :])
        hid = (gate * _sigmoid(gate) * up).astype(BF16)
        _rows_to_tiles(y_ref, _pack_halves(dot(hid, wdb_ref[...])))

    @pl.when(jnp.logical_not(used))
    def _():
        y_ref[...] = jnp.zeros_like(y_ref)


def _experts(bexp, nused, x_pad, w_gate, w_up, w_down, tb):
    P = x_pad.shape[0] // SUBLANES
    D = 2 * TILE_WORDS
    n_blocks = P // tb
    rows = (tb * SUBLANES, LANES)
    F = w_gate.shape[-1]
    blk = lambda b, be, nu: jnp.maximum(jnp.minimum(b, nu[0] - 1), 0)
    hbm = pl.BlockSpec(memory_space=pl.ANY)
    return pl.pallas_call(
        _expert_kernel,
        grid_spec=pltpu.PrefetchScalarGridSpec(
            num_scalar_prefetch=2,
            grid=(n_blocks,),
            in_specs=[pl.BlockSpec(rows, lambda b, be, nu: (blk(b, be, nu), 0)), hbm, hbm, hbm],
            out_specs=pl.BlockSpec(rows, lambda b, be, nu: (b, 0)),
            scratch_shapes=[
                pltpu.VMEM((2, D, F), F32), pltpu.VMEM((2, D, F), F32), pltpu.VMEM((2, F, D), F32),
                pltpu.VMEM((D, F), BF16), pltpu.VMEM((D, F), BF16), pltpu.VMEM((F, D), BF16),
                pltpu.SemaphoreType.DMA((2, 3)),
                pltpu.SMEM((1,), I32),
            ],
        ),
        out_shape=jax.ShapeDtypeStruct(x_pad.shape, U32),
        compiler_params=_cparams(("arbitrary",)),
        name="moe_experts",
    )(bexp, nused, x_pad, w_gate, w_up, w_down)


COMBINE_SLOTS = 3


def _combine_kernel(pstart_ref, ri_ref, r1_ref, r2_ref, x1_ref, rw_ref, g_ref, b_ref, ypad_ref, o_ref, ybuf_ref, sem):
    i = pl.program_id(0)
    n = pl.num_programs(0)
    tm = x1_ref.shape[0]
    slot = i % COMBINE_SLOTS

    def gather_loop(route_ref, s):
        def issue(t, carry):
            for k in range(TOP_K):
                pos = pstart_ref[route_ref[k, t]] + route_ref[TOP_K + k, t]
                _row_copy(ypad_ref, pos, ybuf_ref.at[s, k], t, sem.at[s]).start(priority=k)
            return carry
        lax.fori_loop(0, tm, issue, 0, unroll=8)

    def wait_slot(s):
        for k in range(TOP_K):
            _rows_wait(ypad_ref, ybuf_ref.at[s, k], tm, sem.at[s])

    @pl.when(i == 0)
    def _():
        gather_loop(ri_ref, 0)
        gather_loop(r1_ref, 1)

    wait_slot(slot)

    rw = rw_ref[...]
    y0 = _unpack_halves(_tiles_to_rows(ybuf_ref.at[slot, 0]))
    y1 = _unpack_halves(_tiles_to_rows(ybuf_ref.at[slot, 1]))
    ffn = jnp.concatenate([y0[h] * rw[:, 0:1] + y1[h] * rw[:, 1:2] for h in range(2)], axis=1)
    pre = DEEPNORM_ALPHA * x1_ref[...] + ffn
    mu = jnp.mean(pre, axis=-1, keepdims=True)
    pc = pre - mu
    var = jnp.mean(pc * pc, axis=-1, keepdims=True)
    o_ref[...] = pc * lax.rsqrt(var + LN_EPS) * g_ref[...] + b_ref[...]

    nxt = (i + 2) % COMBINE_SLOTS
    for t in range(tm):
        for k in range(TOP_K):
            pos = pstart_ref[r2_ref[k, t]] + r2_ref[TOP_K + k, t]
            _row_copy(ypad_ref, pos, ybuf_ref.at[nxt, k], t, sem.at[nxt]).start(priority=k)

    @pl.when(i == n - 1)
    def _():
        wait_slot((i + 1) % COMBINE_SLOTS)
        wait_slot(nxt)


def _combine(pstart, route_i, x1, rw_col, ln_g, ln_b, y_pad):
    T, D = x1.shape
    tm = min(COMBINE_TM, T)
    n = T // tm
    return pl.pallas_call(
        _combine_kernel,
        grid=(n,),
        in_specs=[
            pl.BlockSpec(memory_space=pltpu.SMEM),
            pl.BlockSpec((SUBLANES, tm), lambda i: (0, i), memory_space=pltpu.SMEM),
            pl.BlockSpec((SUBLANES, tm), lambda i: (0, jnp.minimum(i + 1, n - 1)), memory_space=pltpu.SMEM),
            pl.BlockSpec((SUBLANES, tm), lambda i: (0, jnp.minimum(i + 2, n - 1)), memory_space=pltpu.SMEM),
            pl.BlockSpec((tm, D), lambda i: (i, 0)),
            pl.BlockSpec((tm, SUBLANES), lambda i: (i, 0)),
            pl.BlockSpec((1, D), lambda i: (0, 0)),
            pl.BlockSpec((1, D), lambda i: (0, 0)),
            pl.BlockSpec(memory_space=pl.ANY),
        ],
        out_specs=pl.BlockSpec((tm, D), lambda i: (i, 0)),
        out_shape=jax.ShapeDtypeStruct((T, D), F32),
        scratch_shapes=[
            pltpu.VMEM((COMBINE_SLOTS, TOP_K, tm * SUBLANES, LANES), U32),
            pltpu.SemaphoreType.DMA((COMBINE_SLOTS,)),
        ],
        compiler_params=_cparams(("arbitrary",)),
        name="moe_combine_ln2",
    )(pstart, route_i, route_i, route_i, x1, rw_col, ln_g, ln_b, y_pad)


N_GATE_COLS = 2 * A_HEADS
SRC_MLSTM = 0
SRC_GATES = 4 * A_WIDTH
SRC_DIFF = SRC_GATES + N_GATE_COLS
SRC_MERGE = SRC_DIFF + 3 * B_WIDTH
PREP_TN = 512


def _wprep_kernel(a_ref, o_ref):
    o_ref[...] = a_ref[...].T.astype(BF16)


def _wgate_kernel(a_ref, o_ref):
    a = a_ref[...]
    rows = jnp.concatenate([a, jnp.zeros((LANES - a.shape[0], a.shape[1]), F32)], axis=0)
    o_ref[...] = rows.T


def _rearrange_in_proj(w_in, b_in):
    K, n_in = w_in.shape
    tn = PREP_TN
    merge_blocks = (2 * D_MODEL) // tn
    mlstm_blocks = (4 * A_WIDTH) // tn
    assert SRC_MERGE % SUBLANES == 0 and SRC_DIFF % SUBLANES == 0

    def src_row(jb):
        s = SUBLANES
        merge = SRC_MERGE // s + (tn // s) * jb
        mlstm = SRC_MLSTM // s + (tn // s) * (jb - merge_blocks)
        diff = SRC_DIFF // s + (tn // s) * (jb - merge_blocks - mlstm_blocks)
        return s * jnp.where(jb < merge_blocks, merge, jnp.where(jb < merge_blocks + mlstm_blocks, mlstm, diff))

    w_t = jnp.swapaxes(w_in, 0, 1)
    w_main = pl.pallas_call(
        _wprep_kernel,
        grid=(N_MAIN // tn,),
        in_specs=[pl.BlockSpec((pl.Element(tn), pl.Element(K)), lambda jb: (src_row(jb), 0))],
        out_specs=pl.BlockSpec((K, tn), lambda jb: (0, jb)),
        out_shape=jax.ShapeDtypeStruct((K, N_MAIN), BF16),
        compiler_params=_cparams(("parallel",)),
        name="w_in_prep",
    )(w_t)
    b_main = jnp.concatenate([b_in[SRC_MERGE:], b_in[SRC_MLSTM:SRC_GATES], b_in[SRC_DIFF:SRC_MERGE]])[None, :]
    w_gate = pl.pallas_call(
        _wgate_kernel,
        grid=(1,),
        in_specs=[pl.BlockSpec((pl.Element(N_GATE_COLS), pl.Element(K)), lambda i: (SRC_GATES, 0))],
        out_specs=pl.BlockSpec((K, LANES), lambda i: (0, 0)),
        out_shape=jax.ShapeDtypeStruct((K, LANES), F32),
        name="w_gate_prep",
    )(w_t)
    b_gate = jnp.pad(b_in[SRC_GATES:SRC_DIFF], (0, LANES - N_GATE_COLS))[None, :]
    return w_main, b_main, _hi_lo_columns(w_gate), b_gate


def _layer(x, w_in, b_in, conv_w, conv_b, norm_a_g, lq1, lk1, lq2, lk2, norm_b_g, w_a, w_b, w_out,
           ln1_g, ln1_b, w_grp, b_grp, w_exp, b_exp, w_gate, w_up, w_down, ln2_g, ln2_b, lambda_init):
    B, S, D = x.shape
    T = B * S
    x2d = x.reshape(T, D)

    w_main, b_main, w_g, b_g = _rearrange_in_proj(w_in, b_in)
    z_main, z_gate = _in_projection(x2d, w_main, b_main, w_g, b_g)

    L = min(MLSTM_CHUNK, S)
    gp = _gate_prep(z_gate[:, :2 * A_HEADS].T, L)
    h_a = _mlstm(z_main, gp.T, gp, conv_w, conv_b[None, :], norm_a_g[None, :], B, S)
    o_b = _diff_attention(z_main, lq1[None, :], lk1[None, :], lq2[None, :], lk2[None, :],
                          norm_b_g[None, :], B, S, lambda_init)

    w_r = jnp.zeros((D, LANES), F32).at[:, :N_GROUPS].set(w_grp).at[:, SUBLANES:SUBLANES + N_EXPERTS].set(w_exp)
    b_r = jnp.zeros((LANES,), F32).at[:N_GROUPS].set(b_grp).at[SUBLANES:SUBLANES + N_EXPERTS].set(b_exp)[None, :]
    x1, x1_packed, route_i, route_w, counts = _mix(h_a, o_b, z_main, x2d, w_a.astype(BF16), w_b.astype(BF16),
                                                   w_out.astype(BF16), ln1_g[None, :], ln1_b[None, :],
                                                   _hi_lo_columns(w_r), b_r)

    tb = MOE_TB
    n_blocks = (T * TOP_K) // tb + N_EXPERTS
    x_pad, pstart, bexp, nused = _dispatch(counts[:, 0], route_i, x1_packed, tb, n_blocks)
    y_pad = _experts(bexp, nused, x_pad, w_gate, w_up, w_down, tb)
    out = _combine(pstart, route_i, x1, route_w.T, ln2_g[None, :], ln2_b[None, :], y_pad)
    return out.reshape(B, S, D)


def kernel(x, w_in, b_in, conv_w, conv_b, mlstm_norm_g, lambda_q1, lambda_k1, lambda_q2, lambda_k2,
           diff_norm_g, w_a, w_b, w_out, ln1_g, ln1_b, w_grp, b_grp, w_exp, b_exp,
           w_gate, w_up, w_down, ln2_g, ln2_b):
    for l in range(DEPTH):
        lambda_init = 0.8 - 0.6 * math.exp(-0.3 * l)
        x = _layer(x, w_in[l], b_in[l], conv_w[l], conv_b[l], mlstm_norm_g[l], lambda_q1[l], lambda_k1[l],
                   lambda_q2[l], lambda_k2[l], diff_norm_g[l], w_a[l], w_b[l], w_out[l], ln1_g[l], ln1_b[l],
                   w_grp[l], b_grp[l], w_exp[l], b_exp[l], w_gate[l], w_up[l], w_down[l], ln2_g[l], ln2_b[l],
                   lambda_init)
    return x
```

```python
import functools
import math

import jax
import jax.numpy as jnp
from jax import lax
from jax.experimental import pallas as pl
from jax.experimental.pallas import tpu as pltpu

F32 = jnp.float32
BF16 = jnp.bfloat16
I32 = jnp.int32
U32 = jnp.uint32

D_MODEL = 2048
A_HEADS = 4
A_HEAD_DIM = 256
A_WIDTH = A_HEADS * A_HEAD_DIM
CONV_WIDTH = 4
B_HEADS = 8
B_HEAD_DIM = 64
B_V_DIM = 2 * B_HEAD_DIM
B_WIDTH = B_HEADS * B_V_DIM
ATTN_CHUNK = 64
N_GROUPS = 4
EXPERTS_PER_GROUP = 8
N_EXPERTS = N_GROUPS * EXPERTS_PER_GROUP
TOP_K = 2
D_EXPERT = 512
DEPTH = 1
DEEPNORM_ALPHA = (2 * DEPTH) ** 0.25
LN_EPS = 1e-5
NEG_INF = float("-inf")
LOG2_E = 1.4426950408889634

LANES = 128
SUBLANES = 8
VMEM_LIMIT_BYTES = 56 * 1024 * 1024

COL_GA = 0
COL_GB = COL_GA + D_MODEL
COL_QA = COL_GB + D_MODEL
COL_KA = COL_QA + A_WIDTH
COL_VA = COL_KA + A_WIDTH
COL_OA = COL_VA + A_WIDTH
COL_QB = COL_OA + A_WIDTH
COL_KB = COL_QB + B_WIDTH
COL_VB = COL_KB + B_WIDTH
N_MAIN = COL_VB + B_WIDTH

PROJ_TM, PROJ_TN = 512, 2816
MLSTM_CHUNK = 512
ATTN_TQ = 1024
ATTN_DIAG_GROUPS = 4
MIX_TM = 512
MOE_TB = 256
DISPATCH_TM = 256
COMBINE_TM = 256


def _cparams(sem, vmem=VMEM_LIMIT_BYTES):
    return pltpu.CompilerParams(dimension_semantics=sem, vmem_limit_bytes=vmem)


def _sigmoid(x):
    return 0.5 * jnp.tanh(0.5 * x) + 0.5


def _pack_halves(x):
    n = x.shape[1] // 2
    lo = lax.bitcast_convert_type(x[:, :n].astype(BF16).astype(F32), U32)
    hi = lax.bitcast_convert_type(x[:, n:].astype(BF16).astype(F32), U32)
    return (lo >> 16) | hi


def _unpack_halves(u):
    lo = lax.bitcast_convert_type(u << 16, F32)
    hi = lax.bitcast_convert_type(u & jnp.uint32(0xFFFF0000), F32)
    return lo, hi


TILE_WORDS = SUBLANES * LANES


def _rows_to_tiles(ref, v):
    rows = v.shape[0]
    for j in range(SUBLANES):
        ref[pl.ds(j, rows, stride=SUBLANES), :] = v[:, j * LANES:(j + 1) * LANES]


def _tiles_to_rows(ref):
    rows = ref.shape[0] // SUBLANES
    return jnp.concatenate([ref[pl.ds(j, rows, stride=SUBLANES), :] for j in range(SUBLANES)], axis=1)


def _split_bf16(a):
    hi = a.astype(BF16)
    return hi, (a - hi.astype(F32)).astype(BF16)


def _hi_lo_columns(w):
    return jnp.concatenate(_split_bf16(w), axis=1)


def _dot_3pass(a, w_hl):
    a_hi, a_lo = _split_bf16(a)
    n = w_hl.shape[1] // 2
    dot = functools.partial(jnp.dot, preferred_element_type=F32)
    r = dot(a_hi, w_hl)
    return r[:, :n] + (r[:, n:] + dot(a_lo, w_hl[:, :n]))


def _proj_kernel(x_ref, w_ref, b_ref, wg_ref, bg_ref, z_ref, zg_ref, xb_ref):
    @pl.when(pl.program_id(1) == 0)
    def _():
        x = x_ref[...]
        xb_ref[...] = x.astype(BF16)
        zg_ref[...] = _dot_3pass(x, wg_ref[...]) + bg_ref[...]

    acc = jnp.dot(xb_ref[...], w_ref[...], preferred_element_type=F32)
    z_ref[...] = (acc + b_ref[...]).astype(z_ref.dtype)


def _in_projection(x2d, w_main, b_main, w_gate, b_gate):
    T, K = x2d.shape
    N = w_main.shape[1]
    tm, tn = min(PROJ_TM, T), PROJ_TN
    return pl.pallas_call(
        _proj_kernel,
        grid=(T // tm, N // tn),
        in_specs=[
            pl.BlockSpec((tm, K), lambda i, j: (i, 0)),
            pl.BlockSpec((K, tn), lambda i, j: (0, j)),
            pl.BlockSpec((1, tn), lambda i, j: (0, j)),
            pl.BlockSpec((K, 2 * LANES), lambda i, j: (0, 0)),
            pl.BlockSpec((1, LANES), lambda i, j: (0, 0)),
        ],
        out_specs=[
            pl.BlockSpec((tm, tn), lambda i, j: (i, j)),
            pl.BlockSpec((tm, LANES), lambda i, j: (i, 0)),
        ],
        out_shape=[
            jax.ShapeDtypeStruct((T, N), BF16),
            jax.ShapeDtypeStruct((T, LANES), F32),
        ],
        scratch_shapes=[pltpu.VMEM((tm, K), BF16)],
        compiler_params=_cparams(("parallel", "arbitrary")),
        name="in_projection",
    )(x2d, w_main, b_main, w_gate, b_gate)


def _gate_prep_kernel(g_ref, o_ref, *, L):
    r = lax.broadcasted_iota(I32, (L, L), 0)
    c = lax.broadcasted_iota(I32, (L, L), 1)
    tri = (r <= c).astype(F32)
    row = lax.broadcasted_iota(I32, (g_ref.shape[0], L), 0)
    for ch in range(g_ref.shape[1] // L):
        g = g_ref[:, ch * L:(ch + 1) * L]
        lf = jnp.minimum(g, 0.0) - jnp.log(1.0 + jnp.exp(-jnp.abs(g)))
        b = jnp.dot(lf, tri, preferred_element_type=F32, precision=lax.Precision.HIGHEST)
        o_ref[:, ch * L:(ch + 1) * L] = jnp.where(row < A_HEADS, g, b)


def _gate_prep(g_rows, L):
    R, T = g_rows.shape
    per_step = math.gcd(T // L, 8)
    W = L * per_step
    return pl.pallas_call(
        functools.partial(_gate_prep_kernel, L=L),
        grid=(T // W,),
        in_specs=[pl.BlockSpec((R, W), lambda i: (0, i))],
        out_specs=pl.BlockSpec((R, W), lambda i: (0, i)),
        out_shape=jax.ShapeDtypeStruct((R, T), F32),
        compiler_params=_cparams(("parallel",)),
        name="gate_prep",
    )(g_rows)


def _mlstm_kernel(q_ref, k_ref, v_ref, o_ref, gcol_ref, grow_ref, cw_ref, cb_ref, ng_ref, out_ref,
                  c_ref, n_ref, m_ref, qcar_ref, kcar_ref):
    ci = pl.program_id(1)
    L = q_ref.shape[0]
    dh = A_HEAD_DIM

    @pl.when(ci == 0)
    def _():
        c_ref[...] = jnp.zeros_like(c_ref)
        n_ref[...] = jnp.zeros_like(n_ref)
        m_ref[...] = jnp.zeros_like(m_ref)
        qcar_ref[...] = jnp.zeros_like(qcar_ref)
        kcar_ref[...] = jnp.zeros_like(kcar_ref)

    def conv_silu(u_ref, car_ref, lo, wcol):
        u = u_ref[:, lo:lo + dh].astype(F32)
        ext = jnp.concatenate([car_ref[:, lo:lo + dh], u], axis=0)
        w = cw_ref[:, wcol:wcol + dh]
        y = cb_ref[:, wcol:wcol + dh] + w[3:4, :] * u
        for j in range(CONV_WIDTH - 1):
            off = SUBLANES - (CONV_WIDTH - 1) + j
            y = y + w[j:j + 1, :] * ext[off:off + L, :]
        car_ref[:, lo:lo + dh] = u[L - SUBLANES:, :]
        return y * _sigmoid(y)

    r = lax.broadcasted_iota(I32, (L, L), 0)
    c = lax.broadcasted_iota(I32, (L, L), 1)
    causal = r >= c
    gcol = gcol_ref[...]
    grow = grow_ref[...]

    for h in range(A_HEADS):
        lo = h * dh
        q = conv_silu(q_ref, qcar_ref, lo, lo)
        k = conv_silu(k_ref, kcar_ref, lo, A_WIDTH + lo) * (dh ** -0.5)
        v_b = v_ref[:, lo:lo + dh]
        q_b = q.astype(BF16)
        k_b = k.astype(BF16)

        i_col, b_col = gcol[:, h:h + 1], gcol[:, A_HEADS + h:A_HEADS + h + 1]
        i_row, b_row = grow[h:h + 1, :], grow[A_HEADS + h:A_HEADS + h + 1, :]
        b_last = b_row[:, L - 1:L]
        m_prev = m_ref[h]

        dmat = jnp.where(causal, b_col - b_row + i_row, NEG_INF)
        inter = b_col + m_prev
        m_t = jnp.maximum(inter, jnp.max(dmat, axis=-1, keepdims=True))
        s = lax.dot_general(q_b, k_b, (((1,), (1,)), ((), ())), preferred_element_type=F32)
        w_intra = jnp.exp(dmat - m_t) * s
        w_inter = jnp.exp(inter - m_t)
        qc = jnp.dot(q_b, c_ref[h].astype(BF16), preferred_element_type=F32)
        num = w_inter * qc + jnp.dot(w_intra.astype(BF16), v_b, preferred_element_type=F32)
        qn = jnp.sum(q * n_ref[h], axis=-1, keepdims=True)
        den = w_inter * qn + jnp.sum(w_intra, axis=-1, keepdims=True)
        hh = num / jnp.maximum(jnp.abs(den), jnp.exp(-m_t))

        g_col = b_last - b_col + i_col
        m_new = jnp.maximum(b_last + m_prev, jnp.max(g_col, axis=0, keepdims=True))
        decay = jnp.exp(b_last + m_prev - m_new)
        kw = jnp.exp(g_col - m_new) * k
        c_ref[h] = decay * c_ref[h] + jnp.dot(kw.T.astype(BF16), v_b, preferred_element_type=F32)
        n_ref[h] = decay * n_ref[h] + jnp.sum(kw, axis=0, keepdims=True)
        m_ref[h] = m_new

        y = _sigmoid(o_ref[:, lo:lo + dh].astype(F32)) * hh
        mu = jnp.mean(y, axis=-1, keepdims=True)
        yc = y - mu
        var = jnp.mean(yc * yc, axis=-1, keepdims=True)
        out_ref[:, lo:lo + dh] = (yc * lax.rsqrt(var + LN_EPS) * ng_ref[:, lo:lo + dh]).astype(out_ref.dtype)


def _mlstm(z_main, gcol, grow, conv_w, conv_b, norm_g, B, S):
    L = min(MLSTM_CHUNK, S)
    nc = S // L
    dh = A_HEAD_DIM
    H = A_HEADS
    W = A_WIDTH
    row = lambda b, c: b * nc + c
    full = lambda shape: pl.BlockSpec(shape, lambda b, c: (0, 0))
    return pl.pallas_call(
        _mlstm_kernel,
        grid=(B, nc),
        in_specs=[
            pl.BlockSpec((L, W), lambda b, c: (row(b, c), COL_QA // W)),
            pl.BlockSpec((L, W), lambda b, c: (row(b, c), COL_KA // W)),
            pl.BlockSpec((L, W), lambda b, c: (row(b, c), COL_VA // W)),
            pl.BlockSpec((L, W), lambda b, c: (row(b, c), COL_OA // W)),
            pl.BlockSpec((L, SUBLANES), lambda b, c: (row(b, c), 0)),
            pl.BlockSpec((SUBLANES, L), lambda b, c: (0, row(b, c))),
            full((CONV_WIDTH, 2 * W)), full((1, 2 * W)), full((1, W)),
        ],
        out_specs=pl.BlockSpec((L, W), lambda b, c: (row(b, c), 0)),
        out_shape=jax.ShapeDtypeStruct((B * S, W), BF16),
        scratch_shapes=[
            pltpu.VMEM((H, dh, dh), F32),
            pltpu.VMEM((H, 1, dh), F32),
            pltpu.VMEM((H, 1, 1), F32),
            pltpu.VMEM((SUBLANES, W), F32),
            pltpu.VMEM((SUBLANES, W), F32),
        ],
        compiler_params=_cparams(("parallel", "arbitrary")),
        name="mlstm",
    )(z_main, z_main, z_main, z_main, gcol, grow, conv_w, conv_b, norm_g)


def _lane_tiles(t):
    return [t[:, i * LANES:(i + 1) * LANES] for i in range(t.shape[1] // LANES)]


def _attn_kernel(slope_ref, q_ref, k_ref, kb_ref, v_ref, lq1_ref, lk1_ref, lq2_ref, lk2_ref, g_ref, o_ref,
                 s_ref, mx_ref, acc_ref, corr_ref, *, lambda_init):
    h = pl.program_id(1)
    qi = pl.program_id(2)
    tq = q_ref.shape[0]
    tk = tq
    slope = slope_ref[h]
    nt = (((1,), (1,)), ((), ()))

    lane = lax.broadcasted_iota(I32, (tq, B_V_DIM), 1)
    q = q_ref[...] * (B_HEAD_DIM ** -0.5)
    zero = jnp.zeros_like(q)
    bias_cols = jnp.where(lane < 2, 1.0, 0.0).astype(BF16)
    qa = (jnp.concatenate([jnp.where(lane < B_HEAD_DIM, q, zero), bias_cols], axis=1),
          jnp.concatenate([jnp.where(lane >= B_HEAD_DIM, q, zero), bias_cols], axis=1))

    def scores(j):
        start = pl.multiple_of(j * tk, tk)
        ka = jnp.concatenate([k_ref[pl.ds(start, tk), :], kb_ref[pl.ds(start, tk), :]], axis=1)
        return [lax.dot_general(qa[n], ka, nt, preferred_element_type=F32) for n in range(2)]

    def keep(j, n, t, rows=slice(None)):
        t = t * LOG2_E
        s_ref[n, j, rows, :t.shape[1]] = t
        mx_ref[n, rows, :] = functools.reduce(jnp.maximum, _lane_tiles(t), mx_ref[n, rows, :])

    mx_ref[...] = jnp.full(mx_ref.shape, NEG_INF, F32)

    def pass_a(j):
        t = scores(j)
        for n in range(2):
            keep(j, n, t[n])

    def pass_a_pair(jj, carry):
        pass_a(2 * jj)
        pass_a(2 * jj + 1)
        return carry
    lax.fori_loop(0, qi // 2, pass_a_pair, 0)

    @pl.when(qi % 2 == 1)
    def _():
        pass_a(qi - 1)

    rg = tq // ATTN_DIAG_GROUPS

    @pl.when(qi == 0)
    def _():
        r = lax.broadcasted_iota(I32, (rg, rg), 0)
        c = lax.broadcasted_iota(I32, (rg, rg), 1)
        ahead = jnp.where(c <= r, 0.0, (r - c).astype(F32) * (2.0 * slope))
        corr_ref[...] = jnp.where((c // ATTN_CHUNK) <= (r // ATTN_CHUNK), ahead, NEG_INF)

    dstart = pl.multiple_of(qi * tk, tk)
    ka = jnp.concatenate([k_ref[pl.ds(dstart, tk), :], kb_ref[pl.ds(dstart, tk), :]], axis=1)
    for n in range(2):
        for g in range(ATTN_DIAG_GROUPS):
            kbeg, kend = g * rg, (g + 1) * rg
            band = lax.dot_general(qa[n][kbeg:kend], ka[:kend], nt, preferred_element_type=F32)
            square = band[:, kbeg:] + corr_ref[...]
            band = square if g == 0 else jnp.concatenate([band[:, :kbeg], square], axis=1)
            keep(qi, n, band, rows=slice(kbeg, kend))

    for n in range(2):
        m = jnp.max(mx_ref[n], axis=-1, keepdims=True)
        mx_ref[n] = jnp.broadcast_to(m, (tq, LANES))
    acc_ref[...] = jnp.zeros_like(acc_ref)

    def weigh(n, rows, s_tile, v_tile):
        mb = mx_ref[n, rows, :]
        p = jnp.concatenate([jnp.exp2(t - mb) for t in _lane_tiles(s_tile)], axis=1).astype(BF16)
        v_ones = jnp.concatenate([v_tile, jnp.ones((v_tile.shape[0], LANES), BF16)], axis=1)
        acc_ref[n, rows, :] += jnp.dot(p, v_ones, preferred_element_type=F32)

    def pass_b(j, carry):
        start = pl.multiple_of(j * tk, tk)
        vj = v_ref[pl.ds(start, tk), :]
        for n in range(2):
            weigh(n, slice(None), s_ref[n, j], vj)
        return carry
    lax.fori_loop(0, qi, pass_b, 0)

    for n in range(2):
        for g in range(ATTN_DIAG_GROUPS):
            kend = (g + 1) * rg
            weigh(n, slice(g * rg, kend), s_ref[n, qi, g * rg:kend, :kend], v_ref[pl.ds(dstart, kend), :])

    lam = (jnp.exp(jnp.sum(lq1_ref[...] * lk1_ref[...], axis=-1, keepdims=True))
           - jnp.exp(jnp.sum(lq2_ref[...] * lk2_ref[...], axis=-1, keepdims=True)) + lambda_init)
    dv = B_V_DIM
    a0, a1 = acc_ref[0], acc_ref[1]
    o = a0[:, :dv] / a0[:, dv:] - lam * (a1[:, :dv] / a1[:, dv:])
    ms = jnp.mean(o * o, axis=-1, keepdims=True)
    o_ref[...] = (o * lax.rsqrt(ms + LN_EPS) * g_ref[...] * (1.0 - lambda_init)).astype(o_ref.dtype)


def _diff_attention(z_main, lq1, lk1, lq2, lk2, norm_g, B, S, lambda_init):
    tq = min(ATTN_TQ, S)
    nq = S // tq
    H = B_HEADS
    dv = B_V_DIM
    slopes = 2.0 ** (-8.0 * jnp.arange(1, H + 1, dtype=F32) / H)
    assert S <= 256 * 256
    pos = lax.broadcasted_iota(I32, (H, S, dv), 1)
    col = lax.broadcasted_iota(I32, (H, S, dv), 2)
    within = pos % 256
    kbias = jnp.where(col == 0, within, jnp.where(col == 1, pos - within, 0)).astype(F32)
    kbias = (kbias * slopes[:, None, None]).astype(BF16)
    small = pl.BlockSpec((1, B_HEAD_DIM), lambda b, h, i: (0, 0))
    return pl.pallas_call(
        functools.partial(_attn_kernel, lambda_init=lambda_init),
        grid=(B, H, nq),
        in_specs=[
            pl.BlockSpec(memory_space=pltpu.SMEM),
            pl.BlockSpec((tq, dv), lambda b, h, i: (b * nq + i, COL_QB // dv + h)),
            pl.BlockSpec((S, dv), lambda b, h, i: (b, COL_KB // dv + h)),
            pl.BlockSpec((None, S, dv), lambda b, h, i: (h, 0, 0)),
            pl.BlockSpec((S, dv), lambda b, h, i: (b, COL_VB // dv + h)),
            small, small, small, small,
            pl.BlockSpec((1, dv), lambda b, h, i: (0, 0)),
        ],
        out_specs=pl.BlockSpec((tq, dv), lambda b, h, i: (b * nq + i, h)),
        out_shape=jax.ShapeDtypeStruct((B * S, B_WIDTH), BF16),
        scratch_shapes=[
            pltpu.VMEM((2, nq, tq, tq), F32),
            pltpu.VMEM((2, tq, LANES), F32),
            pltpu.VMEM((2, tq, dv + LANES), F32),
            pltpu.VMEM((tq // ATTN_DIAG_GROUPS, tq // ATTN_DIAG_GROUPS), F32),
        ],
        compiler_params=_cparams(("parallel", "parallel", "arbitrary")),
        name="diff_attention",
    )(slopes, z_main, z_main, kbias, z_main, lq1, lk1, lq2, lk2, norm_g)


def _mix_kernel(ha_ref, ob_ref, ga_ref, gb_ref, x_ref, wa_ref, wb_ref, wo_ref, g1_ref, b1_ref, wr_ref, br_ref,
                x1_ref, x1p_ref, ri_ref, rw_ref, cnt_ref, carry_ref):
    i = pl.program_id(0)
    tm = x_ref.shape[0]

    @pl.when(i == 0)
    def _():
        carry_ref[...] = jnp.zeros_like(carry_ref)

    ya = jnp.dot(ha_ref[...], wa_ref[...], preferred_element_type=F32)
    yb = jnp.dot(ob_ref[...], wb_ref[...], preferred_element_type=F32)
    merged = _sigmoid(ga_ref[...].astype(F32)) * ya + _sigmoid(gb_ref[...].astype(F32)) * yb
    mix = jnp.dot(merged.astype(BF16), wo_ref[...], preferred_element_type=F32)
    pre = DEEPNORM_ALPHA * x_ref[...] + mix
    mu = jnp.mean(pre, axis=-1, keepdims=True)
    pc = pre - mu
    var = jnp.mean(pc * pc, axis=-1, keepdims=True)
    x1 = pc * lax.rsqrt(var + LN_EPS) * g1_ref[...] + b1_ref[...]
    x1_ref[...] = x1
    _rows_to_tiles(x1p_ref, _pack_halves(x1))

    logits = _dot_3pass(x1, wr_ref[...]) + br_ref[...]
    lt = logits.T
    row8 = lax.broadcasted_iota(I32, (SUBLANES, tm), 0).astype(F32)
    gl = jnp.where(row8 < N_GROUPS, lt[0:SUBLANES, :], NEG_INF)
    gmax = jnp.max(gl, axis=0, keepdims=True)
    gsel = jnp.min(jnp.where(gl == gmax, row8, float(SUBLANES)), axis=0, keepdims=True)
    gprob = 1.0 / jnp.sum(jnp.exp(gl - gmax), axis=0, keepdims=True)
    ing = lt[SUBLANES:2 * SUBLANES, :]
    for g in range(1, N_GROUPS):
        ing = jnp.where(gsel == g, lt[(g + 1) * SUBLANES:(g + 2) * SUBLANES, :], ing)
    v0 = jnp.max(ing, axis=0, keepdims=True)
    i0 = jnp.min(jnp.where(ing == v0, row8, float(SUBLANES)), axis=0, keepdims=True)
    ing2 = jnp.where(row8 == i0, -jnp.inf, ing)
    v1 = jnp.max(ing2, axis=0, keepdims=True)
    i1 = jnp.min(jnp.where(ing2 == v1, row8, float(SUBLANES)), axis=0, keepdims=True)
    ex = jnp.exp(v1 - v0)
    inv = 1.0 / (1.0 + ex)
    w0 = gprob * inv
    w1 = gprob * (ex * inv)
    e0 = (gsel * EXPERTS_PER_GROUP + i0).astype(I32)
    e1 = (gsel * EXPERTS_PER_GROUP + i1).astype(I32)

    rowe = lax.broadcasted_iota(I32, (N_EXPERTS, tm), 0)
    is0 = rowe == e0
    is1 = rowe == e1
    oh = jnp.where(is0, 1.0, 0.0) + jnp.where(is1, 1.0, 0.0)
    tr = lax.broadcasted_iota(I32, (tm, tm), 0)
    tc = lax.broadcasted_iota(I32, (tm, tm), 1)
    su = jnp.where(tr < tc, 1.0, 0.0).astype(BF16)
    tot = jnp.dot(oh.astype(BF16), su, preferred_element_type=F32) + carry_ref[...]
    rank0 = jnp.sum(jnp.where(is0, tot, 0.0), axis=0, keepdims=True)
    rank1 = jnp.sum(jnp.where(is1, tot, 0.0), axis=0, keepdims=True)
    carry = carry_ref[...] + jnp.sum(oh, axis=1, keepdims=True)
    carry_ref[...] = carry
    cnt_ref[...] = jnp.broadcast_to(carry, cnt_ref.shape).astype(I32)

    zi = jnp.zeros((1, tm), I32)
    ri_ref[...] = jnp.concatenate([e0, e1, rank0.astype(I32), rank1.astype(I32), zi, zi, zi, zi], axis=0)
    zf = jnp.zeros((1, tm), F32)
    rw_ref[...] = jnp.concatenate([w0, w1, zf, zf, zf, zf, zf, zf], axis=0)


def _mix(h_a, o_b, z_main, x2d, w_a, w_b, w_out, ln_g, ln_b, w_r, b_r):
    T, D = x2d.shape
    tm = min(MIX_TM, T)
    const = lambda shape: pl.BlockSpec(shape, lambda i: (0, 0), pipeline_mode=pl.Buffered(1))
    return pl.pallas_call(
        _mix_kernel,
        grid=(T // tm,),
        in_specs=[
            pl.BlockSpec((tm, A_WIDTH), lambda i: (i, 0)),
            pl.BlockSpec((tm, B_WIDTH), lambda i: (i, 0)),
            pl.BlockSpec((tm, D), lambda i: (i, COL_GA // D)),
            pl.BlockSpec((tm, D), lambda i: (i, COL_GB // D)),
            pl.BlockSpec((tm, D), lambda i: (i, 0)),
            const((A_WIDTH, D)), const((B_WIDTH, D)), const((D, D)),
            const((1, D)), const((1, D)), const((D, 2 * LANES)), const((1, LANES)),
        ],
        out_specs=[
            pl.BlockSpec((tm, D), lambda i: (i, 0)),
            pl.BlockSpec((tm * SUBLANES, LANES), lambda i: (i, 0)),
            pl.BlockSpec((SUBLANES, tm), lambda i: (0, i)),
            pl.BlockSpec((SUBLANES, tm), lambda i: (0, i)),
            pl.BlockSpec((N_EXPERTS, LANES), lambda i: (0, 0)),
        ],
        out_shape=[
            jax.ShapeDtypeStruct((T, D), F32),
            jax.ShapeDtypeStruct((T * SUBLANES, LANES), U32),
            jax.ShapeDtypeStruct((SUBLANES, T), I32),
            jax.ShapeDtypeStruct((SUBLANES, T), F32),
            jax.ShapeDtypeStruct((N_EXPERTS, LANES), I32),
        ],
        scratch_shapes=[pltpu.VMEM((N_EXPERTS, 1), F32)],
        compiler_params=_cparams(("arbitrary",)),
        name="mix_ln1_router",
    )(h_a, o_b, z_main, z_main, x2d, w_a, w_b, w_out, ln_g, ln_b, w_r, b_r)


def _row_copy(src_ref, src_row, dst_ref, dst_row, sem):
    return pltpu.make_async_copy(src_ref.at[pl.ds(src_row * SUBLANES, SUBLANES)],
                                 dst_ref.at[pl.ds(dst_row * SUBLANES, SUBLANES)], sem)


def _rows_wait(src_ref, dst_ref, rows, sem):
    n = rows * SUBLANES
    pltpu.make_async_copy(src_ref.at[pl.ds(0, n)], dst_ref.at[pl.ds(0, n)], sem).wait()


DISPATCH_SLOTS = 3


def _dispatch_kernel(cnt_ref, ri_ref, x1_ref, xpad_ref, pstart_ref, bexp_ref, nused_ref,
                     pst_ref, zero_ref, xbuf_ref, sem, lsem, zsem, *, tb, n_blocks, tm):
    i = pl.program_id(0)
    last = pl.num_programs(0) - 1

    @pl.when(i == 0)
    def _():
        def fill(b, carry):
            bexp_ref[b] = 0
            return carry
        lax.fori_loop(0, n_blocks, fill, 0)

        def per_expert(e, blk):
            pst_ref[e] = blk * tb
            pstart_ref[e] = blk * tb
            nb = (cnt_ref[e] + (tb - 1)) // tb

            def mark(b, carry):
                bexp_ref[blk + b] = e
                return carry
            lax.fori_loop(0, nb, mark, 0)
            return blk + nb
        nused = lax.fori_loop(0, N_EXPERTS, per_expert, 0)
        nused_ref[0] = nused
        pst_ref[N_EXPERTS] = nused

    rows = tm * SUBLANES
    slot = i % DISPATCH_SLOTS

    def tile_load(tile, s):
        src = x1_ref.at[pl.ds(pl.multiple_of(tile * rows, rows), rows)]
        return pltpu.make_async_copy(src, xbuf_ref.at[s], lsem.at[s])

    def wait_rows(s):
        for k in range(TOP_K):
            _rows_wait(xbuf_ref.at[s], xpad_ref, tm, sem.at[s])

    @pl.when(i == 0)
    def _():
        tile_load(0, 0).start()

        @pl.when(last >= 1)
        def _():
            tile_load(1, 1).start()

    tile_load(i, slot).wait()
    for t in range(tm):
        for k in range(TOP_K):
            pos = pst_ref[ri_ref[k, t]] + ri_ref[TOP_K + k, t]
            _row_copy(xbuf_ref.at[slot], t, xpad_ref, pos, sem.at[slot]).start(priority=k)

    prev = (i + DISPATCH_SLOTS - 1) % DISPATCH_SLOTS

    @pl.when(i >= 1)
    def _():
        wait_rows(prev)

    @pl.when(i + 2 <= last)
    def _():
        tile_load(i + 2, prev).start()

    @pl.when(i == last)
    def _():
        wait_rows(slot)
        zero_ref[...] = jnp.zeros_like(zero_ref)

        def per_expert(e, carry):
            cnt = cnt_ref[e]
            first = pst_ref[e] + cnt
            npad = ((cnt + (tb - 1)) // tb) * tb - cnt

            def start(p, c2):
                _row_copy(zero_ref, 0, xpad_ref, first + p, zsem).start()
                return c2
            lax.fori_loop(0, npad, start, 0)

            def wait(p, c2):
                _row_copy(zero_ref, 0, xpad_ref, 0, zsem).wait()
                return c2
            lax.fori_loop(0, npad, wait, 0)
            return carry
        lax.fori_loop(0, N_EXPERTS, per_expert, 0)

        def block_copy(b):
            n = tb * SUBLANES
            return pltpu.make_async_copy(zero_ref, xpad_ref.at[pl.ds(pl.multiple_of(b * n, n), n)], zsem)

        def start_block(b, carry):
            block_copy(b).start()
            return carry
        lax.fori_loop(pst_ref[N_EXPERTS], n_blocks, start_block, 0)

        def wait_block(b, carry):
            block_copy(b).wait()
            return carry
        lax.fori_loop(pst_ref[N_EXPERTS], n_blocks, wait_block, 0)


def _dispatch(counts, route_i, x1, tb, n_blocks):
    T = x1.shape[0] // SUBLANES
    tm = min(DISPATCH_TM, T)
    smem_full = pl.BlockSpec(memory_space=pltpu.SMEM)
    return pl.pallas_call(
        functools.partial(_dispatch_kernel, tb=tb, n_blocks=n_blocks, tm=tm),
        grid=(T // tm,),
        in_specs=[
            smem_full,
            pl.BlockSpec((SUBLANES, tm), lambda i: (0, i), memory_space=pltpu.SMEM),
            pl.BlockSpec(memory_space=pl.ANY),
        ],
        out_specs=[
            pl.BlockSpec(memory_space=pl.ANY),
            smem_full, smem_full, smem_full,
        ],
        out_shape=[
            jax.ShapeDtypeStruct((n_blocks * tb * SUBLANES, LANES), x1.dtype),
            jax.ShapeDtypeStruct((N_EXPERTS,), I32),
            jax.ShapeDtypeStruct((n_blocks,), I32),
            jax.ShapeDtypeStruct((1,), I32),
        ],
        scratch_shapes=[
            pltpu.SMEM((N_EXPERTS + 1,), I32),
            pltpu.VMEM((tb * SUBLANES, LANES), x1.dtype),
            pltpu.VMEM((DISPATCH_SLOTS, tm * SUBLANES, LANES), x1.dtype),
            pltpu.SemaphoreType.DMA((DISPATCH_SLOTS,)),
            pltpu.SemaphoreType.DMA((DISPATCH_SLOTS,)),
            pltpu.SemaphoreType.DMA(()),
        ],
        compiler_params=_cparams(("arbitrary",)),
        name="moe_dispatch",
    )(counts, route_i, x1)


def _expert_kernel(bexp_ref, nused_ref, x_ref, wg_hbm, wu_hbm, wd_hbm, y_ref,
                   wgf_ref, wuf_ref, wdf_ref, wgb_ref, wub_ref, wdb_ref, sem, run_ref):
    b = pl.program_id(0)
    nused = nused_ref[0]
    used = b < nused
    e = bexp_ref[b]
    new_expert = jnp.logical_or(b == 0, e != bexp_ref[jnp.maximum(b - 1, 0)])
    streams = ((wg_hbm, wgf_ref), (wu_hbm, wuf_ref), (wd_hbm, wdf_ref))

    def weight_copies(expert, slot):
        return [pltpu.make_async_copy(w_hbm.at[expert], wf_ref.at[slot], sem.at[slot, j])
                for j, (w_hbm, wf_ref) in enumerate(streams)]

    @pl.when(jnp.logical_and(used, b == 0))
    def _():
        run_ref[0] = 0
        for cp in weight_copies(e, 0):
            cp.start()

    @pl.when(jnp.logical_and(used, new_expert))
    def _():
        slot = run_ref[0] % 2
        nxt = lax.while_loop(lambda j: jnp.logical_and(j < nused, bexp_ref[jnp.minimum(j, nused - 1)] == e),
                             lambda j: j + 1, b + 1)

        @pl.when(nxt < nused)
        def _():
            for cp in weight_copies(bexp_ref[jnp.minimum(nxt, nused - 1)], 1 - slot):
                cp.start()

        for cp in weight_copies(e, slot):
            cp.wait()
        wgb_ref[...] = wgf_ref[slot].astype(BF16)
        wub_ref[...] = wuf_ref[slot].astype(BF16)
        wdb_ref[...] = wdf_ref[slot].astype(BF16)
        run_ref[0] = run_ref[0] + 1

    @pl.when(used)
    def _():
        half = SUBLANES * LANES
        x_lo, x_hi = (t.astype(BF16) for t in _unpack_halves(_tiles_to_rows(x_ref)))
        dot = functools.partial(jnp.dot, preferred_element_type=F32)
        gate = dot(x_lo, wgb_ref[:half, :]) + dot(x_hi, wgb_ref[half:, :])
        up = dot(x_lo, wub_ref[:half, :]) + dot(x_hi, wub_ref[half:, :])
        hid = (gate * _sigmoid(gate) * up).astype(BF16)
        _rows_to_tiles(y_ref, _pack_halves(dot(hid, wdb_ref[...])))

    @pl.when(jnp.logical_not(used))
    def _():
        y_ref[...] = jnp.zeros_like(y_ref)


def _experts(bexp, nused, x_pad, w_gate, w_up, w_down, tb):
    P = x_pad.shape[0] // SUBLANES
    D = 2 * TILE_WORDS
    n_blocks = P // tb
    rows = (tb * SUBLANES, LANES)
    F = w_gate.shape[-1]
    blk = lambda b, be, nu: jnp.maximum(jnp.minimum(b, nu[0] - 1), 0)
    hbm = pl.BlockSpec(memory_space=pl.ANY)
    return pl.pallas_call(
        _expert_kernel,
        grid_spec=pltpu.PrefetchScalarGridSpec(
            num_scalar_prefetch=2,
            grid=(n_blocks,),
            in_specs=[pl.BlockSpec(rows, lambda b, be, nu: (blk(b, be, nu), 0)), hbm, hbm, hbm],
            out_specs=pl.BlockSpec(rows, lambda b, be, nu: (b, 0)),
            scratch_shapes=[
                pltpu.VMEM((2, D, F), F32), pltpu.VMEM((2, D, F), F32), pltpu.VMEM((2, F, D), F32),
                pltpu.VMEM((D, F), BF16), pltpu.VMEM((D, F), BF16), pltpu.VMEM((F, D), BF16),
                pltpu.SemaphoreType.DMA((2, 3)),
                pltpu.SMEM((1,), I32),
            ],
        ),
        out_shape=jax.ShapeDtypeStruct(x_pad.shape, U32),
        compiler_params=_cparams(("arbitrary",)),
        name="moe_experts",
    )(bexp, nused, x_pad, w_gate, w_up, w_down)


COMBINE_SLOTS = 3


def _combine_kernel(pstart_ref, ri_ref, r1_ref, r2_ref, x1_ref, rw_ref, g_ref, b_ref, ypad_ref, o_ref, ybuf_ref, sem):
    i = pl.program_id(0)
    n = pl.num_programs(0)
    tm = x1_ref.shape[0]
    slot = i % COMBINE_SLOTS

    def gather_loop(route_ref, s):
        def issue(t, carry):
            for k in range(TOP_K):
                pos = pstart_ref[route_ref[k, t]] + route_ref[TOP_K + k, t]
                _row_copy(ypad_ref, pos, ybuf_ref.at[s, k], t, sem.at[s]).start(priority=k)
            return carry
        lax.fori_loop(0, tm, issue, 0, unroll=8)

    def wait_slot(s):
        for k in range(TOP_K):
            _rows_wait(ypad_ref, ybuf_ref.at[s, k], tm, sem.at[s])

    @pl.when(i == 0)
    def _():
        gather_loop(ri_ref, 0)
        gather_loop(r1_ref, 1)

    wait_slot(slot)

    rw = rw_ref[...]
    y0 = _unpack_halves(_tiles_to_rows(ybuf_ref.at[slot, 0]))
    y1 = _unpack_halves(_tiles_to_rows(ybuf_ref.at[slot, 1]))
    ffn = jnp.concatenate([y0[h] * rw[:, 0:1] + y1[h] * rw[:, 1:2] for h in range(2)], axis=1)
    pre = DEEPNORM_ALPHA * x1_ref[...] + ffn
    mu = jnp.mean(pre, axis=-1, keepdims=True)
    pc = pre - mu
    var = jnp.mean(pc * pc, axis=-1, keepdims=True)
    o_ref[...] = pc * lax.rsqrt(var + LN_EPS) * g_ref[...] + b_ref[...]

    nxt = (i + 2) % COMBINE_SLOTS
    for t in range(tm):
        for k in range(TOP_K):
            pos = pstart_ref[r2_ref[k, t]] + r2_ref[TOP_K + k, t]
            _row_copy(ypad_ref, pos, ybuf_ref.at[nxt, k], t, sem.at[nxt]).start(priority=k)

    @pl.when(i == n - 1)
    def _():
        wait_slot((i + 1) % COMBINE_SLOTS)
        wait_slot(nxt)


def _combine(pstart, route_i, x1, rw_col, ln_g, ln_b, y_pad):
    T, D = x1.shape
    tm = min(COMBINE_TM, T)
    n = T // tm
    return pl.pallas_call(
        _combine_kernel,
        grid=(n,),
        in_specs=[
            pl.BlockSpec(memory_space=pltpu.SMEM),
            pl.BlockSpec((SUBLANES, tm), lambda i: (0, i), memory_space=pltpu.SMEM),
            pl.BlockSpec((SUBLANES, tm), lambda i: (0, jnp.minimum(i + 1, n - 1)), memory_space=pltpu.SMEM),
            pl.BlockSpec((SUBLANES, tm), lambda i: (0, jnp.minimum(i + 2, n - 1)), memory_space=pltpu.SMEM),
            pl.BlockSpec((tm, D), lambda i: (i, 0)),
            pl.BlockSpec((tm, SUBLANES), lambda i: (i, 0)),
            pl.BlockSpec((1, D), lambda i: (0, 0)),
            pl.BlockSpec((1, D), lambda i: (0, 0)),
            pl.BlockSpec(memory_space=pl.ANY),
        ],
        out_specs=pl.BlockSpec((tm, D), lambda i: (i, 0)),
        out_shape=jax.ShapeDtypeStruct((T, D), F32),
        scratch_shapes=[
            pltpu.VMEM((COMBINE_SLOTS, TOP_K, tm * SUBLANES, LANES), U32),
            pltpu.SemaphoreType.DMA((COMBINE_SLOTS,)),
        ],
        compiler_params=_cparams(("arbitrary",)),
        name="moe_combine_ln2",
    )(pstart, route_i, route_i, route_i, x1, rw_col, ln_g, ln_b, y_pad)


N_GATE_COLS = 2 * A_HEADS
SRC_MLSTM = 0
SRC_GATES = 4 * A_WIDTH
SRC_DIFF = SRC_GATES + N_GATE_COLS
SRC_MERGE = SRC_DIFF + 3 * B_WIDTH
PREP_TN = 512


def _wprep_kernel(a_ref, o_ref):
    o_ref[...] = a_ref[...].T.astype(BF16)


def _wgate_kernel(a_ref, o_ref):
    a = a_ref[...]
    rows = jnp.concatenate([a, jnp.zeros((LANES - a.shape[0], a.shape[1]), F32)], axis=0)
    o_ref[...] = rows.T


def _rearrange_in_proj(w_in, b_in):
    K, n_in = w_in.shape
    tn = PREP_TN
    merge_blocks = (2 * D_MODEL) // tn
    mlstm_blocks = (4 * A_WIDTH) // tn
    assert SRC_MERGE % SUBLANES == 0 and SRC_DIFF % SUBLANES == 0

    def src_row(jb):
        s = SUBLANES
        merge = SRC_MERGE // s + (tn // s) * jb
        mlstm = SRC_MLSTM // s + (tn // s) * (jb - merge_blocks)
        diff = SRC_DIFF // s + (tn // s) * (jb - merge_blocks - mlstm_blocks)
        return s * jnp.where(jb < merge_blocks, merge, jnp.where(jb < merge_blocks + mlstm_blocks, mlstm, diff))

    w_t = jnp.swapaxes(w_in, 0, 1)
    w_main = pl.pallas_call(
        _wprep_kernel,
        grid=(N_MAIN // tn,),
        in_specs=[pl.BlockSpec((pl.Element(tn), pl.Element(K)), lambda jb: (src_row(jb), 0))],
        out_specs=pl.BlockSpec((K, tn), lambda jb: (0, jb)),
        out_shape=jax.ShapeDtypeStruct((K, N_MAIN), BF16),
        compiler_params=_cparams(("parallel",)),
        name="w_in_prep",
    )(w_t)
    b_main = jnp.concatenate([b_in[SRC_MERGE:], b_in[SRC_MLSTM:SRC_GATES], b_in[SRC_DIFF:SRC_MERGE]])[None, :]
    w_gate = pl.pallas_call(
        _wgate_kernel,
        grid=(1,),
        in_specs=[pl.BlockSpec((pl.Element(N_GATE_COLS), pl.Element(K)), lambda i: (SRC_GATES, 0))],
        out_specs=pl.BlockSpec((K, LANES), lambda i: (0, 0)),
        out_shape=jax.ShapeDtypeStruct((K, LANES), F32),
        name="w_gate_prep",
    )(w_t)
    b_gate = jnp.pad(b_in[SRC_GATES:SRC_DIFF], (0, LANES - N_GATE_COLS))[None, :]
    return w_main, b_main, _hi_lo_columns(w_gate), b_gate


def _layer(x, w_in, b_in, conv_w, conv_b, norm_a_g, lq1, lk1, lq2, lk2, norm_b_g, w_a, w_b, w_out,
           ln1_g, ln1_b, w_grp, b_grp, w_exp, b_exp, w_gate, w_up, w_down, ln2_g, ln2_b, lambda_init):
    B, S, D = x.shape
    T = B * S
    x2d = x.reshape(T, D)

    w_main, b_main, w_g, b_g = _rearrange_in_proj(w_in, b_in)
    z_main, z_gate = _in_projection(x2d, w_main, b_main, w_g, b_g)

    L = min(MLSTM_CHUNK, S)
    gp = _gate_prep(z_gate[:, :2 * A_HEADS].T, L)
    h_a = _mlstm(z_main, gp.T, gp, conv_w, conv_b[None, :], norm_a_g[None, :], B, S)
    o_b = _diff_attention(z_main, lq1[None, :], lk1[None, :], lq2[None, :], lk2[None, :],
                          norm_b_g[None, :], B, S, lambda_init)

    w_r = jnp.zeros((D, LANES), F32).at[:, :N_GROUPS].set(w_grp).at[:, SUBLANES:SUBLANES + N_EXPERTS].set(w_exp)
    b_r = jnp.zeros((LANES,), F32).at[:N_GROUPS].set(b_grp).at[SUBLANES:SUBLANES + N_EXPERTS].set(b_exp)[None, :]
    x1, x1_packed, route_i, route_w, counts = _mix(h_a, o_b, z_main, x2d, w_a.astype(BF16), w_b.astype(BF16),
                                                   w_out.astype(BF16), ln1_g[None, :], ln1_b[None, :],
                                                   _hi_lo_columns(w_r), b_r)

    tb = MOE_TB
    n_blocks = (T * TOP_K) // tb + N_EXPERTS
    x_pad, pstart, bexp, nused = _dispatch(counts[:, 0], route_i, x1_packed, tb, n_blocks)
    y_pad = _experts(bexp, nused, x_pad, w_gate, w_up, w_down, tb)
    out = _combine(pstart, route_i, x1, route_w.T, ln2_g[None, :], ln2_b[None, :], y_pad)
    return out.reshape(B, S, D)


def kernel(x, w_in, b_in, conv_w, conv_b, mlstm_norm_g, lambda_q1, lambda_k1, lambda_q2, lambda_k2,
           diff_norm_g, w_a, w_b, w_out, ln1_g, ln1_b, w_grp, b_grp, w_exp, b_exp,
           w_gate, w_up, w_down, ln2_g, ln2_b):
    for l in range(DEPTH):
        lambda_init = 0.8 - 0.6 * math.exp(-0.3 * l)
        x = _layer(x, w_in[l], b_in[l], conv_w[l], conv_b[l], mlstm_norm_g[l], lambda_q1[l], lambda_k1[l],
                   lambda_q2[l], lambda_k2[l], diff_norm_g[l], w_a[l], w_b[l], w_out[l], ln1_g[l], ln1_b[l],
                   w_grp[l], b_grp[l], w_exp[l], b_exp[l], w_gate[l], w_up[l], w_down[l], ln2_g[l], ln2_b[l],
                   lambda_init)
    return x
```

```python
import functools
import math

import jax
import jax.numpy as jnp
from jax import lax
from jax.experimental import pallas as pl
from jax.experimental.pallas import tpu as pltpu

F32 = jnp.float32
BF16 = jnp.bfloat16
I32 = jnp.int32
U32 = jnp.uint32

D_MODEL = 2048
A_HEADS = 4
A_HEAD_DIM = 256
A_WIDTH = A_HEADS * A_HEAD_DIM
CONV_WIDTH = 4
B_HEADS = 8
B_HEAD_DIM = 64
B_V_DIM = 2 * B_HEAD_DIM
B_WIDTH = B_HEADS * B_V_DIM
ATTN_CHUNK = 64
N_GROUPS = 4
EXPERTS_PER_GROUP = 8
N_EXPERTS = N_GROUPS * EXPERTS_PER_GROUP
TOP_K = 2
D_EXPERT = 512
DEPTH = 1
DEEPNORM_ALPHA = (2 * DEPTH) ** 0.25
LN_EPS = 1e-5
NEG_INF = float("-inf")
LOG2_E = 1.4426950408889634

LANES = 128
SUBLANES = 8
VMEM_LIMIT_BYTES = 56 * 1024 * 1024

COL_GA = 0
COL_GB = COL_GA + D_MODEL
COL_QA = COL_GB + D_MODEL
COL_KA = COL_QA + A_WIDTH
COL_VA = COL_KA + A_WIDTH
COL_OA = COL_VA + A_WIDTH
COL_QB = COL_OA + A_WIDTH
COL_KB = COL_QB + B_WIDTH
COL_VB = COL_KB + B_WIDTH
N_MAIN = COL_VB + B_WIDTH

PROJ_TM, PROJ_TN = 512, 2816
MLSTM_CHUNK = 512
ATTN_TQ = 1024
ATTN_DIAG_GROUPS = 4
MIX_TM = 512
MOE_TB = 256
DISPATCH_TM = 256
COMBINE_TM = 256


def _cparams(sem, vmem=VMEM_LIMIT_BYTES):
    return pltpu.CompilerParams(dimension_semantics=sem, vmem_limit_bytes=vmem)


def _sigmoid(x):
    return 0.5 * jnp.tanh(0.5 * x) + 0.5


def _pack_halves(x):
    n = x.shape[1] // 2
    lo = lax.bitcast_convert_type(x[:, :n].astype(BF16).astype(F32), U32)
    hi = lax.bitcast_convert_type(x[:, n:].astype(BF16).astype(F32), U32)
    return (lo >> 16) | hi


def _unpack_halves(u):
    lo = lax.bitcast_convert_type(u << 16, F32)
    hi = lax.bitcast_convert_type(u & jnp.uint32(0xFFFF0000), F32)
    return lo, hi


TILE_WORDS = SUBLANES * LANES


def _rows_to_tiles(ref, v):
    rows = v.shape[0]
    for j in range(SUBLANES):
        ref[pl.ds(j, rows, stride=SUBLANES), :] = v[:, j * LANES:(j + 1) * LANES]


def _tiles_to_rows(ref):
    rows = ref.shape[0] // SUBLANES
    return jnp.concatenate([ref[pl.ds(j, rows, stride=SUBLANES), :] for j in range(SUBLANES)], axis=1)


def _split_bf16(a):
    hi = a.astype(BF16)
    return hi, (a - hi.astype(F32)).astype(BF16)


def _hi_lo_columns(w):
    return jnp.concatenate(_split_bf16(w), axis=1)


def _dot_3pass(a, w_hl):
    a_hi, a_lo = _split_bf16(a)
    n = w_hl.shape[1] // 2
    dot = functools.partial(jnp.dot, preferred_element_type=F32)
    r = dot(a_hi, w_hl)
    return r[:, :n] + (r[:, n:] + dot(a_lo, w_hl[:, :n]))


def _proj_kernel(x_ref, w_ref, b_ref, wg_ref, bg_ref, z_ref, zg_ref, xb_ref):
    @pl.when(pl.program_id(1) == 0)
    def _():
        x = x_ref[...]
        xb_ref[...] = x.astype(BF16)
        zg_ref[...] = _dot_3pass(x, wg_ref[...]) + bg_ref[...]

    acc = jnp.dot(xb_ref[...], w_ref[...], preferred_element_type=F32)
    z_ref[...] = (acc + b_ref[...]).astype(z_ref.dtype)


def _in_projection(x2d, w_main, b_main, w_gate, b_gate):
    T, K = x2d.shape
    N = w_main.shape[1]
    tm, tn = min(PROJ_TM, T), PROJ_TN
    return pl.pallas_call(
        _proj_kernel,
        grid=(T // tm, N // tn),
        in_specs=[
            pl.BlockSpec((tm, K), lambda i, j: (i, 0)),
            pl.BlockSpec((K, tn), lambda i, j: (0, j)),
            pl.BlockSpec((1, tn), lambda i, j: (0, j)),
            pl.BlockSpec((K, 2 * LANES), lambda i, j: (0, 0)),
            pl.BlockSpec((1, LANES), lambda i, j: (0, 0)),
        ],
        out_specs=[
            pl.BlockSpec((tm, tn), lambda i, j: (i, j)),
            pl.BlockSpec((tm, LANES), lambda i, j: (i, 0)),
        ],
        out_shape=[
            jax.ShapeDtypeStruct((T, N), BF16),
            jax.ShapeDtypeStruct((T, LANES), F32),
        ],
        scratch_shapes=[pltpu.VMEM((tm, K), BF16)],
        compiler_params=_cparams(("parallel", "arbitrary")),
        name="in_projection",
    )(x2d, w_main, b_main, w_gate, b_gate)


def _gate_prep_kernel(g_ref, o_ref, *, L):
    r = lax.broadcasted_iota(I32, (L, L), 0)
    c = lax.broadcasted_iota(I32, (L, L), 1)
    tri = (r <= c).astype(F32)
    row = lax.broadcasted_iota(I32, (g_ref.shape[0], L), 0)
    for ch in range(g_ref.shape[1] // L):
        g = g_ref[:, ch * L:(ch + 1) * L]
        lf = jnp.minimum(g, 0.0) - jnp.log(1.0 + jnp.exp(-jnp.abs(g)))
        b = jnp.dot(lf, tri, preferred_element_type=F32, precision=lax.Precision.HIGHEST)
        o_ref[:, ch * L:(ch + 1) * L] = jnp.where(row < A_HEADS, g, b)


def _gate_prep(g_rows, L):
    R, T = g_rows.shape
    per_step = math.gcd(T // L, 8)
    W = L * per_step
    return pl.pallas_call(
        functools.partial(_gate_prep_kernel, L=L),
        grid=(T // W,),
        in_specs=[pl.BlockSpec((R, W), lambda i: (0, i))],
        out_specs=pl.BlockSpec((R, W), lambda i: (0, i)),
        out_shape=jax.ShapeDtypeStruct((R, T), F32),
        compiler_params=_cparams(("parallel",)),
        name="gate_prep",
    )(g_rows)


def _mlstm_kernel(q_ref, k_ref, v_ref, o_ref, gcol_ref, grow_ref, cw_ref, cb_ref, ng_ref, out_ref,
                  c_ref, n_ref, m_ref, qcar_ref, kcar_ref):
    ci = pl.program_id(1)
    L = q_ref.shape[0]
    dh = A_HEAD_DIM

    @pl.when(ci == 0)
    def _():
        c_ref[...] = jnp.zeros_like(c_ref)
        n_ref[...] = jnp.zeros_like(n_ref)
        m_ref[...] = jnp.zeros_like(m_ref)
        qcar_ref[...] = jnp.zeros_like(qcar_ref)
        kcar_ref[...] = jnp.zeros_like(kcar_ref)

    def conv_silu(u_ref, car_ref, lo, wcol):
        u = u_ref[:, lo:lo + dh].astype(F32)
        ext = jnp.concatenate([car_ref[:, lo:lo + dh], u], axis=0)
        w = 0.5 * cw_ref[:, wcol:wcol + dh]
        h = 0.5 * cb_ref[:, wcol:wcol + dh] + w[3:4, :] * u
        for j in range(CONV_WIDTH - 1):
            off = SUBLANES - (CONV_WIDTH - 1) + j
            h = h + w[j:j + 1, :] * ext[off:off + L, :]
        car_ref[:, lo:lo + dh] = u[L - SUBLANES:, :]
        return h * jnp.tanh(h) + h

    r = lax.broadcasted_iota(I32, (L, L), 0)
    c = lax.broadcasted_iota(I32, (L, L), 1)
    causal = r >= c
    gcol = gcol_ref[...]
    grow = grow_ref[...]

    for h in range(A_HEADS):
        lo = h * dh
        q = conv_silu(q_ref, qcar_ref, lo, lo)
        k = conv_silu(k_ref, kcar_ref, lo, A_WIDTH + lo) * (dh ** -0.5)
        v_b = v_ref[:, lo:lo + dh]
        q_b = q.astype(BF16)
        k_b = k.astype(BF16)

        i_col, b_col = gcol[:, h:h + 1], gcol[:, A_HEADS + h:A_HEADS + h + 1]
        i_row, b_row = grow[h:h + 1, :], grow[A_HEADS + h:A_HEADS + h + 1, :]
        b_last = b_row[:, L - 1:L]
        m_prev = m_ref[h]

        dmat = jnp.where(causal, b_col - b_row + i_row, NEG_INF)
        inter = b_col + m_prev
        m_t = jnp.maximum(inter, jnp.max(dmat, axis=-1, keepdims=True))
        s = lax.dot_general(q_b, k_b, (((1,), (1,)), ((), ())), preferred_element_type=F32)
        w_intra = jnp.exp(dmat - m_t) * s
        w_inter = jnp.exp(inter - m_t)
        qc = jnp.dot(q_b, c_ref[h].astype(BF16), preferred_element_type=F32)
        num = w_inter * qc + jnp.dot(w_intra.astype(BF16), v_b, preferred_element_type=F32)
        qn = jnp.sum(q * n_ref[h], axis=-1, keepdims=True)
        den = w_inter * qn + jnp.sum(w_intra, axis=-1, keepdims=True)
        hh = num / jnp.maximum(jnp.abs(den), jnp.exp(-m_t))

        g_col = b_last - b_col + i_col
        m_new = jnp.maximum(b_last + m_prev, jnp.max(g_col, axis=0, keepdims=True))
        decay = jnp.exp(b_last + m_prev - m_new)
        kw = jnp.exp(g_col - m_new) * k
        c_ref[h] = decay * c_ref[h] + jnp.dot(kw.T.astype(BF16), v_b, preferred_element_type=F32)
        n_ref[h] = decay * n_ref[h] + jnp.sum(kw, axis=0, keepdims=True)
        m_ref[h] = m_new

        y = _sigmoid(o_ref[:, lo:lo + dh].astype(F32)) * hh
        mu = jnp.mean(y, axis=-1, keepdims=True)
        yc = y - mu
        var = jnp.mean(yc * yc, axis=-1, keepdims=True)
        out_ref[:, lo:lo + dh] = (yc * lax.rsqrt(var + LN_EPS) * ng_ref[:, lo:lo + dh]).astype(out_ref.dtype)


def _mlstm(z_main, gcol, grow, conv_w, conv_b, norm_g, B, S):
    L = min(MLSTM_CHUNK, S)
    nc = S // L
    dh = A_HEAD_DIM
    H = A_HEADS
    W = A_WIDTH
    row = lambda b, c: b * nc + c
    full = lambda shape: pl.BlockSpec(shape, lambda b, c: (0, 0))
    return pl.pallas_call(
        _mlstm_kernel,
        grid=(B, nc),
        in_specs=[
            pl.BlockSpec((L, W), lambda b, c: (row(b, c), COL_QA // W)),
            pl.BlockSpec((L, W), lambda b, c: (row(b, c), COL_KA // W)),
            pl.BlockSpec((L, W), lambda b, c: (row(b, c), COL_VA // W)),
            pl.BlockSpec((L, W), lambda b, c: (row(b, c), COL_OA // W)),
            pl.BlockSpec((L, SUBLANES), lambda b, c: (row(b, c), 0)),
            pl.BlockSpec((SUBLANES, L), lambda b, c: (0, row(b, c))),
            full((CONV_WIDTH, 2 * W)), full((1, 2 * W)), full((1, W)),
        ],
        out_specs=pl.BlockSpec((L, W), lambda b, c: (row(b, c), 0)),
        out_shape=jax.ShapeDtypeStruct((B * S, W), BF16),
        scratch_shapes=[
            pltpu.VMEM((H, dh, dh), F32),
            pltpu.VMEM((H, 1, dh), F32),
            pltpu.VMEM((H, 1, 1), F32),
            pltpu.VMEM((SUBLANES, W), F32),
            pltpu.VMEM((SUBLANES, W), F32),
        ],
        compiler_params=_cparams(("parallel", "arbitrary")),
        name="mlstm",
    )(z_main, z_main, z_main, z_main, gcol, grow, conv_w, conv_b, norm_g)


def _lane_tiles(t):
    return [t[:, i * LANES:(i + 1) * LANES] for i in range(t.shape[1] // LANES)]


def _attn_kernel(slope_ref, q_ref, k_ref, kb_ref, v_ref, lq1_ref, lk1_ref, lq2_ref, lk2_ref, g_ref, o_ref,
                 s_ref, mx_ref, acc_ref, corr_ref, *, lambda_init):
    h = pl.program_id(1)
    qi = pl.program_id(2)
    tq = q_ref.shape[0]
    tk = tq
    slope = slope_ref[h]
    nt = (((1,), (1,)), ((), ()))

    lane = lax.broadcasted_iota(I32, (tq, B_V_DIM), 1)
    q = q_ref[...] * (B_HEAD_DIM ** -0.5)
    zero = jnp.zeros_like(q)
    bias_cols = jnp.where(lane < 2, 1.0, 0.0).astype(BF16)
    qa = (jnp.concatenate([jnp.where(lane < B_HEAD_DIM, q, zero), bias_cols], axis=1),
          jnp.concatenate([jnp.where(lane >= B_HEAD_DIM, q, zero), bias_cols], axis=1))

    def scores(j):
        start = pl.multiple_of(j * tk, tk)
        ka = jnp.concatenate([k_ref[pl.ds(start, tk), :], kb_ref[pl.ds(start, tk), :]], axis=1)
        return [lax.dot_general(qa[n], ka, nt, preferred_element_type=F32) for n in range(2)]

    def keep(j, n, t, rows=slice(None)):
        t = t * LOG2_E
        s_ref[n, j, rows, :t.shape[1]] = t
        mx_ref[n, rows, :] = functools.reduce(jnp.maximum, _lane_tiles(t), mx_ref[n, rows, :])

    mx_ref[...] = jnp.full(mx_ref.shape, NEG_INF, F32)

    def pass_a(j):
        t = scores(j)
        for n in range(2):
            keep(j, n, t[n])

    def pass_a_pair(jj, carry):
        pass_a(2 * jj)
        pass_a(2 * jj + 1)
        return carry
    lax.fori_loop(0, qi // 2, pass_a_pair, 0)

    @pl.when(qi % 2 == 1)
    def _():
        pass_a(qi - 1)

    rg = tq // ATTN_DIAG_GROUPS

    @pl.when(qi == 0)
    def _():
        r = lax.broadcasted_iota(I32, (rg, rg), 0)
        c = lax.broadcasted_iota(I32, (rg, rg), 1)
        ahead = jnp.where(c <= r, 0.0, (r - c).astype(F32) * (2.0 * slope))
        corr_ref[...] = jnp.where((c // ATTN_CHUNK) <= (r // ATTN_CHUNK), ahead, NEG_INF)

    dstart = pl.multiple_of(qi * tk, tk)
    ka = jnp.concatenate([k_ref[pl.ds(dstart, tk), :], kb_ref[pl.ds(dstart, tk), :]], axis=1)
    for n in range(2):
        for g in range(ATTN_DIAG_GROUPS):
            kbeg, kend = g * rg, (g + 1) * rg
            band = lax.dot_general(qa[n][kbeg:kend], ka[:kend], nt, preferred_element_type=F32)
            square = band[:, kbeg:] + corr_ref[...]
            band = square if g == 0 else jnp.concatenate([band[:, :kbeg], square], axis=1)
            keep(qi, n, band, rows=slice(kbeg, kend))

    for n in range(2):
        m = jnp.max(mx_ref[n], axis=-1, keepdims=True)
        mx_ref[n] = jnp.broadcast_to(m, (tq, LANES))
    acc_ref[...] = jnp.zeros_like(acc_ref)

    def weigh(n, rows, s_tile, v_tile):
        mb = mx_ref[n, rows, :]
        p = jnp.concatenate([jnp.exp2(t - mb) for t in _lane_tiles(s_tile)], axis=1).astype(BF16)
        v_ones = jnp.concatenate([v_tile, jnp.ones((v_tile.shape[0], LANES), BF16)], axis=1)
        acc_ref[n, rows, :] += jnp.dot(p, v_ones, preferred_element_type=F32)

    def pass_b(j, carry):
        start = pl.multiple_of(j * tk, tk)
        vj = v_ref[pl.ds(start, tk), :]
        for n in range(2):
            weigh(n, slice(None), s_ref[n, j], vj)
        return carry
    lax.fori_loop(0, qi, pass_b, 0)

    for n in range(2):
        for g in range(ATTN_DIAG_GROUPS):
            kend = (g + 1) * rg
            weigh(n, slice(g * rg, kend), s_ref[n, qi, g * rg:kend, :kend], v_ref[pl.ds(dstart, kend), :])

    lam = (jnp.exp(jnp.sum(lq1_ref[...] * lk1_ref[...], axis=-1, keepdims=True))
           - jnp.exp(jnp.sum(lq2_ref[...] * lk2_ref[...], axis=-1, keepdims=True)) + lambda_init)
    dv = B_V_DIM
    a0, a1 = acc_ref[0], acc_ref[1]
    o = a0[:, :dv] / a0[:, dv:] - lam * (a1[:, :dv] / a1[:, dv:])
    ms = jnp.mean(o * o, axis=-1, keepdims=True)
    o_ref[...] = (o * lax.rsqrt(ms + LN_EPS) * g_ref[...] * (1.0 - lambda_init)).astype(o_ref.dtype)


def _diff_attention(z_main, lq1, lk1, lq2, lk2, norm_g, B, S, lambda_init):
    tq = min(ATTN_TQ, S)
    nq = S // tq
    H = B_HEADS
    dv = B_V_DIM
    slopes = 2.0 ** (-8.0 * jnp.arange(1, H + 1, dtype=F32) / H)
    assert S <= 256 * 256
    pos = lax.broadcasted_iota(I32, (H, S, dv), 1)
    col = lax.broadcasted_iota(I32, (H, S, dv), 2)
    within = pos % 256
    kbias = jnp.where(col == 0, within, jnp.where(col == 1, pos - within, 0)).astype(F32)
    kbias = (kbias * slopes[:, None, None]).astype(BF16)
    small = pl.BlockSpec((1, B_HEAD_DIM), lambda b, h, i: (0, 0))
    return pl.pallas_call(
        functools.partial(_attn_kernel, lambda_init=lambda_init),
        grid=(B, H, nq),
        in_specs=[
            pl.BlockSpec(memory_space=pltpu.SMEM),
            pl.BlockSpec((tq, dv), lambda b, h, i: (b * nq + i, COL_QB // dv + h)),
            pl.BlockSpec((S, dv), lambda b, h, i: (b, COL_KB // dv + h)),
            pl.BlockSpec((None, S, dv), lambda b, h, i: (h, 0, 0)),
            pl.BlockSpec((S, dv), lambda b, h, i: (b, COL_VB // dv + h)),
            small, small, small, small,
            pl.BlockSpec((1, dv), lambda b, h, i: (0, 0)),
        ],
        out_specs=pl.BlockSpec((tq, dv), lambda b, h, i: (b * nq + i, h)),
        out_shape=jax.ShapeDtypeStruct((B * S, B_WIDTH), BF16),
        scratch_shapes=[
            pltpu.VMEM((2, nq, tq, tq), F32),
            pltpu.VMEM((2, tq, LANES), F32),
            pltpu.VMEM((2, tq, dv + LANES), F32),
            pltpu.VMEM((tq // ATTN_DIAG_GROUPS, tq // ATTN_DIAG_GROUPS), F32),
        ],
        compiler_params=_cparams(("parallel", "parallel", "arbitrary")),
        name="diff_attention",
    )(slopes, z_main, z_main, kbias, z_main, lq1, lk1, lq2, lk2, norm_g)


def _mix_kernel(ha_ref, ob_ref, ga_ref, gb_ref, x_ref, wa_ref, wb_ref, wo_ref, g1_ref, b1_ref, wr_ref, br_ref,
                x1_ref, x1p_ref, ri_ref, rw_ref, cnt_ref, carry_ref):
    i = pl.program_id(0)
    tm = x_ref.shape[0]

    @pl.when(i == 0)
    def _():
        carry_ref[...] = jnp.zeros_like(carry_ref)

    ya = jnp.dot(ha_ref[...], wa_ref[...], preferred_element_type=F32)
    yb = jnp.dot(ob_ref[...], wb_ref[...], preferred_element_type=F32)
    merged = _sigmoid(ga_ref[...].astype(F32)) * ya + _sigmoid(gb_ref[...].astype(F32)) * yb
    mix = jnp.dot(merged.astype(BF16), wo_ref[...], preferred_element_type=F32)
    pre = DEEPNORM_ALPHA * x_ref[...] + mix
    mu = jnp.mean(pre, axis=-1, keepdims=True)
    pc = pre - mu
    var = jnp.mean(pc * pc, axis=-1, keepdims=True)
    x1 = pc * lax.rsqrt(var + LN_EPS) * g1_ref[...] + b1_ref[...]
    x1_ref[...] = x1
    _rows_to_tiles(x1p_ref, _pack_halves(x1))

    logits = _dot_3pass(x1, wr_ref[...]) + br_ref[...]
    lt = logits.T
    row8 = lax.broadcasted_iota(I32, (SUBLANES, tm), 0).astype(F32)
    gl = jnp.where(row8 < N_GROUPS, lt[0:SUBLANES, :], NEG_INF)
    gmax = jnp.max(gl, axis=0, keepdims=True)
    gsel = jnp.min(jnp.where(gl == gmax, row8, float(SUBLANES)), axis=0, keepdims=True)
    gprob = 1.0 / jnp.sum(jnp.exp(gl - gmax), axis=0, keepdims=True)
    ing = lt[SUBLANES:2 * SUBLANES, :]
    for g in range(1, N_GROUPS):
        ing = jnp.where(gsel == g, lt[(g + 1) * SUBLANES:(g + 2) * SUBLANES, :], ing)
    v0 = jnp.max(ing, axis=0, keepdims=True)
    i0 = jnp.min(jnp.where(ing == v0, row8, float(SUBLANES)), axis=0, keepdims=True)
    ing2 = jnp.where(row8 == i0, -jnp.inf, ing)
    v1 = jnp.max(ing2, axis=0, keepdims=True)
    i1 = jnp.min(jnp.where(ing2 == v1, row8, float(SUBLANES)), axis=0, keepdims=True)
    ex = jnp.exp(v1 - v0)
    inv = 1.0 / (1.0 + ex)
    w0 = gprob * inv
    w1 = gprob * (ex * inv)
    e0 = (gsel * EXPERTS_PER_GROUP + i0).astype(I32)
    e1 = (gsel * EXPERTS_PER_GROUP + i1).astype(I32)

    rowe = lax.broadcasted_iota(I32, (N_EXPERTS, tm), 0)
    is0 = rowe == e0
    is1 = rowe == e1
    oh = jnp.where(is0, 1.0, 0.0) + jnp.where(is1, 1.0, 0.0)
    tr = lax.broadcasted_iota(I32, (tm, tm), 0)
    tc = lax.broadcasted_iota(I32, (tm, tm), 1)
    su = jnp.where(tr < tc, 1.0, 0.0).astype(BF16)
    tot = jnp.dot(oh.astype(BF16), su, preferred_element_type=F32) + carry_ref[...]
    rank0 = jnp.sum(jnp.where(is0, tot, 0.0), axis=0, keepdims=True)
    rank1 = jnp.sum(jnp.where(is1, tot, 0.0), axis=0, keepdims=True)
    carry = carry_ref[...] + jnp.sum(oh, axis=1, keepdims=True)
    carry_ref[...] = carry
    cnt_ref[...] = jnp.broadcast_to(carry, cnt_ref.shape).astype(I32)

    zi = jnp.zeros((1, tm), I32)
    ri_ref[...] = jnp.concatenate([e0, e1, rank0.astype(I32), rank1.astype(I32), zi, zi, zi, zi], axis=0)
    zf = jnp.zeros((1, tm), F32)
    rw_ref[...] = jnp.concatenate([w0, w1, zf, zf, zf, zf, zf, zf], axis=0)


def _mix(h_a, o_b, z_main, x2d, w_a, w_b, w_out, ln_g, ln_b, w_r, b_r):
    T, D = x2d.shape
    tm = min(MIX_TM, T)
    const = lambda shape: pl.BlockSpec(shape, lambda i: (0, 0), pipeline_mode=pl.Buffered(1))
    return pl.pallas_call(
        _mix_kernel,
        grid=(T // tm,),
        in_specs=[
            pl.BlockSpec((tm, A_WIDTH), lambda i: (i, 0)),
            pl.BlockSpec((tm, B_WIDTH), lambda i: (i, 0)),
            pl.BlockSpec((tm, D), lambda i: (i, COL_GA // D)),
            pl.BlockSpec((tm, D), lambda i: (i, COL_GB // D)),
            pl.BlockSpec((tm, D), lambda i: (i, 0)),
            const((A_WIDTH, D)), const((B_WIDTH, D)), const((D, D)),
            const((1, D)), const((1, D)), const((D, 2 * LANES)), const((1, LANES)),
        ],
        out_specs=[
            pl.BlockSpec((tm, D), lambda i: (i, 0)),
            pl.BlockSpec((tm * SUBLANES, LANES), lambda i: (i, 0)),
            pl.BlockSpec((SUBLANES, tm), lambda i: (0, i)),
            pl.BlockSpec((SUBLANES, tm), lambda i: (0, i)),
            pl.BlockSpec((N_EXPERTS, LANES), lambda i: (0, 0)),
        ],
        out_shape=[
            jax.ShapeDtypeStruct((T, D), F32),
            jax.ShapeDtypeStruct((T * SUBLANES, LANES), U32),
            jax.ShapeDtypeStruct((SUBLANES, T), I32),
            jax.ShapeDtypeStruct((SUBLANES, T), F32),
            jax.ShapeDtypeStruct((N_EXPERTS, LANES), I32),
        ],
        scratch_shapes=[pltpu.VMEM((N_EXPERTS, 1), F32)],
        compiler_params=_cparams(("arbitrary",)),
        name="mix_ln1_router",
    )(h_a, o_b, z_main, z_main, x2d, w_a, w_b, w_out, ln_g, ln_b, w_r, b_r)


def _row_copy(src_ref, src_row, dst_ref, dst_row, sem):
    return pltpu.make_async_copy(src_ref.at[pl.ds(src_row * SUBLANES, SUBLANES)],
                                 dst_ref.at[pl.ds(dst_row * SUBLANES, SUBLANES)], sem)


def _rows_wait(src_ref, dst_ref, rows, sem):
    n = rows * SUBLANES
    pltpu.make_async_copy(src_ref.at[pl.ds(0, n)], dst_ref.at[pl.ds(0, n)], sem).wait()


DISPATCH_SLOTS = 3


def _dispatch_kernel(cnt_ref, ri_ref, x1_ref, xpad_ref, pstart_ref, bexp_ref, nused_ref,
                     pst_ref, zero_ref, xbuf_ref, sem, lsem, zsem, *, tb, n_blocks, tm):
    i = pl.program_id(0)
    last = pl.num_programs(0) - 1

    @pl.when(i == 0)
    def _():
        def fill(b, carry):
            bexp_ref[b] = 0
            return carry
        lax.fori_loop(0, n_blocks, fill, 0)

        def per_expert(e, blk):
            pst_ref[e] = blk * tb
            pstart_ref[e] = blk * tb
            nb = (cnt_ref[e] + (tb - 1)) // tb

            def mark(b, carry):
                bexp_ref[blk + b] = e
                return carry
            lax.fori_loop(0, nb, mark, 0)
            return blk + nb
        nused = lax.fori_loop(0, N_EXPERTS, per_expert, 0)
        nused_ref[0] = nused
        pst_ref[N_EXPERTS] = nused

    rows = tm * SUBLANES
    slot = i % DISPATCH_SLOTS

    def tile_load(tile, s):
        src = x1_ref.at[pl.ds(pl.multiple_of(tile * rows, rows), rows)]
        return pltpu.make_async_copy(src, xbuf_ref.at[s], lsem.at[s])

    def wait_rows(s):
        for k in range(TOP_K):
            _rows_wait(xbuf_ref.at[s], xpad_ref, tm, sem.at[s])

    @pl.when(i == 0)
    def _():
        tile_load(0, 0).start()

        @pl.when(last >= 1)
        def _():
            tile_load(1, 1).start()

    tile_load(i, slot).wait()
    for t in range(tm):
        for k in range(TOP_K):
            pos = pst_ref[ri_ref[k, t]] + ri_ref[TOP_K + k, t]
            _row_copy(xbuf_ref.at[slot], t, xpad_ref, pos, sem.at[slot]).start(priority=k)

    prev = (i + DISPATCH_SLOTS - 1) % DISPATCH_SLOTS

    @pl.when(i >= 1)
    def _():
        wait_rows(prev)

    @pl.when(i + 2 <= last)
    def _():
        tile_load(i + 2, prev).start()

    @pl.when(i == last)
    def _():
        wait_rows(slot)
        zero_ref[...] = jnp.zeros_like(zero_ref)

        def per_expert(e, carry):
            cnt = cnt_ref[e]
            first = pst_ref[e] + cnt
            npad = ((cnt + (tb - 1)) // tb) * tb - cnt

            def start(p, c2):
                _row_copy(zero_ref, 0, xpad_ref, first + p, zsem).start()
                return c2
            lax.fori_loop(0, npad, start, 0)

            def wait(p, c2):
                _row_copy(zero_ref, 0, xpad_ref, 0, zsem).wait()
                return c2
            lax.fori_loop(0, npad, wait, 0)
            return carry
        lax.fori_loop(0, N_EXPERTS, per_expert, 0)

        def block_copy(b):
            n = tb * SUBLANES
            return pltpu.make_async_copy(zero_ref, xpad_ref.at[pl.ds(pl.multiple_of(b * n, n), n)], zsem)

        def start_block(b, carry):
            block_copy(b).start()
            return carry
        lax.fori_loop(pst_ref[N_EXPERTS], n_blocks, start_block, 0)

        def wait_block(b, carry):
            block_copy(b).wait()
            return carry
        lax.fori_loop(pst_ref[N_EXPERTS], n_blocks, wait_block, 0)


def _dispatch(counts, route_i, x1, tb, n_blocks):
    T = x1.shape[0] // SUBLANES
    tm = min(DISPATCH_TM, T)
    smem_full = pl.BlockSpec(memory_space=pltpu.SMEM)
    return pl.pallas_call(
        functools.partial(_dispatch_kernel, tb=tb, n_blocks=n_blocks, tm=tm),
        grid=(T // tm,),
        in_specs=[
            smem_full,
            pl.BlockSpec((SUBLANES, tm), lambda i: (0, i), memory_space=pltpu.SMEM),
            pl.BlockSpec(memory_space=pl.ANY),
        ],
        out_specs=[
            pl.BlockSpec(memory_space=pl.ANY),
            smem_full, smem_full, smem_full,
        ],
        out_shape=[
            jax.ShapeDtypeStruct((n_blocks * tb * SUBLANES, LANES), x1.dtype),
            jax.ShapeDtypeStruct((N_EXPERTS,), I32),
            jax.ShapeDtypeStruct((n_blocks,), I32),
            jax.ShapeDtypeStruct((1,), I32),
        ],
        scratch_shapes=[
            pltpu.SMEM((N_EXPERTS + 1,), I32),
            pltpu.VMEM((tb * SUBLANES, LANES), x1.dtype),
            pltpu.VMEM((DISPATCH_SLOTS, tm * SUBLANES, LANES), x1.dtype),
            pltpu.SemaphoreType.DMA((DISPATCH_SLOTS,)),
            pltpu.SemaphoreType.DMA((DISPATCH_SLOTS,)),
            pltpu.SemaphoreType.DMA(()),
        ],
        compiler_params=_cparams(("arbitrary",)),
        name="moe_dispatch",
    )(counts, route_i, x1)


def _expert_kernel(bexp_ref, nused_ref, x_ref, wg_hbm, wu_hbm, wd_hbm, y_ref,
                   wgf_ref, wuf_ref, wdf_ref, wgb_ref, wub_ref, wdb_ref, sem, run_ref):
    b = pl.program_id(0)
    nused = nused_ref[0]
    used = b < nused
    e = bexp_ref[b]
    new_expert = jnp.logical_or(b == 0, e != bexp_ref[jnp.maximum(b - 1, 0)])
    streams = ((wg_hbm, wgf_ref), (wu_hbm, wuf_ref), (wd_hbm, wdf_ref))

    def weight_copies(expert, slot):
        return [pltpu.make_async_copy(w_hbm.at[expert], wf_ref.at[slot], sem.at[slot, j])
                for j, (w_hbm, wf_ref) in enumerate(streams)]

    @pl.when(jnp.logical_and(used, b == 0))
    def _():
        run_ref[0] = 0
        for cp in weight_copies(e, 0):
            cp.start()

    @pl.when(jnp.logical_and(used, new_expert))
    def _():
        slot = run_ref[0] % 2
        nxt = lax.while_loop(lambda j: jnp.logical_and(j < nused, bexp_ref[jnp.minimum(j, nused - 1)] == e),
                             lambda j: j + 1, b + 1)

        @pl.when(nxt < nused)
        def _():
            for cp in weight_copies(bexp_ref[jnp.minimum(nxt, nused - 1)], 1 - slot):
                cp.start()

        for cp in weight_copies(e, slot):
            cp.wait()
        wgb_ref[...] = wgf_ref[slot].astype(BF16)
        wub_ref[...] = wuf_ref[slot].astype(BF16)
        wdb_ref[...] = wdf_ref[slot].astype(BF16)
        run_ref[0] = run_ref[0] + 1

    @pl.when(used)
    def _():
        half = SUBLANES * LANES
        x_lo, x_hi = (t.astype(BF16) for t in _unpack_halves(_tiles_to_rows(x_ref)))
        dot = functools.partial(jnp.dot, preferred_element_type=F32)
        gate = dot(x_lo, wgb_ref[:half, :]) + dot(x_hi, wgb_ref[half:, :])
        up = dot(x_lo, wub_ref[:half, :]) + dot(x_hi, wub_ref[half:, :])
        hid = (gate * _sigmoid(gate) * up).astype(BF16)
        _rows_to_tiles(y_ref, _pack_halves(dot(hid, wdb_ref[...])))

    @pl.when(jnp.logical_not(used))
    def _():
        y_ref[...] = jnp.zeros_like(y_ref)


def _experts(bexp, nused, x_pad, w_gate, w_up, w_down, tb):
    P = x_pad.shape[0] // SUBLANES
    D = 2 * TILE_WORDS
    n_blocks = P // tb
    rows = (tb * SUBLANES, LANES)
    F = w_gate.shape[-1]
    blk = lambda b, be, nu: jnp.maximum(jnp.minimum(b, nu[0] - 1), 0)
    hbm = pl.BlockSpec(memory_space=pl.ANY)
    return pl.pallas_call(
        _expert_kernel,
        grid_spec=pltpu.PrefetchScalarGridSpec(
            num_scalar_prefetch=2,
            grid=(n_blocks,),
            in_specs=[pl.BlockSpec(rows, lambda b, be, nu: (blk(b, be, nu), 0)), hbm, hbm, hbm],
            out_specs=pl.BlockSpec(rows, lambda b, be, nu: (b, 0)),
            scratch_shapes=[
                pltpu.VMEM((2, D, F), F32), pltpu.VMEM((2, D, F), F32), pltpu.VMEM((2, F, D), F32),
                pltpu.VMEM((D, F), BF16), pltpu.VMEM((D, F), BF16), pltpu.VMEM((F, D), BF16),
                pltpu.SemaphoreType.DMA((2, 3)),
                pltpu.SMEM((1,), I32),
            ],
        ),
        out_shape=jax.ShapeDtypeStruct(x_pad.shape, U32),
        compiler_params=_cparams(("arbitrary",)),
        name="moe_experts",
    )(bexp, nused, x_pad, w_gate, w_up, w_down)


COMBINE_SLOTS = 3


def _combine_kernel(pstart_ref, ri_ref, r1_ref, r2_ref, x1_ref, rw_ref, g_ref, b_ref, ypad_ref, o_ref, ybuf_ref, sem):
    i = pl.program_id(0)
    n = pl.num_programs(0)
    tm = x1_ref.shape[0]
    slot = i % COMBINE_SLOTS

    def gather_loop(route_ref, s):
        def issue(t, carry):
            for k in range(TOP_K):
                pos = pstart_ref[route_ref[k, t]] + route_ref[TOP_K + k, t]
                _row_copy(ypad_ref, pos, ybuf_ref.at[s, k], t, sem.at[s]).start(priority=k)
            return carry
        lax.fori_loop(0, tm, issue, 0, unroll=8)

    def wait_slot(s):
        for k in range(TOP_K):
            _rows_wait(ypad_ref, ybuf_ref.at[s, k], tm, sem.at[s])

    @pl.when(i == 0)
    def _():
        gather_loop(ri_ref, 0)
        gather_loop(r1_ref, 1)

    wait_slot(slot)

    rw = rw_ref[...]
    y0 = _unpack_halves(_tiles_to_rows(ybuf_ref.at[slot, 0]))
    y1 = _unpack_halves(_tiles_to_rows(ybuf_ref.at[slot, 1]))
    ffn = jnp.concatenate([y0[h] * rw[:, 0:1] + y1[h] * rw[:, 1:2] for h in range(2)], axis=1)
    pre = DEEPNORM_ALPHA * x1_ref[...] + ffn
    mu = jnp.mean(pre, axis=-1, keepdims=True)
    pc = pre - mu
    var = jnp.mean(pc * pc, axis=-1, keepdims=True)
    o_ref[...] = pc * lax.rsqrt(var + LN_EPS) * g_ref[...] + b_ref[...]

    nxt = (i + 2) % COMBINE_SLOTS
    for t in range(tm):
        for k in range(TOP_K):
            pos = pstart_ref[r2_ref[k, t]] + r2_ref[TOP_K + k, t]
            _row_copy(ypad_ref, pos, ybuf_ref.at[nxt, k], t, sem.at[nxt]).start(priority=k)

    @pl.when(i == n - 1)
    def _():
        wait_slot((i + 1) % COMBINE_SLOTS)
        wait_slot(nxt)


def _combine(pstart, route_i, x1, rw_col, ln_g, ln_b, y_pad):
    T, D = x1.shape
    tm = min(COMBINE_TM, T)
    n = T // tm
    return pl.pallas_call(
        _combine_kernel,
        grid=(n,),
        in_specs=[
            pl.BlockSpec(memory_space=pltpu.SMEM),
            pl.BlockSpec((SUBLANES, tm), lambda i: (0, i), memory_space=pltpu.SMEM),
            pl.BlockSpec((SUBLANES, tm), lambda i: (0, jnp.minimum(i + 1, n - 1)), memory_space=pltpu.SMEM),
            pl.BlockSpec((SUBLANES, tm), lambda i: (0, jnp.minimum(i + 2, n - 1)), memory_space=pltpu.SMEM),
            pl.BlockSpec((tm, D), lambda i: (i, 0)),
            pl.BlockSpec((tm, SUBLANES), lambda i: (i, 0)),
            pl.BlockSpec((1, D), lambda i: (0, 0)),
            pl.BlockSpec((1, D), lambda i: (0, 0)),
            pl.BlockSpec(memory_space=pl.ANY),
        ],
        out_specs=pl.BlockSpec((tm, D), lambda i: (i, 0)),
        out_shape=jax.ShapeDtypeStruct((T, D), F32),
        scratch_shapes=[
            pltpu.VMEM((COMBINE_SLOTS, TOP_K, tm * SUBLANES, LANES), U32),
            pltpu.SemaphoreType.DMA((COMBINE_SLOTS,)),
        ],
        compiler_params=_cparams(("arbitrary",)),
        name="moe_combine_ln2",
    )(pstart, route_i, route_i, route_i, x1, rw_col, ln_g, ln_b, y_pad)


N_GATE_COLS = 2 * A_HEADS
SRC_MLSTM = 0
SRC_GATES = 4 * A_WIDTH
SRC_DIFF = SRC_GATES + N_GATE_COLS
SRC_MERGE = SRC_DIFF + 3 * B_WIDTH
PREP_TN = 512


def _wprep_kernel(a_ref, o_ref):
    o_ref[...] = a_ref[...].T.astype(BF16)


def _wgate_kernel(a_ref, o_ref):
    a = a_ref[...]
    rows = jnp.concatenate([a, jnp.zeros((LANES - a.shape[0], a.shape[1]), F32)], axis=0)
    o_ref[...] = rows.T


def _rearrange_in_proj(w_in, b_in):
    K, n_in = w_in.shape
    tn = PREP_TN
    merge_blocks = (2 * D_MODEL) // tn
    mlstm_blocks = (4 * A_WIDTH) // tn
    assert SRC_MERGE % SUBLANES == 0 and SRC_DIFF % SUBLANES == 0

    def src_row(jb):
        s = SUBLANES
        merge = SRC_MERGE // s + (tn // s) * jb
        mlstm = SRC_MLSTM // s + (tn // s) * (jb - merge_blocks)
        diff = SRC_DIFF // s + (tn // s) * (jb - merge_blocks - mlstm_blocks)
        return s * jnp.where(jb < merge_blocks, merge, jnp.where(jb < merge_blocks + mlstm_blocks, mlstm, diff))

    w_t = jnp.swapaxes(w_in, 0, 1)
    w_main = pl.pallas_call(
        _wprep_kernel,
        grid=(N_MAIN // tn,),
        in_specs=[pl.BlockSpec((pl.Element(tn), pl.Element(K)), lambda jb: (src_row(jb), 0))],
        out_specs=pl.BlockSpec((K, tn), lambda jb: (0, jb)),
        out_shape=jax.ShapeDtypeStruct((K, N_MAIN), BF16),
        compiler_params=_cparams(("parallel",)),
        name="w_in_prep",
    )(w_t)
    b_main = jnp.concatenate([b_in[SRC_MERGE:], b_in[SRC_MLSTM:SRC_GATES], b_in[SRC_DIFF:SRC_MERGE]])[None, :]
    w_gate = pl.pallas_call(
        _wgate_kernel,
        grid=(1,),
        in_specs=[pl.BlockSpec((pl.Element(N_GATE_COLS), pl.Element(K)), lambda i: (SRC_GATES, 0))],
        out_specs=pl.BlockSpec((K, LANES), lambda i: (0, 0)),
        out_shape=jax.ShapeDtypeStruct((K, LANES), F32),
        name="w_gate_prep",
    )(w_t)
    b_gate = jnp.pad(b_in[SRC_GATES:SRC_DIFF], (0, LANES - N_GATE_COLS))[None, :]
    return w_main, b_main, _hi_lo_columns(w_gate), b_gate


def _layer(x, w_in, b_in, conv_w, conv_b, norm_a_g, lq1, lk1, lq2, lk2, norm_b_g, w_a, w_b, w_out,
           ln1_g, ln1_b, w_grp, b_grp, w_exp, b_exp, w_gate, w_up, w_down, ln2_g, ln2_b, lambda_init):
    B, S, D = x.shape
    T = B * S
    x2d = x.reshape(T, D)

    w_main, b_main, w_g, b_g = _rearrange_in_proj(w_in, b_in)
    z_main, z_gate = _in_projection(x2d, w_main, b_main, w_g, b_g)

    L = min(MLSTM_CHUNK, S)
    gp = _gate_prep(z_gate[:, :2 * A_HEADS].T, L)
    h_a = _mlstm(z_main, gp.T, gp, conv_w, conv_b[None, :], norm_a_g[None, :], B, S)
    o_b = _diff_attention(z_main, lq1[None, :], lk1[None, :], lq2[None, :], lk2[None, :],
                          norm_b_g[None, :], B, S, lambda_init)

    w_r = jnp.zeros((D, LANES), F32).at[:, :N_GROUPS].set(w_grp).at[:, SUBLANES:SUBLANES + N_EXPERTS].set(w_exp)
    b_r = jnp.zeros((LANES,), F32).at[:N_GROUPS].set(b_grp).at[SUBLANES:SUBLANES + N_EXPERTS].set(b_exp)[None, :]
    x1, x1_packed, route_i, route_w, counts = _mix(h_a, o_b, z_main, x2d, w_a.astype(BF16), w_b.astype(BF16),
                                                   w_out.astype(BF16), ln1_g[None, :], ln1_b[None, :],
                                                   _hi_lo_columns(w_r), b_r)

    tb = MOE_TB
    n_blocks = (T * TOP_K) // tb + N_EXPERTS
    x_pad, pstart, bexp, nused = _dispatch(counts[:, 0], route_i, x1_packed, tb, n_blocks)
    y_pad = _experts(bexp, nused, x_pad, w_gate, w_up, w_down, tb)
    out = _combine(pstart, route_i, x1, route_w.T, ln2_g[None, :], ln2_b[None, :], y_pad)
    return out.reshape(B, S, D)


def kernel(x, w_in, b_in, conv_w, conv_b, mlstm_norm_g, lambda_q1, lambda_k1, lambda_q2, lambda_k2,
           diff_norm_g, w_a, w_b, w_out, ln1_g, ln1_b, w_grp, b_grp, w_exp, b_exp,
           w_gate, w_up, w_down, ln2_g, ln2_b):
    for l in range(DEPTH):
        lambda_init = 0.8 - 0.6 * math.exp(-0.3 * l)
        x = _layer(x, w_in[l], b_in[l], conv_w[l], conv_b[l], mlstm_norm_g[l], lambda_q1[l], lambda_k1[l],
                   lambda_q2[l], lambda_k2[l], diff_norm_g[l], w_a[l], w_b[l], w_out[l], ln1_g[l], ln1_b[l],
                   w_grp[l], b_grp[l], w_exp[l], b_exp[l], w_gate[l], w_up[l], w_down[l], ln2_g[l], ln2_b[l],
                   lambda_init)
    return x
```

```python
import functools
import math

import jax
import jax.numpy as jnp
from jax import lax
from jax.experimental import pallas as pl
from jax.experimental.pallas import tpu as pltpu

F32 = jnp.float32
BF16 = jnp.bfloat16
I32 = jnp.int32
U32 = jnp.uint32

D_MODEL = 2048
A_HEADS = 4
A_HEAD_DIM = 256
A_WIDTH = A_HEADS * A_HEAD_DIM
CONV_WIDTH = 4
B_HEADS = 8
B_HEAD_DIM = 64
B_V_DIM = 2 * B_HEAD_DIM
B_WIDTH = B_HEADS * B_V_DIM
ATTN_CHUNK = 64
N_GROUPS = 4
EXPERTS_PER_GROUP = 8
N_EXPERTS = N_GROUPS * EXPERTS_PER_GROUP
TOP_K = 2
D_EXPERT = 512
DEPTH = 1
DEEPNORM_ALPHA = (2 * DEPTH) ** 0.25
LN_EPS = 1e-5
NEG_INF = float("-inf")
LOG2_E = 1.4426950408889634

LANES = 128
SUBLANES = 8
VMEM_LIMIT_BYTES = 56 * 1024 * 1024

COL_GA = 0
COL_GB = COL_GA + D_MODEL
COL_QA = COL_GB + D_MODEL
COL_KA = COL_QA + A_WIDTH
COL_VA = COL_KA + A_WIDTH
COL_OA = COL_VA + A_WIDTH
COL_QB = COL_OA + A_WIDTH
COL_KB = COL_QB + B_WIDTH
COL_VB = COL_KB + B_WIDTH
N_MAIN = COL_VB + B_WIDTH

PROJ_TM, PROJ_TN = 512, 2816
MLSTM_CHUNK = 512
ATTN_TQ = 1024
ATTN_DIAG_GROUPS = 4
MIX_TM = 512
MOE_TB = 512
DISPATCH_TM = 256
COMBINE_TM = 256


def _cparams(sem, vmem=VMEM_LIMIT_BYTES):
    return pltpu.CompilerParams(dimension_semantics=sem, vmem_limit_bytes=vmem)


def _sigmoid(x):
    return 0.5 * jnp.tanh(0.5 * x) + 0.5


def _pack_halves(x):
    n = x.shape[1] // 2
    lo = lax.bitcast_convert_type(x[:, :n].astype(BF16).astype(F32), U32)
    hi = lax.bitcast_convert_type(x[:, n:].astype(BF16).astype(F32), U32)
    return (lo >> 16) | hi


def _unpack_halves(u):
    lo = lax.bitcast_convert_type(u << 16, F32)
    hi = lax.bitcast_convert_type(u & jnp.uint32(0xFFFF0000), F32)
    return lo, hi


TILE_WORDS = SUBLANES * LANES


def _rows_to_tiles(ref, v):
    rows = v.shape[0]
    for j in range(SUBLANES):
        ref[pl.ds(j, rows, stride=SUBLANES), :] = v[:, j * LANES:(j + 1) * LANES]


def _tiles_to_rows(ref):
    rows = ref.shape[0] // SUBLANES
    return jnp.concatenate([ref[pl.ds(j, rows, stride=SUBLANES), :] for j in range(SUBLANES)], axis=1)


def _split_bf16(a):
    hi = a.astype(BF16)
    return hi, (a - hi.astype(F32)).astype(BF16)


def _hi_lo_columns(w):
    return jnp.concatenate(_split_bf16(w), axis=1)


def _dot_3pass(a, w_hl):
    a_hi, a_lo = _split_bf16(a)
    n = w_hl.shape[1] // 2
    dot = functools.partial(jnp.dot, preferred_element_type=F32)
    r = dot(a_hi, w_hl)
    return r[:, :n] + (r[:, n:] + dot(a_lo, w_hl[:, :n]))


def _proj_kernel(x_ref, w_ref, b_ref, wg_ref, bg_ref, z_ref, zg_ref, xb_ref):
    @pl.when(pl.program_id(1) == 0)
    def _():
        x = x_ref[...]
        xb_ref[...] = x.astype(BF16)
        zg_ref[...] = _dot_3pass(x, wg_ref[...]) + bg_ref[...]

    acc = jnp.dot(xb_ref[...], w_ref[...], preferred_element_type=F32)
    z_ref[...] = (acc + b_ref[...]).astype(z_ref.dtype)


def _in_projection(x2d, w_main, b_main, w_gate, b_gate):
    T, K = x2d.shape
    N = w_main.shape[1]
    tm, tn = min(PROJ_TM, T), PROJ_TN
    return pl.pallas_call(
        _proj_kernel,
        grid=(T // tm, N // tn),
        in_specs=[
            pl.BlockSpec((tm, K), lambda i, j: (i, 0)),
            pl.BlockSpec((K, tn), lambda i, j: (0, j)),
            pl.BlockSpec((1, tn), lambda i, j: (0, j)),
            pl.BlockSpec((K, 2 * LANES), lambda i, j: (0, 0)),
            pl.BlockSpec((1, LANES), lambda i, j: (0, 0)),
        ],
        out_specs=[
            pl.BlockSpec((tm, tn), lambda i, j: (i, j)),
            pl.BlockSpec((tm, LANES), lambda i, j: (i, 0)),
        ],
        out_shape=[
            jax.ShapeDtypeStruct((T, N), BF16),
            jax.ShapeDtypeStruct((T, LANES), F32),
        ],
        scratch_shapes=[pltpu.VMEM((tm, K), BF16)],
        compiler_params=_cparams(("parallel", "arbitrary")),
        name="in_projection",
    )(x2d, w_main, b_main, w_gate, b_gate)


def _gate_prep_kernel(g_ref, o_ref, *, L):
    r = lax.broadcasted_iota(I32, (L, L), 0)
    c = lax.broadcasted_iota(I32, (L, L), 1)
    tri = (r <= c).astype(F32)
    row = lax.broadcasted_iota(I32, (g_ref.shape[0], L), 0)
    for ch in range(g_ref.shape[1] // L):
        g = g_ref[:, ch * L:(ch + 1) * L]
        lf = jnp.minimum(g, 0.0) - jnp.log(1.0 + jnp.exp(-jnp.abs(g)))
        b = jnp.dot(lf, tri, preferred_element_type=F32, precision=lax.Precision.HIGHEST)
        o_ref[:, ch * L:(ch + 1) * L] = jnp.where(row < A_HEADS, g, b)


def _gate_prep(g_rows, L):
    R, T = g_rows.shape
    per_step = math.gcd(T // L, 8)
    W = L * per_step
    return pl.pallas_call(
        functools.partial(_gate_prep_kernel, L=L),
        grid=(T // W,),
        in_specs=[pl.BlockSpec((R, W), lambda i: (0, i))],
        out_specs=pl.BlockSpec((R, W), lambda i: (0, i)),
        out_shape=jax.ShapeDtypeStruct((R, T), F32),
        compiler_params=_cparams(("parallel",)),
        name="gate_prep",
    )(g_rows)


def _mlstm_kernel(q_ref, k_ref, v_ref, o_ref, gcol_ref, grow_ref, cw_ref, cb_ref, ng_ref, out_ref,
                  c_ref, n_ref, m_ref, qcar_ref, kcar_ref):
    ci = pl.program_id(1)
    L = q_ref.shape[0]
    dh = A_HEAD_DIM

    @pl.when(ci == 0)
    def _():
        c_ref[...] = jnp.zeros_like(c_ref)
        n_ref[...] = jnp.zeros_like(n_ref)
        m_ref[...] = jnp.zeros_like(m_ref)
        qcar_ref[...] = jnp.zeros_like(qcar_ref)
        kcar_ref[...] = jnp.zeros_like(kcar_ref)

    def conv_silu(u_ref, car_ref, lo, wcol):
        u = u_ref[:, lo:lo + dh].astype(F32)
        ext = jnp.concatenate([car_ref[:, lo:lo + dh], u], axis=0)
        w = 0.5 * cw_ref[:, wcol:wcol + dh]
        h = 0.5 * cb_ref[:, wcol:wcol + dh] + w[3:4, :] * u
        for j in range(CONV_WIDTH - 1):
            off = SUBLANES - (CONV_WIDTH - 1) + j
            h = h + w[j:j + 1, :] * ext[off:off + L, :]
        car_ref[:, lo:lo + dh] = u[L - SUBLANES:, :]
        return h * jnp.tanh(h) + h

    r = lax.broadcasted_iota(I32, (L, L), 0)
    c = lax.broadcasted_iota(I32, (L, L), 1)
    causal = r >= c
    gcol = gcol_ref[...]
    grow = grow_ref[...]

    for h in range(A_HEADS):
        lo = h * dh
        q = conv_silu(q_ref, qcar_ref, lo, lo)
        k = conv_silu(k_ref, kcar_ref, lo, A_WIDTH + lo) * (dh ** -0.5)
        v_b = v_ref[:, lo:lo + dh]
        q_b = q.astype(BF16)
        k_b = k.astype(BF16)

        i_col, b_col = gcol[:, h:h + 1], gcol[:, A_HEADS + h:A_HEADS + h + 1]
        i_row, b_row = grow[h:h + 1, :], grow[A_HEADS + h:A_HEADS + h + 1, :]
        b_last = b_row[:, L - 1:L]
        m_prev = m_ref[h]

        dmat = jnp.where(causal, b_col - b_row + i_row, NEG_INF)
        inter = b_col + m_prev
        m_t = jnp.maximum(inter, jnp.max(dmat, axis=-1, keepdims=True))
        s = lax.dot_general(q_b, k_b, (((1,), (1,)), ((), ())), preferred_element_type=F32)
        w_intra = jnp.exp(dmat - m_t) * s
        w_inter = jnp.exp(inter - m_t)
        qc = jnp.dot(q_b, c_ref[h].astype(BF16), preferred_element_type=F32)
        num = w_inter * qc + jnp.dot(w_intra.astype(BF16), v_b, preferred_element_type=F32)
        qn = jnp.sum(q * n_ref[h], axis=-1, keepdims=True)
        den = w_inter * qn + jnp.sum(w_intra, axis=-1, keepdims=True)
        hh = num / jnp.maximum(jnp.abs(den), jnp.exp(-m_t))

        g_col = b_last - b_col + i_col
        m_new = jnp.maximum(b_last + m_prev, jnp.max(g_col, axis=0, keepdims=True))
        decay = jnp.exp(b_last + m_prev - m_new)
        kw = jnp.exp(g_col - m_new) * k
        c_ref[h] = decay * c_ref[h] + jnp.dot(kw.T.astype(BF16), v_b, preferred_element_type=F32)
        n_ref[h] = decay * n_ref[h] + jnp.sum(kw, axis=0, keepdims=True)
        m_ref[h] = m_new

        y = _sigmoid(o_ref[:, lo:lo + dh].astype(F32)) * hh
        mu = jnp.mean(y, axis=-1, keepdims=True)
        yc = y - mu
        var = jnp.mean(yc * yc, axis=-1, keepdims=True)
        out_ref[:, lo:lo + dh] = (yc * lax.rsqrt(var + LN_EPS) * ng_ref[:, lo:lo + dh]).astype(out_ref.dtype)


def _mlstm(z_main, gcol, grow, conv_w, conv_b, norm_g, B, S):
    L = min(MLSTM_CHUNK, S)
    nc = S // L
    dh = A_HEAD_DIM
    H = A_HEADS
    W = A_WIDTH
    row = lambda b, c: b * nc + c
    full = lambda shape: pl.BlockSpec(shape, lambda b, c: (0, 0))
    return pl.pallas_call(
        _mlstm_kernel,
        grid=(B, nc),
        in_specs=[
            pl.BlockSpec((L, W), lambda b, c: (row(b, c), COL_QA // W)),
            pl.BlockSpec((L, W), lambda b, c: (row(b, c), COL_KA // W)),
            pl.BlockSpec((L, W), lambda b, c: (row(b, c), COL_VA // W)),
            pl.BlockSpec((L, W), lambda b, c: (row(b, c), COL_OA // W)),
            pl.BlockSpec((L, SUBLANES), lambda b, c: (row(b, c), 0)),
            pl.BlockSpec((SUBLANES, L), lambda b, c: (0, row(b, c))),
            full((CONV_WIDTH, 2 * W)), full((1, 2 * W)), full((1, W)),
        ],
        out_specs=pl.BlockSpec((L, W), lambda b, c: (row(b, c), 0)),
        out_shape=jax.ShapeDtypeStruct((B * S, W), BF16),
        scratch_shapes=[
            pltpu.VMEM((H, dh, dh), F32),
            pltpu.VMEM((H, 1, dh), F32),
            pltpu.VMEM((H, 1, 1), F32),
            pltpu.VMEM((SUBLANES, W), F32),
            pltpu.VMEM((SUBLANES, W), F32),
        ],
        compiler_params=_cparams(("parallel", "arbitrary")),
        name="mlstm",
    )(z_main, z_main, z_main, z_main, gcol, grow, conv_w, conv_b, norm_g)


def _lane_tiles(t):
    return [t[:, i * LANES:(i + 1) * LANES] for i in range(t.shape[1] // LANES)]


def _attn_kernel(slope_ref, q_ref, k_ref, kb_ref, v_ref, lq1_ref, lk1_ref, lq2_ref, lk2_ref, g_ref, o_ref,
                 s_ref, mx_ref, acc_ref, corr_ref, *, lambda_init):
    h = pl.program_id(1)
    qi = pl.program_id(2)
    tq = q_ref.shape[0]
    tk = tq
    slope = slope_ref[h]
    nt = (((1,), (1,)), ((), ()))

    lane = lax.broadcasted_iota(I32, (tq, B_V_DIM), 1)
    q = q_ref[...] * (B_HEAD_DIM ** -0.5)
    zero = jnp.zeros_like(q)
    bias_cols = jnp.where(lane < 2, 1.0, 0.0).astype(BF16)
    qa = (jnp.concatenate([jnp.where(lane < B_HEAD_DIM, q, zero), bias_cols], axis=1),
          jnp.concatenate([jnp.where(lane >= B_HEAD_DIM, q, zero), bias_cols], axis=1))

    def scores(j):
        start = pl.multiple_of(j * tk, tk)
        ka = jnp.concatenate([k_ref[pl.ds(start, tk), :], kb_ref[pl.ds(start, tk), :]], axis=1)
        return [lax.dot_general(qa[n], ka, nt, preferred_element_type=F32) for n in range(2)]

    def keep(j, n, t, rows=slice(None)):
        t = t * LOG2_E
        s_ref[n, j, rows, :t.shape[1]] = t
        mx_ref[n, rows, :] = functools.reduce(jnp.maximum, _lane_tiles(t), mx_ref[n, rows, :])

    mx_ref[...] = jnp.full(mx_ref.shape, NEG_INF, F32)

    def pass_a(j):
        t = scores(j)
        for n in range(2):
            keep(j, n, t[n])

    def pass_a_pair(jj, carry):
        pass_a(2 * jj)
        pass_a(2 * jj + 1)
        return carry
    lax.fori_loop(0, qi // 2, pass_a_pair, 0)

    @pl.when(qi % 2 == 1)
    def _():
        pass_a(qi - 1)

    rg = tq // ATTN_DIAG_GROUPS

    @pl.when(qi == 0)
    def _():
        r = lax.broadcasted_iota(I32, (rg, rg), 0)
        c = lax.broadcasted_iota(I32, (rg, rg), 1)
        ahead = jnp.where(c <= r, 0.0, (r - c).astype(F32) * (2.0 * slope))
        corr_ref[...] = jnp.where((c // ATTN_CHUNK) <= (r // ATTN_CHUNK), ahead, NEG_INF)

    dstart = pl.multiple_of(qi * tk, tk)
    ka = jnp.concatenate([k_ref[pl.ds(dstart, tk), :], kb_ref[pl.ds(dstart, tk), :]], axis=1)
    for n in range(2):
        for g in range(ATTN_DIAG_GROUPS):
            kbeg, kend = g * rg, (g + 1) * rg
            band = lax.dot_general(qa[n][kbeg:kend], ka[:kend], nt, preferred_element_type=F32)
            square = band[:, kbeg:] + corr_ref[...]
            band = square if g == 0 else jnp.concatenate([band[:, :kbeg], square], axis=1)
            keep(qi, n, band, rows=slice(kbeg, kend))

    for n in range(2):
        m = jnp.max(mx_ref[n], axis=-1, keepdims=True)
        mx_ref[n] = jnp.broadcast_to(m, (tq, LANES))
    acc_ref[...] = jnp.zeros_like(acc_ref)

    def weigh(n, rows, s_tile, v_tile):
        mb = mx_ref[n, rows, :]
        p = jnp.concatenate([jnp.exp2(t - mb) for t in _lane_tiles(s_tile)], axis=1).astype(BF16)
        v_ones = jnp.concatenate([v_tile, jnp.ones((v_tile.shape[0], LANES), BF16)], axis=1)
        acc_ref[n, rows, :] += jnp.dot(p, v_ones, preferred_element_type=F32)

    def pass_b(j, carry):
        start = pl.multiple_of(j * tk, tk)
        vj = v_ref[pl.ds(start, tk), :]
        for n in range(2):
            weigh(n, slice(None), s_ref[n, j], vj)
        return carry
    lax.fori_loop(0, qi, pass_b, 0)

    for n in range(2):
        for g in range(ATTN_DIAG_GROUPS):
            kend = (g + 1) * rg
            weigh(n, slice(g * rg, kend), s_ref[n, qi, g * rg:kend, :kend], v_ref[pl.ds(dstart, kend), :])

    lam = (jnp.exp(jnp.sum(lq1_ref[...] * lk1_ref[...], axis=-1, keepdims=True))
           - jnp.exp(jnp.sum(lq2_ref[...] * lk2_ref[...], axis=-1, keepdims=True)) + lambda_init)
    dv = B_V_DIM
    a0, a1 = acc_ref[0], acc_ref[1]
    o = a0[:, :dv] / a0[:, dv:] - lam * (a1[:, :dv] / a1[:, dv:])
    ms = jnp.mean(o * o, axis=-1, keepdims=True)
    o_ref[...] = (o * lax.rsqrt(ms + LN_EPS) * g_ref[...] * (1.0 - lambda_init)).astype(o_ref.dtype)


def _diff_attention(z_main, lq1, lk1, lq2, lk2, norm_g, B, S, lambda_init):
    tq = min(ATTN_TQ, S)
    nq = S // tq
    H = B_HEADS
    dv = B_V_DIM
    slopes = 2.0 ** (-8.0 * jnp.arange(1, H + 1, dtype=F32) / H)
    assert S <= 256 * 256
    pos = lax.broadcasted_iota(I32, (H, S, dv), 1)
    col = lax.broadcasted_iota(I32, (H, S, dv), 2)
    within = pos % 256
    kbias = jnp.where(col == 0, within, jnp.where(col == 1, pos - within, 0)).astype(F32)
    kbias = (kbias * slopes[:, None, None]).astype(BF16)
    small = pl.BlockSpec((1, B_HEAD_DIM), lambda b, h, i: (0, 0))
    return pl.pallas_call(
        functools.partial(_attn_kernel, lambda_init=lambda_init),
        grid=(B, H, nq),
        in_specs=[
            pl.BlockSpec(memory_space=pltpu.SMEM),
            pl.BlockSpec((tq, dv), lambda b, h, i: (b * nq + i, COL_QB // dv + h)),
            pl.BlockSpec((S, dv), lambda b, h, i: (b, COL_KB // dv + h)),
            pl.BlockSpec((None, S, dv), lambda b, h, i: (h, 0, 0)),
            pl.BlockSpec((S, dv), lambda b, h, i: (b, COL_VB // dv + h)),
            small, small, small, small,
            pl.BlockSpec((1, dv), lambda b, h, i: (0, 0)),
        ],
        out_specs=pl.BlockSpec((tq, dv), lambda b, h, i: (b * nq + i, h)),
        out_shape=jax.ShapeDtypeStruct((B * S, B_WIDTH), BF16),
        scratch_shapes=[
            pltpu.VMEM((2, nq, tq, tq), F32),
            pltpu.VMEM((2, tq, LANES), F32),
            pltpu.VMEM((2, tq, dv + LANES), F32),
            pltpu.VMEM((tq // ATTN_DIAG_GROUPS, tq // ATTN_DIAG_GROUPS), F32),
        ],
        compiler_params=_cparams(("parallel", "parallel", "arbitrary")),
        name="diff_attention",
    )(slopes, z_main, z_main, kbias, z_main, lq1, lk1, lq2, lk2, norm_g)


def _mix_kernel(ha_ref, ob_ref, ga_ref, gb_ref, x_ref, wa_ref, wb_ref, wo_ref, g1_ref, b1_ref, wr_ref, br_ref,
                x1_ref, x1p_ref, ri_ref, rw_ref, cnt_ref, carry_ref):
    i = pl.program_id(0)
    tm = x_ref.shape[0]

    @pl.when(i == 0)
    def _():
        carry_ref[...] = jnp.zeros_like(carry_ref)

    ya = jnp.dot(ha_ref[...], wa_ref[...], preferred_element_type=F32)
    yb = jnp.dot(ob_ref[...], wb_ref[...], preferred_element_type=F32)
    merged = _sigmoid(ga_ref[...].astype(F32)) * ya + _sigmoid(gb_ref[...].astype(F32)) * yb
    mix = jnp.dot(merged.astype(BF16), wo_ref[...], preferred_element_type=F32)
    pre = DEEPNORM_ALPHA * x_ref[...] + mix
    mu = jnp.mean(pre, axis=-1, keepdims=True)
    pc = pre - mu
    var = jnp.mean(pc * pc, axis=-1, keepdims=True)
    x1 = pc * lax.rsqrt(var + LN_EPS) * g1_ref[...] + b1_ref[...]
    x1_ref[...] = x1
    _rows_to_tiles(x1p_ref, _pack_halves(x1))

    logits = _dot_3pass(x1, wr_ref[...]) + br_ref[...]
    lt = logits.T
    row8 = lax.broadcasted_iota(I32, (SUBLANES, tm), 0).astype(F32)
    gl = jnp.where(row8 < N_GROUPS, lt[0:SUBLANES, :], NEG_INF)
    gmax = jnp.max(gl, axis=0, keepdims=True)
    gsel = jnp.min(jnp.where(gl == gmax, row8, float(SUBLANES)), axis=0, keepdims=True)
    gprob = 1.0 / jnp.sum(jnp.exp(gl - gmax), axis=0, keepdims=True)
    ing = lt[SUBLANES:2 * SUBLANES, :]
    for g in range(1, N_GROUPS):
        ing = jnp.where(gsel == g, lt[(g + 1) * SUBLANES:(g + 2) * SUBLANES, :], ing)
    v0 = jnp.max(ing, axis=0, keepdims=True)
    i0 = jnp.min(jnp.where(ing == v0, row8, float(SUBLANES)), axis=0, keepdims=True)
    ing2 = jnp.where(row8 == i0, -jnp.inf, ing)
    v1 = jnp.max(ing2, axis=0, keepdims=True)
    i1 = jnp.min(jnp.where(ing2 == v1, row8, float(SUBLANES)), axis=0, keepdims=True)
    ex = jnp.exp(v1 - v0)
    inv = 1.0 / (1.0 + ex)
    w0 = gprob * inv
    w1 = gprob * (ex * inv)
    e0 = (gsel * EXPERTS_PER_GROUP + i0).astype(I32)
    e1 = (gsel * EXPERTS_PER_GROUP + i1).astype(I32)

    rowe = lax.broadcasted_iota(I32, (N_EXPERTS, tm), 0)
    is0 = rowe == e0
    is1 = rowe == e1
    oh = jnp.where(is0, 1.0, 0.0) + jnp.where(is1, 1.0, 0.0)
    tr = lax.broadcasted_iota(I32, (tm, tm), 0)
    tc = lax.broadcasted_iota(I32, (tm, tm), 1)
    su = jnp.where(tr < tc, 1.0, 0.0).astype(BF16)
    tot = jnp.dot(oh.astype(BF16), su, preferred_element_type=F32) + carry_ref[...]
    rank0 = jnp.sum(jnp.where(is0, tot, 0.0), axis=0, keepdims=True)
    rank1 = jnp.sum(jnp.where(is1, tot, 0.0), axis=0, keepdims=True)
    carry = carry_ref[...] + jnp.sum(oh, axis=1, keepdims=True)
    carry_ref[...] = carry
    cnt_ref[...] = jnp.broadcast_to(carry, cnt_ref.shape).astype(I32)

    zi = jnp.zeros((1, tm), I32)
    ri_ref[...] = jnp.concatenate([e0, e1, rank0.astype(I32), rank1.astype(I32), zi, zi, zi, zi], axis=0)
    zf = jnp.zeros((1, tm), F32)
    rw_ref[...] = jnp.concatenate([w0, w1, zf, zf, zf, zf, zf, zf], axis=0)


def _mix(h_a, o_b, z_main, x2d, w_a, w_b, w_out, ln_g, ln_b, w_r, b_r):
    T, D = x2d.shape
    tm = min(MIX_TM, T)
    const = lambda shape: pl.BlockSpec(shape, lambda i: (0, 0), pipeline_mode=pl.Buffered(1))
    return pl.pallas_call(
        _mix_kernel,
        grid=(T // tm,),
        in_specs=[
            pl.BlockSpec((tm, A_WIDTH), lambda i: (i, 0)),
            pl.BlockSpec((tm, B_WIDTH), lambda i: (i, 0)),
            pl.BlockSpec((tm, D), lambda i: (i, COL_GA // D)),
            pl.BlockSpec((tm, D), lambda i: (i, COL_GB // D)),
            pl.BlockSpec((tm, D), lambda i: (i, 0)),
            const((A_WIDTH, D)), const((B_WIDTH, D)), const((D, D)),
            const((1, D)), const((1, D)), const((D, 2 * LANES)), const((1, LANES)),
        ],
        out_specs=[
            pl.BlockSpec((tm, D), lambda i: (i, 0)),
            pl.BlockSpec((tm * SUBLANES, LANES), lambda i: (i, 0)),
            pl.BlockSpec((SUBLANES, tm), lambda i: (0, i)),
            pl.BlockSpec((SUBLANES, tm), lambda i: (0, i)),
            pl.BlockSpec((N_EXPERTS, LANES), lambda i: (0, 0)),
        ],
        out_shape=[
            jax.ShapeDtypeStruct((T, D), F32),
            jax.ShapeDtypeStruct((T * SUBLANES, LANES), U32),
            jax.ShapeDtypeStruct((SUBLANES, T), I32),
            jax.ShapeDtypeStruct((SUBLANES, T), F32),
            jax.ShapeDtypeStruct((N_EXPERTS, LANES), I32),
        ],
        scratch_shapes=[pltpu.VMEM((N_EXPERTS, 1), F32)],
        compiler_params=_cparams(("arbitrary",)),
        name="mix_ln1_router",
    )(h_a, o_b, z_main, z_main, x2d, w_a, w_b, w_out, ln_g, ln_b, w_r, b_r)


def _row_copy(src_ref, src_row, dst_ref, dst_row, sem):
    return pltpu.make_async_copy(src_ref.at[pl.ds(src_row * SUBLANES, SUBLANES)],
                                 dst_ref.at[pl.ds(dst_row * SUBLANES, SUBLANES)], sem)


def _rows_wait(src_ref, dst_ref, rows, sem):
    n = rows * SUBLANES
    pltpu.make_async_copy(src_ref.at[pl.ds(0, n)], dst_ref.at[pl.ds(0, n)], sem).wait()


DISPATCH_SLOTS = 3


def _dispatch_kernel(cnt_ref, ri_ref, x1_ref, xpad_ref, pstart_ref, bexp_ref, nused_ref,
                     pst_ref, zero_ref, xbuf_ref, sem, lsem, zsem, *, tb, n_blocks, tm):
    i = pl.program_id(0)
    last = pl.num_programs(0) - 1

    @pl.when(i == 0)
    def _():
        def fill(b, carry):
            bexp_ref[b] = 0
            return carry
        lax.fori_loop(0, n_blocks, fill, 0)

        def per_expert(e, blk):
            pst_ref[e] = blk * tb
            pstart_ref[e] = blk * tb
            nb = (cnt_ref[e] + (tb - 1)) // tb

            def mark(b, carry):
                bexp_ref[blk + b] = e
                return carry
            lax.fori_loop(0, nb, mark, 0)
            return blk + nb
        nused = lax.fori_loop(0, N_EXPERTS, per_expert, 0)
        nused_ref[0] = nused
        pst_ref[N_EXPERTS] = nused

    rows = tm * SUBLANES
    slot = i % DISPATCH_SLOTS

    def tile_load(tile, s):
        src = x1_ref.at[pl.ds(pl.multiple_of(tile * rows, rows), rows)]
        return pltpu.make_async_copy(src, xbuf_ref.at[s], lsem.at[s])

    def wait_rows(s):
        for k in range(TOP_K):
            _rows_wait(xbuf_ref.at[s], xpad_ref, tm, sem.at[s])

    @pl.when(i == 0)
    def _():
        tile_load(0, 0).start()

        @pl.when(last >= 1)
        def _():
            tile_load(1, 1).start()

    tile_load(i, slot).wait()
    for t in range(tm):
        for k in range(TOP_K):
            pos = pst_ref[ri_ref[k, t]] + ri_ref[TOP_K + k, t]
            _row_copy(xbuf_ref.at[slot], t, xpad_ref, pos, sem.at[slot]).start(priority=k)

    prev = (i + DISPATCH_SLOTS - 1) % DISPATCH_SLOTS

    @pl.when(i >= 1)
    def _():
        wait_rows(prev)

    @pl.when(i + 2 <= last)
    def _():
        tile_load(i + 2, prev).start()

    @pl.when(i == last)
    def _():
        wait_rows(slot)
        zero_ref[...] = jnp.zeros_like(zero_ref)

        def per_expert(e, carry):
            cnt = cnt_ref[e]
            first = pst_ref[e] + cnt
            npad = ((cnt + (tb - 1)) // tb) * tb - cnt

            def start(p, c2):
                _row_copy(zero_ref, 0, xpad_ref, first + p, zsem).start()
                return c2
            lax.fori_loop(0, npad, start, 0)

            def wait(p, c2):
                _row_copy(zero_ref, 0, xpad_ref, 0, zsem).wait()
                return c2
            lax.fori_loop(0, npad, wait, 0)
            return carry
        lax.fori_loop(0, N_EXPERTS, per_expert, 0)

        def block_copy(b):
            n = tb * SUBLANES
            return pltpu.make_async_copy(zero_ref, xpad_ref.at[pl.ds(pl.multiple_of(b * n, n), n)], zsem)

        def start_block(b, carry):
            block_copy(b).start()
            return carry
        lax.fori_loop(pst_ref[N_EXPERTS], n_blocks, start_block, 0)

        def wait_block(b, carry):
            block_copy(b).wait()
            return carry
        lax.fori_loop(pst_ref[N_EXPERTS], n_blocks, wait_block, 0)


def _dispatch(counts, route_i, x1, tb, n_blocks):
    T = x1.shape[0] // SUBLANES
    tm = min(DISPATCH_TM, T)
    smem_full = pl.BlockSpec(memory_space=pltpu.SMEM)
    return pl.pallas_call(
        functools.partial(_dispatch_kernel, tb=tb, n_blocks=n_blocks, tm=tm),
        grid=(T // tm,),
        in_specs=[
            smem_full,
            pl.BlockSpec((SUBLANES, tm), lambda i: (0, i), memory_space=pltpu.SMEM),
            pl.BlockSpec(memory_space=pl.ANY),
        ],
        out_specs=[
            pl.BlockSpec(memory_space=pl.ANY),
            smem_full, smem_full, smem_full,
        ],
        out_shape=[
            jax.ShapeDtypeStruct((n_blocks * tb * SUBLANES, LANES), x1.dtype),
            jax.ShapeDtypeStruct((N_EXPERTS,), I32),
            jax.ShapeDtypeStruct((n_blocks,), I32),
            jax.ShapeDtypeStruct((1,), I32),
        ],
        scratch_shapes=[
            pltpu.SMEM((N_EXPERTS + 1,), I32),
            pltpu.VMEM((tb * SUBLANES, LANES), x1.dtype),
            pltpu.VMEM((DISPATCH_SLOTS, tm * SUBLANES, LANES), x1.dtype),
            pltpu.SemaphoreType.DMA((DISPATCH_SLOTS,)),
            pltpu.SemaphoreType.DMA((DISPATCH_SLOTS,)),
            pltpu.SemaphoreType.DMA(()),
        ],
        compiler_params=_cparams(("arbitrary",)),
        name="moe_dispatch",
    )(counts, route_i, x1)


def _expert_kernel(bexp_ref, nused_ref, x_ref, wg_hbm, wu_hbm, wd_hbm, y_ref,
                   wgf_ref, wuf_ref, wdf_ref, wgb_ref, wub_ref, wdb_ref, sem, run_ref):
    b = pl.program_id(0)
    nused = nused_ref[0]
    used = b < nused
    e = bexp_ref[b]
    new_expert = jnp.logical_or(b == 0, e != bexp_ref[jnp.maximum(b - 1, 0)])
    streams = ((wg_hbm, wgf_ref), (wu_hbm, wuf_ref), (wd_hbm, wdf_ref))

    def weight_copies(expert, slot):
        return [pltpu.make_async_copy(w_hbm.at[expert], wf_ref.at[slot], sem.at[slot, j])
                for j, (w_hbm, wf_ref) in enumerate(streams)]

    @pl.when(jnp.logical_and(used, b == 0))
    def _():
        run_ref[0] = 0
        for cp in weight_copies(e, 0):
            cp.start()

    @pl.when(jnp.logical_and(used, new_expert))
    def _():
        slot = run_ref[0] % 2
        nxt = lax.while_loop(lambda j: jnp.logical_and(j < nused, bexp_ref[jnp.minimum(j, nused - 1)] == e),
                             lambda j: j + 1, b + 1)

        @pl.when(nxt < nused)
        def _():
            for cp in weight_copies(bexp_ref[jnp.minimum(nxt, nused - 1)], 1 - slot):
                cp.start()

        for cp in weight_copies(e, slot):
            cp.wait()
        wgb_ref[...] = wgf_ref[slot].astype(BF16)
        wub_ref[...] = wuf_ref[slot].astype(BF16)
        wdb_ref[...] = wdf_ref[slot].astype(BF16)
        run_ref[0] = run_ref[0] + 1

    @pl.when(used)
    def _():
        half = SUBLANES * LANES
        x_lo, x_hi = (t.astype(BF16) for t in _unpack_halves(_tiles_to_rows(x_ref)))
        dot = functools.partial(jnp.dot, preferred_element_type=F32)
        gate = dot(x_lo, wgb_ref[:half, :]) + dot(x_hi, wgb_ref[half:, :])
        up = dot(x_lo, wub_ref[:half, :]) + dot(x_hi, wub_ref[half:, :])
        hid = (gate * _sigmoid(gate) * up).astype(BF16)
        _rows_to_tiles(y_ref, _pack_halves(dot(hid, wdb_ref[...])))

    @pl.when(jnp.logical_not(used))
    def _():
        y_ref[...] = jnp.zeros_like(y_ref)


def _experts(bexp, nused, x_pad, w_gate, w_up, w_down, tb):
    P = x_pad.shape[0] // SUBLANES
    D = 2 * TILE_WORDS
    n_blocks = P // tb
    rows = (tb * SUBLANES, LANES)
    F = w_gate.shape[-1]
    blk = lambda b, be, nu: jnp.maximum(jnp.minimum(b, nu[0] - 1), 0)
    hbm = pl.BlockSpec(memory_space=pl.ANY)
    return pl.pallas_call(
        _expert_kernel,
        grid_spec=pltpu.PrefetchScalarGridSpec(
            num_scalar_prefetch=2,
            grid=(n_blocks,),
            in_specs=[pl.BlockSpec(rows, lambda b, be, nu: (blk(b, be, nu), 0)), hbm, hbm, hbm],
            out_specs=pl.BlockSpec(rows, lambda b, be, nu: (b, 0)),
            scratch_shapes=[
                pltpu.VMEM((2, D, F), F32), pltpu.VMEM((2, D, F), F32), pltpu.VMEM((2, F, D), F32),
                pltpu.VMEM((D, F), BF16), pltpu.VMEM((D, F), BF16), pltpu.VMEM((F, D), BF16),
                pltpu.SemaphoreType.DMA((2, 3)),
                pltpu.SMEM((1,), I32),
            ],
        ),
        out_shape=jax.ShapeDtypeStruct(x_pad.shape, U32),
        compiler_params=_cparams(("arbitrary",)),
        name="moe_experts",
    )(bexp, nused, x_pad, w_gate, w_up, w_down)


COMBINE_SLOTS = 3


def _combine_kernel(pstart_ref, ri_ref, r1_ref, r2_ref, x1_ref, rw_ref, g_ref, b_ref, ypad_ref, o_ref, ybuf_ref, sem):
    i = pl.program_id(0)
    n = pl.num_programs(0)
    tm = x1_ref.shape[0]
    slot = i % COMBINE_SLOTS

    def gather_loop(route_ref, s):
        def issue(t, carry):
            for k in range(TOP_K):
                pos = pstart_ref[route_ref[k, t]] + route_ref[TOP_K + k, t]
                _row_copy(ypad_ref, pos, ybuf_ref.at[s, k], t, sem.at[s]).start(priority=k)
            return carry
        lax.fori_loop(0, tm, issue, 0, unroll=8)

    def wait_slot(s):
        for k in range(TOP_K):
            _rows_wait(ypad_ref, ybuf_ref.at[s, k], tm, sem.at[s])

    @pl.when(i == 0)
    def _():
        gather_loop(ri_ref, 0)
        gather_loop(r1_ref, 1)

    wait_slot(slot)

    rw = rw_ref[...]
    y0 = _unpack_halves(_tiles_to_rows(ybuf_ref.at[slot, 0]))
    y1 = _unpack_halves(_tiles_to_rows(ybuf_ref.at[slot, 1]))
    ffn = jnp.concatenate([y0[h] * rw[:, 0:1] + y1[h] * rw[:, 1:2] for h in range(2)], axis=1)
    pre = DEEPNORM_ALPHA * x1_ref[...] + ffn
    mu = jnp.mean(pre, axis=-1, keepdims=True)
    pc = pre - mu
    var = jnp.mean(pc * pc, axis=-1, keepdims=True)
    o_ref[...] = pc * lax.rsqrt(var + LN_EPS) * g_ref[...] + b_ref[...]

    nxt = (i + 2) % COMBINE_SLOTS
    for t in range(tm):
        for k in range(TOP_K):
            pos = pstart_ref[r2_ref[k, t]] + r2_ref[TOP_K + k, t]
            _row_copy(ypad_ref, pos, ybuf_ref.at[nxt, k], t, sem.at[nxt]).start(priority=k)

    @pl.when(i == n - 1)
    def _():
        wait_slot((i + 1) % COMBINE_SLOTS)
        wait_slot(nxt)


def _combine(pstart, route_i, x1, rw_col, ln_g, ln_b, y_pad):
    T, D = x1.shape
    tm = min(COMBINE_TM, T)
    n = T // tm
    return pl.pallas_call(
        _combine_kernel,
        grid=(n,),
        in_specs=[
            pl.BlockSpec(memory_space=pltpu.SMEM),
            pl.BlockSpec((SUBLANES, tm), lambda i: (0, i), memory_space=pltpu.SMEM),
            pl.BlockSpec((SUBLANES, tm), lambda i: (0, jnp.minimum(i + 1, n - 1)), memory_space=pltpu.SMEM),
            pl.BlockSpec((SUBLANES, tm), lambda i: (0, jnp.minimum(i + 2, n - 1)), memory_space=pltpu.SMEM),
            pl.BlockSpec((tm, D), lambda i: (i, 0)),
            pl.BlockSpec((tm, SUBLANES), lambda i: (i, 0)),
            pl.BlockSpec((1, D), lambda i: (0, 0)),
            pl.BlockSpec((1, D), lambda i: (0, 0)),
            pl.BlockSpec(memory_space=pl.ANY),
        ],
        out_specs=pl.BlockSpec((tm, D), lambda i: (i, 0)),
        out_shape=jax.ShapeDtypeStruct((T, D), F32),
        scratch_shapes=[
            pltpu.VMEM((COMBINE_SLOTS, TOP_K, tm * SUBLANES, LANES), U32),
            pltpu.SemaphoreType.DMA((COMBINE_SLOTS,)),
        ],
        compiler_params=_cparams(("arbitrary",)),
        name="moe_combine_ln2",
    )(pstart, route_i, route_i, route_i, x1, rw_col, ln_g, ln_b, y_pad)


N_GATE_COLS = 2 * A_HEADS
SRC_MLSTM = 0
SRC_GATES = 4 * A_WIDTH
SRC_DIFF = SRC_GATES + N_GATE_COLS
SRC_MERGE = SRC_DIFF + 3 * B_WIDTH
PREP_TN = 512


def _wprep_kernel(a_ref, o_ref):
    o_ref[...] = a_ref[...].T.astype(BF16)


def _wgate_kernel(a_ref, o_ref):
    a = a_ref[...]
    rows = jnp.concatenate([a, jnp.zeros((LANES - a.shape[0], a.shape[1]), F32)], axis=0)
    o_ref[...] = rows.T


def _rearrange_in_proj(w_in, b_in):
    K, n_in = w_in.shape
    tn = PREP_TN
    merge_blocks = (2 * D_MODEL) // tn
    mlstm_blocks = (4 * A_WIDTH) // tn
    assert SRC_MERGE % SUBLANES == 0 and SRC_DIFF % SUBLANES == 0

    def src_row(jb):
        s = SUBLANES
        merge = SRC_MERGE // s + (tn // s) * jb
        mlstm = SRC_MLSTM // s + (tn // s) * (jb - merge_blocks)
        diff = SRC_DIFF // s + (tn // s) * (jb - merge_blocks - mlstm_blocks)
        return s * jnp.where(jb < merge_blocks, merge, jnp.where(jb < merge_blocks + mlstm_blocks, mlstm, diff))

    w_t = jnp.swapaxes(w_in, 0, 1)
    w_main = pl.pallas_call(
        _wprep_kernel,
        grid=(N_MAIN // tn,),
        in_specs=[pl.BlockSpec((pl.Element(tn), pl.Element(K)), lambda jb: (src_row(jb), 0))],
        out_specs=pl.BlockSpec((K, tn), lambda jb: (0, jb)),
        out_shape=jax.ShapeDtypeStruct((K, N_MAIN), BF16),
        compiler_params=_cparams(("parallel",)),
        name="w_in_prep",
    )(w_t)
    b_main = jnp.concatenate([b_in[SRC_MERGE:], b_in[SRC_MLSTM:SRC_GATES], b_in[SRC_DIFF:SRC_MERGE]])[None, :]
    w_gate = pl.pallas_call(
        _wgate_kernel,
        grid=(1,),
        in_specs=[pl.BlockSpec((pl.Element(N_GATE_COLS), pl.Element(K)), lambda i: (SRC_GATES, 0))],
        out_specs=pl.BlockSpec((K, LANES), lambda i: (0, 0)),
        out_shape=jax.ShapeDtypeStruct((K, LANES), F32),
        name="w_gate_prep",
    )(w_t)
    b_gate = jnp.pad(b_in[SRC_GATES:SRC_DIFF], (0, LANES - N_GATE_COLS))[None, :]
    return w_main, b_main, _hi_lo_columns(w_gate), b_gate


def _layer(x, w_in, b_in, conv_w, conv_b, norm_a_g, lq1, lk1, lq2, lk2, norm_b_g, w_a, w_b, w_out,
           ln1_g, ln1_b, w_grp, b_grp, w_exp, b_exp, w_gate, w_up, w_down, ln2_g, ln2_b, lambda_init):
    B, S, D = x.shape
    T = B * S
    x2d = x.reshape(T, D)

    w_main, b_main, w_g, b_g = _rearrange_in_proj(w_in, b_in)
    z_main, z_gate = _in_projection(x2d, w_main, b_main, w_g, b_g)

    L = min(MLSTM_CHUNK, S)
    gp = _gate_prep(z_gate[:, :2 * A_HEADS].T, L)
    h_a = _mlstm(z_main, gp.T, gp, conv_w, conv_b[None, :], norm_a_g[None, :], B, S)
    o_b = _diff_attention(z_main, lq1[None, :], lk1[None, :], lq2[None, :], lk2[None, :],
                          norm_b_g[None, :], B, S, lambda_init)

    w_r = jnp.zeros((D, LANES), F32).at[:, :N_GROUPS].set(w_grp).at[:, SUBLANES:SUBLANES + N_EXPERTS].set(w_exp)
    b_r = jnp.zeros((LANES,), F32).at[:N_GROUPS].set(b_grp).at[SUBLANES:SUBLANES + N_EXPERTS].set(b_exp)[None, :]
    x1, x1_packed, route_i, route_w, counts = _mix(h_a, o_b, z_main, x2d, w_a.astype(BF16), w_b.astype(BF16),
                                                   w_out.astype(BF16), ln1_g[None, :], ln1_b[None, :],
                                                   _hi_lo_columns(w_r), b_r)

    tb = MOE_TB
    n_blocks = (T * TOP_K) // tb + N_EXPERTS
    x_pad, pstart, bexp, nused = _dispatch(counts[:, 0], route_i, x1_packed, tb, n_blocks)
    y_pad = _experts(bexp, nused, x_pad, w_gate, w_up, w_down, tb)
    out = _combine(pstart, route_i, x1, route_w.T, ln2_g[None, :], ln2_b[None, :], y_pad)
    return out.reshape(B, S, D)


def kernel(x, w_in, b_in, conv_w, conv_b, mlstm_norm_g, lambda_q1, lambda_k1, lambda_q2, lambda_k2,
           diff_norm_g, w_a, w_b, w_out, ln1_g, ln1_b, w_grp, b_grp, w_exp, b_exp,
           w_gate, w_up, w_down, ln2_g, ln2_b):
    for l in range(DEPTH):
        lambda_init = 0.8 - 0.6 * math.exp(-0.3 * l)
        x = _layer(x, w_in[l], b_in[l], conv_w[l], conv_b[l], mlstm_norm_g[l], lambda_q1[l], lambda_k1[l],
                   lambda_q2[l], lambda_k2[l], diff_norm_g[l], w_a[l], w_b[l], w_out[l], ln1_g[l], ln1_b[l],
                   w_grp[l], b_grp[l], w_exp[l], b_exp[l], w_gate[l], w_up[l], w_down[l], ln2_g[l], ln2_b[l],
                   lambda_init)
    return x
```

```python
import functools
import math

import jax
import jax.numpy as jnp
from jax import lax
from jax.experimental import pallas as pl
from jax.experimental.pallas import tpu as pltpu

F32 = jnp.float32
BF16 = jnp.bfloat16
I32 = jnp.int32
U32 = jnp.uint32

D_MODEL = 2048
A_HEADS = 4
A_HEAD_DIM = 256
A_WIDTH = A_HEADS * A_HEAD_DIM
CONV_WIDTH = 4
B_HEADS = 8
B_HEAD_DIM = 64
B_V_DIM = 2 * B_HEAD_DIM
B_WIDTH = B_HEADS * B_V_DIM
ATTN_CHUNK = 64
N_GROUPS = 4
EXPERTS_PER_GROUP = 8
N_EXPERTS = N_GROUPS * EXPERTS_PER_GROUP
TOP_K = 2
D_EXPERT = 512
DEPTH = 1
DEEPNORM_ALPHA = (2 * DEPTH) ** 0.25
LN_EPS = 1e-5
NEG_INF = float("-inf")
LOG2_E = 1.4426950408889634

LANES = 128
SUBLANES = 8
VMEM_LIMIT_BYTES = 56 * 1024 * 1024

COL_GA = 0
COL_GB = COL_GA + D_MODEL
COL_QA = COL_GB + D_MODEL
COL_KA = COL_QA + A_WIDTH
COL_VA = COL_KA + A_WIDTH
COL_OA = COL_VA + A_WIDTH
COL_QB = COL_OA + A_WIDTH
COL_KB = COL_QB + B_WIDTH
COL_VB = COL_KB + B_WIDTH
N_MAIN = COL_VB + B_WIDTH

PROJ_TM, PROJ_TN = 512, 2816
MLSTM_CHUNK = 512
ATTN_TQ = 1024
ATTN_DIAG_GROUPS = 4
MIX_TM = 512
MOE_TB = 256
DISPATCH_TM = 256
COMBINE_TM = 256


def _cparams(sem, vmem=VMEM_LIMIT_BYTES):
    return pltpu.CompilerParams(dimension_semantics=sem, vmem_limit_bytes=vmem)


def _sigmoid(x):
    return 0.5 * jnp.tanh(0.5 * x) + 0.5


def _pack_halves(x):
    n = x.shape[1] // 2
    lo = lax.bitcast_convert_type(x[:, :n].astype(BF16).astype(F32), U32)
    hi = lax.bitcast_convert_type(x[:, n:].astype(BF16).astype(F32), U32)
    return (lo >> 16) | hi


def _unpack_halves(u):
    lo = lax.bitcast_convert_type(u << 16, F32)
    hi = lax.bitcast_convert_type(u & jnp.uint32(0xFFFF0000), F32)
    return lo, hi


TILE_WORDS = SUBLANES * LANES


def _rows_to_tiles(ref, v):
    rows = v.shape[0]
    for j in range(SUBLANES):
        ref[pl.ds(j, rows, stride=SUBLANES), :] = v[:, j * LANES:(j + 1) * LANES]


def _tiles_to_rows(ref):
    rows = ref.shape[0] // SUBLANES
    return jnp.concatenate([ref[pl.ds(j, rows, stride=SUBLANES), :] for j in range(SUBLANES)], axis=1)


def _split_bf16(a):
    hi = a.astype(BF16)
    return hi, (a - hi.astype(F32)).astype(BF16)


def _hi_lo_columns(w):
    return jnp.concatenate(_split_bf16(w), axis=1)


def _dot_3pass(a, w_hl):
    a_hi, a_lo = _split_bf16(a)
    n = w_hl.shape[1] // 2
    dot = functools.partial(jnp.dot, preferred_element_type=F32)
    r = dot(a_hi, w_hl)
    return r[:, :n] + (r[:, n:] + dot(a_lo, w_hl[:, :n]))


def _proj_kernel(x_ref, w_ref, b_ref, wg_ref, bg_ref, z_ref, zg_ref, xb_ref):
    @pl.when(pl.program_id(1) == 0)
    def _():
        x = x_ref[...]
        xb_ref[...] = x.astype(BF16)
        zg_ref[...] = _dot_3pass(x, wg_ref[...]) + bg_ref[...]

    acc = jnp.dot(xb_ref[...], w_ref[...], preferred_element_type=F32)
    z_ref[...] = (acc + b_ref[...]).astype(z_ref.dtype)


def _in_projection(x2d, w_main, b_main, w_gate, b_gate):
    T, K = x2d.shape
    N = w_main.shape[1]
    tm, tn = min(PROJ_TM, T), PROJ_TN
    return pl.pallas_call(
        _proj_kernel,
        grid=(T // tm, N // tn),
        in_specs=[
            pl.BlockSpec((tm, K), lambda i, j: (i, 0)),
            pl.BlockSpec((K, tn), lambda i, j: (0, j)),
            pl.BlockSpec((1, tn), lambda i, j: (0, j)),
            pl.BlockSpec((K, 2 * LANES), lambda i, j: (0, 0)),
            pl.BlockSpec((1, LANES), lambda i, j: (0, 0)),
        ],
        out_specs=[
            pl.BlockSpec((tm, tn), lambda i, j: (i, j)),
            pl.BlockSpec((tm, LANES), lambda i, j: (i, 0)),
        ],
        out_shape=[
            jax.ShapeDtypeStruct((T, N), BF16),
            jax.ShapeDtypeStruct((T, LANES), F32),
        ],
        scratch_shapes=[pltpu.VMEM((tm, K), BF16)],
        compiler_params=_cparams(("parallel", "arbitrary")),
        name="in_projection",
    )(x2d, w_main, b_main, w_gate, b_gate)


def _gate_prep_kernel(g_ref, o_ref, *, L):
    r = lax.broadcasted_iota(I32, (L, L), 0)
    c = lax.broadcasted_iota(I32, (L, L), 1)
    tri = (r <= c).astype(F32)
    row = lax.broadcasted_iota(I32, (g_ref.shape[0], L), 0)
    for ch in range(g_ref.shape[1] // L):
        g = g_ref[:, ch * L:(ch + 1) * L]
        lf = jnp.minimum(g, 0.0) - jnp.log(1.0 + jnp.exp(-jnp.abs(g)))
        b = jnp.dot(lf, tri, preferred_element_type=F32, precision=lax.Precision.HIGHEST)
        o_ref[:, ch * L:(ch + 1) * L] = jnp.where(row < A_HEADS, g, b)


def _gate_prep(g_rows, L):
    R, T = g_rows.shape
    per_step = math.gcd(T // L, 8)
    W = L * per_step
    return pl.pallas_call(
        functools.partial(_gate_prep_kernel, L=L),
        grid=(T // W,),
        in_specs=[pl.BlockSpec((R, W), lambda i: (0, i))],
        out_specs=pl.BlockSpec((R, W), lambda i: (0, i)),
        out_shape=jax.ShapeDtypeStruct((R, T), F32),
        compiler_params=_cparams(("parallel",)),
        name="gate_prep",
    )(g_rows)


def _mlstm_kernel(q_ref, k_ref, v_ref, o_ref, gcol_ref, grow_ref, cw_ref, cb_ref, ng_ref, out_ref,
                  c_ref, n_ref, m_ref, qcar_ref, kcar_ref):
    ci = pl.program_id(1)
    L = q_ref.shape[0]
    dh = A_HEAD_DIM

    @pl.when(ci == 0)
    def _():
        c_ref[...] = jnp.zeros_like(c_ref)
        n_ref[...] = jnp.zeros_like(n_ref)
        m_ref[...] = jnp.zeros_like(m_ref)
        qcar_ref[...] = jnp.zeros_like(qcar_ref)
        kcar_ref[...] = jnp.zeros_like(kcar_ref)

    def conv_silu(u_ref, car_ref, lo, wcol):
        u = u_ref[:, lo:lo + dh].astype(F32)
        ext = jnp.concatenate([car_ref[:, lo:lo + dh], u], axis=0)
        w = 0.5 * cw_ref[:, wcol:wcol + dh]
        h = 0.5 * cb_ref[:, wcol:wcol + dh] + w[3:4, :] * u
        for j in range(CONV_WIDTH - 1):
            off = SUBLANES - (CONV_WIDTH - 1) + j
            h = h + w[j:j + 1, :] * ext[off:off + L, :]
        car_ref[:, lo:lo + dh] = u[L - SUBLANES:, :]
        return h * jnp.tanh(h) + h

    r = lax.broadcasted_iota(I32, (L, L), 0)
    c = lax.broadcasted_iota(I32, (L, L), 1)
    causal = r >= c
    gcol = gcol_ref[...]
    grow = grow_ref[...]

    for h in range(A_HEADS):
        lo = h * dh
        q = conv_silu(q_ref, qcar_ref, lo, lo)
        k = conv_silu(k_ref, kcar_ref, lo, A_WIDTH + lo) * (dh ** -0.5)
        v_b = v_ref[:, lo:lo + dh]
        q_b = q.astype(BF16)
        k_b = k.astype(BF16)

        i_col, b_col = gcol[:, h:h + 1], gcol[:, A_HEADS + h:A_HEADS + h + 1]
        i_row, b_row = grow[h:h + 1, :], grow[A_HEADS + h:A_HEADS + h + 1, :]
        b_last = b_row[:, L - 1:L]
        m_prev = m_ref[h]

        dmat = jnp.where(causal, b_col - b_row + i_row, NEG_INF)
        inter = b_col + m_prev
        m_t = jnp.maximum(inter, jnp.max(dmat, axis=-1, keepdims=True))
        s = lax.dot_general(q_b, k_b, (((1,), (1,)), ((), ())), preferred_element_type=F32)
        w_intra = jnp.exp(dmat - m_t) * s
        w_inter = jnp.exp(inter - m_t)
        qc = jnp.dot(q_b, c_ref[h].astype(BF16), preferred_element_type=F32)
        num = w_inter * qc + jnp.dot(w_intra.astype(BF16), v_b, preferred_element_type=F32)
        qn = jnp.sum(q * n_ref[h], axis=-1, keepdims=True)
        den = w_inter * qn + jnp.sum(w_intra, axis=-1, keepdims=True)
        hh = num / jnp.maximum(jnp.abs(den), jnp.exp(-m_t))

        g_col = b_last - b_col + i_col
        m_new = jnp.maximum(b_last + m_prev, jnp.max(g_col, axis=0, keepdims=True))
        decay = jnp.exp(b_last + m_prev - m_new)
        kw = jnp.exp(g_col - m_new) * k
        c_ref[h] = decay * c_ref[h] + jnp.dot(kw.T.astype(BF16), v_b, preferred_element_type=F32)
        n_ref[h] = decay * n_ref[h] + jnp.sum(kw, axis=0, keepdims=True)
        m_ref[h] = m_new

        y = _sigmoid(o_ref[:, lo:lo + dh].astype(F32)) * hh
        mu = jnp.mean(y, axis=-1, keepdims=True)
        yc = y - mu
        var = jnp.mean(yc * yc, axis=-1, keepdims=True)
        out_ref[:, lo:lo + dh] = (yc * lax.rsqrt(var + LN_EPS) * ng_ref[:, lo:lo + dh]).astype(out_ref.dtype)


def _mlstm(z_main, gcol, grow, conv_w, conv_b, norm_g, B, S):
    L = min(MLSTM_CHUNK, S)
    nc = S // L
    dh = A_HEAD_DIM
    H = A_HEADS
    W = A_WIDTH
    row = lambda b, c: b * nc + c
    full = lambda shape: pl.BlockSpec(shape, lambda b, c: (0, 0))
    return pl.pallas_call(
        _mlstm_kernel,
        grid=(B, nc),
        in_specs=[
            pl.BlockSpec((L, W), lambda b, c: (row(b, c), COL_QA // W)),
            pl.BlockSpec((L, W), lambda b, c: (row(b, c), COL_KA // W)),
            pl.BlockSpec((L, W), lambda b, c: (row(b, c), COL_VA // W)),
            pl.BlockSpec((L, W), lambda b, c: (row(b, c), COL_OA // W)),
            pl.BlockSpec((L, SUBLANES), lambda b, c: (row(b, c), 0)),
            pl.BlockSpec((SUBLANES, L), lambda b, c: (0, row(b, c))),
            full((CONV_WIDTH, 2 * W)), full((1, 2 * W)), full((1, W)),
        ],
        out_specs=pl.BlockSpec((L, W), lambda b, c: (row(b, c), 0)),
        out_shape=jax.ShapeDtypeStruct((B * S, W), BF16),
        scratch_shapes=[
            pltpu.VMEM((H, dh, dh), F32),
            pltpu.VMEM((H, 1, dh), F32),
            pltpu.VMEM((H, 1, 1), F32),
            pltpu.VMEM((SUBLANES, W), F32),
            pltpu.VMEM((SUBLANES, W), F32),
        ],
        compiler_params=_cparams(("parallel", "arbitrary")),
        name="mlstm",
    )(z_main, z_main, z_main, z_main, gcol, grow, conv_w, conv_b, norm_g)


def _lane_tiles(t):
    return [t[:, i * LANES:(i + 1) * LANES] for i in range(t.shape[1] // LANES)]


def _attn_kernel(slope_ref, q_ref, k_ref, kb_ref, v_ref, lq1_ref, lk1_ref, lq2_ref, lk2_ref, g_ref, o_ref,
                 s_ref, mx_ref, acc_ref, corr_ref, *, lambda_init):
    h = pl.program_id(1)
    qi = pl.program_id(2)
    tq = q_ref.shape[0]
    tk = tq
    slope = slope_ref[h]
    nt = (((1,), (1,)), ((), ()))

    lane = lax.broadcasted_iota(I32, (tq, B_V_DIM), 1)
    q = q_ref[...] * (B_HEAD_DIM ** -0.5)
    zero = jnp.zeros_like(q)
    bias_cols = jnp.where(lane < 2, 1.0, 0.0).astype(BF16)
    qa = (jnp.concatenate([jnp.where(lane < B_HEAD_DIM, q, zero), bias_cols], axis=1),
          jnp.concatenate([jnp.where(lane >= B_HEAD_DIM, q, zero), bias_cols], axis=1))

    def scores(j):
        start = pl.multiple_of(j * tk, tk)
        ka = jnp.concatenate([k_ref[pl.ds(start, tk), :], kb_ref[pl.ds(start, tk), :]], axis=1)
        return [lax.dot_general(qa[n], ka, nt, preferred_element_type=F32) for n in range(2)]

    def keep(j, n, t, rows=slice(None)):
        t = t * LOG2_E
        s_ref[n, j, rows, :t.shape[1]] = t
        mx_ref[n, rows, :] = functools.reduce(jnp.maximum, _lane_tiles(t), mx_ref[n, rows, :])

    mx_ref[...] = jnp.full(mx_ref.shape, NEG_INF, F32)

    def pass_a(j):
        t = scores(j)
        for n in range(2):
            keep(j, n, t[n])

    def pass_a_pair(jj, carry):
        pass_a(2 * jj)
        pass_a(2 * jj + 1)
        return carry
    lax.fori_loop(0, qi // 2, pass_a_pair, 0)

    @pl.when(qi % 2 == 1)
    def _():
        pass_a(qi - 1)

    rg = tq // ATTN_DIAG_GROUPS

    @pl.when(qi == 0)
    def _():
        r = lax.broadcasted_iota(I32, (rg, rg), 0)
        c = lax.broadcasted_iota(I32, (rg, rg), 1)
        ahead = jnp.where(c <= r, 0.0, (r - c).astype(F32) * (2.0 * slope))
        corr_ref[...] = jnp.where((c // ATTN_CHUNK) <= (r // ATTN_CHUNK), ahead, NEG_INF)

    dstart = pl.multiple_of(qi * tk, tk)
    ka = jnp.concatenate([k_ref[pl.ds(dstart, tk), :], kb_ref[pl.ds(dstart, tk), :]], axis=1)
    for n in range(2):
        for g in range(ATTN_DIAG_GROUPS):
            kbeg, kend = g * rg, (g + 1) * rg
            band = lax.dot_general(qa[n][kbeg:kend], ka[:kend], nt, preferred_element_type=F32)
            square = band[:, kbeg:] + corr_ref[...]
            band = square if g == 0 else jnp.concatenate([band[:, :kbeg], square], axis=1)
            keep(qi, n, band, rows=slice(kbeg, kend))

    for n in range(2):
        m = jnp.max(mx_ref[n], axis=-1, keepdims=True)
        mx_ref[n] = jnp.broadcast_to(m, (tq, LANES))
    acc_ref[...] = jnp.zeros_like(acc_ref)

    def weigh(n, rows, s_tile, v_tile):
        mb = mx_ref[n, rows, :]
        p = jnp.concatenate([jnp.exp2(t - mb) for t in _lane_tiles(s_tile)], axis=1).astype(BF16)
        v_ones = jnp.concatenate([v_tile, jnp.ones((v_tile.shape[0], LANES), BF16)], axis=1)
        acc_ref[n, rows, :] += jnp.dot(p, v_ones, preferred_element_type=F32)

    def pass_b(j, carry):
        start = pl.multiple_of(j * tk, tk)
        vj = v_ref[pl.ds(start, tk), :]
        for n in range(2):
            weigh(n, slice(None), s_ref[n, j], vj)
        return carry
    lax.fori_loop(0, qi, pass_b, 0)

    for n in range(2):
        for g in range(ATTN_DIAG_GROUPS):
            kend = (g + 1) * rg
            weigh(n, slice(g * rg, kend), s_ref[n, qi, g * rg:kend, :kend], v_ref[pl.ds(dstart, kend), :])

    lam = (jnp.exp(jnp.sum(lq1_ref[...] * lk1_ref[...], axis=-1, keepdims=True))
           - jnp.exp(jnp.sum(lq2_ref[...] * lk2_ref[...], axis=-1, keepdims=True)) + lambda_init)
    dv = B_V_DIM
    a0, a1 = acc_ref[0], acc_ref[1]
    o = a0[:, :dv] / a0[:, dv:] - lam * (a1[:, :dv] / a1[:, dv:])
    ms = jnp.mean(o * o, axis=-1, keepdims=True)
    o_ref[...] = (o * lax.rsqrt(ms + LN_EPS) * g_ref[...] * (1.0 - lambda_init)).astype(o_ref.dtype)


def _diff_attention(z_main, lq1, lk1, lq2, lk2, norm_g, B, S, lambda_init):
    tq = min(ATTN_TQ, S)
    nq = S // tq
    H = B_HEADS
    dv = B_V_DIM
    slopes = 2.0 ** (-8.0 * jnp.arange(1, H + 1, dtype=F32) / H)
    assert S <= 256 * 256
    pos = lax.broadcasted_iota(I32, (H, S, dv), 1)
    col = lax.broadcasted_iota(I32, (H, S, dv), 2)
    within = pos % 256
    kbias = jnp.where(col == 0, within, jnp.where(col == 1, pos - within, 0)).astype(F32)
    kbias = (kbias * slopes[:, None, None]).astype(BF16)
    small = pl.BlockSpec((1, B_HEAD_DIM), lambda b, h, i: (0, 0))
    return pl.pallas_call(
        functools.partial(_attn_kernel, lambda_init=lambda_init),
        grid=(B, H, nq),
        in_specs=[
            pl.BlockSpec(memory_space=pltpu.SMEM),
            pl.BlockSpec((tq, dv), lambda b, h, i: (b * nq + i, COL_QB // dv + h)),
            pl.BlockSpec((S, dv), lambda b, h, i: (b, COL_KB // dv + h)),
            pl.BlockSpec((None, S, dv), lambda b, h, i: (h, 0, 0)),
            pl.BlockSpec((S, dv), lambda b, h, i: (b, COL_VB // dv + h)),
            small, small, small, small,
            pl.BlockSpec((1, dv), lambda b, h, i: (0, 0)),
        ],
        out_specs=pl.BlockSpec((tq, dv), lambda b, h, i: (b * nq + i, h)),
        out_shape=jax.ShapeDtypeStruct((B * S, B_WIDTH), BF16),
        scratch_shapes=[
            pltpu.VMEM((2, nq, tq, tq), F32),
            pltpu.VMEM((2, tq, LANES), F32),
            pltpu.VMEM((2, tq, dv + LANES), F32),
            pltpu.VMEM((tq // ATTN_DIAG_GROUPS, tq // ATTN_DIAG_GROUPS), F32),
        ],
        compiler_params=_cparams(("parallel", "parallel", "arbitrary")),
        name="diff_attention",
    )(slopes, z_main, z_main, kbias, z_main, lq1, lk1, lq2, lk2, norm_g)


def _mix_kernel(ha_ref, ob_ref, ga_ref, gb_ref, x_ref, wa_ref, wb_ref, wo_ref, g1_ref, b1_ref, wr_ref, br_ref,
                x1_ref, x1p_ref, ri_ref, rw_ref, cnt_ref, carry_ref):
    i = pl.program_id(0)
    tm = x_ref.shape[0]

    @pl.when(i == 0)
    def _():
        carry_ref[...] = jnp.zeros_like(carry_ref)

    ya = jnp.dot(ha_ref[...], wa_ref[...], preferred_element_type=F32)
    yb = jnp.dot(ob_ref[...], wb_ref[...], preferred_element_type=F32)
    merged = _sigmoid(ga_ref[...].astype(F32)) * ya + _sigmoid(gb_ref[...].astype(F32)) * yb
    mix = jnp.dot(merged.astype(BF16), wo_ref[...], preferred_element_type=F32)
    pre = DEEPNORM_ALPHA * x_ref[...] + mix
    mu = jnp.mean(pre, axis=-1, keepdims=True)
    pc = pre - mu
    var = jnp.mean(pc * pc, axis=-1, keepdims=True)
    x1 = pc * lax.rsqrt(var + LN_EPS) * g1_ref[...] + b1_ref[...]
    x1_ref[...] = x1
    _rows_to_tiles(x1p_ref, _pack_halves(x1))

    logits = _dot_3pass(x1, wr_ref[...]) + br_ref[...]
    lt = logits.T
    row8 = lax.broadcasted_iota(I32, (SUBLANES, tm), 0).astype(F32)
    gl = jnp.where(row8 < N_GROUPS, lt[0:SUBLANES, :], NEG_INF)
    gmax = jnp.max(gl, axis=0, keepdims=True)
    gsel = jnp.min(jnp.where(gl == gmax, row8, float(SUBLANES)), axis=0, keepdims=True)
    gprob = 1.0 / jnp.sum(jnp.exp(gl - gmax), axis=0, keepdims=True)
    ing = lt[SUBLANES:2 * SUBLANES, :]
    for g in range(1, N_GROUPS):
        ing = jnp.where(gsel == g, lt[(g + 1) * SUBLANES:(g + 2) * SUBLANES, :], ing)
    v0 = jnp.max(ing, axis=0, keepdims=True)
    i0 = jnp.min(jnp.where(ing == v0, row8, float(SUBLANES)), axis=0, keepdims=True)
    ing2 = jnp.where(row8 == i0, -jnp.inf, ing)
    v1 = jnp.max(ing2, axis=0, keepdims=True)
    i1 = jnp.min(jnp.where(ing2 == v1, row8, float(SUBLANES)), axis=0, keepdims=True)
    ex = jnp.exp(v1 - v0)
    inv = 1.0 / (1.0 + ex)
    w0 = gprob * inv
    w1 = gprob * (ex * inv)
    e0 = (gsel * EXPERTS_PER_GROUP + i0).astype(I32)
    e1 = (gsel * EXPERTS_PER_GROUP + i1).astype(I32)

    rowe = lax.broadcasted_iota(I32, (N_EXPERTS, tm), 0)
    is0 = rowe == e0
    is1 = rowe == e1
    oh = jnp.where(is0, 1.0, 0.0) + jnp.where(is1, 1.0, 0.0)
    tr = lax.broadcasted_iota(I32, (tm, tm), 0)
    tc = lax.broadcasted_iota(I32, (tm, tm), 1)
    su = jnp.where(tr < tc, 1.0, 0.0).astype(BF16)
    tot = jnp.dot(oh.astype(BF16), su, preferred_element_type=F32) + carry_ref[...]
    rank0 = jnp.sum(jnp.where(is0, tot, 0.0), axis=0, keepdims=True)
    rank1 = jnp.sum(jnp.where(is1, tot, 0.0), axis=0, keepdims=True)
    carry = carry_ref[...] + jnp.sum(oh, axis=1, keepdims=True)
    carry_ref[...] = carry
    cnt_ref[...] = jnp.broadcast_to(carry, cnt_ref.shape).astype(I32)

    zi = jnp.zeros((1, tm), I32)
    ri_ref[...] = jnp.concatenate([e0, e1, rank0.astype(I32), rank1.astype(I32), zi, zi, zi, zi], axis=0)
    zf = jnp.zeros((1, tm), F32)
    rw_ref[...] = jnp.concatenate([w0, w1, zf, zf, zf, zf, zf, zf], axis=0)


def _mix(h_a, o_b, z_main, x2d, w_a, w_b, w_out, ln_g, ln_b, w_r, b_r):
    T, D = x2d.shape
    tm = min(MIX_TM, T)
    const = lambda shape: pl.BlockSpec(shape, lambda i: (0, 0), pipeline_mode=pl.Buffered(1))
    return pl.pallas_call(
        _mix_kernel,
        grid=(T // tm,),
        in_specs=[
            pl.BlockSpec((tm, A_WIDTH), lambda i: (i, 0)),
            pl.BlockSpec((tm, B_WIDTH), lambda i: (i, 0)),
            pl.BlockSpec((tm, D), lambda i: (i, COL_GA // D)),
            pl.BlockSpec((tm, D), lambda i: (i, COL_GB // D)),
            pl.BlockSpec((tm, D), lambda i: (i, 0)),
            const((A_WIDTH, D)), const((B_WIDTH, D)), const((D, D)),
            const((1, D)), const((1, D)), const((D, 2 * LANES)), const((1, LANES)),
        ],
        out_specs=[
            pl.BlockSpec((tm, D), lambda i: (i, 0)),
            pl.BlockSpec((tm * SUBLANES, LANES), lambda i: (i, 0)),
            pl.BlockSpec((SUBLANES, tm), lambda i: (0, i)),
            pl.BlockSpec((SUBLANES, tm), lambda i: (0, i)),
            pl.BlockSpec((N_EXPERTS, LANES), lambda i: (0, 0)),
        ],
        out_shape=[
            jax.ShapeDtypeStruct((T, D), F32),
            jax.ShapeDtypeStruct((T * SUBLANES, LANES), U32),
            jax.ShapeDtypeStruct((SUBLANES, T), I32),
            jax.ShapeDtypeStruct((SUBLANES, T), F32),
            jax.ShapeDtypeStruct((N_EXPERTS, LANES), I32),
        ],
        scratch_shapes=[pltpu.VMEM((N_EXPERTS, 1), F32)],
        compiler_params=_cparams(("arbitrary",)),
        name="mix_ln1_router",
    )(h_a, o_b, z_main, z_main, x2d, w_a, w_b, w_out, ln_g, ln_b, w_r, b_r)


def _row_copy(src_ref, src_row, dst_ref, dst_row, sem):
    return pltpu.make_async_copy(src_ref.at[pl.ds(src_row * SUBLANES, SUBLANES)],
                                 dst_ref.at[pl.ds(dst_row * SUBLANES, SUBLANES)], sem)


def _rows_wait(src_ref, dst_ref, rows, sem):
    n = rows * SUBLANES
    pltpu.make_async_copy(src_ref.at[pl.ds(0, n)], dst_ref.at[pl.ds(0, n)], sem).wait()


DISPATCH_SLOTS = 3


def _dispatch_kernel(cnt_ref, ri_ref, x1_ref, xpad_ref, pstart_ref, bexp_ref, nused_ref,
                     pst_ref, zero_ref, xbuf_ref, sem, lsem, zsem, *, tb, n_blocks, tm):
    i = pl.program_id(0)
    last = pl.num_programs(0) - 1

    @pl.when(i == 0)
    def _():
        def fill(b, carry):
            bexp_ref[b] = 0
            return carry
        lax.fori_loop(0, n_blocks, fill, 0)

        def per_expert(e, blk):
            pst_ref[e] = blk * tb
            pstart_ref[e] = blk * tb
            nb = (cnt_ref[e] + (tb - 1)) // tb

            def mark(b, carry):
                bexp_ref[blk + b] = e
                return carry
            lax.fori_loop(0, nb, mark, 0)
            return blk + nb
        nused = lax.fori_loop(0, N_EXPERTS, per_expert, 0)
        nused_ref[0] = nused
        pst_ref[N_EXPERTS] = nused

    rows = tm * SUBLANES
    slot = i % DISPATCH_SLOTS

    def tile_load(tile, s):
        src = x1_ref.at[pl.ds(pl.multiple_of(tile * rows, rows), rows)]
        return pltpu.make_async_copy(src, xbuf_ref.at[s], lsem.at[s])

    def wait_rows(s):
        for k in range(TOP_K):
            _rows_wait(xbuf_ref.at[s], xpad_ref, tm, sem.at[s])

    @pl.when(i == 0)
    def _():
        tile_load(0, 0).start()

        @pl.when(last >= 1)
        def _():
            tile_load(1, 1).start()

    tile_load(i, slot).wait()
    for t in range(tm):
        for k in range(TOP_K):
            pos = pst_ref[ri_ref[k, t]] + ri_ref[TOP_K + k, t]
            _row_copy(xbuf_ref.at[slot], t, xpad_ref, pos, sem.at[slot]).start(priority=k)

    prev = (i + DISPATCH_SLOTS - 1) % DISPATCH_SLOTS

    @pl.when(i >= 1)
    def _():
        wait_rows(prev)

    @pl.when(i + 2 <= last)
    def _():
        tile_load(i + 2, prev).start()

    @pl.when(i == last)
    def _():
        wait_rows(slot)
        zero_ref[...] = jnp.zeros_like(zero_ref)

        def pad_copies(e, wait):
            cnt = cnt_ref[e]
            npad = ((cnt + (tb - 1)) // tb) * tb - cnt
            off = pst_ref[e] + cnt
            size = tb // 2
            while size >= 1:
                has = (npad & size) != 0
                n = size * SUBLANES
                dst = 0 if wait else pl.multiple_of(off * SUBLANES, SUBLANES)
                copy = pltpu.make_async_copy(zero_ref.at[pl.ds(0, n)], xpad_ref.at[pl.ds(dst, n)], zsem)

                @pl.when(has)
                def _():
                    copy.wait() if wait else copy.start()
                off = off + jnp.where(has, size, 0)
                size //= 2

        def start_pads(e, carry):
            pad_copies(e, False)
            return carry
        lax.fori_loop(0, N_EXPERTS, start_pads, 0)

        def wait_pads(e, carry):
            pad_copies(e, True)
            return carry
        lax.fori_loop(0, N_EXPERTS, wait_pads, 0)

        def block_copy(b):
            n = tb * SUBLANES
            return pltpu.make_async_copy(zero_ref, xpad_ref.at[pl.ds(pl.multiple_of(b * n, n), n)], zsem)

        def start_block(b, carry):
            block_copy(b).start()
            return carry
        lax.fori_loop(pst_ref[N_EXPERTS], n_blocks, start_block, 0)

        def wait_block(b, carry):
            block_copy(b).wait()
            return carry
        lax.fori_loop(pst_ref[N_EXPERTS], n_blocks, wait_block, 0)


def _dispatch(counts, route_i, x1, tb, n_blocks):
    T = x1.shape[0] // SUBLANES
    tm = min(DISPATCH_TM, T)
    smem_full = pl.BlockSpec(memory_space=pltpu.SMEM)
    return pl.pallas_call(
        functools.partial(_dispatch_kernel, tb=tb, n_blocks=n_blocks, tm=tm),
        grid=(T // tm,),
        in_specs=[
            smem_full,
            pl.BlockSpec((SUBLANES, tm), lambda i: (0, i), memory_space=pltpu.SMEM),
            pl.BlockSpec(memory_space=pl.ANY),
        ],
        out_specs=[
            pl.BlockSpec(memory_space=pl.ANY),
            smem_full, smem_full, smem_full,
        ],
        out_shape=[
            jax.ShapeDtypeStruct((n_blocks * tb * SUBLANES, LANES), x1.dtype),
            jax.ShapeDtypeStruct((N_EXPERTS,), I32),
            jax.ShapeDtypeStruct((n_blocks,), I32),
            jax.ShapeDtypeStruct((1,), I32),
        ],
        scratch_shapes=[
            pltpu.SMEM((N_EXPERTS + 1,), I32),
            pltpu.VMEM((tb * SUBLANES, LANES), x1.dtype),
            pltpu.VMEM((DISPATCH_SLOTS, tm * SUBLANES, LANES), x1.dtype),
            pltpu.SemaphoreType.DMA((DISPATCH_SLOTS,)),
            pltpu.SemaphoreType.DMA((DISPATCH_SLOTS,)),
            pltpu.SemaphoreType.DMA(()),
        ],
        compiler_params=_cparams(("arbitrary",)),
        name="moe_dispatch",
    )(counts, route_i, x1)


def _expert_kernel(bexp_ref, nused_ref, x_ref, wg_hbm, wu_hbm, wd_hbm, y_ref,
                   wgf_ref, wuf_ref, wdf_ref, wgb_ref, wub_ref, wdb_ref, sem, run_ref):
    b = pl.program_id(0)
    nused = nused_ref[0]
    used = b < nused
    e = bexp_ref[b]
    new_expert = jnp.logical_or(b == 0, e != bexp_ref[jnp.maximum(b - 1, 0)])
    streams = ((wg_hbm, wgf_ref), (wu_hbm, wuf_ref), (wd_hbm, wdf_ref))

    def weight_copies(expert, slot):
        return [pltpu.make_async_copy(w_hbm.at[expert], wf_ref.at[slot], sem.at[slot, j])
                for j, (w_hbm, wf_ref) in enumerate(streams)]

    @pl.when(jnp.logical_and(used, b == 0))
    def _():
        run_ref[0] = 0
        for cp in weight_copies(e, 0):
            cp.start()

    @pl.when(jnp.logical_and(used, new_expert))
    def _():
        slot = run_ref[0] % 2
        nxt = lax.while_loop(lambda j: jnp.logical_and(j < nused, bexp_ref[jnp.minimum(j, nused - 1)] == e),
                             lambda j: j + 1, b + 1)

        @pl.when(nxt < nused)
        def _():
            for cp in weight_copies(bexp_ref[jnp.minimum(nxt, nused - 1)], 1 - slot):
                cp.start()

        for cp in weight_copies(e, slot):
            cp.wait()
        wgb_ref[...] = wgf_ref[slot].astype(BF16)
        wub_ref[...] = wuf_ref[slot].astype(BF16)
        wdb_ref[...] = wdf_ref[slot].astype(BF16)
        run_ref[0] = run_ref[0] + 1

    @pl.when(used)
    def _():
        half = SUBLANES * LANES
        x_lo, x_hi = (t.astype(BF16) for t in _unpack_halves(_tiles_to_rows(x_ref)))
        dot = functools.partial(jnp.dot, preferred_element_type=F32)
        gate = dot(x_lo, wgb_ref[:half, :]) + dot(x_hi, wgb_ref[half:, :])
        up = dot(x_lo, wub_ref[:half, :]) + dot(x_hi, wub_ref[half:, :])
        hid = (gate * _sigmoid(gate) * up).astype(BF16)
        _rows_to_tiles(y_ref, _pack_halves(dot(hid, wdb_ref[...])))

    @pl.when(jnp.logical_not(used))
    def _():
        y_ref[...] = jnp.zeros_like(y_ref)


def _experts(bexp, nused, x_pad, w_gate, w_up, w_down, tb):
    P = x_pad.shape[0] // SUBLANES
    D = 2 * TILE_WORDS
    n_blocks = P // tb
    rows = (tb * SUBLANES, LANES)
    F = w_gate.shape[-1]
    blk = lambda b, be, nu: jnp.maximum(jnp.minimum(b, nu[0] - 1), 0)
    hbm = pl.BlockSpec(memory_space=pl.ANY)
    return pl.pallas_call(
        _expert_kernel,
        grid_spec=pltpu.PrefetchScalarGridSpec(
            num_scalar_prefetch=2,
            grid=(n_blocks,),
            in_specs=[pl.BlockSpec(rows, lambda b, be, nu: (blk(b, be, nu), 0)), hbm, hbm, hbm],
            out_specs=pl.BlockSpec(rows, lambda b, be, nu: (b, 0)),
            scratch_shapes=[
                pltpu.VMEM((2, D, F), F32), pltpu.VMEM((2, D, F), F32), pltpu.VMEM((2, F, D), F32),
                pltpu.VMEM((D, F), BF16), pltpu.VMEM((D, F), BF16), pltpu.VMEM((F, D), BF16),
                pltpu.SemaphoreType.DMA((2, 3)),
                pltpu.SMEM((1,), I32),
            ],
        ),
        out_shape=jax.ShapeDtypeStruct(x_pad.shape, U32),
        compiler_params=_cparams(("arbitrary",)),
        name="moe_experts",
    )(bexp, nused, x_pad, w_gate, w_up, w_down)


COMBINE_SLOTS = 3


def _combine_kernel(pstart_ref, ri_ref, r1_ref, r2_ref, x1_ref, rw_ref, g_ref, b_ref, ypad_ref, o_ref, ybuf_ref, sem):
    i = pl.program_id(0)
    n = pl.num_programs(0)
    tm = x1_ref.shape[0]
    slot = i % COMBINE_SLOTS

    def gather_loop(route_ref, s):
        def issue(t, carry):
            for k in range(TOP_K):
                pos = pstart_ref[route_ref[k, t]] + route_ref[TOP_K + k, t]
                _row_copy(ypad_ref, pos, ybuf_ref.at[s, k], t, sem.at[s]).start(priority=k)
            return carry
        lax.fori_loop(0, tm, issue, 0, unroll=8)

    def wait_slot(s):
        for k in range(TOP_K):
            _rows_wait(ypad_ref, ybuf_ref.at[s, k], tm, sem.at[s])

    @pl.when(i == 0)
    def _():
        gather_loop(ri_ref, 0)
        gather_loop(r1_ref, 1)

    wait_slot(slot)

    rw = rw_ref[...]
    y0 = _unpack_halves(_tiles_to_rows(ybuf_ref.at[slot, 0]))
    y1 = _unpack_halves(_tiles_to_rows(ybuf_ref.at[slot, 1]))
    ffn = jnp.concatenate([y0[h] * rw[:, 0:1] + y1[h] * rw[:, 1:2] for h in range(2)], axis=1)
    pre = DEEPNORM_ALPHA * x1_ref[...] + ffn
    mu = jnp.mean(pre, axis=-1, keepdims=True)
    pc = pre - mu
    var = jnp.mean(pc * pc, axis=-1, keepdims=True)
    o_ref[...] = pc * lax.rsqrt(var + LN_EPS) * g_ref[...] + b_ref[...]

    nxt = (i + 2) % COMBINE_SLOTS
    for t in range(tm):
        for k in range(TOP_K):
            pos = pstart_ref[r2_ref[k, t]] + r2_ref[TOP_K + k, t]
            _row_copy(ypad_ref, pos, ybuf_ref.at[nxt, k], t, sem.at[nxt]).start(priority=k)

    @pl.when(i == n - 1)
    def _():
        wait_slot((i + 1) % COMBINE_SLOTS)
        wait_slot(nxt)


def _combine(pstart, route_i, x1, rw_col, ln_g, ln_b, y_pad):
    T, D = x1.shape
    tm = min(COMBINE_TM, T)
    n = T // tm
    return pl.pallas_call(
        _combine_kernel,
        grid=(n,),
        in_specs=[
            pl.BlockSpec(memory_space=pltpu.SMEM),
            pl.BlockSpec((SUBLANES, tm), lambda i: (0, i), memory_space=pltpu.SMEM),
            pl.BlockSpec((SUBLANES, tm), lambda i: (0, jnp.minimum(i + 1, n - 1)), memory_space=pltpu.SMEM),
            pl.BlockSpec((SUBLANES, tm), lambda i: (0, jnp.minimum(i + 2, n - 1)), memory_space=pltpu.SMEM),
            pl.BlockSpec((tm, D), lambda i: (i, 0)),
            pl.BlockSpec((tm, SUBLANES), lambda i: (i, 0)),
            pl.BlockSpec((1, D), lambda i: (0, 0)),
            pl.BlockSpec((1, D), lambda i: (0, 0)),
            pl.BlockSpec(memory_space=pl.ANY),
        ],
        out_specs=pl.BlockSpec((tm, D), lambda i: (i, 0)),
        out_shape=jax.ShapeDtypeStruct((T, D), F32),
        scratch_shapes=[
            pltpu.VMEM((COMBINE_SLOTS, TOP_K, tm * SUBLANES, LANES), U32),
            pltpu.SemaphoreType.DMA((COMBINE_SLOTS,)),
        ],
        compiler_params=_cparams(("arbitrary",)),
        name="moe_combine_ln2",
    )(pstart, route_i, route_i, route_i, x1, rw_col, ln_g, ln_b, y_pad)


N_GATE_COLS = 2 * A_HEADS
SRC_MLSTM = 0
SRC_GATES = 4 * A_WIDTH
SRC_DIFF = SRC_GATES + N_GATE_COLS
SRC_MERGE = SRC_DIFF + 3 * B_WIDTH
PREP_TN = 512


def _wprep_kernel(a_ref, o_ref):
    o_ref[...] = a_ref[...].T.astype(BF16)


def _wgate_kernel(a_ref, o_ref):
    a = a_ref[...]
    rows = jnp.concatenate([a, jnp.zeros((LANES - a.shape[0], a.shape[1]), F32)], axis=0)
    o_ref[...] = rows.T


def _rearrange_in_proj(w_in, b_in):
    K, n_in = w_in.shape
    tn = PREP_TN
    merge_blocks = (2 * D_MODEL) // tn
    mlstm_blocks = (4 * A_WIDTH) // tn
    assert SRC_MERGE % SUBLANES == 0 and SRC_DIFF % SUBLANES == 0

    def src_row(jb):
        s = SUBLANES
        merge = SRC_MERGE // s + (tn // s) * jb
        mlstm = SRC_MLSTM // s + (tn // s) * (jb - merge_blocks)
        diff = SRC_DIFF // s + (tn // s) * (jb - merge_blocks - mlstm_blocks)
        return s * jnp.where(jb < merge_blocks, merge, jnp.where(jb < merge_blocks + mlstm_blocks, mlstm, diff))

    w_t = jnp.swapaxes(w_in, 0, 1)
    w_main = pl.pallas_call(
        _wprep_kernel,
        grid=(N_MAIN // tn,),
        in_specs=[pl.BlockSpec((pl.Element(tn), pl.Element(K)), lambda jb: (src_row(jb), 0))],
        out_specs=pl.BlockSpec((K, tn), lambda jb: (0, jb)),
        out_shape=jax.ShapeDtypeStruct((K, N_MAIN), BF16),
        compiler_params=_cparams(("parallel",)),
        name="w_in_prep",
    )(w_t)
    b_main = jnp.concatenate([b_in[SRC_MERGE:], b_in[SRC_MLSTM:SRC_GATES], b_in[SRC_DIFF:SRC_MERGE]])[None, :]
    w_gate = pl.pallas_call(
        _wgate_kernel,
        grid=(1,),
        in_specs=[pl.BlockSpec((pl.Element(N_GATE_COLS), pl.Element(K)), lambda i: (SRC_GATES, 0))],
        out_specs=pl.BlockSpec((K, LANES), lambda i: (0, 0)),
        out_shape=jax.ShapeDtypeStruct((K, LANES), F32),
        name="w_gate_prep",
    )(w_t)
    b_gate = jnp.pad(b_in[SRC_GATES:SRC_DIFF], (0, LANES - N_GATE_COLS))[None, :]
    return w_main, b_main, _hi_lo_columns(w_gate), b_gate


def _layer(x, w_in, b_in, conv_w, conv_b, norm_a_g, lq1, lk1, lq2, lk2, norm_b_g, w_a, w_b, w_out,
           ln1_g, ln1_b, w_grp, b_grp, w_exp, b_exp, w_gate, w_up, w_down, ln2_g, ln2_b, lambda_init):
    B, S, D = x.shape
    T = B * S
    x2d = x.reshape(T, D)

    w_main, b_main, w_g, b_g = _rearrange_in_proj(w_in, b_in)
    z_main, z_gate = _in_projection(x2d, w_main, b_main, w_g, b_g)

    L = min(MLSTM_CHUNK, S)
    gp = _gate_prep(z_gate[:, :2 * A_HEADS].T, L)
    h_a = _mlstm(z_main, gp.T, gp, conv_w, conv_b[None, :], norm_a_g[None, :], B, S)
    o_b = _diff_attention(z_main, lq1[None, :], lk1[None, :], lq2[None, :], lk2[None, :],
                          norm_b_g[None, :], B, S, lambda_init)

    w_r = jnp.zeros((D, LANES), F32).at[:, :N_GROUPS].set(w_grp).at[:, SUBLANES:SUBLANES + N_EXPERTS].set(w_exp)
    b_r = jnp.zeros((LANES,), F32).at[:N_GROUPS].set(b_grp).at[SUBLANES:SUBLANES + N_EXPERTS].set(b_exp)[None, :]
    x1, x1_packed, route_i, route_w, counts = _mix(h_a, o_b, z_main, x2d, w_a.astype(BF16), w_b.astype(BF16),
                                                   w_out.astype(BF16), ln1_g[None, :], ln1_b[None, :],
                                                   _hi_lo_columns(w_r), b_r)

    tb = MOE_TB
    n_blocks = (T * TOP_K) // tb + N_EXPERTS
    x_pad, pstart, bexp, nused = _dispatch(counts[:, 0], route_i, x1_packed, tb, n_blocks)
    y_pad = _experts(bexp, nused, x_pad, w_gate, w_up, w_down, tb)
    out = _combine(pstart, route_i, x1, route_w.T, ln2_g[None, :], ln2_b[None, :], y_pad)
    return out.reshape(B, S, D)


def kernel(x, w_in, b_in, conv_w, conv_b, mlstm_norm_g, lambda_q1, lambda_k1, lambda_q2, lambda_k2,
           diff_norm_g, w_a, w_b, w_out, ln1_g, ln1_b, w_grp, b_grp, w_exp, b_exp,
           w_gate, w_up, w_down, ln2_g, ln2_b):
    for l in range(DEPTH):
        lambda_init = 0.8 - 0.6 * math.exp(-0.3 * l)
        x = _layer(x, w_in[l], b_in[l], conv_w[l], conv_b[l], mlstm_norm_g[l], lambda_q1[l], lambda_k1[l],
                   lambda_q2[l], lambda_k2[l], diff_norm_g[l], w_a[l], w_b[l], w_out[l], ln1_g[l], ln1_b[l],
                   w_grp[l], b_grp[l], w_exp[l], b_exp[l], w_gate[l], w_up[l], w_down[l], ln2_g[l], ln2_b[l],
                   lambda_init)
    return x
```
